```python
import jax, jax.numpy as jnp
from jax import lax
import numpy as np

D_MODEL = 1024
BATCH = 4
SEQ = 8192
DEPTH = 1

N_META = 16
BLOCK = 128
WINDOW = 128
N_PAD = BLOCK - N_META
ROPE_THETA = 10000.0
SWA_HEADS = 8
SWA_KV_HEADS = 2
SWA_HEAD_DIM = 64
MLA_HEADS = 4
MLA_Q_RANK = 256
MLA_KV_RANK = 256
MLA_NOPE_DIM = 128
MLA_ROPE_DIM = 64
MLA_V_DIM = 128
SWA_Q_COLS = SWA_HEADS * SWA_HEAD_DIM
SWA_KV_COLS = SWA_KV_HEADS * SWA_HEAD_DIM
MLA_OUT_COLS = MLA_HEADS * MLA_V_DIM
D_MIX = SWA_Q_COLS + MLA_OUT_COLS
IN_SPLITS = (SWA_Q_COLS, SWA_Q_COLS + SWA_KV_COLS, SWA_Q_COLS + 2 * SWA_KV_COLS,
             SWA_Q_COLS + 2 * SWA_KV_COLS + MLA_Q_RANK,
             SWA_Q_COLS + 2 * SWA_KV_COLS + MLA_Q_RANK + MLA_KV_RANK)
IN_COLS = SWA_Q_COLS + 2 * SWA_KV_COLS + MLA_Q_RANK + MLA_KV_RANK + MLA_ROPE_DIM
N_GROUPS = 4
EXPERTS_PER_GROUP = 8
N_EXPERTS = N_GROUPS * EXPERTS_PER_GROUP
TOP_K = 2
D_EXPERT = 256
LN_EPS = 1e-5
RMS_EPS = 1e-6
ALPHA = (2.0 * DEPTH) ** 0.25
BETA = (8.0 * DEPTH) ** -0.25
NEG = -1e30

kernel_name = "hymba_swa_sink_mla_hier_moe_deepnorm"


def layer_norm(x, g, b):
    xf = x.astype(jnp.float32)
    mu = jnp.mean(xf, axis=-1, keepdims=True)
    var = jnp.mean(jnp.square(xf - mu), axis=-1, keepdims=True)
    return ((xf - mu) * lax.rsqrt(var + LN_EPS) * g + b).astype(x.dtype)


def rms_norm(x, g):
    xf = x.astype(jnp.float32)
    ms = jnp.mean(jnp.square(xf), axis=-1, keepdims=True)
    return (xf * lax.rsqrt(ms + RMS_EPS) * g).astype(x.dtype)


def rope(x, pos):
    d = x.shape[-1]
    inv_freq = ROPE_THETA ** (-jnp.arange(0, d, 2, dtype=jnp.float32) / d)
    ang = pos[:, None] * inv_freq[None, :]
    cos = jnp.concatenate([jnp.cos(ang), jnp.cos(ang)], axis=-1)[:, None, :]
    sin = jnp.concatenate([jnp.sin(ang), jnp.sin(ang)], axis=-1)[:, None, :]
    xf = x.astype(jnp.float32)
    rot = jnp.concatenate([-xf[..., d // 2:], xf[..., :d // 2]], axis=-1)
    return (xf * cos + rot * sin).astype(x.dtype)


def swa_sink_attention(q, k, v, sinks, pos):
    B, Lp = q.shape[0], q.shape[1]
    nb = Lp // BLOCK
    G = SWA_HEADS // SWA_KV_HEADS
    q = rope(q, pos)
    k = rope(k, pos)
    qb = q.reshape(B, nb, BLOCK, SWA_KV_HEADS, G, SWA_HEAD_DIM)
    kb = k.reshape(B, nb, BLOCK, SWA_KV_HEADS, SWA_HEAD_DIM)
    vb = v.reshape(B, nb, BLOCK, SWA_KV_HEADS, SWA_HEAD_DIM)

    def with_prev(t):
        prev = jnp.pad(t[:, :-1], ((0, 0), (1, 0), (0, 0), (0, 0), (0, 0)))
        return jnp.concatenate([prev, t], axis=2)

    kw, vw = with_prev(kb), with_prev(vb)
    q_idx = jnp.arange(Lp).reshape(nb, BLOCK)
    k_idx = q_idx[:, :1] - BLOCK + jnp.arange(2 * BLOCK)[None, :]
    diff = q_idx[:, :, None] - k_idx[:, None, :]
    mask = (diff >= 0) & (diff < WINDOW) & (k_idx[:, None, :] >= N_PAD)
    s = jnp.einsum('bnqhgd,bnkhd->bnhgqk', qb, kw).astype(jnp.float32) * (SWA_HEAD_DIM ** -0.5)
    s = jnp.where(mask[None, :, None, None], s, NEG)
    sink = sinks.astype(jnp.float32).reshape(SWA_KV_HEADS, G)[None, None, :, :, None, None]
    m = jnp.maximum(jnp.max(s, axis=-1, keepdims=True), sink)
    e = jnp.exp(s - m)
    p = e / (jnp.sum(e, axis=-1, keepdims=True) + jnp.exp(sink - m))
    o = jnp.einsum('bnhgqk,bnkhd->bnqhgd', p.astype(v.dtype), vw)
    return o.reshape(B, Lp, SWA_Q_COLS)


def mla_attention(c_q, c_kv, k_rope, q_norm_g, w_uq, kv_norm_g, w_ukv, pos):
    B, Lp = c_q.shape[0], c_q.shape[1]
    H = MLA_HEADS
    q = (rms_norm(c_q, q_norm_g) @ w_uq).reshape(B, Lp, H, MLA_NOPE_DIM + MLA_ROPE_DIM)
    q = jnp.concatenate([q[..., :MLA_NOPE_DIM], rope(q[..., MLA_NOPE_DIM:], pos)], axis=-1)
    kv = (rms_norm(c_kv, kv_norm_g) @ w_ukv).reshape(B, Lp, H, MLA_NOPE_DIM + MLA_V_DIM)
    k_nope, v = kv[..., :MLA_NOPE_DIM], kv[..., MLA_NOPE_DIM:]
    k_r = jnp.broadcast_to(rope(k_rope[:, :, None, :], pos), (B, Lp, H, MLA_ROPE_DIM))
    k = jnp.concatenate([k_nope, k_r], axis=-1)
    scale = (MLA_NOPE_DIM + MLA_ROPE_DIM) ** -0.5
    nb = Lp // BLOCK
    qb = jnp.moveaxis(q.reshape(B, nb, BLOCK, H, MLA_NOPE_DIM + MLA_ROPE_DIM), 1, 0)
    k_idx = jnp.arange(Lp)

    def block_attn(args):
        q_blk, n = args
        q_idx = n * BLOCK + jnp.arange(BLOCK)
        mask = (k_idx[None, :] <= q_idx[:, None]) & (k_idx[None, :] >= N_PAD)
        s = jnp.einsum('bqhd,bkhd->bhqk', q_blk, k).astype(jnp.float32) * scale
        p = jax.nn.softmax(jnp.where(mask[None, None], s, NEG), axis=-1)
        return jnp.einsum('bhqk,bkhd->bqhd', p.astype(v.dtype), v)

    o = lax.map(block_attn, (qb, jnp.arange(nb)))
    return jnp.moveaxis(o, 0, 1).reshape(B, Lp, MLA_OUT_COLS)


def hierarchical_moe(h, w_group, b_group, w_router, b_router, w_gate, w_up, w_down):
    B, Lp, D = h.shape
    t = h.reshape(B * Lp, D)
    T = t.shape[0]
    group_probs = jax.nn.softmax((t @ w_group).astype(jnp.float32) + b_group, axis=-1)
    g_top, g_idx = lax.top_k(group_probs, 1)
    e_logits = ((t @ w_router).astype(jnp.float32) + b_router).reshape(T, N_GROUPS, EXPERTS_PER_GROUP)
    sel = jnp.broadcast_to(g_idx[:, :, None], (T, 1, EXPERTS_PER_GROUP))
    in_group = jnp.take_along_axis(e_logits, sel, axis=1)[:, 0]
    e_top, e_idx = lax.top_k(in_group, TOP_K)
    weights = g_top * jax.nn.softmax(e_top, axis=-1)
    expert_ids = g_idx * EXPERTS_PER_GROUP + e_idx
    combine = jnp.einsum('tk,tke->te', weights, jax.nn.one_hot(expert_ids, N_EXPERTS, dtype=jnp.float32))
    nblk = T // BLOCK

    def expert_block(args):
        xb, cb = args
        hid = jax.nn.silu(jnp.einsum('td,edf->tef', xb, w_gate)) * jnp.einsum('td,edf->tef', xb, w_up)
        hid = hid * cb.astype(hid.dtype)[:, :, None]
        return jnp.einsum('tef,efd->td', hid, w_down)

    out = lax.map(expert_block, (t.reshape(nblk, BLOCK, D), combine.reshape(nblk, BLOCK, N_EXPERTS)))
    return out.reshape(B, Lp, D)


def setup_inputs(seed: int = 0) -> dict:
    key = jax.random.key(seed)
    ks = jax.random.split(key, 24)
    f32 = jnp.float32

    def nrm(k, shape, scale):
        return jax.random.normal(k, shape, f32) * scale

    L = DEPTH
    return {
        "x": nrm(ks[0], (BATCH, SEQ, D_MODEL), 1.0),
        "meta_tokens": nrm(ks[1], (N_META, D_MODEL), 1.0),
        "ln_in_g": 1.0 + nrm(ks[2], (D_MODEL,), 0.02),
        "ln_in_b": nrm(ks[3], (D_MODEL,), 0.02),
        "w_in": nrm(ks[4], (L, D_MODEL, IN_COLS), D_MODEL ** -0.5),
        "swa_sinks": nrm(ks[5], (L, SWA_HEADS), 0.5),
        "mla_q_norm_g": 1.0 + nrm(ks[6], (L, MLA_Q_RANK), 0.02),
        "mla_w_uq": nrm(ks[7], (L, MLA_Q_RANK, MLA_HEADS * (MLA_NOPE_DIM + MLA_ROPE_DIM)), MLA_Q_RANK ** -0.5),
        "mla_kv_norm_g": 1.0 + nrm(ks[8], (L, MLA_KV_RANK), 0.02),
        "mla_w_ukv": nrm(ks[9], (L, MLA_KV_RANK, MLA_HEADS * (MLA_NOPE_DIM + MLA_V_DIM)), MLA_KV_RANK ** -0.5),
        "swa_out_norm_g": 1.0 + nrm(ks[10], (L, SWA_Q_COLS), 0.02),
        "mla_out_norm_g": 1.0 + nrm(ks[11], (L, MLA_OUT_COLS), 0.02),
        "w_o": nrm(ks[12], (L, D_MIX, D_MODEL), (D_MIX ** -0.5) * BETA),
        "ln1_g": 1.0 + nrm(ks[13], (L, D_MODEL), 0.02),
        "ln1_b": nrm(ks[14], (L, D_MODEL), 0.02),
        "moe_w_group": nrm(ks[15], (L, D_MODEL, N_GROUPS), D_MODEL ** -0.5),
        "moe_b_group": nrm(ks[16], (L, N_GROUPS), 0.01),
        "moe_w_router": nrm(ks[17], (L, D_MODEL, N_EXPERTS), D_MODEL ** -0.5),
        "moe_b_router": nrm(ks[18], (L, N_EXPERTS), 0.01),
        "moe_w_gate": nrm(ks[19], (L, N_EXPERTS, D_MODEL, D_EXPERT), D_MODEL ** -0.5),
        "moe_w_up": nrm(ks[20], (L, N_EXPERTS, D_MODEL, D_EXPERT), D_MODEL ** -0.5),
        "moe_w_down": nrm(ks[21], (L, N_EXPERTS, D_EXPERT, D_MODEL), (D_EXPERT ** -0.5) * BETA),
        "ln2_g": 1.0 + nrm(ks[22], (L, D_MODEL), 0.02),
        "ln2_b": nrm(ks[23], (L, D_MODEL), 0.02),
    }


def reference(x, meta_tokens, ln_in_g, ln_in_b, w_in, swa_sinks, mla_q_norm_g, mla_w_uq,
              mla_kv_norm_g, mla_w_ukv, swa_out_norm_g, mla_out_norm_g, w_o, ln1_g, ln1_b,
              moe_w_group, moe_b_group, moe_w_router, moe_b_router, moe_w_gate, moe_w_up,
              moe_w_down, ln2_g, ln2_b):
    B = x.shape[0]
    pads = jnp.zeros((B, N_PAD, D_MODEL), x.dtype)
    meta = jnp.broadcast_to(meta_tokens.astype(x.dtype)[None], (B, N_META, D_MODEL))
    h = jnp.concatenate([pads, meta, x], axis=1)
    Lp = h.shape[1]
    pos = jnp.maximum(jnp.arange(Lp) - N_PAD, 0).astype(jnp.float32)
    h = layer_norm(h, ln_in_g, ln_in_b)
    for l in range(DEPTH):
        u = h @ w_in[l]
        q_a, k_a, v_a, c_q, c_kv, k_rope = jnp.split(u, IN_SPLITS, axis=-1)
        a_out = swa_sink_attention(
            q_a.reshape(B, Lp, SWA_HEADS, SWA_HEAD_DIM),
            k_a.reshape(B, Lp, SWA_KV_HEADS, SWA_HEAD_DIM),
            v_a.reshape(B, Lp, SWA_KV_HEADS, SWA_HEAD_DIM),
            swa_sinks[l], pos)
        b_out = mla_attention(c_q, c_kv, k_rope, mla_q_norm_g[l], mla_w_uq[l],
                              mla_kv_norm_g[l], mla_w_ukv[l], pos)
        mix = jnp.concatenate([rms_norm(a_out, swa_out_norm_g[l]),
                               rms_norm(b_out, mla_out_norm_g[l])], axis=-1) @ w_o[l]
        h = layer_norm(ALPHA * h + mix, ln1_g[l], ln1_b[l])
        ffn = hierarchical_moe(h, moe_w_group[l], moe_b_group[l], moe_w_router[l], moe_b_router[l],
                               moe_w_gate[l], moe_w_up[l], moe_w_down[l])
        h = layer_norm(ALPHA * h + ffn, ln2_g[l], ln2_b[l])
    return h[:, BLOCK:]
```

```python
import functools

import jax
import jax.numpy as jnp
from jax import lax
from jax.experimental import pallas as pl
from jax.experimental.pallas import tpu as pltpu

D_MODEL = 1024
N_META = 16
BLOCK = 128
N_PAD = BLOCK - N_META
ROPE_THETA = 10000.0
SWA_HEADS = 8
SWA_KV_HEADS = 2
SWA_HEAD_DIM = 64
MLA_HEADS = 4
MLA_Q_RANK = 256
MLA_KV_RANK = 256
MLA_NOPE_DIM = 128
MLA_ROPE_DIM = 64
MLA_V_DIM = 128
SWA_Q_COLS = SWA_HEADS * SWA_HEAD_DIM
SWA_KV_COLS = SWA_KV_HEADS * SWA_HEAD_DIM
MLA_OUT_COLS = MLA_HEADS * MLA_V_DIM
N_GROUPS = 4
EXPERTS_PER_GROUP = 8
N_EXPERTS = N_GROUPS * EXPERTS_PER_GROUP
D_EXPERT = 256
LN_EPS = 1e-5
RMS_EPS = 1e-6
DEPTH = 1
ALPHA = (2.0 * DEPTH) ** 0.25
NEG = -1e30

LANES = 128
IN_COLS_PADDED = 1408
MLA_QK_PAD = 256
ROUTE_LANES = 128
EXPERT_LANE0 = 32
VMEM_LIMIT = 56 * 1024 * 1024

SWA_HEAD_ORDER = (0, 4, 1, 5, 2, 6, 3, 7)

BF16 = jnp.bfloat16
F32 = jnp.float32


def _cparams(sem):
    return pltpu.CompilerParams(dimension_semantics=sem, vmem_limit_bytes=VMEM_LIMIT)


def _layer_norm(x, g, b):
    mu = jnp.mean(x, axis=-1, keepdims=True)
    xc = x - mu
    var = jnp.mean(xc * xc, axis=-1, keepdims=True)
    return xc * lax.rsqrt(var + LN_EPS) * g + b


def _rms_scale(x):
    return lax.rsqrt(jnp.mean(x * x, axis=-1, keepdims=True) + RMS_EPS)


def _rope128(x, cos, sin_signed, lane_lo):
    up = pltpu.roll(x, LANES - 32, 1)
    dn = pltpu.roll(x, 32, 1)
    return x * cos + jnp.where(lane_lo, up, dn) * sin_signed


def _dot(a, b):
    return jnp.dot(a, b, preferred_element_type=F32)


def _dot_nt(a, b):
    return lax.dot_general(a, b, (((1,), (1,)), ((), ())), preferred_element_type=F32)


def _in_proj_kernel(x_ref, g_ref, b_ref, w_ref, cos_ref, sin_ref, gq_ref, wuq_ref, gkv_ref,
                    wukv_ref, qa_ref, ka_ref, va_ref, qm_ref, km_ref, vm_ref):
    h = _layer_norm(x_ref[0], g_ref[...], b_ref[...])
    u = _dot(h.astype(BF16), w_ref[...])
    cos = cos_ref[...]
    sin = sin_ref[...]
    lane = lax.broadcasted_iota(jnp.int32, (1, LANES), 1)
    lane_lo = (lane % 64) < 32
    rope = functools.partial(_rope128, cos=cos, sin_signed=sin, lane_lo=lane_lo)

    swa_scale = SWA_HEAD_DIM ** -0.5
    for c in range(SWA_Q_COLS // LANES):
        qc = rope(u[:, c * LANES:(c + 1) * LANES]) * swa_scale
        qa_ref[0, :, c * LANES:(c + 1) * LANES] = qc.astype(BF16)
    ka_ref[0] = rope(u[:, 512:640]).astype(BF16)
    va_ref[0] = u[:, 640:768].astype(BF16)

    cq = u[:, 768:1024]
    cqn = (cq * _rms_scale(cq) * gq_ref[...]).astype(BF16)
    qm = _dot(cqn, wuq_ref[...])
    mla_scale = (MLA_NOPE_DIM + MLA_ROPE_DIM) ** -0.5
    ckv = u[:, 1024:1280]
    ckvn = (ckv * _rms_scale(ckv) * gkv_ref[...]).astype(BF16)
    kv = _dot(ckvn, wukv_ref[...])
    kr = rope(u[:, 1280:1408]).astype(BF16)
    for hd in range(MLA_HEADS):
        base = hd * MLA_QK_PAD
        qm_ref[0, hd, :, 0:LANES] = (qm[:, base:base + LANES] * mla_scale).astype(BF16)
        qm_ref[0, hd, :, LANES:2 * LANES] = (
            rope(qm[:, base + LANES:base + 2 * LANES]) * mla_scale).astype(BF16)
        km_ref[0, hd, :, 0:LANES] = kv[:, base:base + LANES].astype(BF16)
        km_ref[0, hd, :, LANES:2 * LANES] = kr
        vm_ref[0, hd] = kv[:, base + LANES:base + 2 * LANES].astype(BF16)


def _in_proj(x, cos, sin, p, tm):
    B, S, D = x.shape
    grid = (B, S // tm)
    full = lambda shape: pl.BlockSpec(shape, lambda b, i: (0,) * len(shape))
    out_shape = (
        jax.ShapeDtypeStruct((B, S, SWA_Q_COLS), BF16),
        jax.ShapeDtypeStruct((B, S, SWA_KV_COLS), BF16),
        jax.ShapeDtypeStruct((B, S, SWA_KV_COLS), BF16),
        jax.ShapeDtypeStruct((B, MLA_HEADS, S, MLA_QK_PAD), BF16),
        jax.ShapeDtypeStruct((B, MLA_HEADS, S, MLA_QK_PAD), BF16),
        jax.ShapeDtypeStruct((B, MLA_HEADS, S, MLA_V_DIM), BF16),
    )
    row = lambda w: pl.BlockSpec((1, tm, w), lambda b, i: (b, i, 0))
    head = lambda w: pl.BlockSpec((1, MLA_HEADS, tm, w), lambda b, i: (b, 0, i, 0))
    return pl.pallas_call(
        _in_proj_kernel,
        grid=grid,
        in_specs=[
            row(D),
            full((1, D)), full((1, D)),
            full((D, IN_COLS_PADDED)),
            pl.BlockSpec((tm, LANES), lambda b, i: (i, 0)),
            pl.BlockSpec((tm, LANES), lambda b, i: (i, 0)),
            full((1, MLA_Q_RANK)), full((MLA_Q_RANK, MLA_HEADS * MLA_QK_PAD)),
            full((1, MLA_KV_RANK)), full((MLA_KV_RANK, MLA_HEADS * MLA_QK_PAD)),
        ],
        out_specs=(row(SWA_Q_COLS), row(SWA_KV_COLS), row(SWA_KV_COLS),
                   head(MLA_QK_PAD), head(MLA_QK_PAD), head(MLA_V_DIM)),
        out_shape=out_shape,
        compiler_params=_cparams(("parallel", "parallel")),
        name="in_proj",
    )(x, p["ln_in_g"], p["ln_in_b"], p["w_in"], cos, sin, p["gq"], p["w_uq"], p["gkv"], p["w_ukv"])


def _swa_kernel(sink_ref, qa_ref, kc_ref, vc_ref, kp_ref, vp_ref, kmeta_ref, vmeta_ref, o_ref,
                *, blocks_per_step):
    i = pl.program_id(1)
    first = i == 0
    kprev = jnp.where(first, kmeta_ref[0], kp_ref[0])
    vprev = jnp.where(first, vmeta_ref[0], vp_ref[0])

    row = lax.broadcasted_iota(jnp.int32, (2 * BLOCK, 2 * BLOCK), 0) % BLOCK
    col = lax.broadcasted_iota(jnp.int32, (2 * BLOCK, 2 * BLOCK), 1)
    base_mask = ((col < BLOCK) & (col > row)) | ((col >= BLOCK) & ((col - BLOCK) <= row))
    first_mask = base_mask & (col >= jnp.where(first, N_PAD, 0))
    lane = lax.broadcasted_iota(jnp.int32, (1, LANES), 1)
    lane_kv0 = lane < SWA_HEAD_DIM
    top_rows = lax.broadcasted_iota(jnp.int32, (2 * BLOCK, 1), 0) < BLOCK

    for r in range(blocks_per_step):
        rows = slice(r * BLOCK, (r + 1) * BLOCK)
        if r == 0:
            kp, vp = kprev, vprev
            mask = first_mask
        else:
            prev_rows = slice((r - 1) * BLOCK, r * BLOCK)
            kp, vp = kc_ref[0, prev_rows, :], vc_ref[0, prev_rows, :]
            mask = base_mask
        keys = jnp.concatenate([kp, kc_ref[0, rows, :]], axis=0)
        vals = jnp.concatenate([vp, vc_ref[0, rows, :]], axis=0)
        for c in range(SWA_Q_COLS // LANES):
            qc = qa_ref[0, rows, c * LANES:(c + 1) * LANES]
            zero = jnp.zeros_like(qc)
            q2 = jnp.concatenate([jnp.where(lane_kv0, qc, zero),
                                  jnp.where(lane_kv0, zero, qc)], axis=0)
            s = jnp.where(mask, _dot_nt(q2, keys), NEG)
            sink = jnp.where(top_rows, sink_ref[c], sink_ref[c + 4])
            m = jnp.maximum(jnp.max(s, axis=-1, keepdims=True), sink)
            e = jnp.exp(s - m)
            denom = jnp.sum(e, axis=-1, keepdims=True) + jnp.exp(sink - m)
            pr = (e * (1.0 / denom)).astype(BF16)
            o2 = _dot(pr, vals)
            o = jnp.where(lane_kv0, o2[:BLOCK], o2[BLOCK:])
            o_ref[0, rows, c * LANES:(c + 1) * LANES] = o.astype(BF16)


def _swa(sinks, qa, ka, va, ka_meta, va_meta, tb):
    B, S, _ = qa.shape
    R = tb // BLOCK
    kvw = SWA_KV_COLS
    cur = lambda w: pl.BlockSpec((1, tb, w), lambda b, i: (b, i, 0))
    prev = pl.BlockSpec((1, BLOCK, kvw), lambda b, i: (b, jnp.maximum(i * R - 1, 0), 0))
    meta = pl.BlockSpec((1, BLOCK, kvw), lambda b, i: (0, 0, 0))
    return pl.pallas_call(
        functools.partial(_swa_kernel, blocks_per_step=R),
        grid=(B, S // tb),
        in_specs=[pl.BlockSpec(memory_space=pltpu.SMEM),
                  cur(SWA_Q_COLS), cur(kvw), cur(kvw), prev, prev, meta, meta],
        out_specs=cur(SWA_Q_COLS),
        out_shape=jax.ShapeDtypeStruct((B, S, SWA_Q_COLS), BF16),
        compiler_params=_cparams(("parallel", "parallel")),
        name="swa_attn",
    )(sinks, qa, ka, va, ka, va, ka_meta, va_meta)


def _mla_kernel(q_ref, k_ref, v_ref, kmeta_ref, vmeta_ref, o_ref, *, tq):
    i = pl.program_id(2)
    q = q_ref[0, 0]

    s = _dot_nt(q, kmeta_ref[0, 0])
    col = lax.broadcasted_iota(jnp.int32, (tq, BLOCK), 1)
    s = jnp.where(col >= N_PAD, s, NEG)
    m = jnp.max(s, axis=-1, keepdims=True)
    p = jnp.exp(s - m)
    l = jnp.sum(p, axis=-1, keepdims=True)
    acc = _dot(p.astype(BF16), vmeta_ref[0, 0])

    def step(k, v, carry, mask=None):
        m, l, acc = carry
        s = _dot_nt(q, k)
        if mask is not None:
            s = jnp.where(mask, s, NEG)
        m_new = jnp.maximum(m, jnp.max(s, axis=-1, keepdims=True))
        alpha = jnp.exp(m - m_new)
        p = jnp.exp(s - m_new)
        l = alpha * l + jnp.sum(p, axis=-1, keepdims=True)
        acc = alpha * acc + _dot(p.astype(BF16), v)
        return m_new, l, acc

    def body(j, carry):
        start = pl.multiple_of(j * tq, tq)
        return step(k_ref[0, 0, pl.ds(start, tq), :], v_ref[0, 0, pl.ds(start, tq), :], carry)

    carry = lax.fori_loop(0, i, body, (m, l, acc))
    start = pl.multiple_of(i * tq, tq)
    row = lax.broadcasted_iota(jnp.int32, (tq, tq), 0)
    colq = lax.broadcasted_iota(jnp.int32, (tq, tq), 1)
    m, l, acc = step(k_ref[0, 0, pl.ds(start, tq), :], v_ref[0, 0, pl.ds(start, tq), :], carry,
                     mask=colq <= row)
    o_ref[0] = (acc * (1.0 / l)).astype(BF16)


def _mla(qm, km, vm, km_meta, vm_meta, tq):
    B, H, S, _ = qm.shape
    return pl.pallas_call(
        functools.partial(_mla_kernel, tq=tq),
        grid=(B, H, S // tq),
        in_specs=[
            pl.BlockSpec((1, 1, tq, MLA_QK_PAD), lambda b, h, i: (b, h, i, 0)),
            pl.BlockSpec((1, 1, S, MLA_QK_PAD), lambda b, h, i: (b, h, 0, 0)),
            pl.BlockSpec((1, 1, S, MLA_V_DIM), lambda b, h, i: (b, h, 0, 0)),
            pl.BlockSpec((1, 1, BLOCK, MLA_QK_PAD), lambda b, h, i: (0, h, 0, 0)),
            pl.BlockSpec((1, 1, BLOCK, MLA_V_DIM), lambda b, h, i: (0, h, 0, 0)),
        ],
        out_specs=pl.BlockSpec((1, tq, MLA_V_DIM), lambda b, h, i: (b, i, h)),
        out_shape=jax.ShapeDtypeStruct((B, S, MLA_OUT_COLS), BF16),
        compiler_params=_cparams(("parallel", "parallel", "arbitrary")),
        name="mla_attn",
    )(qm, km, vm, km_meta, vm_meta)


def _out_proj_kernel(x_ref, a_ref, b_ref, gin_ref, bin_ref, ga_ref, gb_ref, wo_ref, g1_ref, b1_ref,
                     wrt_ref, brt_ref, h1_ref, comb_ref):
    h = _layer_norm(x_ref[0], gin_ref[...], bin_ref[...])
    a = a_ref[0].astype(F32)
    b = b_ref[0].astype(F32)
    an = (a * _rms_scale(a) * ga_ref[...]).astype(BF16)
    bn = (b * _rms_scale(b) * gb_ref[...]).astype(BF16)
    mix = _dot(jnp.concatenate([an, bn], axis=-1), wo_ref[...])
    h1 = _layer_norm(ALPHA * h + mix, g1_ref[...], b1_ref[...])
    h1_ref[0] = h1

    logits = jnp.dot(h1, wrt_ref[...], precision=lax.Precision.HIGHEST,
                     preferred_element_type=F32) + brt_ref[...]
    lane = lax.broadcasted_iota(jnp.int32, logits.shape, 1).astype(F32)
    ninf = -jnp.inf
    no_lane = float(ROUTE_LANES)
    gl = jnp.where(lane < N_GROUPS, logits, ninf)
    gmax = jnp.max(gl, axis=-1, keepdims=True)
    g_top = 1.0 / jnp.sum(jnp.exp(gl - gmax), axis=-1, keepdims=True)
    g_idx = jnp.min(jnp.where(gl == gmax, lane, no_lane), axis=-1, keepdims=True)
    lo = EXPERT_LANE0 + g_idx * EXPERTS_PER_GROUP
    el = jnp.where((lane >= lo) & (lane < lo + EXPERTS_PER_GROUP), logits, ninf)
    m1 = jnp.max(el, axis=-1, keepdims=True)
    i1 = jnp.min(jnp.where(el == m1, lane, no_lane), axis=-1, keepdims=True)
    el2 = jnp.where(lane == i1, ninf, el)
    m2 = jnp.max(el2, axis=-1, keepdims=True)
    i2 = jnp.min(jnp.where(el2 == m2, lane, no_lane), axis=-1, keepdims=True)
    e2 = jnp.exp(m2 - m1)
    w1 = g_top / (1.0 + e2)
    w2 = g_top * e2 / (1.0 + e2)
    comb_ref[0] = jnp.where(lane == i1, w1, 0.0) + jnp.where(lane == i2, w2, 0.0)


def _out_proj(x, a_out, b_out, p, tm):
    B, S, D = x.shape
    full = lambda shape: pl.BlockSpec(shape, lambda b, i: (0,) * len(shape))
    row = lambda w: pl.BlockSpec((1, tm, w), lambda b, i: (b, i, 0))
    return pl.pallas_call(
        _out_proj_kernel,
        grid=(B, S // tm),
        in_specs=[row(D), row(SWA_Q_COLS), row(MLA_OUT_COLS),
                  full((1, D)), full((1, D)), full((1, SWA_Q_COLS)), full((1, MLA_OUT_COLS)),
                  full((D, D)), full((1, D)), full((1, D)),
                  full((D, ROUTE_LANES)), full((1, ROUTE_LANES))],
        out_specs=(row(D), row(ROUTE_LANES)),
        out_shape=(jax.ShapeDtypeStruct((B, S, D), F32),
                   jax.ShapeDtypeStruct((B, S, ROUTE_LANES), F32)),
        compiler_params=_cparams(("parallel", "parallel")),
        name="out_proj_route",
    )(x, a_out, b_out, p["ln_in_g"], p["ln_in_b"], p["ga"], p["gb"], p["w_o"], p["ln1_g"],
      p["ln1_b"], p["w_route"], p["b_route"])


def _moe_kernel(h1_ref, comb_ref, wgu_ref, wd_ref, g2_ref, b2_ref, o_ref, xb_ref, acc_ref):
    e = pl.program_id(1)

    @pl.when(e == 0)
    def _():
        xb_ref[...] = h1_ref[...].astype(BF16)
        acc_ref[...] = jnp.zeros_like(acc_ref)

    gu = _dot(xb_ref[...], wgu_ref[0])
    g = gu[:, :D_EXPERT]
    u = gu[:, D_EXPERT:]
    comb = comb_ref[...]
    lane = lax.broadcasted_iota(jnp.int32, comb.shape, 1)
    ce = jnp.sum(jnp.where(lane == EXPERT_LANE0 + e, comb, 0.0), axis=-1, keepdims=True)
    hid = g * (1.0 / (1.0 + jnp.exp(-g))) * u * ce
    acc_ref[...] += _dot(hid.astype(BF16), wd_ref[0])

    @pl.when(e == N_EXPERTS - 1)
    def _():
        o_ref[...] = _layer_norm(ALPHA * h1_ref[...] + acc_ref[...], g2_ref[...], b2_ref[...])


def _moe(h1, comb, p, tm):
    T, D = h1.shape
    full = lambda shape: pl.BlockSpec(shape, lambda t, e: (0,) * len(shape))
    return pl.pallas_call(
        _moe_kernel,
        grid=(T // tm, N_EXPERTS),
        in_specs=[pl.BlockSpec((tm, D), lambda t, e: (t, 0)),
                  pl.BlockSpec((tm, ROUTE_LANES), lambda t, e: (t, 0)),
                  pl.BlockSpec((1, D, 2 * D_EXPERT), lambda t, e: (e, 0, 0)),
                  pl.BlockSpec((1, D_EXPERT, D), lambda t, e: (e, 0, 0)),
                  full((1, D)), full((1, D))],
        out_specs=pl.BlockSpec((tm, D), lambda t, e: (t, 0)),
        out_shape=jax.ShapeDtypeStruct((T, D), F32),
        scratch_shapes=[pltpu.VMEM((tm, D), BF16), pltpu.VMEM((tm, D), F32)],
        compiler_params=_cparams(("parallel", "arbitrary")),
        name="moe_ln2",
    )(h1, comb, p["w_gu"], p["w_down"], p["ln2_g"], p["ln2_b"])


def _rope_tables(pos):
    d = SWA_HEAD_DIM
    inv_freq = ROPE_THETA ** (-jnp.arange(0, d, 2, dtype=F32) / d)
    ang = pos[:, None] * inv_freq[None, :]
    c, s = jnp.cos(ang), jnp.sin(ang)
    return jnp.concatenate([c, c, c, c], axis=-1), jnp.concatenate([-s, s, -s, s], axis=-1)


def _prep_params(ln_in_g, ln_in_b, w_in, swa_sinks, mla_q_norm_g, mla_w_uq, mla_kv_norm_g, mla_w_ukv,
                 swa_out_norm_g, mla_out_norm_g, w_o, ln1_g, ln1_b, moe_w_group, moe_b_group,
                 moe_w_router, moe_b_router, moe_w_gate, moe_w_up, moe_w_down, ln2_g, ln2_b):
    order = jnp.asarray(SWA_HEAD_ORDER)
    perm = (order[:, None] * SWA_HEAD_DIM + jnp.arange(SWA_HEAD_DIM)[None, :]).reshape(-1)
    w = w_in[0]
    w_q = w[:, :SWA_Q_COLS][:, perm]
    w_pad = jnp.zeros((D_MODEL, IN_COLS_PADDED - w.shape[1]), w.dtype)
    w_in_p = jnp.concatenate([w_q, w[:, SWA_Q_COLS:], w_pad], axis=1).astype(BF16)

    dqk = MLA_NOPE_DIM + MLA_ROPE_DIM
    wuq = mla_w_uq[0].reshape(MLA_Q_RANK, MLA_HEADS, dqk)
    wuq = jnp.pad(wuq, ((0, 0), (0, 0), (0, MLA_QK_PAD - dqk)))
    wuq = wuq.reshape(MLA_Q_RANK, MLA_HEADS * MLA_QK_PAD).astype(BF16)

    w_o0 = w_o[0]
    w_o_p = jnp.concatenate([w_o0[:SWA_Q_COLS][perm], w_o0[SWA_Q_COLS:]], axis=0).astype(BF16)

    w_route = jnp.zeros((D_MODEL, ROUTE_LANES), F32)
    w_route = w_route.at[:, :N_GROUPS].set(moe_w_group[0])
    w_route = w_route.at[:, EXPERT_LANE0:EXPERT_LANE0 + N_EXPERTS].set(moe_w_router[0])
    b_route = jnp.zeros((1, ROUTE_LANES), F32)
    b_route = b_route.at[0, :N_GROUPS].set(moe_b_group[0])
    b_route = b_route.at[0, EXPERT_LANE0:EXPERT_LANE0 + N_EXPERTS].set(moe_b_router[0])

    return {
        "ln_in_g": ln_in_g[None, :], "ln_in_b": ln_in_b[None, :],
        "w_in": w_in_p,
        "sinks": swa_sinks[0],
        "gq": mla_q_norm_g, "w_uq": wuq,
        "gkv": mla_kv_norm_g, "w_ukv": mla_w_ukv[0].astype(BF16),
        "ga": swa_out_norm_g[0][perm][None, :], "gb": mla_out_norm_g,
        "w_o": w_o_p, "ln1_g": ln1_g, "ln1_b": ln1_b,
        "w_route": w_route, "b_route": b_route,
        "w_gu": jnp.concatenate([moe_w_gate[0], moe_w_up[0]], axis=-1).astype(BF16),
        "w_down": moe_w_down[0].astype(BF16),
        "ln2_g": ln2_g, "ln2_b": ln2_b,
    }


def kernel(x, meta_tokens, ln_in_g, ln_in_b, w_in, swa_sinks, mla_q_norm_g, mla_w_uq, mla_kv_norm_g, mla_w_ukv, swa_out_norm_g, mla_out_norm_g, w_o, ln1_g, ln1_b, moe_w_group, moe_b_group, moe_w_router, moe_b_router, moe_w_gate, moe_w_up, moe_w_down, ln2_g, ln2_b):
    B, S, D = x.shape
    p = _prep_params(ln_in_g, ln_in_b, w_in, swa_sinks, mla_q_norm_g, mla_w_uq, mla_kv_norm_g,
                     mla_w_ukv, swa_out_norm_g, mla_out_norm_g, w_o, ln1_g, ln1_b, moe_w_group,
                     moe_b_group, moe_w_router, moe_b_router, moe_w_gate, moe_w_up, moe_w_down,
                     ln2_g, ln2_b)
    tm = min(512, S)

    x_meta = jnp.concatenate([jnp.zeros((N_PAD, D), x.dtype), meta_tokens.astype(x.dtype)])[None]
    pos_meta = jnp.maximum(jnp.arange(BLOCK) - N_PAD, 0).astype(F32)
    pos_tok = (jnp.arange(S) + N_META).astype(F32)
    cos_m, sin_m = _rope_tables(pos_meta)
    cos_t, sin_t = _rope_tables(pos_tok)

    _, ka_m, va_m, _, km_m, vm_m = _in_proj(x_meta, cos_m, sin_m, p, BLOCK)
    qa, ka, va, qm, km, vm = _in_proj(x, cos_t, sin_t, p, tm)

    a_out = _swa(p["sinks"], qa, ka, va, ka_m, va_m, tm)
    b_out = _mla(qm, km, vm, km_m, vm_m, tm)
    h1, comb = _out_proj(x, a_out, b_out, p, tm)
    out = _moe(h1.reshape(B * S, D), comb.reshape(B * S, ROUTE_LANES), p, min(1024, B * S))
    return out.reshape(B, S, D)
```

```python
import functools

import jax
import jax.numpy as jnp
from jax import lax
from jax.experimental import pallas as pl
from jax.experimental.pallas import tpu as pltpu

D_MODEL = 1024
N_META = 16
BLOCK = 128
N_PAD = BLOCK - N_META
ROPE_THETA = 10000.0
SWA_HEADS = 8
SWA_KV_HEADS = 2
SWA_HEAD_DIM = 64
MLA_HEADS = 4
MLA_Q_RANK = 256
MLA_KV_RANK = 256
MLA_NOPE_DIM = 128
MLA_ROPE_DIM = 64
MLA_V_DIM = 128
SWA_Q_COLS = SWA_HEADS * SWA_HEAD_DIM
SWA_KV_COLS = SWA_KV_HEADS * SWA_HEAD_DIM
MLA_OUT_COLS = MLA_HEADS * MLA_V_DIM
N_GROUPS = 4
EXPERTS_PER_GROUP = 8
N_EXPERTS = N_GROUPS * EXPERTS_PER_GROUP
D_EXPERT = 256
LN_EPS = 1e-5
RMS_EPS = 1e-6
DEPTH = 1
ALPHA = (2.0 * DEPTH) ** 0.25
NEG = -1e30

LANES = 128
IN_COLS_PADDED = 1408
MLA_QK_PAD = 256
ROUTE_LANES = 128
EXPERT_LANE0 = 32
ROUTE_E1, ROUTE_E2, ROUTE_W1, ROUTE_W2, ROUTE_R1, ROUTE_R2 = range(6)
TOP_K = 2
MOE_TILE_ROWS = 256
VMEM_LIMIT = 56 * 1024 * 1024

SWA_HEAD_ORDER = (0, 4, 1, 5, 2, 6, 3, 7)

BF16 = jnp.bfloat16
F32 = jnp.float32


def _cparams(sem):
    return pltpu.CompilerParams(dimension_semantics=sem, vmem_limit_bytes=VMEM_LIMIT)


def _layer_norm(x, g, b):
    mu = jnp.mean(x, axis=-1, keepdims=True)
    xc = x - mu
    var = jnp.mean(xc * xc, axis=-1, keepdims=True)
    return xc * lax.rsqrt(var + LN_EPS) * g + b


def _rms_scale(x):
    return lax.rsqrt(jnp.mean(x * x, axis=-1, keepdims=True) + RMS_EPS)


def _rope128(x, cos, sin_signed, lane_lo):
    up = pltpu.roll(x, LANES - 32, 1)
    dn = pltpu.roll(x, 32, 1)
    return x * cos + jnp.where(lane_lo, up, dn) * sin_signed


def _dot(a, b):
    return jnp.dot(a, b, preferred_element_type=F32)


def _dot_nt(a, b):
    return lax.dot_general(a, b, (((1,), (1,)), ((), ())), preferred_element_type=F32)


def _in_proj_kernel(x_ref, g_ref, b_ref, w_ref, cos_ref, sin_ref, gq_ref, wuq_ref, gkv_ref,
                    wukv_ref, qa_ref, ka_ref, va_ref, qm_ref, km_ref, vm_ref):
    h = _layer_norm(x_ref[0], g_ref[...], b_ref[...])
    u = _dot(h.astype(BF16), w_ref[...])
    cos = cos_ref[...]
    sin = sin_ref[...]
    lane = lax.broadcasted_iota(jnp.int32, (1, LANES), 1)
    lane_lo = (lane % 64) < 32
    rope = functools.partial(_rope128, cos=cos, sin_signed=sin, lane_lo=lane_lo)

    swa_scale = SWA_HEAD_DIM ** -0.5
    for c in range(SWA_Q_COLS // LANES):
        qc = rope(u[:, c * LANES:(c + 1) * LANES]) * swa_scale
        qa_ref[0, :, c * LANES:(c + 1) * LANES] = qc.astype(BF16)
    ka_ref[0] = rope(u[:, 512:640]).astype(BF16)
    va_ref[0] = u[:, 640:768].astype(BF16)

    cq = u[:, 768:1024]
    cqn = (cq * _rms_scale(cq) * gq_ref[...]).astype(BF16)
    qm = _dot(cqn, wuq_ref[...])
    mla_scale = (MLA_NOPE_DIM + MLA_ROPE_DIM) ** -0.5
    ckv = u[:, 1024:1280]
    ckvn = (ckv * _rms_scale(ckv) * gkv_ref[...]).astype(BF16)
    kv = _dot(ckvn, wukv_ref[...])
    kr = rope(u[:, 1280:1408]).astype(BF16)
    for hd in range(MLA_HEADS):
        base = hd * MLA_QK_PAD
        qm_ref[0, hd, :, 0:LANES] = (qm[:, base:base + LANES] * mla_scale).astype(BF16)
        qm_ref[0, hd, :, LANES:2 * LANES] = (
            rope(qm[:, base + LANES:base + 2 * LANES]) * mla_scale).astype(BF16)
        km_ref[0, hd, :, 0:LANES] = kv[:, base:base + LANES].astype(BF16)
        km_ref[0, hd, :, LANES:2 * LANES] = kr
        vm_ref[0, hd] = kv[:, base + LANES:base + 2 * LANES].astype(BF16)


def _in_proj(x, cos, sin, p, tm):
    B, S, D = x.shape
    grid = (B, S // tm)
    full = lambda shape: pl.BlockSpec(shape, lambda b, i: (0,) * len(shape))
    out_shape = (
        jax.ShapeDtypeStruct((B, S, SWA_Q_COLS), BF16),
        jax.ShapeDtypeStruct((B, S, SWA_KV_COLS), BF16),
        jax.ShapeDtypeStruct((B, S, SWA_KV_COLS), BF16),
        jax.ShapeDtypeStruct((B, MLA_HEADS, S, MLA_QK_PAD), BF16),
        jax.ShapeDtypeStruct((B, MLA_HEADS, S, MLA_QK_PAD), BF16),
        jax.ShapeDtypeStruct((B, MLA_HEADS, S, MLA_V_DIM), BF16),
    )
    row = lambda w: pl.BlockSpec((1, tm, w), lambda b, i: (b, i, 0))
    head = lambda w: pl.BlockSpec((1, MLA_HEADS, tm, w), lambda b, i: (b, 0, i, 0))
    return pl.pallas_call(
        _in_proj_kernel,
        grid=grid,
        in_specs=[
            row(D),
            full((1, D)), full((1, D)),
            full((D, IN_COLS_PADDED)),
            pl.BlockSpec((tm, LANES), lambda b, i: (i, 0)),
            pl.BlockSpec((tm, LANES), lambda b, i: (i, 0)),
            full((1, MLA_Q_RANK)), full((MLA_Q_RANK, MLA_HEADS * MLA_QK_PAD)),
            full((1, MLA_KV_RANK)), full((MLA_KV_RANK, MLA_HEADS * MLA_QK_PAD)),
        ],
        out_specs=(row(SWA_Q_COLS), row(SWA_KV_COLS), row(SWA_KV_COLS),
                   head(MLA_QK_PAD), head(MLA_QK_PAD), head(MLA_V_DIM)),
        out_shape=out_shape,
        compiler_params=_cparams(("parallel", "parallel")),
        name="in_proj",
    )(x, p["ln_in_g"], p["ln_in_b"], p["w_in"], cos, sin, p["gq"], p["w_uq"], p["gkv"], p["w_ukv"])


def _swa_kernel(sink_ref, qa_ref, kc_ref, vc_ref, kp_ref, vp_ref, kmeta_ref, vmeta_ref, o_ref,
                *, blocks_per_step):
    i = pl.program_id(1)
    first = i == 0
    kprev = jnp.where(first, kmeta_ref[0], kp_ref[0])
    vprev = jnp.where(first, vmeta_ref[0], vp_ref[0])

    row = lax.broadcasted_iota(jnp.int32, (2 * BLOCK, 2 * BLOCK), 0) % BLOCK
    col = lax.broadcasted_iota(jnp.int32, (2 * BLOCK, 2 * BLOCK), 1)
    base_mask = ((col < BLOCK) & (col > row)) | ((col >= BLOCK) & ((col - BLOCK) <= row))
    first_mask = base_mask & (col >= jnp.where(first, N_PAD, 0))
    lane = lax.broadcasted_iota(jnp.int32, (1, LANES), 1)
    lane_kv0 = lane < SWA_HEAD_DIM
    top_rows = lax.broadcasted_iota(jnp.int32, (2 * BLOCK, 1), 0) < BLOCK

    for r in range(blocks_per_step):
        rows = slice(r * BLOCK, (r + 1) * BLOCK)
        if r == 0:
            kp, vp = kprev, vprev
            mask = first_mask
        else:
            prev_rows = slice((r - 1) * BLOCK, r * BLOCK)
            kp, vp = kc_ref[0, prev_rows, :], vc_ref[0, prev_rows, :]
            mask = base_mask
        keys = jnp.concatenate([kp, kc_ref[0, rows, :]], axis=0)
        vals = jnp.concatenate([vp, vc_ref[0, rows, :]], axis=0)
        for c in range(SWA_Q_COLS // LANES):
            qc = qa_ref[0, rows, c * LANES:(c + 1) * LANES]
            zero = jnp.zeros_like(qc)
            q2 = jnp.concatenate([jnp.where(lane_kv0, qc, zero),
                                  jnp.where(lane_kv0, zero, qc)], axis=0)
            s = jnp.where(mask, _dot_nt(q2, keys), NEG)
            sink = jnp.where(top_rows, sink_ref[c], sink_ref[c + 4])
            m = jnp.maximum(jnp.max(s, axis=-1, keepdims=True), sink)
            e = jnp.exp(s - m)
            denom = jnp.sum(e, axis=-1, keepdims=True) + jnp.exp(sink - m)
            pr = (e * (1.0 / denom)).astype(BF16)
            o2 = _dot(pr, vals)
            o = jnp.where(lane_kv0, o2[:BLOCK], o2[BLOCK:])
            o_ref[0, rows, c * LANES:(c + 1) * LANES] = o.astype(BF16)


def _swa(sinks, qa, ka, va, ka_meta, va_meta, tb):
    B, S, _ = qa.shape
    R = tb // BLOCK
    kvw = SWA_KV_COLS
    cur = lambda w: pl.BlockSpec((1, tb, w), lambda b, i: (b, i, 0))
    prev = pl.BlockSpec((1, BLOCK, kvw), lambda b, i: (b, jnp.maximum(i * R - 1, 0), 0))
    meta = pl.BlockSpec((1, BLOCK, kvw), lambda b, i: (0, 0, 0))
    return pl.pallas_call(
        functools.partial(_swa_kernel, blocks_per_step=R),
        grid=(B, S // tb),
        in_specs=[pl.BlockSpec(memory_space=pltpu.SMEM),
                  cur(SWA_Q_COLS), cur(kvw), cur(kvw), prev, prev, meta, meta],
        out_specs=cur(SWA_Q_COLS),
        out_shape=jax.ShapeDtypeStruct((B, S, SWA_Q_COLS), BF16),
        compiler_params=_cparams(("parallel", "parallel")),
        name="swa_attn",
    )(sinks, qa, ka, va, ka, va, ka_meta, va_meta)


def _mla_kernel(q_ref, k_ref, v_ref, kmeta_ref, vmeta_ref, o_ref, *, tq):
    i = pl.program_id(2)
    q = q_ref[0, 0]

    s = _dot_nt(q, kmeta_ref[0, 0])
    col = lax.broadcasted_iota(jnp.int32, (tq, BLOCK), 1)
    s = jnp.where(col >= N_PAD, s, NEG)
    m = jnp.max(s, axis=-1, keepdims=True)
    p = jnp.exp(s - m)
    l = jnp.sum(p, axis=-1, keepdims=True)
    acc = _dot(p.astype(BF16), vmeta_ref[0, 0])

    def step(k, v, carry, mask=None):
        m, l, acc = carry
        s = _dot_nt(q, k)
        if mask is not None:
            s = jnp.where(mask, s, NEG)
        m_new = jnp.maximum(m, jnp.max(s, axis=-1, keepdims=True))
        alpha = jnp.exp(m - m_new)
        p = jnp.exp(s - m_new)
        l = alpha * l + jnp.sum(p, axis=-1, keepdims=True)
        acc = alpha * acc + _dot(p.astype(BF16), v)
        return m_new, l, acc

    def body(j, carry):
        start = pl.multiple_of(j * tq, tq)
        return step(k_ref[0, 0, pl.ds(start, tq), :], v_ref[0, 0, pl.ds(start, tq), :], carry)

    carry = lax.fori_loop(0, i, body, (m, l, acc))
    start = pl.multiple_of(i * tq, tq)
    row = lax.broadcasted_iota(jnp.int32, (tq, tq), 0)
    colq = lax.broadcasted_iota(jnp.int32, (tq, tq), 1)
    m, l, acc = step(k_ref[0, 0, pl.ds(start, tq), :], v_ref[0, 0, pl.ds(start, tq), :], carry,
                     mask=colq <= row)
    o_ref[0] = (acc * (1.0 / l)).astype(BF16)


def _mla(qm, km, vm, km_meta, vm_meta, tq):
    B, H, S, _ = qm.shape
    return pl.pallas_call(
        functools.partial(_mla_kernel, tq=tq),
        grid=(B, H, S // tq),
        in_specs=[
            pl.BlockSpec((1, 1, tq, MLA_QK_PAD), lambda b, h, i: (b, h, i, 0)),
            pl.BlockSpec((1, 1, S, MLA_QK_PAD), lambda b, h, i: (b, h, 0, 0)),
            pl.BlockSpec((1, 1, S, MLA_V_DIM), lambda b, h, i: (b, h, 0, 0)),
            pl.BlockSpec((1, 1, BLOCK, MLA_QK_PAD), lambda b, h, i: (0, h, 0, 0)),
            pl.BlockSpec((1, 1, BLOCK, MLA_V_DIM), lambda b, h, i: (0, h, 0, 0)),
        ],
        out_specs=pl.BlockSpec((1, tq, MLA_V_DIM), lambda b, h, i: (b, i, h)),
        out_shape=jax.ShapeDtypeStruct((B, S, MLA_OUT_COLS), BF16),
        compiler_params=_cparams(("parallel", "parallel", "arbitrary")),
        name="mla_attn",
    )(qm, km, vm, km_meta, vm_meta)


def _out_proj_kernel(x_ref, a_ref, b_ref, gin_ref, bin_ref, ga_ref, gb_ref, wo_ref, g1_ref, b1_ref,
                     wrt_ref, brt_ref, h1_ref, route_ref, counts_out_ref, count_ref):
    h = _layer_norm(x_ref[0], gin_ref[...], bin_ref[...])
    a = a_ref[0].astype(F32)
    b = b_ref[0].astype(F32)
    an = (a * _rms_scale(a) * ga_ref[...]).astype(BF16)
    bn = (b * _rms_scale(b) * gb_ref[...]).astype(BF16)
    mix = _dot(jnp.concatenate([an, bn], axis=-1), wo_ref[...])
    h1 = _layer_norm(ALPHA * h + mix, g1_ref[...], b1_ref[...])
    h1_ref[0] = h1

    logits = jnp.dot(h1, wrt_ref[...], precision=lax.Precision.HIGHEST,
                     preferred_element_type=F32) + brt_ref[...]
    lane = lax.broadcasted_iota(jnp.int32, logits.shape, 1).astype(F32)
    ninf = -jnp.inf
    no_lane = float(ROUTE_LANES)
    gl = jnp.where(lane < N_GROUPS, logits, ninf)
    gmax = jnp.max(gl, axis=-1, keepdims=True)
    g_top = 1.0 / jnp.sum(jnp.exp(gl - gmax), axis=-1, keepdims=True)
    g_idx = jnp.min(jnp.where(gl == gmax, lane, no_lane), axis=-1, keepdims=True)
    lo = EXPERT_LANE0 + g_idx * EXPERTS_PER_GROUP
    el = jnp.where((lane >= lo) & (lane < lo + EXPERTS_PER_GROUP), logits, ninf)
    m1 = jnp.max(el, axis=-1, keepdims=True)
    i1 = jnp.min(jnp.where(el == m1, lane, no_lane), axis=-1, keepdims=True)
    el2 = jnp.where(lane == i1, ninf, el)
    m2 = jnp.max(el2, axis=-1, keepdims=True)
    i2 = jnp.min(jnp.where(el2 == m2, lane, no_lane), axis=-1, keepdims=True)
    e2 = jnp.exp(m2 - m1)
    w1 = g_top / (1.0 + e2)
    w2 = g_top * e2 / (1.0 + e2)

    @pl.when((pl.program_id(0) == 0) & (pl.program_id(1) == 0))
    def _():
        count_ref[...] = jnp.zeros_like(count_ref)

    tm = logits.shape[0]
    onehot = jnp.where((lane == i1) | (lane == i2), 1.0, 0.0)
    r_i = lax.broadcasted_iota(jnp.int32, (tm, tm), 0)
    c_i = lax.broadcasted_iota(jnp.int32, (tm, tm), 1)
    earlier = jnp.where(r_i > c_i, 1.0, 0.0).astype(BF16)
    before = _dot(earlier, onehot.astype(BF16)) + count_ref[...]
    rank1 = jnp.sum(jnp.where(lane == i1, before, 0.0), axis=-1, keepdims=True)
    rank2 = jnp.sum(jnp.where(lane == i2, before, 0.0), axis=-1, keepdims=True)
    count_ref[...] += jnp.sum(onehot, axis=0, keepdims=True)
    counts_out_ref[...] = count_ref[...]

    route = jnp.where(lane == ROUTE_E1, i1 - EXPERT_LANE0, 0.0)
    route = jnp.where(lane == ROUTE_E2, i2 - EXPERT_LANE0, route)
    route = jnp.where(lane == ROUTE_W1, w1, route)
    route = jnp.where(lane == ROUTE_W2, w2, route)
    route = jnp.where(lane == ROUTE_R1, rank1, route)
    route = jnp.where(lane == ROUTE_R2, rank2, route)
    route_ref[0] = route


def _out_proj(x, a_out, b_out, p, tm):
    B, S, D = x.shape
    full = lambda shape: pl.BlockSpec(shape, lambda b, i: (0,) * len(shape))
    row = lambda w: pl.BlockSpec((1, tm, w), lambda b, i: (b, i, 0))
    return pl.pallas_call(
        _out_proj_kernel,
        grid=(B, S // tm),
        in_specs=[row(D), row(SWA_Q_COLS), row(MLA_OUT_COLS),
                  full((1, D)), full((1, D)), full((1, SWA_Q_COLS)), full((1, MLA_OUT_COLS)),
                  full((D, D)), full((1, D)), full((1, D)),
                  full((D, ROUTE_LANES)), full((1, ROUTE_LANES))],
        out_specs=(row(D), row(ROUTE_LANES), full((1, ROUTE_LANES))),
        out_shape=(jax.ShapeDtypeStruct((B, S, D), F32),
                   jax.ShapeDtypeStruct((B, S, ROUTE_LANES), F32),
                   jax.ShapeDtypeStruct((1, ROUTE_LANES), F32)),
        scratch_shapes=[pltpu.VMEM((1, ROUTE_LANES), F32)],
        compiler_params=_cparams(("arbitrary", "arbitrary")),
        name="out_proj_route",
    )(x, a_out, b_out, p["ln_in_g"], p["ln_in_b"], p["ga"], p["gb"], p["w_o"], p["ln1_g"],
      p["ln1_b"], p["w_route"], p["b_route"])


SLAB = D_MODEL // LANES


def _slab_copy(src_ref, src_tok8, dst_ref, dst_tok8, sem):
    return pltpu.make_async_copy(src_ref.at[pl.ds(pl.multiple_of(src_tok8, SLAB), SLAB)],
                                 dst_ref.at[pl.ds(pl.multiple_of(dst_tok8, SLAB), SLAB)], sem)


def _to_slabs(x, slab_ref, n):
    for s in range(SLAB):
        slab_ref[pl.ds(s, n, stride=SLAB), :] = x[:, s * LANES:(s + 1) * LANES]


def _from_slabs(slab_ref, n):
    return jnp.concatenate([slab_ref[pl.ds(s, n, stride=SLAB), :] for s in range(SLAB)], axis=1)


def _dispatch_kernel(padstart_ref, nu_ref, pos_ref, h1_ref, xs_ref, rows_ref, zero_ref, sem, *,
                     tm, n_tiles):
    tail = zero_ref.shape[0]

    def zero_fill(start):
        return pltpu.make_async_copy(
            zero_ref, xs_ref.at[pl.ds(pl.multiple_of(start, SLAB), tail)], sem)

    @pl.when(pl.program_id(0) == 0)
    def _():
        zero_ref[...] = jnp.zeros_like(zero_ref)
        for parity in range(2):
            tails = [zero_fill(padstart_ref[e]) for e in range(parity, N_EXPERTS, 2)]
            for t in tails:
                t.start()
            for t in tails:
                t.wait()

        def fill_unused(j, carry):
            fill = zero_fill(j * tail)
            fill.start()
            fill.wait()
            return carry

        lax.fori_loop(nu_ref[0], n_tiles + 1, fill_unused, 0)

    _to_slabs(h1_ref[...], rows_ref, tm)

    def issue(r, carry):
        for k in range(TOP_K):
            _slab_copy(rows_ref, r * SLAB, xs_ref, pos_ref[0, 0, TOP_K * r + k], sem).start()
        return carry

    lax.fori_loop(0, tm, issue, 0, unroll=8)
    for k in range(TOP_K):
        pltpu.make_async_copy(rows_ref, xs_ref.at[pl.ds(0, tm * SLAB)], sem).wait()


def _dispatch(h1, pos8, pad_start8, n_used, n_tiles, tm):
    T, D = h1.shape
    n_rows = (n_tiles + 1) * MOE_TILE_ROWS
    grid_spec = pltpu.PrefetchScalarGridSpec(
        num_scalar_prefetch=2,
        grid=(T // tm,),
        in_specs=[pl.BlockSpec((1, 1, TOP_K * tm), lambda t, ps, nu: (t, 0, 0),
                               memory_space=pltpu.SMEM),
                  pl.BlockSpec((tm, D), lambda t, ps, nu: (t, 0))],
        out_specs=pl.BlockSpec(memory_space=pl.ANY),
        scratch_shapes=[pltpu.VMEM((tm * SLAB, LANES), F32),
                        pltpu.VMEM((MOE_TILE_ROWS * SLAB, LANES), F32),
                        pltpu.SemaphoreType.DMA],
    )
    return pl.pallas_call(
        functools.partial(_dispatch_kernel, tm=tm, n_tiles=n_tiles),
        grid_spec=grid_spec,
        out_shape=jax.ShapeDtypeStruct((n_rows * SLAB, LANES), F32),
        compiler_params=_cparams(("arbitrary",)),
        name="moe_dispatch",
    )(pad_start8, n_used, pos8, h1)


def _experts_kernel(te_ref, nu_ref, x_ref, wgu_ref, wd_ref, y_ref):
    j = pl.program_id(0)
    tr = MOE_TILE_ROWS

    @pl.when(j < nu_ref[0])
    def _():
        gu = _dot(_from_slabs(x_ref, tr).astype(BF16), wgu_ref[0])
        g = gu[:, :D_EXPERT]
        u = gu[:, D_EXPERT:]
        hid = g * (1.0 / (1.0 + jnp.exp(-g))) * u
        _to_slabs(_dot(hid.astype(BF16), wd_ref[0]), y_ref, tr)

    @pl.when(j >= nu_ref[0])
    def _():
        y_ref[...] = jnp.zeros_like(y_ref)


def _experts(xs, tile_expert, n_used, p, n_tiles):
    D = D_MODEL
    tr = MOE_TILE_ROWS
    last_used = lambda j, nu: jnp.minimum(j, nu[0] - 1)
    grid_spec = pltpu.PrefetchScalarGridSpec(
        num_scalar_prefetch=2,
        grid=(n_tiles,),
        in_specs=[pl.BlockSpec((tr * SLAB, LANES), lambda j, te, nu: (last_used(j, nu), 0)),
                  pl.BlockSpec((1, D, 2 * D_EXPERT), lambda j, te, nu: (te[j], 0, 0)),
                  pl.BlockSpec((1, D_EXPERT, D), lambda j, te, nu: (te[j], 0, 0))],
        out_specs=pl.BlockSpec((tr * SLAB, LANES), lambda j, te, nu: (j, 0)),
    )
    return pl.pallas_call(
        _experts_kernel,
        grid_spec=grid_spec,
        out_shape=jax.ShapeDtypeStruct((n_tiles * tr * SLAB, LANES), F32),
        compiler_params=_cparams(("arbitrary",)),
        name="moe_experts",
    )(tile_expert, n_used, xs, p["w_gu"], p["w_down"])


def _combine_kernel(pos_ref, h1_ref, route_ref, ys_ref, g2_ref, b2_ref, o_ref, ybuf_ref, sem, *, tm):
    def issue(r, carry):
        for k in range(TOP_K):
            _slab_copy(ys_ref, pos_ref[0, 0, TOP_K * r + k], ybuf_ref.at[k], r * SLAB, sem).start()
        return carry

    lax.fori_loop(0, tm, issue, 0, unroll=8)
    for k in range(TOP_K):
        pltpu.make_async_copy(ys_ref.at[pl.ds(0, tm * SLAB)], ybuf_ref.at[k], sem).wait()

    route = route_ref[...]
    w1 = route[:, ROUTE_W1:ROUTE_W1 + 1]
    w2 = route[:, ROUTE_W2:ROUTE_W2 + 1]
    ffn = w1 * _from_slabs(ybuf_ref.at[0], tm) + w2 * _from_slabs(ybuf_ref.at[1], tm)
    o_ref[...] = _layer_norm(ALPHA * h1_ref[...] + ffn, g2_ref[...], b2_ref[...])


def _combine(h1, route, pos8, ys, p, tm):
    T, D = h1.shape
    full = lambda shape: pl.BlockSpec(shape, lambda t: (0,) * len(shape))
    return pl.pallas_call(
        functools.partial(_combine_kernel, tm=tm),
        grid=(T // tm,),
        in_specs=[pl.BlockSpec((1, 1, TOP_K * tm), lambda t: (t, 0, 0), memory_space=pltpu.SMEM),
                  pl.BlockSpec((tm, D), lambda t: (t, 0)),
                  pl.BlockSpec((tm, ROUTE_LANES), lambda t: (t, 0)),
                  pl.BlockSpec(memory_space=pl.ANY),
                  full((1, D)), full((1, D))],
        out_specs=pl.BlockSpec((tm, D), lambda t: (t, 0)),
        out_shape=jax.ShapeDtypeStruct((T, D), F32),
        scratch_shapes=[pltpu.VMEM((TOP_K, tm * SLAB, LANES), F32), pltpu.SemaphoreType.DMA],
        compiler_params=_cparams(("arbitrary",)),
        name="moe_combine_ln2",
    )(pos8, h1, route, ys, p["ln2_g"], p["ln2_b"])


def _dispatch_plan(route, counts, tm):
    T = route.shape[0]
    tr = MOE_TILE_ROWS
    n_tiles = (T * TOP_K) // tr + N_EXPERTS
    cnt = counts[0, EXPERT_LANE0:EXPERT_LANE0 + N_EXPERTS].astype(jnp.int32)
    padded = ((cnt + tr - 1) // tr) * tr
    seg_end = jnp.cumsum(padded)
    seg_start = seg_end - padded
    e = route[:, ROUTE_E1:ROUTE_E2 + 1].astype(jnp.int32)
    rank = route[:, ROUTE_R1:ROUTE_R2 + 1].astype(jnp.int32)
    pos8 = ((seg_start[e] + rank) * SLAB).reshape(T // tm, 1, TOP_K * tm)
    tile_expert = jnp.searchsorted(seg_end, jnp.arange(n_tiles, dtype=jnp.int32) * tr, side="right")
    tile_expert = jnp.minimum(tile_expert, N_EXPERTS - 1).astype(jnp.int32)
    n_used = (seg_end[-1:] // tr).astype(jnp.int32)
    return pos8, (seg_start + cnt) * SLAB, tile_expert, n_used, n_tiles


def _rope_tables(pos):
    d = SWA_HEAD_DIM
    inv_freq = ROPE_THETA ** (-jnp.arange(0, d, 2, dtype=F32) / d)
    ang = pos[:, None] * inv_freq[None, :]
    c, s = jnp.cos(ang), jnp.sin(ang)
    return jnp.concatenate([c, c, c, c], axis=-1), jnp.concatenate([-s, s, -s, s], axis=-1)


def _prep_params(ln_in_g, ln_in_b, w_in, swa_sinks, mla_q_norm_g, mla_w_uq, mla_kv_norm_g, mla_w_ukv,
                 swa_out_norm_g, mla_out_norm_g, w_o, ln1_g, ln1_b, moe_w_group, moe_b_group,
                 moe_w_router, moe_b_router, moe_w_gate, moe_w_up, moe_w_down, ln2_g, ln2_b):
    order = jnp.asarray(SWA_HEAD_ORDER)
    perm = (order[:, None] * SWA_HEAD_DIM + jnp.arange(SWA_HEAD_DIM)[None, :]).reshape(-1)
    w = w_in[0]
    w_q = w[:, :SWA_Q_COLS][:, perm]
    w_pad = jnp.zeros((D_MODEL, IN_COLS_PADDED - w.shape[1]), w.dtype)
    w_in_p = jnp.concatenate([w_q, w[:, SWA_Q_COLS:], w_pad], axis=1).astype(BF16)

    dqk = MLA_NOPE_DIM + MLA_ROPE_DIM
    wuq = mla_w_uq[0].reshape(MLA_Q_RANK, MLA_HEADS, dqk)
    wuq = jnp.pad(wuq, ((0, 0), (0, 0), (0, MLA_QK_PAD - dqk)))
    wuq = wuq.reshape(MLA_Q_RANK, MLA_HEADS * MLA_QK_PAD).astype(BF16)

    w_o0 = w_o[0]
    w_o_p = jnp.concatenate([w_o0[:SWA_Q_COLS][perm], w_o0[SWA_Q_COLS:]], axis=0).astype(BF16)

    w_route = jnp.zeros((D_MODEL, ROUTE_LANES), F32)
    w_route = w_route.at[:, :N_GROUPS].set(moe_w_group[0])
    w_route = w_route.at[:, EXPERT_LANE0:EXPERT_LANE0 + N_EXPERTS].set(moe_w_router[0])
    b_route = jnp.zeros((1, ROUTE_LANES), F32)
    b_route = b_route.at[0, :N_GROUPS].set(moe_b_group[0])
    b_route = b_route.at[0, EXPERT_LANE0:EXPERT_LANE0 + N_EXPERTS].set(moe_b_router[0])

    return {
        "ln_in_g": ln_in_g[None, :], "ln_in_b": ln_in_b[None, :],
        "w_in": w_in_p,
        "sinks": swa_sinks[0],
        "gq": mla_q_norm_g, "w_uq": wuq,
        "gkv": mla_kv_norm_g, "w_ukv": mla_w_ukv[0].astype(BF16),
        "ga": swa_out_norm_g[0][perm][None, :], "gb": mla_out_norm_g,
        "w_o": w_o_p, "ln1_g": ln1_g, "ln1_b": ln1_b,
        "w_route": w_route, "b_route": b_route,
        "w_gu": jnp.concatenate([moe_w_gate[0], moe_w_up[0]], axis=-1).astype(BF16),
        "w_down": moe_w_down[0].astype(BF16),
        "ln2_g": ln2_g, "ln2_b": ln2_b,
    }


def kernel(x, meta_tokens, ln_in_g, ln_in_b, w_in, swa_sinks, mla_q_norm_g, mla_w_uq, mla_kv_norm_g, mla_w_ukv, swa_out_norm_g, mla_out_norm_g, w_o, ln1_g, ln1_b, moe_w_group, moe_b_group, moe_w_router, moe_b_router, moe_w_gate, moe_w_up, moe_w_down, ln2_g, ln2_b):
    B, S, D = x.shape
    p = _prep_params(ln_in_g, ln_in_b, w_in, swa_sinks, mla_q_norm_g, mla_w_uq, mla_kv_norm_g,
                     mla_w_ukv, swa_out_norm_g, mla_out_norm_g, w_o, ln1_g, ln1_b, moe_w_group,
                     moe_b_group, moe_w_router, moe_b_router, moe_w_gate, moe_w_up, moe_w_down,
                     ln2_g, ln2_b)
    tm = min(512, S)

    x_meta = jnp.concatenate([jnp.zeros((N_PAD, D), x.dtype), meta_tokens.astype(x.dtype)])[None]
    pos_meta = jnp.maximum(jnp.arange(BLOCK) - N_PAD, 0).astype(F32)
    pos_tok = (jnp.arange(S) + N_META).astype(F32)
    cos_m, sin_m = _rope_tables(pos_meta)
    cos_t, sin_t = _rope_tables(pos_tok)

    _, ka_m, va_m, _, km_m, vm_m = _in_proj(x_meta, cos_m, sin_m, p, BLOCK)
    qa, ka, va, qm, km, vm = _in_proj(x, cos_t, sin_t, p, tm)

    a_out = _swa(p["sinks"], qa, ka, va, ka_m, va_m, tm)
    b_out = _mla(qm, km, vm, km_m, vm_m, tm)
    h1, route, counts = _out_proj(x, a_out, b_out, p, tm)
    h1 = h1.reshape(B * S, D)
    route = route.reshape(B * S, ROUTE_LANES)

    pos8, pad_start8, tile_expert, n_used, n_tiles = _dispatch_plan(route, counts, tm)
    xs = _dispatch(h1, pos8, pad_start8, n_used, n_tiles, tm)
    ys = _experts(xs, tile_expert, n_used, p, n_tiles)
    out = _combine(h1, route, pos8, ys, p, tm)
    return out.reshape(B, S, D)
```

```python
import functools

import jax
import jax.numpy as jnp
from jax import lax
from jax.experimental import pallas as pl
from jax.experimental.pallas import tpu as pltpu

D_MODEL = 1024
N_META = 16
BLOCK = 128
N_PAD = BLOCK - N_META
ROPE_THETA = 10000.0
SWA_HEADS = 8
SWA_KV_HEADS = 2
SWA_HEAD_DIM = 64
MLA_HEADS = 4
MLA_Q_RANK = 256
MLA_KV_RANK = 256
MLA_NOPE_DIM = 128
MLA_ROPE_DIM = 64
MLA_V_DIM = 128
SWA_Q_COLS = SWA_HEADS * SWA_HEAD_DIM
SWA_KV_COLS = SWA_KV_HEADS * SWA_HEAD_DIM
MLA_OUT_COLS = MLA_HEADS * MLA_V_DIM
N_GROUPS = 4
EXPERTS_PER_GROUP = 8
N_EXPERTS = N_GROUPS * EXPERTS_PER_GROUP
D_EXPERT = 256
LN_EPS = 1e-5
RMS_EPS = 1e-6
DEPTH = 1
ALPHA = (2.0 * DEPTH) ** 0.25
NEG = -1e30
LOG2_E = 1.4426950408889634

LANES = 128
IN_COLS_PADDED = 1408
MLA_QK_PAD = 256
ROUTE_LANES = 128
EXPERT_LANE0 = 32
ROUTE_E1, ROUTE_E2, ROUTE_W1, ROUTE_W2, ROUTE_R1, ROUTE_R2 = range(6)
TOP_K = 2
MOE_TILE_ROWS = 256
VMEM_LIMIT = 56 * 1024 * 1024

SWA_HEAD_ORDER = (0, 4, 1, 5, 2, 6, 3, 7)

BF16 = jnp.bfloat16
F32 = jnp.float32


def _cparams(sem):
    return pltpu.CompilerParams(dimension_semantics=sem, vmem_limit_bytes=VMEM_LIMIT)


def _layer_norm(x, g, b):
    mu = jnp.mean(x, axis=-1, keepdims=True)
    xc = x - mu
    var = jnp.mean(xc * xc, axis=-1, keepdims=True)
    return xc * lax.rsqrt(var + LN_EPS) * g + b


def _rms_scale(x):
    return lax.rsqrt(jnp.mean(x * x, axis=-1, keepdims=True) + RMS_EPS)


def _rope128(x, cos, sin_signed, lane_lo):
    up = pltpu.roll(x, LANES - 32, 1)
    dn = pltpu.roll(x, 32, 1)
    return x * cos + jnp.where(lane_lo, up, dn) * sin_signed


def _dot(a, b):
    return jnp.dot(a, b, preferred_element_type=F32)


def _dot_nt(a, b):
    return lax.dot_general(a, b, (((1,), (1,)), ((), ())), preferred_element_type=F32)


def _in_proj_kernel(x_ref, g_ref, b_ref, w_ref, cos_ref, sin_ref, cost_ref, sint_ref, gq_ref,
                    wuqt_ref, gkv_ref, wuk_ref, wuvt_ref,
                    qa_ref, ka_ref, va_ref, qt_ref, km_ref, vt_ref):
    h = _layer_norm(x_ref[0], g_ref[...], b_ref[...])
    u = _dot(h.astype(BF16), w_ref[...])
    cos = cos_ref[...]
    sin = sin_ref[...]
    lane = lax.broadcasted_iota(jnp.int32, (1, LANES), 1)
    lane_lo = (lane % 64) < 32
    rope = functools.partial(_rope128, cos=cos, sin_signed=sin, lane_lo=lane_lo)

    swa_scale = SWA_HEAD_DIM ** -0.5
    for c in range(SWA_Q_COLS // LANES):
        qc = rope(u[:, c * LANES:(c + 1) * LANES]) * swa_scale
        qa_ref[0, :, c * LANES:(c + 1) * LANES] = qc.astype(BF16)
    ka_ref[0] = rope(u[:, 512:640]).astype(BF16)
    va_ref[0] = u[:, 640:768].astype(BF16)

    cq = u[:, 768:1024]
    cqn = (cq * _rms_scale(cq) * gq_ref[...]).astype(BF16)
    ckv = u[:, 1024:1280]
    ckvn = (ckv * _rms_scale(ckv) * gkv_ref[...]).astype(BF16)
    k_nope = _dot(ckvn, wuk_ref[...])
    kr = rope(u[:, 1280:1408]).astype(BF16)
    q_scale = (MLA_NOPE_DIM + MLA_ROPE_DIM) ** -0.5 * LOG2_E
    cost = cost_ref[...]
    sint = sint_ref[...]
    half = MLA_ROPE_DIM // 2
    for hd in range(MLA_HEADS):
        qt = _dot_nt(wuqt_ref[hd], cqn)
        qr = qt[MLA_NOPE_DIM:MLA_NOPE_DIM + MLA_ROPE_DIM]
        rot = jnp.concatenate([-qr[half:], qr[:half]], axis=0)
        qt_ref[0, hd, 0:MLA_NOPE_DIM, :] = (qt[:MLA_NOPE_DIM] * q_scale).astype(BF16)
        qt_ref[0, hd, MLA_NOPE_DIM:MLA_NOPE_DIM + MLA_ROPE_DIM, :] = (
            (qr * cost + rot * sint) * q_scale).astype(BF16)
        qt_ref[0, hd, MLA_NOPE_DIM + MLA_ROPE_DIM:, :] = (
            qt[MLA_NOPE_DIM + MLA_ROPE_DIM:]).astype(BF16)
        km_ref[0, hd, :, 0:LANES] = k_nope[:, hd * LANES:(hd + 1) * LANES].astype(BF16)
        km_ref[0, hd, :, LANES:2 * LANES] = kr
        vt_ref[0, hd, 0] = _dot_nt(wuvt_ref[hd], ckvn).astype(BF16)


def _in_proj(x, tables, p, tm):
    cos, sin, cos_t, sin_t = tables
    B, S, D = x.shape
    grid = (B, S // tm)
    full = lambda shape: pl.BlockSpec(shape, lambda b, i: (0,) * len(shape))
    out_shape = (
        jax.ShapeDtypeStruct((B, S, SWA_Q_COLS), BF16),
        jax.ShapeDtypeStruct((B, S, SWA_KV_COLS), BF16),
        jax.ShapeDtypeStruct((B, S, SWA_KV_COLS), BF16),
        jax.ShapeDtypeStruct((B, MLA_HEADS, MLA_QK_PAD, S), BF16),
        jax.ShapeDtypeStruct((B, MLA_HEADS, S, MLA_QK_PAD), BF16),
        jax.ShapeDtypeStruct((B, MLA_HEADS, S // tm, MLA_V_DIM, tm), BF16),
    )
    row = lambda w: pl.BlockSpec((1, tm, w), lambda b, i: (b, i, 0))
    head = lambda w: pl.BlockSpec((1, MLA_HEADS, tm, w), lambda b, i: (b, 0, i, 0))
    return pl.pallas_call(
        _in_proj_kernel,
        grid=grid,
        in_specs=[
            row(D),
            full((1, D)), full((1, D)),
            full((D, IN_COLS_PADDED)),
            pl.BlockSpec((tm, LANES), lambda b, i: (i, 0)),
            pl.BlockSpec((tm, LANES), lambda b, i: (i, 0)),
            pl.BlockSpec((MLA_ROPE_DIM, tm), lambda b, i: (0, i)),
            pl.BlockSpec((MLA_ROPE_DIM, tm), lambda b, i: (0, i)),
            full((1, MLA_Q_RANK)), full((MLA_HEADS, MLA_QK_PAD, MLA_Q_RANK)),
            full((1, MLA_KV_RANK)), full((MLA_KV_RANK, MLA_HEADS * MLA_NOPE_DIM)),
            full((MLA_HEADS, MLA_V_DIM, MLA_KV_RANK)),
        ],
        out_specs=(row(SWA_Q_COLS), row(SWA_KV_COLS), row(SWA_KV_COLS),
                   pl.BlockSpec((1, MLA_HEADS, MLA_QK_PAD, tm), lambda b, i: (b, 0, 0, i)),
                   head(MLA_QK_PAD),
                   pl.BlockSpec((1, MLA_HEADS, 1, MLA_V_DIM, tm), lambda b, i: (b, 0, i, 0, 0))),
        out_shape=out_shape,
        compiler_params=_cparams(("parallel", "parallel")),
        name="in_proj",
    )(x, p["ln_in_g"], p["ln_in_b"], p["w_in"], cos, sin, cos_t, sin_t, p["gq"], p["w_uq_t"],
      p["gkv"], p["w_uk"], p["w_uv_t"])


def _swa_kernel(sink_ref, qa_ref, kc_ref, vc_ref, kp_ref, vp_ref, kmeta_ref, vmeta_ref, o_ref,
                *, blocks_per_step):
    i = pl.program_id(1)
    first = i == 0
    kprev = jnp.where(first, kmeta_ref[0], kp_ref[0])
    vprev = jnp.where(first, vmeta_ref[0], vp_ref[0])

    row = lax.broadcasted_iota(jnp.int32, (2 * BLOCK, 2 * BLOCK), 0) % BLOCK
    col = lax.broadcasted_iota(jnp.int32, (2 * BLOCK, 2 * BLOCK), 1)
    base_mask = ((col < BLOCK) & (col > row)) | ((col >= BLOCK) & ((col - BLOCK) <= row))
    first_mask = base_mask & (col >= jnp.where(first, N_PAD, 0))
    lane = lax.broadcasted_iota(jnp.int32, (1, LANES), 1)
    lane_kv0 = lane < SWA_HEAD_DIM
    top_rows = lax.broadcasted_iota(jnp.int32, (2 * BLOCK, 1), 0) < BLOCK

    for r in range(blocks_per_step):
        rows = slice(r * BLOCK, (r + 1) * BLOCK)
        if r == 0:
            kp, vp = kprev, vprev
            mask = first_mask
        else:
            prev_rows = slice((r - 1) * BLOCK, r * BLOCK)
            kp, vp = kc_ref[0, prev_rows, :], vc_ref[0, prev_rows, :]
            mask = base_mask
        keys = jnp.concatenate([kp, kc_ref[0, rows, :]], axis=0)
        vals = jnp.concatenate([vp, vc_ref[0, rows, :]], axis=0)
        for c in range(SWA_Q_COLS // LANES):
            qc = qa_ref[0, rows, c * LANES:(c + 1) * LANES]
            zero = jnp.zeros_like(qc)
            q2 = jnp.concatenate([jnp.where(lane_kv0, qc, zero),
                                  jnp.where(lane_kv0, zero, qc)], axis=0)
            s = jnp.where(mask, _dot_nt(q2, keys), NEG)
            sink = jnp.where(top_rows, sink_ref[c], sink_ref[c + 4])
            m = jnp.maximum(jnp.max(s, axis=-1, keepdims=True), sink)
            e = jnp.exp(s - m)
            denom = jnp.sum(e, axis=-1, keepdims=True) + jnp.exp(sink - m)
            pr = (e * (1.0 / denom)).astype(BF16)
            o2 = _dot(pr, vals)
            o = jnp.where(lane_kv0, o2[:BLOCK], o2[BLOCK:])
            o_ref[0, rows, c * LANES:(c + 1) * LANES] = o.astype(BF16)


def _swa(sinks, qa, ka, va, ka_meta, va_meta, tb):
    B, S, _ = qa.shape
    R = tb // BLOCK
    kvw = SWA_KV_COLS
    cur = lambda w: pl.BlockSpec((1, tb, w), lambda b, i: (b, i, 0))
    prev = pl.BlockSpec((1, BLOCK, kvw), lambda b, i: (b, jnp.maximum(i * R - 1, 0), 0))
    meta = pl.BlockSpec((1, BLOCK, kvw), lambda b, i: (0, 0, 0))
    return pl.pallas_call(
        functools.partial(_swa_kernel, blocks_per_step=R),
        grid=(B, S // tb),
        in_specs=[pl.BlockSpec(memory_space=pltpu.SMEM),
                  cur(SWA_Q_COLS), cur(kvw), cur(kvw), prev, prev, meta, meta],
        out_specs=cur(SWA_Q_COLS),
        out_shape=jax.ShapeDtypeStruct((B, S, SWA_Q_COLS), BF16),
        compiler_params=_cparams(("parallel", "parallel")),
        name="swa_attn",
    )(sinks, qa, ka, va, ka, va, ka_meta, va_meta)


def _mla_kernel(qt_ref, k_ref, vt_ref, kmeta_ref, vtmeta_ref, o_ref,
                sa_ref, sb_ref, sa_max_ref, sb_max_ref, m_ref, l_ref, acc_ref, *, tq):
    i = pl.program_id(2)
    qt = qt_ref[0, 0]

    s = _dot(kmeta_ref[0, 0], qt)
    key = lax.broadcasted_iota(jnp.int32, (BLOCK, tq), 0)
    s = jnp.where(key >= N_PAD, s, NEG)
    m = jnp.max(s, axis=0, keepdims=True)
    p = jnp.exp2(s - m)
    m_ref[...] = m
    l_ref[...] = jnp.sum(p, axis=0, keepdims=True)
    acc_ref[...] = _dot(vtmeta_ref[0, 0, 0], p.astype(BF16))

    def scores_into(s_ref, smax_ref, j):
        start = pl.multiple_of(j * tq, tq)
        s = _dot(k_ref[0, 0, pl.ds(start, tq), :], qt)
        s_ref[...] = s
        smax_ref[...] = jnp.max(s, axis=0, keepdims=True)

    def absorb(s_ref, smax_ref, j, causal=False):
        s = s_ref[...]
        if causal:
            key = lax.broadcasted_iota(jnp.int32, (tq, tq), 0)
            query = lax.broadcasted_iota(jnp.int32, (tq, tq), 1)
            s = jnp.where(key <= query, s, NEG)
            smax = jnp.max(s, axis=0, keepdims=True)
        else:
            smax = smax_ref[...]
        m = m_ref[...]
        m_new = jnp.maximum(m, smax)
        alpha = jnp.exp2(m - m_new)
        p = jnp.exp2(s - m_new)
        m_ref[...] = m_new
        l_ref[...] = alpha * l_ref[...] + jnp.sum(p, axis=0, keepdims=True)
        acc_ref[...] = alpha * acc_ref[...] + _dot(vt_ref[0, 0, j], p.astype(BF16))

    scores_into(sa_ref, sa_max_ref, 0)

    def pair(t, carry):
        scores_into(sb_ref, sb_max_ref, 2 * t + 1)
        absorb(sa_ref, sa_max_ref, 2 * t)
        scores_into(sa_ref, sa_max_ref, 2 * t + 2)
        absorb(sb_ref, sb_max_ref, 2 * t + 1)
        return carry

    lax.fori_loop(0, i // 2, pair, 0)

    @pl.when(i % 2 == 1)
    def _():
        scores_into(sb_ref, sb_max_ref, i)
        absorb(sa_ref, sa_max_ref, i - 1)
        absorb(sb_ref, sb_max_ref, i, causal=True)

    @pl.when(i % 2 == 0)
    def _():
        absorb(sa_ref, sa_max_ref, i, causal=True)

    o_ref[0] = (acc_ref[...] * (1.0 / l_ref[...])).T.astype(BF16)


def _mla(qt, km, vt, km_meta, vt_meta, tq):
    B, H, _, S = qt.shape
    return pl.pallas_call(
        functools.partial(_mla_kernel, tq=tq),
        grid=(B, H, S // tq),
        in_specs=[
            pl.BlockSpec((1, 1, MLA_QK_PAD, tq), lambda b, h, i: (b, h, 0, i)),
            pl.BlockSpec((1, 1, S, MLA_QK_PAD), lambda b, h, i: (b, h, 0, 0)),
            pl.BlockSpec((1, 1, S // tq, MLA_V_DIM, tq), lambda b, h, i: (b, h, 0, 0, 0)),
            pl.BlockSpec((1, 1, BLOCK, MLA_QK_PAD), lambda b, h, i: (0, h, 0, 0)),
            pl.BlockSpec((1, 1, 1, MLA_V_DIM, BLOCK), lambda b, h, i: (0, h, 0, 0, 0)),
        ],
        out_specs=pl.BlockSpec((1, tq, MLA_V_DIM), lambda b, h, i: (b, i, h)),
        out_shape=jax.ShapeDtypeStruct((B, S, MLA_OUT_COLS), BF16),
        scratch_shapes=[pltpu.VMEM((tq, tq), F32), pltpu.VMEM((tq, tq), F32),
                        pltpu.VMEM((1, tq), F32), pltpu.VMEM((1, tq), F32),
                        pltpu.VMEM((1, tq), F32), pltpu.VMEM((1, tq), F32),
                        pltpu.VMEM((MLA_V_DIM, tq), F32)],
        compiler_params=_cparams(("parallel", "parallel", "arbitrary")),
        name="mla_attn",
    )(qt, km, vt, km_meta, vt_meta)


def _out_proj_kernel(x_ref, a_ref, b_ref, gin_ref, bin_ref, ga_ref, gb_ref, wo_ref, g1_ref, b1_ref,
                     wrt_ref, brt_ref, h1_ref, route_ref, counts_out_ref, count_ref):
    h = _layer_norm(x_ref[0], gin_ref[...], bin_ref[...])
    a = a_ref[0].astype(F32)
    b = b_ref[0].astype(F32)
    an = (a * _rms_scale(a) * ga_ref[...]).astype(BF16)
    bn = (b * _rms_scale(b) * gb_ref[...]).astype(BF16)
    mix = _dot(jnp.concatenate([an, bn], axis=-1), wo_ref[...])
    h1 = _layer_norm(ALPHA * h + mix, g1_ref[...], b1_ref[...])
    h1_ref[0] = h1

    logits = jnp.dot(h1, wrt_ref[...], precision=lax.Precision.HIGHEST,
                     preferred_element_type=F32) + brt_ref[...]
    lane = lax.broadcasted_iota(jnp.int32, logits.shape, 1).astype(F32)
    ninf = -jnp.inf
    no_lane = float(ROUTE_LANES)
    gl = jnp.where(lane < N_GROUPS, logits, ninf)
    gmax = jnp.max(gl, axis=-1, keepdims=True)
    g_top = 1.0 / jnp.sum(jnp.exp(gl - gmax), axis=-1, keepdims=True)
    g_idx = jnp.min(jnp.where(gl == gmax, lane, no_lane), axis=-1, keepdims=True)
    lo = EXPERT_LANE0 + g_idx * EXPERTS_PER_GROUP
    el = jnp.where((lane >= lo) & (lane < lo + EXPERTS_PER_GROUP), logits, ninf)
    m1 = jnp.max(el, axis=-1, keepdims=True)
    i1 = jnp.min(jnp.where(el == m1, lane, no_lane), axis=-1, keepdims=True)
    el2 = jnp.where(lane == i1, ninf, el)
    m2 = jnp.max(el2, axis=-1, keepdims=True)
    i2 = jnp.min(jnp.where(el2 == m2, lane, no_lane), axis=-1, keepdims=True)
    e2 = jnp.exp(m2 - m1)
    w1 = g_top / (1.0 + e2)
    w2 = g_top * e2 / (1.0 + e2)

    @pl.when((pl.program_id(0) == 0) & (pl.program_id(1) == 0))
    def _():
        count_ref[...] = jnp.zeros_like(count_ref)

    tm = logits.shape[0]
    onehot = jnp.where((lane == i1) | (lane == i2), 1.0, 0.0)
    r_i = lax.broadcasted_iota(jnp.int32, (tm, tm), 0)
    c_i = lax.broadcasted_iota(jnp.int32, (tm, tm), 1)
    earlier = jnp.where(r_i > c_i, 1.0, 0.0).astype(BF16)
    before = _dot(earlier, onehot.astype(BF16)) + count_ref[...]
    rank1 = jnp.sum(jnp.where(lane == i1, before, 0.0), axis=-1, keepdims=True)
    rank2 = jnp.sum(jnp.where(lane == i2, before, 0.0), axis=-1, keepdims=True)
    count_ref[...] += jnp.sum(onehot, axis=0, keepdims=True)
    counts_out_ref[...] = count_ref[...]

    route = jnp.where(lane == ROUTE_E1, i1 - EXPERT_LANE0, 0.0)
    route = jnp.where(lane == ROUTE_E2, i2 - EXPERT_LANE0, route)
    route = jnp.where(lane == ROUTE_W1, w1, route)
    route = jnp.where(lane == ROUTE_W2, w2, route)
    route = jnp.where(lane == ROUTE_R1, rank1, route)
    route = jnp.where(lane == ROUTE_R2, rank2, route)
    route_ref[0] = route


def _out_proj(x, a_out, b_out, p, tm):
    B, S, D = x.shape
    full = lambda shape: pl.BlockSpec(shape, lambda b, i: (0,) * len(shape))
    row = lambda w: pl.BlockSpec((1, tm, w), lambda b, i: (b, i, 0))
    return pl.pallas_call(
        _out_proj_kernel,
        grid=(B, S // tm),
        in_specs=[row(D), row(SWA_Q_COLS), row(MLA_OUT_COLS),
                  full((1, D)), full((1, D)), full((1, SWA_Q_COLS)), full((1, MLA_OUT_COLS)),
                  full((D, D)), full((1, D)), full((1, D)),
                  full((D, ROUTE_LANES)), full((1, ROUTE_LANES))],
        out_specs=(row(D), row(ROUTE_LANES), full((1, ROUTE_LANES))),
        out_shape=(jax.ShapeDtypeStruct((B, S, D), F32),
                   jax.ShapeDtypeStruct((B, S, ROUTE_LANES), F32),
                   jax.ShapeDtypeStruct((1, ROUTE_LANES), F32)),
        scratch_shapes=[pltpu.VMEM((1, ROUTE_LANES), F32)],
        compiler_params=_cparams(("arbitrary", "arbitrary")),
        name="out_proj_route",
    )(x, a_out, b_out, p["ln_in_g"], p["ln_in_b"], p["ga"], p["gb"], p["w_o"], p["ln1_g"],
      p["ln1_b"], p["w_route"], p["b_route"])


SLAB = D_MODEL // LANES


def _slab_copy(src_ref, src_tok8, dst_ref, dst_tok8, sem):
    return pltpu.make_async_copy(src_ref.at[pl.ds(pl.multiple_of(src_tok8, SLAB), SLAB)],
                                 dst_ref.at[pl.ds(pl.multiple_of(dst_tok8, SLAB), SLAB)], sem)


def _to_slabs(x, slab_ref, n):
    for s in range(SLAB):
        slab_ref[pl.ds(s, n, stride=SLAB), :] = x[:, s * LANES:(s + 1) * LANES]


def _from_slabs(slab_ref, n):
    return jnp.concatenate([slab_ref[pl.ds(s, n, stride=SLAB), :] for s in range(SLAB)], axis=1)


def _dispatch_kernel(padstart_ref, nu_ref, pos_ref, h1_ref, xs_ref, rows_ref, zero_ref, sem, *,
                     tm, n_tiles):
    tail = zero_ref.shape[0]

    def zero_fill(start):
        return pltpu.make_async_copy(
            zero_ref, xs_ref.at[pl.ds(pl.multiple_of(start, SLAB), tail)], sem)

    @pl.when(pl.program_id(0) == 0)
    def _():
        zero_ref[...] = jnp.zeros_like(zero_ref)
        for parity in range(2):
            tails = [zero_fill(padstart_ref[e]) for e in range(parity, N_EXPERTS, 2)]
            for t in tails:
                t.start()
            for t in tails:
                t.wait()

        def fill_unused(j, carry):
            fill = zero_fill(j * tail)
            fill.start()
            fill.wait()
            return carry

        lax.fori_loop(nu_ref[0], n_tiles + 1, fill_unused, 0)

    _to_slabs(h1_ref[...], rows_ref, tm)

    def issue(r, carry):
        for k in range(TOP_K):
            _slab_copy(rows_ref, r * SLAB, xs_ref, pos_ref[0, 0, TOP_K * r + k], sem).start()
        return carry

    lax.fori_loop(0, tm, issue, 0, unroll=8)
    for k in range(TOP_K):
        pltpu.make_async_copy(rows_ref, xs_ref.at[pl.ds(0, tm * SLAB)], sem).wait()


def _dispatch(h1, pos8, pad_start8, n_used, n_tiles, tm):
    T, D = h1.shape
    n_rows = (n_tiles + 1) * MOE_TILE_ROWS
    grid_spec = pltpu.PrefetchScalarGridSpec(
        num_scalar_prefetch=2,
        grid=(T // tm,),
        in_specs=[pl.BlockSpec((1, 1, TOP_K * tm), lambda t, ps, nu: (t, 0, 0),
                               memory_space=pltpu.SMEM),
                  pl.BlockSpec((tm, D), lambda t, ps, nu: (t, 0))],
        out_specs=pl.BlockSpec(memory_space=pl.ANY),
        scratch_shapes=[pltpu.VMEM((tm * SLAB, LANES), F32),
                        pltpu.VMEM((MOE_TILE_ROWS * SLAB, LANES), F32),
                        pltpu.SemaphoreType.DMA],
    )
    return pl.pallas_call(
        functools.partial(_dispatch_kernel, tm=tm, n_tiles=n_tiles),
        grid_spec=grid_spec,
        out_shape=jax.ShapeDtypeStruct((n_rows * SLAB, LANES), F32),
        compiler_params=_cparams(("arbitrary",)),
        name="moe_dispatch",
    )(pad_start8, n_used, pos8, h1)


def _experts_kernel(te_ref, nu_ref, x_ref, wgu_ref, wd_ref, y_ref):
    j = pl.program_id(0)
    tr = MOE_TILE_ROWS

    @pl.when(j < nu_ref[0])
    def _():
        gu = _dot(_from_slabs(x_ref, tr).astype(BF16), wgu_ref[0])
        g = gu[:, :D_EXPERT]
        u = gu[:, D_EXPERT:]
        hid = g * (1.0 / (1.0 + jnp.exp(-g))) * u
        _to_slabs(_dot(hid.astype(BF16), wd_ref[0]), y_ref, tr)

    @pl.when(j >= nu_ref[0])
    def _():
        y_ref[...] = jnp.zeros_like(y_ref)


def _experts(xs, tile_expert, n_used, p, n_tiles):
    D = D_MODEL
    tr = MOE_TILE_ROWS
    last_used = lambda j, nu: jnp.minimum(j, nu[0] - 1)
    grid_spec = pltpu.PrefetchScalarGridSpec(
        num_scalar_prefetch=2,
        grid=(n_tiles,),
        in_specs=[pl.BlockSpec((tr * SLAB, LANES), lambda j, te, nu: (last_used(j, nu), 0)),
                  pl.BlockSpec((1, D, 2 * D_EXPERT), lambda j, te, nu: (te[j], 0, 0)),
                  pl.BlockSpec((1, D_EXPERT, D), lambda j, te, nu: (te[j], 0, 0))],
        out_specs=pl.BlockSpec((tr * SLAB, LANES), lambda j, te, nu: (j, 0)),
    )
    return pl.pallas_call(
        _experts_kernel,
        grid_spec=grid_spec,
        out_shape=jax.ShapeDtypeStruct((n_tiles * tr * SLAB, LANES), F32),
        compiler_params=_cparams(("arbitrary",)),
        name="moe_experts",
    )(tile_expert, n_used, xs, p["w_gu"], p["w_down"])


def _combine_kernel(pos_ref, h1_ref, route_ref, ys_ref, g2_ref, b2_ref, o_ref, ybuf_ref, sem, *, tm):
    def issue(r, carry):
        for k in range(TOP_K):
            _slab_copy(ys_ref, pos_ref[0, 0, TOP_K * r + k], ybuf_ref.at[k], r * SLAB, sem).start()
        return carry

    lax.fori_loop(0, tm, issue, 0, unroll=8)
    for k in range(TOP_K):
        pltpu.make_async_copy(ys_ref.at[pl.ds(0, tm * SLAB)], ybuf_ref.at[k], sem).wait()

    route = route_ref[...]
    w1 = route[:, ROUTE_W1:ROUTE_W1 + 1]
    w2 = route[:, ROUTE_W2:ROUTE_W2 + 1]
    ffn = w1 * _from_slabs(ybuf_ref.at[0], tm) + w2 * _from_slabs(ybuf_ref.at[1], tm)
    o_ref[...] = _layer_norm(ALPHA * h1_ref[...] + ffn, g2_ref[...], b2_ref[...])


def _combine(h1, route, pos8, ys, p, tm):
    T, D = h1.shape
    full = lambda shape: pl.BlockSpec(shape, lambda t: (0,) * len(shape))
    return pl.pallas_call(
        functools.partial(_combine_kernel, tm=tm),
        grid=(T // tm,),
        in_specs=[pl.BlockSpec((1, 1, TOP_K * tm), lambda t: (t, 0, 0), memory_space=pltpu.SMEM),
                  pl.BlockSpec((tm, D), lambda t: (t, 0)),
                  pl.BlockSpec((tm, ROUTE_LANES), lambda t: (t, 0)),
                  pl.BlockSpec(memory_space=pl.ANY),
                  full((1, D)), full((1, D))],
        out_specs=pl.BlockSpec((tm, D), lambda t: (t, 0)),
        out_shape=jax.ShapeDtypeStruct((T, D), F32),
        scratch_shapes=[pltpu.VMEM((TOP_K, tm * SLAB, LANES), F32), pltpu.SemaphoreType.DMA],
        compiler_params=_cparams(("arbitrary",)),
        name="moe_combine_ln2",
    )(pos8, h1, route, ys, p["ln2_g"], p["ln2_b"])


def _dispatch_plan(route, counts, tm):
    T = route.shape[0]
    tr = MOE_TILE_ROWS
    n_tiles = (T * TOP_K) // tr + N_EXPERTS
    cnt = counts[0, EXPERT_LANE0:EXPERT_LANE0 + N_EXPERTS].astype(jnp.int32)
    padded = ((cnt + tr - 1) // tr) * tr
    seg_end = jnp.cumsum(padded)
    seg_start = seg_end - padded
    e = route[:, ROUTE_E1:ROUTE_E2 + 1].astype(jnp.int32)
    rank = route[:, ROUTE_R1:ROUTE_R2 + 1].astype(jnp.int32)
    experts = jnp.arange(N_EXPERTS, dtype=jnp.int32)
    start_of = jnp.sum(jnp.where(e[..., None] == experts, seg_start, 0), axis=-1)
    pos8 = ((start_of + rank) * SLAB).reshape(T // tm, 1, TOP_K * tm)
    tile_row0 = jnp.arange(n_tiles, dtype=jnp.int32) * tr
    tile_expert = jnp.sum((seg_end[None, :] <= tile_row0[:, None]).astype(jnp.int32), axis=-1)
    tile_expert = jnp.minimum(tile_expert, N_EXPERTS - 1)
    n_used = (seg_end[-1:] // tr).astype(jnp.int32)
    return pos8, (seg_start + cnt) * SLAB, tile_expert, n_used, n_tiles


def _rope_tables(pos):
    d = SWA_HEAD_DIM
    inv_freq = ROPE_THETA ** (-jnp.arange(0, d, 2, dtype=F32) / d)
    ang = pos[:, None] * inv_freq[None, :]
    c, s = jnp.cos(ang), jnp.sin(ang)
    return (jnp.concatenate([c, c, c, c], axis=-1), jnp.concatenate([-s, s, -s, s], axis=-1),
            jnp.concatenate([c, c], axis=-1).T, jnp.concatenate([s, s], axis=-1).T)


def _prep_params(ln_in_g, ln_in_b, w_in, swa_sinks, mla_q_norm_g, mla_w_uq, mla_kv_norm_g, mla_w_ukv,
                 swa_out_norm_g, mla_out_norm_g, w_o, ln1_g, ln1_b, moe_w_group, moe_b_group,
                 moe_w_router, moe_b_router, moe_w_gate, moe_w_up, moe_w_down, ln2_g, ln2_b):
    order = jnp.asarray(SWA_HEAD_ORDER)
    perm = (order[:, None] * SWA_HEAD_DIM + jnp.arange(SWA_HEAD_DIM)[None, :]).reshape(-1)
    w = w_in[0]
    w_q = w[:, :SWA_Q_COLS][:, perm]
    w_pad = jnp.zeros((D_MODEL, IN_COLS_PADDED - w.shape[1]), w.dtype)
    w_in_p = jnp.concatenate([w_q, w[:, SWA_Q_COLS:], w_pad], axis=1).astype(BF16)

    dqk = MLA_NOPE_DIM + MLA_ROPE_DIM
    wuq = mla_w_uq[0].reshape(MLA_Q_RANK, MLA_HEADS, dqk)
    wuq = jnp.pad(wuq, ((0, 0), (0, 0), (0, MLA_QK_PAD - dqk)))
    wuq_t = jnp.transpose(wuq, (1, 2, 0)).astype(BF16)
    wukv = mla_w_ukv[0].reshape(MLA_KV_RANK, MLA_HEADS, MLA_NOPE_DIM + MLA_V_DIM)
    wuk = wukv[:, :, :MLA_NOPE_DIM].reshape(MLA_KV_RANK, MLA_HEADS * MLA_NOPE_DIM).astype(BF16)
    wuv_t = jnp.transpose(wukv[:, :, MLA_NOPE_DIM:], (1, 2, 0)).astype(BF16)

    w_o0 = w_o[0]
    w_o_p = jnp.concatenate([w_o0[:SWA_Q_COLS][perm], w_o0[SWA_Q_COLS:]], axis=0).astype(BF16)

    w_route = jnp.zeros((D_MODEL, ROUTE_LANES), F32)
    w_route = w_route.at[:, :N_GROUPS].set(moe_w_group[0])
    w_route = w_route.at[:, EXPERT_LANE0:EXPERT_LANE0 + N_EXPERTS].set(moe_w_router[0])
    b_route = jnp.zeros((1, ROUTE_LANES), F32)
    b_route = b_route.at[0, :N_GROUPS].set(moe_b_group[0])
    b_route = b_route.at[0, EXPERT_LANE0:EXPERT_LANE0 + N_EXPERTS].set(moe_b_router[0])

    return {
        "ln_in_g": ln_in_g[None, :], "ln_in_b": ln_in_b[None, :],
        "w_in": w_in_p,
        "sinks": swa_sinks[0],
        "gq": mla_q_norm_g, "w_uq_t": wuq_t,
        "gkv": mla_kv_norm_g, "w_uk": wuk, "w_uv_t": wuv_t,
        "ga": swa_out_norm_g[0][perm][None, :], "gb": mla_out_norm_g,
        "w_o": w_o_p, "ln1_g": ln1_g, "ln1_b": ln1_b,
        "w_route": w_route, "b_route": b_route,
        "w_gu": jnp.concatenate([moe_w_gate[0], moe_w_up[0]], axis=-1).astype(BF16),
        "w_down": moe_w_down[0].astype(BF16),
        "ln2_g": ln2_g, "ln2_b": ln2_b,
    }


def kernel(x, meta_tokens, ln_in_g, ln_in_b, w_in, swa_sinks, mla_q_norm_g, mla_w_uq, mla_kv_norm_g, mla_w_ukv, swa_out_norm_g, mla_out_norm_g, w_o, ln1_g, ln1_b, moe_w_group, moe_b_group, moe_w_router, moe_b_router, moe_w_gate, moe_w_up, moe_w_down, ln2_g, ln2_b):
    B, S, D = x.shape
    p = _prep_params(ln_in_g, ln_in_b, w_in, swa_sinks, mla_q_norm_g, mla_w_uq, mla_kv_norm_g,
                     mla_w_ukv, swa_out_norm_g, mla_out_norm_g, w_o, ln1_g, ln1_b, moe_w_group,
                     moe_b_group, moe_w_router, moe_b_router, moe_w_gate, moe_w_up, moe_w_down,
                     ln2_g, ln2_b)
    tm = min(512, S)

    x_meta = jnp.concatenate([jnp.zeros((N_PAD, D), x.dtype), meta_tokens.astype(x.dtype)])[None]
    pos_meta = jnp.maximum(jnp.arange(BLOCK) - N_PAD, 0).astype(F32)
    pos_tok = (jnp.arange(S) + N_META).astype(F32)
    _, ka_m, va_m, _, km_m, vt_m = _in_proj(x_meta, _rope_tables(pos_meta), p, BLOCK)
    qa, ka, va, qt, km, vt = _in_proj(x, _rope_tables(pos_tok), p, tm)

    a_out = _swa(p["sinks"], qa, ka, va, ka_m, va_m, tm)
    b_out = _mla(qt, km, vt, km_m, vt_m, tm)
    h1, route, counts = _out_proj(x, a_out, b_out, p, tm)
    h1 = h1.reshape(B * S, D)
    route = route.reshape(B * S, ROUTE_LANES)

    pos8, pad_start8, tile_expert, n_used, n_tiles = _dispatch_plan(route, counts, tm)
    xs = _dispatch(h1, pos8, pad_start8, n_used, n_tiles, tm)
    ys = _experts(xs, tile_expert, n_used, p, n_tiles)
    out = _combine(h1, route, pos8, ys, p, tm)
    return out.reshape(B, S, D)
```

```python
import functools

import jax
import jax.numpy as jnp
from jax import lax
from jax.experimental import pallas as pl
from jax.experimental.pallas import tpu as pltpu

D_MODEL = 1024
N_META = 16
BLOCK = 128
N_PAD = BLOCK - N_META
ROPE_THETA = 10000.0
SWA_HEADS = 8
SWA_KV_HEADS = 2
SWA_HEAD_DIM = 64
MLA_HEADS = 4
MLA_Q_RANK = 256
MLA_KV_RANK = 256
MLA_NOPE_DIM = 128
MLA_ROPE_DIM = 64
MLA_V_DIM = 128
SWA_Q_COLS = SWA_HEADS * SWA_HEAD_DIM
SWA_KV_COLS = SWA_KV_HEADS * SWA_HEAD_DIM
MLA_OUT_COLS = MLA_HEADS * MLA_V_DIM
N_GROUPS = 4
EXPERTS_PER_GROUP = 8
N_EXPERTS = N_GROUPS * EXPERTS_PER_GROUP
D_EXPERT = 256
LN_EPS = 1e-5
RMS_EPS = 1e-6
DEPTH = 1
ALPHA = (2.0 * DEPTH) ** 0.25
NEG = -1e30
LOG2_E = 1.4426950408889634

LANES = 128
IN_COLS_PADDED = 1408
MLA_QK_PAD = 256
ROUTE_LANES = 128
EXPERT_LANE0 = 32
ROUTE_CLS, ROUTE_W_LO, ROUTE_W_HI, ROUTE_RANK = range(4)
TOP_K = 2
PAIRS_PER_GROUP = EXPERTS_PER_GROUP * (EXPERTS_PER_GROUP - 1) // 2
N_CLASSES = N_GROUPS * PAIRS_PER_GROUP
MOE_TILE_ROWS = 128
VMEM_LIMIT = 56 * 1024 * 1024

SWA_HEAD_ORDER = (0, 4, 1, 5, 2, 6, 3, 7)

BF16 = jnp.bfloat16
F32 = jnp.float32


def _cparams(sem):
    return pltpu.CompilerParams(dimension_semantics=sem, vmem_limit_bytes=VMEM_LIMIT)


def _layer_norm(x, g, b):
    mu = jnp.mean(x, axis=-1, keepdims=True)
    xc = x - mu
    var = jnp.mean(xc * xc, axis=-1, keepdims=True)
    return xc * lax.rsqrt(var + LN_EPS) * g + b


def _rms_scale(x):
    return lax.rsqrt(jnp.mean(x * x, axis=-1, keepdims=True) + RMS_EPS)


def _rope128(x, cos, sin_signed, lane_lo):
    up = pltpu.roll(x, LANES - 32, 1)
    dn = pltpu.roll(x, 32, 1)
    return x * cos + jnp.where(lane_lo, up, dn) * sin_signed


def _dot(a, b):
    return jnp.dot(a, b, preferred_element_type=F32)


def _dot_nt(a, b):
    return lax.dot_general(a, b, (((1,), (1,)), ((), ())), preferred_element_type=F32)


def _in_proj_kernel(x_ref, g_ref, b_ref, w_ref, cos_ref, sin_ref, cost_ref, sint_ref, gq_ref,
                    wuqt_ref, gkv_ref, wuk_ref, wuvt_ref,
                    qa_ref, ka_ref, va_ref, qt_ref, km_ref, vt_ref):
    h = _layer_norm(x_ref[0], g_ref[...], b_ref[...])
    u = _dot(h.astype(BF16), w_ref[...])
    cos = cos_ref[...]
    sin = sin_ref[...]
    lane = lax.broadcasted_iota(jnp.int32, (1, LANES), 1)
    lane_lo = (lane % 64) < 32
    rope = functools.partial(_rope128, cos=cos, sin_signed=sin, lane_lo=lane_lo)

    swa_scale = SWA_HEAD_DIM ** -0.5
    for c in range(SWA_Q_COLS // LANES):
        qc = rope(u[:, c * LANES:(c + 1) * LANES]) * swa_scale
        qa_ref[0, :, c * LANES:(c + 1) * LANES] = qc.astype(BF16)
    ka_ref[0] = rope(u[:, 512:640]).astype(BF16)
    va_ref[0] = u[:, 640:768].astype(BF16)

    cq = u[:, 768:1024]
    cqn = (cq * _rms_scale(cq) * gq_ref[...]).astype(BF16)
    ckv = u[:, 1024:1280]
    ckvn = (ckv * _rms_scale(ckv) * gkv_ref[...]).astype(BF16)
    k_nope = _dot(ckvn, wuk_ref[...])
    kr = rope(u[:, 1280:1408]).astype(BF16)
    q_scale = (MLA_NOPE_DIM + MLA_ROPE_DIM) ** -0.5 * LOG2_E
    cost = cost_ref[...]
    sint = sint_ref[...]
    half = MLA_ROPE_DIM // 2
    for hd in range(MLA_HEADS):
        qt = _dot_nt(wuqt_ref[hd], cqn)
        qr = qt[MLA_NOPE_DIM:MLA_NOPE_DIM + MLA_ROPE_DIM]
        rot = jnp.concatenate([-qr[half:], qr[:half]], axis=0)
        qt_ref[0, hd, 0:MLA_NOPE_DIM, :] = (qt[:MLA_NOPE_DIM] * q_scale).astype(BF16)
        qt_ref[0, hd, MLA_NOPE_DIM:MLA_NOPE_DIM + MLA_ROPE_DIM, :] = (
            (qr * cost + rot * sint) * q_scale).astype(BF16)
        qt_ref[0, hd, MLA_NOPE_DIM + MLA_ROPE_DIM:, :] = (
            qt[MLA_NOPE_DIM + MLA_ROPE_DIM:]).astype(BF16)
        km_ref[0, hd, :, 0:LANES] = k_nope[:, hd * LANES:(hd + 1) * LANES].astype(BF16)
        km_ref[0, hd, :, LANES:2 * LANES] = kr
        vt_ref[0, hd, 0] = _dot_nt(wuvt_ref[hd], ckvn).astype(BF16)


def _in_proj(x, tables, p, tm):
    cos, sin, cos_t, sin_t = tables
    B, S, D = x.shape
    grid = (B, S // tm)
    full = lambda shape: pl.BlockSpec(shape, lambda b, i: (0,) * len(shape))
    out_shape = (
        jax.ShapeDtypeStruct((B, S, SWA_Q_COLS), BF16),
        jax.ShapeDtypeStruct((B, S, SWA_KV_COLS), BF16),
        jax.ShapeDtypeStruct((B, S, SWA_KV_COLS), BF16),
        jax.ShapeDtypeStruct((B, MLA_HEADS, MLA_QK_PAD, S), BF16),
        jax.ShapeDtypeStruct((B, MLA_HEADS, S, MLA_QK_PAD), BF16),
        jax.ShapeDtypeStruct((B, MLA_HEADS, S // tm, MLA_V_DIM, tm), BF16),
    )
    row = lambda w: pl.BlockSpec((1, tm, w), lambda b, i: (b, i, 0))
    head = lambda w: pl.BlockSpec((1, MLA_HEADS, tm, w), lambda b, i: (b, 0, i, 0))
    return pl.pallas_call(
        _in_proj_kernel,
        grid=grid,
        in_specs=[
            row(D),
            full((1, D)), full((1, D)),
            full((D, IN_COLS_PADDED)),
            pl.BlockSpec((tm, LANES), lambda b, i: (i, 0)),
            pl.BlockSpec((tm, LANES), lambda b, i: (i, 0)),
            pl.BlockSpec((MLA_ROPE_DIM, tm), lambda b, i: (0, i)),
            pl.BlockSpec((MLA_ROPE_DIM, tm), lambda b, i: (0, i)),
            full((1, MLA_Q_RANK)), full((MLA_HEADS, MLA_QK_PAD, MLA_Q_RANK)),
            full((1, MLA_KV_RANK)), full((MLA_KV_RANK, MLA_HEADS * MLA_NOPE_DIM)),
            full((MLA_HEADS, MLA_V_DIM, MLA_KV_RANK)),
        ],
        out_specs=(row(SWA_Q_COLS), row(SWA_KV_COLS), row(SWA_KV_COLS),
                   pl.BlockSpec((1, MLA_HEADS, MLA_QK_PAD, tm), lambda b, i: (b, 0, 0, i)),
                   head(MLA_QK_PAD),
                   pl.BlockSpec((1, MLA_HEADS, 1, MLA_V_DIM, tm), lambda b, i: (b, 0, i, 0, 0))),
        out_shape=out_shape,
        compiler_params=_cparams(("parallel", "parallel")),
        name="in_proj",
    )(x, p["ln_in_g"], p["ln_in_b"], p["w_in"], cos, sin, cos_t, sin_t, p["gq"], p["w_uq_t"],
      p["gkv"], p["w_uk"], p["w_uv_t"])


def _swa_kernel(sink_ref, qa_ref, kc_ref, vc_ref, kp_ref, vp_ref, kmeta_ref, vmeta_ref, o_ref,
                *, blocks_per_step):
    i = pl.program_id(1)
    first = i == 0
    kprev = jnp.where(first, kmeta_ref[0], kp_ref[0])
    vprev = jnp.where(first, vmeta_ref[0], vp_ref[0])

    row = lax.broadcasted_iota(jnp.int32, (2 * BLOCK, 2 * BLOCK), 0) % BLOCK
    col = lax.broadcasted_iota(jnp.int32, (2 * BLOCK, 2 * BLOCK), 1)
    base_mask = ((col < BLOCK) & (col > row)) | ((col >= BLOCK) & ((col - BLOCK) <= row))
    first_mask = base_mask & (col >= jnp.where(first, N_PAD, 0))
    lane = lax.broadcasted_iota(jnp.int32, (1, LANES), 1)
    lane_kv0 = lane < SWA_HEAD_DIM
    top_rows = lax.broadcasted_iota(jnp.int32, (2 * BLOCK, 1), 0) < BLOCK

    for r in range(blocks_per_step):
        rows = slice(r * BLOCK, (r + 1) * BLOCK)
        if r == 0:
            kp, vp = kprev, vprev
            mask = first_mask
        else:
            prev_rows = slice((r - 1) * BLOCK, r * BLOCK)
            kp, vp = kc_ref[0, prev_rows, :], vc_ref[0, prev_rows, :]
            mask = base_mask
        keys = jnp.concatenate([kp, kc_ref[0, rows, :]], axis=0)
        vals = jnp.concatenate([vp, vc_ref[0, rows, :]], axis=0)
        for c in range(SWA_Q_COLS // LANES):
            qc = qa_ref[0, rows, c * LANES:(c + 1) * LANES]
            zero = jnp.zeros_like(qc)
            q2 = jnp.concatenate([jnp.where(lane_kv0, qc, zero),
                                  jnp.where(lane_kv0, zero, qc)], axis=0)
            s = jnp.where(mask, _dot_nt(q2, keys), NEG)
            sink = jnp.where(top_rows, sink_ref[c], sink_ref[c + 4])
            m = jnp.maximum(jnp.max(s, axis=-1, keepdims=True), sink)
            e = jnp.exp(s - m)
            denom = jnp.sum(e, axis=-1, keepdims=True) + jnp.exp(sink - m)
            pr = (e * (1.0 / denom)).astype(BF16)
            o2 = _dot(pr, vals)
            o = jnp.where(lane_kv0, o2[:BLOCK], o2[BLOCK:])
            o_ref[0, rows, c * LANES:(c + 1) * LANES] = o.astype(BF16)


def _swa(sinks, qa, ka, va, ka_meta, va_meta, tb):
    B, S, _ = qa.shape
    R = tb // BLOCK
    kvw = SWA_KV_COLS
    cur = lambda w: pl.BlockSpec((1, tb, w), lambda b, i: (b, i, 0))
    prev = pl.BlockSpec((1, BLOCK, kvw), lambda b, i: (b, jnp.maximum(i * R - 1, 0), 0))
    meta = pl.BlockSpec((1, BLOCK, kvw), lambda b, i: (0, 0, 0))
    return pl.pallas_call(
        functools.partial(_swa_kernel, blocks_per_step=R),
        grid=(B, S // tb),
        in_specs=[pl.BlockSpec(memory_space=pltpu.SMEM),
                  cur(SWA_Q_COLS), cur(kvw), cur(kvw), prev, prev, meta, meta],
        out_specs=cur(SWA_Q_COLS),
        out_shape=jax.ShapeDtypeStruct((B, S, SWA_Q_COLS), BF16),
        compiler_params=_cparams(("parallel", "parallel")),
        name="swa_attn",
    )(sinks, qa, ka, va, ka, va, ka_meta, va_meta)


def _mla_kernel(qt_ref, k_ref, vt_ref, kmeta_ref, vtmeta_ref, o_ref,
                sa_ref, sb_ref, sa_max_ref, sb_max_ref, m_ref, l_ref, acc_ref, *, tq):
    i = pl.program_id(2)
    qt = qt_ref[0, 0]

    s = _dot(kmeta_ref[0, 0], qt)
    key = lax.broadcasted_iota(jnp.int32, (BLOCK, tq), 0)
    s = jnp.where(key >= N_PAD, s, NEG)
    m = jnp.max(s, axis=0, keepdims=True)
    p = jnp.exp2(s - m)
    m_ref[...] = m
    l_ref[...] = jnp.sum(p, axis=0, keepdims=True)
    acc_ref[...] = _dot(vtmeta_ref[0, 0, 0], p.astype(BF16))

    def scores_into(s_ref, smax_ref, j):
        start = pl.multiple_of(j * tq, tq)
        s = _dot(k_ref[0, 0, pl.ds(start, tq), :], qt)
        s_ref[...] = s
        smax_ref[...] = jnp.max(s, axis=0, keepdims=True)

    def absorb(s_ref, smax_ref, j, causal=False):
        s = s_ref[...]
        if causal:
            key = lax.broadcasted_iota(jnp.int32, (tq, tq), 0)
            query = lax.broadcasted_iota(jnp.int32, (tq, tq), 1)
            s = jnp.where(key <= query, s, NEG)
            smax = jnp.max(s, axis=0, keepdims=True)
        else:
            smax = smax_ref[...]
        m = m_ref[...]
        m_new = jnp.maximum(m, smax)
        alpha = jnp.exp2(m - m_new)
        p = jnp.exp2(s - m_new)
        m_ref[...] = m_new
        l_ref[...] = alpha * l_ref[...] + jnp.sum(p, axis=0, keepdims=True)
        acc_ref[...] = alpha * acc_ref[...] + _dot(vt_ref[0, 0, j], p.astype(BF16))

    scores_into(sa_ref, sa_max_ref, 0)

    def pair(t, carry):
        scores_into(sb_ref, sb_max_ref, 2 * t + 1)
        absorb(sa_ref, sa_max_ref, 2 * t)
        scores_into(sa_ref, sa_max_ref, 2 * t + 2)
        absorb(sb_ref, sb_max_ref, 2 * t + 1)
        return carry

    lax.fori_loop(0, i // 2, pair, 0)

    @pl.when(i % 2 == 1)
    def _():
        scores_into(sb_ref, sb_max_ref, i)
        absorb(sa_ref, sa_max_ref, i - 1)
        absorb(sb_ref, sb_max_ref, i, causal=True)

    @pl.when(i % 2 == 0)
    def _():
        absorb(sa_ref, sa_max_ref, i, causal=True)

    o_ref[0] = (acc_ref[...] * (1.0 / l_ref[...])).T.astype(BF16)


def _mla(qt, km, vt, km_meta, vt_meta, tq):
    B, H, _, S = qt.shape
    return pl.pallas_call(
        functools.partial(_mla_kernel, tq=tq),
        grid=(B, H, S // tq),
        in_specs=[
            pl.BlockSpec((1, 1, MLA_QK_PAD, tq), lambda b, h, i: (b, h, 0, i)),
            pl.BlockSpec((1, 1, S, MLA_QK_PAD), lambda b, h, i: (b, h, 0, 0)),
            pl.BlockSpec((1, 1, S // tq, MLA_V_DIM, tq), lambda b, h, i: (b, h, 0, 0, 0)),
            pl.BlockSpec((1, 1, BLOCK, MLA_QK_PAD), lambda b, h, i: (0, h, 0, 0)),
            pl.BlockSpec((1, 1, 1, MLA_V_DIM, BLOCK), lambda b, h, i: (0, h, 0, 0, 0)),
        ],
        out_specs=pl.BlockSpec((1, tq, MLA_V_DIM), lambda b, h, i: (b, i, h)),
        out_shape=jax.ShapeDtypeStruct((B, S, MLA_OUT_COLS), BF16),
        scratch_shapes=[pltpu.VMEM((tq, tq), F32), pltpu.VMEM((tq, tq), F32),
                        pltpu.VMEM((1, tq), F32), pltpu.VMEM((1, tq), F32),
                        pltpu.VMEM((1, tq), F32), pltpu.VMEM((1, tq), F32),
                        pltpu.VMEM((MLA_V_DIM, tq), F32)],
        compiler_params=_cparams(("parallel", "parallel", "arbitrary")),
        name="mla_attn",
    )(qt, km, vt, km_meta, vt_meta)


def _out_proj_kernel(x_ref, a_ref, b_ref, gin_ref, bin_ref, ga_ref, gb_ref, wo_ref, g1_ref, b1_ref,
                     wrt_ref, brt_ref, h1_ref, route_ref, counts_out_ref, count_ref):
    h = _layer_norm(x_ref[0], gin_ref[...], bin_ref[...])
    a = a_ref[0].astype(F32)
    b = b_ref[0].astype(F32)
    an = (a * _rms_scale(a) * ga_ref[...]).astype(BF16)
    bn = (b * _rms_scale(b) * gb_ref[...]).astype(BF16)
    mix = _dot(jnp.concatenate([an, bn], axis=-1), wo_ref[...])
    h1 = _layer_norm(ALPHA * h + mix, g1_ref[...], b1_ref[...])
    h1_ref[0] = h1

    x_hi = h1.astype(BF16)
    x_lo = (h1 - x_hi.astype(F32)).astype(BF16)
    hi_both = _dot(x_hi, wrt_ref[...])
    logits = (hi_both[:, :ROUTE_LANES] + hi_both[:, ROUTE_LANES:]
              + _dot(x_lo, wrt_ref[:, :ROUTE_LANES]) + brt_ref[...])
    lane = lax.broadcasted_iota(jnp.int32, logits.shape, 1).astype(F32)
    ninf = -jnp.inf
    no_lane = float(ROUTE_LANES)
    gl = jnp.where(lane < N_GROUPS, logits, ninf)
    gmax = jnp.max(gl, axis=-1, keepdims=True)
    g_top = 1.0 / jnp.sum(jnp.exp(gl - gmax), axis=-1, keepdims=True)
    g_idx = jnp.min(jnp.where(gl == gmax, lane, no_lane), axis=-1, keepdims=True)
    lo = EXPERT_LANE0 + g_idx * EXPERTS_PER_GROUP
    el = jnp.where((lane >= lo) & (lane < lo + EXPERTS_PER_GROUP), logits, ninf)
    m1 = jnp.max(el, axis=-1, keepdims=True)
    i1 = jnp.min(jnp.where(el == m1, lane, no_lane), axis=-1, keepdims=True)
    el2 = jnp.where(lane == i1, ninf, el)
    m2 = jnp.max(el2, axis=-1, keepdims=True)
    i2 = jnp.min(jnp.where(el2 == m2, lane, no_lane), axis=-1, keepdims=True)
    e2 = jnp.exp(m2 - m1)
    w1 = g_top / (1.0 + e2)
    w2 = g_top * e2 / (1.0 + e2)

    first_is_lo = i1 < i2
    a = jnp.minimum(i1, i2) - lo
    b = jnp.maximum(i1, i2) - lo
    cls = g_idx * PAIRS_PER_GROUP + a * (15.0 - a) * 0.5 + (b - a - 1.0)
    w_lo = jnp.where(first_is_lo, w1, w2)
    w_hi = jnp.where(first_is_lo, w2, w1)

    @pl.when((pl.program_id(0) == 0) & (pl.program_id(1) == 0))
    def _():
        count_ref[...] = jnp.zeros_like(count_ref)

    tm = logits.shape[0]
    onehot = jnp.where(lane == cls, 1.0, 0.0)
    r_i = lax.broadcasted_iota(jnp.int32, (tm, tm), 0)
    c_i = lax.broadcasted_iota(jnp.int32, (tm, tm), 1)
    earlier = jnp.where(r_i > c_i, 1.0, 0.0).astype(BF16)
    before = _dot(earlier, onehot.astype(BF16)) + count_ref[...]
    rank = jnp.sum(jnp.where(lane == cls, before, 0.0), axis=-1, keepdims=True)
    count_ref[...] += jnp.sum(onehot, axis=0, keepdims=True)
    counts_out_ref[...] = count_ref[...]

    route = jnp.where(lane == ROUTE_CLS, cls, 0.0)
    route = jnp.where(lane == ROUTE_W_LO, w_lo, route)
    route = jnp.where(lane == ROUTE_W_HI, w_hi, route)
    route = jnp.where(lane == ROUTE_RANK, rank, route)
    route_ref[0] = route


def _out_proj(x, a_out, b_out, p, tm):
    B, S, D = x.shape
    full = lambda shape: pl.BlockSpec(shape, lambda b, i: (0,) * len(shape))
    row = lambda w: pl.BlockSpec((1, tm, w), lambda b, i: (b, i, 0))
    return pl.pallas_call(
        _out_proj_kernel,
        grid=(B, S // tm),
        in_specs=[row(D), row(SWA_Q_COLS), row(MLA_OUT_COLS),
                  full((1, D)), full((1, D)), full((1, SWA_Q_COLS)), full((1, MLA_OUT_COLS)),
                  full((D, D)), full((1, D)), full((1, D)),
                  full((D, 2 * ROUTE_LANES)), full((1, ROUTE_LANES))],
        out_specs=(row(D), row(ROUTE_LANES), full((1, ROUTE_LANES))),
        out_shape=(jax.ShapeDtypeStruct((B, S, D), F32),
                   jax.ShapeDtypeStruct((B, S, ROUTE_LANES), F32),
                   jax.ShapeDtypeStruct((1, ROUTE_LANES), F32)),
        scratch_shapes=[pltpu.VMEM((1, ROUTE_LANES), F32)],
        compiler_params=_cparams(("arbitrary", "arbitrary")),
        name="out_proj_route",
    )(x, a_out, b_out, p["ln_in_g"], p["ln_in_b"], p["ga"], p["gb"], p["w_o"], p["ln1_g"],
      p["ln1_b"], p["w_route"], p["b_route"])


SLAB = D_MODEL // LANES


def _slab_copy(src_ref, src_tok8, dst_ref, dst_tok8, sem):
    return pltpu.make_async_copy(src_ref.at[pl.ds(pl.multiple_of(src_tok8, SLAB), SLAB)],
                                 dst_ref.at[pl.ds(pl.multiple_of(dst_tok8, SLAB), SLAB)], sem)


def _to_slabs(x, slab_ref, n):
    for s in range(SLAB):
        slab_ref[pl.ds(s, n, stride=SLAB), :] = x[:, s * LANES:(s + 1) * LANES]


def _from_slabs(slab_ref, n):
    return jnp.concatenate([slab_ref[pl.ds(s, n, stride=SLAB), :] for s in range(SLAB)], axis=1)


def _dispatch_kernel(padstart_ref, nu_ref, pos_ref, h1_ref, xs_ref, rows_ref, zero_ref, sem, *,
                     tm, n_tiles):
    tail = zero_ref.shape[0]

    def zero_fill(start):
        return pltpu.make_async_copy(
            zero_ref, xs_ref.at[pl.ds(pl.multiple_of(start, SLAB), tail)], sem)

    @pl.when(pl.program_id(0) == 0)
    def _():
        zero_ref[...] = jnp.zeros_like(zero_ref)
        for parity in range(2):
            tails = [zero_fill(padstart_ref[c]) for c in range(parity, N_CLASSES, 2)]
            for t in tails:
                t.start()
            for t in tails:
                t.wait()

        def fill_unused(j, carry):
            fill = zero_fill(j * tail)
            fill.start()
            fill.wait()
            return carry

        lax.fori_loop(nu_ref[0], n_tiles + 1, fill_unused, 0)

    _to_slabs(h1_ref[...], rows_ref, tm)

    def issue(r, carry):
        _slab_copy(rows_ref, r * SLAB, xs_ref, pos_ref[0, 0, r], sem).start()
        return carry

    lax.fori_loop(0, tm, issue, 0, unroll=8)
    pltpu.make_async_copy(rows_ref, xs_ref.at[pl.ds(0, tm * SLAB)], sem).wait()


def _dispatch(h1, pos8, pad_start8, n_used, n_tiles, tm):
    T, D = h1.shape
    n_rows = (n_tiles + 1) * MOE_TILE_ROWS
    grid_spec = pltpu.PrefetchScalarGridSpec(
        num_scalar_prefetch=2,
        grid=(T // tm,),
        in_specs=[pl.BlockSpec((1, 1, tm), lambda t, ps, nu: (t, 0, 0),
                               memory_space=pltpu.SMEM),
                  pl.BlockSpec((tm, D), lambda t, ps, nu: (t, 0))],
        out_specs=pl.BlockSpec(memory_space=pl.ANY),
        scratch_shapes=[pltpu.VMEM((tm * SLAB, LANES), F32),
                        pltpu.VMEM((MOE_TILE_ROWS * SLAB, LANES), F32),
                        pltpu.SemaphoreType.DMA],
    )
    return pl.pallas_call(
        functools.partial(_dispatch_kernel, tm=tm, n_tiles=n_tiles),
        grid_spec=grid_spec,
        out_shape=jax.ShapeDtypeStruct((n_rows * SLAB, LANES), F32),
        compiler_params=_cparams(("arbitrary",)),
        name="moe_dispatch",
    )(pad_start8, n_used, pos8, h1)


def _experts_kernel(tlo_ref, thi_ref, nu_ref, x_ref, wgu_lo_ref, wgu_hi_ref, wd_lo_ref, wd_hi_ref,
                    y_ref):
    j = pl.program_id(0)
    tr = MOE_TILE_ROWS

    @pl.when(j < nu_ref[0])
    def _():
        x = _from_slabs(x_ref, tr).astype(BF16)
        for k, (wgu_ref, wd_ref) in enumerate(((wgu_lo_ref, wd_lo_ref), (wgu_hi_ref, wd_hi_ref))):
            gu = _dot(x, wgu_ref[0])
            g = gu[:, :D_EXPERT]
            u = gu[:, D_EXPERT:]
            hid = g * (1.0 / (1.0 + jnp.exp(-g))) * u
            y = _dot(hid.astype(BF16), wd_ref[0])
            for s in range(SLAB):
                y_ref[pl.ds(k * SLAB + s, tr, stride=TOP_K * SLAB), :] = (
                    y[:, s * LANES:(s + 1) * LANES])

    @pl.when(j >= nu_ref[0])
    def _():
        y_ref[...] = jnp.zeros_like(y_ref)


def _experts(xs, tile_lo, tile_hi, n_used, p, n_tiles):
    D = D_MODEL
    tr = MOE_TILE_ROWS
    last_used = lambda j, nu: jnp.minimum(j, nu[0] - 1)
    wgu = lambda pick: pl.BlockSpec((1, D, 2 * D_EXPERT),
                                    lambda j, tlo, thi, nu: (pick(tlo, thi)[j], 0, 0))
    wd = lambda pick: pl.BlockSpec((1, D_EXPERT, D),
                                   lambda j, tlo, thi, nu: (pick(tlo, thi)[j], 0, 0))
    lo = lambda tlo, thi: tlo
    hi = lambda tlo, thi: thi
    grid_spec = pltpu.PrefetchScalarGridSpec(
        num_scalar_prefetch=3,
        grid=(n_tiles,),
        in_specs=[pl.BlockSpec((tr * SLAB, LANES), lambda j, tlo, thi, nu: (last_used(j, nu), 0)),
                  wgu(lo), wgu(hi), wd(lo), wd(hi)],
        out_specs=pl.BlockSpec((tr * TOP_K * SLAB, LANES), lambda j, tlo, thi, nu: (j, 0)),
    )
    return pl.pallas_call(
        _experts_kernel,
        grid_spec=grid_spec,
        out_shape=jax.ShapeDtypeStruct((n_tiles * tr * TOP_K * SLAB, LANES), F32),
        compiler_params=_cparams(("arbitrary",)),
        name="moe_experts",
    )(tile_lo, tile_hi, n_used, xs, p["w_gu"], p["w_gu"], p["w_down"], p["w_down"])


def _combine_kernel(pos_ref, h1_ref, route_ref, ys_ref, g2_ref, b2_ref, o_ref, ybuf_ref, sem, *, tm):
    pair = TOP_K * SLAB

    def issue(r, carry):
        src = pl.multiple_of(pos_ref[0, 0, r] * TOP_K, pair)
        pltpu.make_async_copy(ys_ref.at[pl.ds(src, pair)],
                              ybuf_ref.at[pl.ds(pl.multiple_of(r * pair, pair), pair)], sem).start()
        return carry

    lax.fori_loop(0, tm, issue, 0, unroll=8)
    pltpu.make_async_copy(ys_ref.at[pl.ds(0, tm * pair)], ybuf_ref, sem).wait()

    def expert_out(k):
        return jnp.concatenate([ybuf_ref[pl.ds(k * SLAB + s, tm, stride=pair), :]
                                for s in range(SLAB)], axis=1)

    route = route_ref[...]
    w_lo = route[:, ROUTE_W_LO:ROUTE_W_LO + 1]
    w_hi = route[:, ROUTE_W_HI:ROUTE_W_HI + 1]
    ffn = w_lo * expert_out(0) + w_hi * expert_out(1)
    o_ref[...] = _layer_norm(ALPHA * h1_ref[...] + ffn, g2_ref[...], b2_ref[...])


def _combine(h1, route, pos8, ys, p, tm):
    T, D = h1.shape
    full = lambda shape: pl.BlockSpec(shape, lambda t: (0,) * len(shape))
    return pl.pallas_call(
        functools.partial(_combine_kernel, tm=tm),
        grid=(T // tm,),
        in_specs=[pl.BlockSpec((1, 1, tm), lambda t: (t, 0, 0), memory_space=pltpu.SMEM),
                  pl.BlockSpec((tm, D), lambda t: (t, 0)),
                  pl.BlockSpec((tm, ROUTE_LANES), lambda t: (t, 0)),
                  pl.BlockSpec(memory_space=pl.ANY),
                  full((1, D)), full((1, D))],
        out_specs=pl.BlockSpec((tm, D), lambda t: (t, 0)),
        out_shape=jax.ShapeDtypeStruct((T, D), F32),
        scratch_shapes=[pltpu.VMEM((tm * TOP_K * SLAB, LANES), F32), pltpu.SemaphoreType.DMA],
        compiler_params=_cparams(("arbitrary",)),
        name="moe_combine_ln2",
    )(pos8, h1, route, ys, p["ln2_g"], p["ln2_b"])


def _dispatch_plan(route, counts, tm):
    T = route.shape[0]
    tr = MOE_TILE_ROWS
    n_tiles = T // tr + N_CLASSES
    cnt = counts[0, :N_CLASSES].astype(jnp.int32)
    padded = jnp.maximum((cnt + tr - 1) // tr, 1) * tr
    seg_end = jnp.cumsum(padded)
    seg_start = seg_end - padded
    cls = route[:, ROUTE_CLS].astype(jnp.int32)
    rank = route[:, ROUTE_RANK].astype(jnp.int32)
    classes = jnp.arange(N_CLASSES, dtype=jnp.int32)
    start_of = jnp.sum(jnp.where(cls[:, None] == classes, seg_start, 0), axis=-1)
    pos8 = ((start_of + rank) * SLAB).reshape(T // tm, 1, tm)
    tile_row0 = jnp.arange(n_tiles, dtype=jnp.int32) * tr
    tile_cls = jnp.sum((seg_end[None, :] <= tile_row0[:, None]).astype(jnp.int32), axis=-1)
    tile_cls = jnp.minimum(tile_cls, N_CLASSES - 1)
    pairs = [(a, b) for a in range(EXPERTS_PER_GROUP) for b in range(a + 1, EXPERTS_PER_GROUP)]
    lo_of = jnp.asarray([g * EXPERTS_PER_GROUP + a for g in range(N_GROUPS) for a, _ in pairs],
                        jnp.int32)
    hi_of = jnp.asarray([g * EXPERTS_PER_GROUP + b for g in range(N_GROUPS) for _, b in pairs],
                        jnp.int32)
    is_cls = tile_cls[:, None] == classes
    tile_lo = jnp.sum(jnp.where(is_cls, lo_of, 0), axis=-1)
    tile_hi = jnp.sum(jnp.where(is_cls, hi_of, 0), axis=-1)
    n_used = (seg_end[-1:] // tr).astype(jnp.int32)
    return pos8, (seg_start + cnt) * SLAB, tile_lo, tile_hi, n_used, n_tiles


def _rope_tables(pos):
    d = SWA_HEAD_DIM
    inv_freq = ROPE_THETA ** (-jnp.arange(0, d, 2, dtype=F32) / d)
    ang = pos[:, None] * inv_freq[None, :]
    c, s = jnp.cos(ang), jnp.sin(ang)
    return (jnp.concatenate([c, c, c, c], axis=-1), jnp.concatenate([-s, s, -s, s], axis=-1),
            jnp.concatenate([c, c], axis=-1).T, jnp.concatenate([s, s], axis=-1).T)


def _prep_params(ln_in_g, ln_in_b, w_in, swa_sinks, mla_q_norm_g, mla_w_uq, mla_kv_norm_g, mla_w_ukv,
                 swa_out_norm_g, mla_out_norm_g, w_o, ln1_g, ln1_b, moe_w_group, moe_b_group,
                 moe_w_router, moe_b_router, moe_w_gate, moe_w_up, moe_w_down, ln2_g, ln2_b):
    order = jnp.asarray(SWA_HEAD_ORDER)
    perm = (order[:, None] * SWA_HEAD_DIM + jnp.arange(SWA_HEAD_DIM)[None, :]).reshape(-1)
    w = w_in[0]
    w_q = w[:, :SWA_Q_COLS][:, perm]
    w_pad = jnp.zeros((D_MODEL, IN_COLS_PADDED - w.shape[1]), w.dtype)
    w_in_p = jnp.concatenate([w_q, w[:, SWA_Q_COLS:], w_pad], axis=1).astype(BF16)

    dqk = MLA_NOPE_DIM + MLA_ROPE_DIM
    wuq = mla_w_uq[0].reshape(MLA_Q_RANK, MLA_HEADS, dqk)
    wuq = jnp.pad(wuq, ((0, 0), (0, 0), (0, MLA_QK_PAD - dqk)))
    wuq_t = jnp.transpose(wuq, (1, 2, 0)).astype(BF16)
    wukv = mla_w_ukv[0].reshape(MLA_KV_RANK, MLA_HEADS, MLA_NOPE_DIM + MLA_V_DIM)
    wuk = wukv[:, :, :MLA_NOPE_DIM].reshape(MLA_KV_RANK, MLA_HEADS * MLA_NOPE_DIM).astype(BF16)
    wuv_t = jnp.transpose(wukv[:, :, MLA_NOPE_DIM:], (1, 2, 0)).astype(BF16)

    w_o0 = w_o[0]
    w_o_p = jnp.concatenate([w_o0[:SWA_Q_COLS][perm], w_o0[SWA_Q_COLS:]], axis=0).astype(BF16)

    w_route = jnp.zeros((D_MODEL, ROUTE_LANES), F32)
    w_route = w_route.at[:, :N_GROUPS].set(moe_w_group[0])
    w_route = w_route.at[:, EXPERT_LANE0:EXPERT_LANE0 + N_EXPERTS].set(moe_w_router[0])
    w_route_hi = w_route.astype(BF16)
    w_route_lo = (w_route - w_route_hi.astype(F32)).astype(BF16)
    w_route = jnp.concatenate([w_route_hi, w_route_lo], axis=1)
    b_route = jnp.zeros((1, ROUTE_LANES), F32)
    b_route = b_route.at[0, :N_GROUPS].set(moe_b_group[0])
    b_route = b_route.at[0, EXPERT_LANE0:EXPERT_LANE0 + N_EXPERTS].set(moe_b_router[0])

    return {
        "ln_in_g": ln_in_g[None, :], "ln_in_b": ln_in_b[None, :],
        "w_in": w_in_p,
        "sinks": swa_sinks[0],
        "gq": mla_q_norm_g, "w_uq_t": wuq_t,
        "gkv": mla_kv_norm_g, "w_uk": wuk, "w_uv_t": wuv_t,
        "ga": swa_out_norm_g[0][perm][None, :], "gb": mla_out_norm_g,
        "w_o": w_o_p, "ln1_g": ln1_g, "ln1_b": ln1_b,
        "w_route": w_route, "b_route": b_route,
        "w_gu": jnp.concatenate([moe_w_gate[0], moe_w_up[0]], axis=-1).astype(BF16),
        "w_down": moe_w_down[0].astype(BF16),
        "ln2_g": ln2_g, "ln2_b": ln2_b,
    }


def kernel(x, meta_tokens, ln_in_g, ln_in_b, w_in, swa_sinks, mla_q_norm_g, mla_w_uq, mla_kv_norm_g, mla_w_ukv, swa_out_norm_g, mla_out_norm_g, w_o, ln1_g, ln1_b, moe_w_group, moe_b_group, moe_w_router, moe_b_router, moe_w_gate, moe_w_up, moe_w_down, ln2_g, ln2_b):
    B, S, D = x.shape
    p = _prep_params(ln_in_g, ln_in_b, w_in, swa_sinks, mla_q_norm_g, mla_w_uq, mla_kv_norm_g,
                     mla_w_ukv, swa_out_norm_g, mla_out_norm_g, w_o, ln1_g, ln1_b, moe_w_group,
                     moe_b_group, moe_w_router, moe_b_router, moe_w_gate, moe_w_up, moe_w_down,
                     ln2_g, ln2_b)
    tm = min(512, S)

    x_meta = jnp.concatenate([jnp.zeros((N_PAD, D), x.dtype), meta_tokens.astype(x.dtype)])[None]
    pos_meta = jnp.maximum(jnp.arange(BLOCK) - N_PAD, 0).astype(F32)
    pos_tok = (jnp.arange(S) + N_META).astype(F32)
    _, ka_m, va_m, _, km_m, vt_m = _in_proj(x_meta, _rope_tables(pos_meta), p, BLOCK)
    qa, ka, va, qt, km, vt = _in_proj(x, _rope_tables(pos_tok), p, tm)

    a_out = _swa(p["sinks"], qa, ka, va, ka_m, va_m, tm)
    b_out = _mla(qt, km, vt, km_m, vt_m, tm)
    h1, route, counts = _out_proj(x, a_out, b_out, p, tm)
    h1 = h1.reshape(B * S, D)
    route = route.reshape(B * S, ROUTE_LANES)

    pos8, pad_start8, tile_lo, tile_hi, n_used, n_tiles = _dispatch_plan(route, counts, tm)
    xs = _dispatch(h1, pos8, pad_start8, n_used, n_tiles, tm)
    ys = _experts(xs, tile_lo, tile_hi, n_used, p, n_tiles)
    out = _combine(h1, route, pos8, ys, p, tm)
    return out.reshape(B, S, D)
```

```python
import functools

import jax
import jax.numpy as jnp
from jax import lax
from jax.experimental import pallas as pl
from jax.experimental.pallas import tpu as pltpu

D_MODEL = 1024
N_META = 16
BLOCK = 128
N_PAD = BLOCK - N_META
ROPE_THETA = 10000.0
SWA_HEADS = 8
SWA_KV_HEADS = 2
SWA_HEAD_DIM = 64
MLA_HEADS = 4
MLA_Q_RANK = 256
MLA_KV_RANK = 256
MLA_NOPE_DIM = 128
MLA_ROPE_DIM = 64
MLA_V_DIM = 128
SWA_Q_COLS = SWA_HEADS * SWA_HEAD_DIM
SWA_KV_COLS = SWA_KV_HEADS * SWA_HEAD_DIM
MLA_OUT_COLS = MLA_HEADS * MLA_V_DIM
N_GROUPS = 4
EXPERTS_PER_GROUP = 8
N_EXPERTS = N_GROUPS * EXPERTS_PER_GROUP
D_EXPERT = 256
LN_EPS = 1e-5
RMS_EPS = 1e-6
DEPTH = 1
ALPHA = (2.0 * DEPTH) ** 0.25
NEG = -1e30
LOG2_E = 1.4426950408889634

LANES = 128
IN_COLS_PADDED = 1408
MLA_QK_PAD = 256
ROUTE_LANES = 128
EXPERT_LANE0 = 32
ROUTE_CLS, ROUTE_W_LO, ROUTE_W_HI, ROUTE_RANK = range(4)
TOP_K = 2
PAIRS_PER_GROUP = EXPERTS_PER_GROUP * (EXPERTS_PER_GROUP - 1) // 2
N_CLASSES = N_GROUPS * PAIRS_PER_GROUP
MOE_TILE_ROWS = 128
PROJ_TILE = 512
DISPATCH_TILE = 2048
COMBINE_TILE = 1024
VMEM_LIMIT = 56 * 1024 * 1024

SWA_HEAD_ORDER = (0, 4, 1, 5, 2, 6, 3, 7)

BF16 = jnp.bfloat16
F32 = jnp.float32


def _cparams(sem):
    return pltpu.CompilerParams(dimension_semantics=sem, vmem_limit_bytes=VMEM_LIMIT)


def _layer_norm(x, g, b):
    mu = jnp.mean(x, axis=-1, keepdims=True)
    xc = x - mu
    var = jnp.mean(xc * xc, axis=-1, keepdims=True)
    return xc * lax.rsqrt(var + LN_EPS) * g + b


def _rms_scale(x):
    return lax.rsqrt(jnp.mean(x * x, axis=-1, keepdims=True) + RMS_EPS)


def _rope128(x, cos, sin_signed, lane_lo):
    up = pltpu.roll(x, LANES - 32, 1)
    dn = pltpu.roll(x, 32, 1)
    return x * cos + jnp.where(lane_lo, up, dn) * sin_signed


def _dot(a, b):
    return jnp.dot(a, b, preferred_element_type=F32)


def _dot_nt(a, b):
    return lax.dot_general(a, b, (((1,), (1,)), ((), ())), preferred_element_type=F32)


def _in_proj_kernel(x_ref, g_ref, b_ref, w_ref, cos_ref, sin_ref, cost_ref, sint_ref, gq_ref,
                    wuqt_ref, gkv_ref, wuk_ref, wuvt_ref,
                    qa_ref, ka_ref, va_ref, qt_ref, km_ref, vt_ref):
    h = _layer_norm(x_ref[0], g_ref[...], b_ref[...])
    u = _dot(h.astype(BF16), w_ref[...])
    cos = cos_ref[...]
    sin = sin_ref[...]
    lane = lax.broadcasted_iota(jnp.int32, (1, LANES), 1)
    lane_lo = (lane % 64) < 32
    rope = functools.partial(_rope128, cos=cos, sin_signed=sin, lane_lo=lane_lo)

    swa_scale = SWA_HEAD_DIM ** -0.5
    for c in range(SWA_Q_COLS // LANES):
        qc = rope(u[:, c * LANES:(c + 1) * LANES]) * swa_scale
        qa_ref[0, :, c * LANES:(c + 1) * LANES] = qc.astype(BF16)
    ka_ref[0] = rope(u[:, 512:640]).astype(BF16)
    va_ref[0] = u[:, 640:768].astype(BF16)

    cq = u[:, 768:1024]
    cqn = (cq * _rms_scale(cq) * gq_ref[...]).astype(BF16)
    ckv = u[:, 1024:1280]
    ckvn = (ckv * _rms_scale(ckv) * gkv_ref[...]).astype(BF16)
    k_nope = _dot(ckvn, wuk_ref[...])
    kr = rope(u[:, 1280:1408]).astype(BF16)
    q_scale = (MLA_NOPE_DIM + MLA_ROPE_DIM) ** -0.5 * LOG2_E
    cost = cost_ref[...]
    sint = sint_ref[...]
    half = MLA_ROPE_DIM // 2
    for hd in range(MLA_HEADS):
        qt = _dot_nt(wuqt_ref[hd], cqn)
        qr = qt[MLA_NOPE_DIM:MLA_NOPE_DIM + MLA_ROPE_DIM]
        rot = jnp.concatenate([-qr[half:], qr[:half]], axis=0)
        qt_ref[0, hd, 0:MLA_NOPE_DIM, :] = (qt[:MLA_NOPE_DIM] * q_scale).astype(BF16)
        qt_ref[0, hd, MLA_NOPE_DIM:MLA_NOPE_DIM + MLA_ROPE_DIM, :] = (
            (qr * cost + rot * sint) * q_scale).astype(BF16)
        qt_ref[0, hd, MLA_NOPE_DIM + MLA_ROPE_DIM:, :] = (
            qt[MLA_NOPE_DIM + MLA_ROPE_DIM:]).astype(BF16)
        km_ref[0, hd, :, 0:LANES] = k_nope[:, hd * LANES:(hd + 1) * LANES].astype(BF16)
        km_ref[0, hd, :, LANES:2 * LANES] = kr
        vt_ref[0, hd, 0] = _dot_nt(wuvt_ref[hd], ckvn).astype(BF16)


def _in_proj(x, tables, p, tm):
    cos, sin, cos_t, sin_t = tables
    B, S, D = x.shape
    grid = (B, S // tm)
    full = lambda shape: pl.BlockSpec(shape, lambda b, i: (0,) * len(shape))
    out_shape = (
        jax.ShapeDtypeStruct((B, S, SWA_Q_COLS), BF16),
        jax.ShapeDtypeStruct((B, S, SWA_KV_COLS), BF16),
        jax.ShapeDtypeStruct((B, S, SWA_KV_COLS), BF16),
        jax.ShapeDtypeStruct((B, MLA_HEADS, MLA_QK_PAD, S), BF16),
        jax.ShapeDtypeStruct((B, MLA_HEADS, S, MLA_QK_PAD), BF16),
        jax.ShapeDtypeStruct((B, MLA_HEADS, S // tm, MLA_V_DIM, tm), BF16),
    )
    row = lambda w: pl.BlockSpec((1, tm, w), lambda b, i: (b, i, 0))
    head = lambda w: pl.BlockSpec((1, MLA_HEADS, tm, w), lambda b, i: (b, 0, i, 0))
    return pl.pallas_call(
        _in_proj_kernel,
        grid=grid,
        in_specs=[
            row(D),
            full((1, D)), full((1, D)),
            full((D, IN_COLS_PADDED)),
            pl.BlockSpec((tm, LANES), lambda b, i: (i, 0)),
            pl.BlockSpec((tm, LANES), lambda b, i: (i, 0)),
            pl.BlockSpec((MLA_ROPE_DIM, tm), lambda b, i: (0, i)),
            pl.BlockSpec((MLA_ROPE_DIM, tm), lambda b, i: (0, i)),
            full((1, MLA_Q_RANK)), full((MLA_HEADS, MLA_QK_PAD, MLA_Q_RANK)),
            full((1, MLA_KV_RANK)), full((MLA_KV_RANK, MLA_HEADS * MLA_NOPE_DIM)),
            full((MLA_HEADS, MLA_V_DIM, MLA_KV_RANK)),
        ],
        out_specs=(row(SWA_Q_COLS), row(SWA_KV_COLS), row(SWA_KV_COLS),
                   pl.BlockSpec((1, MLA_HEADS, MLA_QK_PAD, tm), lambda b, i: (b, 0, 0, i)),
                   head(MLA_QK_PAD),
                   pl.BlockSpec((1, MLA_HEADS, 1, MLA_V_DIM, tm), lambda b, i: (b, 0, i, 0, 0))),
        out_shape=out_shape,
        compiler_params=_cparams(("parallel", "parallel")),
        name="in_proj",
    )(x, p["ln_in_g"], p["ln_in_b"], p["w_in"], cos, sin, cos_t, sin_t, p["gq"], p["w_uq_t"],
      p["gkv"], p["w_uk"], p["w_uv_t"])


def _swa_kernel(sink_ref, qa_ref, kc_ref, vc_ref, kp_ref, vp_ref, kmeta_ref, vmeta_ref, o_ref,
                *, blocks_per_step):
    i = pl.program_id(1)
    first = i == 0
    kprev = jnp.where(first, kmeta_ref[0], kp_ref[0])
    vprev = jnp.where(first, vmeta_ref[0], vp_ref[0])

    row = lax.broadcasted_iota(jnp.int32, (2 * BLOCK, 2 * BLOCK), 0) % BLOCK
    col = lax.broadcasted_iota(jnp.int32, (2 * BLOCK, 2 * BLOCK), 1)
    base_mask = ((col < BLOCK) & (col > row)) | ((col >= BLOCK) & ((col - BLOCK) <= row))
    first_mask = base_mask & (col >= jnp.where(first, N_PAD, 0))
    lane = lax.broadcasted_iota(jnp.int32, (1, LANES), 1)
    lane_kv0 = lane < SWA_HEAD_DIM
    top_rows = lax.broadcasted_iota(jnp.int32, (2 * BLOCK, 1), 0) < BLOCK

    for r in range(blocks_per_step):
        rows = slice(r * BLOCK, (r + 1) * BLOCK)
        if r == 0:
            kp, vp = kprev, vprev
            mask = first_mask
        else:
            prev_rows = slice((r - 1) * BLOCK, r * BLOCK)
            kp, vp = kc_ref[0, prev_rows, :], vc_ref[0, prev_rows, :]
            mask = base_mask
        keys = jnp.concatenate([kp, kc_ref[0, rows, :]], axis=0)
        vals = jnp.concatenate([vp, vc_ref[0, rows, :]], axis=0)
        for c in range(SWA_Q_COLS // LANES):
            qc = qa_ref[0, rows, c * LANES:(c + 1) * LANES]
            zero = jnp.zeros_like(qc)
            q2 = jnp.concatenate([jnp.where(lane_kv0, qc, zero),
                                  jnp.where(lane_kv0, zero, qc)], axis=0)
            s = jnp.where(mask, _dot_nt(q2, keys), NEG)
            sink = jnp.where(top_rows, sink_ref[c], sink_ref[c + 4])
            m = jnp.maximum(jnp.max(s, axis=-1, keepdims=True), sink)
            e = jnp.exp(s - m)
            denom = jnp.sum(e, axis=-1, keepdims=True) + jnp.exp(sink - m)
            pr = (e * (1.0 / denom)).astype(BF16)
            o2 = _dot(pr, vals)
            o = jnp.where(lane_kv0, o2[:BLOCK], o2[BLOCK:])
            o_ref[0, rows, c * LANES:(c + 1) * LANES] = o.astype(BF16)


def _swa(sinks, qa, ka, va, ka_meta, va_meta, tb):
    B, S, _ = qa.shape
    R = tb // BLOCK
    kvw = SWA_KV_COLS
    cur = lambda w: pl.BlockSpec((1, tb, w), lambda b, i: (b, i, 0))
    prev = pl.BlockSpec((1, BLOCK, kvw), lambda b, i: (b, jnp.maximum(i * R - 1, 0), 0))
    meta = pl.BlockSpec((1, BLOCK, kvw), lambda b, i: (0, 0, 0))
    return pl.pallas_call(
        functools.partial(_swa_kernel, blocks_per_step=R),
        grid=(B, S // tb),
        in_specs=[pl.BlockSpec(memory_space=pltpu.SMEM),
                  cur(SWA_Q_COLS), cur(kvw), cur(kvw), prev, prev, meta, meta],
        out_specs=cur(SWA_Q_COLS),
        out_shape=jax.ShapeDtypeStruct((B, S, SWA_Q_COLS), BF16),
        compiler_params=_cparams(("parallel", "parallel")),
        name="swa_attn",
    )(sinks, qa, ka, va, ka, va, ka_meta, va_meta)


def _mla_kernel(qt_ref, k_ref, vt_ref, kmeta_ref, vtmeta_ref, o_ref,
                sa_ref, sb_ref, sa_max_ref, sb_max_ref, m_ref, l_ref, acc_ref, *, tq):
    i = pl.program_id(2)
    qt = qt_ref[0, 0]

    s = _dot(kmeta_ref[0, 0], qt)
    key = lax.broadcasted_iota(jnp.int32, (BLOCK, tq), 0)
    s = jnp.where(key >= N_PAD, s, NEG)
    m = jnp.max(s, axis=0, keepdims=True)
    p = jnp.exp2(s - m)
    m_ref[...] = m
    l_ref[...] = jnp.sum(p, axis=0, keepdims=True)
    acc_ref[...] = _dot(vtmeta_ref[0, 0, 0], p.astype(BF16))

    def scores_into(s_ref, smax_ref, j):
        start = pl.multiple_of(j * tq, tq)
        s = _dot(k_ref[0, 0, pl.ds(start, tq), :], qt)
        s_ref[...] = s
        smax_ref[...] = jnp.max(s, axis=0, keepdims=True)

    def absorb(s_ref, smax_ref, j, causal=False):
        s = s_ref[...]
        if causal:
            key = lax.broadcasted_iota(jnp.int32, (tq, tq), 0)
            query = lax.broadcasted_iota(jnp.int32, (tq, tq), 1)
            s = jnp.where(key <= query, s, NEG)
            smax = jnp.max(s, axis=0, keepdims=True)
        else:
            smax = smax_ref[...]
        m = m_ref[...]
        m_new = jnp.maximum(m, smax)
        alpha = jnp.exp2(m - m_new)
        p = jnp.exp2(s - m_new)
        m_ref[...] = m_new
        l_ref[...] = alpha * l_ref[...] + jnp.sum(p, axis=0, keepdims=True)
        acc_ref[...] = alpha * acc_ref[...] + _dot(vt_ref[0, 0, j], p.astype(BF16))

    scores_into(sa_ref, sa_max_ref, 0)

    def pair(t, carry):
        scores_into(sb_ref, sb_max_ref, 2 * t + 1)
        absorb(sa_ref, sa_max_ref, 2 * t)
        scores_into(sa_ref, sa_max_ref, 2 * t + 2)
        absorb(sb_ref, sb_max_ref, 2 * t + 1)
        return carry

    lax.fori_loop(0, i // 2, pair, 0)

    @pl.when(i % 2 == 1)
    def _():
        scores_into(sb_ref, sb_max_ref, i)
        absorb(sa_ref, sa_max_ref, i - 1)
        absorb(sb_ref, sb_max_ref, i, causal=True)

    @pl.when(i % 2 == 0)
    def _():
        absorb(sa_ref, sa_max_ref, i, causal=True)

    o_ref[0] = (acc_ref[...] * (1.0 / l_ref[...])).T.astype(BF16)


def _mla(qt, km, vt, km_meta, vt_meta, tq):
    B, H, _, S = qt.shape
    return pl.pallas_call(
        functools.partial(_mla_kernel, tq=tq),
        grid=(B, H, S // tq),
        in_specs=[
            pl.BlockSpec((1, 1, MLA_QK_PAD, tq), lambda b, h, i: (b, h, 0, i)),
            pl.BlockSpec((1, 1, S, MLA_QK_PAD), lambda b, h, i: (b, h, 0, 0)),
            pl.BlockSpec((1, 1, S // tq, MLA_V_DIM, tq), lambda b, h, i: (b, h, 0, 0, 0)),
            pl.BlockSpec((1, 1, BLOCK, MLA_QK_PAD), lambda b, h, i: (0, h, 0, 0)),
            pl.BlockSpec((1, 1, 1, MLA_V_DIM, BLOCK), lambda b, h, i: (0, h, 0, 0, 0)),
        ],
        out_specs=pl.BlockSpec((1, tq, MLA_V_DIM), lambda b, h, i: (b, i, h)),
        out_shape=jax.ShapeDtypeStruct((B, S, MLA_OUT_COLS), BF16),
        scratch_shapes=[pltpu.VMEM((tq, tq), F32), pltpu.VMEM((tq, tq), F32),
                        pltpu.VMEM((1, tq), F32), pltpu.VMEM((1, tq), F32),
                        pltpu.VMEM((1, tq), F32), pltpu.VMEM((1, tq), F32),
                        pltpu.VMEM((MLA_V_DIM, tq), F32)],
        compiler_params=_cparams(("parallel", "parallel", "arbitrary")),
        name="mla_attn",
    )(qt, km, vt, km_meta, vt_meta)


def _out_proj_kernel(x_ref, a_ref, b_ref, gin_ref, bin_ref, ga_ref, gb_ref, wo_ref, g1_ref, b1_ref,
                     wrt_ref, brt_ref, h1_ref, route_ref, counts_out_ref, count_ref):
    h = _layer_norm(x_ref[0], gin_ref[...], bin_ref[...])
    a = a_ref[0].astype(F32)
    b = b_ref[0].astype(F32)
    an = (a * _rms_scale(a) * ga_ref[...]).astype(BF16)
    bn = (b * _rms_scale(b) * gb_ref[...]).astype(BF16)
    mix = _dot(jnp.concatenate([an, bn], axis=-1), wo_ref[...])
    h1 = _layer_norm(ALPHA * h + mix, g1_ref[...], b1_ref[...])
    h1_ref[0] = h1

    x_hi = h1.astype(BF16)
    x_lo = (h1 - x_hi.astype(F32)).astype(BF16)
    hi_both = _dot(x_hi, wrt_ref[...])
    logits = (hi_both[:, :ROUTE_LANES] + hi_both[:, ROUTE_LANES:]
              + _dot(x_lo, wrt_ref[:, :ROUTE_LANES]) + brt_ref[...])
    lane = lax.broadcasted_iota(jnp.int32, logits.shape, 1).astype(F32)
    ninf = -jnp.inf
    no_lane = float(ROUTE_LANES)
    gl = jnp.where(lane < N_GROUPS, logits, ninf)
    gmax = jnp.max(gl, axis=-1, keepdims=True)
    g_top = 1.0 / jnp.sum(jnp.exp(gl - gmax), axis=-1, keepdims=True)
    g_idx = jnp.min(jnp.where(gl == gmax, lane, no_lane), axis=-1, keepdims=True)
    lo = EXPERT_LANE0 + g_idx * EXPERTS_PER_GROUP
    el = jnp.where((lane >= lo) & (lane < lo + EXPERTS_PER_GROUP), logits, ninf)
    m1 = jnp.max(el, axis=-1, keepdims=True)
    i1 = jnp.min(jnp.where(el == m1, lane, no_lane), axis=-1, keepdims=True)
    el2 = jnp.where(lane == i1, ninf, el)
    m2 = jnp.max(el2, axis=-1, keepdims=True)
    i2 = jnp.min(jnp.where(el2 == m2, lane, no_lane), axis=-1, keepdims=True)
    e2 = jnp.exp(m2 - m1)
    w1 = g_top / (1.0 + e2)
    w2 = g_top * e2 / (1.0 + e2)

    first_is_lo = i1 < i2
    a = jnp.minimum(i1, i2) - lo
    b = jnp.maximum(i1, i2) - lo
    cls = g_idx * PAIRS_PER_GROUP + a * (15.0 - a) * 0.5 + (b - a - 1.0)
    w_lo = jnp.where(first_is_lo, w1, w2)
    w_hi = jnp.where(first_is_lo, w2, w1)

    @pl.when((pl.program_id(0) == 0) & (pl.program_id(1) == 0))
    def _():
        count_ref[...] = jnp.zeros_like(count_ref)

    tm = logits.shape[0]
    onehot = jnp.where(lane == cls, 1.0, 0.0)
    r_i = lax.broadcasted_iota(jnp.int32, (tm, tm), 0)
    c_i = lax.broadcasted_iota(jnp.int32, (tm, tm), 1)
    earlier = jnp.where(r_i > c_i, 1.0, 0.0).astype(BF16)
    before = _dot(earlier, onehot.astype(BF16)) + count_ref[...]
    rank = jnp.sum(jnp.where(lane == cls, before, 0.0), axis=-1, keepdims=True)
    count_ref[...] += jnp.sum(onehot, axis=0, keepdims=True)
    counts_out_ref[...] = count_ref[...]

    route = jnp.where(lane == ROUTE_CLS, cls, 0.0)
    route = jnp.where(lane == ROUTE_W_LO, w_lo, route)
    route = jnp.where(lane == ROUTE_W_HI, w_hi, route)
    route = jnp.where(lane == ROUTE_RANK, rank, route)
    route_ref[0] = route


def _out_proj(x, a_out, b_out, p, tm):
    B, S, D = x.shape
    full = lambda shape: pl.BlockSpec(shape, lambda b, i: (0,) * len(shape))
    row = lambda w: pl.BlockSpec((1, tm, w), lambda b, i: (b, i, 0))
    return pl.pallas_call(
        _out_proj_kernel,
        grid=(B, S // tm),
        in_specs=[row(D), row(SWA_Q_COLS), row(MLA_OUT_COLS),
                  full((1, D)), full((1, D)), full((1, SWA_Q_COLS)), full((1, MLA_OUT_COLS)),
                  full((D, D)), full((1, D)), full((1, D)),
                  full((D, 2 * ROUTE_LANES)), full((1, ROUTE_LANES))],
        out_specs=(row(D), row(ROUTE_LANES), full((1, ROUTE_LANES))),
        out_shape=(jax.ShapeDtypeStruct((B, S, D), F32),
                   jax.ShapeDtypeStruct((B, S, ROUTE_LANES), F32),
                   jax.ShapeDtypeStruct((1, ROUTE_LANES), F32)),
        scratch_shapes=[pltpu.VMEM((1, ROUTE_LANES), F32)],
        compiler_params=_cparams(("arbitrary", "arbitrary")),
        name="out_proj_route",
    )(x, a_out, b_out, p["ln_in_g"], p["ln_in_b"], p["ga"], p["gb"], p["w_o"], p["ln1_g"],
      p["ln1_b"], p["w_route"], p["b_route"])


SLAB = D_MODEL // LANES


def _slab_copy(src_ref, src_tok8, dst_ref, dst_tok8, sem):
    return pltpu.make_async_copy(src_ref.at[pl.ds(pl.multiple_of(src_tok8, SLAB), SLAB)],
                                 dst_ref.at[pl.ds(pl.multiple_of(dst_tok8, SLAB), SLAB)], sem)


def _to_slabs(x, slab_ref, n):
    for s in range(SLAB):
        slab_ref[pl.ds(s, n, stride=SLAB), :] = x[:, s * LANES:(s + 1) * LANES]


def _from_slabs(slab_ref, n):
    return jnp.concatenate([slab_ref[pl.ds(s, n, stride=SLAB), :] for s in range(SLAB)], axis=1)


def _dispatch_kernel(padstart_ref, nu_ref, pos_ref, h1_ref, xs_ref, rows_ref, zero_ref, sem, *,
                     tm, n_tiles):
    tail = zero_ref.shape[0]

    def zero_fill(start):
        return pltpu.make_async_copy(
            zero_ref, xs_ref.at[pl.ds(pl.multiple_of(start, SLAB), tail)], sem)

    @pl.when(pl.program_id(0) == 0)
    def _():
        zero_ref[...] = jnp.zeros_like(zero_ref)
        for parity in range(2):
            tails = [zero_fill(padstart_ref[c]) for c in range(parity, N_CLASSES, 2)]
            for t in tails:
                t.start()
            for t in tails:
                t.wait()

        def fill_unused(j, carry):
            fill = zero_fill(j * tail)
            fill.start()
            fill.wait()
            return carry

        lax.fori_loop(nu_ref[0], n_tiles + 1, fill_unused, 0)

    _to_slabs(h1_ref[...], rows_ref, tm)

    def issue(r, carry):
        _slab_copy(rows_ref, r * SLAB, xs_ref, pos_ref[0, 0, r], sem).start()
        return carry

    lax.fori_loop(0, tm, issue, 0, unroll=8)
    pltpu.make_async_copy(rows_ref, xs_ref.at[pl.ds(0, tm * SLAB)], sem).wait()


def _dispatch(h1, pos8, pad_start8, n_used, n_tiles, tm):
    T, D = h1.shape
    pos8 = pos8.reshape(T // tm, 1, tm)
    n_rows = (n_tiles + 1) * MOE_TILE_ROWS
    grid_spec = pltpu.PrefetchScalarGridSpec(
        num_scalar_prefetch=2,
        grid=(T // tm,),
        in_specs=[pl.BlockSpec((1, 1, tm), lambda t, ps, nu: (t, 0, 0),
                               memory_space=pltpu.SMEM),
                  pl.BlockSpec((tm, D), lambda t, ps, nu: (t, 0))],
        out_specs=pl.BlockSpec(memory_space=pl.ANY),
        scratch_shapes=[pltpu.VMEM((tm * SLAB, LANES), F32),
                        pltpu.VMEM((MOE_TILE_ROWS * SLAB, LANES), F32),
                        pltpu.SemaphoreType.DMA],
    )
    return pl.pallas_call(
        functools.partial(_dispatch_kernel, tm=tm, n_tiles=n_tiles),
        grid_spec=grid_spec,
        out_shape=jax.ShapeDtypeStruct((n_rows * SLAB, LANES), F32),
        compiler_params=_cparams(("arbitrary",)),
        name="moe_dispatch",
    )(pad_start8, n_used, pos8, h1)


def _experts_kernel(tlo_ref, thi_ref, nu_ref, x_ref, wgu_lo_ref, wgu_hi_ref, wd_lo_ref, wd_hi_ref,
                    y_ref):
    j = pl.program_id(0)
    tr = MOE_TILE_ROWS

    @pl.when(j < nu_ref[0])
    def _():
        x = _from_slabs(x_ref, tr).astype(BF16)
        for k, (wgu_ref, wd_ref) in enumerate(((wgu_lo_ref, wd_lo_ref), (wgu_hi_ref, wd_hi_ref))):
            gu = _dot(x, wgu_ref[0])
            g = gu[:, :D_EXPERT]
            u = gu[:, D_EXPERT:]
            hid = g * (1.0 / (1.0 + jnp.exp(-g))) * u
            y = _dot(hid.astype(BF16), wd_ref[0])
            for s in range(SLAB):
                y_ref[pl.ds(k * SLAB + s, tr, stride=TOP_K * SLAB), :] = (
                    y[:, s * LANES:(s + 1) * LANES])

    @pl.when(j >= nu_ref[0])
    def _():
        y_ref[...] = jnp.zeros_like(y_ref)


def _experts(xs, tile_lo, tile_hi, n_used, p, n_tiles):
    D = D_MODEL
    tr = MOE_TILE_ROWS
    last_used = lambda j, nu: jnp.minimum(j, nu[0] - 1)
    wgu = lambda pick: pl.BlockSpec((1, D, 2 * D_EXPERT),
                                    lambda j, tlo, thi, nu: (pick(tlo, thi)[j], 0, 0))
    wd = lambda pick: pl.BlockSpec((1, D_EXPERT, D),
                                   lambda j, tlo, thi, nu: (pick(tlo, thi)[j], 0, 0))
    lo = lambda tlo, thi: tlo
    hi = lambda tlo, thi: thi
    grid_spec = pltpu.PrefetchScalarGridSpec(
        num_scalar_prefetch=3,
        grid=(n_tiles,),
        in_specs=[pl.BlockSpec((tr * SLAB, LANES), lambda j, tlo, thi, nu: (last_used(j, nu), 0)),
                  wgu(lo), wgu(hi), wd(lo), wd(hi)],
        out_specs=pl.BlockSpec((tr * TOP_K * SLAB, LANES), lambda j, tlo, thi, nu: (j, 0)),
    )
    return pl.pallas_call(
        _experts_kernel,
        grid_spec=grid_spec,
        out_shape=jax.ShapeDtypeStruct((n_tiles * tr * TOP_K * SLAB, LANES), F32),
        compiler_params=_cparams(("arbitrary",)),
        name="moe_experts",
    )(tile_lo, tile_hi, n_used, xs, p["w_gu"], p["w_gu"], p["w_down"], p["w_down"])


def _combine_kernel(pos_ref, pos_next_ref, h1_ref, route_ref, ys_ref, g2_ref, b2_ref, o_ref,
                    ybuf_ref, sem, *, tm):
    pair = TOP_K * SLAB
    t = pl.program_id(0)
    slot = t % 2

    def gather(idx_ref, dst_slot):
        def issue(r, carry):
            src = pl.multiple_of(idx_ref[0, 0, r] * TOP_K, pair)
            pltpu.make_async_copy(
                ys_ref.at[pl.ds(src, pair)],
                ybuf_ref.at[dst_slot, pl.ds(pl.multiple_of(r * pair, pair), pair)],
                sem.at[dst_slot]).start()
            return carry

        lax.fori_loop(0, tm, issue, 0, unroll=8)

    @pl.when(t == 0)
    def _():
        gather(pos_ref, 0)

    @pl.when(t + 1 < pl.num_programs(0))
    def _():
        gather(pos_next_ref, 1 - slot)

    pltpu.make_async_copy(ys_ref.at[pl.ds(0, tm * pair)], ybuf_ref.at[slot], sem.at[slot]).wait()

    def expert_out(k):
        return jnp.concatenate([ybuf_ref[slot, pl.ds(k * SLAB + s, tm, stride=pair), :]
                                for s in range(SLAB)], axis=1)

    route = route_ref[...]
    w_lo = route[:, ROUTE_W_LO:ROUTE_W_LO + 1]
    w_hi = route[:, ROUTE_W_HI:ROUTE_W_HI + 1]
    ffn = w_lo * expert_out(0) + w_hi * expert_out(1)
    o_ref[...] = _layer_norm(ALPHA * h1_ref[...] + ffn, g2_ref[...], b2_ref[...])


def _combine(h1, route, pos8, ys, p, tm):
    T, D = h1.shape
    n_steps = T // tm
    pos8 = pos8.reshape(n_steps, 1, tm)
    full = lambda shape: pl.BlockSpec(shape, lambda t: (0,) * len(shape))
    return pl.pallas_call(
        functools.partial(_combine_kernel, tm=tm),
        grid=(n_steps,),
        in_specs=[pl.BlockSpec((1, 1, tm), lambda t: (t, 0, 0), memory_space=pltpu.SMEM),
                  pl.BlockSpec((1, 1, tm), lambda t: (jnp.minimum(t + 1, n_steps - 1), 0, 0),
                               memory_space=pltpu.SMEM),
                  pl.BlockSpec((tm, D), lambda t: (t, 0)),
                  pl.BlockSpec((tm, ROUTE_LANES), lambda t: (t, 0)),
                  pl.BlockSpec(memory_space=pl.ANY),
                  full((1, D)), full((1, D))],
        out_specs=pl.BlockSpec((tm, D), lambda t: (t, 0)),
        out_shape=jax.ShapeDtypeStruct((T, D), F32),
        scratch_shapes=[pltpu.VMEM((2, tm * TOP_K * SLAB, LANES), F32),
                        pltpu.SemaphoreType.DMA((2,))],
        compiler_params=_cparams(("arbitrary",)),
        name="moe_combine_ln2",
    )(pos8, pos8, h1, route, ys, p["ln2_g"], p["ln2_b"])


def _dispatch_plan(route, counts):
    T = route.shape[0]
    tr = MOE_TILE_ROWS
    n_tiles = T // tr + N_CLASSES
    cnt = counts[0, :N_CLASSES].astype(jnp.int32)
    padded = jnp.maximum((cnt + tr - 1) // tr, 1) * tr
    seg_end = jnp.cumsum(padded)
    seg_start = seg_end - padded
    cls = route[:, ROUTE_CLS].astype(jnp.int32)
    rank = route[:, ROUTE_RANK].astype(jnp.int32)
    classes = jnp.arange(N_CLASSES, dtype=jnp.int32)
    start_of = jnp.sum(jnp.where(cls[:, None] == classes, seg_start, 0), axis=-1)
    pos8 = (start_of + rank) * SLAB
    tile_row0 = jnp.arange(n_tiles, dtype=jnp.int32) * tr
    tile_cls = jnp.sum((seg_end[None, :] <= tile_row0[:, None]).astype(jnp.int32), axis=-1)
    tile_cls = jnp.minimum(tile_cls, N_CLASSES - 1)
    pairs = [(a, b) for a in range(EXPERTS_PER_GROUP) for b in range(a + 1, EXPERTS_PER_GROUP)]
    lo_of = jnp.asarray([g * EXPERTS_PER_GROUP + a for g in range(N_GROUPS) for a, _ in pairs],
                        jnp.int32)
    hi_of = jnp.asarray([g * EXPERTS_PER_GROUP + b for g in range(N_GROUPS) for _, b in pairs],
                        jnp.int32)
    is_cls = tile_cls[:, None] == classes
    tile_lo = jnp.sum(jnp.where(is_cls, lo_of, 0), axis=-1)
    tile_hi = jnp.sum(jnp.where(is_cls, hi_of, 0), axis=-1)
    n_used = (seg_end[-1:] // tr).astype(jnp.int32)
    return pos8, (seg_start + cnt) * SLAB, tile_lo, tile_hi, n_used, n_tiles


def _rope_tables(pos):
    d = SWA_HEAD_DIM
    inv_freq = ROPE_THETA ** (-jnp.arange(0, d, 2, dtype=F32) / d)
    ang = pos[:, None] * inv_freq[None, :]
    c, s = jnp.cos(ang), jnp.sin(ang)
    return (jnp.concatenate([c, c, c, c], axis=-1), jnp.concatenate([-s, s, -s, s], axis=-1),
            jnp.concatenate([c, c], axis=-1).T, jnp.concatenate([s, s], axis=-1).T)


def _prep_params(ln_in_g, ln_in_b, w_in, swa_sinks, mla_q_norm_g, mla_w_uq, mla_kv_norm_g, mla_w_ukv,
                 swa_out_norm_g, mla_out_norm_g, w_o, ln1_g, ln1_b, moe_w_group, moe_b_group,
                 moe_w_router, moe_b_router, moe_w_gate, moe_w_up, moe_w_down, ln2_g, ln2_b):
    order = jnp.asarray(SWA_HEAD_ORDER)
    perm = (order[:, None] * SWA_HEAD_DIM + jnp.arange(SWA_HEAD_DIM)[None, :]).reshape(-1)
    w = w_in[0]
    w_q = w[:, :SWA_Q_COLS][:, perm]
    w_pad = jnp.zeros((D_MODEL, IN_COLS_PADDED - w.shape[1]), w.dtype)
    w_in_p = jnp.concatenate([w_q, w[:, SWA_Q_COLS:], w_pad], axis=1).astype(BF16)

    dqk = MLA_NOPE_DIM + MLA_ROPE_DIM
    wuq = mla_w_uq[0].reshape(MLA_Q_RANK, MLA_HEADS, dqk)
    wuq = jnp.pad(wuq, ((0, 0), (0, 0), (0, MLA_QK_PAD - dqk)))
    wuq_t = jnp.transpose(wuq, (1, 2, 0)).astype(BF16)
    wukv = mla_w_ukv[0].reshape(MLA_KV_RANK, MLA_HEADS, MLA_NOPE_DIM + MLA_V_DIM)
    wuk = wukv[:, :, :MLA_NOPE_DIM].reshape(MLA_KV_RANK, MLA_HEADS * MLA_NOPE_DIM).astype(BF16)
    wuv_t = jnp.transpose(wukv[:, :, MLA_NOPE_DIM:], (1, 2, 0)).astype(BF16)

    w_o0 = w_o[0]
    w_o_p = jnp.concatenate([w_o0[:SWA_Q_COLS][perm], w_o0[SWA_Q_COLS:]], axis=0).astype(BF16)

    w_route = jnp.zeros((D_MODEL, ROUTE_LANES), F32)
    w_route = w_route.at[:, :N_GROUPS].set(moe_w_group[0])
    w_route = w_route.at[:, EXPERT_LANE0:EXPERT_LANE0 + N_EXPERTS].set(moe_w_router[0])
    w_route_hi = w_route.astype(BF16)
    w_route_lo = (w_route - w_route_hi.astype(F32)).astype(BF16)
    w_route = jnp.concatenate([w_route_hi, w_route_lo], axis=1)
    b_route = jnp.zeros((1, ROUTE_LANES), F32)
    b_route = b_route.at[0, :N_GROUPS].set(moe_b_group[0])
    b_route = b_route.at[0, EXPERT_LANE0:EXPERT_LANE0 + N_EXPERTS].set(moe_b_router[0])

    return {
        "ln_in_g": ln_in_g[None, :], "ln_in_b": ln_in_b[None, :],
        "w_in": w_in_p,
        "sinks": swa_sinks[0],
        "gq": mla_q_norm_g, "w_uq_t": wuq_t,
        "gkv": mla_kv_norm_g, "w_uk": wuk, "w_uv_t": wuv_t,
        "ga": swa_out_norm_g[0][perm][None, :], "gb": mla_out_norm_g,
        "w_o": w_o_p, "ln1_g": ln1_g, "ln1_b": ln1_b,
        "w_route": w_route, "b_route": b_route,
        "w_gu": jnp.concatenate([moe_w_gate[0], moe_w_up[0]], axis=-1).astype(BF16),
        "w_down": moe_w_down[0].astype(BF16),
        "ln2_g": ln2_g, "ln2_b": ln2_b,
    }


def kernel(x, meta_tokens, ln_in_g, ln_in_b, w_in, swa_sinks, mla_q_norm_g, mla_w_uq, mla_kv_norm_g, mla_w_ukv, swa_out_norm_g, mla_out_norm_g, w_o, ln1_g, ln1_b, moe_w_group, moe_b_group, moe_w_router, moe_b_router, moe_w_gate, moe_w_up, moe_w_down, ln2_g, ln2_b):
    B, S, D = x.shape
    p = _prep_params(ln_in_g, ln_in_b, w_in, swa_sinks, mla_q_norm_g, mla_w_uq, mla_kv_norm_g,
                     mla_w_ukv, swa_out_norm_g, mla_out_norm_g, w_o, ln1_g, ln1_b, moe_w_group,
                     moe_b_group, moe_w_router, moe_b_router, moe_w_gate, moe_w_up, moe_w_down,
                     ln2_g, ln2_b)
    T = B * S
    tm = min(PROJ_TILE, S)

    x_meta = jnp.concatenate([jnp.zeros((N_PAD, D), x.dtype), meta_tokens.astype(x.dtype)])[None]
    pos_meta = jnp.maximum(jnp.arange(BLOCK) - N_PAD, 0).astype(F32)
    pos_tok = (jnp.arange(S) + N_META).astype(F32)
    _, ka_m, va_m, _, km_m, vt_m = _in_proj(x_meta, _rope_tables(pos_meta), p, BLOCK)
    qa, ka, va, qt, km, vt = _in_proj(x, _rope_tables(pos_tok), p, tm)

    a_out = _swa(p["sinks"], qa, ka, va, ka_m, va_m, tm)
    b_out = _mla(qt, km, vt, km_m, vt_m, tm)
    h1, route, counts = _out_proj(x, a_out, b_out, p, tm)
    h1 = h1.reshape(B * S, D)
    route = route.reshape(B * S, ROUTE_LANES)

    pos8, pad_start8, tile_lo, tile_hi, n_used, n_tiles = _dispatch_plan(route, counts)
    xs = _dispatch(h1, pos8, pad_start8, n_used, n_tiles, min(DISPATCH_TILE, T))
    ys = _experts(xs, tile_lo, tile_hi, n_used, p, n_tiles)
    out = _combine(h1, route, pos8, ys, p, min(COMBINE_TILE, T))
    return out.reshape(B, S, D)
```

```python
import functools

import jax
import jax.numpy as jnp
from jax import lax
from jax.experimental import pallas as pl
from jax.experimental.pallas import tpu as pltpu

D_MODEL = 1024
N_META = 16
BLOCK = 128
N_PAD = BLOCK - N_META
ROPE_THETA = 10000.0
SWA_HEADS = 8
SWA_KV_HEADS = 2
SWA_HEAD_DIM = 64
MLA_HEADS = 4
MLA_Q_RANK = 256
MLA_KV_RANK = 256
MLA_NOPE_DIM = 128
MLA_ROPE_DIM = 64
MLA_V_DIM = 128
SWA_Q_COLS = SWA_HEADS * SWA_HEAD_DIM
SWA_KV_COLS = SWA_KV_HEADS * SWA_HEAD_DIM
MLA_OUT_COLS = MLA_HEADS * MLA_V_DIM
N_GROUPS = 4
EXPERTS_PER_GROUP = 8
N_EXPERTS = N_GROUPS * EXPERTS_PER_GROUP
D_EXPERT = 256
LN_EPS = 1e-5
RMS_EPS = 1e-6
DEPTH = 1
ALPHA = (2.0 * DEPTH) ** 0.25
NEG = -1e30
LOG2_E = 1.4426950408889634

LANES = 128
IN_COLS_PADDED = 1408
MLA_QK_PAD = 256
ROUTE_LANES = 128
EXPERT_LANE0 = 32
ROUTE_CLS, ROUTE_W_LO, ROUTE_W_HI, ROUTE_RANK = range(4)
TOP_K = 2
PAIRS_PER_GROUP = EXPERTS_PER_GROUP * (EXPERTS_PER_GROUP - 1) // 2
N_CLASSES = N_GROUPS * PAIRS_PER_GROUP
MOE_TILE_ROWS = 128
PROJ_TILE = 1024
WIDE_TILE = 1024
DISPATCH_TILE = 2048
COMBINE_TILE = 1024
DMA_BURST = 8
VMEM_LIMIT = 56 * 1024 * 1024

SWA_HEAD_ORDER = (0, 4, 1, 5, 2, 6, 3, 7)

BF16 = jnp.bfloat16
F32 = jnp.float32


def _cparams(sem):
    return pltpu.CompilerParams(dimension_semantics=sem, vmem_limit_bytes=VMEM_LIMIT)


def _layer_norm(x, g, b):
    mu = jnp.mean(x, axis=-1, keepdims=True)
    xc = x - mu
    var = jnp.mean(xc * xc, axis=-1, keepdims=True)
    return xc * lax.rsqrt(var + LN_EPS) * g + b


def _rms_scale(x):
    return lax.rsqrt(jnp.mean(x * x, axis=-1, keepdims=True) + RMS_EPS)


def _rope128(x, cos, sin_signed, lane_lo):
    up = pltpu.roll(x, LANES - 32, 1)
    dn = pltpu.roll(x, 32, 1)
    return x * cos + jnp.where(lane_lo, up, dn) * sin_signed


def _dot(a, b):
    return jnp.dot(a, b, preferred_element_type=F32)


def _dot_nt(a, b):
    return lax.dot_general(a, b, (((1,), (1,)), ((), ())), preferred_element_type=F32)


def _in_proj_kernel(x_ref, g_ref, b_ref, w_ref, cos_ref, sin_ref, cost_ref, sint_ref, gq_ref,
                    wuqt_ref, gkv_ref, wuk_ref, wuvt_ref,
                    qa_ref, ka_ref, va_ref, qt_ref, km_ref, vt_ref):
    h = _layer_norm(x_ref[0], g_ref[...], b_ref[...])
    u = _dot(h.astype(BF16), w_ref[...])
    cos = cos_ref[...]
    sin = sin_ref[...]
    lane = lax.broadcasted_iota(jnp.int32, (1, LANES), 1)
    lane_lo = (lane % 64) < 32
    rope = functools.partial(_rope128, cos=cos, sin_signed=sin, lane_lo=lane_lo)

    swa_scale = SWA_HEAD_DIM ** -0.5
    for c in range(SWA_Q_COLS // LANES):
        qc = rope(u[:, c * LANES:(c + 1) * LANES]) * swa_scale
        qa_ref[0, :, c * LANES:(c + 1) * LANES] = qc.astype(BF16)
    ka_ref[0] = rope(u[:, 512:640]).astype(BF16)
    va_ref[0] = u[:, 640:768].astype(BF16)

    cq = u[:, 768:1024]
    cqn = (cq * _rms_scale(cq) * gq_ref[...]).astype(BF16)
    ckv = u[:, 1024:1280]
    ckvn = (ckv * _rms_scale(ckv) * gkv_ref[...]).astype(BF16)
    k_nope = _dot(ckvn, wuk_ref[...])
    kr = rope(u[:, 1280:1408]).astype(BF16)
    q_scale = (MLA_NOPE_DIM + MLA_ROPE_DIM) ** -0.5 * LOG2_E
    cost = cost_ref[...]
    sint = sint_ref[...]
    half = MLA_ROPE_DIM // 2
    for hd in range(MLA_HEADS):
        qt = _dot_nt(wuqt_ref[hd], cqn)
        qr = qt[MLA_NOPE_DIM:MLA_NOPE_DIM + MLA_ROPE_DIM]
        rot = jnp.concatenate([-qr[half:], qr[:half]], axis=0)
        qt_ref[0, hd, 0:MLA_NOPE_DIM, :] = (qt[:MLA_NOPE_DIM] * q_scale).astype(BF16)
        qt_ref[0, hd, MLA_NOPE_DIM:MLA_NOPE_DIM + MLA_ROPE_DIM, :] = (
            (qr * cost + rot * sint) * q_scale).astype(BF16)
        qt_ref[0, hd, MLA_NOPE_DIM + MLA_ROPE_DIM:, :] = (
            qt[MLA_NOPE_DIM + MLA_ROPE_DIM:]).astype(BF16)
        km_ref[0, hd, :, 0:LANES] = k_nope[:, hd * LANES:(hd + 1) * LANES].astype(BF16)
        km_ref[0, hd, :, LANES:2 * LANES] = kr
        vt_ref[0, hd, 0] = _dot_nt(wuvt_ref[hd], ckvn).astype(BF16)


def _in_proj(x, tables, p, tm):
    cos, sin, cos_t, sin_t = tables
    B, S, D = x.shape
    grid = (B, S // tm)
    full = lambda shape: pl.BlockSpec(shape, lambda b, i: (0,) * len(shape))
    out_shape = (
        jax.ShapeDtypeStruct((B, S, SWA_Q_COLS), BF16),
        jax.ShapeDtypeStruct((B, S, SWA_KV_COLS), BF16),
        jax.ShapeDtypeStruct((B, S, SWA_KV_COLS), BF16),
        jax.ShapeDtypeStruct((B, MLA_HEADS, MLA_QK_PAD, S), BF16),
        jax.ShapeDtypeStruct((B, MLA_HEADS, S, MLA_QK_PAD), BF16),
        jax.ShapeDtypeStruct((B, MLA_HEADS, S // tm, MLA_V_DIM, tm), BF16),
    )
    row = lambda w: pl.BlockSpec((1, tm, w), lambda b, i: (b, i, 0))
    head = lambda w: pl.BlockSpec((1, MLA_HEADS, tm, w), lambda b, i: (b, 0, i, 0))
    return pl.pallas_call(
        _in_proj_kernel,
        grid=grid,
        in_specs=[
            row(D),
            full((1, D)), full((1, D)),
            full((D, IN_COLS_PADDED)),
            pl.BlockSpec((tm, LANES), lambda b, i: (i, 0)),
            pl.BlockSpec((tm, LANES), lambda b, i: (i, 0)),
            pl.BlockSpec((MLA_ROPE_DIM, tm), lambda b, i: (0, i)),
            pl.BlockSpec((MLA_ROPE_DIM, tm), lambda b, i: (0, i)),
            full((1, MLA_Q_RANK)), full((MLA_HEADS, MLA_QK_PAD, MLA_Q_RANK)),
            full((1, MLA_KV_RANK)), full((MLA_KV_RANK, MLA_HEADS * MLA_NOPE_DIM)),
            full((MLA_HEADS, MLA_V_DIM, MLA_KV_RANK)),
        ],
        out_specs=(row(SWA_Q_COLS), row(SWA_KV_COLS), row(SWA_KV_COLS),
                   pl.BlockSpec((1, MLA_HEADS, MLA_QK_PAD, tm), lambda b, i: (b, 0, 0, i)),
                   head(MLA_QK_PAD),
                   pl.BlockSpec((1, MLA_HEADS, 1, MLA_V_DIM, tm), lambda b, i: (b, 0, i, 0, 0))),
        out_shape=out_shape,
        compiler_params=_cparams(("parallel", "parallel")),
        name="in_proj",
    )(x, p["ln_in_g"], p["ln_in_b"], p["w_in"], cos, sin, cos_t, sin_t, p["gq"], p["w_uq_t"],
      p["gkv"], p["w_uk"], p["w_uv_t"])


def _swa_kernel(sink_ref, qa_ref, kc_ref, vc_ref, kp_ref, vp_ref, kmeta_ref, vmeta_ref, o_ref,
                *, blocks_per_step):
    i = pl.program_id(1)
    first = i == 0
    kprev = jnp.where(first, kmeta_ref[0], kp_ref[0])
    vprev = jnp.where(first, vmeta_ref[0], vp_ref[0])

    row = lax.broadcasted_iota(jnp.int32, (2 * BLOCK, 2 * BLOCK), 0) % BLOCK
    col = lax.broadcasted_iota(jnp.int32, (2 * BLOCK, 2 * BLOCK), 1)
    base_mask = ((col < BLOCK) & (col > row)) | ((col >= BLOCK) & ((col - BLOCK) <= row))
    first_mask = base_mask & (col >= jnp.where(first, N_PAD, 0))
    lane = lax.broadcasted_iota(jnp.int32, (1, LANES), 1)
    lane_kv0 = lane < SWA_HEAD_DIM
    top_rows = lax.broadcasted_iota(jnp.int32, (2 * BLOCK, 1), 0) < BLOCK

    for r in range(blocks_per_step):
        rows = slice(r * BLOCK, (r + 1) * BLOCK)
        if r == 0:
            kp, vp = kprev, vprev
            mask = first_mask
        else:
            prev_rows = slice((r - 1) * BLOCK, r * BLOCK)
            kp, vp = kc_ref[0, prev_rows, :], vc_ref[0, prev_rows, :]
            mask = base_mask
        keys = jnp.concatenate([kp, kc_ref[0, rows, :]], axis=0)
        vals = jnp.concatenate([vp, vc_ref[0, rows, :]], axis=0)
        for c in range(SWA_Q_COLS // LANES):
            qc = qa_ref[0, rows, c * LANES:(c + 1) * LANES]
            zero = jnp.zeros_like(qc)
            q2 = jnp.concatenate([jnp.where(lane_kv0, qc, zero),
                                  jnp.where(lane_kv0, zero, qc)], axis=0)
            s = jnp.where(mask, _dot_nt(q2, keys), NEG)
            sink = jnp.where(top_rows, sink_ref[c], sink_ref[c + 4])
            m = jnp.maximum(jnp.max(s, axis=-1, keepdims=True), sink)
            e = jnp.exp(s - m)
            denom = jnp.sum(e, axis=-1, keepdims=True) + jnp.exp(sink - m)
            pr = (e * (1.0 / denom)).astype(BF16)
            o2 = _dot(pr, vals)
            o = jnp.where(lane_kv0, o2[:BLOCK], o2[BLOCK:])
            o_ref[0, rows, c * LANES:(c + 1) * LANES] = o.astype(BF16)


def _swa(sinks, qa, ka, va, ka_meta, va_meta, tb):
    B, S, _ = qa.shape
    R = tb // BLOCK
    kvw = SWA_KV_COLS
    cur = lambda w: pl.BlockSpec((1, tb, w), lambda b, i: (b, i, 0))
    prev = pl.BlockSpec((1, BLOCK, kvw), lambda b, i: (b, jnp.maximum(i * R - 1, 0), 0))
    meta = pl.BlockSpec((1, BLOCK, kvw), lambda b, i: (0, 0, 0))
    return pl.pallas_call(
        functools.partial(_swa_kernel, blocks_per_step=R),
        grid=(B, S // tb),
        in_specs=[pl.BlockSpec(memory_space=pltpu.SMEM),
                  cur(SWA_Q_COLS), cur(kvw), cur(kvw), prev, prev, meta, meta],
        out_specs=cur(SWA_Q_COLS),
        out_shape=jax.ShapeDtypeStruct((B, S, SWA_Q_COLS), BF16),
        compiler_params=_cparams(("parallel", "parallel")),
        name="swa_attn",
    )(sinks, qa, ka, va, ka, va, ka_meta, va_meta)


def _mla_kernel(qt_ref, k_ref, vt_ref, kmeta_ref, vtmeta_ref, o_ref,
                sa_ref, sb_ref, sa_max_ref, sb_max_ref, m_ref, l_ref, acc_ref, *, tq):
    i = pl.program_id(2)
    qt = qt_ref[0, 0]

    s = _dot(kmeta_ref[0, 0], qt)
    m = jnp.max(s, axis=0, keepdims=True)
    p = jnp.exp2(s - m)
    m_ref[...] = m
    l_ref[...] = jnp.sum(p, axis=0, keepdims=True)
    acc_ref[...] = _dot(vtmeta_ref[0, 0, 0], p.astype(BF16))

    def scores_into(s_ref, smax_ref, j):
        start = pl.multiple_of(j * tq, tq)
        s = _dot(k_ref[0, 0, pl.ds(start, tq), :], qt)
        s_ref[...] = s
        smax_ref[...] = jnp.max(s, axis=0, keepdims=True)

    def absorb(s_ref, smax_ref, j, causal=False):
        s = s_ref[...]
        if causal:
            key = lax.broadcasted_iota(jnp.int32, (tq, tq), 0)
            query = lax.broadcasted_iota(jnp.int32, (tq, tq), 1)
            s = jnp.where(key <= query, s, NEG)
            smax = jnp.max(s, axis=0, keepdims=True)
        else:
            smax = smax_ref[...]
        m = m_ref[...]
        m_new = jnp.maximum(m, smax)
        alpha = jnp.exp2(m - m_new)
        p = jnp.exp2(s - m_new)
        m_ref[...] = m_new
        l_ref[...] = alpha * l_ref[...] + jnp.sum(p, axis=0, keepdims=True)
        acc_ref[...] = alpha * acc_ref[...] + _dot(vt_ref[0, 0, j], p.astype(BF16))

    scores_into(sa_ref, sa_max_ref, 0)

    def pair(t, carry):
        scores_into(sb_ref, sb_max_ref, 2 * t + 1)
        absorb(sa_ref, sa_max_ref, 2 * t)
        scores_into(sa_ref, sa_max_ref, 2 * t + 2)
        absorb(sb_ref, sb_max_ref, 2 * t + 1)
        return carry

    lax.fori_loop(0, i // 2, pair, 0)

    @pl.when(i % 2 == 1)
    def _():
        scores_into(sb_ref, sb_max_ref, i)
        absorb(sa_ref, sa_max_ref, i - 1)
        absorb(sb_ref, sb_max_ref, i, causal=True)

    @pl.when(i % 2 == 0)
    def _():
        absorb(sa_ref, sa_max_ref, i, causal=True)

    o_ref[0] = (acc_ref[...] * (1.0 / l_ref[...])).T.astype(BF16)


def _mla(qt, km, vt, km_meta, vt_meta, tq):
    B, H, _, S = qt.shape
    return pl.pallas_call(
        functools.partial(_mla_kernel, tq=tq),
        grid=(B, H, S // tq),
        in_specs=[
            pl.BlockSpec((1, 1, MLA_QK_PAD, tq), lambda b, h, i: (b, h, 0, i)),
            pl.BlockSpec((1, 1, S, MLA_QK_PAD), lambda b, h, i: (b, h, 0, 0)),
            pl.BlockSpec((1, 1, S // tq, MLA_V_DIM, tq), lambda b, h, i: (b, h, 0, 0, 0)),
            pl.BlockSpec((1, 1, N_META, MLA_QK_PAD), lambda b, h, i: (0, h, 0, 0)),
            pl.BlockSpec((1, 1, 1, MLA_V_DIM, N_META), lambda b, h, i: (0, h, 0, 0, 0)),
        ],
        out_specs=pl.BlockSpec((1, tq, MLA_V_DIM), lambda b, h, i: (b, i, h)),
        out_shape=jax.ShapeDtypeStruct((B, S, MLA_OUT_COLS), BF16),
        scratch_shapes=[pltpu.VMEM((tq, tq), F32), pltpu.VMEM((tq, tq), F32),
                        pltpu.VMEM((1, tq), F32), pltpu.VMEM((1, tq), F32),
                        pltpu.VMEM((1, tq), F32), pltpu.VMEM((1, tq), F32),
                        pltpu.VMEM((MLA_V_DIM, tq), F32)],
        compiler_params=_cparams(("parallel", "parallel", "arbitrary")),
        name="mla_attn",
    )(qt, km, vt, km_meta, vt_meta)


def _out_proj_kernel(x_ref, a_ref, b_ref, gin_ref, bin_ref, ga_ref, gb_ref, wo_ref, g1_ref, b1_ref,
                     wrt_ref, brt_ref, h1_ref, route_ref, counts_out_ref, count_ref):
    h = _layer_norm(x_ref[0], gin_ref[...], bin_ref[...])
    a = a_ref[0].astype(F32)
    b = b_ref[0].astype(F32)
    an = (a * _rms_scale(a) * ga_ref[...]).astype(BF16)
    bn = (b * _rms_scale(b) * gb_ref[...]).astype(BF16)
    mix = _dot(jnp.concatenate([an, bn], axis=-1), wo_ref[...])
    h1 = _layer_norm(ALPHA * h + mix, g1_ref[...], b1_ref[...])
    h1_ref[0] = h1

    x_hi = h1.astype(BF16)
    x_lo = (h1 - x_hi.astype(F32)).astype(BF16)
    hi_both = _dot(x_hi, wrt_ref[...])
    logits = (hi_both[:, :ROUTE_LANES] + hi_both[:, ROUTE_LANES:]
              + _dot(x_lo, wrt_ref[:, :ROUTE_LANES]) + brt_ref[...])
    lane = lax.broadcasted_iota(jnp.int32, logits.shape, 1).astype(F32)
    ninf = -jnp.inf
    no_lane = float(ROUTE_LANES)
    gl = jnp.where(lane < N_GROUPS, logits, ninf)
    gmax = jnp.max(gl, axis=-1, keepdims=True)
    g_top = 1.0 / jnp.sum(jnp.exp(gl - gmax), axis=-1, keepdims=True)
    g_idx = jnp.min(jnp.where(gl == gmax, lane, no_lane), axis=-1, keepdims=True)
    lo = EXPERT_LANE0 + g_idx * EXPERTS_PER_GROUP
    el = jnp.where((lane >= lo) & (lane < lo + EXPERTS_PER_GROUP), logits, ninf)
    m1 = jnp.max(el, axis=-1, keepdims=True)
    i1 = jnp.min(jnp.where(el == m1, lane, no_lane), axis=-1, keepdims=True)
    el2 = jnp.where(lane == i1, ninf, el)
    m2 = jnp.max(el2, axis=-1, keepdims=True)
    i2 = jnp.min(jnp.where(el2 == m2, lane, no_lane), axis=-1, keepdims=True)
    e2 = jnp.exp(m2 - m1)
    w1 = g_top / (1.0 + e2)
    w2 = g_top * e2 / (1.0 + e2)

    first_is_lo = i1 < i2
    a = jnp.minimum(i1, i2) - lo
    b = jnp.maximum(i1, i2) - lo
    cls = g_idx * PAIRS_PER_GROUP + a * (15.0 - a) * 0.5 + (b - a - 1.0)
    w_lo = jnp.where(first_is_lo, w1, w2)
    w_hi = jnp.where(first_is_lo, w2, w1)

    @pl.when((pl.program_id(0) == 0) & (pl.program_id(1) == 0))
    def _():
        count_ref[...] = jnp.zeros_like(count_ref)

    tm = logits.shape[0]
    onehot = jnp.where(lane == cls, 1.0, 0.0)
    r_i = lax.broadcasted_iota(jnp.int32, (tm, tm), 0)
    c_i = lax.broadcasted_iota(jnp.int32, (tm, tm), 1)
    earlier = jnp.where(r_i > c_i, 1.0, 0.0).astype(BF16)
    before = _dot(earlier, onehot.astype(BF16)) + count_ref[...]
    rank = jnp.sum(jnp.where(lane == cls, before, 0.0), axis=-1, keepdims=True)
    count_ref[...] += jnp.sum(onehot, axis=0, keepdims=True)
    counts_out_ref[...] = count_ref[...]

    route = jnp.where(lane == ROUTE_CLS, cls, 0.0)
    route = jnp.where(lane == ROUTE_W_LO, w_lo, route)
    route = jnp.where(lane == ROUTE_W_HI, w_hi, route)
    route = jnp.where(lane == ROUTE_RANK, rank, route)
    route_ref[0] = route


def _out_proj(x, a_out, b_out, p, tm):
    B, S, D = x.shape
    full = lambda shape: pl.BlockSpec(shape, lambda b, i: (0,) * len(shape))
    row = lambda w: pl.BlockSpec((1, tm, w), lambda b, i: (b, i, 0))
    return pl.pallas_call(
        _out_proj_kernel,
        grid=(B, S // tm),
        in_specs=[row(D), row(SWA_Q_COLS), row(MLA_OUT_COLS),
                  full((1, D)), full((1, D)), full((1, SWA_Q_COLS)), full((1, MLA_OUT_COLS)),
                  full((D, D)), full((1, D)), full((1, D)),
                  full((D, 2 * ROUTE_LANES)), full((1, ROUTE_LANES))],
        out_specs=(row(D), row(ROUTE_LANES), full((1, ROUTE_LANES))),
        out_shape=(jax.ShapeDtypeStruct((B, S, D), F32),
                   jax.ShapeDtypeStruct((B, S, ROUTE_LANES), F32),
                   jax.ShapeDtypeStruct((1, ROUTE_LANES), F32)),
        scratch_shapes=[pltpu.VMEM((1, ROUTE_LANES), F32)],
        compiler_params=_cparams(("arbitrary", "arbitrary")),
        name="out_proj_route",
    )(x, a_out, b_out, p["ln_in_g"], p["ln_in_b"], p["ga"], p["gb"], p["w_o"], p["ln1_g"],
      p["ln1_b"], p["w_route"], p["b_route"])


SLAB = D_MODEL // LANES


def _slab_copy(src_ref, src_tok8, dst_ref, dst_tok8, sem):
    return pltpu.make_async_copy(src_ref.at[pl.ds(pl.multiple_of(src_tok8, SLAB), SLAB)],
                                 dst_ref.at[pl.ds(pl.multiple_of(dst_tok8, SLAB), SLAB)], sem)


def _to_slabs(x, slab_ref, n):
    for s in range(SLAB):
        slab_ref[pl.ds(s, n, stride=SLAB), :] = x[:, s * LANES:(s + 1) * LANES]


def _from_slabs(slab_ref, n):
    return jnp.concatenate([slab_ref[pl.ds(s, n, stride=SLAB), :] for s in range(SLAB)], axis=1)


def _dispatch_kernel(padstart_ref, nu_ref, pos_ref, h1_ref, xs_ref, rows_ref, zero_ref, sem, *,
                     tm, n_tiles):
    tail = zero_ref.shape[0]

    def zero_fill(start):
        return pltpu.make_async_copy(
            zero_ref, xs_ref.at[pl.ds(pl.multiple_of(start, SLAB), tail)], sem)

    @pl.when(pl.program_id(0) == 0)
    def _():
        zero_ref[...] = jnp.zeros_like(zero_ref)
        for parity in range(2):
            tails = [zero_fill(padstart_ref[c]) for c in range(parity, N_CLASSES, 2)]
            for t in tails:
                t.start()
            for t in tails:
                t.wait()

        def fill_unused(j, carry):
            fill = zero_fill(j * tail)
            fill.start()
            fill.wait()
            return carry

        lax.fori_loop(nu_ref[0], n_tiles + 1, fill_unused, 0)

    _to_slabs(h1_ref[...], rows_ref, tm)

    def issue(r0, carry):
        for k in range(DMA_BURST):
            r = r0 * DMA_BURST + k
            _slab_copy(rows_ref, r * SLAB, xs_ref, pos_ref[0, 0, r], sem).start(priority=k % 2)
        return carry

    lax.fori_loop(0, tm // DMA_BURST, issue, 0)
    pltpu.make_async_copy(rows_ref, xs_ref.at[pl.ds(0, tm * SLAB)], sem).wait()


def _dispatch(h1, pos8, pad_start8, n_used, n_tiles, tm):
    T, D = h1.shape
    pos8 = pos8.reshape(T // tm, 1, tm)
    n_rows = (n_tiles + 1) * MOE_TILE_ROWS
    grid_spec = pltpu.PrefetchScalarGridSpec(
        num_scalar_prefetch=2,
        grid=(T // tm,),
        in_specs=[pl.BlockSpec((1, 1, tm), lambda t, ps, nu: (t, 0, 0),
                               memory_space=pltpu.SMEM),
                  pl.BlockSpec((tm, D), lambda t, ps, nu: (t, 0))],
        out_specs=pl.BlockSpec(memory_space=pl.ANY),
        scratch_shapes=[pltpu.VMEM((tm * SLAB, LANES), F32),
                        pltpu.VMEM((MOE_TILE_ROWS * SLAB, LANES), F32),
                        pltpu.SemaphoreType.DMA],
    )
    return pl.pallas_call(
        functools.partial(_dispatch_kernel, tm=tm, n_tiles=n_tiles),
        grid_spec=grid_spec,
        out_shape=jax.ShapeDtypeStruct((n_rows * SLAB, LANES), F32),
        compiler_params=_cparams(("arbitrary",)),
        name="moe_dispatch",
    )(pad_start8, n_used, pos8, h1)


def _experts_kernel(tlo_ref, thi_ref, nu_ref, x_ref, wgu_lo_ref, wgu_hi_ref, wd_lo_ref, wd_hi_ref,
                    y_ref):
    j = pl.program_id(0)
    tr = MOE_TILE_ROWS

    @pl.when(j < nu_ref[0])
    def _():
        x = _from_slabs(x_ref, tr).astype(BF16)
        for k, (wgu_ref, wd_ref) in enumerate(((wgu_lo_ref, wd_lo_ref), (wgu_hi_ref, wd_hi_ref))):
            gu = _dot(x, wgu_ref[0])
            g = gu[:, :D_EXPERT]
            u = gu[:, D_EXPERT:]
            hid = g * (1.0 / (1.0 + jnp.exp(-g))) * u
            y = _dot(hid.astype(BF16), wd_ref[0])
            for s in range(SLAB):
                y_ref[pl.ds(k * SLAB + s, tr, stride=TOP_K * SLAB), :] = (
                    y[:, s * LANES:(s + 1) * LANES])

    @pl.when(j >= nu_ref[0])
    def _():
        y_ref[...] = jnp.zeros_like(y_ref)


def _experts(xs, tile_lo, tile_hi, n_used, p, n_tiles):
    D = D_MODEL
    tr = MOE_TILE_ROWS
    last_used = lambda j, nu: jnp.minimum(j, nu[0] - 1)
    wgu = lambda pick: pl.BlockSpec((1, D, 2 * D_EXPERT),
                                    lambda j, tlo, thi, nu: (pick(tlo, thi)[j], 0, 0))
    wd = lambda pick: pl.BlockSpec((1, D_EXPERT, D),
                                   lambda j, tlo, thi, nu: (pick(tlo, thi)[j], 0, 0))
    lo = lambda tlo, thi: tlo
    hi = lambda tlo, thi: thi
    grid_spec = pltpu.PrefetchScalarGridSpec(
        num_scalar_prefetch=3,
        grid=(n_tiles,),
        in_specs=[pl.BlockSpec((tr * SLAB, LANES), lambda j, tlo, thi, nu: (last_used(j, nu), 0)),
                  wgu(lo), wgu(hi), wd(lo), wd(hi)],
        out_specs=pl.BlockSpec((tr * TOP_K * SLAB, LANES), lambda j, tlo, thi, nu: (j, 0)),
    )
    return pl.pallas_call(
        _experts_kernel,
        grid_spec=grid_spec,
        out_shape=jax.ShapeDtypeStruct((n_tiles * tr * TOP_K * SLAB, LANES), F32),
        compiler_params=_cparams(("arbitrary",)),
        name="moe_experts",
    )(tile_lo, tile_hi, n_used, xs, p["w_gu"], p["w_gu"], p["w_down"], p["w_down"])


def _combine_kernel(pos_ref, pos_next_ref, h1_ref, route_ref, ys_ref, g2_ref, b2_ref, o_ref,
                    ybuf_ref, sem, *, tm):
    pair = TOP_K * SLAB
    t = pl.program_id(0)
    slot = t % 2

    def gather(idx_ref, dst_slot):
        def issue(r0, carry):
            for k in range(DMA_BURST):
                r = r0 * DMA_BURST + k
                src = pl.multiple_of(idx_ref[0, 0, r] * TOP_K, pair)
                pltpu.make_async_copy(
                    ys_ref.at[pl.ds(src, pair)],
                    ybuf_ref.at[dst_slot, pl.ds(pl.multiple_of(r * pair, pair), pair)],
                    sem.at[dst_slot]).start(priority=k % 2)
            return carry

        lax.fori_loop(0, tm // DMA_BURST, issue, 0)

    @pl.when(t == 0)
    def _():
        gather(pos_ref, 0)

    @pl.when(t + 1 < pl.num_programs(0))
    def _():
        gather(pos_next_ref, 1 - slot)

    pltpu.make_async_copy(ys_ref.at[pl.ds(0, tm * pair)], ybuf_ref.at[slot], sem.at[slot]).wait()

    def expert_out(k):
        return jnp.concatenate([ybuf_ref[slot, pl.ds(k * SLAB + s, tm, stride=pair), :]
                                for s in range(SLAB)], axis=1)

    route = route_ref[...]
    w_lo = route[:, ROUTE_W_LO:ROUTE_W_LO + 1]
    w_hi = route[:, ROUTE_W_HI:ROUTE_W_HI + 1]
    ffn = w_lo * expert_out(0) + w_hi * expert_out(1)
    o_ref[...] = _layer_norm(ALPHA * h1_ref[...] + ffn, g2_ref[...], b2_ref[...])


def _combine(h1, route, pos8, ys, p, tm):
    T, D = h1.shape
    n_steps = T // tm
    pos8 = pos8.reshape(n_steps, 1, tm)
    full = lambda shape: pl.BlockSpec(shape, lambda t: (0,) * len(shape))
    return pl.pallas_call(
        functools.partial(_combine_kernel, tm=tm),
        grid=(n_steps,),
        in_specs=[pl.BlockSpec((1, 1, tm), lambda t: (t, 0, 0), memory_space=pltpu.SMEM),
                  pl.BlockSpec((1, 1, tm), lambda t: (jnp.minimum(t + 1, n_steps - 1), 0, 0),
                               memory_space=pltpu.SMEM),
                  pl.BlockSpec((tm, D), lambda t: (t, 0)),
                  pl.BlockSpec((tm, ROUTE_LANES), lambda t: (t, 0)),
                  pl.BlockSpec(memory_space=pl.ANY),
                  full((1, D)), full((1, D))],
        out_specs=pl.BlockSpec((tm, D), lambda t: (t, 0)),
        out_shape=jax.ShapeDtypeStruct((T, D), F32),
        scratch_shapes=[pltpu.VMEM((2, tm * TOP_K * SLAB, LANES), F32),
                        pltpu.SemaphoreType.DMA((2,))],
        compiler_params=_cparams(("arbitrary",)),
        name="moe_combine_ln2",
    )(pos8, pos8, h1, route, ys, p["ln2_g"], p["ln2_b"])


def _dispatch_plan(route, counts):
    T = route.shape[0]
    tr = MOE_TILE_ROWS
    n_tiles = T // tr + N_CLASSES
    cnt = counts[0, :N_CLASSES].astype(jnp.int32)
    padded = jnp.maximum((cnt + tr - 1) // tr, 1) * tr
    seg_end = jnp.cumsum(padded)
    seg_start = seg_end - padded
    cls = route[:, ROUTE_CLS].astype(jnp.int32)
    rank = route[:, ROUTE_RANK].astype(jnp.int32)
    classes = jnp.arange(N_CLASSES, dtype=jnp.int32)
    start_of = jnp.sum(jnp.where(cls[:, None] == classes, seg_start, 0), axis=-1)
    pos8 = (start_of + rank) * SLAB
    tile_row0 = jnp.arange(n_tiles, dtype=jnp.int32) * tr
    tile_cls = jnp.sum((seg_end[None, :] <= tile_row0[:, None]).astype(jnp.int32), axis=-1)
    tile_cls = jnp.minimum(tile_cls, N_CLASSES - 1)
    pairs = [(a, b) for a in range(EXPERTS_PER_GROUP) for b in range(a + 1, EXPERTS_PER_GROUP)]
    lo_of = jnp.asarray([g * EXPERTS_PER_GROUP + a for g in range(N_GROUPS) for a, _ in pairs],
                        jnp.int32)
    hi_of = jnp.asarray([g * EXPERTS_PER_GROUP + b for g in range(N_GROUPS) for _, b in pairs],
                        jnp.int32)
    is_cls = tile_cls[:, None] == classes
    tile_lo = jnp.sum(jnp.where(is_cls, lo_of, 0), axis=-1)
    tile_hi = jnp.sum(jnp.where(is_cls, hi_of, 0), axis=-1)
    n_used = (seg_end[-1:] // tr).astype(jnp.int32)
    return pos8, (seg_start + cnt) * SLAB, tile_lo, tile_hi, n_used, n_tiles


def _rope_tables(pos):
    d = SWA_HEAD_DIM
    inv_freq = ROPE_THETA ** (-jnp.arange(0, d, 2, dtype=F32) / d)
    ang = pos[:, None] * inv_freq[None, :]
    c, s = jnp.cos(ang), jnp.sin(ang)
    return (jnp.concatenate([c, c, c, c], axis=-1), jnp.concatenate([-s, s, -s, s], axis=-1),
            jnp.concatenate([c, c], axis=-1).T, jnp.concatenate([s, s], axis=-1).T)


def _prep_params(ln_in_g, ln_in_b, w_in, swa_sinks, mla_q_norm_g, mla_w_uq, mla_kv_norm_g, mla_w_ukv,
                 swa_out_norm_g, mla_out_norm_g, w_o, ln1_g, ln1_b, moe_w_group, moe_b_group,
                 moe_w_router, moe_b_router, moe_w_gate, moe_w_up, moe_w_down, ln2_g, ln2_b):
    order = jnp.asarray(SWA_HEAD_ORDER)
    perm = (order[:, None] * SWA_HEAD_DIM + jnp.arange(SWA_HEAD_DIM)[None, :]).reshape(-1)
    w = w_in[0]
    w_q = w[:, :SWA_Q_COLS][:, perm]
    w_pad = jnp.zeros((D_MODEL, IN_COLS_PADDED - w.shape[1]), w.dtype)
    w_in_p = jnp.concatenate([w_q, w[:, SWA_Q_COLS:], w_pad], axis=1).astype(BF16)

    dqk = MLA_NOPE_DIM + MLA_ROPE_DIM
    wuq = mla_w_uq[0].reshape(MLA_Q_RANK, MLA_HEADS, dqk)
    wuq = jnp.pad(wuq, ((0, 0), (0, 0), (0, MLA_QK_PAD - dqk)))
    wuq_t = jnp.transpose(wuq, (1, 2, 0)).astype(BF16)
    wukv = mla_w_ukv[0].reshape(MLA_KV_RANK, MLA_HEADS, MLA_NOPE_DIM + MLA_V_DIM)
    wuk = wukv[:, :, :MLA_NOPE_DIM].reshape(MLA_KV_RANK, MLA_HEADS * MLA_NOPE_DIM).astype(BF16)
    wuv_t = jnp.transpose(wukv[:, :, MLA_NOPE_DIM:], (1, 2, 0)).astype(BF16)

    w_o0 = w_o[0]
    w_o_p = jnp.concatenate([w_o0[:SWA_Q_COLS][perm], w_o0[SWA_Q_COLS:]], axis=0).astype(BF16)

    w_route = jnp.zeros((D_MODEL, ROUTE_LANES), F32)
    w_route = w_route.at[:, :N_GROUPS].set(moe_w_group[0])
    w_route = w_route.at[:, EXPERT_LANE0:EXPERT_LANE0 + N_EXPERTS].set(moe_w_router[0])
    w_route_hi = w_route.astype(BF16)
    w_route_lo = (w_route - w_route_hi.astype(F32)).astype(BF16)
    w_route = jnp.concatenate([w_route_hi, w_route_lo], axis=1)
    b_route = jnp.zeros((1, ROUTE_LANES), F32)
    b_route = b_route.at[0, :N_GROUPS].set(moe_b_group[0])
    b_route = b_route.at[0, EXPERT_LANE0:EXPERT_LANE0 + N_EXPERTS].set(moe_b_router[0])

    return {
        "ln_in_g": ln_in_g[None, :], "ln_in_b": ln_in_b[None, :],
        "w_in": w_in_p,
        "sinks": swa_sinks[0],
        "gq": mla_q_norm_g, "w_uq_t": wuq_t,
        "gkv": mla_kv_norm_g, "w_uk": wuk, "w_uv_t": wuv_t,
        "ga": swa_out_norm_g[0][perm][None, :], "gb": mla_out_norm_g,
        "w_o": w_o_p, "ln1_g": ln1_g, "ln1_b": ln1_b,
        "w_route": w_route, "b_route": b_route,
        "w_gu": jnp.concatenate([moe_w_gate[0], moe_w_up[0]], axis=-1).astype(BF16),
        "w_down": moe_w_down[0].astype(BF16),
        "ln2_g": ln2_g, "ln2_b": ln2_b,
    }


def kernel(x, meta_tokens, ln_in_g, ln_in_b, w_in, swa_sinks, mla_q_norm_g, mla_w_uq, mla_kv_norm_g, mla_w_ukv, swa_out_norm_g, mla_out_norm_g, w_o, ln1_g, ln1_b, moe_w_group, moe_b_group, moe_w_router, moe_b_router, moe_w_gate, moe_w_up, moe_w_down, ln2_g, ln2_b):
    B, S, D = x.shape
    p = _prep_params(ln_in_g, ln_in_b, w_in, swa_sinks, mla_q_norm_g, mla_w_uq, mla_kv_norm_g,
                     mla_w_ukv, swa_out_norm_g, mla_out_norm_g, w_o, ln1_g, ln1_b, moe_w_group,
                     moe_b_group, moe_w_router, moe_b_router, moe_w_gate, moe_w_up, moe_w_down,
                     ln2_g, ln2_b)
    T = B * S
    tm = min(PROJ_TILE, S)

    x_meta = jnp.concatenate([jnp.zeros((N_PAD, D), x.dtype), meta_tokens.astype(x.dtype)])[None]
    pos_meta = jnp.maximum(jnp.arange(BLOCK) - N_PAD, 0).astype(F32)
    pos_tok = (jnp.arange(S) + N_META).astype(F32)
    _, ka_m, va_m, _, km_m, vt_m = _in_proj(x_meta, _rope_tables(pos_meta), p, BLOCK)
    qa, ka, va, qt, km, vt = _in_proj(x, _rope_tables(pos_tok), p, tm)

    a_out = _swa(p["sinks"], qa, ka, va, ka_m, va_m, min(WIDE_TILE, S))
    b_out = _mla(qt, km, vt, km_m[:, :, N_PAD:, :], vt_m[..., N_PAD:], tm)
    h1, route, counts = _out_proj(x, a_out, b_out, p, min(WIDE_TILE, S))
    h1 = h1.reshape(B * S, D)
    route = route.reshape(B * S, ROUTE_LANES)

    pos8, pad_start8, tile_lo, tile_hi, n_used, n_tiles = _dispatch_plan(route, counts)
    xs = _dispatch(h1, pos8, pad_start8, n_used, n_tiles, min(DISPATCH_TILE, T))
    ys = _experts(xs, tile_lo, tile_hi, n_used, p, n_tiles)
    out = _combine(h1, route, pos8, ys, p, min(COMBINE_TILE, T))
    return out.reshape(B, S, D)
```

```python
import functools

import jax
import jax.numpy as jnp
from jax import lax
from jax.experimental import pallas as pl
from jax.experimental.pallas import tpu as pltpu

D_MODEL = 1024
N_META = 16
BLOCK = 128
N_PAD = BLOCK - N_META
ROPE_THETA = 10000.0
SWA_HEADS = 8
SWA_KV_HEADS = 2
SWA_HEAD_DIM = 64
MLA_HEADS = 4
MLA_Q_RANK = 256
MLA_KV_RANK = 256
MLA_NOPE_DIM = 128
MLA_ROPE_DIM = 64
MLA_V_DIM = 128
SWA_Q_COLS = SWA_HEADS * SWA_HEAD_DIM
SWA_KV_COLS = SWA_KV_HEADS * SWA_HEAD_DIM
MLA_OUT_COLS = MLA_HEADS * MLA_V_DIM
N_GROUPS = 4
EXPERTS_PER_GROUP = 8
N_EXPERTS = N_GROUPS * EXPERTS_PER_GROUP
D_EXPERT = 256
LN_EPS = 1e-5
RMS_EPS = 1e-6
DEPTH = 1
ALPHA = (2.0 * DEPTH) ** 0.25
NEG = -1e30
LOG2_E = 1.4426950408889634

LANES = 128
IN_COLS_PADDED = 1408
MLA_QK_PAD = 256
ROUTE_LANES = 128
EXPERT_LANE0 = 32
ROUTE_CLS, ROUTE_W_LO, ROUTE_W_HI, ROUTE_RANK = range(4)
TOP_K = 2
PAIRS_PER_GROUP = EXPERTS_PER_GROUP * (EXPERTS_PER_GROUP - 1) // 2
N_CLASSES = N_GROUPS * PAIRS_PER_GROUP
MOE_TILE_ROWS = 256
PROJ_TILE = 1024
WIDE_TILE = 1024
DISPATCH_TILE = 2048
COMBINE_TILE = 1024
DMA_BURST = 8
VMEM_LIMIT = 56 * 1024 * 1024

SWA_HEAD_ORDER = (0, 4, 1, 5, 2, 6, 3, 7)

BF16 = jnp.bfloat16
F32 = jnp.float32


def _cparams(sem):
    return pltpu.CompilerParams(dimension_semantics=sem, vmem_limit_bytes=VMEM_LIMIT)


def _layer_norm(x, g, b):
    mu = jnp.mean(x, axis=-1, keepdims=True)
    xc = x - mu
    var = jnp.mean(xc * xc, axis=-1, keepdims=True)
    return xc * lax.rsqrt(var + LN_EPS) * g + b


def _rms_scale(x):
    return lax.rsqrt(jnp.mean(x * x, axis=-1, keepdims=True) + RMS_EPS)


def _rope128(x, cos, sin_signed, lane_lo):
    up = pltpu.roll(x, LANES - 32, 1)
    dn = pltpu.roll(x, 32, 1)
    return x * cos + jnp.where(lane_lo, up, dn) * sin_signed


def _dot(a, b):
    return jnp.dot(a, b, preferred_element_type=F32)


def _dot_nt(a, b):
    return lax.dot_general(a, b, (((1,), (1,)), ((), ())), preferred_element_type=F32)


def _in_proj_kernel(x_ref, g_ref, b_ref, w_ref, cos_ref, sin_ref, cost_ref, sint_ref, gq_ref,
                    wuqt_ref, gkv_ref, wuk_ref, wuvt_ref,
                    qa_ref, ka_ref, va_ref, qt_ref, km_ref, vt_ref):
    h = _layer_norm(x_ref[0], g_ref[...], b_ref[...])
    u = _dot(h.astype(BF16), w_ref[...])
    cos = cos_ref[...]
    sin = sin_ref[...]
    lane = lax.broadcasted_iota(jnp.int32, (1, LANES), 1)
    lane_lo = (lane % 64) < 32
    rope = functools.partial(_rope128, cos=cos, sin_signed=sin, lane_lo=lane_lo)

    swa_scale = SWA_HEAD_DIM ** -0.5
    for c in range(SWA_Q_COLS // LANES):
        qc = rope(u[:, c * LANES:(c + 1) * LANES]) * swa_scale
        qa_ref[0, :, c * LANES:(c + 1) * LANES] = qc.astype(BF16)
    ka_ref[0] = rope(u[:, 512:640]).astype(BF16)
    va_ref[0] = u[:, 640:768].astype(BF16)

    cq = u[:, 768:1024]
    cqn = (cq * _rms_scale(cq) * gq_ref[...]).astype(BF16)
    ckv = u[:, 1024:1280]
    ckvn = (ckv * _rms_scale(ckv) * gkv_ref[...]).astype(BF16)
    k_nope = _dot(ckvn, wuk_ref[...])
    kr = rope(u[:, 1280:1408]).astype(BF16)
    q_scale = (MLA_NOPE_DIM + MLA_ROPE_DIM) ** -0.5 * LOG2_E
    cost = cost_ref[...]
    sint = sint_ref[...]
    half = MLA_ROPE_DIM // 2
    for hd in range(MLA_HEADS):
        qt = _dot_nt(wuqt_ref[hd], cqn)
        qr = qt[MLA_NOPE_DIM:MLA_NOPE_DIM + MLA_ROPE_DIM]
        rot = jnp.concatenate([-qr[half:], qr[:half]], axis=0)
        qt_ref[0, hd, 0:MLA_NOPE_DIM, :] = (qt[:MLA_NOPE_DIM] * q_scale).astype(BF16)
        qt_ref[0, hd, MLA_NOPE_DIM:MLA_NOPE_DIM + MLA_ROPE_DIM, :] = (
            (qr * cost + rot * sint) * q_scale).astype(BF16)
        qt_ref[0, hd, MLA_NOPE_DIM + MLA_ROPE_DIM:, :] = (
            qt[MLA_NOPE_DIM + MLA_ROPE_DIM:]).astype(BF16)
        km_ref[0, hd, :, 0:LANES] = k_nope[:, hd * LANES:(hd + 1) * LANES].astype(BF16)
        km_ref[0, hd, :, LANES:2 * LANES] = kr
        vt_ref[0, hd, 0] = _dot_nt(wuvt_ref[hd], ckvn).astype(BF16)


def _in_proj(x, tables, p, tm):
    cos, sin, cos_t, sin_t = tables
    B, S, D = x.shape
    grid = (B, S // tm)
    full = lambda shape: pl.BlockSpec(shape, lambda b, i: (0,) * len(shape))
    out_shape = (
        jax.ShapeDtypeStruct((B, S, SWA_Q_COLS), BF16),
        jax.ShapeDtypeStruct((B, S, SWA_KV_COLS), BF16),
        jax.ShapeDtypeStruct((B, S, SWA_KV_COLS), BF16),
        jax.ShapeDtypeStruct((B, MLA_HEADS, MLA_QK_PAD, S), BF16),
        jax.ShapeDtypeStruct((B, MLA_HEADS, S, MLA_QK_PAD), BF16),
        jax.ShapeDtypeStruct((B, MLA_HEADS, S // tm, MLA_V_DIM, tm), BF16),
    )
    row = lambda w: pl.BlockSpec((1, tm, w), lambda b, i: (b, i, 0))
    head = lambda w: pl.BlockSpec((1, MLA_HEADS, tm, w), lambda b, i: (b, 0, i, 0))
    return pl.pallas_call(
        _in_proj_kernel,
        grid=grid,
        in_specs=[
            row(D),
            full((1, D)), full((1, D)),
            full((D, IN_COLS_PADDED)),
            pl.BlockSpec((tm, LANES), lambda b, i: (i, 0)),
            pl.BlockSpec((tm, LANES), lambda b, i: (i, 0)),
            pl.BlockSpec((MLA_ROPE_DIM, tm), lambda b, i: (0, i)),
            pl.BlockSpec((MLA_ROPE_DIM, tm), lambda b, i: (0, i)),
            full((1, MLA_Q_RANK)), full((MLA_HEADS, MLA_QK_PAD, MLA_Q_RANK)),
            full((1, MLA_KV_RANK)), full((MLA_KV_RANK, MLA_HEADS * MLA_NOPE_DIM)),
            full((MLA_HEADS, MLA_V_DIM, MLA_KV_RANK)),
        ],
        out_specs=(row(SWA_Q_COLS), row(SWA_KV_COLS), row(SWA_KV_COLS),
                   pl.BlockSpec((1, MLA_HEADS, MLA_QK_PAD, tm), lambda b, i: (b, 0, 0, i)),
                   head(MLA_QK_PAD),
                   pl.BlockSpec((1, MLA_HEADS, 1, MLA_V_DIM, tm), lambda b, i: (b, 0, i, 0, 0))),
        out_shape=out_shape,
        compiler_params=_cparams(("parallel", "parallel")),
        name="in_proj",
    )(x, p["ln_in_g"], p["ln_in_b"], p["w_in"], cos, sin, cos_t, sin_t, p["gq"], p["w_uq_t"],
      p["gkv"], p["w_uk"], p["w_uv_t"])


def _swa_kernel(sink_ref, qa_ref, kc_ref, vc_ref, kp_ref, vp_ref, kmeta_ref, vmeta_ref, o_ref,
                *, blocks_per_step):
    i = pl.program_id(1)
    first = i == 0
    kprev = jnp.where(first, kmeta_ref[0], kp_ref[0])
    vprev = jnp.where(first, vmeta_ref[0], vp_ref[0])

    row = lax.broadcasted_iota(jnp.int32, (2 * BLOCK, 2 * BLOCK), 0) % BLOCK
    col = lax.broadcasted_iota(jnp.int32, (2 * BLOCK, 2 * BLOCK), 1)
    base_mask = ((col < BLOCK) & (col > row)) | ((col >= BLOCK) & ((col - BLOCK) <= row))
    first_mask = base_mask & (col >= jnp.where(first, N_PAD, 0))
    lane = lax.broadcasted_iota(jnp.int32, (1, LANES), 1)
    lane_kv0 = lane < SWA_HEAD_DIM
    top_rows = lax.broadcasted_iota(jnp.int32, (2 * BLOCK, 1), 0) < BLOCK

    for r in range(blocks_per_step):
        rows = slice(r * BLOCK, (r + 1) * BLOCK)
        if r == 0:
            kp, vp = kprev, vprev
            mask = first_mask
        else:
            prev_rows = slice((r - 1) * BLOCK, r * BLOCK)
            kp, vp = kc_ref[0, prev_rows, :], vc_ref[0, prev_rows, :]
            mask = base_mask
        keys = jnp.concatenate([kp, kc_ref[0, rows, :]], axis=0)
        vals = jnp.concatenate([vp, vc_ref[0, rows, :]], axis=0)
        for c in range(SWA_Q_COLS // LANES):
            qc = qa_ref[0, rows, c * LANES:(c + 1) * LANES]
            zero = jnp.zeros_like(qc)
            q2 = jnp.concatenate([jnp.where(lane_kv0, qc, zero),
                                  jnp.where(lane_kv0, zero, qc)], axis=0)
            s = jnp.where(mask, _dot_nt(q2, keys), NEG)
            sink = jnp.where(top_rows, sink_ref[c], sink_ref[c + 4])
            m = jnp.maximum(jnp.max(s, axis=-1, keepdims=True), sink)
            e = jnp.exp(s - m)
            denom = jnp.sum(e, axis=-1, keepdims=True) + jnp.exp(sink - m)
            pr = (e * (1.0 / denom)).astype(BF16)
            o2 = _dot(pr, vals)
            o = jnp.where(lane_kv0, o2[:BLOCK], o2[BLOCK:])
            o_ref[0, rows, c * LANES:(c + 1) * LANES] = o.astype(BF16)


def _swa(sinks, qa, ka, va, ka_meta, va_meta, tb):
    B, S, _ = qa.shape
    R = tb // BLOCK
    kvw = SWA_KV_COLS
    cur = lambda w: pl.BlockSpec((1, tb, w), lambda b, i: (b, i, 0))
    prev = pl.BlockSpec((1, BLOCK, kvw), lambda b, i: (b, jnp.maximum(i * R - 1, 0), 0))
    meta = pl.BlockSpec((1, BLOCK, kvw), lambda b, i: (0, 0, 0))
    return pl.pallas_call(
        functools.partial(_swa_kernel, blocks_per_step=R),
        grid=(B, S // tb),
        in_specs=[pl.BlockSpec(memory_space=pltpu.SMEM),
                  cur(SWA_Q_COLS), cur(kvw), cur(kvw), prev, prev, meta, meta],
        out_specs=cur(SWA_Q_COLS),
        out_shape=jax.ShapeDtypeStruct((B, S, SWA_Q_COLS), BF16),
        compiler_params=_cparams(("parallel", "parallel")),
        name="swa_attn",
    )(sinks, qa, ka, va, ka, va, ka_meta, va_meta)


def _mla_kernel(qt_ref, k_ref, vt_ref, kmeta_ref, vtmeta_ref, o_ref,
                sa_ref, sb_ref, sa_max_ref, sb_max_ref, m_ref, l_ref, acc_ref, *, tq):
    i = pl.program_id(2)
    qt = qt_ref[0, 0]

    s = _dot(kmeta_ref[0, 0], qt)
    m = jnp.max(s, axis=0, keepdims=True)
    p = jnp.exp2(s - m)
    m_ref[...] = m
    l_ref[...] = jnp.sum(p, axis=0, keepdims=True)
    acc_ref[...] = _dot(vtmeta_ref[0, 0, 0], p.astype(BF16))

    def scores_into(s_ref, smax_ref, j):
        start = pl.multiple_of(j * tq, tq)
        s = _dot(k_ref[0, 0, pl.ds(start, tq), :], qt)
        s_ref[...] = s
        smax_ref[...] = jnp.max(s, axis=0, keepdims=True)

    def absorb(s_ref, smax_ref, j, causal=False):
        s = s_ref[...]
        if causal:
            key = lax.broadcasted_iota(jnp.int32, (tq, tq), 0)
            query = lax.broadcasted_iota(jnp.int32, (tq, tq), 1)
            s = jnp.where(key <= query, s, NEG)
            smax = jnp.max(s, axis=0, keepdims=True)
        else:
            smax = smax_ref[...]
        m = m_ref[...]
        m_new = jnp.maximum(m, smax)
        alpha = jnp.exp2(m - m_new)
        p = jnp.exp2(s - m_new)
        m_ref[...] = m_new
        l_ref[...] = alpha * l_ref[...] + jnp.sum(p, axis=0, keepdims=True)
        acc_ref[...] = alpha * acc_ref[...] + _dot(vt_ref[0, 0, j], p.astype(BF16))

    scores_into(sa_ref, sa_max_ref, 0)

    def pair(t, carry):
        scores_into(sb_ref, sb_max_ref, 2 * t + 1)
        absorb(sa_ref, sa_max_ref, 2 * t)
        scores_into(sa_ref, sa_max_ref, 2 * t + 2)
        absorb(sb_ref, sb_max_ref, 2 * t + 1)
        return carry

    lax.fori_loop(0, i // 2, pair, 0)

    @pl.when(i % 2 == 1)
    def _():
        scores_into(sb_ref, sb_max_ref, i)
        absorb(sa_ref, sa_max_ref, i - 1)
        absorb(sb_ref, sb_max_ref, i, causal=True)

    @pl.when(i % 2 == 0)
    def _():
        absorb(sa_ref, sa_max_ref, i, causal=True)

    o_ref[0] = (acc_ref[...] * (1.0 / l_ref[...])).T.astype(BF16)


def _mla(qt, km, vt, km_meta, vt_meta, tq):
    B, H, _, S = qt.shape
    return pl.pallas_call(
        functools.partial(_mla_kernel, tq=tq),
        grid=(B, H, S // tq),
        in_specs=[
            pl.BlockSpec((1, 1, MLA_QK_PAD, tq), lambda b, h, i: (b, h, 0, i)),
            pl.BlockSpec((1, 1, S, MLA_QK_PAD), lambda b, h, i: (b, h, 0, 0)),
            pl.BlockSpec((1, 1, S // tq, MLA_V_DIM, tq), lambda b, h, i: (b, h, 0, 0, 0)),
            pl.BlockSpec((1, 1, N_META, MLA_QK_PAD), lambda b, h, i: (0, h, 0, 0)),
            pl.BlockSpec((1, 1, 1, MLA_V_DIM, N_META), lambda b, h, i: (0, h, 0, 0, 0)),
        ],
        out_specs=pl.BlockSpec((1, tq, MLA_V_DIM), lambda b, h, i: (b, i, h)),
        out_shape=jax.ShapeDtypeStruct((B, S, MLA_OUT_COLS), BF16),
        scratch_shapes=[pltpu.VMEM((tq, tq), F32), pltpu.VMEM((tq, tq), F32),
                        pltpu.VMEM((1, tq), F32), pltpu.VMEM((1, tq), F32),
                        pltpu.VMEM((1, tq), F32), pltpu.VMEM((1, tq), F32),
                        pltpu.VMEM((MLA_V_DIM, tq), F32)],
        compiler_params=_cparams(("parallel", "parallel", "arbitrary")),
        name="mla_attn",
    )(qt, km, vt, km_meta, vt_meta)


def _out_proj_kernel(x_ref, a_ref, b_ref, gin_ref, bin_ref, ga_ref, gb_ref, wo_ref, g1_ref, b1_ref,
                     wrt_ref, brt_ref, h1_ref, route_ref, counts_out_ref, count_ref):
    h = _layer_norm(x_ref[0], gin_ref[...], bin_ref[...])
    a = a_ref[0].astype(F32)
    b = b_ref[0].astype(F32)
    an = (a * _rms_scale(a) * ga_ref[...]).astype(BF16)
    bn = (b * _rms_scale(b) * gb_ref[...]).astype(BF16)
    mix = _dot(jnp.concatenate([an, bn], axis=-1), wo_ref[...])
    h1 = _layer_norm(ALPHA * h + mix, g1_ref[...], b1_ref[...])
    h1_ref[0] = h1

    x_hi = h1.astype(BF16)
    x_lo = (h1 - x_hi.astype(F32)).astype(BF16)
    hi_both = _dot(x_hi, wrt_ref[...])
    logits = (hi_both[:, :ROUTE_LANES] + hi_both[:, ROUTE_LANES:]
              + _dot(x_lo, wrt_ref[:, :ROUTE_LANES]) + brt_ref[...])
    lane = lax.broadcasted_iota(jnp.int32, logits.shape, 1).astype(F32)
    ninf = -jnp.inf
    no_lane = float(ROUTE_LANES)
    gl = jnp.where(lane < N_GROUPS, logits, ninf)
    gmax = jnp.max(gl, axis=-1, keepdims=True)
    g_top = 1.0 / jnp.sum(jnp.exp(gl - gmax), axis=-1, keepdims=True)
    g_idx = jnp.min(jnp.where(gl == gmax, lane, no_lane), axis=-1, keepdims=True)
    lo = EXPERT_LANE0 + g_idx * EXPERTS_PER_GROUP
    el = jnp.where((lane >= lo) & (lane < lo + EXPERTS_PER_GROUP), logits, ninf)
    m1 = jnp.max(el, axis=-1, keepdims=True)
    i1 = jnp.min(jnp.where(el == m1, lane, no_lane), axis=-1, keepdims=True)
    el2 = jnp.where(lane == i1, ninf, el)
    m2 = jnp.max(el2, axis=-1, keepdims=True)
    i2 = jnp.min(jnp.where(el2 == m2, lane, no_lane), axis=-1, keepdims=True)
    e2 = jnp.exp(m2 - m1)
    w1 = g_top / (1.0 + e2)
    w2 = g_top * e2 / (1.0 + e2)

    first_is_lo = i1 < i2
    a = jnp.minimum(i1, i2) - lo
    b = jnp.maximum(i1, i2) - lo
    cls = g_idx * PAIRS_PER_GROUP + a * (15.0 - a) * 0.5 + (b - a - 1.0)
    w_lo = jnp.where(first_is_lo, w1, w2)
    w_hi = jnp.where(first_is_lo, w2, w1)

    @pl.when((pl.program_id(0) == 0) & (pl.program_id(1) == 0))
    def _():
        count_ref[...] = jnp.zeros_like(count_ref)

    tm = logits.shape[0]
    onehot = jnp.where(lane == cls, 1.0, 0.0)
    r_i = lax.broadcasted_iota(jnp.int32, (tm, tm), 0)
    c_i = lax.broadcasted_iota(jnp.int32, (tm, tm), 1)
    earlier = jnp.where(r_i > c_i, 1.0, 0.0).astype(BF16)
    before = _dot(earlier, onehot.astype(BF16)) + count_ref[...]
    rank = jnp.sum(jnp.where(lane == cls, before, 0.0), axis=-1, keepdims=True)
    count_ref[...] += jnp.sum(onehot, axis=0, keepdims=True)
    counts_out_ref[...] = count_ref[...]

    route = jnp.where(lane == ROUTE_CLS, cls, 0.0)
    route = jnp.where(lane == ROUTE_W_LO, w_lo, route)
    route = jnp.where(lane == ROUTE_W_HI, w_hi, route)
    route = jnp.where(lane == ROUTE_RANK, rank, route)
    route_ref[0] = route


def _out_proj(x, a_out, b_out, p, tm):
    B, S, D = x.shape
    full = lambda shape: pl.BlockSpec(shape, lambda b, i: (0,) * len(shape))
    row = lambda w: pl.BlockSpec((1, tm, w), lambda b, i: (b, i, 0))
    return pl.pallas_call(
        _out_proj_kernel,
        grid=(B, S // tm),
        in_specs=[row(D), row(SWA_Q_COLS), row(MLA_OUT_COLS),
                  full((1, D)), full((1, D)), full((1, SWA_Q_COLS)), full((1, MLA_OUT_COLS)),
                  full((D, D)), full((1, D)), full((1, D)),
                  full((D, 2 * ROUTE_LANES)), full((1, ROUTE_LANES))],
        out_specs=(row(D), row(ROUTE_LANES), full((1, ROUTE_LANES))),
        out_shape=(jax.ShapeDtypeStruct((B, S, D), F32),
                   jax.ShapeDtypeStruct((B, S, ROUTE_LANES), F32),
                   jax.ShapeDtypeStruct((1, ROUTE_LANES), F32)),
        scratch_shapes=[pltpu.VMEM((1, ROUTE_LANES), F32)],
        compiler_params=_cparams(("arbitrary", "arbitrary")),
        name="out_proj_route",
    )(x, a_out, b_out, p["ln_in_g"], p["ln_in_b"], p["ga"], p["gb"], p["w_o"], p["ln1_g"],
      p["ln1_b"], p["w_route"], p["b_route"])


SLAB = D_MODEL // LANES
PAIR_PITCH = 3 * SLAB


def _slab_copy(src_ref, src_tok8, dst_ref, dst_tok8, sem):
    return pltpu.make_async_copy(src_ref.at[pl.ds(pl.multiple_of(src_tok8, SLAB), SLAB)],
                                 dst_ref.at[pl.ds(pl.multiple_of(dst_tok8, SLAB), SLAB)], sem)


def _to_slabs(x, slab_ref, n):
    for s in range(SLAB):
        slab_ref[pl.ds(s, n, stride=SLAB), :] = x[:, s * LANES:(s + 1) * LANES]


def _from_slabs(slab_ref, n):
    return jnp.concatenate([slab_ref[pl.ds(s, n, stride=SLAB), :] for s in range(SLAB)], axis=1)


def _dispatch_kernel(padstart_ref, nu_ref, pos_ref, h1_ref, xs_ref, rows_ref, zero_ref, sem, *,
                     tm, n_tiles):
    tail = zero_ref.shape[0]

    def zero_fill(start):
        return pltpu.make_async_copy(
            zero_ref, xs_ref.at[pl.ds(pl.multiple_of(start, SLAB), tail)], sem)

    @pl.when(pl.program_id(0) == 0)
    def _():
        zero_ref[...] = jnp.zeros_like(zero_ref)
        for parity in range(2):
            tails = [zero_fill(padstart_ref[c]) for c in range(parity, N_CLASSES, 2)]
            for t in tails:
                t.start()
            for t in tails:
                t.wait()

        def fill_unused(j, carry):
            fill = zero_fill(j * tail)
            fill.start()
            fill.wait()
            return carry

        lax.fori_loop(nu_ref[0], n_tiles + 1, fill_unused, 0)

    _to_slabs(h1_ref[...], rows_ref, tm)

    def issue(r0, carry):
        for k in range(DMA_BURST):
            r = r0 * DMA_BURST + k
            _slab_copy(rows_ref, r * SLAB, xs_ref, pos_ref[0, 0, r], sem).start(priority=k % 2)
        return carry

    lax.fori_loop(0, tm // DMA_BURST, issue, 0)
    pltpu.make_async_copy(rows_ref, xs_ref.at[pl.ds(0, tm * SLAB)], sem).wait()


def _dispatch(h1, pos8, pad_start8, n_used, n_tiles, tm):
    T, D = h1.shape
    pos8 = pos8.reshape(T // tm, 1, tm)
    n_rows = (n_tiles + 1) * MOE_TILE_ROWS
    grid_spec = pltpu.PrefetchScalarGridSpec(
        num_scalar_prefetch=2,
        grid=(T // tm,),
        in_specs=[pl.BlockSpec((1, 1, tm), lambda t, ps, nu: (t, 0, 0),
                               memory_space=pltpu.SMEM),
                  pl.BlockSpec((tm, D), lambda t, ps, nu: (t, 0))],
        out_specs=pl.BlockSpec(memory_space=pl.ANY),
        scratch_shapes=[pltpu.VMEM((tm * SLAB, LANES), F32),
                        pltpu.VMEM((MOE_TILE_ROWS * SLAB, LANES), F32),
                        pltpu.SemaphoreType.DMA],
    )
    return pl.pallas_call(
        functools.partial(_dispatch_kernel, tm=tm, n_tiles=n_tiles),
        grid_spec=grid_spec,
        out_shape=jax.ShapeDtypeStruct((n_rows * SLAB, LANES), F32),
        compiler_params=_cparams(("arbitrary",)),
        name="moe_dispatch",
    )(pad_start8, n_used, pos8, h1)


def _experts_kernel(tlo_ref, thi_ref, nu_ref, x_ref, wgu_lo_ref, wgu_hi_ref, wd_lo_ref, wd_hi_ref,
                    y_ref):
    j = pl.program_id(0)
    tr = MOE_TILE_ROWS

    @pl.when(j < nu_ref[0])
    def _():
        for s in range(TOP_K * SLAB, PAIR_PITCH):
            y_ref[pl.ds(s, tr, stride=PAIR_PITCH), :] = jnp.zeros((tr, LANES), F32)
        x = _from_slabs(x_ref, tr).astype(BF16)
        for k, (wgu_ref, wd_ref) in enumerate(((wgu_lo_ref, wd_lo_ref), (wgu_hi_ref, wd_hi_ref))):
            gu = _dot(x, wgu_ref[0])
            g = gu[:, :D_EXPERT]
            u = gu[:, D_EXPERT:]
            hid = g * (1.0 / (1.0 + jnp.exp(-g))) * u
            y = _dot(hid.astype(BF16), wd_ref[0])
            for s in range(SLAB):
                y_ref[pl.ds(k * SLAB + s, tr, stride=PAIR_PITCH), :] = (
                    y[:, s * LANES:(s + 1) * LANES])

    @pl.when(j >= nu_ref[0])
    def _():
        y_ref[...] = jnp.zeros_like(y_ref)


def _experts(xs, tile_lo, tile_hi, n_used, p, n_tiles):
    D = D_MODEL
    tr = MOE_TILE_ROWS
    last_used = lambda j, nu: jnp.minimum(j, nu[0] - 1)
    wgu = lambda pick: pl.BlockSpec((1, D, 2 * D_EXPERT),
                                    lambda j, tlo, thi, nu: (pick(tlo, thi)[j], 0, 0))
    wd = lambda pick: pl.BlockSpec((1, D_EXPERT, D),
                                   lambda j, tlo, thi, nu: (pick(tlo, thi)[j], 0, 0))
    lo = lambda tlo, thi: tlo
    hi = lambda tlo, thi: thi
    grid_spec = pltpu.PrefetchScalarGridSpec(
        num_scalar_prefetch=3,
        grid=(n_tiles,),
        in_specs=[pl.BlockSpec((tr * SLAB, LANES), lambda j, tlo, thi, nu: (last_used(j, nu), 0)),
                  wgu(lo), wgu(hi), wd(lo), wd(hi)],
        out_specs=pl.BlockSpec((tr * PAIR_PITCH, LANES), lambda j, tlo, thi, nu: (j, 0)),
    )
    return pl.pallas_call(
        _experts_kernel,
        grid_spec=grid_spec,
        out_shape=jax.ShapeDtypeStruct((n_tiles * tr * PAIR_PITCH, LANES), F32),
        compiler_params=_cparams(("arbitrary",)),
        name="moe_experts",
    )(tile_lo, tile_hi, n_used, xs, p["w_gu"], p["w_gu"], p["w_down"], p["w_down"])


def _combine_kernel(pos_ref, pos_next_ref, h1_ref, route_ref, ys_ref, g2_ref, b2_ref, o_ref,
                    ybuf_ref, sem, *, tm):
    pair = TOP_K * SLAB
    t = pl.program_id(0)
    slot = t % 2

    def gather(idx_ref, dst_slot):
        def issue(r0, carry):
            for k in range(DMA_BURST):
                r = r0 * DMA_BURST + k
                src = pl.multiple_of(idx_ref[0, 0, r] * (PAIR_PITCH // SLAB), SLAB)
                pltpu.make_async_copy(
                    ys_ref.at[pl.ds(src, pair)],
                    ybuf_ref.at[dst_slot, pl.ds(pl.multiple_of(r * pair, pair), pair)],
                    sem.at[dst_slot]).start(priority=k % 2)
            return carry

        lax.fori_loop(0, tm // DMA_BURST, issue, 0)

    @pl.when(t == 0)
    def _():
        gather(pos_ref, 0)

    @pl.when(t + 1 < pl.num_programs(0))
    def _():
        gather(pos_next_ref, 1 - slot)

    pltpu.make_async_copy(ys_ref.at[pl.ds(0, tm * pair)], ybuf_ref.at[slot], sem.at[slot]).wait()

    def expert_out(k):
        return jnp.concatenate([ybuf_ref[slot, pl.ds(k * SLAB + s, tm, stride=pair), :]
                                for s in range(SLAB)], axis=1)

    route = route_ref[...]
    w_lo = route[:, ROUTE_W_LO:ROUTE_W_LO + 1]
    w_hi = route[:, ROUTE_W_HI:ROUTE_W_HI + 1]
    ffn = w_lo * expert_out(0) + w_hi * expert_out(1)
    o_ref[...] = _layer_norm(ALPHA * h1_ref[...] + ffn, g2_ref[...], b2_ref[...])


def _combine(h1, route, pos8, ys, p, tm):
    T, D = h1.shape
    n_steps = T // tm
    pos8 = pos8.reshape(n_steps, 1, tm)
    full = lambda shape: pl.BlockSpec(shape, lambda t: (0,) * len(shape))
    return pl.pallas_call(
        functools.partial(_combine_kernel, tm=tm),
        grid=(n_steps,),
        in_specs=[pl.BlockSpec((1, 1, tm), lambda t: (t, 0, 0), memory_space=pltpu.SMEM),
                  pl.BlockSpec((1, 1, tm), lambda t: (jnp.minimum(t + 1, n_steps - 1), 0, 0),
                               memory_space=pltpu.SMEM),
                  pl.BlockSpec((tm, D), lambda t: (t, 0)),
                  pl.BlockSpec((tm, ROUTE_LANES), lambda t: (t, 0)),
                  pl.BlockSpec(memory_space=pl.ANY),
                  full((1, D)), full((1, D))],
        out_specs=pl.BlockSpec((tm, D), lambda t: (t, 0)),
        out_shape=jax.ShapeDtypeStruct((T, D), F32),
        scratch_shapes=[pltpu.VMEM((2, tm * TOP_K * SLAB, LANES), F32),
                        pltpu.SemaphoreType.DMA((2,))],
        compiler_params=_cparams(("arbitrary",)),
        name="moe_combine_ln2",
    )(pos8, pos8, h1, route, ys, p["ln2_g"], p["ln2_b"])


def _dispatch_plan(route, counts):
    T = route.shape[0]
    tr = MOE_TILE_ROWS
    n_tiles = T // tr + N_CLASSES
    cnt = counts[0, :N_CLASSES].astype(jnp.int32)
    padded = jnp.maximum((cnt + tr - 1) // tr, 1) * tr
    seg_end = jnp.cumsum(padded)
    seg_start = seg_end - padded
    cls = route[:, ROUTE_CLS].astype(jnp.int32)
    rank = route[:, ROUTE_RANK].astype(jnp.int32)
    classes = jnp.arange(N_CLASSES, dtype=jnp.int32)
    start_of = jnp.sum(jnp.where(cls[:, None] == classes, seg_start, 0), axis=-1)
    pos8 = (start_of + rank) * SLAB
    tile_row0 = jnp.arange(n_tiles, dtype=jnp.int32) * tr
    tile_cls = jnp.sum((seg_end[None, :] <= tile_row0[:, None]).astype(jnp.int32), axis=-1)
    tile_cls = jnp.minimum(tile_cls, N_CLASSES - 1)
    pairs = [(a, b) for a in range(EXPERTS_PER_GROUP) for b in range(a + 1, EXPERTS_PER_GROUP)]
    lo_of = jnp.asarray([g * EXPERTS_PER_GROUP + a for g in range(N_GROUPS) for a, _ in pairs],
                        jnp.int32)
    hi_of = jnp.asarray([g * EXPERTS_PER_GROUP + b for g in range(N_GROUPS) for _, b in pairs],
                        jnp.int32)
    is_cls = tile_cls[:, None] == classes
    tile_lo = jnp.sum(jnp.where(is_cls, lo_of, 0), axis=-1)
    tile_hi = jnp.sum(jnp.where(is_cls, hi_of, 0), axis=-1)
    n_used = (seg_end[-1:] // tr).astype(jnp.int32)
    return pos8, (seg_start + cnt) * SLAB, tile_lo, tile_hi, n_used, n_tiles


def _rope_tables(pos):
    d = SWA_HEAD_DIM
    inv_freq = ROPE_THETA ** (-jnp.arange(0, d, 2, dtype=F32) / d)
    ang = pos[:, None] * inv_freq[None, :]
    c, s = jnp.cos(ang), jnp.sin(ang)
    return (jnp.concatenate([c, c, c, c], axis=-1), jnp.concatenate([-s, s, -s, s], axis=-1),
            jnp.concatenate([c, c], axis=-1).T, jnp.concatenate([s, s], axis=-1).T)


def _prep_params(ln_in_g, ln_in_b, w_in, swa_sinks, mla_q_norm_g, mla_w_uq, mla_kv_norm_g, mla_w_ukv,
                 swa_out_norm_g, mla_out_norm_g, w_o, ln1_g, ln1_b, moe_w_group, moe_b_group,
                 moe_w_router, moe_b_router, moe_w_gate, moe_w_up, moe_w_down, ln2_g, ln2_b):
    order = jnp.asarray(SWA_HEAD_ORDER)
    perm = (order[:, None] * SWA_HEAD_DIM + jnp.arange(SWA_HEAD_DIM)[None, :]).reshape(-1)
    w = w_in[0]
    w_q = w[:, :SWA_Q_COLS][:, perm]
    w_pad = jnp.zeros((D_MODEL, IN_COLS_PADDED - w.shape[1]), w.dtype)
    w_in_p = jnp.concatenate([w_q, w[:, SWA_Q_COLS:], w_pad], axis=1).astype(BF16)

    dqk = MLA_NOPE_DIM + MLA_ROPE_DIM
    wuq = mla_w_uq[0].reshape(MLA_Q_RANK, MLA_HEADS, dqk)
    wuq = jnp.pad(wuq, ((0, 0), (0, 0), (0, MLA_QK_PAD - dqk)))
    wuq_t = jnp.transpose(wuq, (1, 2, 0)).astype(BF16)
    wukv = mla_w_ukv[0].reshape(MLA_KV_RANK, MLA_HEADS, MLA_NOPE_DIM + MLA_V_DIM)
    wuk = wukv[:, :, :MLA_NOPE_DIM].reshape(MLA_KV_RANK, MLA_HEADS * MLA_NOPE_DIM).astype(BF16)
    wuv_t = jnp.transpose(wukv[:, :, MLA_NOPE_DIM:], (1, 2, 0)).astype(BF16)

    w_o0 = w_o[0]
    w_o_p = jnp.concatenate([w_o0[:SWA_Q_COLS][perm], w_o0[SWA_Q_COLS:]], axis=0).astype(BF16)

    w_route = jnp.zeros((D_MODEL, ROUTE_LANES), F32)
    w_route = w_route.at[:, :N_GROUPS].set(moe_w_group[0])
    w_route = w_route.at[:, EXPERT_LANE0:EXPERT_LANE0 + N_EXPERTS].set(moe_w_router[0])
    w_route_hi = w_route.astype(BF16)
    w_route_lo = (w_route - w_route_hi.astype(F32)).astype(BF16)
    w_route = jnp.concatenate([w_route_hi, w_route_lo], axis=1)
    b_route = jnp.zeros((1, ROUTE_LANES), F32)
    b_route = b_route.at[0, :N_GROUPS].set(moe_b_group[0])
    b_route = b_route.at[0, EXPERT_LANE0:EXPERT_LANE0 + N_EXPERTS].set(moe_b_router[0])

    return {
        "ln_in_g": ln_in_g[None, :], "ln_in_b": ln_in_b[None, :],
        "w_in": w_in_p,
        "sinks": swa_sinks[0],
        "gq": mla_q_norm_g, "w_uq_t": wuq_t,
        "gkv": mla_kv_norm_g, "w_uk": wuk, "w_uv_t": wuv_t,
        "ga": swa_out_norm_g[0][perm][None, :], "gb": mla_out_norm_g,
        "w_o": w_o_p, "ln1_g": ln1_g, "ln1_b": ln1_b,
        "w_route": w_route, "b_route": b_route,
        "w_gu": jnp.concatenate([moe_w_gate[0], moe_w_up[0]], axis=-1).astype(BF16),
        "w_down": moe_w_down[0].astype(BF16),
        "ln2_g": ln2_g, "ln2_b": ln2_b,
    }


def kernel(x, meta_tokens, ln_in_g, ln_in_b, w_in, swa_sinks, mla_q_norm_g, mla_w_uq, mla_kv_norm_g, mla_w_ukv, swa_out_norm_g, mla_out_norm_g, w_o, ln1_g, ln1_b, moe_w_group, moe_b_group, moe_w_router, moe_b_router, moe_w_gate, moe_w_up, moe_w_down, ln2_g, ln2_b):
    B, S, D = x.shape
    p = _prep_params(ln_in_g, ln_in_b, w_in, swa_sinks, mla_q_norm_g, mla_w_uq, mla_kv_norm_g,
                     mla_w_ukv, swa_out_norm_g, mla_out_norm_g, w_o, ln1_g, ln1_b, moe_w_group,
                     moe_b_group, moe_w_router, moe_b_router, moe_w_gate, moe_w_up, moe_w_down,
                     ln2_g, ln2_b)
    T = B * S
    tm = min(PROJ_TILE, S)

    x_meta = jnp.concatenate([jnp.zeros((N_PAD, D), x.dtype), meta_tokens.astype(x.dtype)])[None]
    pos_meta = jnp.maximum(jnp.arange(BLOCK) - N_PAD, 0).astype(F32)
    pos_tok = (jnp.arange(S) + N_META).astype(F32)
    _, ka_m, va_m, _, km_m, vt_m = _in_proj(x_meta, _rope_tables(pos_meta), p, BLOCK)
    qa, ka, va, qt, km, vt = _in_proj(x, _rope_tables(pos_tok), p, tm)

    a_out = _swa(p["sinks"], qa, ka, va, ka_m, va_m, min(WIDE_TILE, S))
    b_out = _mla(qt, km, vt, km_m[:, :, N_PAD:, :], vt_m[..., N_PAD:], tm)
    h1, route, counts = _out_proj(x, a_out, b_out, p, min(WIDE_TILE, S))
    h1 = h1.reshape(B * S, D)
    route = route.reshape(B * S, ROUTE_LANES)

    pos8, pad_start8, tile_lo, tile_hi, n_used, n_tiles = _dispatch_plan(route, counts)
    xs = _dispatch(h1, pos8, pad_start8, n_used, n_tiles, min(DISPATCH_TILE, T))
    ys = _experts(xs, tile_lo, tile_hi, n_used, p, n_tiles)
    out = _combine(h1, route, pos8, ys, p, min(COMBINE_TILE, T))
    return out.reshape(B, S, D)
```

```python
import functools

import jax
import jax.numpy as jnp
from jax import lax
from jax.experimental import pallas as pl
from jax.experimental.pallas import tpu as pltpu

D_MODEL = 1024
N_META = 16
BLOCK = 128
N_PAD = BLOCK - N_META
ROPE_THETA = 10000.0
SWA_HEADS = 8
SWA_KV_HEADS = 2
SWA_HEAD_DIM = 64
MLA_HEADS = 4
MLA_Q_RANK = 256
MLA_KV_RANK = 256
MLA_NOPE_DIM = 128
MLA_ROPE_DIM = 64
MLA_V_DIM = 128
SWA_Q_COLS = SWA_HEADS * SWA_HEAD_DIM
SWA_KV_COLS = SWA_KV_HEADS * SWA_HEAD_DIM
MLA_OUT_COLS = MLA_HEADS * MLA_V_DIM
N_GROUPS = 4
EXPERTS_PER_GROUP = 8
N_EXPERTS = N_GROUPS * EXPERTS_PER_GROUP
D_EXPERT = 256
LN_EPS = 1e-5
RMS_EPS = 1e-6
DEPTH = 1
ALPHA = (2.0 * DEPTH) ** 0.25
NEG = -1e30
LOG2_E = 1.4426950408889634

LANES = 128
IN_COLS_PADDED = 1408
MLA_QK_PAD = 256
ROUTE_LANES = 128
EXPERT_LANE0 = 32
ROUTE_CLS, ROUTE_W_LO, ROUTE_W_HI, ROUTE_RANK = range(4)
TOP_K = 2
PAIRS_PER_GROUP = EXPERTS_PER_GROUP * (EXPERTS_PER_GROUP - 1) // 2
N_CLASSES = N_GROUPS * PAIRS_PER_GROUP
MOE_TILE_ROWS = 256
PROJ_TILE = 1024
WIDE_TILE = 1024
DISPATCH_TILE = 2048
COMBINE_TILE = 1024
DMA_BURST = 8
VMEM_LIMIT = 56 * 1024 * 1024

SWA_HEAD_ORDER = (0, 4, 1, 5, 2, 6, 3, 7)

BF16 = jnp.bfloat16
F32 = jnp.float32


def _cparams(sem):
    return pltpu.CompilerParams(dimension_semantics=sem, vmem_limit_bytes=VMEM_LIMIT)


def _layer_norm(x, g, b):
    mu = jnp.mean(x, axis=-1, keepdims=True)
    xc = x - mu
    var = jnp.mean(xc * xc, axis=-1, keepdims=True)
    return xc * lax.rsqrt(var + LN_EPS) * g + b


def _rms_scale(x):
    return lax.rsqrt(jnp.mean(x * x, axis=-1, keepdims=True) + RMS_EPS)


def _rope128(x, cos, sin_signed, lane_lo):
    up = pltpu.roll(x, LANES - 32, 1)
    dn = pltpu.roll(x, 32, 1)
    return x * cos + jnp.where(lane_lo, up, dn) * sin_signed


def _dot(a, b):
    return jnp.dot(a, b, preferred_element_type=F32)


def _dot_nt(a, b):
    return lax.dot_general(a, b, (((1,), (1,)), ((), ())), preferred_element_type=F32)


def _in_proj_kernel(x_ref, g_ref, b_ref, w_ref, cos_ref, sin_ref, cost_ref, sint_ref, gq_ref,
                    wuqt_ref, gkv_ref, wuk_ref, wuvt_ref,
                    qa_ref, ka_ref, va_ref, qt_ref, km_ref, vt_ref):
    h = _layer_norm(x_ref[0], g_ref[...], b_ref[...])
    u = _dot(h.astype(BF16), w_ref[...])
    cos = cos_ref[...]
    sin = sin_ref[...]
    lane = lax.broadcasted_iota(jnp.int32, (1, LANES), 1)
    lane_lo = (lane % 64) < 32
    rope = functools.partial(_rope128, cos=cos, sin_signed=sin, lane_lo=lane_lo)

    swa_scale = SWA_HEAD_DIM ** -0.5
    for c in range(SWA_Q_COLS // LANES):
        qc = rope(u[:, c * LANES:(c + 1) * LANES]) * swa_scale
        qa_ref[0, :, c * LANES:(c + 1) * LANES] = qc.astype(BF16)
    ka_ref[0] = rope(u[:, 512:640]).astype(BF16)
    va_ref[0] = u[:, 640:768].astype(BF16)

    cq = u[:, 768:1024]
    cqn = (cq * _rms_scale(cq) * gq_ref[...]).astype(BF16)
    ckv = u[:, 1024:1280]
    ckvn = (ckv * _rms_scale(ckv) * gkv_ref[...]).astype(BF16)
    k_nope = _dot(ckvn, wuk_ref[...])
    kr = rope(u[:, 1280:1408]).astype(BF16)
    q_scale = (MLA_NOPE_DIM + MLA_ROPE_DIM) ** -0.5 * LOG2_E
    cost = cost_ref[...]
    sint = sint_ref[...]
    half = MLA_ROPE_DIM // 2
    for hd in range(MLA_HEADS):
        qt = _dot_nt(wuqt_ref[hd], cqn)
        qr = qt[MLA_NOPE_DIM:MLA_NOPE_DIM + MLA_ROPE_DIM]
        rot = jnp.concatenate([-qr[half:], qr[:half]], axis=0)
        qt_ref[0, hd, 0:MLA_NOPE_DIM, :] = (qt[:MLA_NOPE_DIM] * q_scale).astype(BF16)
        qt_ref[0, hd, MLA_NOPE_DIM:MLA_NOPE_DIM + MLA_ROPE_DIM, :] = (
            (qr * cost + rot * sint) * q_scale).astype(BF16)
        qt_ref[0, hd, MLA_NOPE_DIM + MLA_ROPE_DIM:, :] = (
            qt[MLA_NOPE_DIM + MLA_ROPE_DIM:]).astype(BF16)
        km_ref[0, hd, :, 0:LANES] = k_nope[:, hd * LANES:(hd + 1) * LANES].astype(BF16)
        km_ref[0, hd, :, LANES:2 * LANES] = kr
        vt_ref[0, hd, 0] = _dot_nt(wuvt_ref[hd], ckvn).astype(BF16)


def _in_proj(x, tables, p, tm):
    cos, sin, cos_t, sin_t = tables
    B, S, D = x.shape
    grid = (B, S // tm)
    full = lambda shape: pl.BlockSpec(shape, lambda b, i: (0,) * len(shape))
    out_shape = (
        jax.ShapeDtypeStruct((B, S, SWA_Q_COLS), BF16),
        jax.ShapeDtypeStruct((B, S, SWA_KV_COLS), BF16),
        jax.ShapeDtypeStruct((B, S, SWA_KV_COLS), BF16),
        jax.ShapeDtypeStruct((B, MLA_HEADS, MLA_QK_PAD, S), BF16),
        jax.ShapeDtypeStruct((B, MLA_HEADS, S, MLA_QK_PAD), BF16),
        jax.ShapeDtypeStruct((B, MLA_HEADS, S // tm, MLA_V_DIM, tm), BF16),
    )
    row = lambda w: pl.BlockSpec((1, tm, w), lambda b, i: (b, i, 0))
    head = lambda w: pl.BlockSpec((1, MLA_HEADS, tm, w), lambda b, i: (b, 0, i, 0))
    return pl.pallas_call(
        _in_proj_kernel,
        grid=grid,
        in_specs=[
            row(D),
            full((1, D)), full((1, D)),
            full((D, IN_COLS_PADDED)),
            pl.BlockSpec((tm, LANES), lambda b, i: (i, 0)),
            pl.BlockSpec((tm, LANES), lambda b, i: (i, 0)),
            pl.BlockSpec((MLA_ROPE_DIM, tm), lambda b, i: (0, i)),
            pl.BlockSpec((MLA_ROPE_DIM, tm), lambda b, i: (0, i)),
            full((1, MLA_Q_RANK)), full((MLA_HEADS, MLA_QK_PAD, MLA_Q_RANK)),
            full((1, MLA_KV_RANK)), full((MLA_KV_RANK, MLA_HEADS * MLA_NOPE_DIM)),
            full((MLA_HEADS, MLA_V_DIM, MLA_KV_RANK)),
        ],
        out_specs=(row(SWA_Q_COLS), row(SWA_KV_COLS), row(SWA_KV_COLS),
                   pl.BlockSpec((1, MLA_HEADS, MLA_QK_PAD, tm), lambda b, i: (b, 0, 0, i)),
                   head(MLA_QK_PAD),
                   pl.BlockSpec((1, MLA_HEADS, 1, MLA_V_DIM, tm), lambda b, i: (b, 0, i, 0, 0))),
        out_shape=out_shape,
        compiler_params=_cparams(("parallel", "parallel")),
        name="in_proj",
    )(x, p["ln_in_g"], p["ln_in_b"], p["w_in"], cos, sin, cos_t, sin_t, p["gq"], p["w_uq_t"],
      p["gkv"], p["w_uk"], p["w_uv_t"])


def _swa_kernel(sink_ref, qa_ref, kc_ref, vc_ref, kp_ref, vp_ref, kmeta_ref, vmeta_ref, o_ref,
                *, blocks_per_step):
    i = pl.program_id(1)
    first = i == 0
    kprev = jnp.where(first, kmeta_ref[0], kp_ref[0])
    vprev = jnp.where(first, vmeta_ref[0], vp_ref[0])

    row = lax.broadcasted_iota(jnp.int32, (2 * BLOCK, 2 * BLOCK), 0) % BLOCK
    col = lax.broadcasted_iota(jnp.int32, (2 * BLOCK, 2 * BLOCK), 1)
    base_mask = ((col < BLOCK) & (col > row)) | ((col >= BLOCK) & ((col - BLOCK) <= row))
    first_mask = base_mask & (col >= jnp.where(first, N_PAD, 0))
    lane = lax.broadcasted_iota(jnp.int32, (1, LANES), 1)
    lane_kv0 = lane < SWA_HEAD_DIM
    top_rows = lax.broadcasted_iota(jnp.int32, (2 * BLOCK, 1), 0) < BLOCK

    for r in range(blocks_per_step):
        rows = slice(r * BLOCK, (r + 1) * BLOCK)
        if r == 0:
            kp, vp = kprev, vprev
            mask = first_mask
        else:
            prev_rows = slice((r - 1) * BLOCK, r * BLOCK)
            kp, vp = kc_ref[0, prev_rows, :], vc_ref[0, prev_rows, :]
            mask = base_mask
        keys = jnp.concatenate([kp, kc_ref[0, rows, :]], axis=0)
        vals = jnp.concatenate([vp, vc_ref[0, rows, :]], axis=0)
        for c in range(SWA_Q_COLS // LANES):
            qc = qa_ref[0, rows, c * LANES:(c + 1) * LANES]
            zero = jnp.zeros_like(qc)
            q2 = jnp.concatenate([jnp.where(lane_kv0, qc, zero),
                                  jnp.where(lane_kv0, zero, qc)], axis=0)
            s = jnp.where(mask, _dot_nt(q2, keys), NEG)
            sink = jnp.where(top_rows, sink_ref[c], sink_ref[c + 4])
            m = jnp.maximum(jnp.max(s, axis=-1, keepdims=True), sink)
            e = jnp.exp(s - m)
            denom = jnp.sum(e, axis=-1, keepdims=True) + jnp.exp(sink - m)
            pr = (e * (1.0 / denom)).astype(BF16)
            o2 = _dot(pr, vals)
            o = jnp.where(lane_kv0, o2[:BLOCK], o2[BLOCK:])
            o_ref[0, rows, c * LANES:(c + 1) * LANES] = o.astype(BF16)


def _swa(sinks, qa, ka, va, ka_meta, va_meta, tb):
    B, S, _ = qa.shape
    R = tb // BLOCK
    kvw = SWA_KV_COLS
    cur = lambda w: pl.BlockSpec((1, tb, w), lambda b, i: (b, i, 0))
    prev = pl.BlockSpec((1, BLOCK, kvw), lambda b, i: (b, jnp.maximum(i * R - 1, 0), 0))
    meta = pl.BlockSpec((1, BLOCK, kvw), lambda b, i: (0, 0, 0))
    return pl.pallas_call(
        functools.partial(_swa_kernel, blocks_per_step=R),
        grid=(B, S // tb),
        in_specs=[pl.BlockSpec(memory_space=pltpu.SMEM),
                  cur(SWA_Q_COLS), cur(kvw), cur(kvw), prev, prev, meta, meta],
        out_specs=cur(SWA_Q_COLS),
        out_shape=jax.ShapeDtypeStruct((B, S, SWA_Q_COLS), BF16),
        compiler_params=_cparams(("parallel", "parallel")),
        name="swa_attn",
    )(sinks, qa, ka, va, ka, va, ka_meta, va_meta)


def _mla_kernel(qt_ref, k_ref, vt_ref, kmeta_ref, vtmeta_ref, o_ref,
                sa_ref, sb_ref, sa_max_ref, sb_max_ref, m_ref, l_ref, acc_ref, *, tq):
    i = pl.program_id(2)
    qt = qt_ref[0, 0]

    s = _dot(kmeta_ref[0, 0], qt)
    m = jnp.max(s, axis=0, keepdims=True)
    p = jnp.exp2(s - m)
    m_ref[...] = m
    l_ref[...] = jnp.sum(p, axis=0, keepdims=True)
    acc_ref[...] = _dot(vtmeta_ref[0, 0, 0], p.astype(BF16))

    def scores_into(s_ref, smax_ref, j):
        start = pl.multiple_of(j * tq, tq)
        s = _dot(k_ref[0, 0, pl.ds(start, tq), :], qt)
        s_ref[...] = s
        smax_ref[...] = jnp.max(s, axis=0, keepdims=True)

    def absorb(s_ref, smax_ref, j, causal=False):
        s = s_ref[...]
        if causal:
            key = lax.broadcasted_iota(jnp.int32, (tq, tq), 0)
            query = lax.broadcasted_iota(jnp.int32, (tq, tq), 1)
            s = jnp.where(key <= query, s, NEG)
            smax = jnp.max(s, axis=0, keepdims=True)
        else:
            smax = smax_ref[...]
        m = m_ref[...]
        m_new = jnp.maximum(m, smax)
        alpha = jnp.exp2(m - m_new)
        p = jnp.exp2(s - m_new)
        m_ref[...] = m_new
        l_ref[...] = alpha * l_ref[...] + jnp.sum(p, axis=0, keepdims=True)
        acc_ref[...] = alpha * acc_ref[...] + _dot(vt_ref[0, 0, j], p.astype(BF16))

    scores_into(sa_ref, sa_max_ref, 0)

    def pair(t, carry):
        scores_into(sb_ref, sb_max_ref, 2 * t + 1)
        absorb(sa_ref, sa_max_ref, 2 * t)
        scores_into(sa_ref, sa_max_ref, 2 * t + 2)
        absorb(sb_ref, sb_max_ref, 2 * t + 1)
        return carry

    lax.fori_loop(0, i // 2, pair, 0)

    @pl.when(i % 2 == 1)
    def _():
        scores_into(sb_ref, sb_max_ref, i)
        absorb(sa_ref, sa_max_ref, i - 1)
        absorb(sb_ref, sb_max_ref, i, causal=True)

    @pl.when(i % 2 == 0)
    def _():
        absorb(sa_ref, sa_max_ref, i, causal=True)

    o_ref[0] = (acc_ref[...] * (1.0 / l_ref[...])).T.astype(BF16)


def _mla(qt, km, vt, km_meta, vt_meta, tq):
    B, H, _, S = qt.shape
    return pl.pallas_call(
        functools.partial(_mla_kernel, tq=tq),
        grid=(B, H, S // tq),
        in_specs=[
            pl.BlockSpec((1, 1, MLA_QK_PAD, tq), lambda b, h, i: (b, h, 0, i)),
            pl.BlockSpec((1, 1, S, MLA_QK_PAD), lambda b, h, i: (b, h, 0, 0)),
            pl.BlockSpec((1, 1, S // tq, MLA_V_DIM, tq), lambda b, h, i: (b, h, 0, 0, 0)),
            pl.BlockSpec((1, 1, N_META, MLA_QK_PAD), lambda b, h, i: (0, h, 0, 0)),
            pl.BlockSpec((1, 1, 1, MLA_V_DIM, N_META), lambda b, h, i: (0, h, 0, 0, 0)),
        ],
        out_specs=pl.BlockSpec((1, tq, MLA_V_DIM), lambda b, h, i: (b, i, h)),
        out_shape=jax.ShapeDtypeStruct((B, S, MLA_OUT_COLS), BF16),
        scratch_shapes=[pltpu.VMEM((tq, tq), F32), pltpu.VMEM((tq, tq), F32),
                        pltpu.VMEM((1, tq), F32), pltpu.VMEM((1, tq), F32),
                        pltpu.VMEM((1, tq), F32), pltpu.VMEM((1, tq), F32),
                        pltpu.VMEM((MLA_V_DIM, tq), F32)],
        compiler_params=_cparams(("parallel", "parallel", "arbitrary")),
        name="mla_attn",
    )(qt, km, vt, km_meta, vt_meta)


def _out_proj_kernel(x_ref, a_ref, b_ref, gin_ref, bin_ref, ga_ref, gb_ref, wo_ref, g1_ref, b1_ref,
                     wrt_ref, brt_ref, h1_ref, route_ref, counts_out_ref, count_ref):
    h = _layer_norm(x_ref[0], gin_ref[...], bin_ref[...])
    a = a_ref[0].astype(F32)
    b = b_ref[0].astype(F32)
    an = (a * _rms_scale(a) * ga_ref[...]).astype(BF16)
    bn = (b * _rms_scale(b) * gb_ref[...]).astype(BF16)
    mix = _dot(jnp.concatenate([an, bn], axis=-1), wo_ref[...])
    h1 = _layer_norm(ALPHA * h + mix, g1_ref[...], b1_ref[...])
    h1_ref[0] = h1

    x_hi = h1.astype(BF16)
    x_lo = (h1 - x_hi.astype(F32)).astype(BF16)
    hi_both = _dot(x_hi, wrt_ref[...])
    logits = (hi_both[:, :ROUTE_LANES] + hi_both[:, ROUTE_LANES:]
              + _dot(x_lo, wrt_ref[:, :ROUTE_LANES]) + brt_ref[...])
    lane = lax.broadcasted_iota(jnp.int32, logits.shape, 1).astype(F32)
    ninf = -jnp.inf
    no_lane = float(ROUTE_LANES)
    gl = jnp.where(lane < N_GROUPS, logits, ninf)
    gmax = jnp.max(gl, axis=-1, keepdims=True)
    g_top = 1.0 / jnp.sum(jnp.exp(gl - gmax), axis=-1, keepdims=True)
    g_idx = jnp.min(jnp.where(gl == gmax, lane, no_lane), axis=-1, keepdims=True)
    lo = EXPERT_LANE0 + g_idx * EXPERTS_PER_GROUP
    el = jnp.where((lane >= lo) & (lane < lo + EXPERTS_PER_GROUP), logits, ninf)
    m1 = jnp.max(el, axis=-1, keepdims=True)
    i1 = jnp.min(jnp.where(el == m1, lane, no_lane), axis=-1, keepdims=True)
    el2 = jnp.where(lane == i1, ninf, el)
    m2 = jnp.max(el2, axis=-1, keepdims=True)
    i2 = jnp.min(jnp.where(el2 == m2, lane, no_lane), axis=-1, keepdims=True)
    e2 = jnp.exp(m2 - m1)
    w1 = g_top / (1.0 + e2)
    w2 = g_top * e2 / (1.0 + e2)

    first_is_lo = i1 < i2
    a = jnp.minimum(i1, i2) - lo
    b = jnp.maximum(i1, i2) - lo
    cls = g_idx * PAIRS_PER_GROUP + a * (15.0 - a) * 0.5 + (b - a - 1.0)
    w_lo = jnp.where(first_is_lo, w1, w2)
    w_hi = jnp.where(first_is_lo, w2, w1)

    @pl.when((pl.program_id(0) == 0) & (pl.program_id(1) == 0))
    def _():
        count_ref[...] = jnp.zeros_like(count_ref)

    tm = logits.shape[0]
    onehot = jnp.where(lane == cls, 1.0, 0.0)
    r_i = lax.broadcasted_iota(jnp.int32, (BLOCK, BLOCK), 0)
    c_i = lax.broadcasted_iota(jnp.int32, (BLOCK, BLOCK), 1)
    earlier = jnp.where(r_i > c_i, 1.0, 0.0).astype(BF16)
    running = count_ref[...]
    before = []
    for blk in range(tm // BLOCK):
        oh = onehot[blk * BLOCK:(blk + 1) * BLOCK]
        before.append(_dot(earlier, oh.astype(BF16)) + running)
        running = running + jnp.sum(oh, axis=0, keepdims=True)
    before = jnp.concatenate(before, axis=0)
    rank = jnp.sum(jnp.where(lane == cls, before, 0.0), axis=-1, keepdims=True)
    count_ref[...] = running
    counts_out_ref[...] = running

    route = jnp.where(lane == ROUTE_CLS, cls, 0.0)
    route = jnp.where(lane == ROUTE_W_LO, w_lo, route)
    route = jnp.where(lane == ROUTE_W_HI, w_hi, route)
    route = jnp.where(lane == ROUTE_RANK, rank, route)
    route_ref[0] = route


def _out_proj(x, a_out, b_out, p, tm):
    B, S, D = x.shape
    full = lambda shape: pl.BlockSpec(shape, lambda b, i: (0,) * len(shape))
    row = lambda w: pl.BlockSpec((1, tm, w), lambda b, i: (b, i, 0))
    return pl.pallas_call(
        _out_proj_kernel,
        grid=(B, S // tm),
        in_specs=[row(D), row(SWA_Q_COLS), row(MLA_OUT_COLS),
                  full((1, D)), full((1, D)), full((1, SWA_Q_COLS)), full((1, MLA_OUT_COLS)),
                  full((D, D)), full((1, D)), full((1, D)),
                  full((D, 2 * ROUTE_LANES)), full((1, ROUTE_LANES))],
        out_specs=(row(D), row(ROUTE_LANES), full((1, ROUTE_LANES))),
        out_shape=(jax.ShapeDtypeStruct((B, S, D), F32),
                   jax.ShapeDtypeStruct((B, S, ROUTE_LANES), F32),
                   jax.ShapeDtypeStruct((1, ROUTE_LANES), F32)),
        scratch_shapes=[pltpu.VMEM((1, ROUTE_LANES), F32)],
        compiler_params=_cparams(("arbitrary", "arbitrary")),
        name="out_proj_route",
    )(x, a_out, b_out, p["ln_in_g"], p["ln_in_b"], p["ga"], p["gb"], p["w_o"], p["ln1_g"],
      p["ln1_b"], p["w_route"], p["b_route"])


SLAB = D_MODEL // LANES
PAIR_PITCH = 2 * SLAB


def _slab_copy(src_ref, src_tok8, dst_ref, dst_tok8, sem):
    return pltpu.make_async_copy(src_ref.at[pl.ds(pl.multiple_of(src_tok8, SLAB), SLAB)],
                                 dst_ref.at[pl.ds(pl.multiple_of(dst_tok8, SLAB), SLAB)], sem)


def _to_slabs(x, slab_ref, n):
    for s in range(SLAB):
        slab_ref[pl.ds(s, n, stride=SLAB), :] = x[:, s * LANES:(s + 1) * LANES]


def _from_slabs(slab_ref, n):
    return jnp.concatenate([slab_ref[pl.ds(s, n, stride=SLAB), :] for s in range(SLAB)], axis=1)


def _dispatch_kernel(padstart_ref, nu_ref, pos_ref, h1_ref, xs_ref, rows_ref, zero_ref, sem, *,
                     tm, n_tiles):
    tail = zero_ref.shape[0]

    def zero_fill(start):
        return pltpu.make_async_copy(
            zero_ref, xs_ref.at[pl.ds(pl.multiple_of(start, SLAB), tail)], sem)

    @pl.when(pl.program_id(0) == 0)
    def _():
        zero_ref[...] = jnp.zeros_like(zero_ref)
        for parity in range(2):
            tails = [zero_fill(padstart_ref[c]) for c in range(parity, N_CLASSES, 2)]
            for t in tails:
                t.start()
            for t in tails:
                t.wait()

        def fill_unused(j, carry):
            fill = zero_fill(j * tail)
            fill.start()
            fill.wait()
            return carry

        lax.fori_loop(nu_ref[0], n_tiles + 1, fill_unused, 0)

    _to_slabs(h1_ref[...], rows_ref, tm)

    def issue(r0, carry):
        for k in range(DMA_BURST):
            r = r0 * DMA_BURST + k
            _slab_copy(rows_ref, r * SLAB, xs_ref, pos_ref[0, 0, r], sem).start(priority=k % 2)
        return carry

    lax.fori_loop(0, tm // DMA_BURST, issue, 0)
    pltpu.make_async_copy(rows_ref, xs_ref.at[pl.ds(0, tm * SLAB)], sem).wait()


def _dispatch(h1, pos8, pad_start8, n_used, n_tiles, tm):
    T, D = h1.shape
    pos8 = pos8.reshape(T // tm, 1, tm)
    n_rows = (n_tiles + 1) * MOE_TILE_ROWS
    grid_spec = pltpu.PrefetchScalarGridSpec(
        num_scalar_prefetch=2,
        grid=(T // tm,),
        in_specs=[pl.BlockSpec((1, 1, tm), lambda t, ps, nu: (t, 0, 0),
                               memory_space=pltpu.SMEM),
                  pl.BlockSpec((tm, D), lambda t, ps, nu: (t, 0))],
        out_specs=pl.BlockSpec(memory_space=pl.ANY),
        scratch_shapes=[pltpu.VMEM((tm * SLAB, LANES), F32),
                        pltpu.VMEM((MOE_TILE_ROWS * SLAB, LANES), F32),
                        pltpu.SemaphoreType.DMA],
    )
    return pl.pallas_call(
        functools.partial(_dispatch_kernel, tm=tm, n_tiles=n_tiles),
        grid_spec=grid_spec,
        out_shape=jax.ShapeDtypeStruct((n_rows * SLAB, LANES), F32),
        compiler_params=_cparams(("arbitrary",)),
        name="moe_dispatch",
    )(pad_start8, n_used, pos8, h1)


def _experts_kernel(tlo_ref, thi_ref, nu_ref, x_ref, wgu_lo_ref, wgu_hi_ref, wd_lo_ref, wd_hi_ref,
                    y_ref):
    j = pl.program_id(0)
    tr = MOE_TILE_ROWS

    @pl.when(j < nu_ref[0])
    def _():
        for s in range(TOP_K * SLAB, PAIR_PITCH):
            y_ref[pl.ds(s, tr, stride=PAIR_PITCH), :] = jnp.zeros((tr, LANES), F32)
        x = _from_slabs(x_ref, tr).astype(BF16)
        for k, (wgu_ref, wd_ref) in enumerate(((wgu_lo_ref, wd_lo_ref), (wgu_hi_ref, wd_hi_ref))):
            gu = _dot(x, wgu_ref[0])
            g = gu[:, :D_EXPERT]
            u = gu[:, D_EXPERT:]
            hid = g * (1.0 / (1.0 + jnp.exp(-g))) * u
            y = _dot(hid.astype(BF16), wd_ref[0])
            for s in range(SLAB):
                y_ref[pl.ds(k * SLAB + s, tr, stride=PAIR_PITCH), :] = (
                    y[:, s * LANES:(s + 1) * LANES])

    @pl.when(j >= nu_ref[0])
    def _():
        y_ref[...] = jnp.zeros_like(y_ref)


def _experts(xs, tile_lo, tile_hi, n_used, p, n_tiles):
    D = D_MODEL
    tr = MOE_TILE_ROWS
    last_used = lambda j, nu: jnp.minimum(j, nu[0] - 1)
    wgu = lambda pick: pl.BlockSpec((1, D, 2 * D_EXPERT),
                                    lambda j, tlo, thi, nu: (pick(tlo, thi)[j], 0, 0))
    wd = lambda pick: pl.BlockSpec((1, D_EXPERT, D),
                                   lambda j, tlo, thi, nu: (pick(tlo, thi)[j], 0, 0))
    lo = lambda tlo, thi: tlo
    hi = lambda tlo, thi: thi
    grid_spec = pltpu.PrefetchScalarGridSpec(
        num_scalar_prefetch=3,
        grid=(n_tiles,),
        in_specs=[pl.BlockSpec((tr * SLAB, LANES), lambda j, tlo, thi, nu: (last_used(j, nu), 0)),
                  wgu(lo), wgu(hi), wd(lo), wd(hi)],
        out_specs=pl.BlockSpec((tr * PAIR_PITCH, LANES), lambda j, tlo, thi, nu: (j, 0)),
    )
    return pl.pallas_call(
        _experts_kernel,
        grid_spec=grid_spec,
        out_shape=jax.ShapeDtypeStruct((n_tiles * tr * PAIR_PITCH, LANES), F32),
        compiler_params=_cparams(("arbitrary",)),
        name="moe_experts",
    )(tile_lo, tile_hi, n_used, xs, p["w_gu"], p["w_gu"], p["w_down"], p["w_down"])


def _combine_kernel(pos_ref, pos_next_ref, h1_ref, route_ref, ys_ref, g2_ref, b2_ref, o_ref,
                    ybuf_ref, sem, *, tm):
    pair = TOP_K * SLAB
    t = pl.program_id(0)
    slot = t % 2

    def gather(idx_ref, dst_slot):
        def issue(r0, carry):
            for k in range(DMA_BURST):
                r = r0 * DMA_BURST + k
                src = pl.multiple_of(idx_ref[0, 0, r] * (PAIR_PITCH // SLAB), SLAB)
                pltpu.make_async_copy(
                    ys_ref.at[pl.ds(src, pair)],
                    ybuf_ref.at[dst_slot, pl.ds(pl.multiple_of(r * pair, pair), pair)],
                    sem.at[dst_slot]).start(priority=k % 2)
            return carry

        lax.fori_loop(0, tm // DMA_BURST, issue, 0)

    @pl.when(t == 0)
    def _():
        gather(pos_ref, 0)

    @pl.when(t + 1 < pl.num_programs(0))
    def _():
        gather(pos_next_ref, 1 - slot)

    pltpu.make_async_copy(ys_ref.at[pl.ds(0, tm * pair)], ybuf_ref.at[slot], sem.at[slot]).wait()

    def expert_out(k):
        return jnp.concatenate([ybuf_ref[slot, pl.ds(k * SLAB + s, tm, stride=pair), :]
                                for s in range(SLAB)], axis=1)

    route = route_ref[...]
    w_lo = route[:, ROUTE_W_LO:ROUTE_W_LO + 1]
    w_hi = route[:, ROUTE_W_HI:ROUTE_W_HI + 1]
    ffn = w_lo * expert_out(0) + w_hi * expert_out(1)
    o_ref[...] = _layer_norm(ALPHA * h1_ref[...] + ffn, g2_ref[...], b2_ref[...])


def _combine(h1, route, pos8, ys, p, tm):
    T, D = h1.shape
    n_steps = T // tm
    pos8 = pos8.reshape(n_steps, 1, tm)
    full = lambda shape: pl.BlockSpec(shape, lambda t: (0,) * len(shape))
    return pl.pallas_call(
        functools.partial(_combine_kernel, tm=tm),
        grid=(n_steps,),
        in_specs=[pl.BlockSpec((1, 1, tm), lambda t: (t, 0, 0), memory_space=pltpu.SMEM),
                  pl.BlockSpec((1, 1, tm), lambda t: (jnp.minimum(t + 1, n_steps - 1), 0, 0),
                               memory_space=pltpu.SMEM),
                  pl.BlockSpec((tm, D), lambda t: (t, 0)),
                  pl.BlockSpec((tm, ROUTE_LANES), lambda t: (t, 0)),
                  pl.BlockSpec(memory_space=pl.ANY),
                  full((1, D)), full((1, D))],
        out_specs=pl.BlockSpec((tm, D), lambda t: (t, 0)),
        out_shape=jax.ShapeDtypeStruct((T, D), F32),
        scratch_shapes=[pltpu.VMEM((2, tm * TOP_K * SLAB, LANES), F32),
                        pltpu.SemaphoreType.DMA((2,))],
        compiler_params=_cparams(("arbitrary",)),
        name="moe_combine_ln2",
    )(pos8, pos8, h1, route, ys, p["ln2_g"], p["ln2_b"])


def _dispatch_plan(route, counts):
    T = route.shape[0]
    tr = MOE_TILE_ROWS
    n_tiles = T // tr + N_CLASSES
    cnt = counts[0, :N_CLASSES].astype(jnp.int32)
    padded = jnp.maximum((cnt + tr - 1) // tr, 1) * tr
    seg_end = jnp.cumsum(padded)
    seg_start = seg_end - padded
    cls = route[:, ROUTE_CLS].astype(jnp.int32)
    rank = route[:, ROUTE_RANK].astype(jnp.int32)
    classes = jnp.arange(N_CLASSES, dtype=jnp.int32)
    start_of = jnp.sum(jnp.where(cls[:, None] == classes, seg_start, 0), axis=-1)
    pos8 = (start_of + rank) * SLAB
    tile_row0 = jnp.arange(n_tiles, dtype=jnp.int32) * tr
    tile_cls = jnp.sum((seg_end[None, :] <= tile_row0[:, None]).astype(jnp.int32), axis=-1)
    tile_cls = jnp.minimum(tile_cls, N_CLASSES - 1)
    pairs = [(a, b) for a in range(EXPERTS_PER_GROUP) for b in range(a + 1, EXPERTS_PER_GROUP)]
    lo_of = jnp.asarray([g * EXPERTS_PER_GROUP + a for g in range(N_GROUPS) for a, _ in pairs],
                        jnp.int32)
    hi_of = jnp.asarray([g * EXPERTS_PER_GROUP + b for g in range(N_GROUPS) for _, b in pairs],
                        jnp.int32)
    is_cls = tile_cls[:, None] == classes
    tile_lo = jnp.sum(jnp.where(is_cls, lo_of, 0), axis=-1)
    tile_hi = jnp.sum(jnp.where(is_cls, hi_of, 0), axis=-1)
    n_used = (seg_end[-1:] // tr).astype(jnp.int32)
    return pos8, (seg_start + cnt) * SLAB, tile_lo, tile_hi, n_used, n_tiles


def _rope_tables(pos):
    d = SWA_HEAD_DIM
    inv_freq = ROPE_THETA ** (-jnp.arange(0, d, 2, dtype=F32) / d)
    ang = pos[:, None] * inv_freq[None, :]
    c, s = jnp.cos(ang), jnp.sin(ang)
    return (jnp.concatenate([c, c, c, c], axis=-1), jnp.concatenate([-s, s, -s, s], axis=-1),
            jnp.concatenate([c, c], axis=-1).T, jnp.concatenate([s, s], axis=-1).T)


def _prep_params(ln_in_g, ln_in_b, w_in, swa_sinks, mla_q_norm_g, mla_w_uq, mla_kv_norm_g, mla_w_ukv,
                 swa_out_norm_g, mla_out_norm_g, w_o, ln1_g, ln1_b, moe_w_group, moe_b_group,
                 moe_w_router, moe_b_router, moe_w_gate, moe_w_up, moe_w_down, ln2_g, ln2_b):
    order = jnp.asarray(SWA_HEAD_ORDER)
    perm = (order[:, None] * SWA_HEAD_DIM + jnp.arange(SWA_HEAD_DIM)[None, :]).reshape(-1)
    w = w_in[0]
    w_q = w[:, :SWA_Q_COLS][:, perm]
    w_pad = jnp.zeros((D_MODEL, IN_COLS_PADDED - w.shape[1]), w.dtype)
    w_in_p = jnp.concatenate([w_q, w[:, SWA_Q_COLS:], w_pad], axis=1).astype(BF16)

    dqk = MLA_NOPE_DIM + MLA_ROPE_DIM
    wuq = mla_w_uq[0].reshape(MLA_Q_RANK, MLA_HEADS, dqk)
    wuq = jnp.pad(wuq, ((0, 0), (0, 0), (0, MLA_QK_PAD - dqk)))
    wuq_t = jnp.transpose(wuq, (1, 2, 0)).astype(BF16)
    wukv = mla_w_ukv[0].reshape(MLA_KV_RANK, MLA_HEADS, MLA_NOPE_DIM + MLA_V_DIM)
    wuk = wukv[:, :, :MLA_NOPE_DIM].reshape(MLA_KV_RANK, MLA_HEADS * MLA_NOPE_DIM).astype(BF16)
    wuv_t = jnp.transpose(wukv[:, :, MLA_NOPE_DIM:], (1, 2, 0)).astype(BF16)

    w_o0 = w_o[0]
    w_o_p = jnp.concatenate([w_o0[:SWA_Q_COLS][perm], w_o0[SWA_Q_COLS:]], axis=0).astype(BF16)

    w_route = jnp.zeros((D_MODEL, ROUTE_LANES), F32)
    w_route = w_route.at[:, :N_GROUPS].set(moe_w_group[0])
    w_route = w_route.at[:, EXPERT_LANE0:EXPERT_LANE0 + N_EXPERTS].set(moe_w_router[0])
    w_route_hi = w_route.astype(BF16)
    w_route_lo = (w_route - w_route_hi.astype(F32)).astype(BF16)
    w_route = jnp.concatenate([w_route_hi, w_route_lo], axis=1)
    b_route = jnp.zeros((1, ROUTE_LANES), F32)
    b_route = b_route.at[0, :N_GROUPS].set(moe_b_group[0])
    b_route = b_route.at[0, EXPERT_LANE0:EXPERT_LANE0 + N_EXPERTS].set(moe_b_router[0])

    return {
        "ln_in_g": ln_in_g[None, :], "ln_in_b": ln_in_b[None, :],
        "w_in": w_in_p,
        "sinks": swa_sinks[0],
        "gq": mla_q_norm_g, "w_uq_t": wuq_t,
        "gkv": mla_kv_norm_g, "w_uk": wuk, "w_uv_t": wuv_t,
        "ga": swa_out_norm_g[0][perm][None, :], "gb": mla_out_norm_g,
        "w_o": w_o_p, "ln1_g": ln1_g, "ln1_b": ln1_b,
        "w_route": w_route, "b_route": b_route,
        "w_gu": jnp.concatenate([moe_w_gate[0], moe_w_up[0]], axis=-1).astype(BF16),
        "w_down": moe_w_down[0].astype(BF16),
        "ln2_g": ln2_g, "ln2_b": ln2_b,
    }


def kernel(x, meta_tokens, ln_in_g, ln_in_b, w_in, swa_sinks, mla_q_norm_g, mla_w_uq, mla_kv_norm_g, mla_w_ukv, swa_out_norm_g, mla_out_norm_g, w_o, ln1_g, ln1_b, moe_w_group, moe_b_group, moe_w_router, moe_b_router, moe_w_gate, moe_w_up, moe_w_down, ln2_g, ln2_b):
    B, S, D = x.shape
    p = _prep_params(ln_in_g, ln_in_b, w_in, swa_sinks, mla_q_norm_g, mla_w_uq, mla_kv_norm_g,
                     mla_w_ukv, swa_out_norm_g, mla_out_norm_g, w_o, ln1_g, ln1_b, moe_w_group,
                     moe_b_group, moe_w_router, moe_b_router, moe_w_gate, moe_w_up, moe_w_down,
                     ln2_g, ln2_b)
    T = B * S
    tm = min(PROJ_TILE, S)

    x_meta = jnp.concatenate([jnp.zeros((N_PAD, D), x.dtype), meta_tokens.astype(x.dtype)])[None]
    pos_meta = jnp.maximum(jnp.arange(BLOCK) - N_PAD, 0).astype(F32)
    pos_tok = (jnp.arange(S) + N_META).astype(F32)
    _, ka_m, va_m, _, km_m, vt_m = _in_proj(x_meta, _rope_tables(pos_meta), p, BLOCK)
    qa, ka, va, qt, km, vt = _in_proj(x, _rope_tables(pos_tok), p, tm)

    a_out = _swa(p["sinks"], qa, ka, va, ka_m, va_m, min(WIDE_TILE, S))
    b_out = _mla(qt, km, vt, km_m[:, :, N_PAD:, :], vt_m[..., N_PAD:], tm)
    h1, route, counts = _out_proj(x, a_out, b_out, p, min(WIDE_TILE, S))
    h1 = h1.reshape(B * S, D)
    route = route.reshape(B * S, ROUTE_LANES)

    pos8, pad_start8, tile_lo, tile_hi, n_used, n_tiles = _dispatch_plan(route, counts)
    xs = _dispatch(h1, pos8, pad_start8, n_used, n_tiles, min(DISPATCH_TILE, T))
    ys = _experts(xs, tile_lo, tile_hi, n_used, p, n_tiles)
    out = _combine(h1, route, pos8, ys, p, min(COMBINE_TILE, T))
    return out.reshape(B, S, D)
```

```python
import functools

import jax
import jax.numpy as jnp
from jax import lax
from jax.experimental import pallas as pl
from jax.experimental.pallas import tpu as pltpu

D_MODEL = 1024
N_META = 16
BLOCK = 128
N_PAD = BLOCK - N_META
ROPE_THETA = 10000.0
SWA_HEADS = 8
SWA_KV_HEADS = 2
SWA_HEAD_DIM = 64
MLA_HEADS = 4
MLA_Q_RANK = 256
MLA_KV_RANK = 256
MLA_NOPE_DIM = 128
MLA_ROPE_DIM = 64
MLA_V_DIM = 128
SWA_Q_COLS = SWA_HEADS * SWA_HEAD_DIM
SWA_KV_COLS = SWA_KV_HEADS * SWA_HEAD_DIM
MLA_OUT_COLS = MLA_HEADS * MLA_V_DIM
N_GROUPS = 4
EXPERTS_PER_GROUP = 8
N_EXPERTS = N_GROUPS * EXPERTS_PER_GROUP
D_EXPERT = 256
LN_EPS = 1e-5
RMS_EPS = 1e-6
DEPTH = 1
ALPHA = (2.0 * DEPTH) ** 0.25
NEG = -1e30
LOG2_E = 1.4426950408889634

LANES = 128
IN_COLS_PADDED = 1408
MLA_QK_PAD = 256
ROUTE_LANES = 128
EXPERT_LANE0 = 32
ROUTE_CLS, ROUTE_W_LO, ROUTE_W_HI, ROUTE_RANK = range(4)
TOP_K = 2
PAIRS_PER_GROUP = EXPERTS_PER_GROUP * (EXPERTS_PER_GROUP - 1) // 2
N_CLASSES = N_GROUPS * PAIRS_PER_GROUP
MOE_TILE_ROWS = 256
PROJ_TILE = 1024
MLA_KEY_TILE = 512
WIDE_TILE = 1024
DISPATCH_TILE = 2048
COMBINE_TILE = 1024
DMA_BURST = 8
VMEM_LIMIT = 56 * 1024 * 1024

SWA_HEAD_ORDER = (0, 4, 1, 5, 2, 6, 3, 7)

BF16 = jnp.bfloat16
F32 = jnp.float32


def _cparams(sem):
    return pltpu.CompilerParams(dimension_semantics=sem, vmem_limit_bytes=VMEM_LIMIT)


def _layer_norm(x, g, b):
    mu = jnp.mean(x, axis=-1, keepdims=True)
    xc = x - mu
    var = jnp.mean(xc * xc, axis=-1, keepdims=True)
    return xc * lax.rsqrt(var + LN_EPS) * g + b


def _rms_scale(x):
    return lax.rsqrt(jnp.mean(x * x, axis=-1, keepdims=True) + RMS_EPS)


def _rope128(x, cos, sin_signed, lane_lo):
    up = pltpu.roll(x, LANES - 32, 1)
    dn = pltpu.roll(x, 32, 1)
    return x * cos + jnp.where(lane_lo, up, dn) * sin_signed


def _dot(a, b):
    return jnp.dot(a, b, preferred_element_type=F32)


def _dot_nt(a, b):
    return lax.dot_general(a, b, (((1,), (1,)), ((), ())), preferred_element_type=F32)


def _in_proj_kernel(x_ref, g_ref, b_ref, w_ref, cos_ref, sin_ref, cost_ref, sint_ref, gq_ref,
                    wuqt_ref, gkv_ref, wuk_ref, wuvt_ref,
                    qa_ref, ka_ref, va_ref, qt_ref, km_ref, vt_ref):
    h = _layer_norm(x_ref[0], g_ref[...], b_ref[...])
    u = _dot(h.astype(BF16), w_ref[...])
    cos = cos_ref[...]
    sin = sin_ref[...]
    lane = lax.broadcasted_iota(jnp.int32, (1, LANES), 1)
    lane_lo = (lane % 64) < 32
    rope = functools.partial(_rope128, cos=cos, sin_signed=sin, lane_lo=lane_lo)

    swa_scale = SWA_HEAD_DIM ** -0.5
    for c in range(SWA_Q_COLS // LANES):
        qc = rope(u[:, c * LANES:(c + 1) * LANES]) * swa_scale
        qa_ref[0, :, c * LANES:(c + 1) * LANES] = qc.astype(BF16)
    ka_ref[0] = rope(u[:, 512:640]).astype(BF16)
    va_ref[0] = u[:, 640:768].astype(BF16)

    cq = u[:, 768:1024]
    cqn = (cq * _rms_scale(cq) * gq_ref[...]).astype(BF16)
    ckv = u[:, 1024:1280]
    ckvn = (ckv * _rms_scale(ckv) * gkv_ref[...]).astype(BF16)
    k_nope = _dot(ckvn, wuk_ref[...])
    kr = rope(u[:, 1280:1408]).astype(BF16)
    q_scale = (MLA_NOPE_DIM + MLA_ROPE_DIM) ** -0.5 * LOG2_E
    cost = cost_ref[...]
    sint = sint_ref[...]
    half = MLA_ROPE_DIM // 2
    for hd in range(MLA_HEADS):
        qt = _dot_nt(wuqt_ref[hd], cqn)
        qr = qt[MLA_NOPE_DIM:MLA_NOPE_DIM + MLA_ROPE_DIM]
        rot = jnp.concatenate([-qr[half:], qr[:half]], axis=0)
        qt_ref[0, hd, 0:MLA_NOPE_DIM, :] = (qt[:MLA_NOPE_DIM] * q_scale).astype(BF16)
        qt_ref[0, hd, MLA_NOPE_DIM:MLA_NOPE_DIM + MLA_ROPE_DIM, :] = (
            (qr * cost + rot * sint) * q_scale).astype(BF16)
        qt_ref[0, hd, MLA_NOPE_DIM + MLA_ROPE_DIM:, :] = (
            qt[MLA_NOPE_DIM + MLA_ROPE_DIM:]).astype(BF16)
        km_ref[0, hd, :, 0:LANES] = k_nope[:, hd * LANES:(hd + 1) * LANES].astype(BF16)
        km_ref[0, hd, :, LANES:2 * LANES] = kr
        v_t = _dot_nt(wuvt_ref[hd], ckvn).astype(BF16)
        kt = vt_ref.shape[-1]
        for part in range(vt_ref.shape[2]):
            vt_ref[0, hd, part] = v_t[:, part * kt:(part + 1) * kt]


def _in_proj(x, tables, p, tm):
    cos, sin, cos_t, sin_t = tables
    kt = min(MLA_KEY_TILE, tm)
    B, S, D = x.shape
    grid = (B, S // tm)
    full = lambda shape: pl.BlockSpec(shape, lambda b, i: (0,) * len(shape))
    out_shape = (
        jax.ShapeDtypeStruct((B, S, SWA_Q_COLS), BF16),
        jax.ShapeDtypeStruct((B, S, SWA_KV_COLS), BF16),
        jax.ShapeDtypeStruct((B, S, SWA_KV_COLS), BF16),
        jax.ShapeDtypeStruct((B, MLA_HEADS, MLA_QK_PAD, S), BF16),
        jax.ShapeDtypeStruct((B, MLA_HEADS, S, MLA_QK_PAD), BF16),
        jax.ShapeDtypeStruct((B, MLA_HEADS, S // kt, MLA_V_DIM, kt), BF16),
    )
    row = lambda w: pl.BlockSpec((1, tm, w), lambda b, i: (b, i, 0))
    head = lambda w: pl.BlockSpec((1, MLA_HEADS, tm, w), lambda b, i: (b, 0, i, 0))
    return pl.pallas_call(
        _in_proj_kernel,
        grid=grid,
        in_specs=[
            row(D),
            full((1, D)), full((1, D)),
            full((D, IN_COLS_PADDED)),
            pl.BlockSpec((tm, LANES), lambda b, i: (i, 0)),
            pl.BlockSpec((tm, LANES), lambda b, i: (i, 0)),
            pl.BlockSpec((MLA_ROPE_DIM, tm), lambda b, i: (0, i)),
            pl.BlockSpec((MLA_ROPE_DIM, tm), lambda b, i: (0, i)),
            full((1, MLA_Q_RANK)), full((MLA_HEADS, MLA_QK_PAD, MLA_Q_RANK)),
            full((1, MLA_KV_RANK)), full((MLA_KV_RANK, MLA_HEADS * MLA_NOPE_DIM)),
            full((MLA_HEADS, MLA_V_DIM, MLA_KV_RANK)),
        ],
        out_specs=(row(SWA_Q_COLS), row(SWA_KV_COLS), row(SWA_KV_COLS),
                   pl.BlockSpec((1, MLA_HEADS, MLA_QK_PAD, tm), lambda b, i: (b, 0, 0, i)),
                   head(MLA_QK_PAD),
                   pl.BlockSpec((1, MLA_HEADS, tm // kt, MLA_V_DIM, kt),
                                lambda b, i: (b, 0, i, 0, 0))),
        out_shape=out_shape,
        compiler_params=_cparams(("parallel", "parallel")),
        name="in_proj",
    )(x, p["ln_in_g"], p["ln_in_b"], p["w_in"], cos, sin, cos_t, sin_t, p["gq"], p["w_uq_t"],
      p["gkv"], p["w_uk"], p["w_uv_t"])


def _swa_kernel(sink_ref, qa_ref, kc_ref, vc_ref, kp_ref, vp_ref, kmeta_ref, vmeta_ref, o_ref,
                *, blocks_per_step):
    i = pl.program_id(1)
    first = i == 0
    kprev = jnp.where(first, kmeta_ref[0], kp_ref[0])
    vprev = jnp.where(first, vmeta_ref[0], vp_ref[0])

    row = lax.broadcasted_iota(jnp.int32, (2 * BLOCK, 2 * BLOCK), 0) % BLOCK
    col = lax.broadcasted_iota(jnp.int32, (2 * BLOCK, 2 * BLOCK), 1)
    base_mask = ((col < BLOCK) & (col > row)) | ((col >= BLOCK) & ((col - BLOCK) <= row))
    first_mask = base_mask & (col >= jnp.where(first, N_PAD, 0))
    lane = lax.broadcasted_iota(jnp.int32, (1, LANES), 1)
    lane_kv0 = lane < SWA_HEAD_DIM
    top_rows = lax.broadcasted_iota(jnp.int32, (2 * BLOCK, 1), 0) < BLOCK

    for r in range(blocks_per_step):
        rows = slice(r * BLOCK, (r + 1) * BLOCK)
        if r == 0:
            kp, vp = kprev, vprev
            mask = first_mask
        else:
            prev_rows = slice((r - 1) * BLOCK, r * BLOCK)
            kp, vp = kc_ref[0, prev_rows, :], vc_ref[0, prev_rows, :]
            mask = base_mask
        keys = jnp.concatenate([kp, kc_ref[0, rows, :]], axis=0)
        vals = jnp.concatenate([vp, vc_ref[0, rows, :]], axis=0)
        for c in range(SWA_Q_COLS // LANES):
            qc = qa_ref[0, rows, c * LANES:(c + 1) * LANES]
            zero = jnp.zeros_like(qc)
            q2 = jnp.concatenate([jnp.where(lane_kv0, qc, zero),
                                  jnp.where(lane_kv0, zero, qc)], axis=0)
            s = jnp.where(mask, _dot_nt(q2, keys), NEG)
            sink = jnp.where(top_rows, sink_ref[c], sink_ref[c + 4])
            m = jnp.maximum(jnp.max(s, axis=-1, keepdims=True), sink)
            e = jnp.exp(s - m)
            denom = jnp.sum(e, axis=-1, keepdims=True) + jnp.exp(sink - m)
            pr = (e * (1.0 / denom)).astype(BF16)
            o2 = _dot(pr, vals)
            o = jnp.where(lane_kv0, o2[:BLOCK], o2[BLOCK:])
            o_ref[0, rows, c * LANES:(c + 1) * LANES] = o.astype(BF16)


def _swa(sinks, qa, ka, va, ka_meta, va_meta, tb):
    B, S, _ = qa.shape
    R = tb // BLOCK
    kvw = SWA_KV_COLS
    cur = lambda w: pl.BlockSpec((1, tb, w), lambda b, i: (b, i, 0))
    prev = pl.BlockSpec((1, BLOCK, kvw), lambda b, i: (b, jnp.maximum(i * R - 1, 0), 0))
    meta = pl.BlockSpec((1, BLOCK, kvw), lambda b, i: (0, 0, 0))
    return pl.pallas_call(
        functools.partial(_swa_kernel, blocks_per_step=R),
        grid=(B, S // tb),
        in_specs=[pl.BlockSpec(memory_space=pltpu.SMEM),
                  cur(SWA_Q_COLS), cur(kvw), cur(kvw), prev, prev, meta, meta],
        out_specs=cur(SWA_Q_COLS),
        out_shape=jax.ShapeDtypeStruct((B, S, SWA_Q_COLS), BF16),
        compiler_params=_cparams(("parallel", "parallel")),
        name="swa_attn",
    )(sinks, qa, ka, va, ka, va, ka_meta, va_meta)


def _mla_kernel(qt_ref, k_ref, vt_ref, kmeta_ref, vtmeta_ref, o_ref,
                sa_ref, sb_ref, sa_max_ref, sb_max_ref, m_ref, l_ref, acc_ref, *, tq, tk):
    i = pl.program_id(2)
    qt = qt_ref[0, 0]

    s = _dot(kmeta_ref[0, 0], qt)
    m = jnp.max(s, axis=0, keepdims=True)
    p = jnp.exp2(s - m)
    m_ref[...] = m
    l_ref[...] = jnp.sum(p, axis=0, keepdims=True)
    acc_ref[...] = _dot(vtmeta_ref[0, 0, 0], p.astype(BF16))

    def key_tile(j):
        return k_ref[0, 0, pl.ds(pl.multiple_of(j * tk, tk), tk), :]

    def scores_into(s_ref, smax_ref, j):
        s = _dot(key_tile(j), qt)
        s_ref[...] = s
        smax_ref[...] = jnp.max(s, axis=0, keepdims=True)

    def update(s, smax, j):
        m = m_ref[...]
        m_new = jnp.maximum(m, smax)
        alpha = jnp.exp2(m - m_new)
        p = jnp.exp2(s - m_new)
        m_ref[...] = m_new
        l_ref[...] = alpha * l_ref[...] + jnp.sum(p, axis=0, keepdims=True)
        acc_ref[...] = alpha * acc_ref[...] + _dot(vt_ref[0, 0, j], p.astype(BF16))

    def absorb(s_ref, smax_ref, j):
        update(s_ref[...], smax_ref[...], j)

    scores_into(sa_ref, sa_max_ref, 0)

    def pair(t):
        scores_into(sb_ref, sb_max_ref, 2 * t + 1)
        absorb(sa_ref, sa_max_ref, 2 * t)
        scores_into(sa_ref, sa_max_ref, 2 * t + 2)
        absorb(sb_ref, sb_max_ref, 2 * t + 1)

    def two_pairs(u, carry):
        pair(2 * u)
        pair(2 * u + 1)
        return carry

    lax.fori_loop(0, i // 2, two_pairs, 0)

    @pl.when(i % 2 == 1)
    def _():
        pair(i - 1)

    scores_into(sb_ref, sb_max_ref, 2 * i + 1)
    key = lax.broadcasted_iota(jnp.int32, (tk, tq), 0)
    query = lax.broadcasted_iota(jnp.int32, (tk, tq), 1)
    for half, s_ref in enumerate((sa_ref, sb_ref)):
        s = jnp.where(key + half * tk <= query, s_ref[...], NEG)
        update(s, jnp.max(s, axis=0, keepdims=True), 2 * i + half)

    o_ref[0] = (acc_ref[...] * (1.0 / l_ref[...])).T.astype(BF16)


def _mla(qt, km, vt, km_meta, vt_meta, tq):
    B, H, _, S = qt.shape
    tk = vt.shape[-1]
    assert tq == 2 * tk, "the diagonal block is handled as exactly two key tiles"
    return pl.pallas_call(
        functools.partial(_mla_kernel, tq=tq, tk=tk),
        grid=(B, H, S // tq),
        in_specs=[
            pl.BlockSpec((1, 1, MLA_QK_PAD, tq), lambda b, h, i: (b, h, 0, i)),
            pl.BlockSpec((1, 1, S, MLA_QK_PAD), lambda b, h, i: (b, h, 0, 0)),
            pl.BlockSpec((1, 1, S // tk, MLA_V_DIM, tk), lambda b, h, i: (b, h, 0, 0, 0)),
            pl.BlockSpec((1, 1, N_META, MLA_QK_PAD), lambda b, h, i: (0, h, 0, 0)),
            pl.BlockSpec((1, 1, 1, MLA_V_DIM, N_META), lambda b, h, i: (0, h, 0, 0, 0)),
        ],
        out_specs=pl.BlockSpec((1, tq, MLA_V_DIM), lambda b, h, i: (b, i, h)),
        out_shape=jax.ShapeDtypeStruct((B, S, MLA_OUT_COLS), BF16),
        scratch_shapes=[pltpu.VMEM((tk, tq), F32), pltpu.VMEM((tk, tq), F32),
                        pltpu.VMEM((1, tq), F32), pltpu.VMEM((1, tq), F32),
                        pltpu.VMEM((1, tq), F32), pltpu.VMEM((1, tq), F32),
                        pltpu.VMEM((MLA_V_DIM, tq), F32)],
        compiler_params=_cparams(("parallel", "parallel", "arbitrary")),
        name="mla_attn",
    )(qt, km, vt, km_meta, vt_meta)


def _out_proj_kernel(x_ref, a_ref, b_ref, gin_ref, bin_ref, ga_ref, gb_ref, wo_ref, g1_ref, b1_ref,
                     wrt_ref, brt_ref, h1_ref, route_ref, counts_out_ref, count_ref):
    h = _layer_norm(x_ref[0], gin_ref[...], bin_ref[...])
    a = a_ref[0].astype(F32)
    b = b_ref[0].astype(F32)
    an = (a * _rms_scale(a) * ga_ref[...]).astype(BF16)
    bn = (b * _rms_scale(b) * gb_ref[...]).astype(BF16)
    mix = _dot(jnp.concatenate([an, bn], axis=-1), wo_ref[...])
    h1 = _layer_norm(ALPHA * h + mix, g1_ref[...], b1_ref[...])
    h1_ref[0] = h1

    x_hi = h1.astype(BF16)
    x_lo = (h1 - x_hi.astype(F32)).astype(BF16)
    hi_both = _dot(x_hi, wrt_ref[...])
    logits = (hi_both[:, :ROUTE_LANES] + hi_both[:, ROUTE_LANES:]
              + _dot(x_lo, wrt_ref[:, :ROUTE_LANES]) + brt_ref[...])
    lane = lax.broadcasted_iota(jnp.int32, logits.shape, 1).astype(F32)
    ninf = -jnp.inf
    no_lane = float(ROUTE_LANES)
    gl = jnp.where(lane < N_GROUPS, logits, ninf)
    gmax = jnp.max(gl, axis=-1, keepdims=True)
    g_top = 1.0 / jnp.sum(jnp.exp(gl - gmax), axis=-1, keepdims=True)
    g_idx = jnp.min(jnp.where(gl == gmax, lane, no_lane), axis=-1, keepdims=True)
    lo = EXPERT_LANE0 + g_idx * EXPERTS_PER_GROUP
    el = jnp.where((lane >= lo) & (lane < lo + EXPERTS_PER_GROUP), logits, ninf)
    m1 = jnp.max(el, axis=-1, keepdims=True)
    i1 = jnp.min(jnp.where(el == m1, lane, no_lane), axis=-1, keepdims=True)
    el2 = jnp.where(lane == i1, ninf, el)
    m2 = jnp.max(el2, axis=-1, keepdims=True)
    i2 = jnp.min(jnp.where(el2 == m2, lane, no_lane), axis=-1, keepdims=True)
    e2 = jnp.exp(m2 - m1)
    w1 = g_top / (1.0 + e2)
    w2 = g_top * e2 / (1.0 + e2)

    first_is_lo = i1 < i2
    a = jnp.minimum(i1, i2) - lo
    b = jnp.maximum(i1, i2) - lo
    cls = g_idx * PAIRS_PER_GROUP + a * (15.0 - a) * 0.5 + (b - a - 1.0)
    w_lo = jnp.where(first_is_lo, w1, w2)
    w_hi = jnp.where(first_is_lo, w2, w1)

    @pl.when((pl.program_id(0) == 0) & (pl.program_id(1) == 0))
    def _():
        count_ref[...] = jnp.zeros_like(count_ref)

    tm = logits.shape[0]
    onehot = jnp.where(lane == cls, 1.0, 0.0)
    r_i = lax.broadcasted_iota(jnp.int32, (BLOCK, BLOCK), 0)
    c_i = lax.broadcasted_iota(jnp.int32, (BLOCK, BLOCK), 1)
    earlier = jnp.where(r_i > c_i, 1.0, 0.0).astype(BF16)
    running = count_ref[...]
    before = []
    for blk in range(tm // BLOCK):
        oh = onehot[blk * BLOCK:(blk + 1) * BLOCK]
        before.append(_dot(earlier, oh.astype(BF16)) + running)
        running = running + jnp.sum(oh, axis=0, keepdims=True)
    before = jnp.concatenate(before, axis=0)
    rank = jnp.sum(jnp.where(lane == cls, before, 0.0), axis=-1, keepdims=True)
    count_ref[...] = running
    counts_out_ref[...] = running

    route = jnp.where(lane == ROUTE_CLS, cls, 0.0)
    route = jnp.where(lane == ROUTE_W_LO, w_lo, route)
    route = jnp.where(lane == ROUTE_W_HI, w_hi, route)
    route = jnp.where(lane == ROUTE_RANK, rank, route)
    route_ref[0] = route


def _out_proj(x, a_out, b_out, p, tm):
    B, S, D = x.shape
    full = lambda shape: pl.BlockSpec(shape, lambda b, i: (0,) * len(shape))
    row = lambda w: pl.BlockSpec((1, tm, w), lambda b, i: (b, i, 0))
    return pl.pallas_call(
        _out_proj_kernel,
        grid=(B, S // tm),
        in_specs=[row(D), row(SWA_Q_COLS), row(MLA_OUT_COLS),
                  full((1, D)), full((1, D)), full((1, SWA_Q_COLS)), full((1, MLA_OUT_COLS)),
                  full((D, D)), full((1, D)), full((1, D)),
                  full((D, 2 * ROUTE_LANES)), full((1, ROUTE_LANES))],
        out_specs=(row(D), row(ROUTE_LANES), full((1, ROUTE_LANES))),
        out_shape=(jax.ShapeDtypeStruct((B, S, D), F32),
                   jax.ShapeDtypeStruct((B, S, ROUTE_LANES), F32),
                   jax.ShapeDtypeStruct((1, ROUTE_LANES), F32)),
        scratch_shapes=[pltpu.VMEM((1, ROUTE_LANES), F32)],
        compiler_params=_cparams(("arbitrary", "arbitrary")),
        name="out_proj_route",
    )(x, a_out, b_out, p["ln_in_g"], p["ln_in_b"], p["ga"], p["gb"], p["w_o"], p["ln1_g"],
      p["ln1_b"], p["w_route"], p["b_route"])


SLAB = D_MODEL // LANES
PAIR_PITCH = 2 * SLAB


def _slab_copy(src_ref, src_tok8, dst_ref, dst_tok8, sem):
    return pltpu.make_async_copy(src_ref.at[pl.ds(pl.multiple_of(src_tok8, SLAB), SLAB)],
                                 dst_ref.at[pl.ds(pl.multiple_of(dst_tok8, SLAB), SLAB)], sem)


def _to_slabs(x, slab_ref, n):
    for s in range(SLAB):
        slab_ref[pl.ds(s, n, stride=SLAB), :] = x[:, s * LANES:(s + 1) * LANES]


def _from_slabs(slab_ref, n):
    return jnp.concatenate([slab_ref[pl.ds(s, n, stride=SLAB), :] for s in range(SLAB)], axis=1)


def _dispatch_kernel(padstart_ref, nu_ref, pos_ref, h1_ref, xs_ref, rows_ref, zero_ref, sem, *,
                     tm, n_tiles):
    tail = zero_ref.shape[0]

    def zero_fill(start):
        return pltpu.make_async_copy(
            zero_ref, xs_ref.at[pl.ds(pl.multiple_of(start, SLAB), tail)], sem)

    @pl.when(pl.program_id(0) == 0)
    def _():
        zero_ref[...] = jnp.zeros_like(zero_ref)
        for parity in range(2):
            tails = [zero_fill(padstart_ref[c]) for c in range(parity, N_CLASSES, 2)]
            for t in tails:
                t.start()
            for t in tails:
                t.wait()

        def fill_unused(j, carry):
            fill = zero_fill(j * tail)
            fill.start()
            fill.wait()
            return carry

        lax.fori_loop(nu_ref[0], n_tiles + 1, fill_unused, 0)

    _to_slabs(h1_ref[...], rows_ref, tm)

    def issue(r0, carry):
        for k in range(DMA_BURST):
            r = r0 * DMA_BURST + k
            _slab_copy(rows_ref, r * SLAB, xs_ref, pos_ref[0, 0, r], sem).start(priority=k % 2)
        return carry

    lax.fori_loop(0, tm // DMA_BURST, issue, 0)
    pltpu.make_async_copy(rows_ref, xs_ref.at[pl.ds(0, tm * SLAB)], sem).wait()


def _dispatch(h1, pos8, pad_start8, n_used, n_tiles, tm):
    T, D = h1.shape
    pos8 = pos8.reshape(T // tm, 1, tm)
    n_rows = (n_tiles + 1) * MOE_TILE_ROWS
    grid_spec = pltpu.PrefetchScalarGridSpec(
        num_scalar_prefetch=2,
        grid=(T // tm,),
        in_specs=[pl.BlockSpec((1, 1, tm), lambda t, ps, nu: (t, 0, 0),
                               memory_space=pltpu.SMEM),
                  pl.BlockSpec((tm, D), lambda t, ps, nu: (t, 0))],
        out_specs=pl.BlockSpec(memory_space=pl.ANY),
        scratch_shapes=[pltpu.VMEM((tm * SLAB, LANES), F32),
                        pltpu.VMEM((MOE_TILE_ROWS * SLAB, LANES), F32),
                        pltpu.SemaphoreType.DMA],
    )
    return pl.pallas_call(
        functools.partial(_dispatch_kernel, tm=tm, n_tiles=n_tiles),
        grid_spec=grid_spec,
        out_shape=jax.ShapeDtypeStruct((n_rows * SLAB, LANES), F32),
        compiler_params=_cparams(("arbitrary",)),
        name="moe_dispatch",
    )(pad_start8, n_used, pos8, h1)


def _experts_kernel(tlo_ref, thi_ref, nu_ref, x_ref, wgu_lo_ref, wgu_hi_ref, wd_lo_ref, wd_hi_ref,
                    y_ref):
    j = pl.program_id(0)
    tr = MOE_TILE_ROWS

    @pl.when(j < nu_ref[0])
    def _():
        for s in range(TOP_K * SLAB, PAIR_PITCH):
            y_ref[pl.ds(s, tr, stride=PAIR_PITCH), :] = jnp.zeros((tr, LANES), F32)
        x = _from_slabs(x_ref, tr).astype(BF16)
        for k, (wgu_ref, wd_ref) in enumerate(((wgu_lo_ref, wd_lo_ref), (wgu_hi_ref, wd_hi_ref))):
            gu = _dot(x, wgu_ref[0])
            g = gu[:, :D_EXPERT]
            u = gu[:, D_EXPERT:]
            hid = g * (1.0 / (1.0 + jnp.exp(-g))) * u
            y = _dot(hid.astype(BF16), wd_ref[0])
            for s in range(SLAB):
                y_ref[pl.ds(k * SLAB + s, tr, stride=PAIR_PITCH), :] = (
                    y[:, s * LANES:(s + 1) * LANES])

    @pl.when(j >= nu_ref[0])
    def _():
        y_ref[...] = jnp.zeros_like(y_ref)


def _experts(xs, tile_lo, tile_hi, n_used, p, n_tiles):
    D = D_MODEL
    tr = MOE_TILE_ROWS
    last_used = lambda j, nu: jnp.minimum(j, nu[0] - 1)
    wgu = lambda pick: pl.BlockSpec((1, D, 2 * D_EXPERT),
                                    lambda j, tlo, thi, nu: (pick(tlo, thi)[j], 0, 0))
    wd = lambda pick: pl.BlockSpec((1, D_EXPERT, D),
                                   lambda j, tlo, thi, nu: (pick(tlo, thi)[j], 0, 0))
    lo = lambda tlo, thi: tlo
    hi = lambda tlo, thi: thi
    grid_spec = pltpu.PrefetchScalarGridSpec(
        num_scalar_prefetch=3,
        grid=(n_tiles,),
        in_specs=[pl.BlockSpec((tr * SLAB, LANES), lambda j, tlo, thi, nu: (last_used(j, nu), 0)),
                  wgu(lo), wgu(hi), wd(lo), wd(hi)],
        out_specs=pl.BlockSpec((tr * PAIR_PITCH, LANES), lambda j, tlo, thi, nu: (j, 0)),
    )
    return pl.pallas_call(
        _experts_kernel,
        grid_spec=grid_spec,
        out_shape=jax.ShapeDtypeStruct((n_tiles * tr * PAIR_PITCH, LANES), F32),
        compiler_params=_cparams(("arbitrary",)),
        name="moe_experts",
    )(tile_lo, tile_hi, n_used, xs, p["w_gu"], p["w_gu"], p["w_down"], p["w_down"])


def _combine_kernel(pos_ref, pos_next_ref, h1_ref, route_ref, ys_ref, g2_ref, b2_ref, o_ref,
                    ybuf_ref, sem, *, tm):
    pair = TOP_K * SLAB
    t = pl.program_id(0)
    slot = t % 2

    def gather(idx_ref, dst_slot):
        def issue(r0, carry):
            for k in range(DMA_BURST):
                r = r0 * DMA_BURST + k
                src = pl.multiple_of(idx_ref[0, 0, r] * (PAIR_PITCH // SLAB), SLAB)
                pltpu.make_async_copy(
                    ys_ref.at[pl.ds(src, pair)],
                    ybuf_ref.at[dst_slot, pl.ds(pl.multiple_of(r * pair, pair), pair)],
                    sem.at[dst_slot]).start(priority=k % 2)
            return carry

        lax.fori_loop(0, tm // DMA_BURST, issue, 0)

    @pl.when(t == 0)
    def _():
        gather(pos_ref, 0)

    @pl.when(t + 1 < pl.num_programs(0))
    def _():
        gather(pos_next_ref, 1 - slot)

    pltpu.make_async_copy(ys_ref.at[pl.ds(0, tm * pair)], ybuf_ref.at[slot], sem.at[slot]).wait()

    def expert_out(k):
        return jnp.concatenate([ybuf_ref[slot, pl.ds(k * SLAB + s, tm, stride=pair), :]
                                for s in range(SLAB)], axis=1)

    route = route_ref[...]
    w_lo = route[:, ROUTE_W_LO:ROUTE_W_LO + 1]
    w_hi = route[:, ROUTE_W_HI:ROUTE_W_HI + 1]
    ffn = w_lo * expert_out(0) + w_hi * expert_out(1)
    o_ref[...] = _layer_norm(ALPHA * h1_ref[...] + ffn, g2_ref[...], b2_ref[...])


def _combine(h1, route, pos8, ys, p, tm):
    T, D = h1.shape
    n_steps = T // tm
    pos8 = pos8.reshape(n_steps, 1, tm)
    full = lambda shape: pl.BlockSpec(shape, lambda t: (0,) * len(shape))
    return pl.pallas_call(
        functools.partial(_combine_kernel, tm=tm),
        grid=(n_steps,),
        in_specs=[pl.BlockSpec((1, 1, tm), lambda t: (t, 0, 0), memory_space=pltpu.SMEM),
                  pl.BlockSpec((1, 1, tm), lambda t: (jnp.minimum(t + 1, n_steps - 1), 0, 0),
                               memory_space=pltpu.SMEM),
                  pl.BlockSpec((tm, D), lambda t: (t, 0)),
                  pl.BlockSpec((tm, ROUTE_LANES), lambda t: (t, 0)),
                  pl.BlockSpec(memory_space=pl.ANY),
                  full((1, D)), full((1, D))],
        out_specs=pl.BlockSpec((tm, D), lambda t: (t, 0)),
        out_shape=jax.ShapeDtypeStruct((T, D), F32),
        scratch_shapes=[pltpu.VMEM((2, tm * TOP_K * SLAB, LANES), F32),
                        pltpu.SemaphoreType.DMA((2,))],
        compiler_params=_cparams(("arbitrary",)),
        name="moe_combine_ln2",
    )(pos8, pos8, h1, route, ys, p["ln2_g"], p["ln2_b"])


def _dispatch_plan(route, counts):
    T = route.shape[0]
    tr = MOE_TILE_ROWS
    n_tiles = T // tr + N_CLASSES
    cnt = counts[0, :N_CLASSES].astype(jnp.int32)
    padded = jnp.maximum((cnt + tr - 1) // tr, 1) * tr
    seg_end = jnp.cumsum(padded)
    seg_start = seg_end - padded
    cls = route[:, ROUTE_CLS].astype(jnp.int32)
    rank = route[:, ROUTE_RANK].astype(jnp.int32)
    classes = jnp.arange(N_CLASSES, dtype=jnp.int32)
    start_of = jnp.sum(jnp.where(cls[:, None] == classes, seg_start, 0), axis=-1)
    pos8 = (start_of + rank) * SLAB
    tile_row0 = jnp.arange(n_tiles, dtype=jnp.int32) * tr
    tile_cls = jnp.sum((seg_end[None, :] <= tile_row0[:, None]).astype(jnp.int32), axis=-1)
    tile_cls = jnp.minimum(tile_cls, N_CLASSES - 1)
    pairs = [(a, b) for a in range(EXPERTS_PER_GROUP) for b in range(a + 1, EXPERTS_PER_GROUP)]
    lo_of = jnp.asarray([g * EXPERTS_PER_GROUP + a for g in range(N_GROUPS) for a, _ in pairs],
                        jnp.int32)
    hi_of = jnp.asarray([g * EXPERTS_PER_GROUP + b for g in range(N_GROUPS) for _, b in pairs],
                        jnp.int32)
    is_cls = tile_cls[:, None] == classes
    tile_lo = jnp.sum(jnp.where(is_cls, lo_of, 0), axis=-1)
    tile_hi = jnp.sum(jnp.where(is_cls, hi_of, 0), axis=-1)
    n_used = (seg_end[-1:] // tr).astype(jnp.int32)
    return pos8, (seg_start + cnt) * SLAB, tile_lo, tile_hi, n_used, n_tiles


def _rope_tables(pos):
    d = SWA_HEAD_DIM
    inv_freq = ROPE_THETA ** (-jnp.arange(0, d, 2, dtype=F32) / d)
    ang = pos[:, None] * inv_freq[None, :]
    c, s = jnp.cos(ang), jnp.sin(ang)
    return (jnp.concatenate([c, c, c, c], axis=-1), jnp.concatenate([-s, s, -s, s], axis=-1),
            jnp.concatenate([c, c], axis=-1).T, jnp.concatenate([s, s], axis=-1).T)


def _prep_params(ln_in_g, ln_in_b, w_in, swa_sinks, mla_q_norm_g, mla_w_uq, mla_kv_norm_g, mla_w_ukv,
                 swa_out_norm_g, mla_out_norm_g, w_o, ln1_g, ln1_b, moe_w_group, moe_b_group,
                 moe_w_router, moe_b_router, moe_w_gate, moe_w_up, moe_w_down, ln2_g, ln2_b):
    order = jnp.asarray(SWA_HEAD_ORDER)
    perm = (order[:, None] * SWA_HEAD_DIM + jnp.arange(SWA_HEAD_DIM)[None, :]).reshape(-1)
    w = w_in[0]
    w_q = w[:, :SWA_Q_COLS][:, perm]
    w_pad = jnp.zeros((D_MODEL, IN_COLS_PADDED - w.shape[1]), w.dtype)
    w_in_p = jnp.concatenate([w_q, w[:, SWA_Q_COLS:], w_pad], axis=1).astype(BF16)

    dqk = MLA_NOPE_DIM + MLA_ROPE_DIM
    wuq = mla_w_uq[0].reshape(MLA_Q_RANK, MLA_HEADS, dqk)
    wuq = jnp.pad(wuq, ((0, 0), (0, 0), (0, MLA_QK_PAD - dqk)))
    wuq_t = jnp.transpose(wuq, (1, 2, 0)).astype(BF16)
    wukv = mla_w_ukv[0].reshape(MLA_KV_RANK, MLA_HEADS, MLA_NOPE_DIM + MLA_V_DIM)
    wuk = wukv[:, :, :MLA_NOPE_DIM].reshape(MLA_KV_RANK, MLA_HEADS * MLA_NOPE_DIM).astype(BF16)
    wuv_t = jnp.transpose(wukv[:, :, MLA_NOPE_DIM:], (1, 2, 0)).astype(BF16)

    w_o0 = w_o[0]
    w_o_p = jnp.concatenate([w_o0[:SWA_Q_COLS][perm], w_o0[SWA_Q_COLS:]], axis=0).astype(BF16)

    w_route = jnp.zeros((D_MODEL, ROUTE_LANES), F32)
    w_route = w_route.at[:, :N_GROUPS].set(moe_w_group[0])
    w_route = w_route.at[:, EXPERT_LANE0:EXPERT_LANE0 + N_EXPERTS].set(moe_w_router[0])
    w_route_hi = w_route.astype(BF16)
    w_route_lo = (w_route - w_route_hi.astype(F32)).astype(BF16)
    w_route = jnp.concatenate([w_route_hi, w_route_lo], axis=1)
    b_route = jnp.zeros((1, ROUTE_LANES), F32)
    b_route = b_route.at[0, :N_GROUPS].set(moe_b_group[0])
    b_route = b_route.at[0, EXPERT_LANE0:EXPERT_LANE0 + N_EXPERTS].set(moe_b_router[0])

    return {
        "ln_in_g": ln_in_g[None, :], "ln_in_b": ln_in_b[None, :],
        "w_in": w_in_p,
        "sinks": swa_sinks[0],
        "gq": mla_q_norm_g, "w_uq_t": wuq_t,
        "gkv": mla_kv_norm_g, "w_uk": wuk, "w_uv_t": wuv_t,
        "ga": swa_out_norm_g[0][perm][None, :], "gb": mla_out_norm_g,
        "w_o": w_o_p, "ln1_g": ln1_g, "ln1_b": ln1_b,
        "w_route": w_route, "b_route": b_route,
        "w_gu": jnp.concatenate([moe_w_gate[0], moe_w_up[0]], axis=-1).astype(BF16),
        "w_down": moe_w_down[0].astype(BF16),
        "ln2_g": ln2_g, "ln2_b": ln2_b,
    }


def kernel(x, meta_tokens, ln_in_g, ln_in_b, w_in, swa_sinks, mla_q_norm_g, mla_w_uq, mla_kv_norm_g, mla_w_ukv, swa_out_norm_g, mla_out_norm_g, w_o, ln1_g, ln1_b, moe_w_group, moe_b_group, moe_w_router, moe_b_router, moe_w_gate, moe_w_up, moe_w_down, ln2_g, ln2_b):
    B, S, D = x.shape
    p = _prep_params(ln_in_g, ln_in_b, w_in, swa_sinks, mla_q_norm_g, mla_w_uq, mla_kv_norm_g,
                     mla_w_ukv, swa_out_norm_g, mla_out_norm_g, w_o, ln1_g, ln1_b, moe_w_group,
                     moe_b_group, moe_w_router, moe_b_router, moe_w_gate, moe_w_up, moe_w_down,
                     ln2_g, ln2_b)
    T = B * S
    tm = min(PROJ_TILE, S)

    x_meta = jnp.concatenate([jnp.zeros((N_PAD, D), x.dtype), meta_tokens.astype(x.dtype)])[None]
    pos_meta = jnp.maximum(jnp.arange(BLOCK) - N_PAD, 0).astype(F32)
    pos_tok = (jnp.arange(S) + N_META).astype(F32)
    _, ka_m, va_m, _, km_m, vt_m = _in_proj(x_meta, _rope_tables(pos_meta), p, BLOCK)
    qa, ka, va, qt, km, vt = _in_proj(x, _rope_tables(pos_tok), p, tm)

    a_out = _swa(p["sinks"], qa, ka, va, ka_m, va_m, min(WIDE_TILE, S))
    b_out = _mla(qt, km, vt, km_m[:, :, N_PAD:, :], vt_m[..., N_PAD:], tm)
    h1, route, counts = _out_proj(x, a_out, b_out, p, min(WIDE_TILE, S))
    h1 = h1.reshape(B * S, D)
    route = route.reshape(B * S, ROUTE_LANES)

    pos8, pad_start8, tile_lo, tile_hi, n_used, n_tiles = _dispatch_plan(route, counts)
    xs = _dispatch(h1, pos8, pad_start8, n_used, n_tiles, min(DISPATCH_TILE, T))
    ys = _experts(xs, tile_lo, tile_hi, n_used, p, n_tiles)
    out = _combine(h1, route, pos8, ys, p, min(COMBINE_TILE, T))
    return out.reshape(B, S, D)
```

```python
import functools

import jax
import jax.numpy as jnp
from jax import lax
from jax.experimental import pallas as pl
from jax.experimental.pallas import tpu as pltpu

D_MODEL = 1024
N_META = 16
BLOCK = 128
N_PAD = BLOCK - N_META
ROPE_THETA = 10000.0
SWA_HEADS = 8
SWA_KV_HEADS = 2
SWA_HEAD_DIM = 64
MLA_HEADS = 4
MLA_Q_RANK = 256
MLA_KV_RANK = 256
MLA_NOPE_DIM = 128
MLA_ROPE_DIM = 64
MLA_V_DIM = 128
SWA_Q_COLS = SWA_HEADS * SWA_HEAD_DIM
SWA_KV_COLS = SWA_KV_HEADS * SWA_HEAD_DIM
MLA_OUT_COLS = MLA_HEADS * MLA_V_DIM
N_GROUPS = 4
EXPERTS_PER_GROUP = 8
N_EXPERTS = N_GROUPS * EXPERTS_PER_GROUP
D_EXPERT = 256
LN_EPS = 1e-5
RMS_EPS = 1e-6
DEPTH = 1
ALPHA = (2.0 * DEPTH) ** 0.25
NEG = -1e30
LOG2_E = 1.4426950408889634

LANES = 128
IN_COLS_PADDED = 1408
MLA_QK_PAD = 256
ROUTE_LANES = 128
EXPERT_LANE0 = 32
ROUTE_CLS, ROUTE_W_LO, ROUTE_W_HI, ROUTE_RANK = range(4)
TOP_K = 2
PAIRS_PER_GROUP = EXPERTS_PER_GROUP * (EXPERTS_PER_GROUP - 1) // 2
N_CLASSES = N_GROUPS * PAIRS_PER_GROUP
MOE_TILE_ROWS = 256
PROJ_TILE = 1024
WIDE_TILE = 1024
DISPATCH_TILE = 2048
COMBINE_TILE = 1024
DMA_BURST = 8
VMEM_LIMIT = 56 * 1024 * 1024

SWA_HEAD_ORDER = (0, 4, 1, 5, 2, 6, 3, 7)

BF16 = jnp.bfloat16
F32 = jnp.float32


def _cparams(sem):
    return pltpu.CompilerParams(dimension_semantics=sem, vmem_limit_bytes=VMEM_LIMIT)


def _layer_norm(x, g, b):
    mu = jnp.mean(x, axis=-1, keepdims=True)
    xc = x - mu
    var = jnp.mean(xc * xc, axis=-1, keepdims=True)
    return xc * lax.rsqrt(var + LN_EPS) * g + b


def _rms_scale(x):
    return lax.rsqrt(jnp.mean(x * x, axis=-1, keepdims=True) + RMS_EPS)


def _rope128(x, cos, sin_signed, lane_lo):
    up = pltpu.roll(x, LANES - 32, 1)
    dn = pltpu.roll(x, 32, 1)
    return x * cos + jnp.where(lane_lo, up, dn) * sin_signed


def _dot(a, b):
    return jnp.dot(a, b, preferred_element_type=F32)


def _dot_nt(a, b):
    return lax.dot_general(a, b, (((1,), (1,)), ((), ())), preferred_element_type=F32)


def _in_proj_kernel(x_ref, g_ref, b_ref, w_ref, cos_ref, sin_ref, cost_ref, sint_ref, gq_ref,
                    wuqt_ref, gkv_ref, wuk_ref, wuvt_ref,
                    qa_ref, ka_ref, va_ref, qt_ref, km_ref, vt_ref):
    h = _layer_norm(x_ref[0], g_ref[...], b_ref[...])
    u = _dot(h.astype(BF16), w_ref[...])
    cos = cos_ref[...]
    sin = sin_ref[...]
    lane = lax.broadcasted_iota(jnp.int32, (1, LANES), 1)
    lane_lo = (lane % 64) < 32
    rope = functools.partial(_rope128, cos=cos, sin_signed=sin, lane_lo=lane_lo)

    swa_scale = SWA_HEAD_DIM ** -0.5
    for c in range(SWA_Q_COLS // LANES):
        qc = rope(u[:, c * LANES:(c + 1) * LANES]) * swa_scale
        qa_ref[0, :, c * LANES:(c + 1) * LANES] = qc.astype(BF16)
    ka_ref[0] = rope(u[:, 512:640]).astype(BF16)
    va_ref[0] = u[:, 640:768].astype(BF16)

    cq = u[:, 768:1024]
    cqn = (cq * _rms_scale(cq) * gq_ref[...]).astype(BF16)
    ckv = u[:, 1024:1280]
    ckvn = (ckv * _rms_scale(ckv) * gkv_ref[...]).astype(BF16)
    k_nope = _dot(ckvn, wuk_ref[...])
    kr = rope(u[:, 1280:1408]).astype(BF16)
    q_scale = (MLA_NOPE_DIM + MLA_ROPE_DIM) ** -0.5 * LOG2_E
    cost = cost_ref[...]
    sint = sint_ref[...]
    half = MLA_ROPE_DIM // 2
    for hd in range(MLA_HEADS):
        qt = _dot_nt(wuqt_ref[hd], cqn)
        qr = qt[MLA_NOPE_DIM:MLA_NOPE_DIM + MLA_ROPE_DIM]
        rot = jnp.concatenate([-qr[half:], qr[:half]], axis=0)
        qt_ref[0, hd, 0:MLA_NOPE_DIM, :] = (qt[:MLA_NOPE_DIM] * q_scale).astype(BF16)
        qt_ref[0, hd, MLA_NOPE_DIM:MLA_NOPE_DIM + MLA_ROPE_DIM, :] = (
            (qr * cost + rot * sint) * q_scale).astype(BF16)
        qt_ref[0, hd, MLA_NOPE_DIM + MLA_ROPE_DIM:, :] = (
            qt[MLA_NOPE_DIM + MLA_ROPE_DIM:]).astype(BF16)
        km_ref[0, hd, :, 0:LANES] = k_nope[:, hd * LANES:(hd + 1) * LANES].astype(BF16)
        km_ref[0, hd, :, LANES:2 * LANES] = kr
        vt_ref[0, hd, 0] = _dot_nt(wuvt_ref[hd], ckvn).astype(BF16)


def _in_proj(x, tables, p, tm):
    cos, sin, cos_t, sin_t = tables
    B, S, D = x.shape
    grid = (B, S // tm)
    full = lambda shape: pl.BlockSpec(shape, lambda b, i: (0,) * len(shape))
    out_shape = (
        jax.ShapeDtypeStruct((B, S, SWA_Q_COLS), BF16),
        jax.ShapeDtypeStruct((B, S, SWA_KV_COLS), BF16),
        jax.ShapeDtypeStruct((B, S, SWA_KV_COLS), BF16),
        jax.ShapeDtypeStruct((B, MLA_HEADS, MLA_QK_PAD, S), BF16),
        jax.ShapeDtypeStruct((B, MLA_HEADS, S, MLA_QK_PAD), BF16),
        jax.ShapeDtypeStruct((B, MLA_HEADS, S // tm, MLA_V_DIM, tm), BF16),
    )
    row = lambda w: pl.BlockSpec((1, tm, w), lambda b, i: (b, i, 0))
    head = lambda w: pl.BlockSpec((1, MLA_HEADS, tm, w), lambda b, i: (b, 0, i, 0))
    return pl.pallas_call(
        _in_proj_kernel,
        grid=grid,
        in_specs=[
            row(D),
            full((1, D)), full((1, D)),
            full((D, IN_COLS_PADDED)),
            pl.BlockSpec((tm, LANES), lambda b, i: (i, 0)),
            pl.BlockSpec((tm, LANES), lambda b, i: (i, 0)),
            pl.BlockSpec((MLA_ROPE_DIM, tm), lambda b, i: (0, i)),
            pl.BlockSpec((MLA_ROPE_DIM, tm), lambda b, i: (0, i)),
            full((1, MLA_Q_RANK)), full((MLA_HEADS, MLA_QK_PAD, MLA_Q_RANK)),
            full((1, MLA_KV_RANK)), full((MLA_KV_RANK, MLA_HEADS * MLA_NOPE_DIM)),
            full((MLA_HEADS, MLA_V_DIM, MLA_KV_RANK)),
        ],
        out_specs=(row(SWA_Q_COLS), row(SWA_KV_COLS), row(SWA_KV_COLS),
                   pl.BlockSpec((1, MLA_HEADS, MLA_QK_PAD, tm), lambda b, i: (b, 0, 0, i)),
                   head(MLA_QK_PAD),
                   pl.BlockSpec((1, MLA_HEADS, 1, MLA_V_DIM, tm), lambda b, i: (b, 0, i, 0, 0))),
        out_shape=out_shape,
        compiler_params=_cparams(("parallel", "parallel")),
        name="in_proj",
    )(x, p["ln_in_g"], p["ln_in_b"], p["w_in"], cos, sin, cos_t, sin_t, p["gq"], p["w_uq_t"],
      p["gkv"], p["w_uk"], p["w_uv_t"])


def _swa_blocks(sink_ref, qa_ref, kc_ref, vc_ref, kprev, vprev, first, o_ref, blocks_per_step):
    row = lax.broadcasted_iota(jnp.int32, (2 * BLOCK, 2 * BLOCK), 0) % BLOCK
    col = lax.broadcasted_iota(jnp.int32, (2 * BLOCK, 2 * BLOCK), 1)
    base_mask = ((col < BLOCK) & (col > row)) | ((col >= BLOCK) & ((col - BLOCK) <= row))
    first_mask = base_mask & (col >= jnp.where(first, N_PAD, 0))
    lane = lax.broadcasted_iota(jnp.int32, (1, LANES), 1)
    lane_kv0 = lane < SWA_HEAD_DIM
    top_rows = lax.broadcasted_iota(jnp.int32, (2 * BLOCK, 1), 0) < BLOCK

    for r in range(blocks_per_step):
        rows = slice(r * BLOCK, (r + 1) * BLOCK)
        if r == 0:
            kp, vp = kprev, vprev
            mask = first_mask
        else:
            prev_rows = slice((r - 1) * BLOCK, r * BLOCK)
            kp, vp = kc_ref[0, prev_rows, :], vc_ref[0, prev_rows, :]
            mask = base_mask
        keys = jnp.concatenate([kp, kc_ref[0, rows, :]], axis=0)
        vals = jnp.concatenate([vp, vc_ref[0, rows, :]], axis=0)
        for c in range(SWA_Q_COLS // LANES):
            qc = qa_ref[0, rows, c * LANES:(c + 1) * LANES]
            zero = jnp.zeros_like(qc)
            q2 = jnp.concatenate([jnp.where(lane_kv0, qc, zero),
                                  jnp.where(lane_kv0, zero, qc)], axis=0)
            s = jnp.where(mask, _dot_nt(q2, keys), NEG)
            sink = jnp.where(top_rows, sink_ref[c], sink_ref[c + 4])
            m = jnp.maximum(jnp.max(s, axis=-1, keepdims=True), sink)
            e = jnp.exp(s - m)
            denom = jnp.sum(e, axis=-1, keepdims=True) + jnp.exp(sink - m)
            pr = (e * (1.0 / denom)).astype(BF16)
            o2 = _dot(pr, vals)
            o = jnp.where(lane_kv0, o2[:BLOCK], o2[BLOCK:])
            o_ref[0, rows, c * LANES:(c + 1) * LANES] = o.astype(BF16)


def _in_proj_swa_kernel(sink_ref, x_ref, g_ref, b_ref, w_ref, cos_ref, sin_ref, cost_ref, sint_ref,
                        gq_ref, wuqt_ref, gkv_ref, wuk_ref, wuvt_ref, kmeta_ref, vmeta_ref,
                        a_ref, qt_ref, km_ref, vt_ref,
                        qa_s, ka_s, va_s, kprev_s, vprev_s):
    _in_proj_kernel(x_ref, g_ref, b_ref, w_ref, cos_ref, sin_ref, cost_ref, sint_ref, gq_ref,
                    wuqt_ref, gkv_ref, wuk_ref, wuvt_ref, qa_s, ka_s, va_s, qt_ref, km_ref, vt_ref)
    first = pl.program_id(1) == 0
    kprev = jnp.where(first, kmeta_ref[0], kprev_s[...])
    vprev = jnp.where(first, vmeta_ref[0], vprev_s[...])
    tm = qa_s.shape[1]
    _swa_blocks(sink_ref, qa_s, ka_s, va_s, kprev, vprev, first, a_ref, tm // BLOCK)
    kprev_s[...] = ka_s[0, tm - BLOCK:, :]
    vprev_s[...] = va_s[0, tm - BLOCK:, :]


def _in_proj_swa(x, tables, p, ka_meta, va_meta, tm):
    cos, sin, cos_t, sin_t = tables
    B, S, D = x.shape
    kvw = SWA_KV_COLS
    full = lambda shape: pl.BlockSpec(shape, lambda b, i: (0,) * len(shape))
    row = lambda w: pl.BlockSpec((1, tm, w), lambda b, i: (b, i, 0))
    out_shape = (
        jax.ShapeDtypeStruct((B, S, SWA_Q_COLS), BF16),
        jax.ShapeDtypeStruct((B, MLA_HEADS, MLA_QK_PAD, S), BF16),
        jax.ShapeDtypeStruct((B, MLA_HEADS, S, MLA_QK_PAD), BF16),
        jax.ShapeDtypeStruct((B, MLA_HEADS, S // tm, MLA_V_DIM, tm), BF16),
    )
    return pl.pallas_call(
        _in_proj_swa_kernel,
        grid=(B, S // tm),
        in_specs=[
            pl.BlockSpec(memory_space=pltpu.SMEM),
            row(D),
            full((1, D)), full((1, D)),
            full((D, IN_COLS_PADDED)),
            pl.BlockSpec((tm, LANES), lambda b, i: (i, 0)),
            pl.BlockSpec((tm, LANES), lambda b, i: (i, 0)),
            pl.BlockSpec((MLA_ROPE_DIM, tm), lambda b, i: (0, i)),
            pl.BlockSpec((MLA_ROPE_DIM, tm), lambda b, i: (0, i)),
            full((1, MLA_Q_RANK)), full((MLA_HEADS, MLA_QK_PAD, MLA_Q_RANK)),
            full((1, MLA_KV_RANK)), full((MLA_KV_RANK, MLA_HEADS * MLA_NOPE_DIM)),
            full((MLA_HEADS, MLA_V_DIM, MLA_KV_RANK)),
            full((1, BLOCK, kvw)), full((1, BLOCK, kvw)),
        ],
        out_specs=(row(SWA_Q_COLS),
                   pl.BlockSpec((1, MLA_HEADS, MLA_QK_PAD, tm), lambda b, i: (b, 0, 0, i)),
                   pl.BlockSpec((1, MLA_HEADS, tm, MLA_QK_PAD), lambda b, i: (b, 0, i, 0)),
                   pl.BlockSpec((1, MLA_HEADS, 1, MLA_V_DIM, tm), lambda b, i: (b, 0, i, 0, 0))),
        out_shape=out_shape,
        scratch_shapes=[pltpu.VMEM((1, tm, SWA_Q_COLS), BF16), pltpu.VMEM((1, tm, kvw), BF16),
                        pltpu.VMEM((1, tm, kvw), BF16), pltpu.VMEM((BLOCK, kvw), BF16),
                        pltpu.VMEM((BLOCK, kvw), BF16)],
        compiler_params=_cparams(("parallel", "arbitrary")),
        name="in_proj_swa",
    )(p["sinks"], x, p["ln_in_g"], p["ln_in_b"], p["w_in"], cos, sin, cos_t, sin_t, p["gq"],
      p["w_uq_t"], p["gkv"], p["w_uk"], p["w_uv_t"], ka_meta, va_meta)


def _mla_kernel(qt_ref, k_ref, vt_ref, kmeta_ref, vtmeta_ref, o_ref,
                sa_ref, sb_ref, sa_max_ref, sb_max_ref, m_ref, l_ref, acc_ref, *, tq):
    i = pl.program_id(2)
    qt = qt_ref[0, 0]

    s = _dot(kmeta_ref[0, 0], qt)
    m = jnp.max(s, axis=0, keepdims=True)
    p = jnp.exp2(s - m)
    m_ref[...] = m
    l_ref[...] = jnp.sum(p, axis=0, keepdims=True)
    acc_ref[...] = _dot(vtmeta_ref[0, 0, 0], p.astype(BF16))

    def scores_into(s_ref, smax_ref, j):
        start = pl.multiple_of(j * tq, tq)
        s = _dot(k_ref[0, 0, pl.ds(start, tq), :], qt)
        s_ref[...] = s
        smax_ref[...] = jnp.max(s, axis=0, keepdims=True)

    def absorb(s_ref, smax_ref, j, causal=False):
        s = s_ref[...]
        if causal:
            key = lax.broadcasted_iota(jnp.int32, (tq, tq), 0)
            query = lax.broadcasted_iota(jnp.int32, (tq, tq), 1)
            s = jnp.where(key <= query, s, NEG)
            smax = jnp.max(s, axis=0, keepdims=True)
        else:
            smax = smax_ref[...]
        m = m_ref[...]
        m_new = jnp.maximum(m, smax)
        alpha = jnp.exp2(m - m_new)
        p = jnp.exp2(s - m_new)
        m_ref[...] = m_new
        l_ref[...] = alpha * l_ref[...] + jnp.sum(p, axis=0, keepdims=True)
        acc_ref[...] = alpha * acc_ref[...] + _dot(vt_ref[0, 0, j], p.astype(BF16))

    scores_into(sa_ref, sa_max_ref, 0)

    def pair(t, carry):
        scores_into(sb_ref, sb_max_ref, 2 * t + 1)
        absorb(sa_ref, sa_max_ref, 2 * t)
        scores_into(sa_ref, sa_max_ref, 2 * t + 2)
        absorb(sb_ref, sb_max_ref, 2 * t + 1)
        return carry

    lax.fori_loop(0, i // 2, pair, 0)

    @pl.when(i % 2 == 1)
    def _():
        scores_into(sb_ref, sb_max_ref, i)
        absorb(sa_ref, sa_max_ref, i - 1)
        absorb(sb_ref, sb_max_ref, i, causal=True)

    @pl.when(i % 2 == 0)
    def _():
        absorb(sa_ref, sa_max_ref, i, causal=True)

    o_ref[0] = (acc_ref[...] * (1.0 / l_ref[...])).T.astype(BF16)


def _mla(qt, km, vt, km_meta, vt_meta, tq):
    B, H, _, S = qt.shape
    return pl.pallas_call(
        functools.partial(_mla_kernel, tq=tq),
        grid=(B, H, S // tq),
        in_specs=[
            pl.BlockSpec((1, 1, MLA_QK_PAD, tq), lambda b, h, i: (b, h, 0, i)),
            pl.BlockSpec((1, 1, S, MLA_QK_PAD), lambda b, h, i: (b, h, 0, 0)),
            pl.BlockSpec((1, 1, S // tq, MLA_V_DIM, tq), lambda b, h, i: (b, h, 0, 0, 0)),
            pl.BlockSpec((1, 1, N_META, MLA_QK_PAD), lambda b, h, i: (0, h, 0, 0)),
            pl.BlockSpec((1, 1, 1, MLA_V_DIM, N_META), lambda b, h, i: (0, h, 0, 0, 0)),
        ],
        out_specs=pl.BlockSpec((1, tq, MLA_V_DIM), lambda b, h, i: (b, i, h)),
        out_shape=jax.ShapeDtypeStruct((B, S, MLA_OUT_COLS), BF16),
        scratch_shapes=[pltpu.VMEM((tq, tq), F32), pltpu.VMEM((tq, tq), F32),
                        pltpu.VMEM((1, tq), F32), pltpu.VMEM((1, tq), F32),
                        pltpu.VMEM((1, tq), F32), pltpu.VMEM((1, tq), F32),
                        pltpu.VMEM((MLA_V_DIM, tq), F32)],
        compiler_params=_cparams(("parallel", "parallel", "arbitrary")),
        name="mla_attn",
    )(qt, km, vt, km_meta, vt_meta)


def _out_proj_kernel(x_ref, a_ref, b_ref, gin_ref, bin_ref, ga_ref, gb_ref, wo_ref, g1_ref, b1_ref,
                     wrt_ref, brt_ref, h1_ref, route_ref, counts_out_ref, count_ref):
    h = _layer_norm(x_ref[0], gin_ref[...], bin_ref[...])
    a = a_ref[0].astype(F32)
    b = b_ref[0].astype(F32)
    an = (a * _rms_scale(a) * ga_ref[...]).astype(BF16)
    bn = (b * _rms_scale(b) * gb_ref[...]).astype(BF16)
    mix = _dot(jnp.concatenate([an, bn], axis=-1), wo_ref[...])
    h1 = _layer_norm(ALPHA * h + mix, g1_ref[...], b1_ref[...])
    h1_ref[0] = h1

    x_hi = h1.astype(BF16)
    x_lo = (h1 - x_hi.astype(F32)).astype(BF16)
    hi_both = _dot(x_hi, wrt_ref[...])
    logits = (hi_both[:, :ROUTE_LANES] + hi_both[:, ROUTE_LANES:]
              + _dot(x_lo, wrt_ref[:, :ROUTE_LANES]) + brt_ref[...])
    lane = lax.broadcasted_iota(jnp.int32, logits.shape, 1).astype(F32)
    ninf = -jnp.inf
    no_lane = float(ROUTE_LANES)
    gl = jnp.where(lane < N_GROUPS, logits, ninf)
    gmax = jnp.max(gl, axis=-1, keepdims=True)
    g_top = 1.0 / jnp.sum(jnp.exp(gl - gmax), axis=-1, keepdims=True)
    g_idx = jnp.min(jnp.where(gl == gmax, lane, no_lane), axis=-1, keepdims=True)
    lo = EXPERT_LANE0 + g_idx * EXPERTS_PER_GROUP
    el = jnp.where((lane >= lo) & (lane < lo + EXPERTS_PER_GROUP), logits, ninf)
    m1 = jnp.max(el, axis=-1, keepdims=True)
    i1 = jnp.min(jnp.where(el == m1, lane, no_lane), axis=-1, keepdims=True)
    el2 = jnp.where(lane == i1, ninf, el)
    m2 = jnp.max(el2, axis=-1, keepdims=True)
    i2 = jnp.min(jnp.where(el2 == m2, lane, no_lane), axis=-1, keepdims=True)
    e2 = jnp.exp(m2 - m1)
    w1 = g_top / (1.0 + e2)
    w2 = g_top * e2 / (1.0 + e2)

    first_is_lo = i1 < i2
    a = jnp.minimum(i1, i2) - lo
    b = jnp.maximum(i1, i2) - lo
    cls = g_idx * PAIRS_PER_GROUP + a * (15.0 - a) * 0.5 + (b - a - 1.0)
    w_lo = jnp.where(first_is_lo, w1, w2)
    w_hi = jnp.where(first_is_lo, w2, w1)

    @pl.when((pl.program_id(0) == 0) & (pl.program_id(1) == 0))
    def _():
        count_ref[...] = jnp.zeros_like(count_ref)

    tm = logits.shape[0]
    onehot = jnp.where(lane == cls, 1.0, 0.0)
    r_i = lax.broadcasted_iota(jnp.int32, (BLOCK, BLOCK), 0)
    c_i = lax.broadcasted_iota(jnp.int32, (BLOCK, BLOCK), 1)
    earlier = jnp.where(r_i > c_i, 1.0, 0.0).astype(BF16)
    running = count_ref[...]
    before = []
    for blk in range(tm // BLOCK):
        oh = onehot[blk * BLOCK:(blk + 1) * BLOCK]
        before.append(_dot(earlier, oh.astype(BF16)) + running)
        running = running + jnp.sum(oh, axis=0, keepdims=True)
    before = jnp.concatenate(before, axis=0)
    rank = jnp.sum(jnp.where(lane == cls, before, 0.0), axis=-1, keepdims=True)
    count_ref[...] = running
    counts_out_ref[...] = running

    route = jnp.where(lane == ROUTE_CLS, cls, 0.0)
    route = jnp.where(lane == ROUTE_W_LO, w_lo, route)
    route = jnp.where(lane == ROUTE_W_HI, w_hi, route)
    route = jnp.where(lane == ROUTE_RANK, rank, route)
    route_ref[0] = route


def _out_proj(x, a_out, b_out, p, tm):
    B, S, D = x.shape
    full = lambda shape: pl.BlockSpec(shape, lambda b, i: (0,) * len(shape))
    row = lambda w: pl.BlockSpec((1, tm, w), lambda b, i: (b, i, 0))
    return pl.pallas_call(
        _out_proj_kernel,
        grid=(B, S // tm),
        in_specs=[row(D), row(SWA_Q_COLS), row(MLA_OUT_COLS),
                  full((1, D)), full((1, D)), full((1, SWA_Q_COLS)), full((1, MLA_OUT_COLS)),
                  full((D, D)), full((1, D)), full((1, D)),
                  full((D, 2 * ROUTE_LANES)), full((1, ROUTE_LANES))],
        out_specs=(row(D), row(ROUTE_LANES), full((1, ROUTE_LANES))),
        out_shape=(jax.ShapeDtypeStruct((B, S, D), F32),
                   jax.ShapeDtypeStruct((B, S, ROUTE_LANES), F32),
                   jax.ShapeDtypeStruct((1, ROUTE_LANES), F32)),
        scratch_shapes=[pltpu.VMEM((1, ROUTE_LANES), F32)],
        compiler_params=_cparams(("arbitrary", "arbitrary")),
        name="out_proj_route",
    )(x, a_out, b_out, p["ln_in_g"], p["ln_in_b"], p["ga"], p["gb"], p["w_o"], p["ln1_g"],
      p["ln1_b"], p["w_route"], p["b_route"])


SLAB = D_MODEL // LANES
PAIR_PITCH = 2 * SLAB


def _slab_copy(src_ref, src_tok8, dst_ref, dst_tok8, sem):
    return pltpu.make_async_copy(src_ref.at[pl.ds(pl.multiple_of(src_tok8, SLAB), SLAB)],
                                 dst_ref.at[pl.ds(pl.multiple_of(dst_tok8, SLAB), SLAB)], sem)


def _to_slabs(x, slab_ref, n):
    for s in range(SLAB):
        slab_ref[pl.ds(s, n, stride=SLAB), :] = x[:, s * LANES:(s + 1) * LANES]


def _from_slabs(slab_ref, n):
    return jnp.concatenate([slab_ref[pl.ds(s, n, stride=SLAB), :] for s in range(SLAB)], axis=1)


def _dispatch_kernel(padstart_ref, nu_ref, pos_ref, h1_ref, xs_ref, rows_ref, zero_ref, sem, *,
                     tm, n_tiles):
    tail = zero_ref.shape[0]

    def zero_fill(start):
        return pltpu.make_async_copy(
            zero_ref, xs_ref.at[pl.ds(pl.multiple_of(start, SLAB), tail)], sem)

    @pl.when(pl.program_id(0) == 0)
    def _():
        zero_ref[...] = jnp.zeros_like(zero_ref)
        for parity in range(2):
            tails = [zero_fill(padstart_ref[c]) for c in range(parity, N_CLASSES, 2)]
            for t in tails:
                t.start()
            for t in tails:
                t.wait()

        def fill_unused(j, carry):
            fill = zero_fill(j * tail)
            fill.start()
            fill.wait()
            return carry

        lax.fori_loop(nu_ref[0], n_tiles + 1, fill_unused, 0)

    _to_slabs(h1_ref[...], rows_ref, tm)

    def issue(r0, carry):
        for k in range(DMA_BURST):
            r = r0 * DMA_BURST + k
            _slab_copy(rows_ref, r * SLAB, xs_ref, pos_ref[0, 0, r], sem).start(priority=k % 2)
        return carry

    lax.fori_loop(0, tm // DMA_BURST, issue, 0)
    pltpu.make_async_copy(rows_ref, xs_ref.at[pl.ds(0, tm * SLAB)], sem).wait()


def _dispatch(h1, pos8, pad_start8, n_used, n_tiles, tm):
    T, D = h1.shape
    pos8 = pos8.reshape(T // tm, 1, tm)
    n_rows = (n_tiles + 1) * MOE_TILE_ROWS
    grid_spec = pltpu.PrefetchScalarGridSpec(
        num_scalar_prefetch=2,
        grid=(T // tm,),
        in_specs=[pl.BlockSpec((1, 1, tm), lambda t, ps, nu: (t, 0, 0),
                               memory_space=pltpu.SMEM),
                  pl.BlockSpec((tm, D), lambda t, ps, nu: (t, 0))],
        out_specs=pl.BlockSpec(memory_space=pl.ANY),
        scratch_shapes=[pltpu.VMEM((tm * SLAB, LANES), F32),
                        pltpu.VMEM((MOE_TILE_ROWS * SLAB, LANES), F32),
                        pltpu.SemaphoreType.DMA],
    )
    return pl.pallas_call(
        functools.partial(_dispatch_kernel, tm=tm, n_tiles=n_tiles),
        grid_spec=grid_spec,
        out_shape=jax.ShapeDtypeStruct((n_rows * SLAB, LANES), F32),
        compiler_params=_cparams(("arbitrary",)),
        name="moe_dispatch",
    )(pad_start8, n_used, pos8, h1)


def _experts_kernel(tlo_ref, thi_ref, nu_ref, x_ref, wgu_lo_ref, wgu_hi_ref, wd_lo_ref, wd_hi_ref,
                    y_ref):
    j = pl.program_id(0)
    tr = MOE_TILE_ROWS

    @pl.when(j < nu_ref[0])
    def _():
        for s in range(TOP_K * SLAB, PAIR_PITCH):
            y_ref[pl.ds(s, tr, stride=PAIR_PITCH), :] = jnp.zeros((tr, LANES), F32)
        x = _from_slabs(x_ref, tr).astype(BF16)
        for k, (wgu_ref, wd_ref) in enumerate(((wgu_lo_ref, wd_lo_ref), (wgu_hi_ref, wd_hi_ref))):
            gu = _dot(x, wgu_ref[0])
            g = gu[:, :D_EXPERT]
            u = gu[:, D_EXPERT:]
            hid = g * (1.0 / (1.0 + jnp.exp(-g))) * u
            y = _dot(hid.astype(BF16), wd_ref[0])
            for s in range(SLAB):
                y_ref[pl.ds(k * SLAB + s, tr, stride=PAIR_PITCH), :] = (
                    y[:, s * LANES:(s + 1) * LANES])

    @pl.when(j >= nu_ref[0])
    def _():
        y_ref[...] = jnp.zeros_like(y_ref)


def _experts(xs, tile_lo, tile_hi, n_used, p, n_tiles):
    D = D_MODEL
    tr = MOE_TILE_ROWS
    last_used = lambda j, nu: jnp.minimum(j, nu[0] - 1)
    wgu = lambda pick: pl.BlockSpec((1, D, 2 * D_EXPERT),
                                    lambda j, tlo, thi, nu: (pick(tlo, thi)[j], 0, 0))
    wd = lambda pick: pl.BlockSpec((1, D_EXPERT, D),
                                   lambda j, tlo, thi, nu: (pick(tlo, thi)[j], 0, 0))
    lo = lambda tlo, thi: tlo
    hi = lambda tlo, thi: thi
    grid_spec = pltpu.PrefetchScalarGridSpec(
        num_scalar_prefetch=3,
        grid=(n_tiles,),
        in_specs=[pl.BlockSpec((tr * SLAB, LANES), lambda j, tlo, thi, nu: (last_used(j, nu), 0)),
                  wgu(lo), wgu(hi), wd(lo), wd(hi)],
        out_specs=pl.BlockSpec((tr * PAIR_PITCH, LANES), lambda j, tlo, thi, nu: (j, 0)),
    )
    return pl.pallas_call(
        _experts_kernel,
        grid_spec=grid_spec,
        out_shape=jax.ShapeDtypeStruct((n_tiles * tr * PAIR_PITCH, LANES), F32),
        compiler_params=_cparams(("arbitrary",)),
        name="moe_experts",
    )(tile_lo, tile_hi, n_used, xs, p["w_gu"], p["w_gu"], p["w_down"], p["w_down"])


def _combine_kernel(pos_ref, pos_next_ref, h1_ref, route_ref, ys_ref, g2_ref, b2_ref, o_ref,
                    ybuf_ref, sem, *, tm):
    pair = TOP_K * SLAB
    t = pl.program_id(0)
    slot = t % 2

    def gather(idx_ref, dst_slot):
        def issue(r0, carry):
            for k in range(DMA_BURST):
                r = r0 * DMA_BURST + k
                src = pl.multiple_of(idx_ref[0, 0, r] * (PAIR_PITCH // SLAB), SLAB)
                pltpu.make_async_copy(
                    ys_ref.at[pl.ds(src, pair)],
                    ybuf_ref.at[dst_slot, pl.ds(pl.multiple_of(r * pair, pair), pair)],
                    sem.at[dst_slot]).start(priority=k % 2)
            return carry

        lax.fori_loop(0, tm // DMA_BURST, issue, 0)

    @pl.when(t == 0)
    def _():
        gather(pos_ref, 0)

    @pl.when(t + 1 < pl.num_programs(0))
    def _():
        gather(pos_next_ref, 1 - slot)

    pltpu.make_async_copy(ys_ref.at[pl.ds(0, tm * pair)], ybuf_ref.at[slot], sem.at[slot]).wait()

    def expert_out(k):
        return jnp.concatenate([ybuf_ref[slot, pl.ds(k * SLAB + s, tm, stride=pair), :]
                                for s in range(SLAB)], axis=1)

    route = route_ref[...]
    w_lo = route[:, ROUTE_W_LO:ROUTE_W_LO + 1]
    w_hi = route[:, ROUTE_W_HI:ROUTE_W_HI + 1]
    ffn = w_lo * expert_out(0) + w_hi * expert_out(1)
    o_ref[...] = _layer_norm(ALPHA * h1_ref[...] + ffn, g2_ref[...], b2_ref[...])


def _combine(h1, route, pos8, ys, p, tm):
    T, D = h1.shape
    n_steps = T // tm
    pos8 = pos8.reshape(n_steps, 1, tm)
    full = lambda shape: pl.BlockSpec(shape, lambda t: (0,) * len(shape))
    return pl.pallas_call(
        functools.partial(_combine_kernel, tm=tm),
        grid=(n_steps,),
        in_specs=[pl.BlockSpec((1, 1, tm), lambda t: (t, 0, 0), memory_space=pltpu.SMEM),
                  pl.BlockSpec((1, 1, tm), lambda t: (jnp.minimum(t + 1, n_steps - 1), 0, 0),
                               memory_space=pltpu.SMEM),
                  pl.BlockSpec((tm, D), lambda t: (t, 0)),
                  pl.BlockSpec((tm, ROUTE_LANES), lambda t: (t, 0)),
                  pl.BlockSpec(memory_space=pl.ANY),
                  full((1, D)), full((1, D))],
        out_specs=pl.BlockSpec((tm, D), lambda t: (t, 0)),
        out_shape=jax.ShapeDtypeStruct((T, D), F32),
        scratch_shapes=[pltpu.VMEM((2, tm * TOP_K * SLAB, LANES), F32),
                        pltpu.SemaphoreType.DMA((2,))],
        compiler_params=_cparams(("arbitrary",)),
        name="moe_combine_ln2",
    )(pos8, pos8, h1, route, ys, p["ln2_g"], p["ln2_b"])


def _dispatch_plan(route, counts):
    T = route.shape[0]
    tr = MOE_TILE_ROWS
    n_tiles = T // tr + N_CLASSES
    cnt = counts[0, :N_CLASSES].astype(jnp.int32)
    padded = jnp.maximum((cnt + tr - 1) // tr, 1) * tr
    seg_end = jnp.cumsum(padded)
    seg_start = seg_end - padded
    cls = route[:, ROUTE_CLS].astype(jnp.int32)
    rank = route[:, ROUTE_RANK].astype(jnp.int32)
    classes = jnp.arange(N_CLASSES, dtype=jnp.int32)
    start_of = jnp.sum(jnp.where(cls[:, None] == classes, seg_start, 0), axis=-1)
    pos8 = (start_of + rank) * SLAB
    tile_row0 = jnp.arange(n_tiles, dtype=jnp.int32) * tr
    tile_cls = jnp.sum((seg_end[None, :] <= tile_row0[:, None]).astype(jnp.int32), axis=-1)
    tile_cls = jnp.minimum(tile_cls, N_CLASSES - 1)
    pairs = [(a, b) for a in range(EXPERTS_PER_GROUP) for b in range(a + 1, EXPERTS_PER_GROUP)]
    lo_of = jnp.asarray([g * EXPERTS_PER_GROUP + a for g in range(N_GROUPS) for a, _ in pairs],
                        jnp.int32)
    hi_of = jnp.asarray([g * EXPERTS_PER_GROUP + b for g in range(N_GROUPS) for _, b in pairs],
                        jnp.int32)
    is_cls = tile_cls[:, None] == classes
    tile_lo = jnp.sum(jnp.where(is_cls, lo_of, 0), axis=-1)
    tile_hi = jnp.sum(jnp.where(is_cls, hi_of, 0), axis=-1)
    n_used = (seg_end[-1:] // tr).astype(jnp.int32)
    return pos8, (seg_start + cnt) * SLAB, tile_lo, tile_hi, n_used, n_tiles


def _rope_tables(pos):
    d = SWA_HEAD_DIM
    inv_freq = ROPE_THETA ** (-jnp.arange(0, d, 2, dtype=F32) / d)
    ang = pos[:, None] * inv_freq[None, :]
    c, s = jnp.cos(ang), jnp.sin(ang)
    return (jnp.concatenate([c, c, c, c], axis=-1), jnp.concatenate([-s, s, -s, s], axis=-1),
            jnp.concatenate([c, c], axis=-1).T, jnp.concatenate([s, s], axis=-1).T)


def _prep_params(ln_in_g, ln_in_b, w_in, swa_sinks, mla_q_norm_g, mla_w_uq, mla_kv_norm_g, mla_w_ukv,
                 swa_out_norm_g, mla_out_norm_g, w_o, ln1_g, ln1_b, moe_w_group, moe_b_group,
                 moe_w_router, moe_b_router, moe_w_gate, moe_w_up, moe_w_down, ln2_g, ln2_b):
    order = jnp.asarray(SWA_HEAD_ORDER)
    perm = (order[:, None] * SWA_HEAD_DIM + jnp.arange(SWA_HEAD_DIM)[None, :]).reshape(-1)
    w = w_in[0]
    w_q = w[:, :SWA_Q_COLS][:, perm]
    w_pad = jnp.zeros((D_MODEL, IN_COLS_PADDED - w.shape[1]), w.dtype)
    w_in_p = jnp.concatenate([w_q, w[:, SWA_Q_COLS:], w_pad], axis=1).astype(BF16)

    dqk = MLA_NOPE_DIM + MLA_ROPE_DIM
    wuq = mla_w_uq[0].reshape(MLA_Q_RANK, MLA_HEADS, dqk)
    wuq = jnp.pad(wuq, ((0, 0), (0, 0), (0, MLA_QK_PAD - dqk)))
    wuq_t = jnp.transpose(wuq, (1, 2, 0)).astype(BF16)
    wukv = mla_w_ukv[0].reshape(MLA_KV_RANK, MLA_HEADS, MLA_NOPE_DIM + MLA_V_DIM)
    wuk = wukv[:, :, :MLA_NOPE_DIM].reshape(MLA_KV_RANK, MLA_HEADS * MLA_NOPE_DIM).astype(BF16)
    wuv_t = jnp.transpose(wukv[:, :, MLA_NOPE_DIM:], (1, 2, 0)).astype(BF16)

    w_o0 = w_o[0]
    w_o_p = jnp.concatenate([w_o0[:SWA_Q_COLS][perm], w_o0[SWA_Q_COLS:]], axis=0).astype(BF16)

    w_route = jnp.zeros((D_MODEL, ROUTE_LANES), F32)
    w_route = w_route.at[:, :N_GROUPS].set(moe_w_group[0])
    w_route = w_route.at[:, EXPERT_LANE0:EXPERT_LANE0 + N_EXPERTS].set(moe_w_router[0])
    w_route_hi = w_route.astype(BF16)
    w_route_lo = (w_route - w_route_hi.astype(F32)).astype(BF16)
    w_route = jnp.concatenate([w_route_hi, w_route_lo], axis=1)
    b_route = jnp.zeros((1, ROUTE_LANES), F32)
    b_route = b_route.at[0, :N_GROUPS].set(moe_b_group[0])
    b_route = b_route.at[0, EXPERT_LANE0:EXPERT_LANE0 + N_EXPERTS].set(moe_b_router[0])

    return {
        "ln_in_g": ln_in_g[None, :], "ln_in_b": ln_in_b[None, :],
        "w_in": w_in_p,
        "sinks": swa_sinks[0],
        "gq": mla_q_norm_g, "w_uq_t": wuq_t,
        "gkv": mla_kv_norm_g, "w_uk": wuk, "w_uv_t": wuv_t,
        "ga": swa_out_norm_g[0][perm][None, :], "gb": mla_out_norm_g,
        "w_o": w_o_p, "ln1_g": ln1_g, "ln1_b": ln1_b,
        "w_route": w_route, "b_route": b_route,
        "w_gu": jnp.concatenate([moe_w_gate[0], moe_w_up[0]], axis=-1).astype(BF16),
        "w_down": moe_w_down[0].astype(BF16),
        "ln2_g": ln2_g, "ln2_b": ln2_b,
    }


def kernel(x, meta_tokens, ln_in_g, ln_in_b, w_in, swa_sinks, mla_q_norm_g, mla_w_uq, mla_kv_norm_g, mla_w_ukv, swa_out_norm_g, mla_out_norm_g, w_o, ln1_g, ln1_b, moe_w_group, moe_b_group, moe_w_router, moe_b_router, moe_w_gate, moe_w_up, moe_w_down, ln2_g, ln2_b):
    B, S, D = x.shape
    p = _prep_params(ln_in_g, ln_in_b, w_in, swa_sinks, mla_q_norm_g, mla_w_uq, mla_kv_norm_g,
                     mla_w_ukv, swa_out_norm_g, mla_out_norm_g, w_o, ln1_g, ln1_b, moe_w_group,
                     moe_b_group, moe_w_router, moe_b_router, moe_w_gate, moe_w_up, moe_w_down,
                     ln2_g, ln2_b)
    T = B * S
    tm = min(PROJ_TILE, S)

    x_meta = jnp.concatenate([jnp.zeros((N_PAD, D), x.dtype), meta_tokens.astype(x.dtype)])[None]
    pos_meta = jnp.maximum(jnp.arange(BLOCK) - N_PAD, 0).astype(F32)
    pos_tok = (jnp.arange(S) + N_META).astype(F32)
    _, ka_m, va_m, _, km_m, vt_m = _in_proj(x_meta, _rope_tables(pos_meta), p, BLOCK)
    a_out, qt, km, vt = _in_proj_swa(x, _rope_tables(pos_tok), p, ka_m, va_m, tm)
    b_out = _mla(qt, km, vt, km_m[:, :, N_PAD:, :], vt_m[..., N_PAD:], tm)
    h1, route, counts = _out_proj(x, a_out, b_out, p, min(WIDE_TILE, S))
    h1 = h1.reshape(B * S, D)
    route = route.reshape(B * S, ROUTE_LANES)

    pos8, pad_start8, tile_lo, tile_hi, n_used, n_tiles = _dispatch_plan(route, counts)
    xs = _dispatch(h1, pos8, pad_start8, n_used, n_tiles, min(DISPATCH_TILE, T))
    ys = _experts(xs, tile_lo, tile_hi, n_used, p, n_tiles)
    out = _combine(h1, route, pos8, ys, p, min(COMBINE_TILE, T))
    return out.reshape(B, S, D)
```

```python
import functools

import jax
import jax.numpy as jnp
from jax import lax
from jax.experimental import pallas as pl
from jax.experimental.pallas import tpu as pltpu

D_MODEL = 1024
N_META = 16
BLOCK = 128
N_PAD = BLOCK - N_META
ROPE_THETA = 10000.0
SWA_HEADS = 8
SWA_KV_HEADS = 2
SWA_HEAD_DIM = 64
MLA_HEADS = 4
MLA_Q_RANK = 256
MLA_KV_RANK = 256
MLA_NOPE_DIM = 128
MLA_ROPE_DIM = 64
MLA_V_DIM = 128
SWA_Q_COLS = SWA_HEADS * SWA_HEAD_DIM
SWA_KV_COLS = SWA_KV_HEADS * SWA_HEAD_DIM
MLA_OUT_COLS = MLA_HEADS * MLA_V_DIM
N_GROUPS = 4
EXPERTS_PER_GROUP = 8
N_EXPERTS = N_GROUPS * EXPERTS_PER_GROUP
D_EXPERT = 256
LN_EPS = 1e-5
RMS_EPS = 1e-6
DEPTH = 1
ALPHA = (2.0 * DEPTH) ** 0.25
NEG = -1e30
LOG2_E = 1.4426950408889634

LANES = 128
IN_COLS_PADDED = 1408
MLA_QK_PAD = 256
ROUTE_LANES = 128
EXPERT_LANE0 = 32
ROUTE_CLS, ROUTE_W_LO, ROUTE_W_HI, ROUTE_RANK = range(4)
TOP_K = 2
PAIRS_PER_GROUP = EXPERTS_PER_GROUP * (EXPERTS_PER_GROUP - 1) // 2
N_CLASSES = N_GROUPS * PAIRS_PER_GROUP
MOE_TILE_ROWS = 256
PROJ_TILE = 1024
WIDE_TILE = 1024
DISPATCH_TILE = 2048
COMBINE_TILE = 1024
DMA_BURST = 8
VMEM_LIMIT = 56 * 1024 * 1024

SWA_HEAD_ORDER = (0, 4, 1, 5, 2, 6, 3, 7)

BF16 = jnp.bfloat16
F32 = jnp.float32


def _cparams(sem):
    return pltpu.CompilerParams(dimension_semantics=sem, vmem_limit_bytes=VMEM_LIMIT)


def _layer_norm(x, g, b):
    mu = jnp.mean(x, axis=-1, keepdims=True)
    xc = x - mu
    var = jnp.mean(xc * xc, axis=-1, keepdims=True)
    return xc * lax.rsqrt(var + LN_EPS) * g + b


def _rms_scale(x):
    return lax.rsqrt(jnp.mean(x * x, axis=-1, keepdims=True) + RMS_EPS)


def _rope128(x, cos, sin_signed, lane_lo):
    up = pltpu.roll(x, LANES - 32, 1)
    dn = pltpu.roll(x, 32, 1)
    return x * cos + jnp.where(lane_lo, up, dn) * sin_signed


def _dot(a, b):
    return jnp.dot(a, b, preferred_element_type=F32)


def _dot_nt(a, b):
    return lax.dot_general(a, b, (((1,), (1,)), ((), ())), preferred_element_type=F32)


def _in_proj_kernel(x_ref, g_ref, b_ref, w_ref, cos_ref, sin_ref, cost_ref, sint_ref, gq_ref,
                    wuqt_ref, gkv_ref, wuk_ref, wuvt_ref,
                    qa_ref, ka_ref, va_ref, qt_ref, km_ref, vt_ref):
    h = _layer_norm(x_ref[0], g_ref[...], b_ref[...])
    u = _dot(h.astype(BF16), w_ref[...])
    cos = cos_ref[...]
    sin = sin_ref[...]
    lane = lax.broadcasted_iota(jnp.int32, (1, LANES), 1)
    lane_lo = (lane % 64) < 32
    rope = functools.partial(_rope128, cos=cos, sin_signed=sin, lane_lo=lane_lo)

    swa_scale = SWA_HEAD_DIM ** -0.5
    for c in range(SWA_Q_COLS // LANES):
        qc = rope(u[:, c * LANES:(c + 1) * LANES]) * swa_scale
        qa_ref[0, :, c * LANES:(c + 1) * LANES] = qc.astype(BF16)
    ka_ref[0] = rope(u[:, 512:640]).astype(BF16)
    va_ref[0] = u[:, 640:768].astype(BF16)

    cq = u[:, 768:1024]
    cqn = (cq * _rms_scale(cq) * gq_ref[...]).astype(BF16)
    ckv = u[:, 1024:1280]
    ckvn = (ckv * _rms_scale(ckv) * gkv_ref[...]).astype(BF16)
    k_nope = _dot(ckvn, wuk_ref[...])
    kr = rope(u[:, 1280:1408]).astype(BF16)
    q_scale = (MLA_NOPE_DIM + MLA_ROPE_DIM) ** -0.5 * LOG2_E
    cost = cost_ref[...]
    sint = sint_ref[...]
    half = MLA_ROPE_DIM // 2
    for hd in range(MLA_HEADS):
        qt = _dot_nt(wuqt_ref[hd], cqn)
        qr = qt[MLA_NOPE_DIM:MLA_NOPE_DIM + MLA_ROPE_DIM]
        rot = jnp.concatenate([-qr[half:], qr[:half]], axis=0)
        qt_ref[0, hd, 0:MLA_NOPE_DIM, :] = (qt[:MLA_NOPE_DIM] * q_scale).astype(BF16)
        qt_ref[0, hd, MLA_NOPE_DIM:MLA_NOPE_DIM + MLA_ROPE_DIM, :] = (
            (qr * cost + rot * sint) * q_scale).astype(BF16)
        qt_ref[0, hd, MLA_NOPE_DIM + MLA_ROPE_DIM:, :] = (
            qt[MLA_NOPE_DIM + MLA_ROPE_DIM:]).astype(BF16)
        km_ref[0, hd, :, 0:LANES] = k_nope[:, hd * LANES:(hd + 1) * LANES].astype(BF16)
        km_ref[0, hd, :, LANES:2 * LANES] = kr
        vt_ref[0, hd, 0] = _dot_nt(wuvt_ref[hd], ckvn).astype(BF16)


def _in_proj(x, tables, p, tm):
    cos, sin, cos_t, sin_t = tables
    B, S, D = x.shape
    grid = (B, S // tm)
    full = lambda shape: pl.BlockSpec(shape, lambda b, i: (0,) * len(shape))
    out_shape = (
        jax.ShapeDtypeStruct((B, S, SWA_Q_COLS), BF16),
        jax.ShapeDtypeStruct((B, S, SWA_KV_COLS), BF16),
        jax.ShapeDtypeStruct((B, S, SWA_KV_COLS), BF16),
        jax.ShapeDtypeStruct((B, MLA_HEADS, MLA_QK_PAD, S), BF16),
        jax.ShapeDtypeStruct((B, MLA_HEADS, S, MLA_QK_PAD), BF16),
        jax.ShapeDtypeStruct((B, MLA_HEADS, S // tm, MLA_V_DIM, tm), BF16),
    )
    row = lambda w: pl.BlockSpec((1, tm, w), lambda b, i: (b, i, 0))
    head = lambda w: pl.BlockSpec((1, MLA_HEADS, tm, w), lambda b, i: (b, 0, i, 0))
    return pl.pallas_call(
        _in_proj_kernel,
        grid=grid,
        in_specs=[
            row(D),
            full((1, D)), full((1, D)),
            full((D, IN_COLS_PADDED)),
            pl.BlockSpec((tm, LANES), lambda b, i: (i, 0)),
            pl.BlockSpec((tm, LANES), lambda b, i: (i, 0)),
            pl.BlockSpec((MLA_ROPE_DIM, tm), lambda b, i: (0, i)),
            pl.BlockSpec((MLA_ROPE_DIM, tm), lambda b, i: (0, i)),
            full((1, MLA_Q_RANK)), full((MLA_HEADS, MLA_QK_PAD, MLA_Q_RANK)),
            full((1, MLA_KV_RANK)), full((MLA_KV_RANK, MLA_HEADS * MLA_NOPE_DIM)),
            full((MLA_HEADS, MLA_V_DIM, MLA_KV_RANK)),
        ],
        out_specs=(row(SWA_Q_COLS), row(SWA_KV_COLS), row(SWA_KV_COLS),
                   pl.BlockSpec((1, MLA_HEADS, MLA_QK_PAD, tm), lambda b, i: (b, 0, 0, i)),
                   head(MLA_QK_PAD),
                   pl.BlockSpec((1, MLA_HEADS, 1, MLA_V_DIM, tm), lambda b, i: (b, 0, i, 0, 0))),
        out_shape=out_shape,
        compiler_params=_cparams(("parallel", "parallel")),
        name="in_proj",
    )(x, p["ln_in_g"], p["ln_in_b"], p["w_in"], cos, sin, cos_t, sin_t, p["gq"], p["w_uq_t"],
      p["gkv"], p["w_uk"], p["w_uv_t"])


def _swa_blocks(sink_ref, qa_ref, kc_ref, vc_ref, kprev, vprev, first, o_ref, blocks_per_step):
    row = lax.broadcasted_iota(jnp.int32, (2 * BLOCK, 2 * BLOCK), 0) % BLOCK
    col = lax.broadcasted_iota(jnp.int32, (2 * BLOCK, 2 * BLOCK), 1)
    base_mask = ((col < BLOCK) & (col > row)) | ((col >= BLOCK) & ((col - BLOCK) <= row))
    first_mask = base_mask & (col >= jnp.where(first, N_PAD, 0))
    lane = lax.broadcasted_iota(jnp.int32, (1, LANES), 1)
    lane_kv0 = lane < SWA_HEAD_DIM
    top_rows = lax.broadcasted_iota(jnp.int32, (2 * BLOCK, 1), 0) < BLOCK

    for r in range(blocks_per_step):
        rows = slice(r * BLOCK, (r + 1) * BLOCK)
        if r == 0:
            kp, vp = kprev, vprev
            mask = first_mask
        else:
            prev_rows = slice((r - 1) * BLOCK, r * BLOCK)
            kp, vp = kc_ref[0, prev_rows, :], vc_ref[0, prev_rows, :]
            mask = base_mask
        keys = jnp.concatenate([kp, kc_ref[0, rows, :]], axis=0)
        vals = jnp.concatenate([vp, vc_ref[0, rows, :]], axis=0)
        for c in range(SWA_Q_COLS // LANES):
            qc = qa_ref[0, rows, c * LANES:(c + 1) * LANES]
            zero = jnp.zeros_like(qc)
            q2 = jnp.concatenate([jnp.where(lane_kv0, qc, zero),
                                  jnp.where(lane_kv0, zero, qc)], axis=0)
            s = jnp.where(mask, _dot_nt(q2, keys), NEG)
            sink = jnp.where(top_rows, sink_ref[c], sink_ref[c + 4])
            m = jnp.maximum(jnp.max(s, axis=-1, keepdims=True), sink)
            e = jnp.exp(s - m)
            denom = jnp.sum(e, axis=-1, keepdims=True) + jnp.exp(sink - m)
            pr = (e * (1.0 / denom)).astype(BF16)
            o2 = _dot(pr, vals)
            o = jnp.where(lane_kv0, o2[:BLOCK], o2[BLOCK:])
            o_ref[0, rows, c * LANES:(c + 1) * LANES] = o.astype(BF16)


def _in_proj_swa_kernel(sink_ref, x_ref, g_ref, b_ref, w_ref, cos_ref, sin_ref, cost_ref, sint_ref,
                        gq_ref, wuqt_ref, gkv_ref, wuk_ref, wuvt_ref, kmeta_ref, vmeta_ref,
                        a_ref, qt_ref, km_ref, vt_ref,
                        qa_s, ka_s, va_s, kprev_s, vprev_s):
    _in_proj_kernel(x_ref, g_ref, b_ref, w_ref, cos_ref, sin_ref, cost_ref, sint_ref, gq_ref,
                    wuqt_ref, gkv_ref, wuk_ref, wuvt_ref, qa_s, ka_s, va_s, qt_ref, km_ref, vt_ref)
    first = pl.program_id(1) == 0
    kprev = jnp.where(first, kmeta_ref[0], kprev_s[...])
    vprev = jnp.where(first, vmeta_ref[0], vprev_s[...])
    tm = qa_s.shape[1]
    _swa_blocks(sink_ref, qa_s, ka_s, va_s, kprev, vprev, first, a_ref, tm // BLOCK)
    kprev_s[...] = ka_s[0, tm - BLOCK:, :]
    vprev_s[...] = va_s[0, tm - BLOCK:, :]


def _in_proj_swa(x, tables, p, ka_meta, va_meta, tm):
    cos, sin, cos_t, sin_t = tables
    B, S, D = x.shape
    kvw = SWA_KV_COLS
    full = lambda shape: pl.BlockSpec(shape, lambda b, i: (0,) * len(shape))
    row = lambda w: pl.BlockSpec((1, tm, w), lambda b, i: (b, i, 0))
    out_shape = (
        jax.ShapeDtypeStruct((B, S, SWA_Q_COLS), BF16),
        jax.ShapeDtypeStruct((B, MLA_HEADS, MLA_QK_PAD, S), BF16),
        jax.ShapeDtypeStruct((B, MLA_HEADS, S, MLA_QK_PAD), BF16),
        jax.ShapeDtypeStruct((B, MLA_HEADS, S // tm, MLA_V_DIM, tm), BF16),
    )
    return pl.pallas_call(
        _in_proj_swa_kernel,
        grid=(B, S // tm),
        in_specs=[
            pl.BlockSpec(memory_space=pltpu.SMEM),
            row(D),
            full((1, D)), full((1, D)),
            full((D, IN_COLS_PADDED)),
            pl.BlockSpec((tm, LANES), lambda b, i: (i, 0)),
            pl.BlockSpec((tm, LANES), lambda b, i: (i, 0)),
            pl.BlockSpec((MLA_ROPE_DIM, tm), lambda b, i: (0, i)),
            pl.BlockSpec((MLA_ROPE_DIM, tm), lambda b, i: (0, i)),
            full((1, MLA_Q_RANK)), full((MLA_HEADS, MLA_QK_PAD, MLA_Q_RANK)),
            full((1, MLA_KV_RANK)), full((MLA_KV_RANK, MLA_HEADS * MLA_NOPE_DIM)),
            full((MLA_HEADS, MLA_V_DIM, MLA_KV_RANK)),
            full((1, BLOCK, kvw)), full((1, BLOCK, kvw)),
        ],
        out_specs=(row(SWA_Q_COLS),
                   pl.BlockSpec((1, MLA_HEADS, MLA_QK_PAD, tm), lambda b, i: (b, 0, 0, i)),
                   pl.BlockSpec((1, MLA_HEADS, tm, MLA_QK_PAD), lambda b, i: (b, 0, i, 0)),
                   pl.BlockSpec((1, MLA_HEADS, 1, MLA_V_DIM, tm), lambda b, i: (b, 0, i, 0, 0))),
        out_shape=out_shape,
        scratch_shapes=[pltpu.VMEM((1, tm, SWA_Q_COLS), BF16), pltpu.VMEM((1, tm, kvw), BF16),
                        pltpu.VMEM((1, tm, kvw), BF16), pltpu.VMEM((BLOCK, kvw), BF16),
                        pltpu.VMEM((BLOCK, kvw), BF16)],
        compiler_params=_cparams(("parallel", "arbitrary")),
        name="in_proj_swa",
    )(p["sinks"], x, p["ln_in_g"], p["ln_in_b"], p["w_in"], cos, sin, cos_t, sin_t, p["gq"],
      p["w_uq_t"], p["gkv"], p["w_uk"], p["w_uv_t"], ka_meta, va_meta)


def _mla_kernel(qt_ref, k_ref, vt_ref, kmeta_ref, vtmeta_ref, o_ref,
                s0a_ref, s0b_ref, s1a_ref, s1b_ref, smax_ref, m_ref, l_ref, acc_ref, *, tq):
    u = pl.program_id(2)
    qts = (qt_ref[0, 0, :, :tq], qt_ref[0, 0, :, tq:])
    bufs = ((s0a_ref, s0b_ref), (s1a_ref, s1b_ref))

    for g in range(2):
        s = _dot(kmeta_ref[0, 0], qts[g])
        m = jnp.max(s, axis=0, keepdims=True)
        p = jnp.exp2(s - m)
        m_ref[g] = m
        l_ref[g] = jnp.sum(p, axis=0, keepdims=True)
        acc_ref[g] = _dot(vtmeta_ref[0, 0, 0], p.astype(BF16))

    def scores_into(g, ab, j):
        start = pl.multiple_of(j * tq, tq)
        s = _dot(k_ref[0, 0, pl.ds(start, tq), :], qts[g])
        bufs[g][ab][...] = s
        smax_ref[g, ab] = jnp.max(s, axis=0, keepdims=True)

    def absorb(g, ab, j, causal=False):
        s = bufs[g][ab][...]
        if causal:
            key = lax.broadcasted_iota(jnp.int32, (tq, tq), 0)
            query = lax.broadcasted_iota(jnp.int32, (tq, tq), 1)
            s = jnp.where(key <= query, s, NEG)
            smax = jnp.max(s, axis=0, keepdims=True)
        else:
            smax = smax_ref[g, ab]
        m = m_ref[g]
        m_new = jnp.maximum(m, smax)
        alpha = jnp.exp2(m - m_new)
        p = jnp.exp2(s - m_new)
        m_ref[g] = m_new
        l_ref[g] = alpha * l_ref[g] + jnp.sum(p, axis=0, keepdims=True)
        acc_ref[g] = alpha * acc_ref[g] + _dot(vt_ref[0, 0, j], p.astype(BF16))

    buf_a, buf_b = 0, 1
    for g in range(2):
        scores_into(g, buf_a, 0)

    def pair(t, carry):
        for g in range(2):
            scores_into(g, buf_b, 2 * t + 1)
            absorb(g, buf_a, 2 * t)
        for g in range(2):
            scores_into(g, buf_a, 2 * t + 2)
            absorb(g, buf_b, 2 * t + 1)
        return carry

    lax.fori_loop(0, u, pair, 0)

    scores_into(1, buf_b, 2 * u + 1)
    absorb(1, buf_a, 2 * u)
    absorb(0, buf_a, 2 * u, causal=True)
    absorb(1, buf_b, 2 * u + 1, causal=True)

    for g in range(2):
        o_ref[0, g * tq:(g + 1) * tq, :] = (acc_ref[g] * (1.0 / l_ref[g])).T.astype(BF16)


def _mla(qt, km, vt, km_meta, vt_meta, tq):
    B, H, _, S = qt.shape
    assert S % (2 * tq) == 0, "a step handles an (even, odd) pair of query tiles"
    return pl.pallas_call(
        functools.partial(_mla_kernel, tq=tq),
        grid=(B, H, S // (2 * tq)),
        in_specs=[
            pl.BlockSpec((1, 1, MLA_QK_PAD, 2 * tq), lambda b, h, u: (b, h, 0, u)),
            pl.BlockSpec((1, 1, S, MLA_QK_PAD), lambda b, h, u: (b, h, 0, 0)),
            pl.BlockSpec((1, 1, S // tq, MLA_V_DIM, tq), lambda b, h, u: (b, h, 0, 0, 0)),
            pl.BlockSpec((1, 1, N_META, MLA_QK_PAD), lambda b, h, u: (0, h, 0, 0)),
            pl.BlockSpec((1, 1, 1, MLA_V_DIM, N_META), lambda b, h, u: (0, h, 0, 0, 0)),
        ],
        out_specs=pl.BlockSpec((1, 2 * tq, MLA_V_DIM), lambda b, h, u: (b, u, h)),
        out_shape=jax.ShapeDtypeStruct((B, S, MLA_OUT_COLS), BF16),
        scratch_shapes=[pltpu.VMEM((tq, tq), F32), pltpu.VMEM((tq, tq), F32),
                        pltpu.VMEM((tq, tq), F32), pltpu.VMEM((tq, tq), F32),
                        pltpu.VMEM((2, 2, 1, tq), F32),
                        pltpu.VMEM((2, 1, tq), F32), pltpu.VMEM((2, 1, tq), F32),
                        pltpu.VMEM((2, MLA_V_DIM, tq), F32)],
        compiler_params=_cparams(("parallel", "parallel", "arbitrary")),
        name="mla_attn",
    )(qt, km, vt, km_meta, vt_meta)


def _out_proj_kernel(x_ref, a_ref, b_ref, gin_ref, bin_ref, ga_ref, gb_ref, wo_ref, g1_ref, b1_ref,
                     wrt_ref, brt_ref, h1_ref, route_ref, counts_out_ref, count_ref):
    h = _layer_norm(x_ref[0], gin_ref[...], bin_ref[...])
    a = a_ref[0].astype(F32)
    b = b_ref[0].astype(F32)
    an = (a * _rms_scale(a) * ga_ref[...]).astype(BF16)
    bn = (b * _rms_scale(b) * gb_ref[...]).astype(BF16)
    mix = _dot(jnp.concatenate([an, bn], axis=-1), wo_ref[...])
    h1 = _layer_norm(ALPHA * h + mix, g1_ref[...], b1_ref[...])
    h1_ref[0] = h1

    x_hi = h1.astype(BF16)
    x_lo = (h1 - x_hi.astype(F32)).astype(BF16)
    hi_both = _dot(x_hi, wrt_ref[...])
    logits = (hi_both[:, :ROUTE_LANES] + hi_both[:, ROUTE_LANES:]
              + _dot(x_lo, wrt_ref[:, :ROUTE_LANES]) + brt_ref[...])
    lane = lax.broadcasted_iota(jnp.int32, logits.shape, 1).astype(F32)
    ninf = -jnp.inf
    no_lane = float(ROUTE_LANES)
    gl = jnp.where(lane < N_GROUPS, logits, ninf)
    gmax = jnp.max(gl, axis=-1, keepdims=True)
    g_top = 1.0 / jnp.sum(jnp.exp(gl - gmax), axis=-1, keepdims=True)
    g_idx = jnp.min(jnp.where(gl == gmax, lane, no_lane), axis=-1, keepdims=True)
    lo = EXPERT_LANE0 + g_idx * EXPERTS_PER_GROUP
    el = jnp.where((lane >= lo) & (lane < lo + EXPERTS_PER_GROUP), logits, ninf)
    m1 = jnp.max(el, axis=-1, keepdims=True)
    i1 = jnp.min(jnp.where(el == m1, lane, no_lane), axis=-1, keepdims=True)
    el2 = jnp.where(lane == i1, ninf, el)
    m2 = jnp.max(el2, axis=-1, keepdims=True)
    i2 = jnp.min(jnp.where(el2 == m2, lane, no_lane), axis=-1, keepdims=True)
    e2 = jnp.exp(m2 - m1)
    w1 = g_top / (1.0 + e2)
    w2 = g_top * e2 / (1.0 + e2)

    first_is_lo = i1 < i2
    a = jnp.minimum(i1, i2) - lo
    b = jnp.maximum(i1, i2) - lo
    cls = g_idx * PAIRS_PER_GROUP + a * (15.0 - a) * 0.5 + (b - a - 1.0)
    w_lo = jnp.where(first_is_lo, w1, w2)
    w_hi = jnp.where(first_is_lo, w2, w1)

    @pl.when((pl.program_id(0) == 0) & (pl.program_id(1) == 0))
    def _():
        count_ref[...] = jnp.zeros_like(count_ref)

    tm = logits.shape[0]
    onehot = jnp.where(lane == cls, 1.0, 0.0)
    r_i = lax.broadcasted_iota(jnp.int32, (BLOCK, BLOCK), 0)
    c_i = lax.broadcasted_iota(jnp.int32, (BLOCK, BLOCK), 1)
    earlier = jnp.where(r_i > c_i, 1.0, 0.0).astype(BF16)
    running = count_ref[...]
    before = []
    for blk in range(tm // BLOCK):
        oh = onehot[blk * BLOCK:(blk + 1) * BLOCK]
        before.append(_dot(earlier, oh.astype(BF16)) + running)
        running = running + jnp.sum(oh, axis=0, keepdims=True)
    before = jnp.concatenate(before, axis=0)
    rank = jnp.sum(jnp.where(lane == cls, before, 0.0), axis=-1, keepdims=True)
    count_ref[...] = running
    counts_out_ref[...] = running

    route = jnp.where(lane == ROUTE_CLS, cls, 0.0)
    route = jnp.where(lane == ROUTE_W_LO, w_lo, route)
    route = jnp.where(lane == ROUTE_W_HI, w_hi, route)
    route = jnp.where(lane == ROUTE_RANK, rank, route)
    route_ref[0] = route


def _out_proj(x, a_out, b_out, p, tm):
    B, S, D = x.shape
    full = lambda shape: pl.BlockSpec(shape, lambda b, i: (0,) * len(shape))
    row = lambda w: pl.BlockSpec((1, tm, w), lambda b, i: (b, i, 0))
    return pl.pallas_call(
        _out_proj_kernel,
        grid=(B, S // tm),
        in_specs=[row(D), row(SWA_Q_COLS), row(MLA_OUT_COLS),
                  full((1, D)), full((1, D)), full((1, SWA_Q_COLS)), full((1, MLA_OUT_COLS)),
                  full((D, D)), full((1, D)), full((1, D)),
                  full((D, 2 * ROUTE_LANES)), full((1, ROUTE_LANES))],
        out_specs=(row(D), row(ROUTE_LANES), full((1, ROUTE_LANES))),
        out_shape=(jax.ShapeDtypeStruct((B, S, D), F32),
                   jax.ShapeDtypeStruct((B, S, ROUTE_LANES), F32),
                   jax.ShapeDtypeStruct((1, ROUTE_LANES), F32)),
        scratch_shapes=[pltpu.VMEM((1, ROUTE_LANES), F32)],
        compiler_params=_cparams(("arbitrary", "arbitrary")),
        name="out_proj_route",
    )(x, a_out, b_out, p["ln_in_g"], p["ln_in_b"], p["ga"], p["gb"], p["w_o"], p["ln1_g"],
      p["ln1_b"], p["w_route"], p["b_route"])


SLAB = D_MODEL // LANES
PAIR_PITCH = 2 * SLAB


def _slab_copy(src_ref, src_tok8, dst_ref, dst_tok8, sem):
    return pltpu.make_async_copy(src_ref.at[pl.ds(pl.multiple_of(src_tok8, SLAB), SLAB)],
                                 dst_ref.at[pl.ds(pl.multiple_of(dst_tok8, SLAB), SLAB)], sem)


def _to_slabs(x, slab_ref, n):
    for s in range(SLAB):
        slab_ref[pl.ds(s, n, stride=SLAB), :] = x[:, s * LANES:(s + 1) * LANES]


def _from_slabs(slab_ref, n):
    return jnp.concatenate([slab_ref[pl.ds(s, n, stride=SLAB), :] for s in range(SLAB)], axis=1)


def _dispatch_kernel(padstart_ref, nu_ref, pos_ref, h1_ref, xs_ref, rows_ref, zero_ref, sem, *,
                     tm, n_tiles):
    tail = zero_ref.shape[0]

    def zero_fill(start):
        return pltpu.make_async_copy(
            zero_ref, xs_ref.at[pl.ds(pl.multiple_of(start, SLAB), tail)], sem)

    @pl.when(pl.program_id(0) == 0)
    def _():
        zero_ref[...] = jnp.zeros_like(zero_ref)
        for parity in range(2):
            tails = [zero_fill(padstart_ref[c]) for c in range(parity, N_CLASSES, 2)]
            for t in tails:
                t.start()
            for t in tails:
                t.wait()

        def fill_unused(j, carry):
            fill = zero_fill(j * tail)
            fill.start()
            fill.wait()
            return carry

        lax.fori_loop(nu_ref[0], n_tiles + 1, fill_unused, 0)

    _to_slabs(h1_ref[...], rows_ref, tm)

    def issue(r0, carry):
        for k in range(DMA_BURST):
            r = r0 * DMA_BURST + k
            _slab_copy(rows_ref, r * SLAB, xs_ref, pos_ref[0, 0, r], sem).start(priority=k % 2)
        return carry

    lax.fori_loop(0, tm // DMA_BURST, issue, 0)
    pltpu.make_async_copy(rows_ref, xs_ref.at[pl.ds(0, tm * SLAB)], sem).wait()


def _dispatch(h1, pos8, pad_start8, n_used, n_tiles, tm):
    T, D = h1.shape
    pos8 = pos8.reshape(T // tm, 1, tm)
    n_rows = (n_tiles + 1) * MOE_TILE_ROWS
    grid_spec = pltpu.PrefetchScalarGridSpec(
        num_scalar_prefetch=2,
        grid=(T // tm,),
        in_specs=[pl.BlockSpec((1, 1, tm), lambda t, ps, nu: (t, 0, 0),
                               memory_space=pltpu.SMEM),
                  pl.BlockSpec((tm, D), lambda t, ps, nu: (t, 0))],
        out_specs=pl.BlockSpec(memory_space=pl.ANY),
        scratch_shapes=[pltpu.VMEM((tm * SLAB, LANES), F32),
                        pltpu.VMEM((MOE_TILE_ROWS * SLAB, LANES), F32),
                        pltpu.SemaphoreType.DMA],
    )
    return pl.pallas_call(
        functools.partial(_dispatch_kernel, tm=tm, n_tiles=n_tiles),
        grid_spec=grid_spec,
        out_shape=jax.ShapeDtypeStruct((n_rows * SLAB, LANES), F32),
        compiler_params=_cparams(("arbitrary",)),
        name="moe_dispatch",
    )(pad_start8, n_used, pos8, h1)


def _experts_kernel(tlo_ref, thi_ref, nu_ref, x_ref, wgu_lo_ref, wgu_hi_ref, wd_lo_ref, wd_hi_ref,
                    y_ref):
    j = pl.program_id(0)
    tr = MOE_TILE_ROWS

    @pl.when(j < nu_ref[0])
    def _():
        for s in range(TOP_K * SLAB, PAIR_PITCH):
            y_ref[pl.ds(s, tr, stride=PAIR_PITCH), :] = jnp.zeros((tr, LANES), F32)
        x = _from_slabs(x_ref, tr).astype(BF16)
        for k, (wgu_ref, wd_ref) in enumerate(((wgu_lo_ref, wd_lo_ref), (wgu_hi_ref, wd_hi_ref))):
            gu = _dot(x, wgu_ref[0])
            g = gu[:, :D_EXPERT]
            u = gu[:, D_EXPERT:]
            hid = g * (1.0 / (1.0 + jnp.exp(-g))) * u
            y = _dot(hid.astype(BF16), wd_ref[0])
            for s in range(SLAB):
                y_ref[pl.ds(k * SLAB + s, tr, stride=PAIR_PITCH), :] = (
                    y[:, s * LANES:(s + 1) * LANES])

    @pl.when(j >= nu_ref[0])
    def _():
        y_ref[...] = jnp.zeros_like(y_ref)


def _experts(xs, tile_lo, tile_hi, n_used, p, n_tiles):
    D = D_MODEL
    tr = MOE_TILE_ROWS
    last_used = lambda j, nu: jnp.minimum(j, nu[0] - 1)
    wgu = lambda pick: pl.BlockSpec((1, D, 2 * D_EXPERT),
                                    lambda j, tlo, thi, nu: (pick(tlo, thi)[j], 0, 0))
    wd = lambda pick: pl.BlockSpec((1, D_EXPERT, D),
                                   lambda j, tlo, thi, nu: (pick(tlo, thi)[j], 0, 0))
    lo = lambda tlo, thi: tlo
    hi = lambda tlo, thi: thi
    grid_spec = pltpu.PrefetchScalarGridSpec(
        num_scalar_prefetch=3,
        grid=(n_tiles,),
        in_specs=[pl.BlockSpec((tr * SLAB, LANES), lambda j, tlo, thi, nu: (last_used(j, nu), 0)),
                  wgu(lo), wgu(hi), wd(lo), wd(hi)],
        out_specs=pl.BlockSpec((tr * PAIR_PITCH, LANES), lambda j, tlo, thi, nu: (j, 0)),
    )
    return pl.pallas_call(
        _experts_kernel,
        grid_spec=grid_spec,
        out_shape=jax.ShapeDtypeStruct((n_tiles * tr * PAIR_PITCH, LANES), F32),
        compiler_params=_cparams(("arbitrary",)),
        name="moe_experts",
    )(tile_lo, tile_hi, n_used, xs, p["w_gu"], p["w_gu"], p["w_down"], p["w_down"])


def _combine_kernel(pos_ref, pos_next_ref, h1_ref, route_ref, ys_ref, g2_ref, b2_ref, o_ref,
                    ybuf_ref, sem, *, tm):
    pair = TOP_K * SLAB
    t = pl.program_id(0)
    slot = t % 2

    def gather(idx_ref, dst_slot):
        def issue(r0, carry):
            for k in range(DMA_BURST):
                r = r0 * DMA_BURST + k
                src = pl.multiple_of(idx_ref[0, 0, r] * (PAIR_PITCH // SLAB), SLAB)
                pltpu.make_async_copy(
                    ys_ref.at[pl.ds(src, pair)],
                    ybuf_ref.at[dst_slot, pl.ds(pl.multiple_of(r * pair, pair), pair)],
                    sem.at[dst_slot]).start(priority=k % 2)
            return carry

        lax.fori_loop(0, tm // DMA_BURST, issue, 0)

    @pl.when(t == 0)
    def _():
        gather(pos_ref, 0)

    @pl.when(t + 1 < pl.num_programs(0))
    def _():
        gather(pos_next_ref, 1 - slot)

    pltpu.make_async_copy(ys_ref.at[pl.ds(0, tm * pair)], ybuf_ref.at[slot], sem.at[slot]).wait()

    def expert_out(k):
        return jnp.concatenate([ybuf_ref[slot, pl.ds(k * SLAB + s, tm, stride=pair), :]
                                for s in range(SLAB)], axis=1)

    route = route_ref[...]
    w_lo = route[:, ROUTE_W_LO:ROUTE_W_LO + 1]
    w_hi = route[:, ROUTE_W_HI:ROUTE_W_HI + 1]
    ffn = w_lo * expert_out(0) + w_hi * expert_out(1)
    o_ref[...] = _layer_norm(ALPHA * h1_ref[...] + ffn, g2_ref[...], b2_ref[...])


def _combine(h1, route, pos8, ys, p, tm):
    T, D = h1.shape
    n_steps = T // tm
    pos8 = pos8.reshape(n_steps, 1, tm)
    full = lambda shape: pl.BlockSpec(shape, lambda t: (0,) * len(shape))
    return pl.pallas_call(
        functools.partial(_combine_kernel, tm=tm),
        grid=(n_steps,),
        in_specs=[pl.BlockSpec((1, 1, tm), lambda t: (t, 0, 0), memory_space=pltpu.SMEM),
                  pl.BlockSpec((1, 1, tm), lambda t: (jnp.minimum(t + 1, n_steps - 1), 0, 0),
                               memory_space=pltpu.SMEM),
                  pl.BlockSpec((tm, D), lambda t: (t, 0)),
                  pl.BlockSpec((tm, ROUTE_LANES), lambda t: (t, 0)),
                  pl.BlockSpec(memory_space=pl.ANY),
                  full((1, D)), full((1, D))],
        out_specs=pl.BlockSpec((tm, D), lambda t: (t, 0)),
        out_shape=jax.ShapeDtypeStruct((T, D), F32),
        scratch_shapes=[pltpu.VMEM((2, tm * TOP_K * SLAB, LANES), F32),
                        pltpu.SemaphoreType.DMA((2,))],
        compiler_params=_cparams(("arbitrary",)),
        name="moe_combine_ln2",
    )(pos8, pos8, h1, route, ys, p["ln2_g"], p["ln2_b"])


def _dispatch_plan(route, counts):
    T = route.shape[0]
    tr = MOE_TILE_ROWS
    n_tiles = T // tr + N_CLASSES
    cnt = counts[0, :N_CLASSES].astype(jnp.int32)
    padded = jnp.maximum((cnt + tr - 1) // tr, 1) * tr
    seg_end = jnp.cumsum(padded)
    seg_start = seg_end - padded
    cls = route[:, ROUTE_CLS].astype(jnp.int32)
    rank = route[:, ROUTE_RANK].astype(jnp.int32)
    classes = jnp.arange(N_CLASSES, dtype=jnp.int32)
    start_of = jnp.sum(jnp.where(cls[:, None] == classes, seg_start, 0), axis=-1)
    pos8 = (start_of + rank) * SLAB
    tile_row0 = jnp.arange(n_tiles, dtype=jnp.int32) * tr
    tile_cls = jnp.sum((seg_end[None, :] <= tile_row0[:, None]).astype(jnp.int32), axis=-1)
    tile_cls = jnp.minimum(tile_cls, N_CLASSES - 1)
    pairs = [(a, b) for a in range(EXPERTS_PER_GROUP) for b in range(a + 1, EXPERTS_PER_GROUP)]
    lo_of = jnp.asarray([g * EXPERTS_PER_GROUP + a for g in range(N_GROUPS) for a, _ in pairs],
                        jnp.int32)
    hi_of = jnp.asarray([g * EXPERTS_PER_GROUP + b for g in range(N_GROUPS) for _, b in pairs],
                        jnp.int32)
    is_cls = tile_cls[:, None] == classes
    tile_lo = jnp.sum(jnp.where(is_cls, lo_of, 0), axis=-1)
    tile_hi = jnp.sum(jnp.where(is_cls, hi_of, 0), axis=-1)
    n_used = (seg_end[-1:] // tr).astype(jnp.int32)
    return pos8, (seg_start + cnt) * SLAB, tile_lo, tile_hi, n_used, n_tiles


def _rope_tables(pos):
    d = SWA_HEAD_DIM
    inv_freq = ROPE_THETA ** (-jnp.arange(0, d, 2, dtype=F32) / d)
    ang = pos[:, None] * inv_freq[None, :]
    c, s = jnp.cos(ang), jnp.sin(ang)
    return (jnp.concatenate([c, c, c, c], axis=-1), jnp.concatenate([-s, s, -s, s], axis=-1),
            jnp.concatenate([c, c], axis=-1).T, jnp.concatenate([s, s], axis=-1).T)


def _prep_params(ln_in_g, ln_in_b, w_in, swa_sinks, mla_q_norm_g, mla_w_uq, mla_kv_norm_g, mla_w_ukv,
                 swa_out_norm_g, mla_out_norm_g, w_o, ln1_g, ln1_b, moe_w_group, moe_b_group,
                 moe_w_router, moe_b_router, moe_w_gate, moe_w_up, moe_w_down, ln2_g, ln2_b):
    order = jnp.asarray(SWA_HEAD_ORDER)
    perm = (order[:, None] * SWA_HEAD_DIM + jnp.arange(SWA_HEAD_DIM)[None, :]).reshape(-1)
    w = w_in[0]
    w_q = w[:, :SWA_Q_COLS][:, perm]
    w_pad = jnp.zeros((D_MODEL, IN_COLS_PADDED - w.shape[1]), w.dtype)
    w_in_p = jnp.concatenate([w_q, w[:, SWA_Q_COLS:], w_pad], axis=1).astype(BF16)

    dqk = MLA_NOPE_DIM + MLA_ROPE_DIM
    wuq = mla_w_uq[0].reshape(MLA_Q_RANK, MLA_HEADS, dqk)
    wuq = jnp.pad(wuq, ((0, 0), (0, 0), (0, MLA_QK_PAD - dqk)))
    wuq_t = jnp.transpose(wuq, (1, 2, 0)).astype(BF16)
    wukv = mla_w_ukv[0].reshape(MLA_KV_RANK, MLA_HEADS, MLA_NOPE_DIM + MLA_V_DIM)
    wuk = wukv[:, :, :MLA_NOPE_DIM].reshape(MLA_KV_RANK, MLA_HEADS * MLA_NOPE_DIM).astype(BF16)
    wuv_t = jnp.transpose(wukv[:, :, MLA_NOPE_DIM:], (1, 2, 0)).astype(BF16)

    w_o0 = w_o[0]
    w_o_p = jnp.concatenate([w_o0[:SWA_Q_COLS][perm], w_o0[SWA_Q_COLS:]], axis=0).astype(BF16)

    w_route = jnp.zeros((D_MODEL, ROUTE_LANES), F32)
    w_route = w_route.at[:, :N_GROUPS].set(moe_w_group[0])
    w_route = w_route.at[:, EXPERT_LANE0:EXPERT_LANE0 + N_EXPERTS].set(moe_w_router[0])
    w_route_hi = w_route.astype(BF16)
    w_route_lo = (w_route - w_route_hi.astype(F32)).astype(BF16)
    w_route = jnp.concatenate([w_route_hi, w_route_lo], axis=1)
    b_route = jnp.zeros((1, ROUTE_LANES), F32)
    b_route = b_route.at[0, :N_GROUPS].set(moe_b_group[0])
    b_route = b_route.at[0, EXPERT_LANE0:EXPERT_LANE0 + N_EXPERTS].set(moe_b_router[0])

    return {
        "ln_in_g": ln_in_g[None, :], "ln_in_b": ln_in_b[None, :],
        "w_in": w_in_p,
        "sinks": swa_sinks[0],
        "gq": mla_q_norm_g, "w_uq_t": wuq_t,
        "gkv": mla_kv_norm_g, "w_uk": wuk, "w_uv_t": wuv_t,
        "ga": swa_out_norm_g[0][perm][None, :], "gb": mla_out_norm_g,
        "w_o": w_o_p, "ln1_g": ln1_g, "ln1_b": ln1_b,
        "w_route": w_route, "b_route": b_route,
        "w_gu": jnp.concatenate([moe_w_gate[0], moe_w_up[0]], axis=-1).astype(BF16),
        "w_down": moe_w_down[0].astype(BF16),
        "ln2_g": ln2_g, "ln2_b": ln2_b,
    }


def kernel(x, meta_tokens, ln_in_g, ln_in_b, w_in, swa_sinks, mla_q_norm_g, mla_w_uq, mla_kv_norm_g, mla_w_ukv, swa_out_norm_g, mla_out_norm_g, w_o, ln1_g, ln1_b, moe_w_group, moe_b_group, moe_w_router, moe_b_router, moe_w_gate, moe_w_up, moe_w_down, ln2_g, ln2_b):
    B, S, D = x.shape
    p = _prep_params(ln_in_g, ln_in_b, w_in, swa_sinks, mla_q_norm_g, mla_w_uq, mla_kv_norm_g,
                     mla_w_ukv, swa_out_norm_g, mla_out_norm_g, w_o, ln1_g, ln1_b, moe_w_group,
                     moe_b_group, moe_w_router, moe_b_router, moe_w_gate, moe_w_up, moe_w_down,
                     ln2_g, ln2_b)
    T = B * S
    tm = min(PROJ_TILE, S)

    x_meta = jnp.concatenate([jnp.zeros((N_PAD, D), x.dtype), meta_tokens.astype(x.dtype)])[None]
    pos_meta = jnp.maximum(jnp.arange(BLOCK) - N_PAD, 0).astype(F32)
    pos_tok = (jnp.arange(S) + N_META).astype(F32)
    _, ka_m, va_m, _, km_m, vt_m = _in_proj(x_meta, _rope_tables(pos_meta), p, BLOCK)
    a_out, qt, km, vt = _in_proj_swa(x, _rope_tables(pos_tok), p, ka_m, va_m, tm)
    b_out = _mla(qt, km, vt, km_m[:, :, N_PAD:, :], vt_m[..., N_PAD:], tm)
    h1, route, counts = _out_proj(x, a_out, b_out, p, min(WIDE_TILE, S))
    h1 = h1.reshape(B * S, D)
    route = route.reshape(B * S, ROUTE_LANES)

    pos8, pad_start8, tile_lo, tile_hi, n_used, n_tiles = _dispatch_plan(route, counts)
    xs = _dispatch(h1, pos8, pad_start8, n_used, n_tiles, min(DISPATCH_TILE, T))
    ys = _experts(xs, tile_lo, tile_hi, n_used, p, n_tiles)
    out = _combine(h1, route, pos8, ys, p, min(COMBINE_TILE, T))
    return out.reshape(B, S, D)
```

```python
import functools

import jax
import jax.numpy as jnp
from jax import lax
from jax.experimental import pallas as pl
from jax.experimental.pallas import tpu as pltpu

D_MODEL = 1024
N_META = 16
BLOCK = 128
N_PAD = BLOCK - N_META
ROPE_THETA = 10000.0
SWA_HEADS = 8
SWA_KV_HEADS = 2
SWA_HEAD_DIM = 64
MLA_HEADS = 4
MLA_Q_RANK = 256
MLA_KV_RANK = 256
MLA_NOPE_DIM = 128
MLA_ROPE_DIM = 64
MLA_V_DIM = 128
SWA_Q_COLS = SWA_HEADS * SWA_HEAD_DIM
SWA_KV_COLS = SWA_KV_HEADS * SWA_HEAD_DIM
MLA_OUT_COLS = MLA_HEADS * MLA_V_DIM
N_GROUPS = 4
EXPERTS_PER_GROUP = 8
N_EXPERTS = N_GROUPS * EXPERTS_PER_GROUP
D_EXPERT = 256
LN_EPS = 1e-5
RMS_EPS = 1e-6
DEPTH = 1
ALPHA = (2.0 * DEPTH) ** 0.25
NEG = -1e30
LOG2_E = 1.4426950408889634

LANES = 128
IN_COLS_PADDED = 1408
MLA_QK_PAD = 256
ROUTE_LANES = 128
EXPERT_LANE0 = 32
ROUTE_CLS, ROUTE_RANK = range(2)
PAIRS_PER_GROUP = EXPERTS_PER_GROUP * (EXPERTS_PER_GROUP - 1) // 2
N_CLASSES = N_GROUPS * PAIRS_PER_GROUP
MOE_TILE_ROWS = 256
PROJ_TILE = 1024
WIDE_TILE = 1024
DISPATCH_TILE = 2048
COMBINE_TILE = 1024
DMA_BURST = 8
VMEM_LIMIT = 56 * 1024 * 1024

SWA_HEAD_ORDER = (0, 4, 1, 5, 2, 6, 3, 7)

BF16 = jnp.bfloat16
F32 = jnp.float32


def _cparams(sem):
    return pltpu.CompilerParams(dimension_semantics=sem, vmem_limit_bytes=VMEM_LIMIT)


def _layer_norm(x, g, b):
    mu = jnp.mean(x, axis=-1, keepdims=True)
    xc = x - mu
    var = jnp.mean(xc * xc, axis=-1, keepdims=True)
    return xc * lax.rsqrt(var + LN_EPS) * g + b


def _rms_scale(x):
    return lax.rsqrt(jnp.mean(x * x, axis=-1, keepdims=True) + RMS_EPS)


def _rope128(x, cos, sin_signed, lane_lo):
    up = pltpu.roll(x, LANES - 32, 1)
    dn = pltpu.roll(x, 32, 1)
    return x * cos + jnp.where(lane_lo, up, dn) * sin_signed


def _dot(a, b):
    return jnp.dot(a, b, preferred_element_type=F32)


def _dot_nt(a, b):
    return lax.dot_general(a, b, (((1,), (1,)), ((), ())), preferred_element_type=F32)


def _in_proj_kernel(x_ref, g_ref, b_ref, w_ref, cos_ref, sin_ref, cost_ref, sint_ref, gq_ref,
                    wuqt_ref, gkv_ref, wuk_ref, wuvt_ref,
                    qa_ref, ka_ref, va_ref, qt_ref, km_ref, vt_ref):
    h = _layer_norm(x_ref[0], g_ref[...], b_ref[...])
    u = _dot(h.astype(BF16), w_ref[...])
    cos = cos_ref[...]
    sin = sin_ref[...]
    lane = lax.broadcasted_iota(jnp.int32, (1, LANES), 1)
    lane_lo = (lane % 64) < 32
    rope = functools.partial(_rope128, cos=cos, sin_signed=sin, lane_lo=lane_lo)

    swa_scale = SWA_HEAD_DIM ** -0.5
    for c in range(SWA_Q_COLS // LANES):
        qc = rope(u[:, c * LANES:(c + 1) * LANES]) * swa_scale
        qa_ref[0, :, c * LANES:(c + 1) * LANES] = qc.astype(BF16)
    ka_ref[0] = rope(u[:, 512:640]).astype(BF16)
    va_ref[0] = u[:, 640:768].astype(BF16)

    cq = u[:, 768:1024]
    cqn = (cq * _rms_scale(cq) * gq_ref[...]).astype(BF16)
    ckv = u[:, 1024:1280]
    ckvn = (ckv * _rms_scale(ckv) * gkv_ref[...]).astype(BF16)
    k_nope = _dot(ckvn, wuk_ref[...])
    kr = rope(u[:, 1280:1408]).astype(BF16)
    q_scale = (MLA_NOPE_DIM + MLA_ROPE_DIM) ** -0.5 * LOG2_E
    cost = cost_ref[...]
    sint = sint_ref[...]
    half = MLA_ROPE_DIM // 2
    for hd in range(MLA_HEADS):
        qt = _dot_nt(wuqt_ref[hd], cqn)
        qr = qt[MLA_NOPE_DIM:MLA_NOPE_DIM + MLA_ROPE_DIM]
        rot = jnp.concatenate([-qr[half:], qr[:half]], axis=0)
        qt_ref[0, hd, 0:MLA_NOPE_DIM, :] = (qt[:MLA_NOPE_DIM] * q_scale).astype(BF16)
        qt_ref[0, hd, MLA_NOPE_DIM:MLA_NOPE_DIM + MLA_ROPE_DIM, :] = (
            (qr * cost + rot * sint) * q_scale).astype(BF16)
        qt_ref[0, hd, MLA_NOPE_DIM + MLA_ROPE_DIM:, :] = (
            qt[MLA_NOPE_DIM + MLA_ROPE_DIM:]).astype(BF16)
        km_ref[0, hd, :, 0:LANES] = k_nope[:, hd * LANES:(hd + 1) * LANES].astype(BF16)
        km_ref[0, hd, :, LANES:2 * LANES] = kr
        vt_ref[0, hd, 0] = _dot_nt(wuvt_ref[hd], ckvn).astype(BF16)


def _in_proj(x, tables, p, tm):
    cos, sin, cos_t, sin_t = tables
    B, S, D = x.shape
    grid = (B, S // tm)
    full = lambda shape: pl.BlockSpec(shape, lambda b, i: (0,) * len(shape))
    out_shape = (
        jax.ShapeDtypeStruct((B, S, SWA_Q_COLS), BF16),
        jax.ShapeDtypeStruct((B, S, SWA_KV_COLS), BF16),
        jax.ShapeDtypeStruct((B, S, SWA_KV_COLS), BF16),
        jax.ShapeDtypeStruct((B, MLA_HEADS, MLA_QK_PAD, S), BF16),
        jax.ShapeDtypeStruct((B, MLA_HEADS, S, MLA_QK_PAD), BF16),
        jax.ShapeDtypeStruct((B, MLA_HEADS, S // tm, MLA_V_DIM, tm), BF16),
    )
    row = lambda w: pl.BlockSpec((1, tm, w), lambda b, i: (b, i, 0))
    head = lambda w: pl.BlockSpec((1, MLA_HEADS, tm, w), lambda b, i: (b, 0, i, 0))
    return pl.pallas_call(
        _in_proj_kernel,
        grid=grid,
        in_specs=[
            row(D),
            full((1, D)), full((1, D)),
            full((D, IN_COLS_PADDED)),
            pl.BlockSpec((tm, LANES), lambda b, i: (i, 0)),
            pl.BlockSpec((tm, LANES), lambda b, i: (i, 0)),
            pl.BlockSpec((MLA_ROPE_DIM, tm), lambda b, i: (0, i)),
            pl.BlockSpec((MLA_ROPE_DIM, tm), lambda b, i: (0, i)),
            full((1, MLA_Q_RANK)), full((MLA_HEADS, MLA_QK_PAD, MLA_Q_RANK)),
            full((1, MLA_KV_RANK)), full((MLA_KV_RANK, MLA_HEADS * MLA_NOPE_DIM)),
            full((MLA_HEADS, MLA_V_DIM, MLA_KV_RANK)),
        ],
        out_specs=(row(SWA_Q_COLS), row(SWA_KV_COLS), row(SWA_KV_COLS),
                   pl.BlockSpec((1, MLA_HEADS, MLA_QK_PAD, tm), lambda b, i: (b, 0, 0, i)),
                   head(MLA_QK_PAD),
                   pl.BlockSpec((1, MLA_HEADS, 1, MLA_V_DIM, tm), lambda b, i: (b, 0, i, 0, 0))),
        out_shape=out_shape,
        compiler_params=_cparams(("parallel", "parallel")),
        name="in_proj",
    )(x, p["ln_in_g"], p["ln_in_b"], p["w_in"], cos, sin, cos_t, sin_t, p["gq"], p["w_uq_t"],
      p["gkv"], p["w_uk"], p["w_uv_t"])


def _swa_blocks(sink_ref, qa_ref, kc_ref, vc_ref, kprev, vprev, first, o_ref, blocks_per_step):
    row = lax.broadcasted_iota(jnp.int32, (2 * BLOCK, 2 * BLOCK), 0) % BLOCK
    col = lax.broadcasted_iota(jnp.int32, (2 * BLOCK, 2 * BLOCK), 1)
    base_mask = ((col < BLOCK) & (col > row)) | ((col >= BLOCK) & ((col - BLOCK) <= row))
    first_mask = base_mask & (col >= jnp.where(first, N_PAD, 0))
    lane = lax.broadcasted_iota(jnp.int32, (1, LANES), 1)
    lane_kv0 = lane < SWA_HEAD_DIM
    top_rows = lax.broadcasted_iota(jnp.int32, (2 * BLOCK, 1), 0) < BLOCK

    for r in range(blocks_per_step):
        rows = slice(r * BLOCK, (r + 1) * BLOCK)
        if r == 0:
            kp, vp = kprev, vprev
            mask = first_mask
        else:
            prev_rows = slice((r - 1) * BLOCK, r * BLOCK)
            kp, vp = kc_ref[0, prev_rows, :], vc_ref[0, prev_rows, :]
            mask = base_mask
        keys = jnp.concatenate([kp, kc_ref[0, rows, :]], axis=0)
        vals = jnp.concatenate([vp, vc_ref[0, rows, :]], axis=0)
        for c in range(SWA_Q_COLS // LANES):
            qc = qa_ref[0, rows, c * LANES:(c + 1) * LANES]
            zero = jnp.zeros_like(qc)
            q2 = jnp.concatenate([jnp.where(lane_kv0, qc, zero),
                                  jnp.where(lane_kv0, zero, qc)], axis=0)
            s = jnp.where(mask, _dot_nt(q2, keys), NEG)
            sink = jnp.where(top_rows, sink_ref[c], sink_ref[c + 4])
            m = jnp.maximum(jnp.max(s, axis=-1, keepdims=True), sink)
            e = jnp.exp(s - m)
            denom = jnp.sum(e, axis=-1, keepdims=True) + jnp.exp(sink - m)
            pr = (e * (1.0 / denom)).astype(BF16)
            o2 = _dot(pr, vals)
            o = jnp.where(lane_kv0, o2[:BLOCK], o2[BLOCK:])
            o_ref[0, rows, c * LANES:(c + 1) * LANES] = o.astype(BF16)


def _in_proj_swa_kernel(sink_ref, x_ref, g_ref, b_ref, w_ref, cos_ref, sin_ref, cost_ref, sint_ref,
                        gq_ref, wuqt_ref, gkv_ref, wuk_ref, wuvt_ref, kmeta_ref, vmeta_ref,
                        a_ref, qt_ref, km_ref, vt_ref,
                        qa_s, ka_s, va_s, kprev_s, vprev_s):
    _in_proj_kernel(x_ref, g_ref, b_ref, w_ref, cos_ref, sin_ref, cost_ref, sint_ref, gq_ref,
                    wuqt_ref, gkv_ref, wuk_ref, wuvt_ref, qa_s, ka_s, va_s, qt_ref, km_ref, vt_ref)
    first = pl.program_id(1) == 0
    kprev = jnp.where(first, kmeta_ref[0], kprev_s[...])
    vprev = jnp.where(first, vmeta_ref[0], vprev_s[...])
    tm = qa_s.shape[1]
    _swa_blocks(sink_ref, qa_s, ka_s, va_s, kprev, vprev, first, a_ref, tm // BLOCK)
    kprev_s[...] = ka_s[0, tm - BLOCK:, :]
    vprev_s[...] = va_s[0, tm - BLOCK:, :]


def _in_proj_swa(x, tables, p, ka_meta, va_meta, tm):
    cos, sin, cos_t, sin_t = tables
    B, S, D = x.shape
    kvw = SWA_KV_COLS
    full = lambda shape: pl.BlockSpec(shape, lambda b, i: (0,) * len(shape))
    row = lambda w: pl.BlockSpec((1, tm, w), lambda b, i: (b, i, 0))
    out_shape = (
        jax.ShapeDtypeStruct((B, S, SWA_Q_COLS), BF16),
        jax.ShapeDtypeStruct((B, MLA_HEADS, MLA_QK_PAD, S), BF16),
        jax.ShapeDtypeStruct((B, MLA_HEADS, S, MLA_QK_PAD), BF16),
        jax.ShapeDtypeStruct((B, MLA_HEADS, S // tm, MLA_V_DIM, tm), BF16),
    )
    return pl.pallas_call(
        _in_proj_swa_kernel,
        grid=(B, S // tm),
        in_specs=[
            pl.BlockSpec(memory_space=pltpu.SMEM),
            row(D),
            full((1, D)), full((1, D)),
            full((D, IN_COLS_PADDED)),
            pl.BlockSpec((tm, LANES), lambda b, i: (i, 0)),
            pl.BlockSpec((tm, LANES), lambda b, i: (i, 0)),
            pl.BlockSpec((MLA_ROPE_DIM, tm), lambda b, i: (0, i)),
            pl.BlockSpec((MLA_ROPE_DIM, tm), lambda b, i: (0, i)),
            full((1, MLA_Q_RANK)), full((MLA_HEADS, MLA_QK_PAD, MLA_Q_RANK)),
            full((1, MLA_KV_RANK)), full((MLA_KV_RANK, MLA_HEADS * MLA_NOPE_DIM)),
            full((MLA_HEADS, MLA_V_DIM, MLA_KV_RANK)),
            full((1, BLOCK, kvw)), full((1, BLOCK, kvw)),
        ],
        out_specs=(row(SWA_Q_COLS),
                   pl.BlockSpec((1, MLA_HEADS, MLA_QK_PAD, tm), lambda b, i: (b, 0, 0, i)),
                   pl.BlockSpec((1, MLA_HEADS, tm, MLA_QK_PAD), lambda b, i: (b, 0, i, 0)),
                   pl.BlockSpec((1, MLA_HEADS, 1, MLA_V_DIM, tm), lambda b, i: (b, 0, i, 0, 0))),
        out_shape=out_shape,
        scratch_shapes=[pltpu.VMEM((1, tm, SWA_Q_COLS), BF16), pltpu.VMEM((1, tm, kvw), BF16),
                        pltpu.VMEM((1, tm, kvw), BF16), pltpu.VMEM((BLOCK, kvw), BF16),
                        pltpu.VMEM((BLOCK, kvw), BF16)],
        compiler_params=_cparams(("parallel", "arbitrary")),
        name="in_proj_swa",
    )(p["sinks"], x, p["ln_in_g"], p["ln_in_b"], p["w_in"], cos, sin, cos_t, sin_t, p["gq"],
      p["w_uq_t"], p["gkv"], p["w_uk"], p["w_uv_t"], ka_meta, va_meta)


def _mla_kernel(qt_ref, k_ref, vt_ref, kmeta_ref, vtmeta_ref, o_ref,
                s0a_ref, s0b_ref, s1a_ref, s1b_ref, smax_ref, m_ref, l_ref, acc_ref, *, tq):
    u = pl.program_id(2)
    qts = (qt_ref[0, 0, :, :tq], qt_ref[0, 0, :, tq:])
    bufs = ((s0a_ref, s0b_ref), (s1a_ref, s1b_ref))

    for g in range(2):
        s = _dot(kmeta_ref[0, 0], qts[g])
        m = jnp.max(s, axis=0, keepdims=True)
        p = jnp.exp2(s - m)
        m_ref[g] = m
        l_ref[g] = jnp.sum(p, axis=0, keepdims=True)
        acc_ref[g] = _dot(vtmeta_ref[0, 0, 0], p.astype(BF16))

    def scores_into(g, ab, j):
        start = pl.multiple_of(j * tq, tq)
        s = _dot(k_ref[0, 0, pl.ds(start, tq), :], qts[g])
        bufs[g][ab][...] = s
        smax_ref[g, ab] = jnp.max(s, axis=0, keepdims=True)

    def absorb(g, ab, j, causal=False):
        s = bufs[g][ab][...]
        if causal:
            key = lax.broadcasted_iota(jnp.int32, (tq, tq), 0)
            query = lax.broadcasted_iota(jnp.int32, (tq, tq), 1)
            s = jnp.where(key <= query, s, NEG)
            smax = jnp.max(s, axis=0, keepdims=True)
        else:
            smax = smax_ref[g, ab]
        m = m_ref[g]
        m_new = jnp.maximum(m, smax)
        alpha = jnp.exp2(m - m_new)
        p = jnp.exp2(s - m_new)
        m_ref[g] = m_new
        l_ref[g] = alpha * l_ref[g] + jnp.sum(p, axis=0, keepdims=True)
        acc_ref[g] = alpha * acc_ref[g] + _dot(vt_ref[0, 0, j], p.astype(BF16))

    buf_a, buf_b = 0, 1
    for g in range(2):
        scores_into(g, buf_a, 0)

    def pair(t, carry):
        for g in range(2):
            scores_into(g, buf_b, 2 * t + 1)
            absorb(g, buf_a, 2 * t)
        for g in range(2):
            scores_into(g, buf_a, 2 * t + 2)
            absorb(g, buf_b, 2 * t + 1)
        return carry

    lax.fori_loop(0, u, pair, 0)

    scores_into(1, buf_b, 2 * u + 1)
    absorb(1, buf_a, 2 * u)
    absorb(0, buf_a, 2 * u, causal=True)
    absorb(1, buf_b, 2 * u + 1, causal=True)

    for g in range(2):
        o_ref[0, g * tq:(g + 1) * tq, :] = (acc_ref[g] * (1.0 / l_ref[g])).T.astype(BF16)


def _mla(qt, km, vt, km_meta, vt_meta, tq):
    B, H, _, S = qt.shape
    assert S % (2 * tq) == 0, "a step handles an (even, odd) pair of query tiles"
    return pl.pallas_call(
        functools.partial(_mla_kernel, tq=tq),
        grid=(B, H, S // (2 * tq)),
        in_specs=[
            pl.BlockSpec((1, 1, MLA_QK_PAD, 2 * tq), lambda b, h, u: (b, h, 0, u)),
            pl.BlockSpec((1, 1, S, MLA_QK_PAD), lambda b, h, u: (b, h, 0, 0)),
            pl.BlockSpec((1, 1, S // tq, MLA_V_DIM, tq), lambda b, h, u: (b, h, 0, 0, 0)),
            pl.BlockSpec((1, 1, N_META, MLA_QK_PAD), lambda b, h, u: (0, h, 0, 0)),
            pl.BlockSpec((1, 1, 1, MLA_V_DIM, N_META), lambda b, h, u: (0, h, 0, 0, 0)),
        ],
        out_specs=pl.BlockSpec((1, 2 * tq, MLA_V_DIM), lambda b, h, u: (b, u, h)),
        out_shape=jax.ShapeDtypeStruct((B, S, MLA_OUT_COLS), BF16),
        scratch_shapes=[pltpu.VMEM((tq, tq), F32), pltpu.VMEM((tq, tq), F32),
                        pltpu.VMEM((tq, tq), F32), pltpu.VMEM((tq, tq), F32),
                        pltpu.VMEM((2, 2, 1, tq), F32),
                        pltpu.VMEM((2, 1, tq), F32), pltpu.VMEM((2, 1, tq), F32),
                        pltpu.VMEM((2, MLA_V_DIM, tq), F32)],
        compiler_params=_cparams(("parallel", "parallel", "arbitrary")),
        name="mla_attn",
    )(qt, km, vt, km_meta, vt_meta)


def _out_proj_kernel(x_ref, a_ref, b_ref, gin_ref, bin_ref, ga_ref, gb_ref, wo_ref, g1_ref, b1_ref,
                     wrt_ref, brt_ref, h1_ref, route_ref, counts_out_ref, count_ref):
    h = _layer_norm(x_ref[0], gin_ref[...], bin_ref[...])
    a = a_ref[0].astype(F32)
    b = b_ref[0].astype(F32)
    an = (a * _rms_scale(a) * ga_ref[...]).astype(BF16)
    bn = (b * _rms_scale(b) * gb_ref[...]).astype(BF16)
    mix = _dot(jnp.concatenate([an, bn], axis=-1), wo_ref[...])
    h1 = _layer_norm(ALPHA * h + mix, g1_ref[...], b1_ref[...])
    h1_ref[0] = h1

    x_hi = h1.astype(BF16)
    x_lo = (h1 - x_hi.astype(F32)).astype(BF16)
    hi_both = _dot(x_hi, wrt_ref[...])
    logits = (hi_both[:, :ROUTE_LANES] + hi_both[:, ROUTE_LANES:]
              + _dot(x_lo, wrt_ref[:, :ROUTE_LANES]) + brt_ref[...])
    lane = lax.broadcasted_iota(jnp.int32, logits.shape, 1).astype(F32)
    ninf = -jnp.inf
    no_lane = float(ROUTE_LANES)
    gl = jnp.where(lane < N_GROUPS, logits, ninf)
    gmax = jnp.max(gl, axis=-1, keepdims=True)
    g_idx = jnp.min(jnp.where(gl == gmax, lane, no_lane), axis=-1, keepdims=True)
    lo = EXPERT_LANE0 + g_idx * EXPERTS_PER_GROUP
    el = jnp.where((lane >= lo) & (lane < lo + EXPERTS_PER_GROUP), logits, ninf)
    m1 = jnp.max(el, axis=-1, keepdims=True)
    i1 = jnp.min(jnp.where(el == m1, lane, no_lane), axis=-1, keepdims=True)
    el2 = jnp.where(lane == i1, ninf, el)
    m2 = jnp.max(el2, axis=-1, keepdims=True)
    i2 = jnp.min(jnp.where(el2 == m2, lane, no_lane), axis=-1, keepdims=True)

    a = jnp.minimum(i1, i2) - lo
    b = jnp.maximum(i1, i2) - lo
    cls = (g_idx * PAIRS_PER_GROUP + a * (2.0 * EXPERTS_PER_GROUP - 1.0 - a) * 0.5
           + (b - a - 1.0))

    @pl.when((pl.program_id(0) == 0) & (pl.program_id(1) == 0))
    def _():
        count_ref[...] = jnp.zeros_like(count_ref)

    tm = logits.shape[0]
    onehot = jnp.where(lane == cls, 1.0, 0.0)
    r_i = lax.broadcasted_iota(jnp.int32, (BLOCK, BLOCK), 0)
    c_i = lax.broadcasted_iota(jnp.int32, (BLOCK, BLOCK), 1)
    earlier = jnp.where(r_i > c_i, 1.0, 0.0).astype(BF16)
    running = count_ref[...]
    before = []
    for blk in range(tm // BLOCK):
        oh = onehot[blk * BLOCK:(blk + 1) * BLOCK]
        before.append(_dot(earlier, oh.astype(BF16)) + running)
        running = running + jnp.sum(oh, axis=0, keepdims=True)
    before = jnp.concatenate(before, axis=0)
    rank = jnp.sum(jnp.where(lane == cls, before, 0.0), axis=-1, keepdims=True)
    count_ref[...] = running
    counts_out_ref[...] = running

    route = jnp.where(lane == ROUTE_CLS, cls, 0.0)
    route = jnp.where(lane == ROUTE_RANK, rank, route)
    route_ref[0] = route


def _out_proj(x, a_out, b_out, p, tm):
    B, S, D = x.shape
    full = lambda shape: pl.BlockSpec(shape, lambda b, i: (0,) * len(shape))
    row = lambda w: pl.BlockSpec((1, tm, w), lambda b, i: (b, i, 0))
    return pl.pallas_call(
        _out_proj_kernel,
        grid=(B, S // tm),
        in_specs=[row(D), row(SWA_Q_COLS), row(MLA_OUT_COLS),
                  full((1, D)), full((1, D)), full((1, SWA_Q_COLS)), full((1, MLA_OUT_COLS)),
                  full((D, D)), full((1, D)), full((1, D)),
                  full((D, 2 * ROUTE_LANES)), full((1, ROUTE_LANES))],
        out_specs=(row(D), row(ROUTE_LANES), full((1, ROUTE_LANES))),
        out_shape=(jax.ShapeDtypeStruct((B, S, D), F32),
                   jax.ShapeDtypeStruct((B, S, ROUTE_LANES), F32),
                   jax.ShapeDtypeStruct((1, ROUTE_LANES), F32)),
        scratch_shapes=[pltpu.VMEM((1, ROUTE_LANES), F32)],
        compiler_params=_cparams(("arbitrary", "arbitrary")),
        name="out_proj_route",
    )(x, a_out, b_out, p["ln_in_g"], p["ln_in_b"], p["ga"], p["gb"], p["w_o"], p["ln1_g"],
      p["ln1_b"], p["w_route"], p["b_route"])


SLAB = D_MODEL // LANES


def _slab_copy(src_ref, src_tok8, dst_ref, dst_tok8, sem):
    return pltpu.make_async_copy(src_ref.at[pl.ds(pl.multiple_of(src_tok8, SLAB), SLAB)],
                                 dst_ref.at[pl.ds(pl.multiple_of(dst_tok8, SLAB), SLAB)], sem)


def _to_slabs(x, slab_ref, n):
    for s in range(SLAB):
        slab_ref[pl.ds(s, n, stride=SLAB), :] = x[:, s * LANES:(s + 1) * LANES]


def _from_slabs(slab_ref, n):
    return jnp.concatenate([slab_ref[pl.ds(s, n, stride=SLAB), :] for s in range(SLAB)], axis=1)


def _dispatch_kernel(padstart_ref, nu_ref, pos_ref, h1_ref, xs_ref, rows_ref, zero_ref, sem, *,
                     tm, n_tiles):
    tail = zero_ref.shape[0]

    def zero_fill(start):
        return pltpu.make_async_copy(
            zero_ref, xs_ref.at[pl.ds(pl.multiple_of(start, SLAB), tail)], sem)

    @pl.when(pl.program_id(0) == 0)
    def _():
        zero_ref[...] = jnp.zeros_like(zero_ref)
        for parity in range(2):
            tails = [zero_fill(padstart_ref[c]) for c in range(parity, N_CLASSES, 2)]
            for t in tails:
                t.start()
            for t in tails:
                t.wait()

        def fill_unused(j, carry):
            fill = zero_fill(j * tail)
            fill.start()
            fill.wait()
            return carry

        lax.fori_loop(nu_ref[0], n_tiles + 1, fill_unused, 0)

    _to_slabs(h1_ref[...], rows_ref, tm)

    def issue(r0, carry):
        for k in range(DMA_BURST):
            r = r0 * DMA_BURST + k
            _slab_copy(rows_ref, r * SLAB, xs_ref, pos_ref[0, 0, r], sem).start(priority=k % 2)
        return carry

    lax.fori_loop(0, tm // DMA_BURST, issue, 0)
    pltpu.make_async_copy(rows_ref, xs_ref.at[pl.ds(0, tm * SLAB)], sem).wait()


def _dispatch(h1, pos8, pad_start8, n_used, n_tiles, tm):
    T, D = h1.shape
    pos8 = pos8.reshape(T // tm, 1, tm)
    n_rows = (n_tiles + 1) * MOE_TILE_ROWS
    grid_spec = pltpu.PrefetchScalarGridSpec(
        num_scalar_prefetch=2,
        grid=(T // tm,),
        in_specs=[pl.BlockSpec((1, 1, tm), lambda t, ps, nu: (t, 0, 0),
                               memory_space=pltpu.SMEM),
                  pl.BlockSpec((tm, D), lambda t, ps, nu: (t, 0))],
        out_specs=pl.BlockSpec(memory_space=pl.ANY),
        scratch_shapes=[pltpu.VMEM((tm * SLAB, LANES), F32),
                        pltpu.VMEM((MOE_TILE_ROWS * SLAB, LANES), F32),
                        pltpu.SemaphoreType.DMA],
    )
    return pl.pallas_call(
        functools.partial(_dispatch_kernel, tm=tm, n_tiles=n_tiles),
        grid_spec=grid_spec,
        out_shape=jax.ShapeDtypeStruct((n_rows * SLAB, LANES), F32),
        compiler_params=_cparams(("arbitrary",)),
        name="moe_dispatch",
    )(pad_start8, n_used, pos8, h1)


def _experts_kernel(tlo_ref, thi_ref, nu_ref, x_ref, wgu_lo_ref, wgu_hi_ref, wd_lo_ref, wd_hi_ref,
                    wrt_ref, brt_ref, y_ref):
    j = pl.program_id(0)
    tr = MOE_TILE_ROWS

    @pl.when(j < nu_ref[0])
    def _():
        x = _from_slabs(x_ref, tr).astype(BF16)
        lo_e = tlo_ref[j]
        hi_e = thi_ref[j]
        logits = _dot(x, wrt_ref[...]) + brt_ref[...]
        lane = lax.broadcasted_iota(jnp.int32, logits.shape, 1)
        gl = jnp.where(lane < N_GROUPS, logits, -jnp.inf)
        ge = jnp.exp(gl - jnp.max(gl, axis=-1, keepdims=True))
        pick = lambda v, at: jnp.sum(jnp.where(lane == at, v, 0.0), axis=-1, keepdims=True)
        g_prob = pick(ge, lo_e // EXPERTS_PER_GROUP) / jnp.sum(ge, axis=-1, keepdims=True)
        e_lo = pick(logits, EXPERT_LANE0 + lo_e)
        e_hi = pick(logits, EXPERT_LANE0 + hi_e)
        gates = (g_prob / (1.0 + jnp.exp(e_hi - e_lo)), g_prob / (1.0 + jnp.exp(e_lo - e_hi)))

        y = None
        for gate, wgu_ref, wd_ref in zip(gates, (wgu_lo_ref, wgu_hi_ref), (wd_lo_ref, wd_hi_ref)):
            gu = _dot(x, wgu_ref[0])
            g = gu[:, :D_EXPERT]
            u = gu[:, D_EXPERT:]
            hid = g * (1.0 / (1.0 + jnp.exp(-g))) * u * gate
            y_k = _dot(hid.astype(BF16), wd_ref[0])
            y = y_k if y is None else y + y_k
        _to_slabs(y, y_ref, tr)

    @pl.when(j >= nu_ref[0])
    def _():
        y_ref[...] = jnp.zeros_like(y_ref)


def _experts(xs, tile_lo, tile_hi, n_used, p, n_tiles):
    D = D_MODEL
    tr = MOE_TILE_ROWS
    last_used = lambda j, nu: jnp.minimum(j, nu[0] - 1)
    wgu = lambda pick: pl.BlockSpec((1, D, 2 * D_EXPERT),
                                    lambda j, tlo, thi, nu: (pick(tlo, thi)[j], 0, 0))
    wd = lambda pick: pl.BlockSpec((1, D_EXPERT, D),
                                   lambda j, tlo, thi, nu: (pick(tlo, thi)[j], 0, 0))
    lo = lambda tlo, thi: tlo
    hi = lambda tlo, thi: thi
    grid_spec = pltpu.PrefetchScalarGridSpec(
        num_scalar_prefetch=3,
        grid=(n_tiles,),
        in_specs=[pl.BlockSpec((tr * SLAB, LANES), lambda j, tlo, thi, nu: (last_used(j, nu), 0)),
                  wgu(lo), wgu(hi), wd(lo), wd(hi),
                  pl.BlockSpec((D, ROUTE_LANES), lambda j, tlo, thi, nu: (0, 0)),
                  pl.BlockSpec((1, ROUTE_LANES), lambda j, tlo, thi, nu: (0, 0))],
        out_specs=pl.BlockSpec((tr * SLAB, LANES), lambda j, tlo, thi, nu: (j, 0)),
    )
    return pl.pallas_call(
        _experts_kernel,
        grid_spec=grid_spec,
        out_shape=jax.ShapeDtypeStruct((n_tiles * tr * SLAB, LANES), F32),
        compiler_params=_cparams(("arbitrary",)),
        name="moe_experts",
    )(tile_lo, tile_hi, n_used, xs, p["w_gu"], p["w_gu"], p["w_down"], p["w_down"],
      p["w_route"], p["b_route"])


def _combine_kernel(pos_ref, pos_next_ref, h1_ref, ys_ref, g2_ref, b2_ref, o_ref,
                    ybuf_ref, sem, *, tm):
    t = pl.program_id(0)
    slot = t % 2

    def gather(idx_ref, dst_slot):
        def issue(r0, carry):
            for k in range(DMA_BURST):
                r = r0 * DMA_BURST + k
                _slab_copy(ys_ref, idx_ref[0, 0, r], ybuf_ref.at[dst_slot], r * SLAB,
                           sem.at[dst_slot]).start(priority=k % 2)
            return carry

        lax.fori_loop(0, tm // DMA_BURST, issue, 0)

    @pl.when(t == 0)
    def _():
        gather(pos_ref, 0)

    @pl.when(t + 1 < pl.num_programs(0))
    def _():
        gather(pos_next_ref, 1 - slot)

    pltpu.make_async_copy(ys_ref.at[pl.ds(0, tm * SLAB)], ybuf_ref.at[slot], sem.at[slot]).wait()
    ffn = _from_slabs(ybuf_ref.at[slot], tm)
    o_ref[...] = _layer_norm(ALPHA * h1_ref[...] + ffn, g2_ref[...], b2_ref[...])


def _combine(h1, pos8, ys, p, tm):
    T, D = h1.shape
    n_steps = T // tm
    pos8 = pos8.reshape(n_steps, 1, tm)
    full = lambda shape: pl.BlockSpec(shape, lambda t: (0,) * len(shape))
    return pl.pallas_call(
        functools.partial(_combine_kernel, tm=tm),
        grid=(n_steps,),
        in_specs=[pl.BlockSpec((1, 1, tm), lambda t: (t, 0, 0), memory_space=pltpu.SMEM),
                  pl.BlockSpec((1, 1, tm), lambda t: (jnp.minimum(t + 1, n_steps - 1), 0, 0),
                               memory_space=pltpu.SMEM),
                  pl.BlockSpec((tm, D), lambda t: (t, 0)),
                  pl.BlockSpec(memory_space=pl.ANY),
                  full((1, D)), full((1, D))],
        out_specs=pl.BlockSpec((tm, D), lambda t: (t, 0)),
        out_shape=jax.ShapeDtypeStruct((T, D), F32),
        scratch_shapes=[pltpu.VMEM((2, tm * SLAB, LANES), F32),
                        pltpu.SemaphoreType.DMA((2,))],
        compiler_params=_cparams(("arbitrary",)),
        name="moe_combine_ln2",
    )(pos8, pos8, h1, ys, p["ln2_g"], p["ln2_b"])


def _dispatch_plan(route, counts):
    T = route.shape[0]
    tr = MOE_TILE_ROWS
    n_tiles = T // tr + N_CLASSES
    cnt = counts[0, :N_CLASSES].astype(jnp.int32)
    padded = jnp.maximum((cnt + tr - 1) // tr, 1) * tr
    seg_end = jnp.cumsum(padded)
    seg_start = seg_end - padded
    cls = route[:, ROUTE_CLS].astype(jnp.int32)
    rank = route[:, ROUTE_RANK].astype(jnp.int32)
    classes = jnp.arange(N_CLASSES, dtype=jnp.int32)
    start_of = jnp.sum(jnp.where(cls[:, None] == classes, seg_start, 0), axis=-1)
    pos8 = (start_of + rank) * SLAB
    tile_row0 = jnp.arange(n_tiles, dtype=jnp.int32) * tr
    tile_cls = jnp.sum((seg_end[None, :] <= tile_row0[:, None]).astype(jnp.int32), axis=-1)
    tile_cls = jnp.minimum(tile_cls, N_CLASSES - 1)
    pairs = [(a, b) for a in range(EXPERTS_PER_GROUP) for b in range(a + 1, EXPERTS_PER_GROUP)]
    lo_of = jnp.asarray([g * EXPERTS_PER_GROUP + a for g in range(N_GROUPS) for a, _ in pairs],
                        jnp.int32)
    hi_of = jnp.asarray([g * EXPERTS_PER_GROUP + b for g in range(N_GROUPS) for _, b in pairs],
                        jnp.int32)
    is_cls = tile_cls[:, None] == classes
    tile_lo = jnp.sum(jnp.where(is_cls, lo_of, 0), axis=-1)
    tile_hi = jnp.sum(jnp.where(is_cls, hi_of, 0), axis=-1)
    n_used = (seg_end[-1:] // tr).astype(jnp.int32)
    return pos8, (seg_start + cnt) * SLAB, tile_lo, tile_hi, n_used, n_tiles


def _rope_tables(pos):
    d = SWA_HEAD_DIM
    inv_freq = ROPE_THETA ** (-jnp.arange(0, d, 2, dtype=F32) / d)
    ang = pos[:, None] * inv_freq[None, :]
    c, s = jnp.cos(ang), jnp.sin(ang)
    return (jnp.concatenate([c, c, c, c], axis=-1), jnp.concatenate([-s, s, -s, s], axis=-1),
            jnp.concatenate([c, c], axis=-1).T, jnp.concatenate([s, s], axis=-1).T)


def _prep_params(ln_in_g, ln_in_b, w_in, swa_sinks, mla_q_norm_g, mla_w_uq, mla_kv_norm_g, mla_w_ukv,
                 swa_out_norm_g, mla_out_norm_g, w_o, ln1_g, ln1_b, moe_w_group, moe_b_group,
                 moe_w_router, moe_b_router, moe_w_gate, moe_w_up, moe_w_down, ln2_g, ln2_b):
    order = jnp.asarray(SWA_HEAD_ORDER)
    perm = (order[:, None] * SWA_HEAD_DIM + jnp.arange(SWA_HEAD_DIM)[None, :]).reshape(-1)
    w = w_in[0]
    w_q = w[:, :SWA_Q_COLS][:, perm]
    w_pad = jnp.zeros((D_MODEL, IN_COLS_PADDED - w.shape[1]), w.dtype)
    w_in_p = jnp.concatenate([w_q, w[:, SWA_Q_COLS:], w_pad], axis=1).astype(BF16)

    dqk = MLA_NOPE_DIM + MLA_ROPE_DIM
    wuq = mla_w_uq[0].reshape(MLA_Q_RANK, MLA_HEADS, dqk)
    wuq = jnp.pad(wuq, ((0, 0), (0, 0), (0, MLA_QK_PAD - dqk)))
    wuq_t = jnp.transpose(wuq, (1, 2, 0)).astype(BF16)
    wukv = mla_w_ukv[0].reshape(MLA_KV_RANK, MLA_HEADS, MLA_NOPE_DIM + MLA_V_DIM)
    wuk = wukv[:, :, :MLA_NOPE_DIM].reshape(MLA_KV_RANK, MLA_HEADS * MLA_NOPE_DIM).astype(BF16)
    wuv_t = jnp.transpose(wukv[:, :, MLA_NOPE_DIM:], (1, 2, 0)).astype(BF16)

    w_o0 = w_o[0]
    w_o_p = jnp.concatenate([w_o0[:SWA_Q_COLS][perm], w_o0[SWA_Q_COLS:]], axis=0).astype(BF16)

    w_route = jnp.zeros((D_MODEL, ROUTE_LANES), F32)
    w_route = w_route.at[:, :N_GROUPS].set(moe_w_group[0])
    w_route = w_route.at[:, EXPERT_LANE0:EXPERT_LANE0 + N_EXPERTS].set(moe_w_router[0])
    w_route_hi = w_route.astype(BF16)
    w_route_lo = (w_route - w_route_hi.astype(F32)).astype(BF16)
    w_route = jnp.concatenate([w_route_hi, w_route_lo], axis=1)
    b_route = jnp.zeros((1, ROUTE_LANES), F32)
    b_route = b_route.at[0, :N_GROUPS].set(moe_b_group[0])
    b_route = b_route.at[0, EXPERT_LANE0:EXPERT_LANE0 + N_EXPERTS].set(moe_b_router[0])

    return {
        "ln_in_g": ln_in_g[None, :], "ln_in_b": ln_in_b[None, :],
        "w_in": w_in_p,
        "sinks": swa_sinks[0],
        "gq": mla_q_norm_g, "w_uq_t": wuq_t,
        "gkv": mla_kv_norm_g, "w_uk": wuk, "w_uv_t": wuv_t,
        "ga": swa_out_norm_g[0][perm][None, :], "gb": mla_out_norm_g,
        "w_o": w_o_p, "ln1_g": ln1_g, "ln1_b": ln1_b,
        "w_route": w_route, "b_route": b_route,
        "w_gu": jnp.concatenate([moe_w_gate[0], moe_w_up[0]], axis=-1).astype(BF16),
        "w_down": moe_w_down[0].astype(BF16),
        "ln2_g": ln2_g, "ln2_b": ln2_b,
    }


def kernel(x, meta_tokens, ln_in_g, ln_in_b, w_in, swa_sinks, mla_q_norm_g, mla_w_uq, mla_kv_norm_g, mla_w_ukv, swa_out_norm_g, mla_out_norm_g, w_o, ln1_g, ln1_b, moe_w_group, moe_b_group, moe_w_router, moe_b_router, moe_w_gate, moe_w_up, moe_w_down, ln2_g, ln2_b):
    B, S, D = x.shape
    p = _prep_params(ln_in_g, ln_in_b, w_in, swa_sinks, mla_q_norm_g, mla_w_uq, mla_kv_norm_g,
                     mla_w_ukv, swa_out_norm_g, mla_out_norm_g, w_o, ln1_g, ln1_b, moe_w_group,
                     moe_b_group, moe_w_router, moe_b_router, moe_w_gate, moe_w_up, moe_w_down,
                     ln2_g, ln2_b)
    T = B * S
    tm = min(PROJ_TILE, S)

    x_meta = jnp.concatenate([jnp.zeros((N_PAD, D), x.dtype), meta_tokens.astype(x.dtype)])[None]
    pos_meta = jnp.maximum(jnp.arange(BLOCK) - N_PAD, 0).astype(F32)
    pos_tok = (jnp.arange(S) + N_META).astype(F32)
    _, ka_m, va_m, _, km_m, vt_m = _in_proj(x_meta, _rope_tables(pos_meta), p, BLOCK)
    a_out, qt, km, vt = _in_proj_swa(x, _rope_tables(pos_tok), p, ka_m, va_m, tm)
    b_out = _mla(qt, km, vt, km_m[:, :, N_PAD:, :], vt_m[..., N_PAD:], tm)
    h1, route, counts = _out_proj(x, a_out, b_out, p, min(WIDE_TILE, S))
    h1 = h1.reshape(B * S, D)
    route = route.reshape(B * S, ROUTE_LANES)

    pos8, pad_start8, tile_lo, tile_hi, n_used, n_tiles = _dispatch_plan(route, counts)
    xs = _dispatch(h1, pos8, pad_start8, n_used, n_tiles, min(DISPATCH_TILE, T))
    ys = _experts(xs, tile_lo, tile_hi, n_used, p, n_tiles)
    out = _combine(h1, pos8, ys, p, min(COMBINE_TILE, T))
    return out.reshape(B, S, D)
```

```python
import functools

import jax
import jax.numpy as jnp
from jax import lax
from jax.experimental import pallas as pl
from jax.experimental.pallas import tpu as pltpu

D_MODEL = 1024
N_META = 16
BLOCK = 128
N_PAD = BLOCK - N_META
ROPE_THETA = 10000.0
SWA_HEADS = 8
SWA_KV_HEADS = 2
SWA_HEAD_DIM = 64
MLA_HEADS = 4
MLA_Q_RANK = 256
MLA_KV_RANK = 256
MLA_NOPE_DIM = 128
MLA_ROPE_DIM = 64
MLA_V_DIM = 128
SWA_Q_COLS = SWA_HEADS * SWA_HEAD_DIM
SWA_KV_COLS = SWA_KV_HEADS * SWA_HEAD_DIM
MLA_OUT_COLS = MLA_HEADS * MLA_V_DIM
N_GROUPS = 4
EXPERTS_PER_GROUP = 8
N_EXPERTS = N_GROUPS * EXPERTS_PER_GROUP
D_EXPERT = 256
LN_EPS = 1e-5
RMS_EPS = 1e-6
DEPTH = 1
ALPHA = (2.0 * DEPTH) ** 0.25
NEG = -1e30
LOG2_E = 1.4426950408889634

LANES = 128
IN_COLS_PADDED = 1408
MLA_QK_PAD = 256
ROUTE_LANES = 128
EXPERT_LANE0 = 32
ROUTE_CLS, ROUTE_RANK = range(2)
PAIRS_PER_GROUP = EXPERTS_PER_GROUP * (EXPERTS_PER_GROUP - 1) // 2
N_CLASSES = N_GROUPS * PAIRS_PER_GROUP
MOE_TILE_ROWS = 256
PROJ_TILE = 1024
WIDE_TILE = 1024
DISPATCH_TILE = 2048
COMBINE_TILE = 1024
DMA_BURST = 8
VMEM_LIMIT = 56 * 1024 * 1024

SWA_HEAD_ORDER = (0, 4, 1, 5, 2, 6, 3, 7)

BF16 = jnp.bfloat16
F32 = jnp.float32


def _cparams(sem):
    return pltpu.CompilerParams(dimension_semantics=sem, vmem_limit_bytes=VMEM_LIMIT)


def _layer_norm(x, g, b):
    mu = jnp.mean(x, axis=-1, keepdims=True)
    xc = x - mu
    var = jnp.mean(xc * xc, axis=-1, keepdims=True)
    return xc * lax.rsqrt(var + LN_EPS) * g + b


def _rms_scale(x):
    return lax.rsqrt(jnp.mean(x * x, axis=-1, keepdims=True) + RMS_EPS)


def _rope128(x, cos, sin_signed, lane_lo):
    up = pltpu.roll(x, LANES - 32, 1)
    dn = pltpu.roll(x, 32, 1)
    return x * cos + jnp.where(lane_lo, up, dn) * sin_signed


def _dot(a, b):
    return jnp.dot(a, b, preferred_element_type=F32)


def _dot_nt(a, b):
    return lax.dot_general(a, b, (((1,), (1,)), ((), ())), preferred_element_type=F32)


def _in_proj_kernel(x_ref, g_ref, b_ref, w_ref, cos_ref, sin_ref, cost_ref, sint_ref, gq_ref,
                    wuqt_ref, gkv_ref, wuk_ref, wuvt_ref,
                    qa_ref, ka_ref, va_ref, qt_ref, km_ref, vt_ref):
    h = _layer_norm(x_ref[0], g_ref[...], b_ref[...])
    u = _dot(h.astype(BF16), w_ref[...])
    cos = cos_ref[...]
    sin = sin_ref[...]
    lane = lax.broadcasted_iota(jnp.int32, (1, LANES), 1)
    lane_lo = (lane % 64) < 32
    rope = functools.partial(_rope128, cos=cos, sin_signed=sin, lane_lo=lane_lo)

    swa_scale = SWA_HEAD_DIM ** -0.5
    for c in range(SWA_Q_COLS // LANES):
        qc = rope(u[:, c * LANES:(c + 1) * LANES]) * swa_scale
        qa_ref[0, :, c * LANES:(c + 1) * LANES] = qc.astype(BF16)
    ka_ref[0] = rope(u[:, 512:640]).astype(BF16)
    va_ref[0] = u[:, 640:768].astype(BF16)

    cq = u[:, 768:1024]
    cqn = (cq * _rms_scale(cq) * gq_ref[...]).astype(BF16)
    ckv = u[:, 1024:1280]
    ckvn = (ckv * _rms_scale(ckv) * gkv_ref[...]).astype(BF16)
    k_nope = _dot(ckvn, wuk_ref[...])
    kr = rope(u[:, 1280:1408]).astype(BF16)
    q_scale = (MLA_NOPE_DIM + MLA_ROPE_DIM) ** -0.5 * LOG2_E
    cost = cost_ref[...]
    sint = sint_ref[...]
    half = MLA_ROPE_DIM // 2
    for hd in range(MLA_HEADS):
        qt = _dot_nt(wuqt_ref[hd], cqn)
        qr = qt[MLA_NOPE_DIM:MLA_NOPE_DIM + MLA_ROPE_DIM]
        rot = jnp.concatenate([-qr[half:], qr[:half]], axis=0)
        qt_ref[0, hd, 0:MLA_NOPE_DIM, :] = (qt[:MLA_NOPE_DIM] * q_scale).astype(BF16)
        qt_ref[0, hd, MLA_NOPE_DIM:MLA_NOPE_DIM + MLA_ROPE_DIM, :] = (
            (qr * cost + rot * sint) * q_scale).astype(BF16)
        qt_ref[0, hd, MLA_NOPE_DIM + MLA_ROPE_DIM:, :] = (
            qt[MLA_NOPE_DIM + MLA_ROPE_DIM:]).astype(BF16)
        km_ref[0, hd, :, 0:LANES] = k_nope[:, hd * LANES:(hd + 1) * LANES].astype(BF16)
        km_ref[0, hd, :, LANES:2 * LANES] = kr
        vt_ref[0, hd, 0] = _dot_nt(wuvt_ref[hd], ckvn).astype(BF16)


def _in_proj(x, tables, p, tm):
    cos, sin, cos_t, sin_t = tables
    B, S, D = x.shape
    grid = (B, S // tm)
    full = lambda shape: pl.BlockSpec(shape, lambda b, i: (0,) * len(shape))
    out_shape = (
        jax.ShapeDtypeStruct((B, S, SWA_Q_COLS), BF16),
        jax.ShapeDtypeStruct((B, S, SWA_KV_COLS), BF16),
        jax.ShapeDtypeStruct((B, S, SWA_KV_COLS), BF16),
        jax.ShapeDtypeStruct((B, MLA_HEADS, MLA_QK_PAD, S), BF16),
        jax.ShapeDtypeStruct((B, MLA_HEADS, S, MLA_QK_PAD), BF16),
        jax.ShapeDtypeStruct((B, MLA_HEADS, S // tm, MLA_V_DIM, tm), BF16),
    )
    row = lambda w: pl.BlockSpec((1, tm, w), lambda b, i: (b, i, 0))
    head = lambda w: pl.BlockSpec((1, MLA_HEADS, tm, w), lambda b, i: (b, 0, i, 0))
    return pl.pallas_call(
        _in_proj_kernel,
        grid=grid,
        in_specs=[
            row(D),
            full((1, D)), full((1, D)),
            full((D, IN_COLS_PADDED)),
            pl.BlockSpec((tm, LANES), lambda b, i: (i, 0)),
            pl.BlockSpec((tm, LANES), lambda b, i: (i, 0)),
            pl.BlockSpec((MLA_ROPE_DIM, tm), lambda b, i: (0, i)),
            pl.BlockSpec((MLA_ROPE_DIM, tm), lambda b, i: (0, i)),
            full((1, MLA_Q_RANK)), full((MLA_HEADS, MLA_QK_PAD, MLA_Q_RANK)),
            full((1, MLA_KV_RANK)), full((MLA_KV_RANK, MLA_HEADS * MLA_NOPE_DIM)),
            full((MLA_HEADS, MLA_V_DIM, MLA_KV_RANK)),
        ],
        out_specs=(row(SWA_Q_COLS), row(SWA_KV_COLS), row(SWA_KV_COLS),
                   pl.BlockSpec((1, MLA_HEADS, MLA_QK_PAD, tm), lambda b, i: (b, 0, 0, i)),
                   head(MLA_QK_PAD),
                   pl.BlockSpec((1, MLA_HEADS, 1, MLA_V_DIM, tm), lambda b, i: (b, 0, i, 0, 0))),
        out_shape=out_shape,
        compiler_params=_cparams(("parallel", "parallel")),
        name="in_proj",
    )(x, p["ln_in_g"], p["ln_in_b"], p["w_in"], cos, sin, cos_t, sin_t, p["gq"], p["w_uq_t"],
      p["gkv"], p["w_uk"], p["w_uv_t"])


def _swa_blocks(sink_ref, qa_ref, kc_ref, vc_ref, kprev, vprev, first, o_ref, blocks_per_step):
    row = lax.broadcasted_iota(jnp.int32, (2 * BLOCK, BLOCK), 0) % BLOCK
    col = lax.broadcasted_iota(jnp.int32, (2 * BLOCK, BLOCK), 1)
    own = col <= row
    meta_ok = col >= jnp.where(first, N_PAD, 0)
    lane = lax.broadcasted_iota(jnp.int32, (1, LANES), 1)
    lane_kv0 = lane < SWA_HEAD_DIM
    top_rows = lax.broadcasted_iota(jnp.int32, (2 * BLOCK, 1), 0) < BLOCK

    for r in range(blocks_per_step):
        rows = slice(r * BLOCK, (r + 1) * BLOCK)
        if r == 0:
            kp, vp = kprev, vprev
        else:
            prev_rows = slice((r - 1) * BLOCK, r * BLOCK)
            kp, vp = kc_ref[0, prev_rows, :], vc_ref[0, prev_rows, :]
        kc, vc = kc_ref[0, rows, :], vc_ref[0, rows, :]
        for c in range(SWA_Q_COLS // LANES):
            qc = qa_ref[0, rows, c * LANES:(c + 1) * LANES]
            zero = jnp.zeros_like(qc)
            q2 = jnp.concatenate([jnp.where(lane_kv0, qc, zero),
                                  jnp.where(lane_kv0, zero, qc)], axis=0)
            s_prev = _dot_nt(q2, kp)
            if r == 0:
                s_prev = jnp.where(meta_ok, s_prev, NEG)
            s = jnp.where(own, _dot_nt(q2, kc), s_prev)
            sink = jnp.where(top_rows, sink_ref[c], sink_ref[c + 4])
            m = jnp.maximum(jnp.max(s, axis=-1, keepdims=True), sink)
            e = jnp.exp(s - m)
            denom = jnp.sum(e, axis=-1, keepdims=True) + jnp.exp(sink - m)
            pr = e * (1.0 / denom)
            o2 = (_dot(jnp.where(own, pr, 0.0).astype(BF16), vc)
                  + _dot(jnp.where(own, 0.0, pr).astype(BF16), vp))
            o = jnp.where(lane_kv0, o2[:BLOCK], o2[BLOCK:])
            o_ref[0, rows, c * LANES:(c + 1) * LANES] = o.astype(BF16)


def _in_proj_swa_kernel(sink_ref, x_ref, g_ref, b_ref, w_ref, cos_ref, sin_ref, cost_ref, sint_ref,
                        gq_ref, wuqt_ref, gkv_ref, wuk_ref, wuvt_ref, kmeta_ref, vmeta_ref,
                        a_ref, qt_ref, km_ref, vt_ref,
                        qa_s, ka_s, va_s, kprev_s, vprev_s):
    _in_proj_kernel(x_ref, g_ref, b_ref, w_ref, cos_ref, sin_ref, cost_ref, sint_ref, gq_ref,
                    wuqt_ref, gkv_ref, wuk_ref, wuvt_ref, qa_s, ka_s, va_s, qt_ref, km_ref, vt_ref)
    first = pl.program_id(1) == 0
    kprev = jnp.where(first, kmeta_ref[0], kprev_s[...])
    vprev = jnp.where(first, vmeta_ref[0], vprev_s[...])
    tm = qa_s.shape[1]
    _swa_blocks(sink_ref, qa_s, ka_s, va_s, kprev, vprev, first, a_ref, tm // BLOCK)
    kprev_s[...] = ka_s[0, tm - BLOCK:, :]
    vprev_s[...] = va_s[0, tm - BLOCK:, :]


def _in_proj_swa(x, tables, p, ka_meta, va_meta, tm):
    cos, sin, cos_t, sin_t = tables
    B, S, D = x.shape
    kvw = SWA_KV_COLS
    full = lambda shape: pl.BlockSpec(shape, lambda b, i: (0,) * len(shape))
    row = lambda w: pl.BlockSpec((1, tm, w), lambda b, i: (b, i, 0))
    out_shape = (
        jax.ShapeDtypeStruct((B, S, SWA_Q_COLS), BF16),
        jax.ShapeDtypeStruct((B, MLA_HEADS, MLA_QK_PAD, S), BF16),
        jax.ShapeDtypeStruct((B, MLA_HEADS, S, MLA_QK_PAD), BF16),
        jax.ShapeDtypeStruct((B, MLA_HEADS, S // tm, MLA_V_DIM, tm), BF16),
    )
    return pl.pallas_call(
        _in_proj_swa_kernel,
        grid=(B, S // tm),
        in_specs=[
            pl.BlockSpec(memory_space=pltpu.SMEM),
            row(D),
            full((1, D)), full((1, D)),
            full((D, IN_COLS_PADDED)),
            pl.BlockSpec((tm, LANES), lambda b, i: (i, 0)),
            pl.BlockSpec((tm, LANES), lambda b, i: (i, 0)),
            pl.BlockSpec((MLA_ROPE_DIM, tm), lambda b, i: (0, i)),
            pl.BlockSpec((MLA_ROPE_DIM, tm), lambda b, i: (0, i)),
            full((1, MLA_Q_RANK)), full((MLA_HEADS, MLA_QK_PAD, MLA_Q_RANK)),
            full((1, MLA_KV_RANK)), full((MLA_KV_RANK, MLA_HEADS * MLA_NOPE_DIM)),
            full((MLA_HEADS, MLA_V_DIM, MLA_KV_RANK)),
            full((1, BLOCK, kvw)), full((1, BLOCK, kvw)),
        ],
        out_specs=(row(SWA_Q_COLS),
                   pl.BlockSpec((1, MLA_HEADS, MLA_QK_PAD, tm), lambda b, i: (b, 0, 0, i)),
                   pl.BlockSpec((1, MLA_HEADS, tm, MLA_QK_PAD), lambda b, i: (b, 0, i, 0)),
                   pl.BlockSpec((1, MLA_HEADS, 1, MLA_V_DIM, tm), lambda b, i: (b, 0, i, 0, 0))),
        out_shape=out_shape,
        scratch_shapes=[pltpu.VMEM((1, tm, SWA_Q_COLS), BF16), pltpu.VMEM((1, tm, kvw), BF16),
                        pltpu.VMEM((1, tm, kvw), BF16), pltpu.VMEM((BLOCK, kvw), BF16),
                        pltpu.VMEM((BLOCK, kvw), BF16)],
        compiler_params=_cparams(("parallel", "arbitrary")),
        name="in_proj_swa",
    )(p["sinks"], x, p["ln_in_g"], p["ln_in_b"], p["w_in"], cos, sin, cos_t, sin_t, p["gq"],
      p["w_uq_t"], p["gkv"], p["w_uk"], p["w_uv_t"], ka_meta, va_meta)


def _mla_kernel(qt_ref, k_ref, vt_ref, kmeta_ref, vtmeta_ref, o_ref,
                s0a_ref, s0b_ref, s1a_ref, s1b_ref, smax_ref, m_ref, l_ref, acc_ref, *, tq):
    u = pl.program_id(2)
    qts = (qt_ref[0, 0, :, :tq], qt_ref[0, 0, :, tq:])
    bufs = ((s0a_ref, s0b_ref), (s1a_ref, s1b_ref))

    for g in range(2):
        s = _dot(kmeta_ref[0, 0], qts[g])
        m = jnp.max(s, axis=0, keepdims=True)
        p = jnp.exp2(s - m)
        m_ref[g] = m
        l_ref[g] = jnp.sum(p, axis=0, keepdims=True)
        acc_ref[g] = _dot(vtmeta_ref[0, 0, 0], p.astype(BF16))

    def scores_into(g, ab, j):
        start = pl.multiple_of(j * tq, tq)
        s = _dot(k_ref[0, 0, pl.ds(start, tq), :], qts[g])
        bufs[g][ab][...] = s
        smax_ref[g, ab] = jnp.max(s, axis=0, keepdims=True)

    def absorb(g, ab, j, causal=False):
        s = bufs[g][ab][...]
        if causal:
            key = lax.broadcasted_iota(jnp.int32, (tq, tq), 0)
            query = lax.broadcasted_iota(jnp.int32, (tq, tq), 1)
            s = jnp.where(key <= query, s, NEG)
            smax = jnp.max(s, axis=0, keepdims=True)
        else:
            smax = smax_ref[g, ab]
        m = m_ref[g]
        m_new = jnp.maximum(m, smax)
        alpha = jnp.exp2(m - m_new)
        p = jnp.exp2(s - m_new)
        m_ref[g] = m_new
        l_ref[g] = alpha * l_ref[g] + jnp.sum(p, axis=0, keepdims=True)
        acc_ref[g] = alpha * acc_ref[g] + _dot(vt_ref[0, 0, j], p.astype(BF16))

    buf_a, buf_b = 0, 1
    for g in range(2):
        scores_into(g, buf_a, 0)

    def pair(t, carry):
        for g in range(2):
            scores_into(g, buf_b, 2 * t + 1)
            absorb(g, buf_a, 2 * t)
        for g in range(2):
            scores_into(g, buf_a, 2 * t + 2)
            absorb(g, buf_b, 2 * t + 1)
        return carry

    lax.fori_loop(0, u, pair, 0)

    scores_into(1, buf_b, 2 * u + 1)
    absorb(1, buf_a, 2 * u)
    absorb(0, buf_a, 2 * u, causal=True)
    absorb(1, buf_b, 2 * u + 1, causal=True)

    for g in range(2):
        o_ref[0, g * tq:(g + 1) * tq, :] = (acc_ref[g] * (1.0 / l_ref[g])).T.astype(BF16)


def _mla(qt, km, vt, km_meta, vt_meta, tq):
    B, H, _, S = qt.shape
    assert S % (2 * tq) == 0, "a step handles an (even, odd) pair of query tiles"
    return pl.pallas_call(
        functools.partial(_mla_kernel, tq=tq),
        grid=(B, H, S // (2 * tq)),
        in_specs=[
            pl.BlockSpec((1, 1, MLA_QK_PAD, 2 * tq), lambda b, h, u: (b, h, 0, u)),
            pl.BlockSpec((1, 1, S, MLA_QK_PAD), lambda b, h, u: (b, h, 0, 0)),
            pl.BlockSpec((1, 1, S // tq, MLA_V_DIM, tq), lambda b, h, u: (b, h, 0, 0, 0)),
            pl.BlockSpec((1, 1, N_META, MLA_QK_PAD), lambda b, h, u: (0, h, 0, 0)),
            pl.BlockSpec((1, 1, 1, MLA_V_DIM, N_META), lambda b, h, u: (0, h, 0, 0, 0)),
        ],
        out_specs=pl.BlockSpec((1, 2 * tq, MLA_V_DIM), lambda b, h, u: (b, u, h)),
        out_shape=jax.ShapeDtypeStruct((B, S, MLA_OUT_COLS), BF16),
        scratch_shapes=[pltpu.VMEM((tq, tq), F32), pltpu.VMEM((tq, tq), F32),
                        pltpu.VMEM((tq, tq), F32), pltpu.VMEM((tq, tq), F32),
                        pltpu.VMEM((2, 2, 1, tq), F32),
                        pltpu.VMEM((2, 1, tq), F32), pltpu.VMEM((2, 1, tq), F32),
                        pltpu.VMEM((2, MLA_V_DIM, tq), F32)],
        compiler_params=_cparams(("parallel", "parallel", "arbitrary")),
        name="mla_attn",
    )(qt, km, vt, km_meta, vt_meta)


def _out_proj_kernel(x_ref, a_ref, b_ref, gin_ref, bin_ref, ga_ref, gb_ref, wo_ref, g1_ref, b1_ref,
                     wrt_ref, brt_ref, h1_ref, route_ref, counts_out_ref, count_ref):
    h = _layer_norm(x_ref[0], gin_ref[...], bin_ref[...])
    a = a_ref[0].astype(F32)
    b = b_ref[0].astype(F32)
    an = (a * _rms_scale(a) * ga_ref[...]).astype(BF16)
    bn = (b * _rms_scale(b) * gb_ref[...]).astype(BF16)
    mix = _dot(jnp.concatenate([an, bn], axis=-1), wo_ref[...])
    h1 = _layer_norm(ALPHA * h + mix, g1_ref[...], b1_ref[...])
    h1_ref[0] = h1

    x_hi = h1.astype(BF16)
    x_lo = (h1 - x_hi.astype(F32)).astype(BF16)
    hi_both = _dot(x_hi, wrt_ref[...])
    logits = (hi_both[:, :ROUTE_LANES] + hi_both[:, ROUTE_LANES:]
              + _dot(x_lo, wrt_ref[:, :ROUTE_LANES]) + brt_ref[...])
    lane = lax.broadcasted_iota(jnp.int32, logits.shape, 1).astype(F32)
    ninf = -jnp.inf
    no_lane = float(ROUTE_LANES)
    gl = jnp.where(lane < N_GROUPS, logits, ninf)
    gmax = jnp.max(gl, axis=-1, keepdims=True)
    g_idx = jnp.min(jnp.where(gl == gmax, lane, no_lane), axis=-1, keepdims=True)
    lo = EXPERT_LANE0 + g_idx * EXPERTS_PER_GROUP
    el = jnp.where((lane >= lo) & (lane < lo + EXPERTS_PER_GROUP), logits, ninf)
    m1 = jnp.max(el, axis=-1, keepdims=True)
    i1 = jnp.min(jnp.where(el == m1, lane, no_lane), axis=-1, keepdims=True)
    el2 = jnp.where(lane == i1, ninf, el)
    m2 = jnp.max(el2, axis=-1, keepdims=True)
    i2 = jnp.min(jnp.where(el2 == m2, lane, no_lane), axis=-1, keepdims=True)

    a = jnp.minimum(i1, i2) - lo
    b = jnp.maximum(i1, i2) - lo
    cls = (g_idx * PAIRS_PER_GROUP + a * (2.0 * EXPERTS_PER_GROUP - 1.0 - a) * 0.5
           + (b - a - 1.0))

    @pl.when((pl.program_id(0) == 0) & (pl.program_id(1) == 0))
    def _():
        count_ref[...] = jnp.zeros_like(count_ref)

    tm = logits.shape[0]
    onehot = jnp.where(lane == cls, 1.0, 0.0)
    r_i = lax.broadcasted_iota(jnp.int32, (BLOCK, BLOCK), 0)
    c_i = lax.broadcasted_iota(jnp.int32, (BLOCK, BLOCK), 1)
    earlier = jnp.where(r_i > c_i, 1.0, 0.0).astype(BF16)
    running = count_ref[...]
    before = []
    for blk in range(tm // BLOCK):
        oh = onehot[blk * BLOCK:(blk + 1) * BLOCK]
        before.append(_dot(earlier, oh.astype(BF16)) + running)
        running = running + jnp.sum(oh, axis=0, keepdims=True)
    before = jnp.concatenate(before, axis=0)
    rank = jnp.sum(jnp.where(lane == cls, before, 0.0), axis=-1, keepdims=True)
    count_ref[...] = running
    counts_out_ref[...] = running

    route = jnp.where(lane == ROUTE_CLS, cls, 0.0)
    route = jnp.where(lane == ROUTE_RANK, rank, route)
    route_ref[0] = route


def _out_proj(x, a_out, b_out, p, tm):
    B, S, D = x.shape
    full = lambda shape: pl.BlockSpec(shape, lambda b, i: (0,) * len(shape))
    row = lambda w: pl.BlockSpec((1, tm, w), lambda b, i: (b, i, 0))
    return pl.pallas_call(
        _out_proj_kernel,
        grid=(B, S // tm),
        in_specs=[row(D), row(SWA_Q_COLS), row(MLA_OUT_COLS),
                  full((1, D)), full((1, D)), full((1, SWA_Q_COLS)), full((1, MLA_OUT_COLS)),
                  full((D, D)), full((1, D)), full((1, D)),
                  full((D, 2 * ROUTE_LANES)), full((1, ROUTE_LANES))],
        out_specs=(row(D), row(ROUTE_LANES), full((1, ROUTE_LANES))),
        out_shape=(jax.ShapeDtypeStruct((B, S, D), F32),
                   jax.ShapeDtypeStruct((B, S, ROUTE_LANES), F32),
                   jax.ShapeDtypeStruct((1, ROUTE_LANES), F32)),
        scratch_shapes=[pltpu.VMEM((1, ROUTE_LANES), F32)],
        compiler_params=_cparams(("arbitrary", "arbitrary")),
        name="out_proj_route",
    )(x, a_out, b_out, p["ln_in_g"], p["ln_in_b"], p["ga"], p["gb"], p["w_o"], p["ln1_g"],
      p["ln1_b"], p["w_route"], p["b_route"])


SLAB = D_MODEL // LANES


def _slab_copy(src_ref, src_tok8, dst_ref, dst_tok8, sem):
    return pltpu.make_async_copy(src_ref.at[pl.ds(pl.multiple_of(src_tok8, SLAB), SLAB)],
                                 dst_ref.at[pl.ds(pl.multiple_of(dst_tok8, SLAB), SLAB)], sem)


def _to_slabs(x, slab_ref, n):
    for s in range(SLAB):
        slab_ref[pl.ds(s, n, stride=SLAB), :] = x[:, s * LANES:(s + 1) * LANES]


def _from_slabs(slab_ref, n):
    return jnp.concatenate([slab_ref[pl.ds(s, n, stride=SLAB), :] for s in range(SLAB)], axis=1)


def _dispatch_kernel(padstart_ref, nu_ref, pos_ref, h1_ref, xs_ref, rows_ref, zero_ref, sem, *,
                     tm, n_tiles):
    tail = zero_ref.shape[0]

    def zero_fill(start):
        return pltpu.make_async_copy(
            zero_ref, xs_ref.at[pl.ds(pl.multiple_of(start, SLAB), tail)], sem)

    @pl.when(pl.program_id(0) == 0)
    def _():
        zero_ref[...] = jnp.zeros_like(zero_ref)
        for parity in range(2):
            tails = [zero_fill(padstart_ref[c]) for c in range(parity, N_CLASSES, 2)]
            for t in tails:
                t.start()
            for t in tails:
                t.wait()

        def fill_unused(j, carry):
            fill = zero_fill(j * tail)
            fill.start()
            fill.wait()
            return carry

        lax.fori_loop(nu_ref[0], n_tiles + 1, fill_unused, 0)

    _to_slabs(h1_ref[...], rows_ref, tm)

    def issue(r0, carry):
        for k in range(DMA_BURST):
            r = r0 * DMA_BURST + k
            _slab_copy(rows_ref, r * SLAB, xs_ref, pos_ref[0, 0, r], sem).start(priority=k % 2)
        return carry

    lax.fori_loop(0, tm // DMA_BURST, issue, 0)
    pltpu.make_async_copy(rows_ref, xs_ref.at[pl.ds(0, tm * SLAB)], sem).wait()


def _dispatch(h1, pos8, pad_start8, n_used, n_tiles, tm):
    T, D = h1.shape
    pos8 = pos8.reshape(T // tm, 1, tm)
    n_rows = (n_tiles + 1) * MOE_TILE_ROWS
    grid_spec = pltpu.PrefetchScalarGridSpec(
        num_scalar_prefetch=2,
        grid=(T // tm,),
        in_specs=[pl.BlockSpec((1, 1, tm), lambda t, ps, nu: (t, 0, 0),
                               memory_space=pltpu.SMEM),
                  pl.BlockSpec((tm, D), lambda t, ps, nu: (t, 0))],
        out_specs=pl.BlockSpec(memory_space=pl.ANY),
        scratch_shapes=[pltpu.VMEM((tm * SLAB, LANES), F32),
                        pltpu.VMEM((MOE_TILE_ROWS * SLAB, LANES), F32),
                        pltpu.SemaphoreType.DMA],
    )
    return pl.pallas_call(
        functools.partial(_dispatch_kernel, tm=tm, n_tiles=n_tiles),
        grid_spec=grid_spec,
        out_shape=jax.ShapeDtypeStruct((n_rows * SLAB, LANES), F32),
        compiler_params=_cparams(("arbitrary",)),
        name="moe_dispatch",
    )(pad_start8, n_used, pos8, h1)


def _experts_kernel(tlo_ref, thi_ref, nu_ref, x_ref, wgu_lo_ref, wgu_hi_ref, wd_lo_ref, wd_hi_ref,
                    wrt_ref, brt_ref, y_ref):
    j = pl.program_id(0)
    tr = MOE_TILE_ROWS

    @pl.when(j < nu_ref[0])
    def _():
        x = _from_slabs(x_ref, tr).astype(BF16)
        lo_e = tlo_ref[j]
        hi_e = thi_ref[j]
        logits = _dot(x, wrt_ref[...]) + brt_ref[...]
        lane = lax.broadcasted_iota(jnp.int32, logits.shape, 1)
        gl = jnp.where(lane < N_GROUPS, logits, -jnp.inf)
        ge = jnp.exp(gl - jnp.max(gl, axis=-1, keepdims=True))
        pick = lambda v, at: jnp.sum(jnp.where(lane == at, v, 0.0), axis=-1, keepdims=True)
        g_prob = pick(ge, lo_e // EXPERTS_PER_GROUP) / jnp.sum(ge, axis=-1, keepdims=True)
        e_lo = pick(logits, EXPERT_LANE0 + lo_e)
        e_hi = pick(logits, EXPERT_LANE0 + hi_e)
        gates = (g_prob / (1.0 + jnp.exp(e_hi - e_lo)), g_prob / (1.0 + jnp.exp(e_lo - e_hi)))

        y = None
        for gate, wgu_ref, wd_ref in zip(gates, (wgu_lo_ref, wgu_hi_ref), (wd_lo_ref, wd_hi_ref)):
            gu = _dot(x, wgu_ref[0])
            g = gu[:, :D_EXPERT]
            u = gu[:, D_EXPERT:]
            hid = g * (1.0 / (1.0 + jnp.exp(-g))) * u * gate
            y_k = _dot(hid.astype(BF16), wd_ref[0])
            y = y_k if y is None else y + y_k
        _to_slabs(y, y_ref, tr)

    @pl.when(j >= nu_ref[0])
    def _():
        y_ref[...] = jnp.zeros_like(y_ref)


def _experts(xs, tile_lo, tile_hi, n_used, p, n_tiles):
    D = D_MODEL
    tr = MOE_TILE_ROWS
    last_used = lambda j, nu: jnp.minimum(j, nu[0] - 1)
    wgu = lambda pick: pl.BlockSpec((1, D, 2 * D_EXPERT),
                                    lambda j, tlo, thi, nu: (pick(tlo, thi)[j], 0, 0))
    wd = lambda pick: pl.BlockSpec((1, D_EXPERT, D),
                                   lambda j, tlo, thi, nu: (pick(tlo, thi)[j], 0, 0))
    lo = lambda tlo, thi: tlo
    hi = lambda tlo, thi: thi
    grid_spec = pltpu.PrefetchScalarGridSpec(
        num_scalar_prefetch=3,
        grid=(n_tiles,),
        in_specs=[pl.BlockSpec((tr * SLAB, LANES), lambda j, tlo, thi, nu: (last_used(j, nu), 0)),
                  wgu(lo), wgu(hi), wd(lo), wd(hi),
                  pl.BlockSpec((D, ROUTE_LANES), lambda j, tlo, thi, nu: (0, 0)),
                  pl.BlockSpec((1, ROUTE_LANES), lambda j, tlo, thi, nu: (0, 0))],
        out_specs=pl.BlockSpec((tr * SLAB, LANES), lambda j, tlo, thi, nu: (j, 0)),
    )
    return pl.pallas_call(
        _experts_kernel,
        grid_spec=grid_spec,
        out_shape=jax.ShapeDtypeStruct((n_tiles * tr * SLAB, LANES), F32),
        compiler_params=_cparams(("arbitrary",)),
        name="moe_experts",
    )(tile_lo, tile_hi, n_used, xs, p["w_gu"], p["w_gu"], p["w_down"], p["w_down"],
      p["w_route"], p["b_route"])


def _combine_kernel(pos_ref, pos_next_ref, h1_ref, ys_ref, g2_ref, b2_ref, o_ref,
                    ybuf_ref, sem, *, tm):
    t = pl.program_id(0)
    slot = t % 2

    def gather(idx_ref, dst_slot):
        def issue(r0, carry):
            for k in range(DMA_BURST):
                r = r0 * DMA_BURST + k
                _slab_copy(ys_ref, idx_ref[0, 0, r], ybuf_ref.at[dst_slot], r * SLAB,
                           sem.at[dst_slot]).start(priority=k % 2)
            return carry

        lax.fori_loop(0, tm // DMA_BURST, issue, 0)

    @pl.when(t == 0)
    def _():
        gather(pos_ref, 0)

    @pl.when(t + 1 < pl.num_programs(0))
    def _():
        gather(pos_next_ref, 1 - slot)

    pltpu.make_async_copy(ys_ref.at[pl.ds(0, tm * SLAB)], ybuf_ref.at[slot], sem.at[slot]).wait()
    ffn = _from_slabs(ybuf_ref.at[slot], tm)
    o_ref[...] = _layer_norm(ALPHA * h1_ref[...] + ffn, g2_ref[...], b2_ref[...])


def _combine(h1, pos8, ys, p, tm):
    T, D = h1.shape
    n_steps = T // tm
    pos8 = pos8.reshape(n_steps, 1, tm)
    full = lambda shape: pl.BlockSpec(shape, lambda t: (0,) * len(shape))
    return pl.pallas_call(
        functools.partial(_combine_kernel, tm=tm),
        grid=(n_steps,),
        in_specs=[pl.BlockSpec((1, 1, tm), lambda t: (t, 0, 0), memory_space=pltpu.SMEM),
                  pl.BlockSpec((1, 1, tm), lambda t: (jnp.minimum(t + 1, n_steps - 1), 0, 0),
                               memory_space=pltpu.SMEM),
                  pl.BlockSpec((tm, D), lambda t: (t, 0)),
                  pl.BlockSpec(memory_space=pl.ANY),
                  full((1, D)), full((1, D))],
        out_specs=pl.BlockSpec((tm, D), lambda t: (t, 0)),
        out_shape=jax.ShapeDtypeStruct((T, D), F32),
        scratch_shapes=[pltpu.VMEM((2, tm * SLAB, LANES), F32),
                        pltpu.SemaphoreType.DMA((2,))],
        compiler_params=_cparams(("arbitrary",)),
        name="moe_combine_ln2",
    )(pos8, pos8, h1, ys, p["ln2_g"], p["ln2_b"])


def _dispatch_plan(route, counts):
    T = route.shape[0]
    tr = MOE_TILE_ROWS
    n_tiles = T // tr + N_CLASSES
    cnt = counts[0, :N_CLASSES].astype(jnp.int32)
    padded = jnp.maximum((cnt + tr - 1) // tr, 1) * tr
    classes = jnp.arange(N_CLASSES, dtype=jnp.int32)
    seg_end = jnp.sum(jnp.where(classes[None, :] <= classes[:, None], padded[None, :], 0), axis=1)
    seg_start = seg_end - padded
    cls = route[:, ROUTE_CLS].astype(jnp.int32)
    rank = route[:, ROUTE_RANK].astype(jnp.int32)
    start_of = jnp.sum(jnp.where(cls[:, None] == classes, seg_start, 0), axis=-1)
    pos8 = (start_of + rank) * SLAB
    tile_row0 = jnp.arange(n_tiles, dtype=jnp.int32) * tr
    tile_cls = jnp.sum((seg_end[None, :] <= tile_row0[:, None]).astype(jnp.int32), axis=-1)
    tile_cls = jnp.minimum(tile_cls, N_CLASSES - 1)
    pairs = [(a, b) for a in range(EXPERTS_PER_GROUP) for b in range(a + 1, EXPERTS_PER_GROUP)]
    lo_of = jnp.asarray([g * EXPERTS_PER_GROUP + a for g in range(N_GROUPS) for a, _ in pairs],
                        jnp.int32)
    hi_of = jnp.asarray([g * EXPERTS_PER_GROUP + b for g in range(N_GROUPS) for _, b in pairs],
                        jnp.int32)
    is_cls = tile_cls[:, None] == classes
    tile_lo = jnp.sum(jnp.where(is_cls, lo_of, 0), axis=-1)
    tile_hi = jnp.sum(jnp.where(is_cls, hi_of, 0), axis=-1)
    n_used = (seg_end[-1:] // tr).astype(jnp.int32)
    return pos8, (seg_start + cnt) * SLAB, tile_lo, tile_hi, n_used, n_tiles


def _rope_tables(pos):
    d = SWA_HEAD_DIM
    inv_freq = ROPE_THETA ** (-jnp.arange(0, d, 2, dtype=F32) / d)
    ang = pos[:, None] * inv_freq[None, :]
    c, s = jnp.cos(ang), jnp.sin(ang)
    return (jnp.concatenate([c, c, c, c], axis=-1), jnp.concatenate([-s, s, -s, s], axis=-1),
            jnp.concatenate([c, c], axis=-1).T, jnp.concatenate([s, s], axis=-1).T)


def _prep_params(ln_in_g, ln_in_b, w_in, swa_sinks, mla_q_norm_g, mla_w_uq, mla_kv_norm_g, mla_w_ukv,
                 swa_out_norm_g, mla_out_norm_g, w_o, ln1_g, ln1_b, moe_w_group, moe_b_group,
                 moe_w_router, moe_b_router, moe_w_gate, moe_w_up, moe_w_down, ln2_g, ln2_b):
    order = jnp.asarray(SWA_HEAD_ORDER)
    perm = (order[:, None] * SWA_HEAD_DIM + jnp.arange(SWA_HEAD_DIM)[None, :]).reshape(-1)
    w = w_in[0]
    w_q = w[:, :SWA_Q_COLS][:, perm]
    w_pad = jnp.zeros((D_MODEL, IN_COLS_PADDED - w.shape[1]), w.dtype)
    w_in_p = jnp.concatenate([w_q, w[:, SWA_Q_COLS:], w_pad], axis=1).astype(BF16)

    dqk = MLA_NOPE_DIM + MLA_ROPE_DIM
    wuq = mla_w_uq[0].reshape(MLA_Q_RANK, MLA_HEADS, dqk)
    wuq = jnp.pad(wuq, ((0, 0), (0, 0), (0, MLA_QK_PAD - dqk)))
    wuq_t = jnp.transpose(wuq, (1, 2, 0)).astype(BF16)
    wukv = mla_w_ukv[0].reshape(MLA_KV_RANK, MLA_HEADS, MLA_NOPE_DIM + MLA_V_DIM)
    wuk = wukv[:, :, :MLA_NOPE_DIM].reshape(MLA_KV_RANK, MLA_HEADS * MLA_NOPE_DIM).astype(BF16)
    wuv_t = jnp.transpose(wukv[:, :, MLA_NOPE_DIM:], (1, 2, 0)).astype(BF16)

    w_o0 = w_o[0]
    w_o_p = jnp.concatenate([w_o0[:SWA_Q_COLS][perm], w_o0[SWA_Q_COLS:]], axis=0).astype(BF16)

    w_route = jnp.zeros((D_MODEL, ROUTE_LANES), F32)
    w_route = w_route.at[:, :N_GROUPS].set(moe_w_group[0])
    w_route = w_route.at[:, EXPERT_LANE0:EXPERT_LANE0 + N_EXPERTS].set(moe_w_router[0])
    w_route_hi = w_route.astype(BF16)
    w_route_lo = (w_route - w_route_hi.astype(F32)).astype(BF16)
    w_route = jnp.concatenate([w_route_hi, w_route_lo], axis=1)
    b_route = jnp.zeros((1, ROUTE_LANES), F32)
    b_route = b_route.at[0, :N_GROUPS].set(moe_b_group[0])
    b_route = b_route.at[0, EXPERT_LANE0:EXPERT_LANE0 + N_EXPERTS].set(moe_b_router[0])

    return {
        "ln_in_g": ln_in_g[None, :], "ln_in_b": ln_in_b[None, :],
        "w_in": w_in_p,
        "sinks": swa_sinks[0],
        "gq": mla_q_norm_g, "w_uq_t": wuq_t,
        "gkv": mla_kv_norm_g, "w_uk": wuk, "w_uv_t": wuv_t,
        "ga": swa_out_norm_g[0][perm][None, :], "gb": mla_out_norm_g,
        "w_o": w_o_p, "ln1_g": ln1_g, "ln1_b": ln1_b,
        "w_route": w_route, "b_route": b_route,
        "w_gu": jnp.concatenate([moe_w_gate[0], moe_w_up[0]], axis=-1).astype(BF16),
        "w_down": moe_w_down[0].astype(BF16),
        "ln2_g": ln2_g, "ln2_b": ln2_b,
    }


def kernel(x, meta_tokens, ln_in_g, ln_in_b, w_in, swa_sinks, mla_q_norm_g, mla_w_uq, mla_kv_norm_g, mla_w_ukv, swa_out_norm_g, mla_out_norm_g, w_o, ln1_g, ln1_b, moe_w_group, moe_b_group, moe_w_router, moe_b_router, moe_w_gate, moe_w_up, moe_w_down, ln2_g, ln2_b):
    B, S, D = x.shape
    p = _prep_params(ln_in_g, ln_in_b, w_in, swa_sinks, mla_q_norm_g, mla_w_uq, mla_kv_norm_g,
                     mla_w_ukv, swa_out_norm_g, mla_out_norm_g, w_o, ln1_g, ln1_b, moe_w_group,
                     moe_b_group, moe_w_router, moe_b_router, moe_w_gate, moe_w_up, moe_w_down,
                     ln2_g, ln2_b)
    T = B * S
    tm = min(PROJ_TILE, S)

    x_meta = jnp.concatenate([jnp.zeros((N_PAD, D), x.dtype), meta_tokens.astype(x.dtype)])[None]
    pos_meta = jnp.maximum(jnp.arange(BLOCK) - N_PAD, 0).astype(F32)
    pos_tok = (jnp.arange(S) + N_META).astype(F32)
    _, ka_m, va_m, _, km_m, vt_m = _in_proj(x_meta, _rope_tables(pos_meta), p, BLOCK)
    a_out, qt, km, vt = _in_proj_swa(x, _rope_tables(pos_tok), p, ka_m, va_m, tm)
    b_out = _mla(qt, km, vt, km_m[:, :, N_PAD:, :], vt_m[..., N_PAD:], tm)
    h1, route, counts = _out_proj(x, a_out, b_out, p, min(WIDE_TILE, S))
    h1 = h1.reshape(B * S, D)
    route = route.reshape(B * S, ROUTE_LANES)

    pos8, pad_start8, tile_lo, tile_hi, n_used, n_tiles = _dispatch_plan(route, counts)
    xs = _dispatch(h1, pos8, pad_start8, n_used, n_tiles, min(DISPATCH_TILE, T))
    ys = _experts(xs, tile_lo, tile_hi, n_used, p, n_tiles)
    out = _combine(h1, pos8, ys, p, min(COMBINE_TILE, T))
    return out.reshape(B, S, D)
```

```python
import functools

import jax
import jax.numpy as jnp
from jax import lax
from jax.experimental import pallas as pl
from jax.experimental.pallas import tpu as pltpu

D_MODEL = 1024
N_META = 16
BLOCK = 128
N_PAD = BLOCK - N_META
ROPE_THETA = 10000.0
SWA_HEADS = 8
SWA_KV_HEADS = 2
SWA_HEAD_DIM = 64
MLA_HEADS = 4
MLA_Q_RANK = 256
MLA_KV_RANK = 256
MLA_NOPE_DIM = 128
MLA_ROPE_DIM = 64
MLA_V_DIM = 128
SWA_Q_COLS = SWA_HEADS * SWA_HEAD_DIM
SWA_KV_COLS = SWA_KV_HEADS * SWA_HEAD_DIM
MLA_OUT_COLS = MLA_HEADS * MLA_V_DIM
N_GROUPS = 4
EXPERTS_PER_GROUP = 8
N_EXPERTS = N_GROUPS * EXPERTS_PER_GROUP
D_EXPERT = 256
LN_EPS = 1e-5
RMS_EPS = 1e-6
DEPTH = 1
ALPHA = (2.0 * DEPTH) ** 0.25
NEG = -1e30
LOG2_E = 1.4426950408889634

LANES = 128
IN_COLS_PADDED = 1408
MLA_QK_PAD = 256
ROUTE_LANES = 128
EXPERT_LANE0 = 32
ROUTE_CLS, ROUTE_RANK = range(2)
ROUTE_ROWS = 8
PAIRS_PER_GROUP = EXPERTS_PER_GROUP * (EXPERTS_PER_GROUP - 1) // 2
N_CLASSES = N_GROUPS * PAIRS_PER_GROUP
MOE_TILE_ROWS = 256
PROJ_TILE = 1024
WIDE_TILE = 1024
DISPATCH_TILE = 2048
COMBINE_TILE = 1024
DMA_BURST = 8
VMEM_LIMIT = 56 * 1024 * 1024

SWA_HEAD_ORDER = (0, 4, 1, 5, 2, 6, 3, 7)

BF16 = jnp.bfloat16
F32 = jnp.float32


def _cparams(sem):
    return pltpu.CompilerParams(dimension_semantics=sem, vmem_limit_bytes=VMEM_LIMIT)


def _layer_norm(x, g, b):
    mu = jnp.mean(x, axis=-1, keepdims=True)
    xc = x - mu
    var = jnp.mean(xc * xc, axis=-1, keepdims=True)
    return xc * lax.rsqrt(var + LN_EPS) * g + b


def _rms_scale(x):
    return lax.rsqrt(jnp.mean(x * x, axis=-1, keepdims=True) + RMS_EPS)


def _rope128(x, cos, sin_signed, lane_lo):
    up = pltpu.roll(x, LANES - 32, 1)
    dn = pltpu.roll(x, 32, 1)
    return x * cos + jnp.where(lane_lo, up, dn) * sin_signed


def _dot(a, b):
    return jnp.dot(a, b, preferred_element_type=F32)


def _dot_nt(a, b):
    return lax.dot_general(a, b, (((1,), (1,)), ((), ())), preferred_element_type=F32)


def _in_proj_kernel(x_ref, g_ref, b_ref, w_ref, cos_ref, sin_ref, cost_ref, sint_ref, gq_ref,
                    wuqt_ref, gkv_ref, wuk_ref, wuvt_ref,
                    qa_ref, ka_ref, va_ref, qt_ref, km_ref, vt_ref):
    h = _layer_norm(x_ref[0], g_ref[...], b_ref[...])
    u = _dot(h.astype(BF16), w_ref[...])
    cos = cos_ref[...]
    sin = sin_ref[...]
    lane = lax.broadcasted_iota(jnp.int32, (1, LANES), 1)
    lane_lo = (lane % 64) < 32
    rope = functools.partial(_rope128, cos=cos, sin_signed=sin, lane_lo=lane_lo)

    swa_scale = SWA_HEAD_DIM ** -0.5
    for c in range(SWA_Q_COLS // LANES):
        qc = rope(u[:, c * LANES:(c + 1) * LANES]) * swa_scale
        qa_ref[0, :, c * LANES:(c + 1) * LANES] = qc.astype(BF16)
    ka_ref[0] = rope(u[:, 512:640]).astype(BF16)
    va_ref[0] = u[:, 640:768].astype(BF16)

    cq = u[:, 768:1024]
    cqn = (cq * _rms_scale(cq) * gq_ref[...]).astype(BF16)
    ckv = u[:, 1024:1280]
    ckvn = (ckv * _rms_scale(ckv) * gkv_ref[...]).astype(BF16)
    k_nope = _dot(ckvn, wuk_ref[...])
    kr = rope(u[:, 1280:1408]).astype(BF16)
    q_scale = (MLA_NOPE_DIM + MLA_ROPE_DIM) ** -0.5 * LOG2_E
    cost = cost_ref[...]
    sint = sint_ref[...]
    half = MLA_ROPE_DIM // 2
    for hd in range(MLA_HEADS):
        qt = _dot_nt(wuqt_ref[hd], cqn)
        qr = qt[MLA_NOPE_DIM:MLA_NOPE_DIM + MLA_ROPE_DIM]
        rot = jnp.concatenate([-qr[half:], qr[:half]], axis=0)
        qt_ref[0, hd, 0:MLA_NOPE_DIM, :] = (qt[:MLA_NOPE_DIM] * q_scale).astype(BF16)
        qt_ref[0, hd, MLA_NOPE_DIM:MLA_NOPE_DIM + MLA_ROPE_DIM, :] = (
            (qr * cost + rot * sint) * q_scale).astype(BF16)
        qt_ref[0, hd, MLA_NOPE_DIM + MLA_ROPE_DIM:, :] = (
            qt[MLA_NOPE_DIM + MLA_ROPE_DIM:]).astype(BF16)
        km_ref[0, hd, :, 0:LANES] = k_nope[:, hd * LANES:(hd + 1) * LANES].astype(BF16)
        km_ref[0, hd, :, LANES:2 * LANES] = kr
        vt_ref[0, hd, 0] = _dot_nt(wuvt_ref[hd], ckvn).astype(BF16)


def _in_proj(x, tables, p, tm):
    cos, sin, cos_t, sin_t = tables
    B, S, D = x.shape
    grid = (B, S // tm)
    full = lambda shape: pl.BlockSpec(shape, lambda b, i: (0,) * len(shape))
    out_shape = (
        jax.ShapeDtypeStruct((B, S, SWA_Q_COLS), BF16),
        jax.ShapeDtypeStruct((B, S, SWA_KV_COLS), BF16),
        jax.ShapeDtypeStruct((B, S, SWA_KV_COLS), BF16),
        jax.ShapeDtypeStruct((B, MLA_HEADS, MLA_QK_PAD, S), BF16),
        jax.ShapeDtypeStruct((B, MLA_HEADS, S, MLA_QK_PAD), BF16),
        jax.ShapeDtypeStruct((B, MLA_HEADS, S // tm, MLA_V_DIM, tm), BF16),
    )
    row = lambda w: pl.BlockSpec((1, tm, w), lambda b, i: (b, i, 0))
    head = lambda w: pl.BlockSpec((1, MLA_HEADS, tm, w), lambda b, i: (b, 0, i, 0))
    return pl.pallas_call(
        _in_proj_kernel,
        grid=grid,
        in_specs=[
            row(D),
            full((1, D)), full((1, D)),
            full((D, IN_COLS_PADDED)),
            pl.BlockSpec((tm, LANES), lambda b, i: (i, 0)),
            pl.BlockSpec((tm, LANES), lambda b, i: (i, 0)),
            pl.BlockSpec((MLA_ROPE_DIM, tm), lambda b, i: (0, i)),
            pl.BlockSpec((MLA_ROPE_DIM, tm), lambda b, i: (0, i)),
            full((1, MLA_Q_RANK)), full((MLA_HEADS, MLA_QK_PAD, MLA_Q_RANK)),
            full((1, MLA_KV_RANK)), full((MLA_KV_RANK, MLA_HEADS * MLA_NOPE_DIM)),
            full((MLA_HEADS, MLA_V_DIM, MLA_KV_RANK)),
        ],
        out_specs=(row(SWA_Q_COLS), row(SWA_KV_COLS), row(SWA_KV_COLS),
                   pl.BlockSpec((1, MLA_HEADS, MLA_QK_PAD, tm), lambda b, i: (b, 0, 0, i)),
                   head(MLA_QK_PAD),
                   pl.BlockSpec((1, MLA_HEADS, 1, MLA_V_DIM, tm), lambda b, i: (b, 0, i, 0, 0))),
        out_shape=out_shape,
        compiler_params=_cparams(("parallel", "parallel")),
        name="in_proj",
    )(x, p["ln_in_g"], p["ln_in_b"], p["w_in"], cos, sin, cos_t, sin_t, p["gq"], p["w_uq_t"],
      p["gkv"], p["w_uk"], p["w_uv_t"])


def _swa_blocks(sink_ref, qa_ref, kc_ref, vc_ref, kprev, vprev, first, o_ref, blocks_per_step):
    row = lax.broadcasted_iota(jnp.int32, (2 * BLOCK, BLOCK), 0) % BLOCK
    col = lax.broadcasted_iota(jnp.int32, (2 * BLOCK, BLOCK), 1)
    own = col <= row
    meta_ok = col >= jnp.where(first, N_PAD, 0)
    lane = lax.broadcasted_iota(jnp.int32, (1, LANES), 1)
    lane_kv0 = lane < SWA_HEAD_DIM
    top_rows = lax.broadcasted_iota(jnp.int32, (2 * BLOCK, 1), 0) < BLOCK

    for r in range(blocks_per_step):
        rows = slice(r * BLOCK, (r + 1) * BLOCK)
        if r == 0:
            kp, vp = kprev, vprev
        else:
            prev_rows = slice((r - 1) * BLOCK, r * BLOCK)
            kp, vp = kc_ref[0, prev_rows, :], vc_ref[0, prev_rows, :]
        kc, vc = kc_ref[0, rows, :], vc_ref[0, rows, :]
        for c in range(SWA_Q_COLS // LANES):
            qc = qa_ref[0, rows, c * LANES:(c + 1) * LANES]
            zero = jnp.zeros_like(qc)
            q2 = jnp.concatenate([jnp.where(lane_kv0, qc, zero),
                                  jnp.where(lane_kv0, zero, qc)], axis=0)
            s_prev = _dot_nt(q2, kp)
            if r == 0:
                s_prev = jnp.where(meta_ok, s_prev, NEG)
            s = jnp.where(own, _dot_nt(q2, kc), s_prev)
            sink = jnp.where(top_rows, sink_ref[c], sink_ref[c + 4])
            m = jnp.maximum(jnp.max(s, axis=-1, keepdims=True), sink)
            e = jnp.exp(s - m)
            denom = jnp.sum(e, axis=-1, keepdims=True) + jnp.exp(sink - m)
            pr = e * (1.0 / denom)
            o2 = (_dot(jnp.where(own, pr, 0.0).astype(BF16), vc)
                  + _dot(jnp.where(own, 0.0, pr).astype(BF16), vp))
            o = jnp.where(lane_kv0, o2[:BLOCK], o2[BLOCK:])
            o_ref[0, rows, c * LANES:(c + 1) * LANES] = o.astype(BF16)


def _in_proj_swa_kernel(sink_ref, x_ref, g_ref, b_ref, w_ref, cos_ref, sin_ref, cost_ref, sint_ref,
                        gq_ref, wuqt_ref, gkv_ref, wuk_ref, wuvt_ref, kmeta_ref, vmeta_ref,
                        a_ref, qt_ref, km_ref, vt_ref,
                        qa_s, ka_s, va_s, kprev_s, vprev_s):
    _in_proj_kernel(x_ref, g_ref, b_ref, w_ref, cos_ref, sin_ref, cost_ref, sint_ref, gq_ref,
                    wuqt_ref, gkv_ref, wuk_ref, wuvt_ref, qa_s, ka_s, va_s, qt_ref, km_ref, vt_ref)
    first = pl.program_id(1) == 0
    kprev = jnp.where(first, kmeta_ref[0], kprev_s[...])
    vprev = jnp.where(first, vmeta_ref[0], vprev_s[...])
    tm = qa_s.shape[1]
    _swa_blocks(sink_ref, qa_s, ka_s, va_s, kprev, vprev, first, a_ref, tm // BLOCK)
    kprev_s[...] = ka_s[0, tm - BLOCK:, :]
    vprev_s[...] = va_s[0, tm - BLOCK:, :]


def _in_proj_swa(x, tables, p, ka_meta, va_meta, tm):
    cos, sin, cos_t, sin_t = tables
    B, S, D = x.shape
    kvw = SWA_KV_COLS
    full = lambda shape: pl.BlockSpec(shape, lambda b, i: (0,) * len(shape))
    row = lambda w: pl.BlockSpec((1, tm, w), lambda b, i: (b, i, 0))
    out_shape = (
        jax.ShapeDtypeStruct((B, S, SWA_Q_COLS), BF16),
        jax.ShapeDtypeStruct((B, MLA_HEADS, MLA_QK_PAD, S), BF16),
        jax.ShapeDtypeStruct((B, MLA_HEADS, S, MLA_QK_PAD), BF16),
        jax.ShapeDtypeStruct((B, MLA_HEADS, S // tm, MLA_V_DIM, tm), BF16),
    )
    return pl.pallas_call(
        _in_proj_swa_kernel,
        grid=(B, S // tm),
        in_specs=[
            pl.BlockSpec(memory_space=pltpu.SMEM),
            row(D),
            full((1, D)), full((1, D)),
            full((D, IN_COLS_PADDED)),
            pl.BlockSpec((tm, LANES), lambda b, i: (i, 0)),
            pl.BlockSpec((tm, LANES), lambda b, i: (i, 0)),
            pl.BlockSpec((MLA_ROPE_DIM, tm), lambda b, i: (0, i)),
            pl.BlockSpec((MLA_ROPE_DIM, tm), lambda b, i: (0, i)),
            full((1, MLA_Q_RANK)), full((MLA_HEADS, MLA_QK_PAD, MLA_Q_RANK)),
            full((1, MLA_KV_RANK)), full((MLA_KV_RANK, MLA_HEADS * MLA_NOPE_DIM)),
            full((MLA_HEADS, MLA_V_DIM, MLA_KV_RANK)),
            full((1, BLOCK, kvw)), full((1, BLOCK, kvw)),
        ],
        out_specs=(row(SWA_Q_COLS),
                   pl.BlockSpec((1, MLA_HEADS, MLA_QK_PAD, tm), lambda b, i: (b, 0, 0, i)),
                   pl.BlockSpec((1, MLA_HEADS, tm, MLA_QK_PAD), lambda b, i: (b, 0, i, 0)),
                   pl.BlockSpec((1, MLA_HEADS, 1, MLA_V_DIM, tm), lambda b, i: (b, 0, i, 0, 0))),
        out_shape=out_shape,
        scratch_shapes=[pltpu.VMEM((1, tm, SWA_Q_COLS), BF16), pltpu.VMEM((1, tm, kvw), BF16),
                        pltpu.VMEM((1, tm, kvw), BF16), pltpu.VMEM((BLOCK, kvw), BF16),
                        pltpu.VMEM((BLOCK, kvw), BF16)],
        compiler_params=_cparams(("parallel", "arbitrary")),
        name="in_proj_swa",
    )(p["sinks"], x, p["ln_in_g"], p["ln_in_b"], p["w_in"], cos, sin, cos_t, sin_t, p["gq"],
      p["w_uq_t"], p["gkv"], p["w_uk"], p["w_uv_t"], ka_meta, va_meta)


def _mla_kernel(qt_ref, k_ref, vt_ref, kmeta_ref, vtmeta_ref, o_ref,
                s0a_ref, s0b_ref, s1a_ref, s1b_ref, smax_ref, m_ref, l_ref, acc_ref, *, tq):
    u = pl.program_id(2)
    qts = (qt_ref[0, 0, :, :tq], qt_ref[0, 0, :, tq:])
    bufs = ((s0a_ref, s0b_ref), (s1a_ref, s1b_ref))

    for g in range(2):
        s = _dot(kmeta_ref[0, 0], qts[g])
        m = jnp.max(s, axis=0, keepdims=True)
        p = jnp.exp2(s - m)
        m_ref[g] = m
        l_ref[g] = jnp.sum(p, axis=0, keepdims=True)
        acc_ref[g] = _dot(vtmeta_ref[0, 0, 0], p.astype(BF16))

    def scores_into(g, ab, j):
        start = pl.multiple_of(j * tq, tq)
        s = _dot(k_ref[0, 0, pl.ds(start, tq), :], qts[g])
        bufs[g][ab][...] = s
        smax_ref[g, ab] = jnp.max(s, axis=0, keepdims=True)

    def absorb(g, ab, j, causal=False):
        s = bufs[g][ab][...]
        if causal:
            key = lax.broadcasted_iota(jnp.int32, (tq, tq), 0)
            query = lax.broadcasted_iota(jnp.int32, (tq, tq), 1)
            s = jnp.where(key <= query, s, NEG)
            smax = jnp.max(s, axis=0, keepdims=True)
        else:
            smax = smax_ref[g, ab]
        m = m_ref[g]
        m_new = jnp.maximum(m, smax)
        alpha = jnp.exp2(m - m_new)
        p = jnp.exp2(s - m_new)
        m_ref[g] = m_new
        l_ref[g] = alpha * l_ref[g] + jnp.sum(p, axis=0, keepdims=True)
        acc_ref[g] = alpha * acc_ref[g] + _dot(vt_ref[0, 0, j], p.astype(BF16))

    buf_a, buf_b = 0, 1
    for g in range(2):
        scores_into(g, buf_a, 0)

    def pair(t, carry):
        for g in range(2):
            scores_into(g, buf_b, 2 * t + 1)
            absorb(g, buf_a, 2 * t)
        for g in range(2):
            scores_into(g, buf_a, 2 * t + 2)
            absorb(g, buf_b, 2 * t + 1)
        return carry

    lax.fori_loop(0, u, pair, 0)

    scores_into(1, buf_b, 2 * u + 1)
    absorb(1, buf_a, 2 * u)
    absorb(0, buf_a, 2 * u, causal=True)
    absorb(1, buf_b, 2 * u + 1, causal=True)

    for g in range(2):
        o_ref[0, g * tq:(g + 1) * tq, :] = (acc_ref[g] * (1.0 / l_ref[g])).T.astype(BF16)


def _mla(qt, km, vt, km_meta, vt_meta, tq):
    B, H, _, S = qt.shape
    assert S % (2 * tq) == 0, "a step handles an (even, odd) pair of query tiles"
    return pl.pallas_call(
        functools.partial(_mla_kernel, tq=tq),
        grid=(B, H, S // (2 * tq)),
        in_specs=[
            pl.BlockSpec((1, 1, MLA_QK_PAD, 2 * tq), lambda b, h, u: (b, h, 0, u)),
            pl.BlockSpec((1, 1, S, MLA_QK_PAD), lambda b, h, u: (b, h, 0, 0)),
            pl.BlockSpec((1, 1, S // tq, MLA_V_DIM, tq), lambda b, h, u: (b, h, 0, 0, 0)),
            pl.BlockSpec((1, 1, N_META, MLA_QK_PAD), lambda b, h, u: (0, h, 0, 0)),
            pl.BlockSpec((1, 1, 1, MLA_V_DIM, N_META), lambda b, h, u: (0, h, 0, 0, 0)),
        ],
        out_specs=pl.BlockSpec((1, 2 * tq, MLA_V_DIM), lambda b, h, u: (b, u, h)),
        out_shape=jax.ShapeDtypeStruct((B, S, MLA_OUT_COLS), BF16),
        scratch_shapes=[pltpu.VMEM((tq, tq), F32), pltpu.VMEM((tq, tq), F32),
                        pltpu.VMEM((tq, tq), F32), pltpu.VMEM((tq, tq), F32),
                        pltpu.VMEM((2, 2, 1, tq), F32),
                        pltpu.VMEM((2, 1, tq), F32), pltpu.VMEM((2, 1, tq), F32),
                        pltpu.VMEM((2, MLA_V_DIM, tq), F32)],
        compiler_params=_cparams(("parallel", "parallel", "arbitrary")),
        name="mla_attn",
    )(qt, km, vt, km_meta, vt_meta)


def _out_proj_kernel(x_ref, a_ref, b_ref, gin_ref, bin_ref, ga_ref, gb_ref, wo_ref, g1_ref, b1_ref,
                     wrt_ref, brt_ref, h1_ref, route_ref, counts_out_ref, count_ref):
    h = _layer_norm(x_ref[0], gin_ref[...], bin_ref[...])
    a = a_ref[0].astype(F32)
    b = b_ref[0].astype(F32)
    an = (a * _rms_scale(a) * ga_ref[...]).astype(BF16)
    bn = (b * _rms_scale(b) * gb_ref[...]).astype(BF16)
    mix = _dot(jnp.concatenate([an, bn], axis=-1), wo_ref[...])
    h1 = _layer_norm(ALPHA * h + mix, g1_ref[...], b1_ref[...])
    h1_ref[0] = h1

    x_hi = h1.astype(BF16)
    x_lo = (h1 - x_hi.astype(F32)).astype(BF16)
    hi_both = _dot(x_hi, wrt_ref[...])
    logits = (hi_both[:, :ROUTE_LANES] + hi_both[:, ROUTE_LANES:]
              + _dot(x_lo, wrt_ref[:, :ROUTE_LANES]) + brt_ref[...])
    lane = lax.broadcasted_iota(jnp.int32, logits.shape, 1).astype(F32)
    ninf = -jnp.inf
    no_lane = float(ROUTE_LANES)
    gl = jnp.where(lane < N_GROUPS, logits, ninf)
    gmax = jnp.max(gl, axis=-1, keepdims=True)
    g_idx = jnp.min(jnp.where(gl == gmax, lane, no_lane), axis=-1, keepdims=True)
    lo = EXPERT_LANE0 + g_idx * EXPERTS_PER_GROUP
    el = jnp.where((lane >= lo) & (lane < lo + EXPERTS_PER_GROUP), logits, ninf)
    m1 = jnp.max(el, axis=-1, keepdims=True)
    i1 = jnp.min(jnp.where(el == m1, lane, no_lane), axis=-1, keepdims=True)
    el2 = jnp.where(lane == i1, ninf, el)
    m2 = jnp.max(el2, axis=-1, keepdims=True)
    i2 = jnp.min(jnp.where(el2 == m2, lane, no_lane), axis=-1, keepdims=True)

    a = jnp.minimum(i1, i2) - lo
    b = jnp.maximum(i1, i2) - lo
    cls = (g_idx * PAIRS_PER_GROUP + a * (2.0 * EXPERTS_PER_GROUP - 1.0 - a) * 0.5
           + (b - a - 1.0))

    @pl.when((pl.program_id(0) == 0) & (pl.program_id(1) == 0))
    def _():
        count_ref[...] = jnp.zeros_like(count_ref)

    tm = logits.shape[0]
    onehot = jnp.where(lane == cls, 1.0, 0.0)
    r_i = lax.broadcasted_iota(jnp.int32, (BLOCK, BLOCK), 0)
    c_i = lax.broadcasted_iota(jnp.int32, (BLOCK, BLOCK), 1)
    earlier = jnp.where(r_i > c_i, 1.0, 0.0).astype(BF16)
    running = count_ref[...]
    before = []
    for blk in range(tm // BLOCK):
        oh = onehot[blk * BLOCK:(blk + 1) * BLOCK]
        before.append(_dot(earlier, oh.astype(BF16)) + running)
        running = running + jnp.sum(oh, axis=0, keepdims=True)
    before = jnp.concatenate(before, axis=0)
    rank = jnp.sum(jnp.where(lane == cls, before, 0.0), axis=-1, keepdims=True)
    count_ref[...] = running
    counts_out_ref[...] = running

    route = jnp.where(lane == ROUTE_CLS, cls, 0.0)
    route = jnp.where(lane == ROUTE_RANK, rank, route)
    route_ref[0, 0] = route.T[:ROUTE_ROWS].astype(jnp.int32)


def _out_proj(x, a_out, b_out, p, tm):
    B, S, D = x.shape
    full = lambda shape: pl.BlockSpec(shape, lambda b, i: (0,) * len(shape))
    row = lambda w: pl.BlockSpec((1, tm, w), lambda b, i: (b, i, 0))
    return pl.pallas_call(
        _out_proj_kernel,
        grid=(B, S // tm),
        in_specs=[row(D), row(SWA_Q_COLS), row(MLA_OUT_COLS),
                  full((1, D)), full((1, D)), full((1, SWA_Q_COLS)), full((1, MLA_OUT_COLS)),
                  full((D, D)), full((1, D)), full((1, D)),
                  full((D, 2 * ROUTE_LANES)), full((1, ROUTE_LANES))],
        out_specs=(row(D), pl.BlockSpec((1, 1, ROUTE_ROWS, tm), lambda b, i: (b, i, 0, 0)),
                   full((1, ROUTE_LANES))),
        out_shape=(jax.ShapeDtypeStruct((B, S, D), F32),
                   jax.ShapeDtypeStruct((B, S // tm, ROUTE_ROWS, tm), jnp.int32),
                   jax.ShapeDtypeStruct((1, ROUTE_LANES), F32)),
        scratch_shapes=[pltpu.VMEM((1, ROUTE_LANES), F32)],
        compiler_params=_cparams(("arbitrary", "arbitrary")),
        name="out_proj_route",
    )(x, a_out, b_out, p["ln_in_g"], p["ln_in_b"], p["ga"], p["gb"], p["w_o"], p["ln1_g"],
      p["ln1_b"], p["w_route"], p["b_route"])


SLAB = D_MODEL // LANES


def _slab_copy(src_ref, src_tok8, dst_ref, dst_tok8, sem):
    return pltpu.make_async_copy(src_ref.at[pl.ds(pl.multiple_of(src_tok8, SLAB), SLAB)],
                                 dst_ref.at[pl.ds(pl.multiple_of(dst_tok8, SLAB), SLAB)], sem)


def _to_slabs(x, slab_ref, n):
    for s in range(SLAB):
        slab_ref[pl.ds(s, n, stride=SLAB), :] = x[:, s * LANES:(s + 1) * LANES]


def _from_slabs(slab_ref, n):
    return jnp.concatenate([slab_ref[pl.ds(s, n, stride=SLAB), :] for s in range(SLAB)], axis=1)


def _dispatch_kernel(padstart_ref, nu_ref, pos_ref, h1_ref, xs_ref, rows_ref, zero_ref, sem, *,
                     tm, n_tiles):
    tail = zero_ref.shape[0]

    def zero_fill(start):
        return pltpu.make_async_copy(
            zero_ref, xs_ref.at[pl.ds(pl.multiple_of(start, SLAB), tail)], sem)

    @pl.when(pl.program_id(0) == 0)
    def _():
        zero_ref[...] = jnp.zeros_like(zero_ref)
        for parity in range(2):
            tails = [zero_fill(padstart_ref[c]) for c in range(parity, N_CLASSES, 2)]
            for t in tails:
                t.start()
            for t in tails:
                t.wait()

        def fill_unused(j, carry):
            fill = zero_fill(j * tail)
            fill.start()
            fill.wait()
            return carry

        lax.fori_loop(nu_ref[0], n_tiles + 1, fill_unused, 0)

    _to_slabs(h1_ref[...], rows_ref, tm)

    def issue(r0, carry):
        for k in range(DMA_BURST):
            r = r0 * DMA_BURST + k
            _slab_copy(rows_ref, r * SLAB, xs_ref, pos_ref[0, 0, r], sem).start(priority=k % 2)
        return carry

    lax.fori_loop(0, tm // DMA_BURST, issue, 0)
    pltpu.make_async_copy(rows_ref, xs_ref.at[pl.ds(0, tm * SLAB)], sem).wait()


def _dispatch(h1, pos8, pad_start8, n_used, n_tiles, tm):
    T, D = h1.shape
    pos8 = pos8.reshape(T // tm, 1, tm)
    n_rows = (n_tiles + 1) * MOE_TILE_ROWS
    grid_spec = pltpu.PrefetchScalarGridSpec(
        num_scalar_prefetch=2,
        grid=(T // tm,),
        in_specs=[pl.BlockSpec((1, 1, tm), lambda t, ps, nu: (t, 0, 0),
                               memory_space=pltpu.SMEM),
                  pl.BlockSpec((tm, D), lambda t, ps, nu: (t, 0))],
        out_specs=pl.BlockSpec(memory_space=pl.ANY),
        scratch_shapes=[pltpu.VMEM((tm * SLAB, LANES), F32),
                        pltpu.VMEM((MOE_TILE_ROWS * SLAB, LANES), F32),
                        pltpu.SemaphoreType.DMA],
    )
    return pl.pallas_call(
        functools.partial(_dispatch_kernel, tm=tm, n_tiles=n_tiles),
        grid_spec=grid_spec,
        out_shape=jax.ShapeDtypeStruct((n_rows * SLAB, LANES), F32),
        compiler_params=_cparams(("arbitrary",)),
        name="moe_dispatch",
    )(pad_start8, n_used, pos8, h1)


def _experts_kernel(tlo_ref, thi_ref, nu_ref, x_ref, wg_lo_ref, wg_hi_ref, wu_lo_ref, wu_hi_ref,
                    wd_lo_ref, wd_hi_ref, wrt_ref, brt_ref, y_ref):
    j = pl.program_id(0)
    tr = MOE_TILE_ROWS

    @pl.when(j < nu_ref[0])
    def _():
        x = _from_slabs(x_ref, tr).astype(BF16)
        lo_e = tlo_ref[j]
        hi_e = thi_ref[j]
        logits = _dot(x, wrt_ref[...]) + brt_ref[...]
        lane = lax.broadcasted_iota(jnp.int32, logits.shape, 1)
        gl = jnp.where(lane < N_GROUPS, logits, -jnp.inf)
        ge = jnp.exp(gl - jnp.max(gl, axis=-1, keepdims=True))
        pick = lambda v, at: jnp.sum(jnp.where(lane == at, v, 0.0), axis=-1, keepdims=True)
        g_prob = pick(ge, lo_e // EXPERTS_PER_GROUP) / jnp.sum(ge, axis=-1, keepdims=True)
        e_lo = pick(logits, EXPERT_LANE0 + lo_e)
        e_hi = pick(logits, EXPERT_LANE0 + hi_e)
        gates = (g_prob / (1.0 + jnp.exp(e_hi - e_lo)), g_prob / (1.0 + jnp.exp(e_lo - e_hi)))

        y = None
        for gate, wg_ref, wu_ref, wd_ref in zip(gates, (wg_lo_ref, wg_hi_ref), (wu_lo_ref, wu_hi_ref),
                                                (wd_lo_ref, wd_hi_ref)):
            g = _dot(x, wg_ref[0])
            u = _dot(x, wu_ref[0])
            hid = g * (1.0 / (1.0 + jnp.exp(-g))) * u * gate
            y_k = _dot(hid.astype(BF16), wd_ref[0])
            y = y_k if y is None else y + y_k
        _to_slabs(y, y_ref, tr)

    @pl.when(j >= nu_ref[0])
    def _():
        y_ref[...] = jnp.zeros_like(y_ref)


def _experts(xs, tile_lo, tile_hi, n_used, p, n_tiles):
    D = D_MODEL
    tr = MOE_TILE_ROWS
    last_used = lambda j, nu: jnp.minimum(j, nu[0] - 1)
    wgu = lambda pick: pl.BlockSpec((1, D, D_EXPERT),
                                    lambda j, tlo, thi, nu: (pick(tlo, thi)[j], 0, 0))
    wd = lambda pick: pl.BlockSpec((1, D_EXPERT, D),
                                   lambda j, tlo, thi, nu: (pick(tlo, thi)[j], 0, 0))
    lo = lambda tlo, thi: tlo
    hi = lambda tlo, thi: thi
    grid_spec = pltpu.PrefetchScalarGridSpec(
        num_scalar_prefetch=3,
        grid=(n_tiles,),
        in_specs=[pl.BlockSpec((tr * SLAB, LANES), lambda j, tlo, thi, nu: (last_used(j, nu), 0)),
                  wgu(lo), wgu(hi), wgu(lo), wgu(hi), wd(lo), wd(hi),
                  pl.BlockSpec((D, ROUTE_LANES), lambda j, tlo, thi, nu: (0, 0)),
                  pl.BlockSpec((1, ROUTE_LANES), lambda j, tlo, thi, nu: (0, 0))],
        out_specs=pl.BlockSpec((tr * SLAB, LANES), lambda j, tlo, thi, nu: (j, 0)),
    )
    return pl.pallas_call(
        _experts_kernel,
        grid_spec=grid_spec,
        out_shape=jax.ShapeDtypeStruct((n_tiles * tr * SLAB, LANES), F32),
        compiler_params=_cparams(("arbitrary",)),
        name="moe_experts",
    )(tile_lo, tile_hi, n_used, xs, p["w_gate"], p["w_gate"], p["w_up"], p["w_up"],
      p["w_down"], p["w_down"],
      p["w_route"], p["b_route"])


def _combine_kernel(pos_ref, pos_next_ref, h1_ref, ys_ref, g2_ref, b2_ref, o_ref,
                    ybuf_ref, sem, *, tm):
    t = pl.program_id(0)
    slot = t % 2

    def gather(idx_ref, dst_slot):
        def issue(r0, carry):
            for k in range(DMA_BURST):
                r = r0 * DMA_BURST + k
                _slab_copy(ys_ref, idx_ref[0, 0, r], ybuf_ref.at[dst_slot], r * SLAB,
                           sem.at[dst_slot]).start(priority=k % 2)
            return carry

        lax.fori_loop(0, tm // DMA_BURST, issue, 0)

    @pl.when(t == 0)
    def _():
        gather(pos_ref, 0)

    @pl.when(t + 1 < pl.num_programs(0))
    def _():
        gather(pos_next_ref, 1 - slot)

    pltpu.make_async_copy(ys_ref.at[pl.ds(0, tm * SLAB)], ybuf_ref.at[slot], sem.at[slot]).wait()
    ffn = _from_slabs(ybuf_ref.at[slot], tm)
    o_ref[...] = _layer_norm(ALPHA * h1_ref[...] + ffn, g2_ref[...], b2_ref[...])


def _combine(h1, pos8, ys, p, tm):
    T, D = h1.shape
    n_steps = T // tm
    pos8 = pos8.reshape(n_steps, 1, tm)
    full = lambda shape: pl.BlockSpec(shape, lambda t: (0,) * len(shape))
    return pl.pallas_call(
        functools.partial(_combine_kernel, tm=tm),
        grid=(n_steps,),
        in_specs=[pl.BlockSpec((1, 1, tm), lambda t: (t, 0, 0), memory_space=pltpu.SMEM),
                  pl.BlockSpec((1, 1, tm), lambda t: (jnp.minimum(t + 1, n_steps - 1), 0, 0),
                               memory_space=pltpu.SMEM),
                  pl.BlockSpec((tm, D), lambda t: (t, 0)),
                  pl.BlockSpec(memory_space=pl.ANY),
                  full((1, D)), full((1, D))],
        out_specs=pl.BlockSpec((tm, D), lambda t: (t, 0)),
        out_shape=jax.ShapeDtypeStruct((T, D), F32),
        scratch_shapes=[pltpu.VMEM((2, tm * SLAB, LANES), F32),
                        pltpu.SemaphoreType.DMA((2,))],
        compiler_params=_cparams(("arbitrary",)),
        name="moe_combine_ln2",
    )(pos8, pos8, h1, ys, p["ln2_g"], p["ln2_b"])


def _dispatch_plan(route, counts):
    T = route.shape[0] * route.shape[1] * route.shape[3]
    tr = MOE_TILE_ROWS
    n_tiles = T // tr + N_CLASSES
    cnt = counts[0, :N_CLASSES].astype(jnp.int32)
    padded = jnp.maximum((cnt + tr - 1) // tr, 1) * tr
    classes = jnp.arange(N_CLASSES, dtype=jnp.int32)
    seg_end = jnp.sum(jnp.where(classes[None, :] <= classes[:, None], padded[None, :], 0), axis=1)
    seg_start = seg_end - padded
    cls = route[:, :, ROUTE_CLS, :].reshape(T)
    rank = route[:, :, ROUTE_RANK, :].reshape(T)
    start_of = jnp.sum(jnp.where(cls[:, None] == classes, seg_start, 0), axis=-1)
    pos8 = (start_of + rank) * SLAB
    tile_row0 = jnp.arange(n_tiles, dtype=jnp.int32) * tr
    tile_cls = jnp.sum((seg_end[None, :] <= tile_row0[:, None]).astype(jnp.int32), axis=-1)
    tile_cls = jnp.minimum(tile_cls, N_CLASSES - 1)
    pairs = [(a, b) for a in range(EXPERTS_PER_GROUP) for b in range(a + 1, EXPERTS_PER_GROUP)]
    lo_of = jnp.asarray([g * EXPERTS_PER_GROUP + a for g in range(N_GROUPS) for a, _ in pairs],
                        jnp.int32)
    hi_of = jnp.asarray([g * EXPERTS_PER_GROUP + b for g in range(N_GROUPS) for _, b in pairs],
                        jnp.int32)
    is_cls = tile_cls[:, None] == classes
    tile_lo = jnp.sum(jnp.where(is_cls, lo_of, 0), axis=-1)
    tile_hi = jnp.sum(jnp.where(is_cls, hi_of, 0), axis=-1)
    n_used = (seg_end[-1:] // tr).astype(jnp.int32)
    return pos8, (seg_start + cnt) * SLAB, tile_lo, tile_hi, n_used, n_tiles


def _rope_tables(pos):
    d = SWA_HEAD_DIM
    inv_freq = ROPE_THETA ** (-jnp.arange(0, d, 2, dtype=F32) / d)
    ang = pos[:, None] * inv_freq[None, :]
    c, s = jnp.cos(ang), jnp.sin(ang)
    return (jnp.concatenate([c, c, c, c], axis=-1), jnp.concatenate([-s, s, -s, s], axis=-1),
            jnp.concatenate([c, c], axis=-1).T, jnp.concatenate([s, s], axis=-1).T)


def _prep_params(ln_in_g, ln_in_b, w_in, swa_sinks, mla_q_norm_g, mla_w_uq, mla_kv_norm_g, mla_w_ukv,
                 swa_out_norm_g, mla_out_norm_g, w_o, ln1_g, ln1_b, moe_w_group, moe_b_group,
                 moe_w_router, moe_b_router, moe_w_gate, moe_w_up, moe_w_down, ln2_g, ln2_b):
    order = jnp.asarray(SWA_HEAD_ORDER)
    perm = (order[:, None] * SWA_HEAD_DIM + jnp.arange(SWA_HEAD_DIM)[None, :]).reshape(-1)
    w = w_in[0]
    w_q = w[:, :SWA_Q_COLS][:, perm]
    w_pad = jnp.zeros((D_MODEL, IN_COLS_PADDED - w.shape[1]), w.dtype)
    w_in_p = jnp.concatenate([w_q, w[:, SWA_Q_COLS:], w_pad], axis=1).astype(BF16)

    dqk = MLA_NOPE_DIM + MLA_ROPE_DIM
    wuq = mla_w_uq[0].reshape(MLA_Q_RANK, MLA_HEADS, dqk)
    wuq = jnp.pad(wuq, ((0, 0), (0, 0), (0, MLA_QK_PAD - dqk)))
    wuq_t = jnp.transpose(wuq, (1, 2, 0)).astype(BF16)
    wukv = mla_w_ukv[0].reshape(MLA_KV_RANK, MLA_HEADS, MLA_NOPE_DIM + MLA_V_DIM)
    wuk = wukv[:, :, :MLA_NOPE_DIM].reshape(MLA_KV_RANK, MLA_HEADS * MLA_NOPE_DIM).astype(BF16)
    wuv_t = jnp.transpose(wukv[:, :, MLA_NOPE_DIM:], (1, 2, 0)).astype(BF16)

    w_o0 = w_o[0]
    w_o_p = jnp.concatenate([w_o0[:SWA_Q_COLS][perm], w_o0[SWA_Q_COLS:]], axis=0).astype(BF16)

    w_route = jnp.zeros((D_MODEL, ROUTE_LANES), F32)
    w_route = w_route.at[:, :N_GROUPS].set(moe_w_group[0])
    w_route = w_route.at[:, EXPERT_LANE0:EXPERT_LANE0 + N_EXPERTS].set(moe_w_router[0])
    w_route_hi = w_route.astype(BF16)
    w_route_lo = (w_route - w_route_hi.astype(F32)).astype(BF16)
    w_route = jnp.concatenate([w_route_hi, w_route_lo], axis=1)
    b_route = jnp.zeros((1, ROUTE_LANES), F32)
    b_route = b_route.at[0, :N_GROUPS].set(moe_b_group[0])
    b_route = b_route.at[0, EXPERT_LANE0:EXPERT_LANE0 + N_EXPERTS].set(moe_b_router[0])

    return {
        "ln_in_g": ln_in_g[None, :], "ln_in_b": ln_in_b[None, :],
        "w_in": w_in_p,
        "sinks": swa_sinks[0],
        "gq": mla_q_norm_g, "w_uq_t": wuq_t,
        "gkv": mla_kv_norm_g, "w_uk": wuk, "w_uv_t": wuv_t,
        "ga": swa_out_norm_g[0][perm][None, :], "gb": mla_out_norm_g,
        "w_o": w_o_p, "ln1_g": ln1_g, "ln1_b": ln1_b,
        "w_route": w_route, "b_route": b_route,
        "w_gate": moe_w_gate[0].astype(BF16), "w_up": moe_w_up[0].astype(BF16),
        "w_down": moe_w_down[0].astype(BF16),
        "ln2_g": ln2_g, "ln2_b": ln2_b,
    }


def kernel(x, meta_tokens, ln_in_g, ln_in_b, w_in, swa_sinks, mla_q_norm_g, mla_w_uq, mla_kv_norm_g, mla_w_ukv, swa_out_norm_g, mla_out_norm_g, w_o, ln1_g, ln1_b, moe_w_group, moe_b_group, moe_w_router, moe_b_router, moe_w_gate, moe_w_up, moe_w_down, ln2_g, ln2_b):
    B, S, D = x.shape
    p = _prep_params(ln_in_g, ln_in_b, w_in, swa_sinks, mla_q_norm_g, mla_w_uq, mla_kv_norm_g,
                     mla_w_ukv, swa_out_norm_g, mla_out_norm_g, w_o, ln1_g, ln1_b, moe_w_group,
                     moe_b_group, moe_w_router, moe_b_router, moe_w_gate, moe_w_up, moe_w_down,
                     ln2_g, ln2_b)
    T = B * S
    tm = min(PROJ_TILE, S)

    x_meta = jnp.concatenate([jnp.zeros((N_PAD, D), x.dtype), meta_tokens.astype(x.dtype)])[None]
    pos_meta = jnp.maximum(jnp.arange(BLOCK) - N_PAD, 0).astype(F32)
    pos_tok = (jnp.arange(S) + N_META).astype(F32)
    _, ka_m, va_m, _, km_m, vt_m = _in_proj(x_meta, _rope_tables(pos_meta), p, BLOCK)
    a_out, qt, km, vt = _in_proj_swa(x, _rope_tables(pos_tok), p, ka_m, va_m, tm)
    b_out = _mla(qt, km, vt, km_m[:, :, N_PAD:, :], vt_m[..., N_PAD:], tm)
    h1, route, counts = _out_proj(x, a_out, b_out, p, min(WIDE_TILE, S))
    h1 = h1.reshape(B * S, D)

    pos8, pad_start8, tile_lo, tile_hi, n_used, n_tiles = _dispatch_plan(route, counts)
    xs = _dispatch(h1, pos8, pad_start8, n_used, n_tiles, min(DISPATCH_TILE, T))
    ys = _experts(xs, tile_lo, tile_hi, n_used, p, n_tiles)
    out = _combine(h1, pos8, ys, p, min(COMBINE_TILE, T))
    return out.reshape(B, S, D)
```

```python
import functools

import jax
import jax.numpy as jnp
from jax import lax
from jax.experimental import pallas as pl
from jax.experimental.pallas import tpu as pltpu

D_MODEL = 1024
N_META = 16
BLOCK = 128
N_PAD = BLOCK - N_META
ROPE_THETA = 10000.0
SWA_HEADS = 8
SWA_KV_HEADS = 2
SWA_HEAD_DIM = 64
MLA_HEADS = 4
MLA_Q_RANK = 256
MLA_KV_RANK = 256
MLA_NOPE_DIM = 128
MLA_ROPE_DIM = 64
MLA_V_DIM = 128
SWA_Q_COLS = SWA_HEADS * SWA_HEAD_DIM
SWA_KV_COLS = SWA_KV_HEADS * SWA_HEAD_DIM
MLA_OUT_COLS = MLA_HEADS * MLA_V_DIM
N_GROUPS = 4
EXPERTS_PER_GROUP = 8
N_EXPERTS = N_GROUPS * EXPERTS_PER_GROUP
D_EXPERT = 256
LN_EPS = 1e-5
RMS_EPS = 1e-6
DEPTH = 1
ALPHA = (2.0 * DEPTH) ** 0.25
NEG = -1e30
LOG2_E = 1.4426950408889634

LANES = 128
IN_COLS_PADDED = 1408
MLA_QK_PAD = 256
ROUTE_LANES = 128
EXPERT_LANE0 = 32
ROUTE_CLS, ROUTE_RANK = range(2)
ROUTE_ROWS = 8
PAIRS_PER_GROUP = EXPERTS_PER_GROUP * (EXPERTS_PER_GROUP - 1) // 2
N_CLASSES = N_GROUPS * PAIRS_PER_GROUP
MOE_TILE_ROWS = 256
PROJ_TILE = 1024
WIDE_TILE = 1024
DISPATCH_TILE = 2048
COMBINE_TILE = 1024
DMA_BURST = 8
VMEM_LIMIT = 56 * 1024 * 1024

SWA_HEAD_ORDER = (0, 4, 1, 5, 2, 6, 3, 7)

BF16 = jnp.bfloat16
F32 = jnp.float32


def _cparams(sem):
    return pltpu.CompilerParams(dimension_semantics=sem, vmem_limit_bytes=VMEM_LIMIT)


def _layer_norm(x, g, b):
    mu = jnp.mean(x, axis=-1, keepdims=True)
    xc = x - mu
    var = jnp.mean(xc * xc, axis=-1, keepdims=True)
    return xc * lax.rsqrt(var + LN_EPS) * g + b


def _rms_scale(x):
    return lax.rsqrt(jnp.mean(x * x, axis=-1, keepdims=True) + RMS_EPS)


def _rope128(x, cos, sin_signed, lane_lo):
    up = pltpu.roll(x, LANES - 32, 1)
    dn = pltpu.roll(x, 32, 1)
    return x * cos + jnp.where(lane_lo, up, dn) * sin_signed


def _dot(a, b):
    return jnp.dot(a, b, preferred_element_type=F32)


def _dot_nt(a, b):
    return lax.dot_general(a, b, (((1,), (1,)), ((), ())), preferred_element_type=F32)


def _in_proj_kernel(x_ref, g_ref, b_ref, w_ref, cos_ref, sin_ref, cost_ref, sint_ref, gq_ref,
                    wuqt_ref, gkv_ref, wuk_ref, wuvt_ref,
                    qa_ref, ka_ref, va_ref, qt_ref, km_ref, vt_ref):
    h = _layer_norm(x_ref[0], g_ref[...], b_ref[...])
    u = _dot(h.astype(BF16), w_ref[...])
    cos = cos_ref[...]
    sin = sin_ref[...]
    lane = lax.broadcasted_iota(jnp.int32, (1, LANES), 1)
    lane_lo = (lane % 64) < 32
    rope = functools.partial(_rope128, cos=cos, sin_signed=sin, lane_lo=lane_lo)

    swa_scale = SWA_HEAD_DIM ** -0.5
    for c in range(SWA_Q_COLS // LANES):
        qc = rope(u[:, c * LANES:(c + 1) * LANES]) * swa_scale
        qa_ref[0, :, c * LANES:(c + 1) * LANES] = qc.astype(BF16)
    ka_ref[0] = rope(u[:, 512:640]).astype(BF16)
    va_ref[0] = u[:, 640:768].astype(BF16)

    cq = u[:, 768:1024]
    cqn = (cq * _rms_scale(cq) * gq_ref[...]).astype(BF16)
    ckv = u[:, 1024:1280]
    ckvn = (ckv * _rms_scale(ckv) * gkv_ref[...]).astype(BF16)
    k_nope = _dot(ckvn, wuk_ref[...])
    kr = rope(u[:, 1280:1408]).astype(BF16)
    q_scale = (MLA_NOPE_DIM + MLA_ROPE_DIM) ** -0.5 * LOG2_E
    cost = cost_ref[...]
    sint = sint_ref[...]
    half = MLA_ROPE_DIM // 2
    for hd in range(MLA_HEADS):
        qt = _dot_nt(wuqt_ref[hd], cqn)
        qr = qt[MLA_NOPE_DIM:MLA_NOPE_DIM + MLA_ROPE_DIM]
        rot = jnp.concatenate([-qr[half:], qr[:half]], axis=0)
        qt_ref[0, hd, 0:MLA_NOPE_DIM, :] = (qt[:MLA_NOPE_DIM] * q_scale).astype(BF16)
        qt_ref[0, hd, MLA_NOPE_DIM:MLA_NOPE_DIM + MLA_ROPE_DIM, :] = (
            (qr * cost + rot * sint) * q_scale).astype(BF16)
        qt_ref[0, hd, MLA_NOPE_DIM + MLA_ROPE_DIM:, :] = (
            qt[MLA_NOPE_DIM + MLA_ROPE_DIM:]).astype(BF16)
        km_ref[0, hd, :, 0:LANES] = k_nope[:, hd * LANES:(hd + 1) * LANES].astype(BF16)
        km_ref[0, hd, :, LANES:2 * LANES] = kr
        vt_ref[0, hd, 0] = _dot_nt(wuvt_ref[hd], ckvn).astype(BF16)


def _in_proj(x, tables, p, tm):
    cos, sin, cos_t, sin_t = tables
    B, S, D = x.shape
    grid = (B, S // tm)
    full = lambda shape: pl.BlockSpec(shape, lambda b, i: (0,) * len(shape))
    out_shape = (
        jax.ShapeDtypeStruct((B, S, SWA_Q_COLS), BF16),
        jax.ShapeDtypeStruct((B, S, SWA_KV_COLS), BF16),
        jax.ShapeDtypeStruct((B, S, SWA_KV_COLS), BF16),
        jax.ShapeDtypeStruct((B, MLA_HEADS, MLA_QK_PAD, S), BF16),
        jax.ShapeDtypeStruct((B, MLA_HEADS, S, MLA_QK_PAD), BF16),
        jax.ShapeDtypeStruct((B, MLA_HEADS, S // tm, MLA_V_DIM, tm), BF16),
    )
    row = lambda w: pl.BlockSpec((1, tm, w), lambda b, i: (b, i, 0))
    head = lambda w: pl.BlockSpec((1, MLA_HEADS, tm, w), lambda b, i: (b, 0, i, 0))
    return pl.pallas_call(
        _in_proj_kernel,
        grid=grid,
        in_specs=[
            row(D),
            full((1, D)), full((1, D)),
            full((D, IN_COLS_PADDED)),
            pl.BlockSpec((tm, LANES), lambda b, i: (i, 0)),
            pl.BlockSpec((tm, LANES), lambda b, i: (i, 0)),
            pl.BlockSpec((MLA_ROPE_DIM, tm), lambda b, i: (0, i)),
            pl.BlockSpec((MLA_ROPE_DIM, tm), lambda b, i: (0, i)),
            full((1, MLA_Q_RANK)), full((MLA_HEADS, MLA_QK_PAD, MLA_Q_RANK)),
            full((1, MLA_KV_RANK)), full((MLA_KV_RANK, MLA_HEADS * MLA_NOPE_DIM)),
            full((MLA_HEADS, MLA_V_DIM, MLA_KV_RANK)),
        ],
        out_specs=(row(SWA_Q_COLS), row(SWA_KV_COLS), row(SWA_KV_COLS),
                   pl.BlockSpec((1, MLA_HEADS, MLA_QK_PAD, tm), lambda b, i: (b, 0, 0, i)),
                   head(MLA_QK_PAD),
                   pl.BlockSpec((1, MLA_HEADS, 1, MLA_V_DIM, tm), lambda b, i: (b, 0, i, 0, 0))),
        out_shape=out_shape,
        compiler_params=_cparams(("parallel", "parallel")),
        name="in_proj",
    )(x, p["ln_in_g"], p["ln_in_b"], p["w_in"], cos, sin, cos_t, sin_t, p["gq"], p["w_uq_t"],
      p["gkv"], p["w_uk"], p["w_uv_t"])


def _swa_blocks(sink_ref, qa_ref, kc_ref, vc_ref, kprev, vprev, first, o_ref, blocks_per_step):
    row = lax.broadcasted_iota(jnp.int32, (2 * BLOCK, BLOCK), 0) % BLOCK
    col = lax.broadcasted_iota(jnp.int32, (2 * BLOCK, BLOCK), 1)
    own = col <= row
    meta_ok = col >= jnp.where(first, N_PAD, 0)
    lane = lax.broadcasted_iota(jnp.int32, (1, LANES), 1)
    lane_kv0 = lane < SWA_HEAD_DIM
    top_rows = lax.broadcasted_iota(jnp.int32, (2 * BLOCK, 1), 0) < BLOCK

    for r in range(blocks_per_step):
        rows = slice(r * BLOCK, (r + 1) * BLOCK)
        if r == 0:
            kp, vp = kprev, vprev
        else:
            prev_rows = slice((r - 1) * BLOCK, r * BLOCK)
            kp, vp = kc_ref[0, prev_rows, :], vc_ref[0, prev_rows, :]
        kc, vc = kc_ref[0, rows, :], vc_ref[0, rows, :]
        for c in range(SWA_Q_COLS // LANES):
            qc = qa_ref[0, rows, c * LANES:(c + 1) * LANES]
            zero = jnp.zeros_like(qc)
            q2 = jnp.concatenate([jnp.where(lane_kv0, qc, zero),
                                  jnp.where(lane_kv0, zero, qc)], axis=0)
            s_prev = _dot_nt(q2, kp)
            if r == 0:
                s_prev = jnp.where(meta_ok, s_prev, NEG)
            s = jnp.where(own, _dot_nt(q2, kc), s_prev)
            sink = jnp.where(top_rows, sink_ref[c], sink_ref[c + 4])
            m = jnp.maximum(jnp.max(s, axis=-1, keepdims=True), sink)
            e = jnp.exp(s - m)
            denom = jnp.sum(e, axis=-1, keepdims=True) + jnp.exp(sink - m)
            pr = e * (1.0 / denom)
            o2 = (_dot(jnp.where(own, pr, 0.0).astype(BF16), vc)
                  + _dot(jnp.where(own, 0.0, pr).astype(BF16), vp))
            o = jnp.where(lane_kv0, o2[:BLOCK], o2[BLOCK:])
            o_ref[0, rows, c * LANES:(c + 1) * LANES] = o.astype(BF16)


def _in_proj_swa_kernel(sink_ref, x_ref, g_ref, b_ref, w_ref, cos_ref, sin_ref, cost_ref, sint_ref,
                        gq_ref, wuqt_ref, gkv_ref, wuk_ref, wuvt_ref, kmeta_ref, vmeta_ref,
                        a_ref, qt_ref, km_ref, vt_ref,
                        qa_s, ka_s, va_s, kprev_s, vprev_s):
    _in_proj_kernel(x_ref, g_ref, b_ref, w_ref, cos_ref, sin_ref, cost_ref, sint_ref, gq_ref,
                    wuqt_ref, gkv_ref, wuk_ref, wuvt_ref, qa_s, ka_s, va_s, qt_ref, km_ref, vt_ref)
    first = pl.program_id(1) == 0
    kprev = jnp.where(first, kmeta_ref[0], kprev_s[...])
    vprev = jnp.where(first, vmeta_ref[0], vprev_s[...])
    tm = qa_s.shape[1]
    _swa_blocks(sink_ref, qa_s, ka_s, va_s, kprev, vprev, first, a_ref, tm // BLOCK)
    kprev_s[...] = ka_s[0, tm - BLOCK:, :]
    vprev_s[...] = va_s[0, tm - BLOCK:, :]


def _in_proj_swa(x, tables, p, ka_meta, va_meta, tm):
    cos, sin, cos_t, sin_t = tables
    B, S, D = x.shape
    kvw = SWA_KV_COLS
    full = lambda shape: pl.BlockSpec(shape, lambda b, i: (0,) * len(shape))
    row = lambda w: pl.BlockSpec((1, tm, w), lambda b, i: (b, i, 0))
    out_shape = (
        jax.ShapeDtypeStruct((B, S, SWA_Q_COLS), BF16),
        jax.ShapeDtypeStruct((B, MLA_HEADS, MLA_QK_PAD, S), BF16),
        jax.ShapeDtypeStruct((B, MLA_HEADS, S, MLA_QK_PAD), BF16),
        jax.ShapeDtypeStruct((B, MLA_HEADS, S // tm, MLA_V_DIM, tm), BF16),
    )
    return pl.pallas_call(
        _in_proj_swa_kernel,
        grid=(B, S // tm),
        in_specs=[
            pl.BlockSpec(memory_space=pltpu.SMEM),
            row(D),
            full((1, D)), full((1, D)),
            full((D, IN_COLS_PADDED)),
            pl.BlockSpec((tm, LANES), lambda b, i: (i, 0)),
            pl.BlockSpec((tm, LANES), lambda b, i: (i, 0)),
            pl.BlockSpec((MLA_ROPE_DIM, tm), lambda b, i: (0, i)),
            pl.BlockSpec((MLA_ROPE_DIM, tm), lambda b, i: (0, i)),
            full((1, MLA_Q_RANK)), full((MLA_HEADS, MLA_QK_PAD, MLA_Q_RANK)),
            full((1, MLA_KV_RANK)), full((MLA_KV_RANK, MLA_HEADS * MLA_NOPE_DIM)),
            full((MLA_HEADS, MLA_V_DIM, MLA_KV_RANK)),
            full((1, BLOCK, kvw)), full((1, BLOCK, kvw)),
        ],
        out_specs=(row(SWA_Q_COLS),
                   pl.BlockSpec((1, MLA_HEADS, MLA_QK_PAD, tm), lambda b, i: (b, 0, 0, i)),
                   pl.BlockSpec((1, MLA_HEADS, tm, MLA_QK_PAD), lambda b, i: (b, 0, i, 0)),
                   pl.BlockSpec((1, MLA_HEADS, 1, MLA_V_DIM, tm), lambda b, i: (b, 0, i, 0, 0))),
        out_shape=out_shape,
        scratch_shapes=[pltpu.VMEM((1, tm, SWA_Q_COLS), BF16), pltpu.VMEM((1, tm, kvw), BF16),
                        pltpu.VMEM((1, tm, kvw), BF16), pltpu.VMEM((BLOCK, kvw), BF16),
                        pltpu.VMEM((BLOCK, kvw), BF16)],
        compiler_params=_cparams(("parallel", "arbitrary")),
        name="in_proj_swa",
    )(p["sinks"], x, p["ln_in_g"], p["ln_in_b"], p["w_in"], cos, sin, cos_t, sin_t, p["gq"],
      p["w_uq_t"], p["gkv"], p["w_uk"], p["w_uv_t"], ka_meta, va_meta)


def _mla_kernel(qt_ref, k_ref, vt_ref, kmeta_ref, vtmeta_ref, o_ref,
                s0a_ref, s0b_ref, s1a_ref, s1b_ref, smax_ref, m_ref, l_ref, acc_ref, *, tq):
    u = pl.program_id(2)
    qts = (qt_ref[0, 0, :, :tq], qt_ref[0, 0, :, tq:])
    bufs = ((s0a_ref, s0b_ref), (s1a_ref, s1b_ref))

    for g in range(2):
        s = _dot(kmeta_ref[0, 0], qts[g])
        m = jnp.max(s, axis=0, keepdims=True)
        p = jnp.exp2(s - m)
        m_ref[g] = m
        l_ref[g] = jnp.sum(p, axis=0, keepdims=True)
        acc_ref[g] = _dot(vtmeta_ref[0, 0, 0], p.astype(BF16))

    def scores_into(g, ab, j):
        start = pl.multiple_of(j * tq, tq)
        s = _dot(k_ref[0, 0, pl.ds(start, tq), :], qts[g])
        bufs[g][ab][...] = s
        smax_ref[g, ab] = jnp.max(s, axis=0, keepdims=True)

    def absorb(g, ab, j, causal=False):
        s = bufs[g][ab][...]
        if causal:
            key = lax.broadcasted_iota(jnp.int32, (tq, tq), 0)
            query = lax.broadcasted_iota(jnp.int32, (tq, tq), 1)
            s = jnp.where(key <= query, s, NEG)
            smax = jnp.max(s, axis=0, keepdims=True)
        else:
            smax = smax_ref[g, ab]
        m = m_ref[g]
        m_new = jnp.maximum(m, smax)
        alpha = jnp.exp2(m - m_new)
        p = jnp.exp2(s - m_new)
        m_ref[g] = m_new
        l_ref[g] = alpha * l_ref[g] + jnp.sum(p, axis=0, keepdims=True)
        acc_ref[g] = alpha * acc_ref[g] + _dot(vt_ref[0, 0, j], p.astype(BF16))

    buf_a, buf_b = 0, 1
    for g in range(2):
        scores_into(g, buf_a, 0)

    def pair(t, carry):
        for g in range(2):
            scores_into(g, buf_b, 2 * t + 1)
            absorb(g, buf_a, 2 * t)
        for g in range(2):
            scores_into(g, buf_a, 2 * t + 2)
            absorb(g, buf_b, 2 * t + 1)
        return carry

    lax.fori_loop(0, u, pair, 0)

    scores_into(1, buf_b, 2 * u + 1)
    absorb(1, buf_a, 2 * u)
    absorb(0, buf_a, 2 * u, causal=True)
    absorb(1, buf_b, 2 * u + 1, causal=True)

    for g in range(2):
        o_ref[0, g * tq:(g + 1) * tq, :] = (acc_ref[g] * (1.0 / l_ref[g])).T.astype(BF16)


def _mla(qt, km, vt, km_meta, vt_meta, tq):
    B, H, _, S = qt.shape
    assert S % (2 * tq) == 0, "a step handles an (even, odd) pair of query tiles"
    return pl.pallas_call(
        functools.partial(_mla_kernel, tq=tq),
        grid=(B, H, S // (2 * tq)),
        in_specs=[
            pl.BlockSpec((1, 1, MLA_QK_PAD, 2 * tq), lambda b, h, u: (b, h, 0, u)),
            pl.BlockSpec((1, 1, S, MLA_QK_PAD), lambda b, h, u: (b, h, 0, 0)),
            pl.BlockSpec((1, 1, S // tq, MLA_V_DIM, tq), lambda b, h, u: (b, h, 0, 0, 0)),
            pl.BlockSpec((1, 1, N_META, MLA_QK_PAD), lambda b, h, u: (0, h, 0, 0)),
            pl.BlockSpec((1, 1, 1, MLA_V_DIM, N_META), lambda b, h, u: (0, h, 0, 0, 0)),
        ],
        out_specs=pl.BlockSpec((1, 2 * tq, MLA_V_DIM), lambda b, h, u: (b, u, h)),
        out_shape=jax.ShapeDtypeStruct((B, S, MLA_OUT_COLS), BF16),
        scratch_shapes=[pltpu.VMEM((tq, tq), F32), pltpu.VMEM((tq, tq), F32),
                        pltpu.VMEM((tq, tq), F32), pltpu.VMEM((tq, tq), F32),
                        pltpu.VMEM((2, 2, 1, tq), F32),
                        pltpu.VMEM((2, 1, tq), F32), pltpu.VMEM((2, 1, tq), F32),
                        pltpu.VMEM((2, MLA_V_DIM, tq), F32)],
        compiler_params=_cparams(("parallel", "parallel", "arbitrary")),
        name="mla_attn",
    )(qt, km, vt, km_meta, vt_meta)


def _out_proj_kernel(x_ref, a_ref, b_ref, gin_ref, bin_ref, ga_ref, gb_ref, wo_ref, g1_ref, b1_ref,
                     wrt_ref, brt_ref, h1_ref, route_ref, counts_out_ref, count_ref):
    h = _layer_norm(x_ref[0], gin_ref[...], bin_ref[...])
    a = a_ref[0].astype(F32)
    b = b_ref[0].astype(F32)
    an = (a * _rms_scale(a) * ga_ref[...]).astype(BF16)
    bn = (b * _rms_scale(b) * gb_ref[...]).astype(BF16)
    mix = _dot(jnp.concatenate([an, bn], axis=-1), wo_ref[...])
    h1 = _layer_norm(ALPHA * h + mix, g1_ref[...], b1_ref[...])
    h1_ref[0] = h1

    x_hi = h1.astype(BF16)
    x_lo = (h1 - x_hi.astype(F32)).astype(BF16)
    hi_both = _dot(x_hi, wrt_ref[...])
    logits = (hi_both[:, :ROUTE_LANES] + hi_both[:, ROUTE_LANES:]
              + _dot(x_lo, wrt_ref[:, :ROUTE_LANES]) + brt_ref[...])
    lane = lax.broadcasted_iota(jnp.int32, logits.shape, 1).astype(F32)
    ninf = -jnp.inf
    no_lane = float(ROUTE_LANES)
    gl = jnp.where(lane < N_GROUPS, logits, ninf)
    gmax = jnp.max(gl, axis=-1, keepdims=True)
    g_idx = jnp.min(jnp.where(gl == gmax, lane, no_lane), axis=-1, keepdims=True)
    lo = EXPERT_LANE0 + g_idx * EXPERTS_PER_GROUP
    el = jnp.where((lane >= lo) & (lane < lo + EXPERTS_PER_GROUP), logits, ninf)
    m1 = jnp.max(el, axis=-1, keepdims=True)
    i1 = jnp.min(jnp.where(el == m1, lane, no_lane), axis=-1, keepdims=True)
    el2 = jnp.where(lane == i1, ninf, el)
    m2 = jnp.max(el2, axis=-1, keepdims=True)
    i2 = jnp.min(jnp.where(el2 == m2, lane, no_lane), axis=-1, keepdims=True)

    a = jnp.minimum(i1, i2) - lo
    b = jnp.maximum(i1, i2) - lo
    cls = (g_idx * PAIRS_PER_GROUP + a * (2.0 * EXPERTS_PER_GROUP - 1.0 - a) * 0.5
           + (b - a - 1.0))

    @pl.when((pl.program_id(0) == 0) & (pl.program_id(1) == 0))
    def _():
        count_ref[...] = jnp.zeros_like(count_ref)

    tm = logits.shape[0]
    onehot = jnp.where(lane == cls, 1.0, 0.0)
    r_i = lax.broadcasted_iota(jnp.int32, (BLOCK, BLOCK), 0)
    c_i = lax.broadcasted_iota(jnp.int32, (BLOCK, BLOCK), 1)
    earlier = jnp.where(r_i > c_i, 1.0, 0.0).astype(BF16)
    running = count_ref[...]
    before = []
    for blk in range(tm // BLOCK):
        oh = onehot[blk * BLOCK:(blk + 1) * BLOCK]
        before.append(_dot(earlier, oh.astype(BF16)) + running)
        running = running + jnp.sum(oh, axis=0, keepdims=True)
    before = jnp.concatenate(before, axis=0)
    rank = jnp.sum(jnp.where(lane == cls, before, 0.0), axis=-1, keepdims=True)
    count_ref[...] = running
    counts_out_ref[...] = running

    route = jnp.where(lane == ROUTE_CLS, cls, 0.0)
    route = jnp.where(lane == ROUTE_RANK, rank, route)
    route_ref[0, 0] = route.T[:ROUTE_ROWS].astype(jnp.int32)


def _out_proj(x, a_out, b_out, p, tm):
    B, S, D = x.shape
    full = lambda shape: pl.BlockSpec(shape, lambda b, i: (0,) * len(shape))
    row = lambda w: pl.BlockSpec((1, tm, w), lambda b, i: (b, i, 0))
    return pl.pallas_call(
        _out_proj_kernel,
        grid=(B, S // tm),
        in_specs=[row(D), row(SWA_Q_COLS), row(MLA_OUT_COLS),
                  full((1, D)), full((1, D)), full((1, SWA_Q_COLS)), full((1, MLA_OUT_COLS)),
                  full((D, D)), full((1, D)), full((1, D)),
                  full((D, 2 * ROUTE_LANES)), full((1, ROUTE_LANES))],
        out_specs=(row(D), pl.BlockSpec((1, 1, ROUTE_ROWS, tm), lambda b, i: (b, i, 0, 0)),
                   full((1, ROUTE_LANES))),
        out_shape=(jax.ShapeDtypeStruct((B, S, D), F32),
                   jax.ShapeDtypeStruct((B, S // tm, ROUTE_ROWS, tm), jnp.int32),
                   jax.ShapeDtypeStruct((1, ROUTE_LANES), F32)),
        scratch_shapes=[pltpu.VMEM((1, ROUTE_LANES), F32)],
        compiler_params=_cparams(("arbitrary", "arbitrary")),
        name="out_proj_route",
    )(x, a_out, b_out, p["ln_in_g"], p["ln_in_b"], p["ga"], p["gb"], p["w_o"], p["ln1_g"],
      p["ln1_b"], p["w_route"], p["b_route"])


SLAB = D_MODEL // LANES


def _slab_copy(src_ref, src_tok8, dst_ref, dst_tok8, sem):
    return pltpu.make_async_copy(src_ref.at[pl.ds(pl.multiple_of(src_tok8, SLAB), SLAB)],
                                 dst_ref.at[pl.ds(pl.multiple_of(dst_tok8, SLAB), SLAB)], sem)


def _to_slabs(x, slab_ref, n):
    for s in range(SLAB):
        slab_ref[pl.ds(s, n, stride=SLAB), :] = x[:, s * LANES:(s + 1) * LANES]


def _from_slabs(slab_ref, n):
    return jnp.concatenate([slab_ref[pl.ds(s, n, stride=SLAB), :] for s in range(SLAB)], axis=1)


def _dispatch_kernel(padstart_ref, nu_ref, pos_ref, h1_ref, xs_ref, rows_ref, zero_ref, sem, *,
                     tm, n_tiles):
    tail = zero_ref.shape[0]

    def zero_fill(start):
        return pltpu.make_async_copy(
            zero_ref, xs_ref.at[pl.ds(pl.multiple_of(start, SLAB), tail)], sem)

    @pl.when(pl.program_id(0) == 0)
    def _():
        zero_ref[...] = jnp.zeros_like(zero_ref)
        for parity in range(2):
            tails = [zero_fill(padstart_ref[c]) for c in range(parity, N_CLASSES, 2)]
            for t in tails:
                t.start()
            for t in tails:
                t.wait()

        def fill_unused(j, carry):
            fill = zero_fill(j * tail)
            fill.start()
            fill.wait()
            return carry

        lax.fori_loop(nu_ref[0], n_tiles + 1, fill_unused, 0)

    _to_slabs(h1_ref[...], rows_ref, tm)

    def issue(r0, carry):
        for k in range(DMA_BURST):
            r = r0 * DMA_BURST + k
            _slab_copy(rows_ref, r * SLAB, xs_ref, pos_ref[0, 0, r], sem).start(priority=k % 2)
        return carry

    lax.fori_loop(0, tm // DMA_BURST, issue, 0)
    pltpu.make_async_copy(rows_ref, xs_ref.at[pl.ds(0, tm * SLAB)], sem).wait()


def _dispatch(h1, pos8, pad_start8, n_used, n_tiles, tm):
    T, D = h1.shape
    pos8 = pos8.reshape(T // tm, 1, tm)
    n_rows = (n_tiles + 1) * MOE_TILE_ROWS
    grid_spec = pltpu.PrefetchScalarGridSpec(
        num_scalar_prefetch=2,
        grid=(T // tm,),
        in_specs=[pl.BlockSpec((1, 1, tm), lambda t, ps, nu: (t, 0, 0),
                               memory_space=pltpu.SMEM),
                  pl.BlockSpec((tm, D), lambda t, ps, nu: (t, 0))],
        out_specs=pl.BlockSpec(memory_space=pl.ANY),
        scratch_shapes=[pltpu.VMEM((tm * SLAB, LANES), F32),
                        pltpu.VMEM((MOE_TILE_ROWS * SLAB, LANES), F32),
                        pltpu.SemaphoreType.DMA],
    )
    return pl.pallas_call(
        functools.partial(_dispatch_kernel, tm=tm, n_tiles=n_tiles),
        grid_spec=grid_spec,
        out_shape=jax.ShapeDtypeStruct((n_rows * SLAB, LANES), F32),
        compiler_params=_cparams(("arbitrary",)),
        name="moe_dispatch",
    )(pad_start8, n_used, pos8, h1)


def _experts_kernel(tlo_ref, thi_ref, nu_ref, x_ref, wg_lo_ref, wg_hi_ref, wu_lo_ref, wu_hi_ref,
                    wd_lo_ref, wd_hi_ref, wrt_ref, brt_ref, y_ref):
    j = pl.program_id(0)
    tr = MOE_TILE_ROWS

    @pl.when(j < nu_ref[0])
    def _():
        x = _from_slabs(x_ref, tr).astype(BF16)
        lo_e = tlo_ref[j]
        hi_e = thi_ref[j]
        logits = _dot(x, wrt_ref[...]) + brt_ref[...]
        lane = lax.broadcasted_iota(jnp.int32, logits.shape, 1)
        gl = jnp.where(lane < N_GROUPS, logits, -jnp.inf)
        ge = jnp.exp(gl - jnp.max(gl, axis=-1, keepdims=True))
        pick = lambda v, at: jnp.sum(jnp.where(lane == at, v, 0.0), axis=-1, keepdims=True)
        g_prob = pick(ge, lo_e // EXPERTS_PER_GROUP) / jnp.sum(ge, axis=-1, keepdims=True)
        e_lo = pick(logits, EXPERT_LANE0 + lo_e)
        e_hi = pick(logits, EXPERT_LANE0 + hi_e)
        gates = (g_prob / (1.0 + jnp.exp(e_hi - e_lo)), g_prob / (1.0 + jnp.exp(e_lo - e_hi)))

        y = None
        for gate, wg_ref, wu_ref, wd_ref in zip(gates, (wg_lo_ref, wg_hi_ref), (wu_lo_ref, wu_hi_ref),
                                                (wd_lo_ref, wd_hi_ref)):
            g = _dot(x, wg_ref[0].astype(BF16))
            u = _dot(x, wu_ref[0].astype(BF16))
            hid = g * (1.0 / (1.0 + jnp.exp(-g))) * u * gate
            y_k = _dot(hid.astype(BF16), wd_ref[0].astype(BF16))
            y = y_k if y is None else y + y_k
        _to_slabs(y, y_ref, tr)

    @pl.when(j >= nu_ref[0])
    def _():
        y_ref[...] = jnp.zeros_like(y_ref)


def _experts(xs, tile_lo, tile_hi, n_used, p, n_tiles):
    D = D_MODEL
    tr = MOE_TILE_ROWS
    last_used = lambda j, nu: jnp.minimum(j, nu[0] - 1)
    wgu = lambda pick: pl.BlockSpec((1, D, D_EXPERT),
                                    lambda j, tlo, thi, nu: (pick(tlo, thi)[j], 0, 0))
    wd = lambda pick: pl.BlockSpec((1, D_EXPERT, D),
                                   lambda j, tlo, thi, nu: (pick(tlo, thi)[j], 0, 0))
    lo = lambda tlo, thi: tlo
    hi = lambda tlo, thi: thi
    grid_spec = pltpu.PrefetchScalarGridSpec(
        num_scalar_prefetch=3,
        grid=(n_tiles,),
        in_specs=[pl.BlockSpec((tr * SLAB, LANES), lambda j, tlo, thi, nu: (last_used(j, nu), 0)),
                  wgu(lo), wgu(hi), wgu(lo), wgu(hi), wd(lo), wd(hi),
                  pl.BlockSpec((D, ROUTE_LANES), lambda j, tlo, thi, nu: (0, 0)),
                  pl.BlockSpec((1, ROUTE_LANES), lambda j, tlo, thi, nu: (0, 0))],
        out_specs=pl.BlockSpec((tr * SLAB, LANES), lambda j, tlo, thi, nu: (j, 0)),
    )
    return pl.pallas_call(
        _experts_kernel,
        grid_spec=grid_spec,
        out_shape=jax.ShapeDtypeStruct((n_tiles * tr * SLAB, LANES), F32),
        compiler_params=_cparams(("arbitrary",)),
        name="moe_experts",
    )(tile_lo, tile_hi, n_used, xs, p["w_gate"], p["w_gate"], p["w_up"], p["w_up"],
      p["w_down"], p["w_down"],
      p["w_route"], p["b_route"])


def _combine_kernel(pos_ref, pos_next_ref, h1_ref, ys_ref, g2_ref, b2_ref, o_ref,
                    ybuf_ref, sem, *, tm):
    t = pl.program_id(0)
    slot = t % 2

    def gather(idx_ref, dst_slot):
        def issue(r0, carry):
            for k in range(DMA_BURST):
                r = r0 * DMA_BURST + k
                _slab_copy(ys_ref, idx_ref[0, 0, r], ybuf_ref.at[dst_slot], r * SLAB,
                           sem.at[dst_slot]).start(priority=k % 2)
            return carry

        lax.fori_loop(0, tm // DMA_BURST, issue, 0)

    @pl.when(t == 0)
    def _():
        gather(pos_ref, 0)

    @pl.when(t + 1 < pl.num_programs(0))
    def _():
        gather(pos_next_ref, 1 - slot)

    pltpu.make_async_copy(ys_ref.at[pl.ds(0, tm * SLAB)], ybuf_ref.at[slot], sem.at[slot]).wait()
    ffn = _from_slabs(ybuf_ref.at[slot], tm)
    o_ref[...] = _layer_norm(ALPHA * h1_ref[...] + ffn, g2_ref[...], b2_ref[...])


def _combine(h1, pos8, ys, p, tm):
    T, D = h1.shape
    n_steps = T // tm
    pos8 = pos8.reshape(n_steps, 1, tm)
    full = lambda shape: pl.BlockSpec(shape, lambda t: (0,) * len(shape))
    return pl.pallas_call(
        functools.partial(_combine_kernel, tm=tm),
        grid=(n_steps,),
        in_specs=[pl.BlockSpec((1, 1, tm), lambda t: (t, 0, 0), memory_space=pltpu.SMEM),
                  pl.BlockSpec((1, 1, tm), lambda t: (jnp.minimum(t + 1, n_steps - 1), 0, 0),
                               memory_space=pltpu.SMEM),
                  pl.BlockSpec((tm, D), lambda t: (t, 0)),
                  pl.BlockSpec(memory_space=pl.ANY),
                  full((1, D)), full((1, D))],
        out_specs=pl.BlockSpec((tm, D), lambda t: (t, 0)),
        out_shape=jax.ShapeDtypeStruct((T, D), F32),
        scratch_shapes=[pltpu.VMEM((2, tm * SLAB, LANES), F32),
                        pltpu.SemaphoreType.DMA((2,))],
        compiler_params=_cparams(("arbitrary",)),
        name="moe_combine_ln2",
    )(pos8, pos8, h1, ys, p["ln2_g"], p["ln2_b"])


def _dispatch_plan(route, counts):
    T = route.shape[0] * route.shape[1] * route.shape[3]
    tr = MOE_TILE_ROWS
    n_tiles = T // tr + N_CLASSES
    cnt = counts[0, :N_CLASSES].astype(jnp.int32)
    padded = jnp.maximum((cnt + tr - 1) // tr, 1) * tr
    classes = jnp.arange(N_CLASSES, dtype=jnp.int32)
    seg_end = jnp.sum(jnp.where(classes[None, :] <= classes[:, None], padded[None, :], 0), axis=1)
    seg_start = seg_end - padded
    cls = route[:, :, ROUTE_CLS, :].reshape(T)
    rank = route[:, :, ROUTE_RANK, :].reshape(T)
    start_of = jnp.sum(jnp.where(cls[:, None] == classes, seg_start, 0), axis=-1)
    pos8 = (start_of + rank) * SLAB
    tile_row0 = jnp.arange(n_tiles, dtype=jnp.int32) * tr
    tile_cls = jnp.sum((seg_end[None, :] <= tile_row0[:, None]).astype(jnp.int32), axis=-1)
    tile_cls = jnp.minimum(tile_cls, N_CLASSES - 1)
    pairs = [(a, b) for a in range(EXPERTS_PER_GROUP) for b in range(a + 1, EXPERTS_PER_GROUP)]
    lo_of = jnp.asarray([g * EXPERTS_PER_GROUP + a for g in range(N_GROUPS) for a, _ in pairs],
                        jnp.int32)
    hi_of = jnp.asarray([g * EXPERTS_PER_GROUP + b for g in range(N_GROUPS) for _, b in pairs],
                        jnp.int32)
    is_cls = tile_cls[:, None] == classes
    tile_lo = jnp.sum(jnp.where(is_cls, lo_of, 0), axis=-1)
    tile_hi = jnp.sum(jnp.where(is_cls, hi_of, 0), axis=-1)
    n_used = (seg_end[-1:] // tr).astype(jnp.int32)
    return pos8, (seg_start + cnt) * SLAB, tile_lo, tile_hi, n_used, n_tiles


def _rope_tables(pos):
    d = SWA_HEAD_DIM
    inv_freq = ROPE_THETA ** (-jnp.arange(0, d, 2, dtype=F32) / d)
    ang = pos[:, None] * inv_freq[None, :]
    c, s = jnp.cos(ang), jnp.sin(ang)
    return (jnp.concatenate([c, c, c, c], axis=-1), jnp.concatenate([-s, s, -s, s], axis=-1),
            jnp.concatenate([c, c], axis=-1).T, jnp.concatenate([s, s], axis=-1).T)


def _prep_params(ln_in_g, ln_in_b, w_in, swa_sinks, mla_q_norm_g, mla_w_uq, mla_kv_norm_g, mla_w_ukv,
                 swa_out_norm_g, mla_out_norm_g, w_o, ln1_g, ln1_b, moe_w_group, moe_b_group,
                 moe_w_router, moe_b_router, moe_w_gate, moe_w_up, moe_w_down, ln2_g, ln2_b):
    order = jnp.asarray(SWA_HEAD_ORDER)
    perm = (order[:, None] * SWA_HEAD_DIM + jnp.arange(SWA_HEAD_DIM)[None, :]).reshape(-1)
    w = w_in[0]
    w_q = w[:, :SWA_Q_COLS][:, perm]
    w_pad = jnp.zeros((D_MODEL, IN_COLS_PADDED - w.shape[1]), w.dtype)
    w_in_p = jnp.concatenate([w_q, w[:, SWA_Q_COLS:], w_pad], axis=1).astype(BF16)

    dqk = MLA_NOPE_DIM + MLA_ROPE_DIM
    wuq = mla_w_uq[0].reshape(MLA_Q_RANK, MLA_HEADS, dqk)
    wuq = jnp.pad(wuq, ((0, 0), (0, 0), (0, MLA_QK_PAD - dqk)))
    wuq_t = jnp.transpose(wuq, (1, 2, 0)).astype(BF16)
    wukv = mla_w_ukv[0].reshape(MLA_KV_RANK, MLA_HEADS, MLA_NOPE_DIM + MLA_V_DIM)
    wuk = wukv[:, :, :MLA_NOPE_DIM].reshape(MLA_KV_RANK, MLA_HEADS * MLA_NOPE_DIM).astype(BF16)
    wuv_t = jnp.transpose(wukv[:, :, MLA_NOPE_DIM:], (1, 2, 0)).astype(BF16)

    w_o0 = w_o[0]
    w_o_p = jnp.concatenate([w_o0[:SWA_Q_COLS][perm], w_o0[SWA_Q_COLS:]], axis=0).astype(BF16)

    w_route = jnp.zeros((D_MODEL, ROUTE_LANES), F32)
    w_route = w_route.at[:, :N_GROUPS].set(moe_w_group[0])
    w_route = w_route.at[:, EXPERT_LANE0:EXPERT_LANE0 + N_EXPERTS].set(moe_w_router[0])
    w_route_hi = w_route.astype(BF16)
    w_route_lo = (w_route - w_route_hi.astype(F32)).astype(BF16)
    w_route = jnp.concatenate([w_route_hi, w_route_lo], axis=1)
    b_route = jnp.zeros((1, ROUTE_LANES), F32)
    b_route = b_route.at[0, :N_GROUPS].set(moe_b_group[0])
    b_route = b_route.at[0, EXPERT_LANE0:EXPERT_LANE0 + N_EXPERTS].set(moe_b_router[0])

    return {
        "ln_in_g": ln_in_g[None, :], "ln_in_b": ln_in_b[None, :],
        "w_in": w_in_p,
        "sinks": swa_sinks[0],
        "gq": mla_q_norm_g, "w_uq_t": wuq_t,
        "gkv": mla_kv_norm_g, "w_uk": wuk, "w_uv_t": wuv_t,
        "ga": swa_out_norm_g[0][perm][None, :], "gb": mla_out_norm_g,
        "w_o": w_o_p, "ln1_g": ln1_g, "ln1_b": ln1_b,
        "w_route": w_route, "b_route": b_route,
        "w_gate": moe_w_gate[0], "w_up": moe_w_up[0], "w_down": moe_w_down[0],
        "ln2_g": ln2_g, "ln2_b": ln2_b,
    }


def kernel(x, meta_tokens, ln_in_g, ln_in_b, w_in, swa_sinks, mla_q_norm_g, mla_w_uq, mla_kv_norm_g, mla_w_ukv, swa_out_norm_g, mla_out_norm_g, w_o, ln1_g, ln1_b, moe_w_group, moe_b_group, moe_w_router, moe_b_router, moe_w_gate, moe_w_up, moe_w_down, ln2_g, ln2_b):
    B, S, D = x.shape
    p = _prep_params(ln_in_g, ln_in_b, w_in, swa_sinks, mla_q_norm_g, mla_w_uq, mla_kv_norm_g,
                     mla_w_ukv, swa_out_norm_g, mla_out_norm_g, w_o, ln1_g, ln1_b, moe_w_group,
                     moe_b_group, moe_w_router, moe_b_router, moe_w_gate, moe_w_up, moe_w_down,
                     ln2_g, ln2_b)
    T = B * S
    tm = min(PROJ_TILE, S)

    x_meta = jnp.concatenate([jnp.zeros((N_PAD, D), x.dtype), meta_tokens.astype(x.dtype)])[None]
    pos_meta = jnp.maximum(jnp.arange(BLOCK) - N_PAD, 0).astype(F32)
    pos_tok = (jnp.arange(S) + N_META).astype(F32)
    _, ka_m, va_m, _, km_m, vt_m = _in_proj(x_meta, _rope_tables(pos_meta), p, BLOCK)
    a_out, qt, km, vt = _in_proj_swa(x, _rope_tables(pos_tok), p, ka_m, va_m, tm)
    b_out = _mla(qt, km, vt, km_m[:, :, N_PAD:, :], vt_m[..., N_PAD:], tm)
    h1, route, counts = _out_proj(x, a_out, b_out, p, min(WIDE_TILE, S))
    h1 = h1.reshape(B * S, D)

    pos8, pad_start8, tile_lo, tile_hi, n_used, n_tiles = _dispatch_plan(route, counts)
    xs = _dispatch(h1, pos8, pad_start8, n_used, n_tiles, min(DISPATCH_TILE, T))
    ys = _experts(xs, tile_lo, tile_hi, n_used, p, n_tiles)
    out = _combine(h1, pos8, ys, p, min(COMBINE_TILE, T))
    return out.reshape(B, S, D)
```

```python
import functools

import jax
import jax.numpy as jnp
from jax import lax
from jax.experimental import pallas as pl
from jax.experimental.pallas import tpu as pltpu

D_MODEL = 1024
N_META = 16
BLOCK = 128
N_PAD = BLOCK - N_META
ROPE_THETA = 10000.0
SWA_HEADS = 8
SWA_KV_HEADS = 2
SWA_HEAD_DIM = 64
MLA_HEADS = 4
MLA_Q_RANK = 256
MLA_KV_RANK = 256
MLA_NOPE_DIM = 128
MLA_ROPE_DIM = 64
MLA_V_DIM = 128
SWA_Q_COLS = SWA_HEADS * SWA_HEAD_DIM
SWA_KV_COLS = SWA_KV_HEADS * SWA_HEAD_DIM
MLA_OUT_COLS = MLA_HEADS * MLA_V_DIM
N_GROUPS = 4
EXPERTS_PER_GROUP = 8
N_EXPERTS = N_GROUPS * EXPERTS_PER_GROUP
D_EXPERT = 256
LN_EPS = 1e-5
RMS_EPS = 1e-6
DEPTH = 1
ALPHA = (2.0 * DEPTH) ** 0.25
NEG = -1e30
LOG2_E = 1.4426950408889634

LANES = 128
COL_KA = SWA_Q_COLS
COL_VA = COL_KA + SWA_KV_COLS
COL_CQ = COL_VA + SWA_KV_COLS
COL_CKV = COL_CQ + MLA_Q_RANK
COL_KR = COL_CKV + MLA_KV_RANK
IN_COLS_PADDED = COL_KR + LANES
MLA_QK_PAD = 256
ROUTE_LANES = 128
EXPERT_LANE0 = 32
ROUTE_CLS, ROUTE_RANK = range(2)
ROUTE_ROWS = 8
PAIRS_PER_GROUP = EXPERTS_PER_GROUP * (EXPERTS_PER_GROUP - 1) // 2
N_CLASSES = N_GROUPS * PAIRS_PER_GROUP
MOE_TILE_ROWS = 256
PROJ_TILE = 1024
WIDE_TILE = 1024
DISPATCH_TILE = 2048
COMBINE_TILE = 1024
DMA_BURST = 8
VMEM_LIMIT = 56 * 1024 * 1024

SWA_HEAD_ORDER = (0, 4, 1, 5, 2, 6, 3, 7)

BF16 = jnp.bfloat16
F32 = jnp.float32


def _cparams(sem):
    return pltpu.CompilerParams(dimension_semantics=sem, vmem_limit_bytes=VMEM_LIMIT)


def _layer_norm(x, g, b):
    mu = jnp.mean(x, axis=-1, keepdims=True)
    xc = x - mu
    var = jnp.mean(xc * xc, axis=-1, keepdims=True)
    return xc * lax.rsqrt(var + LN_EPS) * g + b


def _rms_scale(x):
    return lax.rsqrt(jnp.mean(x * x, axis=-1, keepdims=True) + RMS_EPS)


def _rope128(x, cos, sin_signed, lane_lo):
    up = pltpu.roll(x, LANES - 32, 1)
    dn = pltpu.roll(x, 32, 1)
    return x * cos + jnp.where(lane_lo, up, dn) * sin_signed


def _dot(a, b):
    return jnp.dot(a, b, preferred_element_type=F32)


def _dot_nt(a, b):
    return lax.dot_general(a, b, (((1,), (1,)), ((), ())), preferred_element_type=F32)


def _in_proj_kernel(x_ref, g_ref, b_ref, w_ref, cos_ref, sin_ref, cost_ref, sint_ref, gq_ref,
                    wuqt_ref, gkv_ref, wuk_ref, wuvt_ref,
                    qa_ref, ka_ref, va_ref, qt_ref, km_ref, vt_ref):
    h = _layer_norm(x_ref[0], g_ref[...], b_ref[...])
    u = _dot(h.astype(BF16), w_ref[...])
    cos = cos_ref[...]
    sin = sin_ref[...]
    lane = lax.broadcasted_iota(jnp.int32, (1, LANES), 1)
    lane_lo = (lane % 64) < 32
    rope = functools.partial(_rope128, cos=cos, sin_signed=sin, lane_lo=lane_lo)

    swa_scale = SWA_HEAD_DIM ** -0.5
    for c in range(SWA_Q_COLS // LANES):
        qc = rope(u[:, c * LANES:(c + 1) * LANES]) * swa_scale
        qa_ref[0, :, c * LANES:(c + 1) * LANES] = qc.astype(BF16)
    ka_ref[0] = rope(u[:, COL_KA:COL_VA]).astype(BF16)
    va_ref[0] = u[:, COL_VA:COL_CQ].astype(BF16)

    cq = u[:, COL_CQ:COL_CKV]
    cqn = (cq * _rms_scale(cq) * gq_ref[...]).astype(BF16)
    ckv = u[:, COL_CKV:COL_KR]
    ckvn = (ckv * _rms_scale(ckv) * gkv_ref[...]).astype(BF16)
    k_nope = _dot(ckvn, wuk_ref[...])
    kr = rope(u[:, COL_KR:IN_COLS_PADDED]).astype(BF16)
    q_scale = (MLA_NOPE_DIM + MLA_ROPE_DIM) ** -0.5 * LOG2_E
    cost = cost_ref[...]
    sint = sint_ref[...]
    half = MLA_ROPE_DIM // 2
    for hd in range(MLA_HEADS):
        qt = _dot_nt(wuqt_ref[hd], cqn)
        qr = qt[MLA_NOPE_DIM:MLA_NOPE_DIM + MLA_ROPE_DIM]
        rot = jnp.concatenate([-qr[half:], qr[:half]], axis=0)
        qt_ref[0, hd, 0:MLA_NOPE_DIM, :] = (qt[:MLA_NOPE_DIM] * q_scale).astype(BF16)
        qt_ref[0, hd, MLA_NOPE_DIM:MLA_NOPE_DIM + MLA_ROPE_DIM, :] = (
            (qr * cost + rot * sint) * q_scale).astype(BF16)
        qt_ref[0, hd, MLA_NOPE_DIM + MLA_ROPE_DIM:, :] = (
            qt[MLA_NOPE_DIM + MLA_ROPE_DIM:]).astype(BF16)
        km_ref[0, hd, :, 0:LANES] = k_nope[:, hd * LANES:(hd + 1) * LANES].astype(BF16)
        km_ref[0, hd, :, LANES:2 * LANES] = kr
        vt_ref[0, hd, 0] = _dot_nt(wuvt_ref[hd], ckvn).astype(BF16)


def _in_proj(x, tables, p, tm):
    cos, sin, cos_t, sin_t = tables
    B, S, D = x.shape
    grid = (B, S // tm)
    full = lambda shape: pl.BlockSpec(shape, lambda b, i: (0,) * len(shape))
    out_shape = (
        jax.ShapeDtypeStruct((B, S, SWA_Q_COLS), BF16),
        jax.ShapeDtypeStruct((B, S, SWA_KV_COLS), BF16),
        jax.ShapeDtypeStruct((B, S, SWA_KV_COLS), BF16),
        jax.ShapeDtypeStruct((B, MLA_HEADS, MLA_QK_PAD, S), BF16),
        jax.ShapeDtypeStruct((B, MLA_HEADS, S, MLA_QK_PAD), BF16),
        jax.ShapeDtypeStruct((B, MLA_HEADS, S // tm, MLA_V_DIM, tm), BF16),
    )
    row = lambda w: pl.BlockSpec((1, tm, w), lambda b, i: (b, i, 0))
    head = lambda w: pl.BlockSpec((1, MLA_HEADS, tm, w), lambda b, i: (b, 0, i, 0))
    return pl.pallas_call(
        _in_proj_kernel,
        grid=grid,
        in_specs=[
            row(D),
            full((1, D)), full((1, D)),
            full((D, IN_COLS_PADDED)),
            pl.BlockSpec((tm, LANES), lambda b, i: (i, 0)),
            pl.BlockSpec((tm, LANES), lambda b, i: (i, 0)),
            pl.BlockSpec((MLA_ROPE_DIM, tm), lambda b, i: (0, i)),
            pl.BlockSpec((MLA_ROPE_DIM, tm), lambda b, i: (0, i)),
            full((1, MLA_Q_RANK)), full((MLA_HEADS, MLA_QK_PAD, MLA_Q_RANK)),
            full((1, MLA_KV_RANK)), full((MLA_KV_RANK, MLA_HEADS * MLA_NOPE_DIM)),
            full((MLA_HEADS, MLA_V_DIM, MLA_KV_RANK)),
        ],
        out_specs=(row(SWA_Q_COLS), row(SWA_KV_COLS), row(SWA_KV_COLS),
                   pl.BlockSpec((1, MLA_HEADS, MLA_QK_PAD, tm), lambda b, i: (b, 0, 0, i)),
                   head(MLA_QK_PAD),
                   pl.BlockSpec((1, MLA_HEADS, 1, MLA_V_DIM, tm), lambda b, i: (b, 0, i, 0, 0))),
        out_shape=out_shape,
        compiler_params=_cparams(("parallel", "parallel")),
        name="in_proj",
    )(x, p["ln_in_g"], p["ln_in_b"], p["w_in"], cos, sin, cos_t, sin_t, p["gq"], p["w_uq_t"],
      p["gkv"], p["w_uk"], p["w_uv_t"])


def _swa_blocks(sink_ref, qa_ref, kc_ref, vc_ref, kprev, vprev, first, o_ref, blocks_per_step):
    row = lax.broadcasted_iota(jnp.int32, (2 * BLOCK, BLOCK), 0) % BLOCK
    col = lax.broadcasted_iota(jnp.int32, (2 * BLOCK, BLOCK), 1)
    own = col <= row
    meta_ok = col >= jnp.where(first, N_PAD, 0)
    lane = lax.broadcasted_iota(jnp.int32, (1, LANES), 1)
    lane_kv0 = lane < SWA_HEAD_DIM
    top_rows = lax.broadcasted_iota(jnp.int32, (2 * BLOCK, 1), 0) < BLOCK

    for r in range(blocks_per_step):
        rows = slice(r * BLOCK, (r + 1) * BLOCK)
        if r == 0:
            kp, vp = kprev, vprev
        else:
            prev_rows = slice((r - 1) * BLOCK, r * BLOCK)
            kp, vp = kc_ref[0, prev_rows, :], vc_ref[0, prev_rows, :]
        kc, vc = kc_ref[0, rows, :], vc_ref[0, rows, :]
        for c in range(SWA_Q_COLS // LANES):
            qc = qa_ref[0, rows, c * LANES:(c + 1) * LANES]
            zero = jnp.zeros_like(qc)
            q2 = jnp.concatenate([jnp.where(lane_kv0, qc, zero),
                                  jnp.where(lane_kv0, zero, qc)], axis=0)
            s_prev = _dot_nt(q2, kp)
            if r == 0:
                s_prev = jnp.where(meta_ok, s_prev, NEG)
            s = jnp.where(own, _dot_nt(q2, kc), s_prev)
            sink = jnp.where(top_rows, sink_ref[c], sink_ref[c + 4])
            m = jnp.maximum(jnp.max(s, axis=-1, keepdims=True), sink)
            e = jnp.exp(s - m)
            denom = jnp.sum(e, axis=-1, keepdims=True) + jnp.exp(sink - m)
            pr = e * (1.0 / denom)
            o2 = (_dot(jnp.where(own, pr, 0.0).astype(BF16), vc)
                  + _dot(jnp.where(own, 0.0, pr).astype(BF16), vp))
            o = jnp.where(lane_kv0, o2[:BLOCK], o2[BLOCK:])
            o_ref[0, rows, c * LANES:(c + 1) * LANES] = o.astype(BF16)


def _in_proj_swa_kernel(sink_ref, x_ref, g_ref, b_ref, w_ref, cos_ref, sin_ref, cost_ref, sint_ref,
                        gq_ref, wuqt_ref, gkv_ref, wuk_ref, wuvt_ref, kmeta_ref, vmeta_ref,
                        a_ref, qt_ref, km_ref, vt_ref,
                        qa_s, ka_s, va_s, kprev_s, vprev_s):
    _in_proj_kernel(x_ref, g_ref, b_ref, w_ref, cos_ref, sin_ref, cost_ref, sint_ref, gq_ref,
                    wuqt_ref, gkv_ref, wuk_ref, wuvt_ref, qa_s, ka_s, va_s, qt_ref, km_ref, vt_ref)
    first = pl.program_id(1) == 0
    kprev = jnp.where(first, kmeta_ref[0], kprev_s[...])
    vprev = jnp.where(first, vmeta_ref[0], vprev_s[...])
    tm = qa_s.shape[1]
    _swa_blocks(sink_ref, qa_s, ka_s, va_s, kprev, vprev, first, a_ref, tm // BLOCK)
    kprev_s[...] = ka_s[0, tm - BLOCK:, :]
    vprev_s[...] = va_s[0, tm - BLOCK:, :]


def _in_proj_swa(x, tables, p, ka_meta, va_meta, tm):
    cos, sin, cos_t, sin_t = tables
    B, S, D = x.shape
    kvw = SWA_KV_COLS
    full = lambda shape: pl.BlockSpec(shape, lambda b, i: (0,) * len(shape))
    row = lambda w: pl.BlockSpec((1, tm, w), lambda b, i: (b, i, 0))
    out_shape = (
        jax.ShapeDtypeStruct((B, S, SWA_Q_COLS), BF16),
        jax.ShapeDtypeStruct((B, MLA_HEADS, MLA_QK_PAD, S), BF16),
        jax.ShapeDtypeStruct((B, MLA_HEADS, S, MLA_QK_PAD), BF16),
        jax.ShapeDtypeStruct((B, MLA_HEADS, S // tm, MLA_V_DIM, tm), BF16),
    )
    return pl.pallas_call(
        _in_proj_swa_kernel,
        grid=(B, S // tm),
        in_specs=[
            pl.BlockSpec(memory_space=pltpu.SMEM),
            row(D),
            full((1, D)), full((1, D)),
            full((D, IN_COLS_PADDED)),
            pl.BlockSpec((tm, LANES), lambda b, i: (i, 0)),
            pl.BlockSpec((tm, LANES), lambda b, i: (i, 0)),
            pl.BlockSpec((MLA_ROPE_DIM, tm), lambda b, i: (0, i)),
            pl.BlockSpec((MLA_ROPE_DIM, tm), lambda b, i: (0, i)),
            full((1, MLA_Q_RANK)), full((MLA_HEADS, MLA_QK_PAD, MLA_Q_RANK)),
            full((1, MLA_KV_RANK)), full((MLA_KV_RANK, MLA_HEADS * MLA_NOPE_DIM)),
            full((MLA_HEADS, MLA_V_DIM, MLA_KV_RANK)),
            full((1, BLOCK, kvw)), full((1, BLOCK, kvw)),
        ],
        out_specs=(row(SWA_Q_COLS),
                   pl.BlockSpec((1, MLA_HEADS, MLA_QK_PAD, tm), lambda b, i: (b, 0, 0, i)),
                   pl.BlockSpec((1, MLA_HEADS, tm, MLA_QK_PAD), lambda b, i: (b, 0, i, 0)),
                   pl.BlockSpec((1, MLA_HEADS, 1, MLA_V_DIM, tm), lambda b, i: (b, 0, i, 0, 0))),
        out_shape=out_shape,
        scratch_shapes=[pltpu.VMEM((1, tm, SWA_Q_COLS), BF16), pltpu.VMEM((1, tm, kvw), BF16),
                        pltpu.VMEM((1, tm, kvw), BF16), pltpu.VMEM((BLOCK, kvw), BF16),
                        pltpu.VMEM((BLOCK, kvw), BF16)],
        compiler_params=_cparams(("parallel", "arbitrary")),
        name="in_proj_swa",
    )(p["sinks"], x, p["ln_in_g"], p["ln_in_b"], p["w_in"], cos, sin, cos_t, sin_t, p["gq"],
      p["w_uq_t"], p["gkv"], p["w_uk"], p["w_uv_t"], ka_meta, va_meta)


def _mla_kernel(qt_ref, k_ref, vt_ref, kmeta_ref, vtmeta_ref, o_ref,
                s0a_ref, s0b_ref, s1a_ref, s1b_ref, smax_ref, m_ref, l_ref, acc_ref, *, tq):
    u = pl.program_id(2)
    qts = (qt_ref[0, 0, :, :tq], qt_ref[0, 0, :, tq:])
    bufs = ((s0a_ref, s0b_ref), (s1a_ref, s1b_ref))

    for g in range(2):
        s = _dot(kmeta_ref[0, 0], qts[g])
        m = jnp.max(s, axis=0, keepdims=True)
        p = jnp.exp2(s - m)
        m_ref[g] = m
        l_ref[g] = jnp.sum(p, axis=0, keepdims=True)
        acc_ref[g] = _dot(vtmeta_ref[0, 0, 0], p.astype(BF16))

    def scores_into(g, ab, j):
        start = pl.multiple_of(j * tq, tq)
        s = _dot(k_ref[0, 0, pl.ds(start, tq), :], qts[g])
        bufs[g][ab][...] = s
        smax_ref[g, ab] = jnp.max(s, axis=0, keepdims=True)

    def absorb(g, ab, j, causal=False):
        s = bufs[g][ab][...]
        if causal:
            key = lax.broadcasted_iota(jnp.int32, (tq, tq), 0)
            query = lax.broadcasted_iota(jnp.int32, (tq, tq), 1)
            s = jnp.where(key <= query, s, NEG)
            smax = jnp.max(s, axis=0, keepdims=True)
        else:
            smax = smax_ref[g, ab]
        m = m_ref[g]
        m_new = jnp.maximum(m, smax)
        alpha = jnp.exp2(m - m_new)
        p = jnp.exp2(s - m_new)
        m_ref[g] = m_new
        l_ref[g] = alpha * l_ref[g] + jnp.sum(p, axis=0, keepdims=True)
        acc_ref[g] = alpha * acc_ref[g] + _dot(vt_ref[0, 0, j], p.astype(BF16))

    buf_a, buf_b = 0, 1
    for g in range(2):
        scores_into(g, buf_a, 0)

    def pair(t, carry):
        for g in range(2):
            scores_into(g, buf_b, 2 * t + 1)
            absorb(g, buf_a, 2 * t)
        for g in range(2):
            scores_into(g, buf_a, 2 * t + 2)
            absorb(g, buf_b, 2 * t + 1)
        return carry

    lax.fori_loop(0, u, pair, 0)

    scores_into(1, buf_b, 2 * u + 1)
    absorb(1, buf_a, 2 * u)
    absorb(0, buf_a, 2 * u, causal=True)
    absorb(1, buf_b, 2 * u + 1, causal=True)

    for g in range(2):
        o_ref[0, g * tq:(g + 1) * tq, :] = (acc_ref[g] * (1.0 / l_ref[g])).T.astype(BF16)


def _mla(qt, km, vt, km_meta, vt_meta, tq):
    B, H, _, S = qt.shape
    assert S % (2 * tq) == 0, "a step handles an (even, odd) pair of query tiles"
    return pl.pallas_call(
        functools.partial(_mla_kernel, tq=tq),
        grid=(B, H, S // (2 * tq)),
        in_specs=[
            pl.BlockSpec((1, 1, MLA_QK_PAD, 2 * tq), lambda b, h, u: (b, h, 0, u)),
            pl.BlockSpec((1, 1, S, MLA_QK_PAD), lambda b, h, u: (b, h, 0, 0)),
            pl.BlockSpec((1, 1, S // tq, MLA_V_DIM, tq), lambda b, h, u: (b, h, 0, 0, 0)),
            pl.BlockSpec((1, 1, N_META, MLA_QK_PAD), lambda b, h, u: (0, h, 0, 0)),
            pl.BlockSpec((1, 1, 1, MLA_V_DIM, N_META), lambda b, h, u: (0, h, 0, 0, 0)),
        ],
        out_specs=pl.BlockSpec((1, 2 * tq, MLA_V_DIM), lambda b, h, u: (b, u, h)),
        out_shape=jax.ShapeDtypeStruct((B, S, MLA_OUT_COLS), BF16),
        scratch_shapes=[pltpu.VMEM((tq, tq), F32), pltpu.VMEM((tq, tq), F32),
                        pltpu.VMEM((tq, tq), F32), pltpu.VMEM((tq, tq), F32),
                        pltpu.VMEM((2, 2, 1, tq), F32),
                        pltpu.VMEM((2, 1, tq), F32), pltpu.VMEM((2, 1, tq), F32),
                        pltpu.VMEM((2, MLA_V_DIM, tq), F32)],
        compiler_params=_cparams(("parallel", "parallel", "arbitrary")),
        name="mla_attn",
    )(qt, km, vt, km_meta, vt_meta)


def _out_proj_kernel(x_ref, a_ref, b_ref, gin_ref, bin_ref, ga_ref, gb_ref, wo_ref, g1_ref, b1_ref,
                     wrt_ref, brt_ref, h1_ref, route_ref, counts_out_ref, count_ref):
    h = _layer_norm(x_ref[0], gin_ref[...], bin_ref[...])
    a = a_ref[0].astype(F32)
    b = b_ref[0].astype(F32)
    an = (a * _rms_scale(a) * ga_ref[...]).astype(BF16)
    bn = (b * _rms_scale(b) * gb_ref[...]).astype(BF16)
    mix = _dot(jnp.concatenate([an, bn], axis=-1), wo_ref[...])
    h1 = _layer_norm(ALPHA * h + mix, g1_ref[...], b1_ref[...])
    h1_ref[0] = h1

    x_hi = h1.astype(BF16)
    x_lo = (h1 - x_hi.astype(F32)).astype(BF16)
    hi_both = _dot(x_hi, wrt_ref[...])
    logits = (hi_both[:, :ROUTE_LANES] + hi_both[:, ROUTE_LANES:]
              + _dot(x_lo, wrt_ref[:, :ROUTE_LANES]) + brt_ref[...])
    lane = lax.broadcasted_iota(jnp.int32, logits.shape, 1).astype(F32)
    ninf = -jnp.inf
    no_lane = float(ROUTE_LANES)
    gl = jnp.where(lane < N_GROUPS, logits, ninf)
    gmax = jnp.max(gl, axis=-1, keepdims=True)
    g_idx = jnp.min(jnp.where(gl == gmax, lane, no_lane), axis=-1, keepdims=True)
    lo = EXPERT_LANE0 + g_idx * EXPERTS_PER_GROUP
    el = jnp.where((lane >= lo) & (lane < lo + EXPERTS_PER_GROUP), logits, ninf)
    m1 = jnp.max(el, axis=-1, keepdims=True)
    i1 = jnp.min(jnp.where(el == m1, lane, no_lane), axis=-1, keepdims=True)
    el2 = jnp.where(lane == i1, ninf, el)
    m2 = jnp.max(el2, axis=-1, keepdims=True)
    i2 = jnp.min(jnp.where(el2 == m2, lane, no_lane), axis=-1, keepdims=True)

    a = jnp.minimum(i1, i2) - lo
    b = jnp.maximum(i1, i2) - lo
    cls = (g_idx * PAIRS_PER_GROUP + a * (2.0 * EXPERTS_PER_GROUP - 1.0 - a) * 0.5
           + (b - a - 1.0))

    @pl.when((pl.program_id(0) == 0) & (pl.program_id(1) == 0))
    def _():
        count_ref[...] = jnp.zeros_like(count_ref)

    tm = logits.shape[0]
    onehot = jnp.where(lane == cls, 1.0, 0.0)
    r_i = lax.broadcasted_iota(jnp.int32, (BLOCK, BLOCK), 0)
    c_i = lax.broadcasted_iota(jnp.int32, (BLOCK, BLOCK), 1)
    earlier = jnp.where(r_i > c_i, 1.0, 0.0).astype(BF16)
    running = count_ref[...]
    before = []
    for blk in range(tm // BLOCK):
        oh = onehot[blk * BLOCK:(blk + 1) * BLOCK]
        before.append(_dot(earlier, oh.astype(BF16)) + running)
        running = running + jnp.sum(oh, axis=0, keepdims=True)
    before = jnp.concatenate(before, axis=0)
    rank = jnp.sum(jnp.where(lane == cls, before, 0.0), axis=-1, keepdims=True)
    count_ref[...] = running
    counts_out_ref[...] = running

    route = jnp.where(lane == ROUTE_CLS, cls, 0.0)
    route = jnp.where(lane == ROUTE_RANK, rank, route)
    route_ref[0, 0] = route.T[:ROUTE_ROWS].astype(jnp.int32)


def _out_proj(x, a_out, b_out, p, tm):
    B, S, D = x.shape
    full = lambda shape: pl.BlockSpec(shape, lambda b, i: (0,) * len(shape))
    row = lambda w: pl.BlockSpec((1, tm, w), lambda b, i: (b, i, 0))
    return pl.pallas_call(
        _out_proj_kernel,
        grid=(B, S // tm),
        in_specs=[row(D), row(SWA_Q_COLS), row(MLA_OUT_COLS),
                  full((1, D)), full((1, D)), full((1, SWA_Q_COLS)), full((1, MLA_OUT_COLS)),
                  full((D, D)), full((1, D)), full((1, D)),
                  full((D, 2 * ROUTE_LANES)), full((1, ROUTE_LANES))],
        out_specs=(row(D), pl.BlockSpec((1, 1, ROUTE_ROWS, tm), lambda b, i: (b, i, 0, 0)),
                   full((1, ROUTE_LANES))),
        out_shape=(jax.ShapeDtypeStruct((B, S, D), F32),
                   jax.ShapeDtypeStruct((B, S // tm, ROUTE_ROWS, tm), jnp.int32),
                   jax.ShapeDtypeStruct((1, ROUTE_LANES), F32)),
        scratch_shapes=[pltpu.VMEM((1, ROUTE_LANES), F32)],
        compiler_params=_cparams(("arbitrary", "arbitrary")),
        name="out_proj_route",
    )(x, a_out, b_out, p["ln_in_g"], p["ln_in_b"], p["ga"], p["gb"], p["w_o"], p["ln1_g"],
      p["ln1_b"], p["w_route"], p["b_route"])


SLAB = D_MODEL // LANES


def _slab_copy(src_ref, src_tok8, dst_ref, dst_tok8, sem):
    return pltpu.make_async_copy(src_ref.at[pl.ds(pl.multiple_of(src_tok8, SLAB), SLAB)],
                                 dst_ref.at[pl.ds(pl.multiple_of(dst_tok8, SLAB), SLAB)], sem)


def _to_slabs(x, slab_ref, n):
    for s in range(SLAB):
        slab_ref[pl.ds(s, n, stride=SLAB), :] = x[:, s * LANES:(s + 1) * LANES]


def _from_slabs(slab_ref, n):
    return jnp.concatenate([slab_ref[pl.ds(s, n, stride=SLAB), :] for s in range(SLAB)], axis=1)


def _dispatch_kernel(padstart_ref, nu_ref, pos_ref, h1_ref, xs_ref, rows_ref, zero_ref, sem, *,
                     tm, n_tiles):
    tail = zero_ref.shape[0]

    def zero_fill(start):
        return pltpu.make_async_copy(
            zero_ref, xs_ref.at[pl.ds(pl.multiple_of(start, SLAB), tail)], sem)

    @pl.when(pl.program_id(0) == 0)
    def _():
        zero_ref[...] = jnp.zeros_like(zero_ref)
        for parity in range(2):
            tails = [zero_fill(padstart_ref[c]) for c in range(parity, N_CLASSES, 2)]
            for t in tails:
                t.start()
            for t in tails:
                t.wait()

        def fill_unused(j, carry):
            fill = zero_fill(j * tail)
            fill.start()
            fill.wait()
            return carry

        lax.fori_loop(nu_ref[0], n_tiles + 1, fill_unused, 0)

    _to_slabs(h1_ref[...], rows_ref, tm)

    def issue(r0, carry):
        for k in range(DMA_BURST):
            r = r0 * DMA_BURST + k
            _slab_copy(rows_ref, r * SLAB, xs_ref, pos_ref[0, 0, r], sem).start(priority=k % 2)
        return carry

    lax.fori_loop(0, tm // DMA_BURST, issue, 0)
    pltpu.make_async_copy(rows_ref, xs_ref.at[pl.ds(0, tm * SLAB)], sem).wait()


def _dispatch(h1, pos8, pad_start8, n_used, n_tiles, tm):
    T, D = h1.shape
    pos8 = pos8.reshape(T // tm, 1, tm)
    n_rows = (n_tiles + 1) * MOE_TILE_ROWS
    grid_spec = pltpu.PrefetchScalarGridSpec(
        num_scalar_prefetch=2,
        grid=(T // tm,),
        in_specs=[pl.BlockSpec((1, 1, tm), lambda t, ps, nu: (t, 0, 0),
                               memory_space=pltpu.SMEM),
                  pl.BlockSpec((tm, D), lambda t, ps, nu: (t, 0))],
        out_specs=pl.BlockSpec(memory_space=pl.ANY),
        scratch_shapes=[pltpu.VMEM((tm * SLAB, LANES), F32),
                        pltpu.VMEM((MOE_TILE_ROWS * SLAB, LANES), F32),
                        pltpu.SemaphoreType.DMA],
    )
    return pl.pallas_call(
        functools.partial(_dispatch_kernel, tm=tm, n_tiles=n_tiles),
        grid_spec=grid_spec,
        out_shape=jax.ShapeDtypeStruct((n_rows * SLAB, LANES), F32),
        compiler_params=_cparams(("arbitrary",)),
        name="moe_dispatch",
    )(pad_start8, n_used, pos8, h1)


def _experts_kernel(tlo_ref, thi_ref, nu_ref, x_ref, wg_lo_ref, wg_hi_ref, wu_lo_ref, wu_hi_ref,
                    wd_lo_ref, wd_hi_ref, wrt_ref, brt_ref, y_ref):
    j = pl.program_id(0)
    tr = MOE_TILE_ROWS

    @pl.when(j < nu_ref[0])
    def _():
        x = _from_slabs(x_ref, tr).astype(BF16)
        lo_e = tlo_ref[j]
        hi_e = thi_ref[j]
        logits = _dot(x, wrt_ref[...]) + brt_ref[...]
        lane = lax.broadcasted_iota(jnp.int32, logits.shape, 1)
        gl = jnp.where(lane < N_GROUPS, logits, -jnp.inf)
        ge = jnp.exp(gl - jnp.max(gl, axis=-1, keepdims=True))
        pick = lambda v, at: jnp.sum(jnp.where(lane == at, v, 0.0), axis=-1, keepdims=True)
        g_prob = pick(ge, lo_e // EXPERTS_PER_GROUP) / jnp.sum(ge, axis=-1, keepdims=True)
        e_lo = pick(logits, EXPERT_LANE0 + lo_e)
        e_hi = pick(logits, EXPERT_LANE0 + hi_e)
        gates = (g_prob / (1.0 + jnp.exp(e_hi - e_lo)), g_prob / (1.0 + jnp.exp(e_lo - e_hi)))

        y = None
        for gate, wg_ref, wu_ref, wd_ref in zip(gates, (wg_lo_ref, wg_hi_ref), (wu_lo_ref, wu_hi_ref),
                                                (wd_lo_ref, wd_hi_ref)):
            g = _dot(x, wg_ref[0])
            u = _dot(x, wu_ref[0])
            hid = g * (1.0 / (1.0 + jnp.exp(-g))) * u * gate
            y_k = _dot(hid.astype(BF16), wd_ref[0])
            y = y_k if y is None else y + y_k
        _to_slabs(y, y_ref, tr)

    @pl.when(j >= nu_ref[0])
    def _():
        y_ref[...] = jnp.zeros_like(y_ref)


def _experts(xs, tile_lo, tile_hi, n_used, p, n_tiles):
    D = D_MODEL
    tr = MOE_TILE_ROWS
    last_used = lambda j, nu: jnp.minimum(j, nu[0] - 1)
    wgu = lambda pick: pl.BlockSpec((1, D, D_EXPERT),
                                    lambda j, tlo, thi, nu: (pick(tlo, thi)[j], 0, 0))
    wd = lambda pick: pl.BlockSpec((1, D_EXPERT, D),
                                   lambda j, tlo, thi, nu: (pick(tlo, thi)[j], 0, 0))
    lo = lambda tlo, thi: tlo
    hi = lambda tlo, thi: thi
    grid_spec = pltpu.PrefetchScalarGridSpec(
        num_scalar_prefetch=3,
        grid=(n_tiles,),
        in_specs=[pl.BlockSpec((tr * SLAB, LANES), lambda j, tlo, thi, nu: (last_used(j, nu), 0)),
                  wgu(lo), wgu(hi), wgu(lo), wgu(hi), wd(lo), wd(hi),
                  pl.BlockSpec((D, ROUTE_LANES), lambda j, tlo, thi, nu: (0, 0)),
                  pl.BlockSpec((1, ROUTE_LANES), lambda j, tlo, thi, nu: (0, 0))],
        out_specs=pl.BlockSpec((tr * SLAB, LANES), lambda j, tlo, thi, nu: (j, 0)),
    )
    return pl.pallas_call(
        _experts_kernel,
        grid_spec=grid_spec,
        out_shape=jax.ShapeDtypeStruct((n_tiles * tr * SLAB, LANES), F32),
        compiler_params=_cparams(("arbitrary",)),
        name="moe_experts",
    )(tile_lo, tile_hi, n_used, xs, p["w_gate"], p["w_gate"], p["w_up"], p["w_up"],
      p["w_down"], p["w_down"],
      p["w_route"], p["b_route"])


def _combine_kernel(pos_ref, pos_next_ref, h1_ref, ys_ref, g2_ref, b2_ref, o_ref,
                    ybuf_ref, sem, *, tm):
    t = pl.program_id(0)
    slot = t % 2

    def gather(idx_ref, dst_slot):
        def issue(r0, carry):
            for k in range(DMA_BURST):
                r = r0 * DMA_BURST + k
                _slab_copy(ys_ref, idx_ref[0, 0, r], ybuf_ref.at[dst_slot], r * SLAB,
                           sem.at[dst_slot]).start(priority=k % 2)
            return carry

        lax.fori_loop(0, tm // DMA_BURST, issue, 0)

    @pl.when(t == 0)
    def _():
        gather(pos_ref, 0)

    @pl.when(t + 1 < pl.num_programs(0))
    def _():
        gather(pos_next_ref, 1 - slot)

    pltpu.make_async_copy(ys_ref.at[pl.ds(0, tm * SLAB)], ybuf_ref.at[slot], sem.at[slot]).wait()
    ffn = _from_slabs(ybuf_ref.at[slot], tm)
    o_ref[...] = _layer_norm(ALPHA * h1_ref[...] + ffn, g2_ref[...], b2_ref[...])


def _combine(h1, pos8, ys, p, tm):
    T, D = h1.shape
    n_steps = T // tm
    pos8 = pos8.reshape(n_steps, 1, tm)
    full = lambda shape: pl.BlockSpec(shape, lambda t: (0,) * len(shape))
    return pl.pallas_call(
        functools.partial(_combine_kernel, tm=tm),
        grid=(n_steps,),
        in_specs=[pl.BlockSpec((1, 1, tm), lambda t: (t, 0, 0), memory_space=pltpu.SMEM),
                  pl.BlockSpec((1, 1, tm), lambda t: (jnp.minimum(t + 1, n_steps - 1), 0, 0),
                               memory_space=pltpu.SMEM),
                  pl.BlockSpec((tm, D), lambda t: (t, 0)),
                  pl.BlockSpec(memory_space=pl.ANY),
                  full((1, D)), full((1, D))],
        out_specs=pl.BlockSpec((tm, D), lambda t: (t, 0)),
        out_shape=jax.ShapeDtypeStruct((T, D), F32),
        scratch_shapes=[pltpu.VMEM((2, tm * SLAB, LANES), F32),
                        pltpu.SemaphoreType.DMA((2,))],
        compiler_params=_cparams(("arbitrary",)),
        name="moe_combine_ln2",
    )(pos8, pos8, h1, ys, p["ln2_g"], p["ln2_b"])


def _dispatch_plan(route, counts):
    T = route.shape[0] * route.shape[1] * route.shape[3]
    tr = MOE_TILE_ROWS
    n_tiles = T // tr + N_CLASSES
    cnt = counts[0, :N_CLASSES].astype(jnp.int32)
    padded = jnp.maximum((cnt + tr - 1) // tr, 1) * tr
    classes = jnp.arange(N_CLASSES, dtype=jnp.int32)
    seg_end = jnp.sum(jnp.where(classes[None, :] <= classes[:, None], padded[None, :], 0), axis=1)
    seg_start = seg_end - padded
    cls = route[:, :, ROUTE_CLS, :].reshape(T)
    rank = route[:, :, ROUTE_RANK, :].reshape(T)
    start_of = jnp.sum(jnp.where(cls[:, None] == classes, seg_start, 0), axis=-1)
    pos8 = (start_of + rank) * SLAB
    tile_row0 = jnp.arange(n_tiles, dtype=jnp.int32) * tr
    tile_cls = jnp.sum((seg_end[None, :] <= tile_row0[:, None]).astype(jnp.int32), axis=-1)
    tile_cls = jnp.minimum(tile_cls, N_CLASSES - 1)
    pairs = [(a, b) for a in range(EXPERTS_PER_GROUP) for b in range(a + 1, EXPERTS_PER_GROUP)]
    lo_of = jnp.asarray([g * EXPERTS_PER_GROUP + a for g in range(N_GROUPS) for a, _ in pairs],
                        jnp.int32)
    hi_of = jnp.asarray([g * EXPERTS_PER_GROUP + b for g in range(N_GROUPS) for _, b in pairs],
                        jnp.int32)
    is_cls = tile_cls[:, None] == classes
    tile_lo = jnp.sum(jnp.where(is_cls, lo_of, 0), axis=-1)
    tile_hi = jnp.sum(jnp.where(is_cls, hi_of, 0), axis=-1)
    n_used = (seg_end[-1:] // tr).astype(jnp.int32)
    return pos8, (seg_start + cnt) * SLAB, tile_lo, tile_hi, n_used, n_tiles


def _rope_tables(pos):
    d = SWA_HEAD_DIM
    inv_freq = ROPE_THETA ** (-jnp.arange(0, d, 2, dtype=F32) / d)
    ang = pos[:, None] * inv_freq[None, :]
    c, s = jnp.cos(ang), jnp.sin(ang)
    return (jnp.concatenate([c, c, c, c], axis=-1), jnp.concatenate([-s, s, -s, s], axis=-1),
            jnp.concatenate([c, c], axis=-1).T, jnp.concatenate([s, s], axis=-1).T)


def _prep_params(ln_in_g, ln_in_b, w_in, swa_sinks, mla_q_norm_g, mla_w_uq, mla_kv_norm_g, mla_w_ukv,
                 swa_out_norm_g, mla_out_norm_g, w_o, ln1_g, ln1_b, moe_w_group, moe_b_group,
                 moe_w_router, moe_b_router, moe_w_gate, moe_w_up, moe_w_down, ln2_g, ln2_b):
    order = jnp.asarray(SWA_HEAD_ORDER)
    perm = (order[:, None] * SWA_HEAD_DIM + jnp.arange(SWA_HEAD_DIM)[None, :]).reshape(-1)
    w = w_in[0]
    w_q = w[:, :SWA_Q_COLS][:, perm]
    w_pad = jnp.zeros((D_MODEL, IN_COLS_PADDED - w.shape[1]), w.dtype)
    w_in_p = jnp.concatenate([w_q, w[:, SWA_Q_COLS:], w_pad], axis=1).astype(BF16)

    dqk = MLA_NOPE_DIM + MLA_ROPE_DIM
    wuq = mla_w_uq[0].reshape(MLA_Q_RANK, MLA_HEADS, dqk)
    wuq = jnp.pad(wuq, ((0, 0), (0, 0), (0, MLA_QK_PAD - dqk)))
    wuq_t = jnp.transpose(wuq, (1, 2, 0)).astype(BF16)
    wukv = mla_w_ukv[0].reshape(MLA_KV_RANK, MLA_HEADS, MLA_NOPE_DIM + MLA_V_DIM)
    wuk = wukv[:, :, :MLA_NOPE_DIM].reshape(MLA_KV_RANK, MLA_HEADS * MLA_NOPE_DIM).astype(BF16)
    wuv_t = jnp.transpose(wukv[:, :, MLA_NOPE_DIM:], (1, 2, 0)).astype(BF16)

    w_o0 = w_o[0]
    w_o_p = jnp.concatenate([w_o0[:SWA_Q_COLS][perm], w_o0[SWA_Q_COLS:]], axis=0).astype(BF16)

    w_route = jnp.zeros((D_MODEL, ROUTE_LANES), F32)
    w_route = w_route.at[:, :N_GROUPS].set(moe_w_group[0])
    w_route = w_route.at[:, EXPERT_LANE0:EXPERT_LANE0 + N_EXPERTS].set(moe_w_router[0])
    w_route_hi = w_route.astype(BF16)
    w_route_lo = (w_route - w_route_hi.astype(F32)).astype(BF16)
    w_route = jnp.concatenate([w_route_hi, w_route_lo], axis=1)
    b_route = jnp.zeros((1, ROUTE_LANES), F32)
    b_route = b_route.at[0, :N_GROUPS].set(moe_b_group[0])
    b_route = b_route.at[0, EXPERT_LANE0:EXPERT_LANE0 + N_EXPERTS].set(moe_b_router[0])

    return {
        "ln_in_g": ln_in_g[None, :], "ln_in_b": ln_in_b[None, :],
        "w_in": w_in_p,
        "sinks": swa_sinks[0],
        "gq": mla_q_norm_g, "w_uq_t": wuq_t,
        "gkv": mla_kv_norm_g, "w_uk": wuk, "w_uv_t": wuv_t,
        "ga": swa_out_norm_g[0][perm][None, :], "gb": mla_out_norm_g,
        "w_o": w_o_p, "ln1_g": ln1_g, "ln1_b": ln1_b,
        "w_route": w_route, "b_route": b_route,
        "w_gate": moe_w_gate[0].astype(BF16), "w_up": moe_w_up[0].astype(BF16),
        "w_down": moe_w_down[0].astype(BF16),
        "ln2_g": ln2_g, "ln2_b": ln2_b,
    }


def kernel(x, meta_tokens, ln_in_g, ln_in_b, w_in, swa_sinks, mla_q_norm_g, mla_w_uq, mla_kv_norm_g, mla_w_ukv, swa_out_norm_g, mla_out_norm_g, w_o, ln1_g, ln1_b, moe_w_group, moe_b_group, moe_w_router, moe_b_router, moe_w_gate, moe_w_up, moe_w_down, ln2_g, ln2_b):
    B, S, D = x.shape
    p = _prep_params(ln_in_g, ln_in_b, w_in, swa_sinks, mla_q_norm_g, mla_w_uq, mla_kv_norm_g,
                     mla_w_ukv, swa_out_norm_g, mla_out_norm_g, w_o, ln1_g, ln1_b, moe_w_group,
                     moe_b_group, moe_w_router, moe_b_router, moe_w_gate, moe_w_up, moe_w_down,
                     ln2_g, ln2_b)
    T = B * S
    tm = min(PROJ_TILE, S)

    x_meta = jnp.concatenate([jnp.zeros((N_PAD, D), x.dtype), meta_tokens.astype(x.dtype)])[None]
    pos_meta = jnp.maximum(jnp.arange(BLOCK) - N_PAD, 0).astype(F32)
    pos_tok = (jnp.arange(S) + N_META).astype(F32)
    _, ka_m, va_m, _, km_m, vt_m = _in_proj(x_meta, _rope_tables(pos_meta), p, BLOCK)
    a_out, qt, km, vt = _in_proj_swa(x, _rope_tables(pos_tok), p, ka_m, va_m, tm)
    b_out = _mla(qt, km, vt, km_m[:, :, N_PAD:, :], vt_m[..., N_PAD:], tm)
    h1, route, counts = _out_proj(x, a_out, b_out, p, min(WIDE_TILE, S))
    h1 = h1.reshape(B * S, D)

    pos8, pad_start8, tile_lo, tile_hi, n_used, n_tiles = _dispatch_plan(route, counts)
    xs = _dispatch(h1, pos8, pad_start8, n_used, n_tiles, min(DISPATCH_TILE, T))
    ys = _experts(xs, tile_lo, tile_hi, n_used, p, n_tiles)
    out = _combine(h1, pos8, ys, p, min(COMBINE_TILE, T))
    return out.reshape(B, S, D)
```

```python
import functools

import jax
import jax.numpy as jnp
from jax import lax
from jax.experimental import pallas as pl
from jax.experimental.pallas import tpu as pltpu

D_MODEL = 1024
N_META = 16
BLOCK = 128
N_PAD = BLOCK - N_META
ROPE_THETA = 10000.0
SWA_HEADS = 8
SWA_KV_HEADS = 2
SWA_HEAD_DIM = 64
MLA_HEADS = 4
MLA_Q_RANK = 256
MLA_KV_RANK = 256
MLA_NOPE_DIM = 128
MLA_ROPE_DIM = 64
MLA_V_DIM = 128
SWA_Q_COLS = SWA_HEADS * SWA_HEAD_DIM
SWA_KV_COLS = SWA_KV_HEADS * SWA_HEAD_DIM
MLA_OUT_COLS = MLA_HEADS * MLA_V_DIM
N_GROUPS = 4
EXPERTS_PER_GROUP = 8
N_EXPERTS = N_GROUPS * EXPERTS_PER_GROUP
D_EXPERT = 256
LN_EPS = 1e-5
RMS_EPS = 1e-6
DEPTH = 1
ALPHA = (2.0 * DEPTH) ** 0.25
NEG = -1e30
LOG2_E = 1.4426950408889634

LANES = 128
COL_KA = SWA_Q_COLS
COL_VA = COL_KA + SWA_KV_COLS
COL_CQ = COL_VA + SWA_KV_COLS
COL_CKV = COL_CQ + MLA_Q_RANK
COL_KR = COL_CKV + MLA_KV_RANK
IN_COLS_PADDED = COL_KR + LANES
MLA_QK_PAD = 256
ROUTE_LANES = 128
EXPERT_LANE0 = 32
ROUTE_CLS, ROUTE_RANK = range(2)
ROUTE_ROWS = 8
PAIRS_PER_GROUP = EXPERTS_PER_GROUP * (EXPERTS_PER_GROUP - 1) // 2
N_CLASSES = N_GROUPS * PAIRS_PER_GROUP
MOE_TILE_ROWS = 256
PROJ_TILE = 1024
WIDE_TILE = 1024
DISPATCH_TILE = 2048
COMBINE_TILE = 1024
DMA_BURST = 8
VMEM_LIMIT = 56 * 1024 * 1024

SWA_HEAD_ORDER = (0, 4, 1, 5, 2, 6, 3, 7)

BF16 = jnp.bfloat16
F32 = jnp.float32


def _cparams(sem):
    return pltpu.CompilerParams(dimension_semantics=sem, vmem_limit_bytes=VMEM_LIMIT)


def _layer_norm(x, g, b):
    mu = jnp.mean(x, axis=-1, keepdims=True)
    xc = x - mu
    var = jnp.mean(xc * xc, axis=-1, keepdims=True)
    return xc * lax.rsqrt(var + LN_EPS) * g + b


def _rms_scale(x):
    return lax.rsqrt(jnp.mean(x * x, axis=-1, keepdims=True) + RMS_EPS)


def _rope128(x, cos, sin_signed, lane_lo):
    up = pltpu.roll(x, LANES - 32, 1)
    dn = pltpu.roll(x, 32, 1)
    return x * cos + jnp.where(lane_lo, up, dn) * sin_signed


def _dot(a, b):
    return jnp.dot(a, b, preferred_element_type=F32)


def _dot_nt(a, b):
    return lax.dot_general(a, b, (((1,), (1,)), ((), ())), preferred_element_type=F32)


def _in_proj_kernel(x_ref, g_ref, b_ref, w_ref, cos_ref, sin_ref, cost_ref, sint_ref, gq_ref,
                    wuqt_ref, gkv_ref, wuk_ref, wuvt_ref,
                    qa_ref, ka_ref, va_ref, qt_ref, km_ref, vt_ref):
    h = _layer_norm(x_ref[0], g_ref[...], b_ref[...])
    u = _dot(h.astype(BF16), w_ref[...])
    cos = cos_ref[...]
    sin = sin_ref[...]
    lane = lax.broadcasted_iota(jnp.int32, (1, LANES), 1)
    lane_lo = (lane % 64) < 32
    rope = functools.partial(_rope128, cos=cos, sin_signed=sin, lane_lo=lane_lo)

    swa_scale = SWA_HEAD_DIM ** -0.5
    for c in range(SWA_Q_COLS // LANES):
        qc = rope(u[:, c * LANES:(c + 1) * LANES]) * swa_scale
        qa_ref[0, :, c * LANES:(c + 1) * LANES] = qc.astype(BF16)
    ka_ref[0] = rope(u[:, COL_KA:COL_VA]).astype(BF16)
    va_ref[0] = u[:, COL_VA:COL_CQ].astype(BF16)

    cq = u[:, COL_CQ:COL_CKV]
    cqn = (cq * _rms_scale(cq) * gq_ref[...]).astype(BF16)
    ckv = u[:, COL_CKV:COL_KR]
    ckvn = (ckv * _rms_scale(ckv) * gkv_ref[...]).astype(BF16)
    k_nope = _dot(ckvn, wuk_ref[...])
    kr = rope(u[:, COL_KR:IN_COLS_PADDED]).astype(BF16)
    q_scale = (MLA_NOPE_DIM + MLA_ROPE_DIM) ** -0.5 * LOG2_E
    cost = cost_ref[...]
    sint = sint_ref[...]
    half = MLA_ROPE_DIM // 2
    for hd in range(MLA_HEADS):
        qt = _dot_nt(wuqt_ref[hd], cqn)
        qr = qt[MLA_NOPE_DIM:MLA_NOPE_DIM + MLA_ROPE_DIM]
        rot = jnp.concatenate([-qr[half:], qr[:half]], axis=0)
        qt_ref[0, hd, 0:MLA_NOPE_DIM, :] = (qt[:MLA_NOPE_DIM] * q_scale).astype(BF16)
        qt_ref[0, hd, MLA_NOPE_DIM:MLA_NOPE_DIM + MLA_ROPE_DIM, :] = (
            (qr * cost + rot * sint) * q_scale).astype(BF16)
        qt_ref[0, hd, MLA_NOPE_DIM + MLA_ROPE_DIM:, :] = (
            qt[MLA_NOPE_DIM + MLA_ROPE_DIM:]).astype(BF16)
        km_ref[0, hd, :, 0:LANES] = k_nope[:, hd * LANES:(hd + 1) * LANES].astype(BF16)
        km_ref[0, hd, :, LANES:2 * LANES] = kr
        vt_ref[0, hd, 0] = _dot_nt(wuvt_ref[hd], ckvn).astype(BF16)


def _in_proj(x, tables, p, tm):
    cos, sin, cos_t, sin_t = tables
    B, S, D = x.shape
    grid = (B, S // tm)
    full = lambda shape: pl.BlockSpec(shape, lambda b, i: (0,) * len(shape))
    out_shape = (
        jax.ShapeDtypeStruct((B, S, SWA_Q_COLS), BF16),
        jax.ShapeDtypeStruct((B, S, SWA_KV_COLS), BF16),
        jax.ShapeDtypeStruct((B, S, SWA_KV_COLS), BF16),
        jax.ShapeDtypeStruct((B, MLA_HEADS, MLA_QK_PAD, S), BF16),
        jax.ShapeDtypeStruct((B, MLA_HEADS, S, MLA_QK_PAD), BF16),
        jax.ShapeDtypeStruct((B, MLA_HEADS, S // tm, MLA_V_DIM, tm), BF16),
    )
    row = lambda w: pl.BlockSpec((1, tm, w), lambda b, i: (b, i, 0))
    head = lambda w: pl.BlockSpec((1, MLA_HEADS, tm, w), lambda b, i: (b, 0, i, 0))
    return pl.pallas_call(
        _in_proj_kernel,
        grid=grid,
        in_specs=[
            row(D),
            full((1, D)), full((1, D)),
            full((D, IN_COLS_PADDED)),
            pl.BlockSpec((tm, LANES), lambda b, i: (i, 0)),
            pl.BlockSpec((tm, LANES), lambda b, i: (i, 0)),
            pl.BlockSpec((MLA_ROPE_DIM, tm), lambda b, i: (0, i)),
            pl.BlockSpec((MLA_ROPE_DIM, tm), lambda b, i: (0, i)),
            full((1, MLA_Q_RANK)), full((MLA_HEADS, MLA_QK_PAD, MLA_Q_RANK)),
            full((1, MLA_KV_RANK)), full((MLA_KV_RANK, MLA_HEADS * MLA_NOPE_DIM)),
            full((MLA_HEADS, MLA_V_DIM, MLA_KV_RANK)),
        ],
        out_specs=(row(SWA_Q_COLS), row(SWA_KV_COLS), row(SWA_KV_COLS),
                   pl.BlockSpec((1, MLA_HEADS, MLA_QK_PAD, tm), lambda b, i: (b, 0, 0, i)),
                   head(MLA_QK_PAD),
                   pl.BlockSpec((1, MLA_HEADS, 1, MLA_V_DIM, tm), lambda b, i: (b, 0, i, 0, 0))),
        out_shape=out_shape,
        compiler_params=_cparams(("parallel", "parallel")),
        name="in_proj",
    )(x, p["ln_in_g"], p["ln_in_b"], p["w_in"], cos, sin, cos_t, sin_t, p["gq"], p["w_uq_t"],
      p["gkv"], p["w_uk"], p["w_uv_t"])


def _swa_blocks(sink_ref, qa_ref, kc_ref, vc_ref, kprev, vprev, first, o_ref, blocks_per_step):
    row = lax.broadcasted_iota(jnp.int32, (2 * BLOCK, BLOCK), 0) % BLOCK
    col = lax.broadcasted_iota(jnp.int32, (2 * BLOCK, BLOCK), 1)
    own = col <= row
    meta_ok = col >= jnp.where(first, N_PAD, 0)
    lane = lax.broadcasted_iota(jnp.int32, (1, LANES), 1)
    lane_kv0 = lane < SWA_HEAD_DIM
    top_rows = lax.broadcasted_iota(jnp.int32, (2 * BLOCK, 1), 0) < BLOCK

    for r in range(blocks_per_step):
        rows = slice(r * BLOCK, (r + 1) * BLOCK)
        if r == 0:
            kp, vp = kprev, vprev
        else:
            prev_rows = slice((r - 1) * BLOCK, r * BLOCK)
            kp, vp = kc_ref[0, prev_rows, :], vc_ref[0, prev_rows, :]
        kc, vc = kc_ref[0, rows, :], vc_ref[0, rows, :]
        for c in range(SWA_Q_COLS // LANES):
            qc = qa_ref[0, rows, c * LANES:(c + 1) * LANES]
            zero = jnp.zeros_like(qc)
            q2 = jnp.concatenate([jnp.where(lane_kv0, qc, zero),
                                  jnp.where(lane_kv0, zero, qc)], axis=0)
            s_prev = _dot_nt(q2, kp)
            if r == 0:
                s_prev = jnp.where(meta_ok, s_prev, NEG)
            s = jnp.where(own, _dot_nt(q2, kc), s_prev)
            sink = jnp.where(top_rows, sink_ref[c], sink_ref[c + 4])
            m = jnp.maximum(jnp.max(s, axis=-1, keepdims=True), sink)
            e = jnp.exp(s - m)
            denom = jnp.sum(e, axis=-1, keepdims=True) + jnp.exp(sink - m)
            pr = e * (1.0 / denom)
            o2 = (_dot(jnp.where(own, pr, 0.0).astype(BF16), vc)
                  + _dot(jnp.where(own, 0.0, pr).astype(BF16), vp))
            o = jnp.where(lane_kv0, o2[:BLOCK], o2[BLOCK:])
            o_ref[0, rows, c * LANES:(c + 1) * LANES] = o.astype(BF16)


def _in_proj_swa_kernel(sink_ref, x_ref, g_ref, b_ref, w_ref, cos_ref, sin_ref, cost_ref, sint_ref,
                        gq_ref, wuqt_ref, gkv_ref, wuk_ref, wuvt_ref, kmeta_ref, vmeta_ref,
                        a_ref, qt_ref, km_ref, vt_ref,
                        qa_s, ka_s, va_s, kprev_s, vprev_s):
    _in_proj_kernel(x_ref, g_ref, b_ref, w_ref, cos_ref, sin_ref, cost_ref, sint_ref, gq_ref,
                    wuqt_ref, gkv_ref, wuk_ref, wuvt_ref, qa_s, ka_s, va_s, qt_ref, km_ref, vt_ref)
    first = pl.program_id(1) == 0
    kprev = jnp.where(first, kmeta_ref[0], kprev_s[...])
    vprev = jnp.where(first, vmeta_ref[0], vprev_s[...])
    tm = qa_s.shape[1]
    _swa_blocks(sink_ref, qa_s, ka_s, va_s, kprev, vprev, first, a_ref, tm // BLOCK)
    kprev_s[...] = ka_s[0, tm - BLOCK:, :]
    vprev_s[...] = va_s[0, tm - BLOCK:, :]


def _in_proj_swa(x, tables, p, ka_meta, va_meta, tm):
    cos, sin, cos_t, sin_t = tables
    B, S, D = x.shape
    kvw = SWA_KV_COLS
    full = lambda shape: pl.BlockSpec(shape, lambda b, i: (0,) * len(shape))
    row = lambda w: pl.BlockSpec((1, tm, w), lambda b, i: (b, i, 0))
    out_shape = (
        jax.ShapeDtypeStruct((B, S, SWA_Q_COLS), BF16),
        jax.ShapeDtypeStruct((B, MLA_HEADS, MLA_QK_PAD, S), BF16),
        jax.ShapeDtypeStruct((B, MLA_HEADS, S, MLA_QK_PAD), BF16),
        jax.ShapeDtypeStruct((B, MLA_HEADS, S // tm, MLA_V_DIM, tm), BF16),
    )
    return pl.pallas_call(
        _in_proj_swa_kernel,
        grid=(B, S // tm),
        in_specs=[
            pl.BlockSpec(memory_space=pltpu.SMEM),
            row(D),
            full((1, D)), full((1, D)),
            full((D, IN_COLS_PADDED)),
            pl.BlockSpec((tm, LANES), lambda b, i: (i, 0)),
            pl.BlockSpec((tm, LANES), lambda b, i: (i, 0)),
            pl.BlockSpec((MLA_ROPE_DIM, tm), lambda b, i: (0, i)),
            pl.BlockSpec((MLA_ROPE_DIM, tm), lambda b, i: (0, i)),
            full((1, MLA_Q_RANK)), full((MLA_HEADS, MLA_QK_PAD, MLA_Q_RANK)),
            full((1, MLA_KV_RANK)), full((MLA_KV_RANK, MLA_HEADS * MLA_NOPE_DIM)),
            full((MLA_HEADS, MLA_V_DIM, MLA_KV_RANK)),
            full((1, BLOCK, kvw)), full((1, BLOCK, kvw)),
        ],
        out_specs=(row(SWA_Q_COLS),
                   pl.BlockSpec((1, MLA_HEADS, MLA_QK_PAD, tm), lambda b, i: (b, 0, 0, i)),
                   pl.BlockSpec((1, MLA_HEADS, tm, MLA_QK_PAD), lambda b, i: (b, 0, i, 0)),
                   pl.BlockSpec((1, MLA_HEADS, 1, MLA_V_DIM, tm), lambda b, i: (b, 0, i, 0, 0))),
        out_shape=out_shape,
        scratch_shapes=[pltpu.VMEM((1, tm, SWA_Q_COLS), BF16), pltpu.VMEM((1, tm, kvw), BF16),
                        pltpu.VMEM((1, tm, kvw), BF16), pltpu.VMEM((BLOCK, kvw), BF16),
                        pltpu.VMEM((BLOCK, kvw), BF16)],
        compiler_params=_cparams(("parallel", "arbitrary")),
        name="in_proj_swa",
    )(p["sinks"], x, p["ln_in_g"], p["ln_in_b"], p["w_in"], cos, sin, cos_t, sin_t, p["gq"],
      p["w_uq_t"], p["gkv"], p["w_uk"], p["w_uv_t"], ka_meta, va_meta)


def _mla_kernel(qt_ref, k_ref, vt_ref, kmeta_ref, vtmeta_ref, o_ref,
                s0a_ref, s0b_ref, s1a_ref, s1b_ref, smax_ref, m_ref, l_ref, acc_ref, *, tq):
    u = pl.program_id(2)
    qts = (qt_ref[0, 0, :, :tq], qt_ref[0, 0, :, tq:])
    bufs = ((s0a_ref, s0b_ref), (s1a_ref, s1b_ref))

    for g in range(2):
        s = _dot(kmeta_ref[0, 0], qts[g])
        m = jnp.max(s, axis=0, keepdims=True)
        p = jnp.exp2(s - m)
        m_ref[g] = m
        l_ref[g] = jnp.sum(p, axis=0, keepdims=True)
        acc_ref[g] = _dot(vtmeta_ref[0, 0, 0], p.astype(BF16))

    def scores_into(g, ab, j):
        start = pl.multiple_of(j * tq, tq)
        s = _dot(k_ref[0, 0, pl.ds(start, tq), :], qts[g])
        bufs[g][ab][...] = s
        smax_ref[g, ab] = jnp.max(s, axis=0, keepdims=True)

    def absorb(g, ab, j, causal=False):
        s = bufs[g][ab][...]
        if causal:
            key = lax.broadcasted_iota(jnp.int32, (tq, tq), 0)
            query = lax.broadcasted_iota(jnp.int32, (tq, tq), 1)
            s = jnp.where(key <= query, s, NEG)
            smax = jnp.max(s, axis=0, keepdims=True)
        else:
            smax = smax_ref[g, ab]
        m = m_ref[g]
        m_new = jnp.maximum(m, smax)
        alpha = jnp.exp2(m - m_new)
        p = jnp.exp2(s - m_new)
        m_ref[g] = m_new
        l_ref[g] = alpha * l_ref[g] + jnp.sum(p, axis=0, keepdims=True)
        acc_ref[g] = alpha * acc_ref[g] + _dot(vt_ref[0, 0, j], p.astype(BF16))

    buf_a, buf_b = 0, 1
    for g in range(2):
        scores_into(g, buf_a, 0)

    def pair(t, carry):
        scores_into(0, buf_b, 2 * t + 1)
        absorb(0, buf_a, 2 * t)
        scores_into(1, buf_b, 2 * t + 1)
        scores_into(0, buf_a, 2 * t + 2)
        absorb(1, buf_a, 2 * t)
        absorb(0, buf_b, 2 * t + 1)
        scores_into(1, buf_a, 2 * t + 2)
        absorb(1, buf_b, 2 * t + 1)
        return carry

    lax.fori_loop(0, u, pair, 0)

    scores_into(1, buf_b, 2 * u + 1)
    absorb(1, buf_a, 2 * u)
    absorb(0, buf_a, 2 * u, causal=True)
    absorb(1, buf_b, 2 * u + 1, causal=True)

    for g in range(2):
        o_ref[0, g * tq:(g + 1) * tq, :] = (acc_ref[g] * (1.0 / l_ref[g])).T.astype(BF16)


def _mla(qt, km, vt, km_meta, vt_meta, tq):
    B, H, _, S = qt.shape
    assert S % (2 * tq) == 0, "a step handles an (even, odd) pair of query tiles"
    return pl.pallas_call(
        functools.partial(_mla_kernel, tq=tq),
        grid=(B, H, S // (2 * tq)),
        in_specs=[
            pl.BlockSpec((1, 1, MLA_QK_PAD, 2 * tq), lambda b, h, u: (b, h, 0, u)),
            pl.BlockSpec((1, 1, S, MLA_QK_PAD), lambda b, h, u: (b, h, 0, 0)),
            pl.BlockSpec((1, 1, S // tq, MLA_V_DIM, tq), lambda b, h, u: (b, h, 0, 0, 0)),
            pl.BlockSpec((1, 1, N_META, MLA_QK_PAD), lambda b, h, u: (0, h, 0, 0)),
            pl.BlockSpec((1, 1, 1, MLA_V_DIM, N_META), lambda b, h, u: (0, h, 0, 0, 0)),
        ],
        out_specs=pl.BlockSpec((1, 2 * tq, MLA_V_DIM), lambda b, h, u: (b, u, h)),
        out_shape=jax.ShapeDtypeStruct((B, S, MLA_OUT_COLS), BF16),
        scratch_shapes=[pltpu.VMEM((tq, tq), F32), pltpu.VMEM((tq, tq), F32),
                        pltpu.VMEM((tq, tq), F32), pltpu.VMEM((tq, tq), F32),
                        pltpu.VMEM((2, 2, 1, tq), F32),
                        pltpu.VMEM((2, 1, tq), F32), pltpu.VMEM((2, 1, tq), F32),
                        pltpu.VMEM((2, MLA_V_DIM, tq), F32)],
        compiler_params=_cparams(("parallel", "parallel", "arbitrary")),
        name="mla_attn",
    )(qt, km, vt, km_meta, vt_meta)


def _out_proj_kernel(x_ref, a_ref, b_ref, gin_ref, bin_ref, ga_ref, gb_ref, wo_ref, g1_ref, b1_ref,
                     wrt_ref, brt_ref, h1_ref, route_ref, counts_out_ref, count_ref):
    h = _layer_norm(x_ref[0], gin_ref[...], bin_ref[...])
    a = a_ref[0].astype(F32)
    b = b_ref[0].astype(F32)
    an = (a * _rms_scale(a) * ga_ref[...]).astype(BF16)
    bn = (b * _rms_scale(b) * gb_ref[...]).astype(BF16)
    mix = _dot(jnp.concatenate([an, bn], axis=-1), wo_ref[...])
    h1 = _layer_norm(ALPHA * h + mix, g1_ref[...], b1_ref[...])
    h1_ref[0] = h1

    x_hi = h1.astype(BF16)
    x_lo = (h1 - x_hi.astype(F32)).astype(BF16)
    hi_both = _dot(x_hi, wrt_ref[...])
    logits = (hi_both[:, :ROUTE_LANES] + hi_both[:, ROUTE_LANES:]
              + _dot(x_lo, wrt_ref[:, :ROUTE_LANES]) + brt_ref[...])
    lane = lax.broadcasted_iota(jnp.int32, logits.shape, 1).astype(F32)
    ninf = -jnp.inf
    no_lane = float(ROUTE_LANES)
    gl = jnp.where(lane < N_GROUPS, logits, ninf)
    gmax = jnp.max(gl, axis=-1, keepdims=True)
    g_idx = jnp.min(jnp.where(gl == gmax, lane, no_lane), axis=-1, keepdims=True)
    lo = EXPERT_LANE0 + g_idx * EXPERTS_PER_GROUP
    el = jnp.where((lane >= lo) & (lane < lo + EXPERTS_PER_GROUP), logits, ninf)
    m1 = jnp.max(el, axis=-1, keepdims=True)
    i1 = jnp.min(jnp.where(el == m1, lane, no_lane), axis=-1, keepdims=True)
    el2 = jnp.where(lane == i1, ninf, el)
    m2 = jnp.max(el2, axis=-1, keepdims=True)
    i2 = jnp.min(jnp.where(el2 == m2, lane, no_lane), axis=-1, keepdims=True)

    a = jnp.minimum(i1, i2) - lo
    b = jnp.maximum(i1, i2) - lo
    cls = (g_idx * PAIRS_PER_GROUP + a * (2.0 * EXPERTS_PER_GROUP - 1.0 - a) * 0.5
           + (b - a - 1.0))

    @pl.when((pl.program_id(0) == 0) & (pl.program_id(1) == 0))
    def _():
        count_ref[...] = jnp.zeros_like(count_ref)

    tm = logits.shape[0]
    onehot = jnp.where(lane == cls, 1.0, 0.0)
    r_i = lax.broadcasted_iota(jnp.int32, (BLOCK, BLOCK), 0)
    c_i = lax.broadcasted_iota(jnp.int32, (BLOCK, BLOCK), 1)
    earlier = jnp.where(r_i > c_i, 1.0, 0.0).astype(BF16)
    running = count_ref[...]
    before = []
    for blk in range(tm // BLOCK):
        oh = onehot[blk * BLOCK:(blk + 1) * BLOCK]
        before.append(_dot(earlier, oh.astype(BF16)) + running)
        running = running + jnp.sum(oh, axis=0, keepdims=True)
    before = jnp.concatenate(before, axis=0)
    rank = jnp.sum(jnp.where(lane == cls, before, 0.0), axis=-1, keepdims=True)
    count_ref[...] = running
    counts_out_ref[...] = running

    route = jnp.where(lane == ROUTE_CLS, cls, 0.0)
    route = jnp.where(lane == ROUTE_RANK, rank, route)
    route_ref[0, 0] = route.T[:ROUTE_ROWS].astype(jnp.int32)


def _out_proj(x, a_out, b_out, p, tm):
    B, S, D = x.shape
    full = lambda shape: pl.BlockSpec(shape, lambda b, i: (0,) * len(shape))
    row = lambda w: pl.BlockSpec((1, tm, w), lambda b, i: (b, i, 0))
    return pl.pallas_call(
        _out_proj_kernel,
        grid=(B, S // tm),
        in_specs=[row(D), row(SWA_Q_COLS), row(MLA_OUT_COLS),
                  full((1, D)), full((1, D)), full((1, SWA_Q_COLS)), full((1, MLA_OUT_COLS)),
                  full((D, D)), full((1, D)), full((1, D)),
                  full((D, 2 * ROUTE_LANES)), full((1, ROUTE_LANES))],
        out_specs=(row(D), pl.BlockSpec((1, 1, ROUTE_ROWS, tm), lambda b, i: (b, i, 0, 0)),
                   full((1, ROUTE_LANES))),
        out_shape=(jax.ShapeDtypeStruct((B, S, D), F32),
                   jax.ShapeDtypeStruct((B, S // tm, ROUTE_ROWS, tm), jnp.int32),
                   jax.ShapeDtypeStruct((1, ROUTE_LANES), F32)),
        scratch_shapes=[pltpu.VMEM((1, ROUTE_LANES), F32)],
        compiler_params=_cparams(("arbitrary", "arbitrary")),
        name="out_proj_route",
    )(x, a_out, b_out, p["ln_in_g"], p["ln_in_b"], p["ga"], p["gb"], p["w_o"], p["ln1_g"],
      p["ln1_b"], p["w_route"], p["b_route"])


SLAB = D_MODEL // LANES


def _slab_copy(src_ref, src_tok8, dst_ref, dst_tok8, sem):
    return pltpu.make_async_copy(src_ref.at[pl.ds(pl.multiple_of(src_tok8, SLAB), SLAB)],
                                 dst_ref.at[pl.ds(pl.multiple_of(dst_tok8, SLAB), SLAB)], sem)


def _to_slabs(x, slab_ref, n):
    for s in range(SLAB):
        slab_ref[pl.ds(s, n, stride=SLAB), :] = x[:, s * LANES:(s + 1) * LANES]


def _from_slabs(slab_ref, n):
    return jnp.concatenate([slab_ref[pl.ds(s, n, stride=SLAB), :] for s in range(SLAB)], axis=1)


def _dispatch_kernel(padstart_ref, nu_ref, pos_ref, h1_ref, xs_ref, rows_ref, zero_ref, sem, *,
                     tm, n_tiles):
    tail = zero_ref.shape[0]

    def zero_fill(start):
        return pltpu.make_async_copy(
            zero_ref, xs_ref.at[pl.ds(pl.multiple_of(start, SLAB), tail)], sem)

    @pl.when(pl.program_id(0) == 0)
    def _():
        zero_ref[...] = jnp.zeros_like(zero_ref)
        for parity in range(2):
            tails = [zero_fill(padstart_ref[c]) for c in range(parity, N_CLASSES, 2)]
            for t in tails:
                t.start()
            for t in tails:
                t.wait()

        def fill_unused(j, carry):
            fill = zero_fill(j * tail)
            fill.start()
            fill.wait()
            return carry

        lax.fori_loop(nu_ref[0], n_tiles + 1, fill_unused, 0)

    _to_slabs(h1_ref[...], rows_ref, tm)

    def issue(r0, carry):
        for k in range(DMA_BURST):
            r = r0 * DMA_BURST + k
            _slab_copy(rows_ref, r * SLAB, xs_ref, pos_ref[0, 0, r], sem).start(priority=k % 2)
        return carry

    lax.fori_loop(0, tm // DMA_BURST, issue, 0)
    pltpu.make_async_copy(rows_ref, xs_ref.at[pl.ds(0, tm * SLAB)], sem).wait()


def _dispatch(h1, pos8, pad_start8, n_used, n_tiles, tm):
    T, D = h1.shape
    pos8 = pos8.reshape(T // tm, 1, tm)
    n_rows = (n_tiles + 1) * MOE_TILE_ROWS
    grid_spec = pltpu.PrefetchScalarGridSpec(
        num_scalar_prefetch=2,
        grid=(T // tm,),
        in_specs=[pl.BlockSpec((1, 1, tm), lambda t, ps, nu: (t, 0, 0),
                               memory_space=pltpu.SMEM),
                  pl.BlockSpec((tm, D), lambda t, ps, nu: (t, 0))],
        out_specs=pl.BlockSpec(memory_space=pl.ANY),
        scratch_shapes=[pltpu.VMEM((tm * SLAB, LANES), F32),
                        pltpu.VMEM((MOE_TILE_ROWS * SLAB, LANES), F32),
                        pltpu.SemaphoreType.DMA],
    )
    return pl.pallas_call(
        functools.partial(_dispatch_kernel, tm=tm, n_tiles=n_tiles),
        grid_spec=grid_spec,
        out_shape=jax.ShapeDtypeStruct((n_rows * SLAB, LANES), F32),
        compiler_params=_cparams(("arbitrary",)),
        name="moe_dispatch",
    )(pad_start8, n_used, pos8, h1)


def _experts_kernel(tlo_ref, thi_ref, nu_ref, x_ref, wg_lo_ref, wg_hi_ref, wu_lo_ref, wu_hi_ref,
                    wd_lo_ref, wd_hi_ref, wrt_ref, brt_ref, y_ref):
    j = pl.program_id(0)
    tr = MOE_TILE_ROWS

    @pl.when(j < nu_ref[0])
    def _():
        x = _from_slabs(x_ref, tr).astype(BF16)
        lo_e = tlo_ref[j]
        hi_e = thi_ref[j]
        logits = _dot(x, wrt_ref[...]) + brt_ref[...]
        lane = lax.broadcasted_iota(jnp.int32, logits.shape, 1)
        gl = jnp.where(lane < N_GROUPS, logits, -jnp.inf)
        ge = jnp.exp(gl - jnp.max(gl, axis=-1, keepdims=True))
        pick = lambda v, at: jnp.sum(jnp.where(lane == at, v, 0.0), axis=-1, keepdims=True)
        g_prob = pick(ge, lo_e // EXPERTS_PER_GROUP) / jnp.sum(ge, axis=-1, keepdims=True)
        e_lo = pick(logits, EXPERT_LANE0 + lo_e)
        e_hi = pick(logits, EXPERT_LANE0 + hi_e)
        gates = (g_prob / (1.0 + jnp.exp(e_hi - e_lo)), g_prob / (1.0 + jnp.exp(e_lo - e_hi)))

        y = None
        for gate, wg_ref, wu_ref, wd_ref in zip(gates, (wg_lo_ref, wg_hi_ref), (wu_lo_ref, wu_hi_ref),
                                                (wd_lo_ref, wd_hi_ref)):
            g = _dot(x, wg_ref[0])
            u = _dot(x, wu_ref[0])
            hid = g * (1.0 / (1.0 + jnp.exp(-g))) * u * gate
            y_k = _dot(hid.astype(BF16), wd_ref[0])
            y = y_k if y is None else y + y_k
        _to_slabs(y, y_ref, tr)

    @pl.when(j >= nu_ref[0])
    def _():
        y_ref[...] = jnp.zeros_like(y_ref)


def _experts(xs, tile_lo, tile_hi, n_used, p, n_tiles):
    D = D_MODEL
    tr = MOE_TILE_ROWS
    last_used = lambda j, nu: jnp.minimum(j, nu[0] - 1)
    wgu = lambda pick: pl.BlockSpec((1, D, D_EXPERT),
                                    lambda j, tlo, thi, nu: (pick(tlo, thi)[j], 0, 0))
    wd = lambda pick: pl.BlockSpec((1, D_EXPERT, D),
                                   lambda j, tlo, thi, nu: (pick(tlo, thi)[j], 0, 0))
    lo = lambda tlo, thi: tlo
    hi = lambda tlo, thi: thi
    grid_spec = pltpu.PrefetchScalarGridSpec(
        num_scalar_prefetch=3,
        grid=(n_tiles,),
        in_specs=[pl.BlockSpec((tr * SLAB, LANES), lambda j, tlo, thi, nu: (last_used(j, nu), 0)),
                  wgu(lo), wgu(hi), wgu(lo), wgu(hi), wd(lo), wd(hi),
                  pl.BlockSpec((D, ROUTE_LANES), lambda j, tlo, thi, nu: (0, 0)),
                  pl.BlockSpec((1, ROUTE_LANES), lambda j, tlo, thi, nu: (0, 0))],
        out_specs=pl.BlockSpec((tr * SLAB, LANES), lambda j, tlo, thi, nu: (j, 0)),
    )
    return pl.pallas_call(
        _experts_kernel,
        grid_spec=grid_spec,
        out_shape=jax.ShapeDtypeStruct((n_tiles * tr * SLAB, LANES), F32),
        compiler_params=_cparams(("arbitrary",)),
        name="moe_experts",
    )(tile_lo, tile_hi, n_used, xs, p["w_gate"], p["w_gate"], p["w_up"], p["w_up"],
      p["w_down"], p["w_down"],
      p["w_route"], p["b_route"])


def _combine_kernel(pos_ref, pos_next_ref, h1_ref, ys_ref, g2_ref, b2_ref, o_ref,
                    ybuf_ref, sem, *, tm):
    t = pl.program_id(0)
    slot = t % 2

    def gather(idx_ref, dst_slot):
        def issue(r0, carry):
            for k in range(DMA_BURST):
                r = r0 * DMA_BURST + k
                _slab_copy(ys_ref, idx_ref[0, 0, r], ybuf_ref.at[dst_slot], r * SLAB,
                           sem.at[dst_slot]).start(priority=k % 2)
            return carry

        lax.fori_loop(0, tm // DMA_BURST, issue, 0)

    @pl.when(t == 0)
    def _():
        gather(pos_ref, 0)

    @pl.when(t + 1 < pl.num_programs(0))
    def _():
        gather(pos_next_ref, 1 - slot)

    pltpu.make_async_copy(ys_ref.at[pl.ds(0, tm * SLAB)], ybuf_ref.at[slot], sem.at[slot]).wait()
    ffn = _from_slabs(ybuf_ref.at[slot], tm)
    o_ref[...] = _layer_norm(ALPHA * h1_ref[...] + ffn, g2_ref[...], b2_ref[...])


def _combine(h1, pos8, ys, p, tm):
    T, D = h1.shape
    n_steps = T // tm
    pos8 = pos8.reshape(n_steps, 1, tm)
    full = lambda shape: pl.BlockSpec(shape, lambda t: (0,) * len(shape))
    return pl.pallas_call(
        functools.partial(_combine_kernel, tm=tm),
        grid=(n_steps,),
        in_specs=[pl.BlockSpec((1, 1, tm), lambda t: (t, 0, 0), memory_space=pltpu.SMEM),
                  pl.BlockSpec((1, 1, tm), lambda t: (jnp.minimum(t + 1, n_steps - 1), 0, 0),
                               memory_space=pltpu.SMEM),
                  pl.BlockSpec((tm, D), lambda t: (t, 0)),
                  pl.BlockSpec(memory_space=pl.ANY),
                  full((1, D)), full((1, D))],
        out_specs=pl.BlockSpec((tm, D), lambda t: (t, 0)),
        out_shape=jax.ShapeDtypeStruct((T, D), F32),
        scratch_shapes=[pltpu.VMEM((2, tm * SLAB, LANES), F32),
                        pltpu.SemaphoreType.DMA((2,))],
        compiler_params=_cparams(("arbitrary",)),
        name="moe_combine_ln2",
    )(pos8, pos8, h1, ys, p["ln2_g"], p["ln2_b"])


def _dispatch_plan(route, counts):
    T = route.shape[0] * route.shape[1] * route.shape[3]
    tr = MOE_TILE_ROWS
    n_tiles = T // tr + N_CLASSES
    cnt = counts[0, :N_CLASSES].astype(jnp.int32)
    padded = jnp.maximum((cnt + tr - 1) // tr, 1) * tr
    classes = jnp.arange(N_CLASSES, dtype=jnp.int32)
    seg_end = jnp.sum(jnp.where(classes[None, :] <= classes[:, None], padded[None, :], 0), axis=1)
    seg_start = seg_end - padded
    cls = route[:, :, ROUTE_CLS, :].reshape(T)
    rank = route[:, :, ROUTE_RANK, :].reshape(T)
    start_of = jnp.sum(jnp.where(cls[:, None] == classes, seg_start, 0), axis=-1)
    pos8 = (start_of + rank) * SLAB
    tile_row0 = jnp.arange(n_tiles, dtype=jnp.int32) * tr
    tile_cls = jnp.sum((seg_end[None, :] <= tile_row0[:, None]).astype(jnp.int32), axis=-1)
    tile_cls = jnp.minimum(tile_cls, N_CLASSES - 1)
    pairs = [(a, b) for a in range(EXPERTS_PER_GROUP) for b in range(a + 1, EXPERTS_PER_GROUP)]
    lo_of = jnp.asarray([g * EXPERTS_PER_GROUP + a for g in range(N_GROUPS) for a, _ in pairs],
                        jnp.int32)
    hi_of = jnp.asarray([g * EXPERTS_PER_GROUP + b for g in range(N_GROUPS) for _, b in pairs],
                        jnp.int32)
    is_cls = tile_cls[:, None] == classes
    tile_lo = jnp.sum(jnp.where(is_cls, lo_of, 0), axis=-1)
    tile_hi = jnp.sum(jnp.where(is_cls, hi_of, 0), axis=-1)
    n_used = (seg_end[-1:] // tr).astype(jnp.int32)
    return pos8, (seg_start + cnt) * SLAB, tile_lo, tile_hi, n_used, n_tiles


def _rope_tables(pos):
    d = SWA_HEAD_DIM
    inv_freq = ROPE_THETA ** (-jnp.arange(0, d, 2, dtype=F32) / d)
    ang = pos[:, None] * inv_freq[None, :]
    c, s = jnp.cos(ang), jnp.sin(ang)
    return (jnp.concatenate([c, c, c, c], axis=-1), jnp.concatenate([-s, s, -s, s], axis=-1),
            jnp.concatenate([c, c], axis=-1).T, jnp.concatenate([s, s], axis=-1).T)


def _prep_params(ln_in_g, ln_in_b, w_in, swa_sinks, mla_q_norm_g, mla_w_uq, mla_kv_norm_g, mla_w_ukv,
                 swa_out_norm_g, mla_out_norm_g, w_o, ln1_g, ln1_b, moe_w_group, moe_b_group,
                 moe_w_router, moe_b_router, moe_w_gate, moe_w_up, moe_w_down, ln2_g, ln2_b):
    order = jnp.asarray(SWA_HEAD_ORDER)
    perm = (order[:, None] * SWA_HEAD_DIM + jnp.arange(SWA_HEAD_DIM)[None, :]).reshape(-1)
    w = w_in[0]
    w_q = w[:, :SWA_Q_COLS][:, perm]
    w_pad = jnp.zeros((D_MODEL, IN_COLS_PADDED - w.shape[1]), w.dtype)
    w_in_p = jnp.concatenate([w_q, w[:, SWA_Q_COLS:], w_pad], axis=1).astype(BF16)

    dqk = MLA_NOPE_DIM + MLA_ROPE_DIM
    wuq = mla_w_uq[0].reshape(MLA_Q_RANK, MLA_HEADS, dqk)
    wuq = jnp.pad(wuq, ((0, 0), (0, 0), (0, MLA_QK_PAD - dqk)))
    wuq_t = jnp.transpose(wuq, (1, 2, 0)).astype(BF16)
    wukv = mla_w_ukv[0].reshape(MLA_KV_RANK, MLA_HEADS, MLA_NOPE_DIM + MLA_V_DIM)
    wuk = wukv[:, :, :MLA_NOPE_DIM].reshape(MLA_KV_RANK, MLA_HEADS * MLA_NOPE_DIM).astype(BF16)
    wuv_t = jnp.transpose(wukv[:, :, MLA_NOPE_DIM:], (1, 2, 0)).astype(BF16)

    w_o0 = w_o[0]
    w_o_p = jnp.concatenate([w_o0[:SWA_Q_COLS][perm], w_o0[SWA_Q_COLS:]], axis=0).astype(BF16)

    w_route = jnp.zeros((D_MODEL, ROUTE_LANES), F32)
    w_route = w_route.at[:, :N_GROUPS].set(moe_w_group[0])
    w_route = w_route.at[:, EXPERT_LANE0:EXPERT_LANE0 + N_EXPERTS].set(moe_w_router[0])
    w_route_hi = w_route.astype(BF16)
    w_route_lo = (w_route - w_route_hi.astype(F32)).astype(BF16)
    w_route = jnp.concatenate([w_route_hi, w_route_lo], axis=1)
    b_route = jnp.zeros((1, ROUTE_LANES), F32)
    b_route = b_route.at[0, :N_GROUPS].set(moe_b_group[0])
    b_route = b_route.at[0, EXPERT_LANE0:EXPERT_LANE0 + N_EXPERTS].set(moe_b_router[0])

    return {
        "ln_in_g": ln_in_g[None, :], "ln_in_b": ln_in_b[None, :],
        "w_in": w_in_p,
        "sinks": swa_sinks[0],
        "gq": mla_q_norm_g, "w_uq_t": wuq_t,
        "gkv": mla_kv_norm_g, "w_uk": wuk, "w_uv_t": wuv_t,
        "ga": swa_out_norm_g[0][perm][None, :], "gb": mla_out_norm_g,
        "w_o": w_o_p, "ln1_g": ln1_g, "ln1_b": ln1_b,
        "w_route": w_route, "b_route": b_route,
        "w_gate": moe_w_gate[0].astype(BF16), "w_up": moe_w_up[0].astype(BF16),
        "w_down": moe_w_down[0].astype(BF16),
        "ln2_g": ln2_g, "ln2_b": ln2_b,
    }


def kernel(x, meta_tokens, ln_in_g, ln_in_b, w_in, swa_sinks, mla_q_norm_g, mla_w_uq, mla_kv_norm_g, mla_w_ukv, swa_out_norm_g, mla_out_norm_g, w_o, ln1_g, ln1_b, moe_w_group, moe_b_group, moe_w_router, moe_b_router, moe_w_gate, moe_w_up, moe_w_down, ln2_g, ln2_b):
    B, S, D = x.shape
    p = _prep_params(ln_in_g, ln_in_b, w_in, swa_sinks, mla_q_norm_g, mla_w_uq, mla_kv_norm_g,
                     mla_w_ukv, swa_out_norm_g, mla_out_norm_g, w_o, ln1_g, ln1_b, moe_w_group,
                     moe_b_group, moe_w_router, moe_b_router, moe_w_gate, moe_w_up, moe_w_down,
                     ln2_g, ln2_b)
    T = B * S
    tm = min(PROJ_TILE, S)

    x_meta = jnp.concatenate([jnp.zeros((N_PAD, D), x.dtype), meta_tokens.astype(x.dtype)])[None]
    pos_meta = jnp.maximum(jnp.arange(BLOCK) - N_PAD, 0).astype(F32)
    pos_tok = (jnp.arange(S) + N_META).astype(F32)
    _, ka_m, va_m, _, km_m, vt_m = _in_proj(x_meta, _rope_tables(pos_meta), p, BLOCK)
    a_out, qt, km, vt = _in_proj_swa(x, _rope_tables(pos_tok), p, ka_m, va_m, tm)
    b_out = _mla(qt, km, vt, km_m[:, :, N_PAD:, :], vt_m[..., N_PAD:], tm)
    h1, route, counts = _out_proj(x, a_out, b_out, p, min(WIDE_TILE, S))
    h1 = h1.reshape(B * S, D)

    pos8, pad_start8, tile_lo, tile_hi, n_used, n_tiles = _dispatch_plan(route, counts)
    xs = _dispatch(h1, pos8, pad_start8, n_used, n_tiles, min(DISPATCH_TILE, T))
    ys = _experts(xs, tile_lo, tile_hi, n_used, p, n_tiles)
    out = _combine(h1, pos8, ys, p, min(COMBINE_TILE, T))
    return out.reshape(B, S, D)
```

```python
import functools

import jax
import jax.numpy as jnp
from jax import lax
from jax.experimental import pallas as pl
from jax.experimental.pallas import tpu as pltpu

D_MODEL = 1024
N_META = 16
BLOCK = 128
N_PAD = BLOCK - N_META
ROPE_THETA = 10000.0
SWA_HEADS = 8
SWA_KV_HEADS = 2
SWA_HEAD_DIM = 64
MLA_HEADS = 4
MLA_Q_RANK = 256
MLA_KV_RANK = 256
MLA_NOPE_DIM = 128
MLA_ROPE_DIM = 64
MLA_V_DIM = 128
SWA_Q_COLS = SWA_HEADS * SWA_HEAD_DIM
SWA_KV_COLS = SWA_KV_HEADS * SWA_HEAD_DIM
MLA_OUT_COLS = MLA_HEADS * MLA_V_DIM
N_GROUPS = 4
EXPERTS_PER_GROUP = 8
N_EXPERTS = N_GROUPS * EXPERTS_PER_GROUP
D_EXPERT = 256
LN_EPS = 1e-5
RMS_EPS = 1e-6
DEPTH = 1
ALPHA = (2.0 * DEPTH) ** 0.25
NEG = -1e30
LOG2_E = 1.4426950408889634

LANES = 128
COL_KA = SWA_Q_COLS
COL_VA = COL_KA + SWA_KV_COLS
COL_CQ = COL_VA + SWA_KV_COLS
COL_CKV = COL_CQ + MLA_Q_RANK
COL_KR = COL_CKV + MLA_KV_RANK
IN_COLS_PADDED = COL_KR + LANES
MLA_QK_PAD = 256
ROUTE_LANES = 128
EXPERT_LANE0 = 32
ROUTE_CLS, ROUTE_RANK = range(2)
ROUTE_ROWS = 8
PAIRS_PER_GROUP = EXPERTS_PER_GROUP * (EXPERTS_PER_GROUP - 1) // 2
N_CLASSES = N_GROUPS * PAIRS_PER_GROUP
MOE_TILE_ROWS = 256
PROJ_TILE = 1024
WIDE_TILE = 1024
DISPATCH_TILE = 2048
COMBINE_TILE = 1024
DMA_BURST = 8
VMEM_LIMIT = 56 * 1024 * 1024

SWA_HEAD_ORDER = (0, 4, 1, 5, 2, 6, 3, 7)

BF16 = jnp.bfloat16
F32 = jnp.float32


def _cparams(sem):
    return pltpu.CompilerParams(dimension_semantics=sem, vmem_limit_bytes=VMEM_LIMIT)


def _layer_norm(x, g, b):
    mu = jnp.mean(x, axis=-1, keepdims=True)
    xc = x - mu
    var = jnp.mean(xc * xc, axis=-1, keepdims=True)
    return xc * lax.rsqrt(var + LN_EPS) * g + b


def _rms_scale(x):
    return lax.rsqrt(jnp.mean(x * x, axis=-1, keepdims=True) + RMS_EPS)


def _rope128(x, cos, sin_signed, lane_lo):
    up = pltpu.roll(x, LANES - 32, 1)
    dn = pltpu.roll(x, 32, 1)
    return x * cos + jnp.where(lane_lo, up, dn) * sin_signed


def _dot(a, b):
    return jnp.dot(a, b, preferred_element_type=F32)


def _dot_nt(a, b):
    return lax.dot_general(a, b, (((1,), (1,)), ((), ())), preferred_element_type=F32)


def _in_proj_kernel(x_ref, g_ref, b_ref, w_ref, cos_ref, sin_ref, cost_ref, sint_ref, gq_ref,
                    wuqt_ref, gkv_ref, wuk_ref, wuvt_ref,
                    qa_ref, ka_ref, va_ref, qt_ref, km_ref, vt_ref):
    h = _layer_norm(x_ref[0], g_ref[...], b_ref[...])
    u = _dot(h.astype(BF16), w_ref[...])
    cos = cos_ref[...]
    sin = sin_ref[...]
    lane = lax.broadcasted_iota(jnp.int32, (1, LANES), 1)
    lane_lo = (lane % 64) < 32
    rope = functools.partial(_rope128, cos=cos, sin_signed=sin, lane_lo=lane_lo)

    swa_scale = SWA_HEAD_DIM ** -0.5
    for c in range(SWA_Q_COLS // LANES):
        qc = rope(u[:, c * LANES:(c + 1) * LANES]) * swa_scale
        qa_ref[0, :, c * LANES:(c + 1) * LANES] = qc.astype(BF16)
    ka_ref[0] = rope(u[:, COL_KA:COL_VA]).astype(BF16)
    va_ref[0] = u[:, COL_VA:COL_CQ].astype(BF16)

    cq = u[:, COL_CQ:COL_CKV]
    cqn = (cq * _rms_scale(cq) * gq_ref[...]).astype(BF16)
    ckv = u[:, COL_CKV:COL_KR]
    ckvn = (ckv * _rms_scale(ckv) * gkv_ref[...]).astype(BF16)
    k_nope = _dot(ckvn, wuk_ref[...])
    kr = rope(u[:, COL_KR:IN_COLS_PADDED]).astype(BF16)
    q_scale = (MLA_NOPE_DIM + MLA_ROPE_DIM) ** -0.5 * LOG2_E
    cost = cost_ref[...]
    sint = sint_ref[...]
    half = MLA_ROPE_DIM // 2
    for hd in range(MLA_HEADS):
        qt = _dot_nt(wuqt_ref[hd], cqn)
        qr = qt[MLA_NOPE_DIM:MLA_NOPE_DIM + MLA_ROPE_DIM]
        rot = jnp.concatenate([-qr[half:], qr[:half]], axis=0)
        qt_ref[0, hd, 0:MLA_NOPE_DIM, :] = (qt[:MLA_NOPE_DIM] * q_scale).astype(BF16)
        qt_ref[0, hd, MLA_NOPE_DIM:MLA_NOPE_DIM + MLA_ROPE_DIM, :] = (
            (qr * cost + rot * sint) * q_scale).astype(BF16)
        qt_ref[0, hd, MLA_NOPE_DIM + MLA_ROPE_DIM:, :] = (
            qt[MLA_NOPE_DIM + MLA_ROPE_DIM:]).astype(BF16)
        km_ref[0, hd, :, 0:LANES] = k_nope[:, hd * LANES:(hd + 1) * LANES].astype(BF16)
        km_ref[0, hd, :, LANES:2 * LANES] = kr
        vt_ref[0, hd, 0] = _dot_nt(wuvt_ref[hd], ckvn).astype(BF16)


def _in_proj(x, tables, p, tm):
    cos, sin, cos_t, sin_t = tables
    B, S, D = x.shape
    grid = (B, S // tm)
    full = lambda shape: pl.BlockSpec(shape, lambda b, i: (0,) * len(shape))
    out_shape = (
        jax.ShapeDtypeStruct((B, S, SWA_Q_COLS), BF16),
        jax.ShapeDtypeStruct((B, S, SWA_KV_COLS), BF16),
        jax.ShapeDtypeStruct((B, S, SWA_KV_COLS), BF16),
        jax.ShapeDtypeStruct((B, MLA_HEADS, MLA_QK_PAD, S), BF16),
        jax.ShapeDtypeStruct((B, MLA_HEADS, S, MLA_QK_PAD), BF16),
        jax.ShapeDtypeStruct((B, MLA_HEADS, S // tm, MLA_V_DIM, tm), BF16),
    )
    row = lambda w: pl.BlockSpec((1, tm, w), lambda b, i: (b, i, 0))
    head = lambda w: pl.BlockSpec((1, MLA_HEADS, tm, w), lambda b, i: (b, 0, i, 0))
    return pl.pallas_call(
        _in_proj_kernel,
        grid=grid,
        in_specs=[
            row(D),
            full((1, D)), full((1, D)),
            full((D, IN_COLS_PADDED)),
            pl.BlockSpec((tm, LANES), lambda b, i: (i, 0)),
            pl.BlockSpec((tm, LANES), lambda b, i: (i, 0)),
            pl.BlockSpec((MLA_ROPE_DIM, tm), lambda b, i: (0, i)),
            pl.BlockSpec((MLA_ROPE_DIM, tm), lambda b, i: (0, i)),
            full((1, MLA_Q_RANK)), full((MLA_HEADS, MLA_QK_PAD, MLA_Q_RANK)),
            full((1, MLA_KV_RANK)), full((MLA_KV_RANK, MLA_HEADS * MLA_NOPE_DIM)),
            full((MLA_HEADS, MLA_V_DIM, MLA_KV_RANK)),
        ],
        out_specs=(row(SWA_Q_COLS), row(SWA_KV_COLS), row(SWA_KV_COLS),
                   pl.BlockSpec((1, MLA_HEADS, MLA_QK_PAD, tm), lambda b, i: (b, 0, 0, i)),
                   head(MLA_QK_PAD),
                   pl.BlockSpec((1, MLA_HEADS, 1, MLA_V_DIM, tm), lambda b, i: (b, 0, i, 0, 0))),
        out_shape=out_shape,
        compiler_params=_cparams(("parallel", "parallel")),
        name="in_proj",
    )(x, p["ln_in_g"], p["ln_in_b"], p["w_in"], cos, sin, cos_t, sin_t, p["gq"], p["w_uq_t"],
      p["gkv"], p["w_uk"], p["w_uv_t"])


def _swa_blocks(sink_ref, qa_ref, kc_ref, vc_ref, kprev, vprev, first, o_ref, blocks_per_step):
    row = lax.broadcasted_iota(jnp.int32, (2 * BLOCK, BLOCK), 0) % BLOCK
    col = lax.broadcasted_iota(jnp.int32, (2 * BLOCK, BLOCK), 1)
    own = col <= row
    meta_ok = col >= jnp.where(first, N_PAD, 0)
    lane = lax.broadcasted_iota(jnp.int32, (1, LANES), 1)
    lane_kv0 = lane < SWA_HEAD_DIM
    top_rows = lax.broadcasted_iota(jnp.int32, (2 * BLOCK, 1), 0) < BLOCK

    for r in range(blocks_per_step):
        rows = slice(r * BLOCK, (r + 1) * BLOCK)
        if r == 0:
            kp, vp = kprev, vprev
        else:
            prev_rows = slice((r - 1) * BLOCK, r * BLOCK)
            kp, vp = kc_ref[0, prev_rows, :], vc_ref[0, prev_rows, :]
        kc, vc = kc_ref[0, rows, :], vc_ref[0, rows, :]
        for c in range(SWA_Q_COLS // LANES):
            qc = qa_ref[0, rows, c * LANES:(c + 1) * LANES]
            zero = jnp.zeros_like(qc)
            q2 = jnp.concatenate([jnp.where(lane_kv0, qc, zero),
                                  jnp.where(lane_kv0, zero, qc)], axis=0)
            s_prev = _dot_nt(q2, kp)
            if r == 0:
                s_prev = jnp.where(meta_ok, s_prev, NEG)
            s = jnp.where(own, _dot_nt(q2, kc), s_prev)
            sink = jnp.where(top_rows, sink_ref[c], sink_ref[c + 4])
            m = jnp.maximum(jnp.max(s, axis=-1, keepdims=True), sink)
            e = jnp.exp(s - m)
            denom = jnp.sum(e, axis=-1, keepdims=True) + jnp.exp(sink - m)
            pr = e * (1.0 / denom)
            o2 = (_dot(jnp.where(own, pr, 0.0).astype(BF16), vc)
                  + _dot(jnp.where(own, 0.0, pr).astype(BF16), vp))
            o = jnp.where(lane_kv0, o2[:BLOCK], o2[BLOCK:])
            o_ref[0, rows, c * LANES:(c + 1) * LANES] = o.astype(BF16)


def _in_proj_swa_kernel(sink_ref, x_ref, g_ref, b_ref, w_ref, cos_ref, sin_ref, cost_ref, sint_ref,
                        gq_ref, wuqt_ref, gkv_ref, wuk_ref, wuvt_ref, kmeta_ref, vmeta_ref,
                        a_ref, qt_ref, km_ref, vt_ref,
                        qa_s, ka_s, va_s, kprev_s, vprev_s):
    _in_proj_kernel(x_ref, g_ref, b_ref, w_ref, cos_ref, sin_ref, cost_ref, sint_ref, gq_ref,
                    wuqt_ref, gkv_ref, wuk_ref, wuvt_ref, qa_s, ka_s, va_s, qt_ref, km_ref, vt_ref)
    first = pl.program_id(1) == 0
    kprev = jnp.where(first, kmeta_ref[0], kprev_s[...])
    vprev = jnp.where(first, vmeta_ref[0], vprev_s[...])
    tm = qa_s.shape[1]
    _swa_blocks(sink_ref, qa_s, ka_s, va_s, kprev, vprev, first, a_ref, tm // BLOCK)
    kprev_s[...] = ka_s[0, tm - BLOCK:, :]
    vprev_s[...] = va_s[0, tm - BLOCK:, :]


def _in_proj_swa(x, tables, p, ka_meta, va_meta, tm):
    cos, sin, cos_t, sin_t = tables
    B, S, D = x.shape
    kvw = SWA_KV_COLS
    full = lambda shape: pl.BlockSpec(shape, lambda b, i: (0,) * len(shape))
    row = lambda w: pl.BlockSpec((1, tm, w), lambda b, i: (b, i, 0))
    out_shape = (
        jax.ShapeDtypeStruct((B, S, SWA_Q_COLS), BF16),
        jax.ShapeDtypeStruct((B, MLA_HEADS, MLA_QK_PAD, S), BF16),
        jax.ShapeDtypeStruct((B, MLA_HEADS, S, MLA_QK_PAD), BF16),
        jax.ShapeDtypeStruct((B, MLA_HEADS, S // tm, MLA_V_DIM, tm), BF16),
    )
    return pl.pallas_call(
        _in_proj_swa_kernel,
        grid=(B, S // tm),
        in_specs=[
            pl.BlockSpec(memory_space=pltpu.SMEM),
            row(D),
            full((1, D)), full((1, D)),
            full((D, IN_COLS_PADDED)),
            pl.BlockSpec((tm, LANES), lambda b, i: (i, 0)),
            pl.BlockSpec((tm, LANES), lambda b, i: (i, 0)),
            pl.BlockSpec((MLA_ROPE_DIM, tm), lambda b, i: (0, i)),
            pl.BlockSpec((MLA_ROPE_DIM, tm), lambda b, i: (0, i)),
            full((1, MLA_Q_RANK)), full((MLA_HEADS, MLA_QK_PAD, MLA_Q_RANK)),
            full((1, MLA_KV_RANK)), full((MLA_KV_RANK, MLA_HEADS * MLA_NOPE_DIM)),
            full((MLA_HEADS, MLA_V_DIM, MLA_KV_RANK)),
            full((1, BLOCK, kvw)), full((1, BLOCK, kvw)),
        ],
        out_specs=(row(SWA_Q_COLS),
                   pl.BlockSpec((1, MLA_HEADS, MLA_QK_PAD, tm), lambda b, i: (b, 0, 0, i)),
                   pl.BlockSpec((1, MLA_HEADS, tm, MLA_QK_PAD), lambda b, i: (b, 0, i, 0)),
                   pl.BlockSpec((1, MLA_HEADS, 1, MLA_V_DIM, tm), lambda b, i: (b, 0, i, 0, 0))),
        out_shape=out_shape,
        scratch_shapes=[pltpu.VMEM((1, tm, SWA_Q_COLS), BF16), pltpu.VMEM((1, tm, kvw), BF16),
                        pltpu.VMEM((1, tm, kvw), BF16), pltpu.VMEM((BLOCK, kvw), BF16),
                        pltpu.VMEM((BLOCK, kvw), BF16)],
        compiler_params=_cparams(("parallel", "arbitrary")),
        name="in_proj_swa",
    )(p["sinks"], x, p["ln_in_g"], p["ln_in_b"], p["w_in"], cos, sin, cos_t, sin_t, p["gq"],
      p["w_uq_t"], p["gkv"], p["w_uk"], p["w_uv_t"], ka_meta, va_meta)


def _mla_kernel(qt_ref, k_ref, vt_ref, kmeta_ref, vtmeta_ref, o_ref,
                s0a_ref, s0b_ref, s1a_ref, s1b_ref, smax_ref, m_ref, l_ref, acc_ref, *, tq):
    u = pl.program_id(2)
    qts = (qt_ref[0, 0, :, :tq], qt_ref[0, 0, :, tq:])
    bufs = ((s0a_ref, s0b_ref), (s1a_ref, s1b_ref))

    for g in range(2):
        s = _dot(kmeta_ref[0, 0], qts[g])
        m = jnp.max(s, axis=0, keepdims=True)
        p = jnp.exp2(s - m)
        m_ref[g] = m
        l_ref[g] = jnp.sum(p, axis=0, keepdims=True)
        acc_ref[g] = _dot(vtmeta_ref[0, 0, 0], p.astype(BF16))

    def scores_into(g, ab, j):
        start = pl.multiple_of(j * tq, tq)
        s = _dot(k_ref[0, 0, pl.ds(start, tq), :], qts[g])
        bufs[g][ab][...] = s
        smax_ref[g, ab] = jnp.max(s, axis=0, keepdims=True)

    def absorb(g, ab, j, causal=False):
        s = bufs[g][ab][...]
        if causal:
            key = lax.broadcasted_iota(jnp.int32, (tq, tq), 0)
            query = lax.broadcasted_iota(jnp.int32, (tq, tq), 1)
            s = jnp.where(key <= query, s, NEG)
            smax = jnp.max(s, axis=0, keepdims=True)
        else:
            smax = smax_ref[g, ab]
        m = m_ref[g]
        m_new = jnp.maximum(m, smax)
        alpha = jnp.exp2(m - m_new)
        p = jnp.exp2(s - m_new)
        m_ref[g] = m_new
        l_ref[g] = alpha * l_ref[g] + jnp.sum(p, axis=0, keepdims=True)
        acc_ref[g] = alpha * acc_ref[g] + _dot(vt_ref[0, 0, j], p.astype(BF16))

    buf_a, buf_b = 0, 1
    for g in range(2):
        scores_into(g, buf_a, 0)

    def pair(t, carry):
        scores_into(0, buf_b, 2 * t + 1)
        absorb(0, buf_a, 2 * t)
        scores_into(1, buf_b, 2 * t + 1)
        scores_into(0, buf_a, 2 * t + 2)
        absorb(1, buf_a, 2 * t)
        absorb(0, buf_b, 2 * t + 1)
        scores_into(1, buf_a, 2 * t + 2)
        absorb(1, buf_b, 2 * t + 1)
        return carry

    lax.fori_loop(0, u, pair, 0)

    scores_into(1, buf_b, 2 * u + 1)
    absorb(1, buf_a, 2 * u)
    absorb(0, buf_a, 2 * u, causal=True)
    absorb(1, buf_b, 2 * u + 1, causal=True)

    for g in range(2):
        o_ref[0, g * tq:(g + 1) * tq, :] = (acc_ref[g] * (1.0 / l_ref[g])).T.astype(BF16)


def _mla(qt, km, vt, km_meta, vt_meta, tq):
    B, H, _, S = qt.shape
    assert S % (2 * tq) == 0, "a step handles an (even, odd) pair of query tiles"
    return pl.pallas_call(
        functools.partial(_mla_kernel, tq=tq),
        grid=(B, H, S // (2 * tq)),
        in_specs=[
            pl.BlockSpec((1, 1, MLA_QK_PAD, 2 * tq), lambda b, h, u: (b, h, 0, u)),
            pl.BlockSpec((1, 1, S, MLA_QK_PAD), lambda b, h, u: (b, h, 0, 0)),
            pl.BlockSpec((1, 1, S // tq, MLA_V_DIM, tq), lambda b, h, u: (b, h, 0, 0, 0)),
            pl.BlockSpec((1, 1, N_META, MLA_QK_PAD), lambda b, h, u: (0, h, 0, 0)),
            pl.BlockSpec((1, 1, 1, MLA_V_DIM, N_META), lambda b, h, u: (0, h, 0, 0, 0)),
        ],
        out_specs=pl.BlockSpec((1, 2 * tq, MLA_V_DIM), lambda b, h, u: (b, u, h)),
        out_shape=jax.ShapeDtypeStruct((B, S, MLA_OUT_COLS), BF16),
        scratch_shapes=[pltpu.VMEM((tq, tq), F32), pltpu.VMEM((tq, tq), F32),
                        pltpu.VMEM((tq, tq), F32), pltpu.VMEM((tq, tq), F32),
                        pltpu.VMEM((2, 2, 1, tq), F32),
                        pltpu.VMEM((2, 1, tq), F32), pltpu.VMEM((2, 1, tq), F32),
                        pltpu.VMEM((2, MLA_V_DIM, tq), F32)],
        compiler_params=_cparams(("parallel", "parallel", "arbitrary")),
        name="mla_attn",
    )(qt, km, vt, km_meta, vt_meta)


def _out_proj_kernel(x_ref, a_ref, b_ref, gin_ref, bin_ref, ga_ref, gb_ref, wo_ref, g1_ref, b1_ref,
                     wrt_ref, brt_ref, h1_ref, route_ref, counts_out_ref, count_ref):
    h = _layer_norm(x_ref[0], gin_ref[...], bin_ref[...])
    a = a_ref[0].astype(F32)
    b = b_ref[0].astype(F32)
    an = (a * _rms_scale(a) * ga_ref[...]).astype(BF16)
    bn = (b * _rms_scale(b) * gb_ref[...]).astype(BF16)
    mix = _dot(jnp.concatenate([an, bn], axis=-1), wo_ref[...])
    h1 = _layer_norm(ALPHA * h + mix, g1_ref[...], b1_ref[...])
    h1_ref[0] = h1

    x_hi = h1.astype(BF16)
    x_lo = (h1 - x_hi.astype(F32)).astype(BF16)
    hi_both = _dot(x_hi, wrt_ref[...])
    logits = (hi_both[:, :ROUTE_LANES] + hi_both[:, ROUTE_LANES:]
              + _dot(x_lo, wrt_ref[:, :ROUTE_LANES]) + brt_ref[...])
    lane = lax.broadcasted_iota(jnp.int32, logits.shape, 1).astype(F32)
    ninf = -jnp.inf
    no_lane = float(ROUTE_LANES)
    gl = jnp.where(lane < N_GROUPS, logits, ninf)
    gmax = jnp.max(gl, axis=-1, keepdims=True)
    g_idx = jnp.min(jnp.where(gl == gmax, lane, no_lane), axis=-1, keepdims=True)
    lo = EXPERT_LANE0 + g_idx * EXPERTS_PER_GROUP
    el = jnp.where((lane >= lo) & (lane < lo + EXPERTS_PER_GROUP), logits, ninf)
    m1 = jnp.max(el, axis=-1, keepdims=True)
    i1 = jnp.min(jnp.where(el == m1, lane, no_lane), axis=-1, keepdims=True)
    el2 = jnp.where(lane == i1, ninf, el)
    m2 = jnp.max(el2, axis=-1, keepdims=True)
    i2 = jnp.min(jnp.where(el2 == m2, lane, no_lane), axis=-1, keepdims=True)

    a = jnp.minimum(i1, i2) - lo
    b = jnp.maximum(i1, i2) - lo
    cls = (g_idx * PAIRS_PER_GROUP + a * (2.0 * EXPERTS_PER_GROUP - 1.0 - a) * 0.5
           + (b - a - 1.0))

    @pl.when((pl.program_id(0) == 0) & (pl.program_id(1) == 0))
    def _():
        count_ref[...] = jnp.zeros_like(count_ref)

    tm = logits.shape[0]
    onehot = jnp.where(lane == cls, 1.0, 0.0)
    r_i = lax.broadcasted_iota(jnp.int32, (BLOCK, BLOCK), 0)
    c_i = lax.broadcasted_iota(jnp.int32, (BLOCK, BLOCK), 1)
    earlier = jnp.where(r_i > c_i, 1.0, 0.0).astype(BF16)
    running = count_ref[...]
    before = []
    for blk in range(tm // BLOCK):
        oh = onehot[blk * BLOCK:(blk + 1) * BLOCK]
        before.append(_dot(earlier, oh.astype(BF16)) + running)
        running = running + jnp.sum(oh, axis=0, keepdims=True)
    before = jnp.concatenate(before, axis=0)
    rank = jnp.sum(jnp.where(lane == cls, before, 0.0), axis=-1, keepdims=True)
    count_ref[...] = running
    counts_out_ref[...] = running

    route = jnp.where(lane == ROUTE_CLS, cls, 0.0)
    route = jnp.where(lane == ROUTE_RANK, rank, route)
    route_ref[0, 0] = route.T[:ROUTE_ROWS].astype(jnp.int32)


def _out_proj(x, a_out, b_out, p, tm):
    B, S, D = x.shape
    full = lambda shape: pl.BlockSpec(shape, lambda b, i: (0,) * len(shape))
    row = lambda w: pl.BlockSpec((1, tm, w), lambda b, i: (b, i, 0))
    return pl.pallas_call(
        _out_proj_kernel,
        grid=(B, S // tm),
        in_specs=[row(D), row(SWA_Q_COLS), row(MLA_OUT_COLS),
                  full((1, D)), full((1, D)), full((1, SWA_Q_COLS)), full((1, MLA_OUT_COLS)),
                  full((D, D)), full((1, D)), full((1, D)),
                  full((D, 2 * ROUTE_LANES)), full((1, ROUTE_LANES))],
        out_specs=(row(D), pl.BlockSpec((1, 1, ROUTE_ROWS, tm), lambda b, i: (b, i, 0, 0)),
                   full((1, ROUTE_LANES))),
        out_shape=(jax.ShapeDtypeStruct((B, S, D), F32),
                   jax.ShapeDtypeStruct((B, S // tm, ROUTE_ROWS, tm), jnp.int32),
                   jax.ShapeDtypeStruct((1, ROUTE_LANES), F32)),
        scratch_shapes=[pltpu.VMEM((1, ROUTE_LANES), F32)],
        compiler_params=_cparams(("arbitrary", "arbitrary")),
        name="out_proj_route",
    )(x, a_out, b_out, p["ln_in_g"], p["ln_in_b"], p["ga"], p["gb"], p["w_o"], p["ln1_g"],
      p["ln1_b"], p["w_route"], p["b_route"])


SLAB = D_MODEL // LANES


def _slab_copy(src_ref, src_tok8, dst_ref, dst_tok8, sem):
    return pltpu.make_async_copy(src_ref.at[pl.ds(pl.multiple_of(src_tok8, SLAB), SLAB)],
                                 dst_ref.at[pl.ds(pl.multiple_of(dst_tok8, SLAB), SLAB)], sem)


def _to_slabs(x, slab_ref, n):
    for s in range(SLAB):
        slab_ref[pl.ds(s, n, stride=SLAB), :] = x[:, s * LANES:(s + 1) * LANES]


def _from_slabs(slab_ref, n):
    return jnp.concatenate([slab_ref[pl.ds(s, n, stride=SLAB), :] for s in range(SLAB)], axis=1)


def _dispatch_kernel(padstart_ref, nu_ref, pos_ref, h1_ref, xs_ref, rows_ref, zero_ref, sem, *,
                     tm, n_tiles):
    tail = zero_ref.shape[0]

    def zero_fill(start):
        return pltpu.make_async_copy(
            zero_ref, xs_ref.at[pl.ds(pl.multiple_of(start, SLAB), tail)], sem)

    @pl.when(pl.program_id(0) == 0)
    def _():
        zero_ref[...] = jnp.zeros_like(zero_ref)
        for parity in range(2):
            tails = [zero_fill(padstart_ref[c]) for c in range(parity, N_CLASSES, 2)]
            for t in tails:
                t.start()
            for t in tails:
                t.wait()

        def fill_unused(j, carry):
            fill = zero_fill(j * tail)
            fill.start()
            fill.wait()
            return carry

        lax.fori_loop(nu_ref[0], n_tiles + 1, fill_unused, 0)

    _to_slabs(h1_ref[...], rows_ref, tm)

    def issue(r0, carry):
        for k in range(DMA_BURST):
            r = r0 * DMA_BURST + k
            _slab_copy(rows_ref, r * SLAB, xs_ref, pos_ref[0, 0, r], sem).start(priority=k % 2)
        return carry

    lax.fori_loop(0, tm // DMA_BURST, issue, 0)
    pltpu.make_async_copy(rows_ref, xs_ref.at[pl.ds(0, tm * SLAB)], sem).wait()


def _dispatch(h1, pos8, pad_start8, n_used, n_tiles, tm):
    T, D = h1.shape
    pos8 = pos8.reshape(T // tm, 1, tm)
    n_rows = (n_tiles + 1) * MOE_TILE_ROWS
    grid_spec = pltpu.PrefetchScalarGridSpec(
        num_scalar_prefetch=2,
        grid=(T // tm,),
        in_specs=[pl.BlockSpec((1, 1, tm), lambda t, ps, nu: (t, 0, 0),
                               memory_space=pltpu.SMEM),
                  pl.BlockSpec((tm, D), lambda t, ps, nu: (t, 0))],
        out_specs=pl.BlockSpec(memory_space=pl.ANY),
        scratch_shapes=[pltpu.VMEM((tm * SLAB, LANES), F32),
                        pltpu.VMEM((MOE_TILE_ROWS * SLAB, LANES), F32),
                        pltpu.SemaphoreType.DMA],
    )
    return pl.pallas_call(
        functools.partial(_dispatch_kernel, tm=tm, n_tiles=n_tiles),
        grid_spec=grid_spec,
        out_shape=jax.ShapeDtypeStruct((n_rows * SLAB, LANES), F32),
        compiler_params=_cparams(("arbitrary",)),
        name="moe_dispatch",
    )(pad_start8, n_used, pos8, h1)


def _experts_kernel(tlo_ref, thi_ref, nu_ref, x_ref, wg_lo_ref, wg_hi_ref, wu_lo_ref, wu_hi_ref,
                    wd_lo_ref, wd_hi_ref, wrt_ref, brt_ref, y_ref):
    j = pl.program_id(0)
    tr = MOE_TILE_ROWS

    @pl.when(j < nu_ref[0])
    def _():
        x = _from_slabs(x_ref, tr).astype(BF16)
        lo_e = tlo_ref[j]
        hi_e = thi_ref[j]
        logits = _dot(x, wrt_ref[...]) + brt_ref[...]
        lane = lax.broadcasted_iota(jnp.int32, logits.shape, 1)
        gl = jnp.where(lane < N_GROUPS, logits, -jnp.inf)
        ge = jnp.exp(gl - jnp.max(gl, axis=-1, keepdims=True))
        pick = lambda v, at: jnp.sum(jnp.where(lane == at, v, 0.0), axis=-1, keepdims=True)
        g_prob = pick(ge, lo_e // EXPERTS_PER_GROUP) / jnp.sum(ge, axis=-1, keepdims=True)
        e_lo = pick(logits, EXPERT_LANE0 + lo_e)
        e_hi = pick(logits, EXPERT_LANE0 + hi_e)
        gates = (g_prob / (1.0 + jnp.exp(e_hi - e_lo)), g_prob / (1.0 + jnp.exp(e_lo - e_hi)))

        y = None
        for gate, wg_ref, wu_ref, wd_ref in zip(gates, (wg_lo_ref, wg_hi_ref), (wu_lo_ref, wu_hi_ref),
                                                (wd_lo_ref, wd_hi_ref)):
            g = _dot(x, wg_ref[0])
            u = _dot(x, wu_ref[0])
            hid = g * (1.0 / (1.0 + jnp.exp(-g))) * u * gate
            y_k = _dot(hid.astype(BF16), wd_ref[0])
            y = y_k if y is None else y + y_k
        _to_slabs(y, y_ref, tr)

    @pl.when(j >= nu_ref[0])
    def _():
        y_ref[...] = jnp.zeros_like(y_ref)


def _experts(xs, tile_lo, tile_hi, n_used, p, n_tiles):
    D = D_MODEL
    tr = MOE_TILE_ROWS
    last_used = lambda j, nu: jnp.minimum(j, nu[0] - 1)
    wgu = lambda pick: pl.BlockSpec((1, D, D_EXPERT),
                                    lambda j, tlo, thi, nu: (pick(tlo, thi)[j], 0, 0))
    wd = lambda pick: pl.BlockSpec((1, D_EXPERT, D),
                                   lambda j, tlo, thi, nu: (pick(tlo, thi)[j], 0, 0))
    lo = lambda tlo, thi: tlo
    hi = lambda tlo, thi: thi
    grid_spec = pltpu.PrefetchScalarGridSpec(
        num_scalar_prefetch=3,
        grid=(n_tiles,),
        in_specs=[pl.BlockSpec((tr * SLAB, LANES), lambda j, tlo, thi, nu: (last_used(j, nu), 0)),
                  wgu(lo), wgu(hi), wgu(lo), wgu(hi), wd(lo), wd(hi),
                  pl.BlockSpec((D, ROUTE_LANES), lambda j, tlo, thi, nu: (0, 0)),
                  pl.BlockSpec((1, ROUTE_LANES), lambda j, tlo, thi, nu: (0, 0))],
        out_specs=pl.BlockSpec((tr * SLAB, LANES), lambda j, tlo, thi, nu: (j, 0)),
    )
    return pl.pallas_call(
        _experts_kernel,
        grid_spec=grid_spec,
        out_shape=jax.ShapeDtypeStruct((n_tiles * tr * SLAB, LANES), F32),
        compiler_params=_cparams(("arbitrary",)),
        name="moe_experts",
    )(tile_lo, tile_hi, n_used, xs, p["w_gate"], p["w_gate"], p["w_up"], p["w_up"],
      p["w_down"], p["w_down"],
      p["w_route"], p["b_route"])


def _combine_kernel(pos_ref, pos_next_ref, h1_ref, ys_ref, g2_ref, b2_ref, o_ref,
                    ybuf_ref, sem, *, tm):
    t = pl.program_id(0)
    slot = t % 2

    def gather(idx_ref, dst_slot):
        def issue(r0, carry):
            for k in range(DMA_BURST):
                r = r0 * DMA_BURST + k
                _slab_copy(ys_ref, idx_ref[0, 0, r], ybuf_ref.at[dst_slot], r * SLAB,
                           sem.at[dst_slot]).start(priority=k % 2)
            return carry

        lax.fori_loop(0, tm // DMA_BURST, issue, 0)

    @pl.when(t == 0)
    def _():
        gather(pos_ref, 0)

    @pl.when(t + 1 < pl.num_programs(0))
    def _():
        gather(pos_next_ref, 1 - slot)

    pltpu.make_async_copy(ys_ref.at[pl.ds(0, tm * SLAB)], ybuf_ref.at[slot], sem.at[slot]).wait()
    ffn = _from_slabs(ybuf_ref.at[slot], tm)
    o_ref[...] = _layer_norm(ALPHA * h1_ref[...] + ffn, g2_ref[...], b2_ref[...])


def _combine(h1, pos8, ys, p, tm):
    T, D = h1.shape
    n_steps = T // tm
    pos8 = pos8.reshape(n_steps, 1, tm)
    full = lambda shape: pl.BlockSpec(shape, lambda t: (0,) * len(shape))
    return pl.pallas_call(
        functools.partial(_combine_kernel, tm=tm),
        grid=(n_steps,),
        in_specs=[pl.BlockSpec((1, 1, tm), lambda t: (t, 0, 0), memory_space=pltpu.SMEM),
                  pl.BlockSpec((1, 1, tm), lambda t: (jnp.minimum(t + 1, n_steps - 1), 0, 0),
                               memory_space=pltpu.SMEM),
                  pl.BlockSpec((tm, D), lambda t: (t, 0)),
                  pl.BlockSpec(memory_space=pl.ANY),
                  full((1, D)), full((1, D))],
        out_specs=pl.BlockSpec((tm, D), lambda t: (t, 0)),
        out_shape=jax.ShapeDtypeStruct((T, D), F32),
        scratch_shapes=[pltpu.VMEM((2, tm * SLAB, LANES), F32),
                        pltpu.SemaphoreType.DMA((2,))],
        compiler_params=_cparams(("arbitrary",)),
        name="moe_combine_ln2",
    )(pos8, pos8, h1, ys, p["ln2_g"], p["ln2_b"])


def _dispatch_plan(route, counts):
    T = route.shape[0] * route.shape[1] * route.shape[3]
    tr = MOE_TILE_ROWS
    n_tiles = T // tr + N_CLASSES
    cnt = counts[0, :N_CLASSES].astype(jnp.int32)
    padded = jnp.maximum((cnt + tr - 1) // tr, 1) * tr
    classes = jnp.arange(N_CLASSES, dtype=jnp.int32)
    seg_end = jnp.sum(jnp.where(classes[None, :] <= classes[:, None], padded[None, :], 0), axis=1)
    seg_start = seg_end - padded
    cls = route[:, :, ROUTE_CLS, :].reshape(T)
    rank = route[:, :, ROUTE_RANK, :].reshape(T)
    start_of = jnp.sum(jnp.where(cls[:, None] == classes, seg_start, 0), axis=-1)
    pos8 = (start_of + rank) * SLAB
    tile_row0 = jnp.arange(n_tiles, dtype=jnp.int32) * tr
    tile_cls = jnp.sum((seg_end[None, :] <= tile_row0[:, None]).astype(jnp.int32), axis=-1)
    tile_cls = jnp.minimum(tile_cls, N_CLASSES - 1)
    pairs = [(a, b) for a in range(EXPERTS_PER_GROUP) for b in range(a + 1, EXPERTS_PER_GROUP)]
    lo_of = jnp.asarray([g * EXPERTS_PER_GROUP + a for g in range(N_GROUPS) for a, _ in pairs],
                        jnp.int32)
    hi_of = jnp.asarray([g * EXPERTS_PER_GROUP + b for g in range(N_GROUPS) for _, b in pairs],
                        jnp.int32)
    is_cls = tile_cls[:, None] == classes
    tile_lo = jnp.sum(jnp.where(is_cls, lo_of, 0), axis=-1)
    tile_hi = jnp.sum(jnp.where(is_cls, hi_of, 0), axis=-1)
    n_used = (seg_end[-1:] // tr).astype(jnp.int32)
    return pos8, (seg_start + cnt) * SLAB, tile_lo, tile_hi, n_used, n_tiles


def _rope_tables(pos):
    d = SWA_HEAD_DIM
    inv_freq = ROPE_THETA ** (-jnp.arange(0, d, 2, dtype=F32) / d)
    ang = pos[:, None] * inv_freq[None, :]
    c, s = jnp.cos(ang), jnp.sin(ang)
    return (jnp.concatenate([c, c, c, c], axis=-1), jnp.concatenate([-s, s, -s, s], axis=-1),
            jnp.concatenate([c, c], axis=-1).T, jnp.concatenate([s, s], axis=-1).T)


def _prep_params(ln_in_g, ln_in_b, w_in, swa_sinks, mla_q_norm_g, mla_w_uq, mla_kv_norm_g, mla_w_ukv,
                 swa_out_norm_g, mla_out_norm_g, w_o, ln1_g, ln1_b, moe_w_group, moe_b_group,
                 moe_w_router, moe_b_router, moe_w_gate, moe_w_up, moe_w_down, ln2_g, ln2_b):
    hd = SWA_HEAD_DIM

    def by_head(t, axis):
        return [lax.slice_in_dim(t, h * hd, (h + 1) * hd, axis=axis) for h in SWA_HEAD_ORDER]

    w = w_in[0]
    w_pad = jnp.zeros((D_MODEL, IN_COLS_PADDED - w.shape[1]), w.dtype)
    w_in_p = jnp.concatenate(by_head(w, 1) + [w[:, SWA_Q_COLS:], w_pad], axis=1).astype(BF16)

    dqk = MLA_NOPE_DIM + MLA_ROPE_DIM
    wuq = mla_w_uq[0].reshape(MLA_Q_RANK, MLA_HEADS, dqk)
    wuq = jnp.pad(wuq, ((0, 0), (0, 0), (0, MLA_QK_PAD - dqk)))
    wuq_t = jnp.transpose(wuq, (1, 2, 0)).astype(BF16)
    wukv = mla_w_ukv[0].reshape(MLA_KV_RANK, MLA_HEADS, MLA_NOPE_DIM + MLA_V_DIM)
    wuk = wukv[:, :, :MLA_NOPE_DIM].reshape(MLA_KV_RANK, MLA_HEADS * MLA_NOPE_DIM).astype(BF16)
    wuv_t = jnp.transpose(wukv[:, :, MLA_NOPE_DIM:], (1, 2, 0)).astype(BF16)

    w_o0 = w_o[0]
    w_o_p = jnp.concatenate(by_head(w_o0, 0) + [w_o0[SWA_Q_COLS:]], axis=0).astype(BF16)

    def route_lanes(group_part, expert_part):
        rows = group_part.shape[0]
        gap = jnp.zeros((rows, EXPERT_LANE0 - N_GROUPS), F32)
        tail = jnp.zeros((rows, ROUTE_LANES - EXPERT_LANE0 - N_EXPERTS), F32)
        return jnp.concatenate([group_part, gap, expert_part, tail], axis=1)

    w_route = route_lanes(moe_w_group[0], moe_w_router[0])
    w_route_hi = w_route.astype(BF16)
    w_route_lo = (w_route - w_route_hi.astype(F32)).astype(BF16)
    w_route = jnp.concatenate([w_route_hi, w_route_lo], axis=1)
    b_route = route_lanes(moe_b_group, moe_b_router)

    return {
        "ln_in_g": ln_in_g[None, :], "ln_in_b": ln_in_b[None, :],
        "w_in": w_in_p,
        "sinks": swa_sinks[0],
        "gq": mla_q_norm_g, "w_uq_t": wuq_t,
        "gkv": mla_kv_norm_g, "w_uk": wuk, "w_uv_t": wuv_t,
        "ga": jnp.concatenate(by_head(swa_out_norm_g, 1), axis=1), "gb": mla_out_norm_g,
        "w_o": w_o_p, "ln1_g": ln1_g, "ln1_b": ln1_b,
        "w_route": w_route, "b_route": b_route,
        "w_gate": moe_w_gate[0].astype(BF16), "w_up": moe_w_up[0].astype(BF16),
        "w_down": moe_w_down[0].astype(BF16),
        "ln2_g": ln2_g, "ln2_b": ln2_b,
    }


def kernel(x, meta_tokens, ln_in_g, ln_in_b, w_in, swa_sinks, mla_q_norm_g, mla_w_uq, mla_kv_norm_g, mla_w_ukv, swa_out_norm_g, mla_out_norm_g, w_o, ln1_g, ln1_b, moe_w_group, moe_b_group, moe_w_router, moe_b_router, moe_w_gate, moe_w_up, moe_w_down, ln2_g, ln2_b):
    B, S, D = x.shape
    p = _prep_params(ln_in_g, ln_in_b, w_in, swa_sinks, mla_q_norm_g, mla_w_uq, mla_kv_norm_g,
                     mla_w_ukv, swa_out_norm_g, mla_out_norm_g, w_o, ln1_g, ln1_b, moe_w_group,
                     moe_b_group, moe_w_router, moe_b_router, moe_w_gate, moe_w_up, moe_w_down,
                     ln2_g, ln2_b)
    T = B * S
    tm = min(PROJ_TILE, S)

    x_meta = jnp.concatenate([jnp.zeros((N_PAD, D), x.dtype), meta_tokens.astype(x.dtype)])[None]
    pos_meta = jnp.maximum(jnp.arange(BLOCK) - N_PAD, 0).astype(F32)
    pos_tok = (jnp.arange(S) + N_META).astype(F32)
    _, ka_m, va_m, _, km_m, vt_m = _in_proj(x_meta, _rope_tables(pos_meta), p, BLOCK)
    a_out, qt, km, vt = _in_proj_swa(x, _rope_tables(pos_tok), p, ka_m, va_m, tm)
    b_out = _mla(qt, km, vt, km_m[:, :, N_PAD:, :], vt_m[..., N_PAD:], tm)
    h1, route, counts = _out_proj(x, a_out, b_out, p, min(WIDE_TILE, S))
    h1 = h1.reshape(B * S, D)

    pos8, pad_start8, tile_lo, tile_hi, n_used, n_tiles = _dispatch_plan(route, counts)
    xs = _dispatch(h1, pos8, pad_start8, n_used, n_tiles, min(DISPATCH_TILE, T))
    ys = _experts(xs, tile_lo, tile_hi, n_used, p, n_tiles)
    out = _combine(h1, pos8, ys, p, min(COMBINE_TILE, T))
    return out.reshape(B, S, D)
```

```python
import functools

import jax
import jax.numpy as jnp
from jax import lax
from jax.experimental import pallas as pl
from jax.experimental.pallas import tpu as pltpu

D_MODEL = 1024
N_META = 16
BLOCK = 128
N_PAD = BLOCK - N_META
ROPE_THETA = 10000.0
SWA_HEADS = 8
SWA_KV_HEADS = 2
SWA_HEAD_DIM = 64
MLA_HEADS = 4
MLA_Q_RANK = 256
MLA_KV_RANK = 256
MLA_NOPE_DIM = 128
MLA_ROPE_DIM = 64
MLA_V_DIM = 128
SWA_Q_COLS = SWA_HEADS * SWA_HEAD_DIM
SWA_KV_COLS = SWA_KV_HEADS * SWA_HEAD_DIM
MLA_OUT_COLS = MLA_HEADS * MLA_V_DIM
N_GROUPS = 4
EXPERTS_PER_GROUP = 8
N_EXPERTS = N_GROUPS * EXPERTS_PER_GROUP
D_EXPERT = 256
LN_EPS = 1e-5
RMS_EPS = 1e-6
DEPTH = 1
ALPHA = (2.0 * DEPTH) ** 0.25
NEG = -1e30
LOG2_E = 1.4426950408889634

LANES = 128
COL_KA = SWA_Q_COLS
COL_VA = COL_KA + SWA_KV_COLS
COL_CQ = COL_VA + SWA_KV_COLS
COL_CKV = COL_CQ + MLA_Q_RANK
COL_KR = COL_CKV + MLA_KV_RANK
IN_COLS_PADDED = COL_KR + LANES
MLA_QK_PAD = 256
ROUTE_LANES = 128
EXPERT_LANE0 = 32
ROUTE_CLS, ROUTE_RANK = range(2)
ROUTE_ROWS = 8
PAIRS_PER_GROUP = EXPERTS_PER_GROUP * (EXPERTS_PER_GROUP - 1) // 2
N_CLASSES = N_GROUPS * PAIRS_PER_GROUP
MOE_TILE_ROWS = 256
PROJ_TILE = 1024
WIDE_TILE = 1024
DISPATCH_TILE = 2048
COMBINE_TILE = 1024
DMA_BURST = 8
VMEM_LIMIT = 56 * 1024 * 1024

SWA_HEAD_ORDER = (0, 4, 1, 5, 2, 6, 3, 7)

BF16 = jnp.bfloat16
F32 = jnp.float32


def _cparams(sem):
    return pltpu.CompilerParams(dimension_semantics=sem, vmem_limit_bytes=VMEM_LIMIT)


def _layer_norm(x, g, b):
    mu = jnp.mean(x, axis=-1, keepdims=True)
    xc = x - mu
    var = jnp.mean(xc * xc, axis=-1, keepdims=True)
    return xc * lax.rsqrt(var + LN_EPS) * g + b


def _rms_scale(x):
    return lax.rsqrt(jnp.mean(x * x, axis=-1, keepdims=True) + RMS_EPS)


def _rope128(x, cos, sin_signed, lane_lo):
    up = pltpu.roll(x, LANES - 32, 1)
    dn = pltpu.roll(x, 32, 1)
    return x * cos + jnp.where(lane_lo, up, dn) * sin_signed


def _dot(a, b):
    return jnp.dot(a, b, preferred_element_type=F32)


def _dot_nt(a, b):
    return lax.dot_general(a, b, (((1,), (1,)), ((), ())), preferred_element_type=F32)


def _in_proj_kernel(x_ref, g_ref, b_ref, w_ref, cos_ref, sin_ref, cost_ref, sint_ref, gq_ref,
                    wuqt_ref, gkv_ref, wuk_ref, wuvt_ref,
                    qa_ref, ka_ref, va_ref, qt_ref, km_ref, vt_ref):
    h = _layer_norm(x_ref[0], g_ref[...], b_ref[...])
    u = _dot(h.astype(BF16), w_ref[...])
    cos = cos_ref[...]
    sin = sin_ref[...]
    lane = lax.broadcasted_iota(jnp.int32, (1, LANES), 1)
    lane_lo = (lane % 64) < 32
    rope = functools.partial(_rope128, cos=cos, sin_signed=sin, lane_lo=lane_lo)

    swa_scale = SWA_HEAD_DIM ** -0.5
    for c in range(SWA_Q_COLS // LANES):
        qc = rope(u[:, c * LANES:(c + 1) * LANES]) * swa_scale
        qa_ref[0, :, c * LANES:(c + 1) * LANES] = qc.astype(BF16)
    ka_ref[0] = rope(u[:, COL_KA:COL_VA]).astype(BF16)
    va_ref[0] = u[:, COL_VA:COL_CQ].astype(BF16)

    cq = u[:, COL_CQ:COL_CKV]
    cqn = (cq * _rms_scale(cq) * gq_ref[...]).astype(BF16)
    ckv = u[:, COL_CKV:COL_KR]
    ckvn = (ckv * _rms_scale(ckv) * gkv_ref[...]).astype(BF16)
    k_nope = _dot(ckvn, wuk_ref[...])
    kr = rope(u[:, COL_KR:IN_COLS_PADDED]).astype(BF16)
    q_scale = (MLA_NOPE_DIM + MLA_ROPE_DIM) ** -0.5 * LOG2_E
    cost = cost_ref[...]
    sint = sint_ref[...]
    half = MLA_ROPE_DIM // 2
    for hd in range(MLA_HEADS):
        qt = _dot_nt(wuqt_ref[hd], cqn)
        qr = qt[MLA_NOPE_DIM:MLA_NOPE_DIM + MLA_ROPE_DIM]
        rot = jnp.concatenate([-qr[half:], qr[:half]], axis=0)
        qt_ref[0, hd, 0:MLA_NOPE_DIM, :] = (qt[:MLA_NOPE_DIM] * q_scale).astype(BF16)
        qt_ref[0, hd, MLA_NOPE_DIM:MLA_NOPE_DIM + MLA_ROPE_DIM, :] = (
            (qr * cost + rot * sint) * q_scale).astype(BF16)
        qt_ref[0, hd, MLA_NOPE_DIM + MLA_ROPE_DIM:, :] = (
            qt[MLA_NOPE_DIM + MLA_ROPE_DIM:]).astype(BF16)
        km_ref[0, hd, :, 0:LANES] = k_nope[:, hd * LANES:(hd + 1) * LANES].astype(BF16)
        km_ref[0, hd, :, LANES:2 * LANES] = kr
        vt_ref[0, hd, 0] = _dot_nt(wuvt_ref[hd], ckvn).astype(BF16)


def _in_proj(x, tables, p, tm):
    cos, sin, cos_t, sin_t = tables
    B, S, D = x.shape
    grid = (B, S // tm)
    full = lambda shape: pl.BlockSpec(shape, lambda b, i: (0,) * len(shape))
    out_shape = (
        jax.ShapeDtypeStruct((B, S, SWA_Q_COLS), BF16),
        jax.ShapeDtypeStruct((B, S, SWA_KV_COLS), BF16),
        jax.ShapeDtypeStruct((B, S, SWA_KV_COLS), BF16),
        jax.ShapeDtypeStruct((B, MLA_HEADS, MLA_QK_PAD, S), BF16),
        jax.ShapeDtypeStruct((B, MLA_HEADS, S, MLA_QK_PAD), BF16),
        jax.ShapeDtypeStruct((B, MLA_HEADS, S // tm, MLA_V_DIM, tm), BF16),
    )
    row = lambda w: pl.BlockSpec((1, tm, w), lambda b, i: (b, i, 0))
    head = lambda w: pl.BlockSpec((1, MLA_HEADS, tm, w), lambda b, i: (b, 0, i, 0))
    return pl.pallas_call(
        _in_proj_kernel,
        grid=grid,
        in_specs=[
            row(D),
            full((1, D)), full((1, D)),
            full((D, IN_COLS_PADDED)),
            pl.BlockSpec((tm, LANES), lambda b, i: (i, 0)),
            pl.BlockSpec((tm, LANES), lambda b, i: (i, 0)),
            pl.BlockSpec((MLA_ROPE_DIM, tm), lambda b, i: (0, i)),
            pl.BlockSpec((MLA_ROPE_DIM, tm), lambda b, i: (0, i)),
            full((1, MLA_Q_RANK)), full((MLA_HEADS, MLA_QK_PAD, MLA_Q_RANK)),
            full((1, MLA_KV_RANK)), full((MLA_KV_RANK, MLA_HEADS * MLA_NOPE_DIM)),
            full((MLA_HEADS, MLA_V_DIM, MLA_KV_RANK)),
        ],
        out_specs=(row(SWA_Q_COLS), row(SWA_KV_COLS), row(SWA_KV_COLS),
                   pl.BlockSpec((1, MLA_HEADS, MLA_QK_PAD, tm), lambda b, i: (b, 0, 0, i)),
                   head(MLA_QK_PAD),
                   pl.BlockSpec((1, MLA_HEADS, 1, MLA_V_DIM, tm), lambda b, i: (b, 0, i, 0, 0))),
        out_shape=out_shape,
        compiler_params=_cparams(("parallel", "parallel")),
        name="in_proj",
    )(x, p["ln_in_g"], p["ln_in_b"], p["w_in"], cos, sin, cos_t, sin_t, p["gq"], p["w_uq_t"],
      p["gkv"], p["w_uk"], p["w_uv_t"])


def _swa_blocks(sink_ref, qa_ref, kc_ref, vc_ref, kprev, vprev, first, o_ref, blocks_per_step):
    row = lax.broadcasted_iota(jnp.int32, (2 * BLOCK, BLOCK), 0) % BLOCK
    col = lax.broadcasted_iota(jnp.int32, (2 * BLOCK, BLOCK), 1)
    own = col <= row
    meta_ok = col >= jnp.where(first, N_PAD, 0)
    lane = lax.broadcasted_iota(jnp.int32, (1, LANES), 1)
    lane_kv0 = lane < SWA_HEAD_DIM
    top_rows = lax.broadcasted_iota(jnp.int32, (2 * BLOCK, 1), 0) < BLOCK

    for r in range(blocks_per_step):
        rows = slice(r * BLOCK, (r + 1) * BLOCK)
        if r == 0:
            kp, vp = kprev, vprev
        else:
            prev_rows = slice((r - 1) * BLOCK, r * BLOCK)
            kp, vp = kc_ref[0, prev_rows, :], vc_ref[0, prev_rows, :]
        kc, vc = kc_ref[0, rows, :], vc_ref[0, rows, :]
        for c in range(SWA_Q_COLS // LANES):
            qc = qa_ref[0, rows, c * LANES:(c + 1) * LANES]
            zero = jnp.zeros_like(qc)
            q2 = jnp.concatenate([jnp.where(lane_kv0, qc, zero),
                                  jnp.where(lane_kv0, zero, qc)], axis=0)
            s_prev = _dot_nt(q2, kp)
            if r == 0:
                s_prev = jnp.where(meta_ok, s_prev, NEG)
            s = jnp.where(own, _dot_nt(q2, kc), s_prev)
            sink = jnp.where(top_rows, sink_ref[c], sink_ref[c + 4])
            m = jnp.maximum(jnp.max(s, axis=-1, keepdims=True), sink)
            e = jnp.exp(s - m)
            denom = jnp.sum(e, axis=-1, keepdims=True) + jnp.exp(sink - m)
            pr = e * (1.0 / denom)
            o2 = (_dot(jnp.where(own, pr, 0.0).astype(BF16), vc)
                  + _dot(jnp.where(own, 0.0, pr).astype(BF16), vp))
            o = jnp.where(lane_kv0, o2[:BLOCK], o2[BLOCK:])
            o_ref[0, rows, c * LANES:(c + 1) * LANES] = o.astype(BF16)


def _in_proj_swa_kernel(sink_ref, x_ref, g_ref, b_ref, w_ref, cos_ref, sin_ref, cost_ref, sint_ref,
                        gq_ref, wuqt_ref, gkv_ref, wuk_ref, wuvt_ref, kmeta_ref, vmeta_ref,
                        a_ref, qt_ref, km_ref, vt_ref,
                        qa_s, ka_s, va_s, kprev_s, vprev_s):
    _in_proj_kernel(x_ref, g_ref, b_ref, w_ref, cos_ref, sin_ref, cost_ref, sint_ref, gq_ref,
                    wuqt_ref, gkv_ref, wuk_ref, wuvt_ref, qa_s, ka_s, va_s, qt_ref, km_ref, vt_ref)
    first = pl.program_id(1) == 0
    kprev = jnp.where(first, kmeta_ref[0], kprev_s[...])
    vprev = jnp.where(first, vmeta_ref[0], vprev_s[...])
    tm = qa_s.shape[1]
    _swa_blocks(sink_ref, qa_s, ka_s, va_s, kprev, vprev, first, a_ref, tm // BLOCK)
    kprev_s[...] = ka_s[0, tm - BLOCK:, :]
    vprev_s[...] = va_s[0, tm - BLOCK:, :]


def _in_proj_swa(x, tables, p, ka_meta, va_meta, tm):
    cos, sin, cos_t, sin_t = tables
    B, S, D = x.shape
    kvw = SWA_KV_COLS
    full = lambda shape: pl.BlockSpec(shape, lambda b, i: (0,) * len(shape))
    row = lambda w: pl.BlockSpec((1, tm, w), lambda b, i: (b, i, 0))
    out_shape = (
        jax.ShapeDtypeStruct((B, S, SWA_Q_COLS), BF16),
        jax.ShapeDtypeStruct((B, MLA_HEADS, MLA_QK_PAD, S), BF16),
        jax.ShapeDtypeStruct((B, MLA_HEADS, S, MLA_QK_PAD), BF16),
        jax.ShapeDtypeStruct((B, MLA_HEADS, S // tm, MLA_V_DIM, tm), BF16),
    )
    return pl.pallas_call(
        _in_proj_swa_kernel,
        grid=(B, S // tm),
        in_specs=[
            pl.BlockSpec(memory_space=pltpu.SMEM),
            row(D),
            full((1, D)), full((1, D)),
            full((D, IN_COLS_PADDED)),
            pl.BlockSpec((tm, LANES), lambda b, i: (i, 0)),
            pl.BlockSpec((tm, LANES), lambda b, i: (i, 0)),
            pl.BlockSpec((MLA_ROPE_DIM, tm), lambda b, i: (0, i)),
            pl.BlockSpec((MLA_ROPE_DIM, tm), lambda b, i: (0, i)),
            full((1, MLA_Q_RANK)), full((MLA_HEADS, MLA_QK_PAD, MLA_Q_RANK)),
            full((1, MLA_KV_RANK)), full((MLA_KV_RANK, MLA_HEADS * MLA_NOPE_DIM)),
            full((MLA_HEADS, MLA_V_DIM, MLA_KV_RANK)),
            full((1, BLOCK, kvw)), full((1, BLOCK, kvw)),
        ],
        out_specs=(row(SWA_Q_COLS),
                   pl.BlockSpec((1, MLA_HEADS, MLA_QK_PAD, tm), lambda b, i: (b, 0, 0, i)),
                   pl.BlockSpec((1, MLA_HEADS, tm, MLA_QK_PAD), lambda b, i: (b, 0, i, 0)),
                   pl.BlockSpec((1, MLA_HEADS, 1, MLA_V_DIM, tm), lambda b, i: (b, 0, i, 0, 0))),
        out_shape=out_shape,
        scratch_shapes=[pltpu.VMEM((1, tm, SWA_Q_COLS), BF16), pltpu.VMEM((1, tm, kvw), BF16),
                        pltpu.VMEM((1, tm, kvw), BF16), pltpu.VMEM((BLOCK, kvw), BF16),
                        pltpu.VMEM((BLOCK, kvw), BF16)],
        compiler_params=_cparams(("parallel", "arbitrary")),
        name="in_proj_swa",
    )(p["sinks"], x, p["ln_in_g"], p["ln_in_b"], p["w_in"], cos, sin, cos_t, sin_t, p["gq"],
      p["w_uq_t"], p["gkv"], p["w_uk"], p["w_uv_t"], ka_meta, va_meta)


def _mla_kernel(qt_ref, k_ref, vt_ref, kmeta_ref, vtmeta_ref, o_ref,
                s0a_ref, s0b_ref, s1a_ref, s1b_ref, smax_ref, m_ref, l_ref, acc_ref, *, tq):
    u = pl.program_id(2)
    qts = (qt_ref[0, 0, :, :tq], qt_ref[0, 0, :, tq:])
    bufs = ((s0a_ref, s0b_ref), (s1a_ref, s1b_ref))

    for g in range(2):
        s = _dot(kmeta_ref[0, 0], qts[g])
        m = jnp.max(s, axis=0, keepdims=True)
        p = jnp.exp2(s - m)
        m_ref[g] = m
        l_ref[g] = jnp.sum(p, axis=0, keepdims=True)
        acc_ref[g] = _dot(vtmeta_ref[0, 0, 0], p.astype(BF16))

    def scores_into(g, ab, j):
        start = pl.multiple_of(j * tq, tq)
        s = _dot(k_ref[0, 0, pl.ds(start, tq), :], qts[g])
        bufs[g][ab][...] = s
        smax_ref[g, ab] = jnp.max(s, axis=0, keepdims=True)

    def absorb(g, ab, j, causal=False):
        s = bufs[g][ab][...]
        if causal:
            key = lax.broadcasted_iota(jnp.int32, (tq, tq), 0)
            query = lax.broadcasted_iota(jnp.int32, (tq, tq), 1)
            s = jnp.where(key <= query, s, NEG)
            smax = jnp.max(s, axis=0, keepdims=True)
        else:
            smax = smax_ref[g, ab]
        m = m_ref[g]
        m_new = jnp.maximum(m, smax)
        alpha = jnp.exp2(m - m_new)
        p = jnp.exp2(s - m_new)
        m_ref[g] = m_new
        l_ref[g] = alpha * l_ref[g] + jnp.sum(p, axis=0, keepdims=True)
        acc_ref[g] = alpha * acc_ref[g] + _dot(vt_ref[0, 0, j], p.astype(BF16))

    buf_a, buf_b = 0, 1
    for g in range(2):
        scores_into(g, buf_a, 0)

    def pair(t, carry):
        scores_into(0, buf_b, 2 * t + 1)
        absorb(0, buf_a, 2 * t)
        scores_into(1, buf_b, 2 * t + 1)
        scores_into(0, buf_a, 2 * t + 2)
        absorb(1, buf_a, 2 * t)
        absorb(0, buf_b, 2 * t + 1)
        scores_into(1, buf_a, 2 * t + 2)
        absorb(1, buf_b, 2 * t + 1)
        return carry

    lax.fori_loop(0, u, pair, 0)

    scores_into(1, buf_b, 2 * u + 1)
    absorb(1, buf_a, 2 * u)
    absorb(0, buf_a, 2 * u, causal=True)
    absorb(1, buf_b, 2 * u + 1, causal=True)

    for g in range(2):
        o_ref[0, g * tq:(g + 1) * tq, :] = (acc_ref[g] * (1.0 / l_ref[g])).T.astype(BF16)


def _mla(qt, km, vt, km_meta, vt_meta, tq):
    B, H, _, S = qt.shape
    assert S % (2 * tq) == 0, "a step handles an (even, odd) pair of query tiles"
    return pl.pallas_call(
        functools.partial(_mla_kernel, tq=tq),
        grid=(B, H, S // (2 * tq)),
        in_specs=[
            pl.BlockSpec((1, 1, MLA_QK_PAD, 2 * tq), lambda b, h, u: (b, h, 0, u)),
            pl.BlockSpec((1, 1, S, MLA_QK_PAD), lambda b, h, u: (b, h, 0, 0)),
            pl.BlockSpec((1, 1, S // tq, MLA_V_DIM, tq), lambda b, h, u: (b, h, 0, 0, 0)),
            pl.BlockSpec((1, 1, N_META, MLA_QK_PAD), lambda b, h, u: (0, h, 0, 0)),
            pl.BlockSpec((1, 1, 1, MLA_V_DIM, N_META), lambda b, h, u: (0, h, 0, 0, 0)),
        ],
        out_specs=pl.BlockSpec((1, 2 * tq, MLA_V_DIM), lambda b, h, u: (b, u, h)),
        out_shape=jax.ShapeDtypeStruct((B, S, MLA_OUT_COLS), BF16),
        scratch_shapes=[pltpu.VMEM((tq, tq), F32), pltpu.VMEM((tq, tq), F32),
                        pltpu.VMEM((tq, tq), F32), pltpu.VMEM((tq, tq), F32),
                        pltpu.VMEM((2, 2, 1, tq), F32),
                        pltpu.VMEM((2, 1, tq), F32), pltpu.VMEM((2, 1, tq), F32),
                        pltpu.VMEM((2, MLA_V_DIM, tq), F32)],
        compiler_params=_cparams(("parallel", "parallel", "arbitrary")),
        name="mla_attn",
    )(qt, km, vt, km_meta, vt_meta)


def _out_proj_kernel(x_ref, a_ref, b_ref, gin_ref, bin_ref, ga_ref, gb_ref, wo_ref, g1_ref, b1_ref,
                     wrt_ref, brt_ref, h1_ref, route_ref, counts_out_ref, count_ref):
    h = _layer_norm(x_ref[0], gin_ref[...], bin_ref[...])
    a = a_ref[0].astype(F32)
    b = b_ref[0].astype(F32)
    an = (a * _rms_scale(a) * ga_ref[...]).astype(BF16)
    bn = (b * _rms_scale(b) * gb_ref[...]).astype(BF16)
    mix = _dot(jnp.concatenate([an, bn], axis=-1), wo_ref[...])
    h1 = _layer_norm(ALPHA * h + mix, g1_ref[...], b1_ref[...])
    h1_ref[0] = h1

    x_hi = h1.astype(BF16)
    x_lo = (h1 - x_hi.astype(F32)).astype(BF16)
    hi_both = _dot(x_hi, wrt_ref[...])
    logits = (hi_both[:, :ROUTE_LANES] + hi_both[:, ROUTE_LANES:]
              + _dot(x_lo, wrt_ref[:, :ROUTE_LANES]) + brt_ref[...])
    lane = lax.broadcasted_iota(jnp.int32, logits.shape, 1).astype(F32)
    ninf = -jnp.inf
    no_lane = float(ROUTE_LANES)
    gl = jnp.where(lane < N_GROUPS, logits, ninf)
    gmax = jnp.max(gl, axis=-1, keepdims=True)
    g_idx = jnp.min(jnp.where(gl == gmax, lane, no_lane), axis=-1, keepdims=True)
    lo = EXPERT_LANE0 + g_idx * EXPERTS_PER_GROUP
    el = jnp.where((lane >= lo) & (lane < lo + EXPERTS_PER_GROUP), logits, ninf)
    m1 = jnp.max(el, axis=-1, keepdims=True)
    i1 = jnp.min(jnp.where(el == m1, lane, no_lane), axis=-1, keepdims=True)
    el2 = jnp.where(lane == i1, ninf, el)
    m2 = jnp.max(el2, axis=-1, keepdims=True)
    i2 = jnp.min(jnp.where(el2 == m2, lane, no_lane), axis=-1, keepdims=True)

    a = jnp.minimum(i1, i2) - lo
    b = jnp.maximum(i1, i2) - lo
    cls = (g_idx * PAIRS_PER_GROUP + a * (2.0 * EXPERTS_PER_GROUP - 1.0 - a) * 0.5
           + (b - a - 1.0))

    @pl.when((pl.program_id(0) == 0) & (pl.program_id(1) == 0))
    def _():
        count_ref[...] = jnp.zeros_like(count_ref)

    tm = logits.shape[0]
    onehot = jnp.where(lane == cls, 1.0, 0.0)
    r_i = lax.broadcasted_iota(jnp.int32, (BLOCK, BLOCK), 0)
    c_i = lax.broadcasted_iota(jnp.int32, (BLOCK, BLOCK), 1)
    earlier = jnp.where(r_i > c_i, 1.0, 0.0).astype(BF16)
    running = count_ref[...]
    before = []
    for blk in range(tm // BLOCK):
        oh = onehot[blk * BLOCK:(blk + 1) * BLOCK]
        before.append(_dot(earlier, oh.astype(BF16)) + running)
        running = running + jnp.sum(oh, axis=0, keepdims=True)
    before = jnp.concatenate(before, axis=0)
    rank = jnp.sum(jnp.where(lane == cls, before, 0.0), axis=-1, keepdims=True)
    count_ref[...] = running
    counts_out_ref[...] = running

    route = jnp.where(lane == ROUTE_CLS, cls, 0.0)
    route = jnp.where(lane == ROUTE_RANK, rank, route)
    route_ref[0, 0] = route.T[:ROUTE_ROWS].astype(jnp.int32)


def _out_proj(x, a_out, b_out, p, tm):
    B, S, D = x.shape
    full = lambda shape: pl.BlockSpec(shape, lambda b, i: (0,) * len(shape))
    row = lambda w: pl.BlockSpec((1, tm, w), lambda b, i: (b, i, 0))
    return pl.pallas_call(
        _out_proj_kernel,
        grid=(B, S // tm),
        in_specs=[row(D), row(SWA_Q_COLS), row(MLA_OUT_COLS),
                  full((1, D)), full((1, D)), full((1, SWA_Q_COLS)), full((1, MLA_OUT_COLS)),
                  full((D, D)), full((1, D)), full((1, D)),
                  full((D, 2 * ROUTE_LANES)), full((1, ROUTE_LANES))],
        out_specs=(row(D), pl.BlockSpec((1, 1, ROUTE_ROWS, tm), lambda b, i: (b, i, 0, 0)),
                   full((1, ROUTE_LANES))),
        out_shape=(jax.ShapeDtypeStruct((B, S, D), F32),
                   jax.ShapeDtypeStruct((B, S // tm, ROUTE_ROWS, tm), jnp.int32),
                   jax.ShapeDtypeStruct((1, ROUTE_LANES), F32)),
        scratch_shapes=[pltpu.VMEM((1, ROUTE_LANES), F32)],
        compiler_params=_cparams(("arbitrary", "arbitrary")),
        name="out_proj_route",
    )(x, a_out, b_out, p["ln_in_g"], p["ln_in_b"], p["ga"], p["gb"], p["w_o"], p["ln1_g"],
      p["ln1_b"], p["w_route"], p["b_route"])


SLAB = D_MODEL // LANES


def _slab_copy(src_ref, src_tok8, dst_ref, dst_tok8, sem):
    return pltpu.make_async_copy(src_ref.at[pl.ds(pl.multiple_of(src_tok8, SLAB), SLAB)],
                                 dst_ref.at[pl.ds(pl.multiple_of(dst_tok8, SLAB), SLAB)], sem)


def _to_slabs(x, slab_ref, n):
    for s in range(SLAB):
        slab_ref[pl.ds(s, n, stride=SLAB), :] = x[:, s * LANES:(s + 1) * LANES]


def _from_slabs(slab_ref, n):
    return jnp.concatenate([slab_ref[pl.ds(s, n, stride=SLAB), :] for s in range(SLAB)], axis=1)


def _dispatch_kernel(padstart_ref, nu_ref, pos_ref, h1_ref, xs_ref, rows_ref, zero_ref, sem, *,
                     tm, n_tiles):
    tail = zero_ref.shape[0]

    def zero_fill(start):
        return pltpu.make_async_copy(
            zero_ref, xs_ref.at[pl.ds(pl.multiple_of(start, SLAB), tail)], sem)

    @pl.when(pl.program_id(0) == 0)
    def _():
        zero_ref[...] = jnp.zeros_like(zero_ref)
        for parity in range(2):
            tails = [zero_fill(padstart_ref[c]) for c in range(parity, N_CLASSES, 2)]
            for t in tails:
                t.start()
            for t in tails:
                t.wait()

        def fill_unused(j, carry):
            fill = zero_fill(j * tail)
            fill.start()
            fill.wait()
            return carry

        lax.fori_loop(nu_ref[0], n_tiles + 1, fill_unused, 0)

    _to_slabs(h1_ref[...], rows_ref, tm)

    def issue(r0, carry):
        for k in range(DMA_BURST):
            r = r0 * DMA_BURST + k
            _slab_copy(rows_ref, r * SLAB, xs_ref, pos_ref[0, 0, r], sem).start(priority=k % 2)
        return carry

    lax.fori_loop(0, tm // DMA_BURST, issue, 0)
    pltpu.make_async_copy(rows_ref, xs_ref.at[pl.ds(0, tm * SLAB)], sem).wait()


def _dispatch(h1, pos8, pad_start8, n_used, n_tiles, tm):
    T, D = h1.shape
    pos8 = pos8.reshape(T // tm, 1, tm)
    n_rows = (n_tiles + 1) * MOE_TILE_ROWS
    grid_spec = pltpu.PrefetchScalarGridSpec(
        num_scalar_prefetch=2,
        grid=(T // tm,),
        in_specs=[pl.BlockSpec((1, 1, tm), lambda t, ps, nu: (t, 0, 0),
                               memory_space=pltpu.SMEM),
                  pl.BlockSpec((tm, D), lambda t, ps, nu: (t, 0))],
        out_specs=pl.BlockSpec(memory_space=pl.ANY),
        scratch_shapes=[pltpu.VMEM((tm * SLAB, LANES), F32),
                        pltpu.VMEM((MOE_TILE_ROWS * SLAB, LANES), F32),
                        pltpu.SemaphoreType.DMA],
    )
    return pl.pallas_call(
        functools.partial(_dispatch_kernel, tm=tm, n_tiles=n_tiles),
        grid_spec=grid_spec,
        out_shape=jax.ShapeDtypeStruct((n_rows * SLAB, LANES), F32),
        compiler_params=_cparams(("arbitrary",)),
        name="moe_dispatch",
    )(pad_start8, n_used, pos8, h1)


def _experts_kernel(tlo_ref, thi_ref, nu_ref, x_ref, wg_lo_ref, wg_hi_ref, wu_lo_ref, wu_hi_ref,
                    wd_lo_ref, wd_hi_ref, wrt_ref, brt_ref, y_ref):
    j = pl.program_id(0)
    tr = MOE_TILE_ROWS

    @pl.when(j < nu_ref[0])
    def _():
        x = _from_slabs(x_ref, tr).astype(BF16)
        lo_e = tlo_ref[j]
        hi_e = thi_ref[j]
        logits = _dot(x, wrt_ref[...]) + brt_ref[...]
        lane = lax.broadcasted_iota(jnp.int32, logits.shape, 1)
        gl = jnp.where(lane < N_GROUPS, logits, -jnp.inf)
        ge = jnp.exp(gl - jnp.max(gl, axis=-1, keepdims=True))
        pick = lambda v, at: jnp.sum(jnp.where(lane == at, v, 0.0), axis=-1, keepdims=True)
        g_prob = pick(ge, lo_e // EXPERTS_PER_GROUP) / jnp.sum(ge, axis=-1, keepdims=True)
        e_lo = pick(logits, EXPERT_LANE0 + lo_e)
        e_hi = pick(logits, EXPERT_LANE0 + hi_e)
        gates = (g_prob / (1.0 + jnp.exp(e_hi - e_lo)), g_prob / (1.0 + jnp.exp(e_lo - e_hi)))

        y = None
        for gate, wg_ref, wu_ref, wd_ref in zip(gates, (wg_lo_ref, wg_hi_ref), (wu_lo_ref, wu_hi_ref),
                                                (wd_lo_ref, wd_hi_ref)):
            g = _dot(x, wg_ref[0])
            u = _dot(x, wu_ref[0])
            hid = g * (1.0 / (1.0 + jnp.exp(-g))) * u * gate
            y_k = _dot(hid.astype(BF16), wd_ref[0])
            y = y_k if y is None else y + y_k
        _to_slabs(y, y_ref, tr)


def _experts(xs, tile_lo, tile_hi, n_used, p, n_tiles):
    D = D_MODEL
    tr = MOE_TILE_ROWS
    last_used = lambda j, nu: jnp.minimum(j, nu[0] - 1)
    wgu = lambda pick: pl.BlockSpec((1, D, D_EXPERT),
                                    lambda j, tlo, thi, nu: (pick(tlo, thi)[j], 0, 0))
    wd = lambda pick: pl.BlockSpec((1, D_EXPERT, D),
                                   lambda j, tlo, thi, nu: (pick(tlo, thi)[j], 0, 0))
    lo = lambda tlo, thi: tlo
    hi = lambda tlo, thi: thi
    grid_spec = pltpu.PrefetchScalarGridSpec(
        num_scalar_prefetch=3,
        grid=(n_tiles,),
        in_specs=[pl.BlockSpec((tr * SLAB, LANES), lambda j, tlo, thi, nu: (last_used(j, nu), 0)),
                  wgu(lo), wgu(hi), wgu(lo), wgu(hi), wd(lo), wd(hi),
                  pl.BlockSpec((D, ROUTE_LANES), lambda j, tlo, thi, nu: (0, 0)),
                  pl.BlockSpec((1, ROUTE_LANES), lambda j, tlo, thi, nu: (0, 0))],
        out_specs=pl.BlockSpec((tr * SLAB, LANES), lambda j, tlo, thi, nu: (last_used(j, nu), 0)),
    )
    return pl.pallas_call(
        _experts_kernel,
        grid_spec=grid_spec,
        out_shape=jax.ShapeDtypeStruct(xs.shape, F32),
        input_output_aliases={3: 0},
        compiler_params=_cparams(("arbitrary",)),
        name="moe_experts",
    )(tile_lo, tile_hi, n_used, xs, p["w_gate"], p["w_gate"], p["w_up"], p["w_up"],
      p["w_down"], p["w_down"],
      p["w_route"], p["b_route"])


def _combine_kernel(pos_ref, pos_next_ref, h1_ref, ys_ref, g2_ref, b2_ref, o_ref,
                    ybuf_ref, sem, *, tm):
    t = pl.program_id(0)
    slot = t % 2

    def gather(idx_ref, dst_slot):
        def issue(r0, carry):
            for k in range(DMA_BURST):
                r = r0 * DMA_BURST + k
                _slab_copy(ys_ref, idx_ref[0, 0, r], ybuf_ref.at[dst_slot], r * SLAB,
                           sem.at[dst_slot]).start(priority=k % 2)
            return carry

        lax.fori_loop(0, tm // DMA_BURST, issue, 0)

    @pl.when(t == 0)
    def _():
        gather(pos_ref, 0)

    @pl.when(t + 1 < pl.num_programs(0))
    def _():
        gather(pos_next_ref, 1 - slot)

    pltpu.make_async_copy(ys_ref.at[pl.ds(0, tm * SLAB)], ybuf_ref.at[slot], sem.at[slot]).wait()
    ffn = _from_slabs(ybuf_ref.at[slot], tm)
    o_ref[...] = _layer_norm(ALPHA * h1_ref[...] + ffn, g2_ref[...], b2_ref[...])


def _combine(h1, pos8, ys, p, tm):
    T, D = h1.shape
    n_steps = T // tm
    pos8 = pos8.reshape(n_steps, 1, tm)
    full = lambda shape: pl.BlockSpec(shape, lambda t: (0,) * len(shape))
    return pl.pallas_call(
        functools.partial(_combine_kernel, tm=tm),
        grid=(n_steps,),
        in_specs=[pl.BlockSpec((1, 1, tm), lambda t: (t, 0, 0), memory_space=pltpu.SMEM),
                  pl.BlockSpec((1, 1, tm), lambda t: (jnp.minimum(t + 1, n_steps - 1), 0, 0),
                               memory_space=pltpu.SMEM),
                  pl.BlockSpec((tm, D), lambda t: (t, 0)),
                  pl.BlockSpec(memory_space=pl.ANY),
                  full((1, D)), full((1, D))],
        out_specs=pl.BlockSpec((tm, D), lambda t: (t, 0)),
        out_shape=jax.ShapeDtypeStruct((T, D), F32),
        scratch_shapes=[pltpu.VMEM((2, tm * SLAB, LANES), F32),
                        pltpu.SemaphoreType.DMA((2,))],
        compiler_params=_cparams(("arbitrary",)),
        name="moe_combine_ln2",
    )(pos8, pos8, h1, ys, p["ln2_g"], p["ln2_b"])


def _dispatch_plan(route, counts):
    T = route.shape[0] * route.shape[1] * route.shape[3]
    tr = MOE_TILE_ROWS
    n_tiles = T // tr + N_CLASSES
    cnt = counts[0, :N_CLASSES].astype(jnp.int32)
    padded = jnp.maximum((cnt + tr - 1) // tr, 1) * tr
    classes = jnp.arange(N_CLASSES, dtype=jnp.int32)
    seg_end = jnp.sum(jnp.where(classes[None, :] <= classes[:, None], padded[None, :], 0), axis=1)
    seg_start = seg_end - padded
    cls = route[:, :, ROUTE_CLS, :].reshape(T)
    rank = route[:, :, ROUTE_RANK, :].reshape(T)
    start_of = jnp.sum(jnp.where(cls[:, None] == classes, seg_start, 0), axis=-1)
    pos8 = (start_of + rank) * SLAB
    tile_row0 = jnp.arange(n_tiles, dtype=jnp.int32) * tr
    tile_cls = jnp.sum((seg_end[None, :] <= tile_row0[:, None]).astype(jnp.int32), axis=-1)
    tile_cls = jnp.minimum(tile_cls, N_CLASSES - 1)
    pairs = [(a, b) for a in range(EXPERTS_PER_GROUP) for b in range(a + 1, EXPERTS_PER_GROUP)]
    lo_of = jnp.asarray([g * EXPERTS_PER_GROUP + a for g in range(N_GROUPS) for a, _ in pairs],
                        jnp.int32)
    hi_of = jnp.asarray([g * EXPERTS_PER_GROUP + b for g in range(N_GROUPS) for _, b in pairs],
                        jnp.int32)
    is_cls = tile_cls[:, None] == classes
    tile_lo = jnp.sum(jnp.where(is_cls, lo_of, 0), axis=-1)
    tile_hi = jnp.sum(jnp.where(is_cls, hi_of, 0), axis=-1)
    n_used = (seg_end[-1:] // tr).astype(jnp.int32)
    return pos8, (seg_start + cnt) * SLAB, tile_lo, tile_hi, n_used, n_tiles


def _rope_tables(pos):
    d = SWA_HEAD_DIM
    inv_freq = ROPE_THETA ** (-jnp.arange(0, d, 2, dtype=F32) / d)
    ang = pos[:, None] * inv_freq[None, :]
    c, s = jnp.cos(ang), jnp.sin(ang)
    return (jnp.concatenate([c, c, c, c], axis=-1), jnp.concatenate([-s, s, -s, s], axis=-1),
            jnp.concatenate([c, c], axis=-1).T, jnp.concatenate([s, s], axis=-1).T)


def _prep_params(ln_in_g, ln_in_b, w_in, swa_sinks, mla_q_norm_g, mla_w_uq, mla_kv_norm_g, mla_w_ukv,
                 swa_out_norm_g, mla_out_norm_g, w_o, ln1_g, ln1_b, moe_w_group, moe_b_group,
                 moe_w_router, moe_b_router, moe_w_gate, moe_w_up, moe_w_down, ln2_g, ln2_b):
    hd = SWA_HEAD_DIM

    def by_head(t, axis):
        return [lax.slice_in_dim(t, h * hd, (h + 1) * hd, axis=axis) for h in SWA_HEAD_ORDER]

    w = w_in[0]
    w_pad = jnp.zeros((D_MODEL, IN_COLS_PADDED - w.shape[1]), w.dtype)
    w_in_p = jnp.concatenate(by_head(w, 1) + [w[:, SWA_Q_COLS:], w_pad], axis=1).astype(BF16)

    dqk = MLA_NOPE_DIM + MLA_ROPE_DIM
    wuq = mla_w_uq[0].reshape(MLA_Q_RANK, MLA_HEADS, dqk)
    wuq = jnp.pad(wuq, ((0, 0), (0, 0), (0, MLA_QK_PAD - dqk)))
    wuq_t = jnp.transpose(wuq, (1, 2, 0)).astype(BF16)
    wukv = mla_w_ukv[0].reshape(MLA_KV_RANK, MLA_HEADS, MLA_NOPE_DIM + MLA_V_DIM)
    wuk = wukv[:, :, :MLA_NOPE_DIM].reshape(MLA_KV_RANK, MLA_HEADS * MLA_NOPE_DIM).astype(BF16)
    wuv_t = jnp.transpose(wukv[:, :, MLA_NOPE_DIM:], (1, 2, 0)).astype(BF16)

    w_o0 = w_o[0]
    w_o_p = jnp.concatenate(by_head(w_o0, 0) + [w_o0[SWA_Q_COLS:]], axis=0).astype(BF16)

    def route_lanes(group_part, expert_part):
        rows = group_part.shape[0]
        gap = jnp.zeros((rows, EXPERT_LANE0 - N_GROUPS), F32)
        tail = jnp.zeros((rows, ROUTE_LANES - EXPERT_LANE0 - N_EXPERTS), F32)
        return jnp.concatenate([group_part, gap, expert_part, tail], axis=1)

    w_route = route_lanes(moe_w_group[0], moe_w_router[0])
    w_route_hi = w_route.astype(BF16)
    w_route_lo = (w_route - w_route_hi.astype(F32)).astype(BF16)
    w_route = jnp.concatenate([w_route_hi, w_route_lo], axis=1)
    b_route = route_lanes(moe_b_group, moe_b_router)

    return {
        "ln_in_g": ln_in_g[None, :], "ln_in_b": ln_in_b[None, :],
        "w_in": w_in_p,
        "sinks": swa_sinks[0],
        "gq": mla_q_norm_g, "w_uq_t": wuq_t,
        "gkv": mla_kv_norm_g, "w_uk": wuk, "w_uv_t": wuv_t,
        "ga": jnp.concatenate(by_head(swa_out_norm_g, 1), axis=1), "gb": mla_out_norm_g,
        "w_o": w_o_p, "ln1_g": ln1_g, "ln1_b": ln1_b,
        "w_route": w_route, "b_route": b_route,
        "w_gate": moe_w_gate[0].astype(BF16), "w_up": moe_w_up[0].astype(BF16),
        "w_down": moe_w_down[0].astype(BF16),
        "ln2_g": ln2_g, "ln2_b": ln2_b,
    }


def kernel(x, meta_tokens, ln_in_g, ln_in_b, w_in, swa_sinks, mla_q_norm_g, mla_w_uq, mla_kv_norm_g, mla_w_ukv, swa_out_norm_g, mla_out_norm_g, w_o, ln1_g, ln1_b, moe_w_group, moe_b_group, moe_w_router, moe_b_router, moe_w_gate, moe_w_up, moe_w_down, ln2_g, ln2_b):
    B, S, D = x.shape
    p = _prep_params(ln_in_g, ln_in_b, w_in, swa_sinks, mla_q_norm_g, mla_w_uq, mla_kv_norm_g,
                     mla_w_ukv, swa_out_norm_g, mla_out_norm_g, w_o, ln1_g, ln1_b, moe_w_group,
                     moe_b_group, moe_w_router, moe_b_router, moe_w_gate, moe_w_up, moe_w_down,
                     ln2_g, ln2_b)
    T = B * S
    tm = min(PROJ_TILE, S)

    x_meta = jnp.concatenate([jnp.zeros((N_PAD, D), x.dtype), meta_tokens.astype(x.dtype)])[None]
    pos_meta = jnp.maximum(jnp.arange(BLOCK) - N_PAD, 0).astype(F32)
    pos_tok = (jnp.arange(S) + N_META).astype(F32)
    _, ka_m, va_m, _, km_m, vt_m = _in_proj(x_meta, _rope_tables(pos_meta), p, BLOCK)
    a_out, qt, km, vt = _in_proj_swa(x, _rope_tables(pos_tok), p, ka_m, va_m, tm)
    b_out = _mla(qt, km, vt, km_m[:, :, N_PAD:, :], vt_m[..., N_PAD:], tm)
    h1, route, counts = _out_proj(x, a_out, b_out, p, min(WIDE_TILE, S))
    h1 = h1.reshape(B * S, D)

    pos8, pad_start8, tile_lo, tile_hi, n_used, n_tiles = _dispatch_plan(route, counts)
    xs = _dispatch(h1, pos8, pad_start8, n_used, n_tiles, min(DISPATCH_TILE, T))
    ys = _experts(xs, tile_lo, tile_hi, n_used, p, n_tiles)
    out = _combine(h1, pos8, ys, p, min(COMBINE_TILE, T))
    return out.reshape(B, S, D)
```

```python
import functools

import jax
import jax.numpy as jnp
from jax import lax
from jax.experimental import pallas as pl
from jax.experimental.pallas import tpu as pltpu

D_MODEL = 1024
N_META = 16
BLOCK = 128
N_PAD = BLOCK - N_META
ROPE_THETA = 10000.0
SWA_HEADS = 8
SWA_KV_HEADS = 2
SWA_HEAD_DIM = 64
MLA_HEADS = 4
MLA_Q_RANK = 256
MLA_KV_RANK = 256
MLA_NOPE_DIM = 128
MLA_ROPE_DIM = 64
MLA_V_DIM = 128
SWA_Q_COLS = SWA_HEADS * SWA_HEAD_DIM
SWA_KV_COLS = SWA_KV_HEADS * SWA_HEAD_DIM
MLA_OUT_COLS = MLA_HEADS * MLA_V_DIM
N_GROUPS = 4
EXPERTS_PER_GROUP = 8
N_EXPERTS = N_GROUPS * EXPERTS_PER_GROUP
D_EXPERT = 256
LN_EPS = 1e-5
RMS_EPS = 1e-6
DEPTH = 1
ALPHA = (2.0 * DEPTH) ** 0.25
NEG = -1e30
LOG2_E = 1.4426950408889634

LANES = 128
COL_KA = SWA_Q_COLS
COL_VA = COL_KA + SWA_KV_COLS
COL_CQ = COL_VA + SWA_KV_COLS
COL_CKV = COL_CQ + MLA_Q_RANK
COL_KR = COL_CKV + MLA_KV_RANK
IN_COLS_PADDED = COL_KR + LANES
MLA_QK_PAD = 256
ROUTE_LANES = 128
EXPERT_LANE0 = 32
ROUTE_CLS, ROUTE_RANK = range(2)
ROUTE_ROWS = 8
PAIRS_PER_GROUP = EXPERTS_PER_GROUP * (EXPERTS_PER_GROUP - 1) // 2
N_CLASSES = N_GROUPS * PAIRS_PER_GROUP
MOE_TILE_ROWS = 256
PROJ_TILE = 1024
WIDE_TILE = 1024
DISPATCH_TILE = 4096
COMBINE_TILE = 1024
DMA_BURST = 8
VMEM_LIMIT = 56 * 1024 * 1024

SWA_HEAD_ORDER = (0, 4, 1, 5, 2, 6, 3, 7)

BF16 = jnp.bfloat16
F32 = jnp.float32


def _cparams(sem):
    return pltpu.CompilerParams(dimension_semantics=sem, vmem_limit_bytes=VMEM_LIMIT)


def _layer_norm(x, g, b):
    mu = jnp.mean(x, axis=-1, keepdims=True)
    xc = x - mu
    var = jnp.mean(xc * xc, axis=-1, keepdims=True)
    return xc * lax.rsqrt(var + LN_EPS) * g + b


def _rms_scale(x):
    return lax.rsqrt(jnp.mean(x * x, axis=-1, keepdims=True) + RMS_EPS)


def _rope128(x, cos, sin_signed, lane_lo):
    up = pltpu.roll(x, LANES - 32, 1)
    dn = pltpu.roll(x, 32, 1)
    return x * cos + jnp.where(lane_lo, up, dn) * sin_signed


def _dot(a, b):
    return jnp.dot(a, b, preferred_element_type=F32)


def _dot_nt(a, b):
    return lax.dot_general(a, b, (((1,), (1,)), ((), ())), preferred_element_type=F32)


def _in_proj_kernel(x_ref, g_ref, b_ref, w_ref, cos_ref, sin_ref, cost_ref, sint_ref, gq_ref,
                    wuqt_ref, gkv_ref, wuk_ref, wuvt_ref,
                    qa_ref, ka_ref, va_ref, qt_ref, km_ref, vt_ref):
    h = _layer_norm(x_ref[0], g_ref[...], b_ref[...])
    u = _dot(h.astype(BF16), w_ref[...])
    cos = cos_ref[...]
    sin = sin_ref[...]
    lane = lax.broadcasted_iota(jnp.int32, (1, LANES), 1)
    lane_lo = (lane % 64) < 32
    rope = functools.partial(_rope128, cos=cos, sin_signed=sin, lane_lo=lane_lo)

    swa_scale = SWA_HEAD_DIM ** -0.5
    for c in range(SWA_Q_COLS // LANES):
        qc = rope(u[:, c * LANES:(c + 1) * LANES]) * swa_scale
        qa_ref[0, :, c * LANES:(c + 1) * LANES] = qc.astype(BF16)
    ka_ref[0] = rope(u[:, COL_KA:COL_VA]).astype(BF16)
    va_ref[0] = u[:, COL_VA:COL_CQ].astype(BF16)

    cq = u[:, COL_CQ:COL_CKV]
    cqn = (cq * _rms_scale(cq) * gq_ref[...]).astype(BF16)
    ckv = u[:, COL_CKV:COL_KR]
    ckvn = (ckv * _rms_scale(ckv) * gkv_ref[...]).astype(BF16)
    k_nope = _dot(ckvn, wuk_ref[...])
    kr = rope(u[:, COL_KR:IN_COLS_PADDED]).astype(BF16)
    q_scale = (MLA_NOPE_DIM + MLA_ROPE_DIM) ** -0.5 * LOG2_E
    cost = cost_ref[...]
    sint = sint_ref[...]
    half = MLA_ROPE_DIM // 2
    for hd in range(MLA_HEADS):
        qt = _dot_nt(wuqt_ref[hd], cqn)
        qr = qt[MLA_NOPE_DIM:MLA_NOPE_DIM + MLA_ROPE_DIM]
        rot = jnp.concatenate([-qr[half:], qr[:half]], axis=0)
        qt_ref[0, hd, 0:MLA_NOPE_DIM, :] = (qt[:MLA_NOPE_DIM] * q_scale).astype(BF16)
        qt_ref[0, hd, MLA_NOPE_DIM:MLA_NOPE_DIM + MLA_ROPE_DIM, :] = (
            (qr * cost + rot * sint) * q_scale).astype(BF16)
        qt_ref[0, hd, MLA_NOPE_DIM + MLA_ROPE_DIM:, :] = (
            qt[MLA_NOPE_DIM + MLA_ROPE_DIM:]).astype(BF16)
        km_ref[0, hd, :, 0:LANES] = k_nope[:, hd * LANES:(hd + 1) * LANES].astype(BF16)
        km_ref[0, hd, :, LANES:2 * LANES] = kr
        vt_ref[0, hd, 0] = _dot_nt(wuvt_ref[hd], ckvn).astype(BF16)


def _in_proj(x, tables, p, tm):
    cos, sin, cos_t, sin_t = tables
    B, S, D = x.shape
    grid = (B, S // tm)
    full = lambda shape: pl.BlockSpec(shape, lambda b, i: (0,) * len(shape))
    out_shape = (
        jax.ShapeDtypeStruct((B, S, SWA_Q_COLS), BF16),
        jax.ShapeDtypeStruct((B, S, SWA_KV_COLS), BF16),
        jax.ShapeDtypeStruct((B, S, SWA_KV_COLS), BF16),
        jax.ShapeDtypeStruct((B, MLA_HEADS, MLA_QK_PAD, S), BF16),
        jax.ShapeDtypeStruct((B, MLA_HEADS, S, MLA_QK_PAD), BF16),
        jax.ShapeDtypeStruct((B, MLA_HEADS, S // tm, MLA_V_DIM, tm), BF16),
    )
    row = lambda w: pl.BlockSpec((1, tm, w), lambda b, i: (b, i, 0))
    head = lambda w: pl.BlockSpec((1, MLA_HEADS, tm, w), lambda b, i: (b, 0, i, 0))
    return pl.pallas_call(
        _in_proj_kernel,
        grid=grid,
        in_specs=[
            row(D),
            full((1, D)), full((1, D)),
            full((D, IN_COLS_PADDED)),
            pl.BlockSpec((tm, LANES), lambda b, i: (i, 0)),
            pl.BlockSpec((tm, LANES), lambda b, i: (i, 0)),
            pl.BlockSpec((MLA_ROPE_DIM, tm), lambda b, i: (0, i)),
            pl.BlockSpec((MLA_ROPE_DIM, tm), lambda b, i: (0, i)),
            full((1, MLA_Q_RANK)), full((MLA_HEADS, MLA_QK_PAD, MLA_Q_RANK)),
            full((1, MLA_KV_RANK)), full((MLA_KV_RANK, MLA_HEADS * MLA_NOPE_DIM)),
            full((MLA_HEADS, MLA_V_DIM, MLA_KV_RANK)),
        ],
        out_specs=(row(SWA_Q_COLS), row(SWA_KV_COLS), row(SWA_KV_COLS),
                   pl.BlockSpec((1, MLA_HEADS, MLA_QK_PAD, tm), lambda b, i: (b, 0, 0, i)),
                   head(MLA_QK_PAD),
                   pl.BlockSpec((1, MLA_HEADS, 1, MLA_V_DIM, tm), lambda b, i: (b, 0, i, 0, 0))),
        out_shape=out_shape,
        compiler_params=_cparams(("parallel", "parallel")),
        name="in_proj",
    )(x, p["ln_in_g"], p["ln_in_b"], p["w_in"], cos, sin, cos_t, sin_t, p["gq"], p["w_uq_t"],
      p["gkv"], p["w_uk"], p["w_uv_t"])


def _swa_blocks(sink_ref, qa_ref, kc_ref, vc_ref, kprev, vprev, first, o_ref, blocks_per_step):
    row = lax.broadcasted_iota(jnp.int32, (2 * BLOCK, BLOCK), 0) % BLOCK
    col = lax.broadcasted_iota(jnp.int32, (2 * BLOCK, BLOCK), 1)
    own = col <= row
    meta_ok = col >= jnp.where(first, N_PAD, 0)
    lane = lax.broadcasted_iota(jnp.int32, (1, LANES), 1)
    lane_kv0 = lane < SWA_HEAD_DIM
    top_rows = lax.broadcasted_iota(jnp.int32, (2 * BLOCK, 1), 0) < BLOCK

    for r in range(blocks_per_step):
        rows = slice(r * BLOCK, (r + 1) * BLOCK)
        if r == 0:
            kp, vp = kprev, vprev
        else:
            prev_rows = slice((r - 1) * BLOCK, r * BLOCK)
            kp, vp = kc_ref[0, prev_rows, :], vc_ref[0, prev_rows, :]
        kc, vc = kc_ref[0, rows, :], vc_ref[0, rows, :]
        for c in range(SWA_Q_COLS // LANES):
            qc = qa_ref[0, rows, c * LANES:(c + 1) * LANES]
            zero = jnp.zeros_like(qc)
            q2 = jnp.concatenate([jnp.where(lane_kv0, qc, zero),
                                  jnp.where(lane_kv0, zero, qc)], axis=0)
            s_prev = _dot_nt(q2, kp)
            if r == 0:
                s_prev = jnp.where(meta_ok, s_prev, NEG)
            s = jnp.where(own, _dot_nt(q2, kc), s_prev)
            sink = jnp.where(top_rows, sink_ref[c], sink_ref[c + 4])
            m = jnp.maximum(jnp.max(s, axis=-1, keepdims=True), sink)
            e = jnp.exp(s - m)
            denom = jnp.sum(e, axis=-1, keepdims=True) + jnp.exp(sink - m)
            pr = e * (1.0 / denom)
            o2 = (_dot(jnp.where(own, pr, 0.0).astype(BF16), vc)
                  + _dot(jnp.where(own, 0.0, pr).astype(BF16), vp))
            o = jnp.where(lane_kv0, o2[:BLOCK], o2[BLOCK:])
            o_ref[0, rows, c * LANES:(c + 1) * LANES] = o.astype(BF16)


def _in_proj_swa_kernel(sink_ref, x_ref, g_ref, b_ref, w_ref, cos_ref, sin_ref, cost_ref, sint_ref,
                        gq_ref, wuqt_ref, gkv_ref, wuk_ref, wuvt_ref, kmeta_ref, vmeta_ref,
                        a_ref, qt_ref, km_ref, vt_ref,
                        qa_s, ka_s, va_s, kprev_s, vprev_s):
    _in_proj_kernel(x_ref, g_ref, b_ref, w_ref, cos_ref, sin_ref, cost_ref, sint_ref, gq_ref,
                    wuqt_ref, gkv_ref, wuk_ref, wuvt_ref, qa_s, ka_s, va_s, qt_ref, km_ref, vt_ref)
    first = pl.program_id(1) == 0
    kprev = jnp.where(first, kmeta_ref[0], kprev_s[...])
    vprev = jnp.where(first, vmeta_ref[0], vprev_s[...])
    tm = qa_s.shape[1]
    _swa_blocks(sink_ref, qa_s, ka_s, va_s, kprev, vprev, first, a_ref, tm // BLOCK)
    kprev_s[...] = ka_s[0, tm - BLOCK:, :]
    vprev_s[...] = va_s[0, tm - BLOCK:, :]


def _in_proj_swa(x, tables, p, ka_meta, va_meta, tm):
    cos, sin, cos_t, sin_t = tables
    B, S, D = x.shape
    kvw = SWA_KV_COLS
    full = lambda shape: pl.BlockSpec(shape, lambda b, i: (0,) * len(shape))
    row = lambda w: pl.BlockSpec((1, tm, w), lambda b, i: (b, i, 0))
    out_shape = (
        jax.ShapeDtypeStruct((B, S, SWA_Q_COLS), BF16),
        jax.ShapeDtypeStruct((B, MLA_HEADS, MLA_QK_PAD, S), BF16),
        jax.ShapeDtypeStruct((B, MLA_HEADS, S, MLA_QK_PAD), BF16),
        jax.ShapeDtypeStruct((B, MLA_HEADS, S // tm, MLA_V_DIM, tm), BF16),
    )
    return pl.pallas_call(
        _in_proj_swa_kernel,
        grid=(B, S // tm),
        in_specs=[
            pl.BlockSpec(memory_space=pltpu.SMEM),
            row(D),
            full((1, D)), full((1, D)),
            full((D, IN_COLS_PADDED)),
            pl.BlockSpec((tm, LANES), lambda b, i: (i, 0)),
            pl.BlockSpec((tm, LANES), lambda b, i: (i, 0)),
            pl.BlockSpec((MLA_ROPE_DIM, tm), lambda b, i: (0, i)),
            pl.BlockSpec((MLA_ROPE_DIM, tm), lambda b, i: (0, i)),
            full((1, MLA_Q_RANK)), full((MLA_HEADS, MLA_QK_PAD, MLA_Q_RANK)),
            full((1, MLA_KV_RANK)), full((MLA_KV_RANK, MLA_HEADS * MLA_NOPE_DIM)),
            full((MLA_HEADS, MLA_V_DIM, MLA_KV_RANK)),
            full((1, BLOCK, kvw)), full((1, BLOCK, kvw)),
        ],
        out_specs=(row(SWA_Q_COLS),
                   pl.BlockSpec((1, MLA_HEADS, MLA_QK_PAD, tm), lambda b, i: (b, 0, 0, i)),
                   pl.BlockSpec((1, MLA_HEADS, tm, MLA_QK_PAD), lambda b, i: (b, 0, i, 0)),
                   pl.BlockSpec((1, MLA_HEADS, 1, MLA_V_DIM, tm), lambda b, i: (b, 0, i, 0, 0))),
        out_shape=out_shape,
        scratch_shapes=[pltpu.VMEM((1, tm, SWA_Q_COLS), BF16), pltpu.VMEM((1, tm, kvw), BF16),
                        pltpu.VMEM((1, tm, kvw), BF16), pltpu.VMEM((BLOCK, kvw), BF16),
                        pltpu.VMEM((BLOCK, kvw), BF16)],
        compiler_params=_cparams(("parallel", "arbitrary")),
        name="in_proj_swa",
    )(p["sinks"], x, p["ln_in_g"], p["ln_in_b"], p["w_in"], cos, sin, cos_t, sin_t, p["gq"],
      p["w_uq_t"], p["gkv"], p["w_uk"], p["w_uv_t"], ka_meta, va_meta)


def _mla_kernel(qt_ref, k_ref, vt_ref, kmeta_ref, vtmeta_ref, o_ref,
                s0a_ref, s0b_ref, s1a_ref, s1b_ref, smax_ref, m_ref, l_ref, acc_ref, *, tq):
    u = pl.program_id(2)
    qts = (qt_ref[0, 0, :, :tq], qt_ref[0, 0, :, tq:])
    bufs = ((s0a_ref, s0b_ref), (s1a_ref, s1b_ref))

    for g in range(2):
        s = _dot(kmeta_ref[0, 0], qts[g])
        m = jnp.max(s, axis=0, keepdims=True)
        p = jnp.exp2(s - m)
        m_ref[g] = m
        l_ref[g] = jnp.sum(p, axis=0, keepdims=True)
        acc_ref[g] = _dot(vtmeta_ref[0, 0, 0], p.astype(BF16))

    def scores_into(g, ab, j):
        start = pl.multiple_of(j * tq, tq)
        s = _dot(k_ref[0, 0, pl.ds(start, tq), :], qts[g])
        bufs[g][ab][...] = s
        smax_ref[g, ab] = jnp.max(s, axis=0, keepdims=True)

    def absorb(g, ab, j, causal=False):
        s = bufs[g][ab][...]
        if causal:
            key = lax.broadcasted_iota(jnp.int32, (tq, tq), 0)
            query = lax.broadcasted_iota(jnp.int32, (tq, tq), 1)
            s = jnp.where(key <= query, s, NEG)
            smax = jnp.max(s, axis=0, keepdims=True)
        else:
            smax = smax_ref[g, ab]
        m = m_ref[g]
        m_new = jnp.maximum(m, smax)
        alpha = jnp.exp2(m - m_new)
        p = jnp.exp2(s - m_new)
        m_ref[g] = m_new
        l_ref[g] = alpha * l_ref[g] + jnp.sum(p, axis=0, keepdims=True)
        acc_ref[g] = alpha * acc_ref[g] + _dot(vt_ref[0, 0, j], p.astype(BF16))

    buf_a, buf_b = 0, 1
    for g in range(2):
        scores_into(g, buf_a, 0)

    def pair(t, carry):
        scores_into(0, buf_b, 2 * t + 1)
        absorb(0, buf_a, 2 * t)
        scores_into(1, buf_b, 2 * t + 1)
        scores_into(0, buf_a, 2 * t + 2)
        absorb(1, buf_a, 2 * t)
        absorb(0, buf_b, 2 * t + 1)
        scores_into(1, buf_a, 2 * t + 2)
        absorb(1, buf_b, 2 * t + 1)
        return carry

    lax.fori_loop(0, u, pair, 0)

    scores_into(1, buf_b, 2 * u + 1)
    absorb(1, buf_a, 2 * u)
    absorb(0, buf_a, 2 * u, causal=True)
    absorb(1, buf_b, 2 * u + 1, causal=True)

    for g in range(2):
        o_ref[0, g * tq:(g + 1) * tq, :] = (acc_ref[g] * (1.0 / l_ref[g])).T.astype(BF16)


def _mla(qt, km, vt, km_meta, vt_meta, tq):
    B, H, _, S = qt.shape
    assert S % (2 * tq) == 0, "a step handles an (even, odd) pair of query tiles"
    return pl.pallas_call(
        functools.partial(_mla_kernel, tq=tq),
        grid=(B, H, S // (2 * tq)),
        in_specs=[
            pl.BlockSpec((1, 1, MLA_QK_PAD, 2 * tq), lambda b, h, u: (b, h, 0, u)),
            pl.BlockSpec((1, 1, S, MLA_QK_PAD), lambda b, h, u: (b, h, 0, 0)),
            pl.BlockSpec((1, 1, S // tq, MLA_V_DIM, tq), lambda b, h, u: (b, h, 0, 0, 0)),
            pl.BlockSpec((1, 1, N_META, MLA_QK_PAD), lambda b, h, u: (0, h, 0, 0)),
            pl.BlockSpec((1, 1, 1, MLA_V_DIM, N_META), lambda b, h, u: (0, h, 0, 0, 0)),
        ],
        out_specs=pl.BlockSpec((1, 2 * tq, MLA_V_DIM), lambda b, h, u: (b, u, h)),
        out_shape=jax.ShapeDtypeStruct((B, S, MLA_OUT_COLS), BF16),
        scratch_shapes=[pltpu.VMEM((tq, tq), F32), pltpu.VMEM((tq, tq), F32),
                        pltpu.VMEM((tq, tq), F32), pltpu.VMEM((tq, tq), F32),
                        pltpu.VMEM((2, 2, 1, tq), F32),
                        pltpu.VMEM((2, 1, tq), F32), pltpu.VMEM((2, 1, tq), F32),
                        pltpu.VMEM((2, MLA_V_DIM, tq), F32)],
        compiler_params=_cparams(("parallel", "parallel", "arbitrary")),
        name="mla_attn",
    )(qt, km, vt, km_meta, vt_meta)


def _out_proj_kernel(x_ref, a_ref, b_ref, gin_ref, bin_ref, ga_ref, gb_ref, wo_ref, g1_ref, b1_ref,
                     wrt_ref, brt_ref, h1_ref, route_ref, counts_out_ref, count_ref):
    h = _layer_norm(x_ref[0], gin_ref[...], bin_ref[...])
    a = a_ref[0].astype(F32)
    b = b_ref[0].astype(F32)
    an = (a * _rms_scale(a) * ga_ref[...]).astype(BF16)
    bn = (b * _rms_scale(b) * gb_ref[...]).astype(BF16)
    mix = _dot(jnp.concatenate([an, bn], axis=-1), wo_ref[...])
    h1 = _layer_norm(ALPHA * h + mix, g1_ref[...], b1_ref[...])
    h1_ref[0] = h1

    x_hi = h1.astype(BF16)
    x_lo = (h1 - x_hi.astype(F32)).astype(BF16)
    hi_both = _dot(x_hi, wrt_ref[...])
    logits = (hi_both[:, :ROUTE_LANES] + hi_both[:, ROUTE_LANES:]
              + _dot(x_lo, wrt_ref[:, :ROUTE_LANES]) + brt_ref[...])
    lane = lax.broadcasted_iota(jnp.int32, logits.shape, 1).astype(F32)
    ninf = -jnp.inf
    no_lane = float(ROUTE_LANES)
    gl = jnp.where(lane < N_GROUPS, logits, ninf)
    gmax = jnp.max(gl, axis=-1, keepdims=True)
    g_idx = jnp.min(jnp.where(gl == gmax, lane, no_lane), axis=-1, keepdims=True)
    lo = EXPERT_LANE0 + g_idx * EXPERTS_PER_GROUP
    el = jnp.where((lane >= lo) & (lane < lo + EXPERTS_PER_GROUP), logits, ninf)
    m1 = jnp.max(el, axis=-1, keepdims=True)
    i1 = jnp.min(jnp.where(el == m1, lane, no_lane), axis=-1, keepdims=True)
    el2 = jnp.where(lane == i1, ninf, el)
    m2 = jnp.max(el2, axis=-1, keepdims=True)
    i2 = jnp.min(jnp.where(el2 == m2, lane, no_lane), axis=-1, keepdims=True)

    a = jnp.minimum(i1, i2) - lo
    b = jnp.maximum(i1, i2) - lo
    cls = (g_idx * PAIRS_PER_GROUP + a * (2.0 * EXPERTS_PER_GROUP - 1.0 - a) * 0.5
           + (b - a - 1.0))

    @pl.when((pl.program_id(0) == 0) & (pl.program_id(1) == 0))
    def _():
        count_ref[...] = jnp.zeros_like(count_ref)

    tm = logits.shape[0]
    onehot = jnp.where(lane == cls, 1.0, 0.0)
    r_i = lax.broadcasted_iota(jnp.int32, (BLOCK, BLOCK), 0)
    c_i = lax.broadcasted_iota(jnp.int32, (BLOCK, BLOCK), 1)
    earlier = jnp.where(r_i > c_i, 1.0, 0.0).astype(BF16)
    running = count_ref[...]
    before = []
    for blk in range(tm // BLOCK):
        oh = onehot[blk * BLOCK:(blk + 1) * BLOCK]
        before.append(_dot(earlier, oh.astype(BF16)) + running)
        running = running + jnp.sum(oh, axis=0, keepdims=True)
    before = jnp.concatenate(before, axis=0)
    rank = jnp.sum(jnp.where(lane == cls, before, 0.0), axis=-1, keepdims=True)
    count_ref[...] = running
    counts_out_ref[...] = running

    route = jnp.where(lane == ROUTE_CLS, cls, 0.0)
    route = jnp.where(lane == ROUTE_RANK, rank, route)
    route_ref[0, 0] = route.T[:ROUTE_ROWS].astype(jnp.int32)


def _out_proj(x, a_out, b_out, p, tm):
    B, S, D = x.shape
    full = lambda shape: pl.BlockSpec(shape, lambda b, i: (0,) * len(shape))
    row = lambda w: pl.BlockSpec((1, tm, w), lambda b, i: (b, i, 0))
    return pl.pallas_call(
        _out_proj_kernel,
        grid=(B, S // tm),
        in_specs=[row(D), row(SWA_Q_COLS), row(MLA_OUT_COLS),
                  full((1, D)), full((1, D)), full((1, SWA_Q_COLS)), full((1, MLA_OUT_COLS)),
                  full((D, D)), full((1, D)), full((1, D)),
                  full((D, 2 * ROUTE_LANES)), full((1, ROUTE_LANES))],
        out_specs=(row(D), pl.BlockSpec((1, 1, ROUTE_ROWS, tm), lambda b, i: (b, i, 0, 0)),
                   full((1, ROUTE_LANES))),
        out_shape=(jax.ShapeDtypeStruct((B, S, D), F32),
                   jax.ShapeDtypeStruct((B, S // tm, ROUTE_ROWS, tm), jnp.int32),
                   jax.ShapeDtypeStruct((1, ROUTE_LANES), F32)),
        scratch_shapes=[pltpu.VMEM((1, ROUTE_LANES), F32)],
        compiler_params=_cparams(("arbitrary", "arbitrary")),
        name="out_proj_route",
    )(x, a_out, b_out, p["ln_in_g"], p["ln_in_b"], p["ga"], p["gb"], p["w_o"], p["ln1_g"],
      p["ln1_b"], p["w_route"], p["b_route"])


SLAB = D_MODEL // LANES


def _slab_copy(src_ref, src_tok8, dst_ref, dst_tok8, sem):
    return pltpu.make_async_copy(src_ref.at[pl.ds(pl.multiple_of(src_tok8, SLAB), SLAB)],
                                 dst_ref.at[pl.ds(pl.multiple_of(dst_tok8, SLAB), SLAB)], sem)


def _to_slabs(x, slab_ref, n):
    for s in range(SLAB):
        slab_ref[pl.ds(s, n, stride=SLAB), :] = x[:, s * LANES:(s + 1) * LANES]


def _from_slabs(slab_ref, n):
    return jnp.concatenate([slab_ref[pl.ds(s, n, stride=SLAB), :] for s in range(SLAB)], axis=1)


def _dispatch_kernel(padstart_ref, nu_ref, pos_ref, h1_ref, xs_ref, rows_ref, zero_ref, sem, *,
                     tm, n_tiles):
    tail = zero_ref.shape[0]

    def zero_fill(start):
        return pltpu.make_async_copy(
            zero_ref, xs_ref.at[pl.ds(pl.multiple_of(start, SLAB), tail)], sem)

    @pl.when(pl.program_id(0) == 0)
    def _():
        zero_ref[...] = jnp.zeros_like(zero_ref)
        for parity in range(2):
            tails = [zero_fill(padstart_ref[c]) for c in range(parity, N_CLASSES, 2)]
            for t in tails:
                t.start()
            for t in tails:
                t.wait()

        def fill_unused(j, carry):
            fill = zero_fill(j * tail)
            fill.start()
            fill.wait()
            return carry

        lax.fori_loop(nu_ref[0], n_tiles + 1, fill_unused, 0)

    _to_slabs(h1_ref[...], rows_ref, tm)

    def issue(r0, carry):
        for k in range(DMA_BURST):
            r = r0 * DMA_BURST + k
            _slab_copy(rows_ref, r * SLAB, xs_ref, pos_ref[0, 0, r], sem).start(priority=k % 2)
        return carry

    lax.fori_loop(0, tm // DMA_BURST, issue, 0)
    pltpu.make_async_copy(rows_ref, xs_ref.at[pl.ds(0, tm * SLAB)], sem).wait()


def _dispatch(h1, pos8, pad_start8, n_used, n_tiles, tm):
    T, D = h1.shape
    pos8 = pos8.reshape(T // tm, 1, tm)
    n_rows = (n_tiles + 1) * MOE_TILE_ROWS
    grid_spec = pltpu.PrefetchScalarGridSpec(
        num_scalar_prefetch=2,
        grid=(T // tm,),
        in_specs=[pl.BlockSpec((1, 1, tm), lambda t, ps, nu: (t, 0, 0),
                               memory_space=pltpu.SMEM),
                  pl.BlockSpec((tm, D), lambda t, ps, nu: (t, 0))],
        out_specs=pl.BlockSpec(memory_space=pl.ANY),
        scratch_shapes=[pltpu.VMEM((tm * SLAB, LANES), F32),
                        pltpu.VMEM((MOE_TILE_ROWS * SLAB, LANES), F32),
                        pltpu.SemaphoreType.DMA],
    )
    return pl.pallas_call(
        functools.partial(_dispatch_kernel, tm=tm, n_tiles=n_tiles),
        grid_spec=grid_spec,
        out_shape=jax.ShapeDtypeStruct((n_rows * SLAB, LANES), F32),
        compiler_params=_cparams(("arbitrary",)),
        name="moe_dispatch",
    )(pad_start8, n_used, pos8, h1)


def _experts_kernel(tlo_ref, thi_ref, nu_ref, x_ref, wg_lo_ref, wg_hi_ref, wu_lo_ref, wu_hi_ref,
                    wd_lo_ref, wd_hi_ref, wrt_ref, brt_ref, y_ref):
    j = pl.program_id(0)
    tr = MOE_TILE_ROWS

    @pl.when(j < nu_ref[0])
    def _():
        x = _from_slabs(x_ref, tr).astype(BF16)
        lo_e = tlo_ref[j]
        hi_e = thi_ref[j]
        logits = _dot(x, wrt_ref[...]) + brt_ref[...]
        lane = lax.broadcasted_iota(jnp.int32, logits.shape, 1)
        gl = jnp.where(lane < N_GROUPS, logits, -jnp.inf)
        ge = jnp.exp(gl - jnp.max(gl, axis=-1, keepdims=True))
        pick = lambda v, at: jnp.sum(jnp.where(lane == at, v, 0.0), axis=-1, keepdims=True)
        g_prob = pick(ge, lo_e // EXPERTS_PER_GROUP) / jnp.sum(ge, axis=-1, keepdims=True)
        e_lo = pick(logits, EXPERT_LANE0 + lo_e)
        e_hi = pick(logits, EXPERT_LANE0 + hi_e)
        gates = (g_prob / (1.0 + jnp.exp(e_hi - e_lo)), g_prob / (1.0 + jnp.exp(e_lo - e_hi)))

        y = None
        for gate, wg_ref, wu_ref, wd_ref in zip(gates, (wg_lo_ref, wg_hi_ref), (wu_lo_ref, wu_hi_ref),
                                                (wd_lo_ref, wd_hi_ref)):
            g = _dot(x, wg_ref[0])
            u = _dot(x, wu_ref[0])
            hid = g * (1.0 / (1.0 + jnp.exp(-g))) * u * gate
            y_k = _dot(hid.astype(BF16), wd_ref[0])
            y = y_k if y is None else y + y_k
        _to_slabs(y, y_ref, tr)


def _experts(xs, tile_lo, tile_hi, n_used, p, n_tiles):
    D = D_MODEL
    tr = MOE_TILE_ROWS
    last_used = lambda j, nu: jnp.minimum(j, nu[0] - 1)
    wgu = lambda pick: pl.BlockSpec((1, D, D_EXPERT),
                                    lambda j, tlo, thi, nu: (pick(tlo, thi)[j], 0, 0))
    wd = lambda pick: pl.BlockSpec((1, D_EXPERT, D),
                                   lambda j, tlo, thi, nu: (pick(tlo, thi)[j], 0, 0))
    lo = lambda tlo, thi: tlo
    hi = lambda tlo, thi: thi
    grid_spec = pltpu.PrefetchScalarGridSpec(
        num_scalar_prefetch=3,
        grid=(n_tiles,),
        in_specs=[pl.BlockSpec((tr * SLAB, LANES), lambda j, tlo, thi, nu: (last_used(j, nu), 0)),
                  wgu(lo), wgu(hi), wgu(lo), wgu(hi), wd(lo), wd(hi),
                  pl.BlockSpec((D, ROUTE_LANES), lambda j, tlo, thi, nu: (0, 0)),
                  pl.BlockSpec((1, ROUTE_LANES), lambda j, tlo, thi, nu: (0, 0))],
        out_specs=pl.BlockSpec((tr * SLAB, LANES), lambda j, tlo, thi, nu: (last_used(j, nu), 0)),
    )
    return pl.pallas_call(
        _experts_kernel,
        grid_spec=grid_spec,
        out_shape=jax.ShapeDtypeStruct(xs.shape, F32),
        input_output_aliases={3: 0},
        compiler_params=_cparams(("arbitrary",)),
        name="moe_experts",
    )(tile_lo, tile_hi, n_used, xs, p["w_gate"], p["w_gate"], p["w_up"], p["w_up"],
      p["w_down"], p["w_down"],
      p["w_route"], p["b_route"])


def _combine_kernel(pos_ref, pos_next_ref, h1_ref, ys_ref, g2_ref, b2_ref, o_ref,
                    ybuf_ref, sem, *, tm):
    t = pl.program_id(0)
    slot = t % 2

    def gather(idx_ref, dst_slot):
        def issue(r0, carry):
            for k in range(DMA_BURST):
                r = r0 * DMA_BURST + k
                _slab_copy(ys_ref, idx_ref[0, 0, r], ybuf_ref.at[dst_slot], r * SLAB,
                           sem.at[dst_slot]).start(priority=k % 2)
            return carry

        lax.fori_loop(0, tm // DMA_BURST, issue, 0)

    @pl.when(t == 0)
    def _():
        gather(pos_ref, 0)

    @pl.when(t + 1 < pl.num_programs(0))
    def _():
        gather(pos_next_ref, 1 - slot)

    pltpu.make_async_copy(ys_ref.at[pl.ds(0, tm * SLAB)], ybuf_ref.at[slot], sem.at[slot]).wait()
    ffn = _from_slabs(ybuf_ref.at[slot], tm)
    o_ref[...] = _layer_norm(ALPHA * h1_ref[...] + ffn, g2_ref[...], b2_ref[...])


def _combine(h1, pos8, ys, p, tm):
    T, D = h1.shape
    n_steps = T // tm
    pos8 = pos8.reshape(n_steps, 1, tm)
    full = lambda shape: pl.BlockSpec(shape, lambda t: (0,) * len(shape))
    return pl.pallas_call(
        functools.partial(_combine_kernel, tm=tm),
        grid=(n_steps,),
        in_specs=[pl.BlockSpec((1, 1, tm), lambda t: (t, 0, 0), memory_space=pltpu.SMEM),
                  pl.BlockSpec((1, 1, tm), lambda t: (jnp.minimum(t + 1, n_steps - 1), 0, 0),
                               memory_space=pltpu.SMEM),
                  pl.BlockSpec((tm, D), lambda t: (t, 0)),
                  pl.BlockSpec(memory_space=pl.ANY),
                  full((1, D)), full((1, D))],
        out_specs=pl.BlockSpec((tm, D), lambda t: (t, 0)),
        out_shape=jax.ShapeDtypeStruct((T, D), F32),
        scratch_shapes=[pltpu.VMEM((2, tm * SLAB, LANES), F32),
                        pltpu.SemaphoreType.DMA((2,))],
        compiler_params=_cparams(("arbitrary",)),
        name="moe_combine_ln2",
    )(pos8, pos8, h1, ys, p["ln2_g"], p["ln2_b"])


def _dispatch_plan(route, counts):
    T = route.shape[0] * route.shape[1] * route.shape[3]
    tr = MOE_TILE_ROWS
    n_tiles = T // tr + N_CLASSES
    cnt = counts[0, :N_CLASSES].astype(jnp.int32)
    padded = jnp.maximum((cnt + tr - 1) // tr, 1) * tr
    classes = jnp.arange(N_CLASSES, dtype=jnp.int32)
    seg_end = jnp.sum(jnp.where(classes[None, :] <= classes[:, None], padded[None, :], 0), axis=1)
    seg_start = seg_end - padded
    cls = route[:, :, ROUTE_CLS, :].reshape(T)
    rank = route[:, :, ROUTE_RANK, :].reshape(T)
    start_of = jnp.sum(jnp.where(cls[:, None] == classes, seg_start, 0), axis=-1)
    pos8 = (start_of + rank) * SLAB
    tile_row0 = jnp.arange(n_tiles, dtype=jnp.int32) * tr
    tile_cls = jnp.sum((seg_end[None, :] <= tile_row0[:, None]).astype(jnp.int32), axis=-1)
    tile_cls = jnp.minimum(tile_cls, N_CLASSES - 1)
    pairs = [(a, b) for a in range(EXPERTS_PER_GROUP) for b in range(a + 1, EXPERTS_PER_GROUP)]
    lo_of = jnp.asarray([g * EXPERTS_PER_GROUP + a for g in range(N_GROUPS) for a, _ in pairs],
                        jnp.int32)
    hi_of = jnp.asarray([g * EXPERTS_PER_GROUP + b for g in range(N_GROUPS) for _, b in pairs],
                        jnp.int32)
    is_cls = tile_cls[:, None] == classes
    tile_lo = jnp.sum(jnp.where(is_cls, lo_of, 0), axis=-1)
    tile_hi = jnp.sum(jnp.where(is_cls, hi_of, 0), axis=-1)
    n_used = (seg_end[-1:] // tr).astype(jnp.int32)
    return pos8, (seg_start + cnt) * SLAB, tile_lo, tile_hi, n_used, n_tiles


def _rope_tables(pos):
    d = SWA_HEAD_DIM
    inv_freq = ROPE_THETA ** (-jnp.arange(0, d, 2, dtype=F32) / d)
    ang = pos[:, None] * inv_freq[None, :]
    c, s = jnp.cos(ang), jnp.sin(ang)
    return (jnp.concatenate([c, c, c, c], axis=-1), jnp.concatenate([-s, s, -s, s], axis=-1),
            jnp.concatenate([c, c], axis=-1).T, jnp.concatenate([s, s], axis=-1).T)


def _prep_params(ln_in_g, ln_in_b, w_in, swa_sinks, mla_q_norm_g, mla_w_uq, mla_kv_norm_g, mla_w_ukv,
                 swa_out_norm_g, mla_out_norm_g, w_o, ln1_g, ln1_b, moe_w_group, moe_b_group,
                 moe_w_router, moe_b_router, moe_w_gate, moe_w_up, moe_w_down, ln2_g, ln2_b):
    hd = SWA_HEAD_DIM

    def by_head(t, axis):
        return [lax.slice_in_dim(t, h * hd, (h + 1) * hd, axis=axis) for h in SWA_HEAD_ORDER]

    w = w_in[0]
    w_pad = jnp.zeros((D_MODEL, IN_COLS_PADDED - w.shape[1]), w.dtype)
    w_in_p = jnp.concatenate(by_head(w, 1) + [w[:, SWA_Q_COLS:], w_pad], axis=1).astype(BF16)

    dqk = MLA_NOPE_DIM + MLA_ROPE_DIM
    wuq = mla_w_uq[0].reshape(MLA_Q_RANK, MLA_HEADS, dqk)
    wuq = jnp.pad(wuq, ((0, 0), (0, 0), (0, MLA_QK_PAD - dqk)))
    wuq_t = jnp.transpose(wuq, (1, 2, 0)).astype(BF16)
    wukv = mla_w_ukv[0].reshape(MLA_KV_RANK, MLA_HEADS, MLA_NOPE_DIM + MLA_V_DIM)
    wuk = wukv[:, :, :MLA_NOPE_DIM].reshape(MLA_KV_RANK, MLA_HEADS * MLA_NOPE_DIM).astype(BF16)
    wuv_t = jnp.transpose(wukv[:, :, MLA_NOPE_DIM:], (1, 2, 0)).astype(BF16)

    w_o0 = w_o[0]
    w_o_p = jnp.concatenate(by_head(w_o0, 0) + [w_o0[SWA_Q_COLS:]], axis=0).astype(BF16)

    def route_lanes(group_part, expert_part):
        rows = group_part.shape[0]
        gap = jnp.zeros((rows, EXPERT_LANE0 - N_GROUPS), F32)
        tail = jnp.zeros((rows, ROUTE_LANES - EXPERT_LANE0 - N_EXPERTS), F32)
        return jnp.concatenate([group_part, gap, expert_part, tail], axis=1)

    w_route = route_lanes(moe_w_group[0], moe_w_router[0])
    w_route_hi = w_route.astype(BF16)
    w_route_lo = (w_route - w_route_hi.astype(F32)).astype(BF16)
    w_route = jnp.concatenate([w_route_hi, w_route_lo], axis=1)
    b_route = route_lanes(moe_b_group, moe_b_router)

    return {
        "ln_in_g": ln_in_g[None, :], "ln_in_b": ln_in_b[None, :],
        "w_in": w_in_p,
        "sinks": swa_sinks[0],
        "gq": mla_q_norm_g, "w_uq_t": wuq_t,
        "gkv": mla_kv_norm_g, "w_uk": wuk, "w_uv_t": wuv_t,
        "ga": jnp.concatenate(by_head(swa_out_norm_g, 1), axis=1), "gb": mla_out_norm_g,
        "w_o": w_o_p, "ln1_g": ln1_g, "ln1_b": ln1_b,
        "w_route": w_route, "b_route": b_route,
        "w_gate": moe_w_gate[0].astype(BF16), "w_up": moe_w_up[0].astype(BF16),
        "w_down": moe_w_down[0].astype(BF16),
        "ln2_g": ln2_g, "ln2_b": ln2_b,
    }


def kernel(x, meta_tokens, ln_in_g, ln_in_b, w_in, swa_sinks, mla_q_norm_g, mla_w_uq, mla_kv_norm_g, mla_w_ukv, swa_out_norm_g, mla_out_norm_g, w_o, ln1_g, ln1_b, moe_w_group, moe_b_group, moe_w_router, moe_b_router, moe_w_gate, moe_w_up, moe_w_down, ln2_g, ln2_b):
    B, S, D = x.shape
    p = _prep_params(ln_in_g, ln_in_b, w_in, swa_sinks, mla_q_norm_g, mla_w_uq, mla_kv_norm_g,
                     mla_w_ukv, swa_out_norm_g, mla_out_norm_g, w_o, ln1_g, ln1_b, moe_w_group,
                     moe_b_group, moe_w_router, moe_b_router, moe_w_gate, moe_w_up, moe_w_down,
                     ln2_g, ln2_b)
    T = B * S
    tm = min(PROJ_TILE, S)

    x_meta = jnp.concatenate([jnp.zeros((N_PAD, D), x.dtype), meta_tokens.astype(x.dtype)])[None]
    pos_meta = jnp.maximum(jnp.arange(BLOCK) - N_PAD, 0).astype(F32)
    pos_tok = (jnp.arange(S) + N_META).astype(F32)
    _, ka_m, va_m, _, km_m, vt_m = _in_proj(x_meta, _rope_tables(pos_meta), p, BLOCK)
    a_out, qt, km, vt = _in_proj_swa(x, _rope_tables(pos_tok), p, ka_m, va_m, tm)
    b_out = _mla(qt, km, vt, km_m[:, :, N_PAD:, :], vt_m[..., N_PAD:], tm)
    h1, route, counts = _out_proj(x, a_out, b_out, p, min(WIDE_TILE, S))
    h1 = h1.reshape(B * S, D)

    pos8, pad_start8, tile_lo, tile_hi, n_used, n_tiles = _dispatch_plan(route, counts)
    xs = _dispatch(h1, pos8, pad_start8, n_used, n_tiles, min(DISPATCH_TILE, T))
    ys = _experts(xs, tile_lo, tile_hi, n_used, p, n_tiles)
    out = _combine(h1, pos8, ys, p, min(COMBINE_TILE, T))
    return out.reshape(B, S, D)
```

```python
import functools

import jax
import jax.numpy as jnp
from jax import lax
from jax.experimental import pallas as pl
from jax.experimental.pallas import tpu as pltpu

D_MODEL = 1024
N_META = 16
BLOCK = 128
N_PAD = BLOCK - N_META
ROPE_THETA = 10000.0
SWA_HEADS = 8
SWA_KV_HEADS = 2
SWA_HEAD_DIM = 64
MLA_HEADS = 4
MLA_Q_RANK = 256
MLA_KV_RANK = 256
MLA_NOPE_DIM = 128
MLA_ROPE_DIM = 64
MLA_V_DIM = 128
SWA_Q_COLS = SWA_HEADS * SWA_HEAD_DIM
SWA_KV_COLS = SWA_KV_HEADS * SWA_HEAD_DIM
MLA_OUT_COLS = MLA_HEADS * MLA_V_DIM
N_GROUPS = 4
EXPERTS_PER_GROUP = 8
N_EXPERTS = N_GROUPS * EXPERTS_PER_GROUP
D_EXPERT = 256
LN_EPS = 1e-5
RMS_EPS = 1e-6
DEPTH = 1
ALPHA = (2.0 * DEPTH) ** 0.25
NEG = -1e30
LOG2_E = 1.4426950408889634

LANES = 128
COL_KA = SWA_Q_COLS
COL_VA = COL_KA + SWA_KV_COLS
COL_CQ = COL_VA + SWA_KV_COLS
COL_CKV = COL_CQ + MLA_Q_RANK
COL_KR = COL_CKV + MLA_KV_RANK
IN_COLS_PADDED = COL_KR + LANES
MLA_QK_PAD = 256
ROUTE_LANES = 128
EXPERT_LANE0 = 32
ROUTE_CLS, ROUTE_RANK = range(2)
ROUTE_ROWS = 8
PAIRS_PER_GROUP = EXPERTS_PER_GROUP * (EXPERTS_PER_GROUP - 1) // 2
N_CLASSES = N_GROUPS * PAIRS_PER_GROUP
MOE_TILE_ROWS = 256
PROJ_TILE = 1024
WIDE_TILE = 1024
DISPATCH_TILE = 2048
COMBINE_TILE = 1024
DMA_BURST = 8
VMEM_LIMIT = 56 * 1024 * 1024

SWA_HEAD_ORDER = (0, 4, 1, 5, 2, 6, 3, 7)

BF16 = jnp.bfloat16
F32 = jnp.float32


def _cparams(sem):
    return pltpu.CompilerParams(dimension_semantics=sem, vmem_limit_bytes=VMEM_LIMIT)


def _layer_norm(x, g, b):
    mu = jnp.mean(x, axis=-1, keepdims=True)
    xc = x - mu
    var = jnp.mean(xc * xc, axis=-1, keepdims=True)
    return xc * lax.rsqrt(var + LN_EPS) * g + b


def _rms_scale(x):
    return lax.rsqrt(jnp.mean(x * x, axis=-1, keepdims=True) + RMS_EPS)


def _rope128(x, cos, sin_signed, lane_lo):
    up = pltpu.roll(x, LANES - 32, 1)
    dn = pltpu.roll(x, 32, 1)
    return x * cos + jnp.where(lane_lo, up, dn) * sin_signed


def _dot(a, b):
    return jnp.dot(a, b, preferred_element_type=F32)


def _dot_nt(a, b):
    return lax.dot_general(a, b, (((1,), (1,)), ((), ())), preferred_element_type=F32)


def _in_proj_kernel(x_ref, g_ref, b_ref, w_ref, cos_ref, sin_ref, cost_ref, sint_ref, gq_ref,
                    wuqt_ref, gkv_ref, wuk_ref, wuvt_ref,
                    qa_ref, ka_ref, va_ref, qt_ref, km_ref, vt_ref):
    h = _layer_norm(x_ref[0], g_ref[...], b_ref[...])
    u = _dot(h.astype(BF16), w_ref[...])
    cos = cos_ref[...]
    sin = sin_ref[...]
    lane = lax.broadcasted_iota(jnp.int32, (1, LANES), 1)
    lane_lo = (lane % 64) < 32
    rope = functools.partial(_rope128, cos=cos, sin_signed=sin, lane_lo=lane_lo)

    swa_scale = SWA_HEAD_DIM ** -0.5
    for c in range(SWA_Q_COLS // LANES):
        qc = rope(u[:, c * LANES:(c + 1) * LANES]) * swa_scale
        qa_ref[0, :, c * LANES:(c + 1) * LANES] = qc.astype(BF16)
    ka_ref[0] = rope(u[:, COL_KA:COL_VA]).astype(BF16)
    va_ref[0] = u[:, COL_VA:COL_CQ].astype(BF16)

    cq = u[:, COL_CQ:COL_CKV]
    cqn = (cq * _rms_scale(cq) * gq_ref[...]).astype(BF16)
    ckv = u[:, COL_CKV:COL_KR]
    ckvn = (ckv * _rms_scale(ckv) * gkv_ref[...]).astype(BF16)
    k_nope = _dot(ckvn, wuk_ref[...])
    kr = rope(u[:, COL_KR:IN_COLS_PADDED]).astype(BF16)
    q_scale = (MLA_NOPE_DIM + MLA_ROPE_DIM) ** -0.5 * LOG2_E
    cost = cost_ref[...]
    sint = sint_ref[...]
    half = MLA_ROPE_DIM // 2
    for hd in range(MLA_HEADS):
        qt = _dot_nt(wuqt_ref[hd], cqn)
        qr = qt[MLA_NOPE_DIM:MLA_NOPE_DIM + MLA_ROPE_DIM]
        rot = jnp.concatenate([-qr[half:], qr[:half]], axis=0)
        qt_ref[0, hd, 0:MLA_NOPE_DIM, :] = (qt[:MLA_NOPE_DIM] * q_scale).astype(BF16)
        qt_ref[0, hd, MLA_NOPE_DIM:MLA_NOPE_DIM + MLA_ROPE_DIM, :] = (
            (qr * cost + rot * sint) * q_scale).astype(BF16)
        qt_ref[0, hd, MLA_NOPE_DIM + MLA_ROPE_DIM:, :] = (
            qt[MLA_NOPE_DIM + MLA_ROPE_DIM:]).astype(BF16)
        km_ref[0, hd, :, 0:LANES] = k_nope[:, hd * LANES:(hd + 1) * LANES].astype(BF16)
        km_ref[0, hd, :, LANES:2 * LANES] = kr
        vt_ref[0, hd, 0] = _dot_nt(wuvt_ref[hd], ckvn).astype(BF16)


def _in_proj(x, tables, p, tm):
    cos, sin, cos_t, sin_t = tables
    B, S, D = x.shape
    grid = (B, S // tm)
    full = lambda shape: pl.BlockSpec(shape, lambda b, i: (0,) * len(shape))
    out_shape = (
        jax.ShapeDtypeStruct((B, S, SWA_Q_COLS), BF16),
        jax.ShapeDtypeStruct((B, S, SWA_KV_COLS), BF16),
        jax.ShapeDtypeStruct((B, S, SWA_KV_COLS), BF16),
        jax.ShapeDtypeStruct((B, MLA_HEADS, MLA_QK_PAD, S), BF16),
        jax.ShapeDtypeStruct((B, MLA_HEADS, S, MLA_QK_PAD), BF16),
        jax.ShapeDtypeStruct((B, MLA_HEADS, S // tm, MLA_V_DIM, tm), BF16),
    )
    row = lambda w: pl.BlockSpec((1, tm, w), lambda b, i: (b, i, 0))
    head = lambda w: pl.BlockSpec((1, MLA_HEADS, tm, w), lambda b, i: (b, 0, i, 0))
    return pl.pallas_call(
        _in_proj_kernel,
        grid=grid,
        in_specs=[
            row(D),
            full((1, D)), full((1, D)),
            full((D, IN_COLS_PADDED)),
            pl.BlockSpec((tm, LANES), lambda b, i: (i, 0)),
            pl.BlockSpec((tm, LANES), lambda b, i: (i, 0)),
            pl.BlockSpec((MLA_ROPE_DIM, tm), lambda b, i: (0, i)),
            pl.BlockSpec((MLA_ROPE_DIM, tm), lambda b, i: (0, i)),
            full((1, MLA_Q_RANK)), full((MLA_HEADS, MLA_QK_PAD, MLA_Q_RANK)),
            full((1, MLA_KV_RANK)), full((MLA_KV_RANK, MLA_HEADS * MLA_NOPE_DIM)),
            full((MLA_HEADS, MLA_V_DIM, MLA_KV_RANK)),
        ],
        out_specs=(row(SWA_Q_COLS), row(SWA_KV_COLS), row(SWA_KV_COLS),
                   pl.BlockSpec((1, MLA_HEADS, MLA_QK_PAD, tm), lambda b, i: (b, 0, 0, i)),
                   head(MLA_QK_PAD),
                   pl.BlockSpec((1, MLA_HEADS, 1, MLA_V_DIM, tm), lambda b, i: (b, 0, i, 0, 0))),
        out_shape=out_shape,
        compiler_params=_cparams(("parallel", "parallel")),
        name="in_proj",
    )(x, p["ln_in_g"], p["ln_in_b"], p["w_in"], cos, sin, cos_t, sin_t, p["gq"], p["w_uq_t"],
      p["gkv"], p["w_uk"], p["w_uv_t"])


def _swa_blocks(sink_ref, qa_ref, kc_ref, vc_ref, kprev, vprev, first, o_ref, blocks_per_step):
    row = lax.broadcasted_iota(jnp.int32, (2 * BLOCK, BLOCK), 0) % BLOCK
    col = lax.broadcasted_iota(jnp.int32, (2 * BLOCK, BLOCK), 1)
    own = col <= row
    meta_ok = col >= jnp.where(first, N_PAD, 0)
    lane = lax.broadcasted_iota(jnp.int32, (1, LANES), 1)
    lane_kv0 = lane < SWA_HEAD_DIM
    top_rows = lax.broadcasted_iota(jnp.int32, (2 * BLOCK, 1), 0) < BLOCK

    for r in range(blocks_per_step):
        rows = slice(r * BLOCK, (r + 1) * BLOCK)
        if r == 0:
            kp, vp = kprev, vprev
        else:
            prev_rows = slice((r - 1) * BLOCK, r * BLOCK)
            kp, vp = kc_ref[0, prev_rows, :], vc_ref[0, prev_rows, :]
        kc, vc = kc_ref[0, rows, :], vc_ref[0, rows, :]
        for c in range(SWA_Q_COLS // LANES):
            qc = qa_ref[0, rows, c * LANES:(c + 1) * LANES]
            zero = jnp.zeros_like(qc)
            q2 = jnp.concatenate([jnp.where(lane_kv0, qc, zero),
                                  jnp.where(lane_kv0, zero, qc)], axis=0)
            s_prev = _dot_nt(q2, kp)
            if r == 0:
                s_prev = jnp.where(meta_ok, s_prev, NEG)
            s = jnp.where(own, _dot_nt(q2, kc), s_prev)
            sink = jnp.where(top_rows, sink_ref[c], sink_ref[c + 4])
            m = jnp.maximum(jnp.max(s, axis=-1, keepdims=True), sink)
            e = jnp.exp(s - m)
            denom = jnp.sum(e, axis=-1, keepdims=True) + jnp.exp(sink - m)
            pr = e * (1.0 / denom)
            o2 = (_dot(jnp.where(own, pr, 0.0).astype(BF16), vc)
                  + _dot(jnp.where(own, 0.0, pr).astype(BF16), vp))
            o = jnp.where(lane_kv0, o2[:BLOCK], o2[BLOCK:])
            o_ref[0, rows, c * LANES:(c + 1) * LANES] = o.astype(BF16)


def _in_proj_swa_kernel(sink_ref, x_ref, g_ref, b_ref, w_ref, cos_ref, sin_ref, cost_ref, sint_ref,
                        gq_ref, wuqt_ref, gkv_ref, wuk_ref, wuvt_ref, kmeta_ref, vmeta_ref,
                        a_ref, qt_ref, km_ref, vt_ref,
                        qa_s, ka_s, va_s, kprev_s, vprev_s):
    _in_proj_kernel(x_ref, g_ref, b_ref, w_ref, cos_ref, sin_ref, cost_ref, sint_ref, gq_ref,
                    wuqt_ref, gkv_ref, wuk_ref, wuvt_ref, qa_s, ka_s, va_s, qt_ref, km_ref, vt_ref)
    first = pl.program_id(1) == 0
    kprev = jnp.where(first, kmeta_ref[0], kprev_s[...])
    vprev = jnp.where(first, vmeta_ref[0], vprev_s[...])
    tm = qa_s.shape[1]
    _swa_blocks(sink_ref, qa_s, ka_s, va_s, kprev, vprev, first, a_ref, tm // BLOCK)
    kprev_s[...] = ka_s[0, tm - BLOCK:, :]
    vprev_s[...] = va_s[0, tm - BLOCK:, :]


def _in_proj_swa(x, tables, p, ka_meta, va_meta, tm):
    cos, sin, cos_t, sin_t = tables
    B, S, D = x.shape
    kvw = SWA_KV_COLS
    full = lambda shape: pl.BlockSpec(shape, lambda b, i: (0,) * len(shape))
    row = lambda w: pl.BlockSpec((1, tm, w), lambda b, i: (b, i, 0))
    out_shape = (
        jax.ShapeDtypeStruct((B, S, SWA_Q_COLS), BF16),
        jax.ShapeDtypeStruct((B, MLA_HEADS, MLA_QK_PAD, S), BF16),
        jax.ShapeDtypeStruct((B, MLA_HEADS, S, MLA_QK_PAD), BF16),
        jax.ShapeDtypeStruct((B, MLA_HEADS, S // tm, MLA_V_DIM, tm), BF16),
    )
    return pl.pallas_call(
        _in_proj_swa_kernel,
        grid=(B, S // tm),
        in_specs=[
            pl.BlockSpec(memory_space=pltpu.SMEM),
            row(D),
            full((1, D)), full((1, D)),
            full((D, IN_COLS_PADDED)),
            pl.BlockSpec((tm, LANES), lambda b, i: (i, 0)),
            pl.BlockSpec((tm, LANES), lambda b, i: (i, 0)),
            pl.BlockSpec((MLA_ROPE_DIM, tm), lambda b, i: (0, i)),
            pl.BlockSpec((MLA_ROPE_DIM, tm), lambda b, i: (0, i)),
            full((1, MLA_Q_RANK)), full((MLA_HEADS, MLA_QK_PAD, MLA_Q_RANK)),
            full((1, MLA_KV_RANK)), full((MLA_KV_RANK, MLA_HEADS * MLA_NOPE_DIM)),
            full((MLA_HEADS, MLA_V_DIM, MLA_KV_RANK)),
            full((1, BLOCK, kvw)), full((1, BLOCK, kvw)),
        ],
        out_specs=(row(SWA_Q_COLS),
                   pl.BlockSpec((1, MLA_HEADS, MLA_QK_PAD, tm), lambda b, i: (b, 0, 0, i)),
                   pl.BlockSpec((1, MLA_HEADS, tm, MLA_QK_PAD), lambda b, i: (b, 0, i, 0)),
                   pl.BlockSpec((1, MLA_HEADS, 1, MLA_V_DIM, tm), lambda b, i: (b, 0, i, 0, 0))),
        out_shape=out_shape,
        scratch_shapes=[pltpu.VMEM((1, tm, SWA_Q_COLS), BF16), pltpu.VMEM((1, tm, kvw), BF16),
                        pltpu.VMEM((1, tm, kvw), BF16), pltpu.VMEM((BLOCK, kvw), BF16),
                        pltpu.VMEM((BLOCK, kvw), BF16)],
        compiler_params=_cparams(("parallel", "arbitrary")),
        name="in_proj_swa",
    )(p["sinks"], x, p["ln_in_g"], p["ln_in_b"], p["w_in"], cos, sin, cos_t, sin_t, p["gq"],
      p["w_uq_t"], p["gkv"], p["w_uk"], p["w_uv_t"], ka_meta, va_meta)


def _mla_kernel(qt_ref, k_ref, vt_ref, kmeta_ref, vtmeta_ref, o_ref,
                s0a_ref, s0b_ref, s1a_ref, s1b_ref, smax_ref, m_ref, l_ref, acc_ref, *, tq):
    u = pl.program_id(2)
    qts = (qt_ref[0, 0, :, :tq], qt_ref[0, 0, :, tq:])
    bufs = ((s0a_ref, s0b_ref), (s1a_ref, s1b_ref))

    for g in range(2):
        s = _dot(kmeta_ref[0, 0], qts[g])
        m = jnp.max(s, axis=0, keepdims=True)
        p = jnp.exp2(s - m)
        m_ref[g] = m
        l_ref[g] = jnp.sum(p, axis=0, keepdims=True)
        acc_ref[g] = _dot(vtmeta_ref[0, 0, 0], p.astype(BF16))

    def scores_into(g, ab, j):
        start = pl.multiple_of(j * tq, tq)
        s = _dot(k_ref[0, 0, pl.ds(start, tq), :], qts[g])
        bufs[g][ab][...] = s
        smax_ref[g, ab] = jnp.max(s, axis=0, keepdims=True)

    def absorb(g, ab, j, causal=False):
        s = bufs[g][ab][...]
        if causal:
            key = lax.broadcasted_iota(jnp.int32, (tq, tq), 0)
            query = lax.broadcasted_iota(jnp.int32, (tq, tq), 1)
            s = jnp.where(key <= query, s, NEG)
            smax = jnp.max(s, axis=0, keepdims=True)
        else:
            smax = smax_ref[g, ab]
        m = m_ref[g]
        m_new = jnp.maximum(m, smax)
        alpha = jnp.exp2(m - m_new)
        p = jnp.exp2(s - m_new)
        m_ref[g] = m_new
        l_ref[g] = alpha * l_ref[g] + jnp.sum(p, axis=0, keepdims=True)
        acc_ref[g] = alpha * acc_ref[g] + _dot(vt_ref[0, 0, j], p.astype(BF16))

    buf_a, buf_b = 0, 1
    for g in range(2):
        scores_into(g, buf_a, 0)

    def pair(t, carry):
        scores_into(0, buf_b, 2 * t + 1)
        absorb(0, buf_a, 2 * t)
        scores_into(1, buf_b, 2 * t + 1)
        scores_into(0, buf_a, 2 * t + 2)
        absorb(1, buf_a, 2 * t)
        absorb(0, buf_b, 2 * t + 1)
        scores_into(1, buf_a, 2 * t + 2)
        absorb(1, buf_b, 2 * t + 1)
        return carry

    lax.fori_loop(0, u, pair, 0)

    scores_into(1, buf_b, 2 * u + 1)
    absorb(1, buf_a, 2 * u)
    absorb(0, buf_a, 2 * u, causal=True)
    absorb(1, buf_b, 2 * u + 1, causal=True)

    for g in range(2):
        o_ref[0, g * tq:(g + 1) * tq, :] = (acc_ref[g] * (1.0 / l_ref[g])).T.astype(BF16)


def _mla(qt, km, vt, km_meta, vt_meta, tq):
    B, H, _, S = qt.shape
    assert S % (2 * tq) == 0, "a step handles an (even, odd) pair of query tiles"
    return pl.pallas_call(
        functools.partial(_mla_kernel, tq=tq),
        grid=(B, H, S // (2 * tq)),
        in_specs=[
            pl.BlockSpec((1, 1, MLA_QK_PAD, 2 * tq), lambda b, h, u: (b, h, 0, u)),
            pl.BlockSpec((1, 1, S, MLA_QK_PAD), lambda b, h, u: (b, h, 0, 0)),
            pl.BlockSpec((1, 1, S // tq, MLA_V_DIM, tq), lambda b, h, u: (b, h, 0, 0, 0)),
            pl.BlockSpec((1, 1, N_META, MLA_QK_PAD), lambda b, h, u: (0, h, 0, 0)),
            pl.BlockSpec((1, 1, 1, MLA_V_DIM, N_META), lambda b, h, u: (0, h, 0, 0, 0)),
        ],
        out_specs=pl.BlockSpec((1, 2 * tq, MLA_V_DIM), lambda b, h, u: (b, u, h)),
        out_shape=jax.ShapeDtypeStruct((B, S, MLA_OUT_COLS), BF16),
        scratch_shapes=[pltpu.VMEM((tq, tq), F32), pltpu.VMEM((tq, tq), F32),
                        pltpu.VMEM((tq, tq), F32), pltpu.VMEM((tq, tq), F32),
                        pltpu.VMEM((2, 2, 1, tq), F32),
                        pltpu.VMEM((2, 1, tq), F32), pltpu.VMEM((2, 1, tq), F32),
                        pltpu.VMEM((2, MLA_V_DIM, tq), F32)],
        compiler_params=_cparams(("parallel", "parallel", "arbitrary")),
        name="mla_attn",
    )(qt, km, vt, km_meta, vt_meta)


def _out_proj_kernel(x_ref, a_ref, b_ref, gin_ref, bin_ref, ga_ref, gb_ref, wo_ref, g1_ref, b1_ref,
                     wrt_ref, brt_ref, h1_ref, route_ref, counts_out_ref, count_ref):
    h = _layer_norm(x_ref[0], gin_ref[...], bin_ref[...])
    a = a_ref[0].astype(F32)
    b = b_ref[0].astype(F32)
    an = (a * _rms_scale(a) * ga_ref[...]).astype(BF16)
    bn = (b * _rms_scale(b) * gb_ref[...]).astype(BF16)
    mix = _dot(jnp.concatenate([an, bn], axis=-1), wo_ref[...])
    h1 = _layer_norm(ALPHA * h + mix, g1_ref[...], b1_ref[...])
    h1_ref[0] = h1

    x_hi = h1.astype(BF16)
    x_lo = (h1 - x_hi.astype(F32)).astype(BF16)
    hi_both = _dot(x_hi, wrt_ref[...])
    logits = (hi_both[:, :ROUTE_LANES] + hi_both[:, ROUTE_LANES:]
              + _dot(x_lo, wrt_ref[:, :ROUTE_LANES]) + brt_ref[...])
    lane = lax.broadcasted_iota(jnp.int32, logits.shape, 1).astype(F32)
    ninf = -jnp.inf
    no_lane = float(ROUTE_LANES)
    gl = jnp.where(lane < N_GROUPS, logits, ninf)
    gmax = jnp.max(gl, axis=-1, keepdims=True)
    g_idx = jnp.min(jnp.where(gl == gmax, lane, no_lane), axis=-1, keepdims=True)
    lo = EXPERT_LANE0 + g_idx * EXPERTS_PER_GROUP
    el = jnp.where((lane >= lo) & (lane < lo + EXPERTS_PER_GROUP), logits, ninf)
    m1 = jnp.max(el, axis=-1, keepdims=True)
    i1 = jnp.min(jnp.where(el == m1, lane, no_lane), axis=-1, keepdims=True)
    el2 = jnp.where(lane == i1, ninf, el)
    m2 = jnp.max(el2, axis=-1, keepdims=True)
    i2 = jnp.min(jnp.where(el2 == m2, lane, no_lane), axis=-1, keepdims=True)

    a = jnp.minimum(i1, i2) - lo
    b = jnp.maximum(i1, i2) - lo
    cls = (g_idx * PAIRS_PER_GROUP + a * (2.0 * EXPERTS_PER_GROUP - 1.0 - a) * 0.5
           + (b - a - 1.0))

    @pl.when((pl.program_id(0) == 0) & (pl.program_id(1) == 0))
    def _():
        count_ref[...] = jnp.zeros_like(count_ref)

    tm = logits.shape[0]
    onehot = jnp.where(lane == cls, 1.0, 0.0)
    r_i = lax.broadcasted_iota(jnp.int32, (BLOCK, BLOCK), 0)
    c_i = lax.broadcasted_iota(jnp.int32, (BLOCK, BLOCK), 1)
    earlier = jnp.where(r_i > c_i, 1.0, 0.0).astype(BF16)
    running = count_ref[...]
    before = []
    for blk in range(tm // BLOCK):
        oh = onehot[blk * BLOCK:(blk + 1) * BLOCK]
        before.append(_dot(earlier, oh.astype(BF16)) + running)
        running = running + jnp.sum(oh, axis=0, keepdims=True)
    before = jnp.concatenate(before, axis=0)
    rank = jnp.sum(jnp.where(lane == cls, before, 0.0), axis=-1, keepdims=True)
    count_ref[...] = running
    counts_out_ref[...] = running

    route = jnp.where(lane == ROUTE_CLS, cls, 0.0)
    route = jnp.where(lane == ROUTE_RANK, rank, route)
    route_ref[0, 0] = route.T[:ROUTE_ROWS].astype(jnp.int32)


def _out_proj(x, a_out, b_out, p, tm):
    B, S, D = x.shape
    full = lambda shape: pl.BlockSpec(shape, lambda b, i: (0,) * len(shape))
    row = lambda w: pl.BlockSpec((1, tm, w), lambda b, i: (b, i, 0))
    return pl.pallas_call(
        _out_proj_kernel,
        grid=(B, S // tm),
        in_specs=[row(D), row(SWA_Q_COLS), row(MLA_OUT_COLS),
                  full((1, D)), full((1, D)), full((1, SWA_Q_COLS)), full((1, MLA_OUT_COLS)),
                  full((D, D)), full((1, D)), full((1, D)),
                  full((D, 2 * ROUTE_LANES)), full((1, ROUTE_LANES))],
        out_specs=(row(D), pl.BlockSpec((1, 1, ROUTE_ROWS, tm), lambda b, i: (b, i, 0, 0)),
                   full((1, ROUTE_LANES))),
        out_shape=(jax.ShapeDtypeStruct((B, S, D), F32),
                   jax.ShapeDtypeStruct((B, S // tm, ROUTE_ROWS, tm), jnp.int32),
                   jax.ShapeDtypeStruct((1, ROUTE_LANES), F32)),
        scratch_shapes=[pltpu.VMEM((1, ROUTE_LANES), F32)],
        compiler_params=_cparams(("arbitrary", "arbitrary")),
        name="out_proj_route",
    )(x, a_out, b_out, p["ln_in_g"], p["ln_in_b"], p["ga"], p["gb"], p["w_o"], p["ln1_g"],
      p["ln1_b"], p["w_route"], p["b_route"])


SLAB = D_MODEL // LANES
TAIL_CHUNKS = tuple(MOE_TILE_ROWS >> b for b in range(MOE_TILE_ROWS.bit_length()))


def _slab_copy(src_ref, src_tok8, dst_ref, dst_tok8, sem):
    return pltpu.make_async_copy(src_ref.at[pl.ds(pl.multiple_of(src_tok8, SLAB), SLAB)],
                                 dst_ref.at[pl.ds(pl.multiple_of(dst_tok8, SLAB), SLAB)], sem)


def _to_slabs(x, slab_ref, n):
    for s in range(SLAB):
        slab_ref[pl.ds(s, n, stride=SLAB), :] = x[:, s * LANES:(s + 1) * LANES]


def _from_slabs(slab_ref, n):
    return jnp.concatenate([slab_ref[pl.ds(s, n, stride=SLAB), :] for s in range(SLAB)], axis=1)


def _dispatch_kernel(padstart_ref, taillen_ref, nu_ref, pos_ref, h1_ref, xs_ref, rows_ref, zero_ref,
                     sem, zero_sem, *, tm, n_tiles, slots_per_step):
    t = pl.program_id(0)

    def zero_fill(start, rows):
        return pltpu.make_async_copy(
            zero_ref.at[pl.ds(0, rows * SLAB)],
            xs_ref.at[pl.ds(pl.multiple_of(start, SLAB), rows * SLAB)], zero_sem)

    def zero_fills(act):
        def one_slot(i, carry):
            c = t * slots_per_step + i

            @pl.when(c < N_CLASSES)
            def _():
                tail = taillen_ref[c]
                for rows in TAIL_CHUNKS:
                    @pl.when((tail & rows) != 0)
                    def _():
                        done = tail & ~(2 * rows - 1)
                        act(zero_fill(padstart_ref[c] + done * SLAB, rows))

            @pl.when(nu_ref[0] + c < n_tiles)
            def _():
                act(zero_fill((nu_ref[0] + c) * (MOE_TILE_ROWS * SLAB), MOE_TILE_ROWS))

            return carry

        lax.fori_loop(0, slots_per_step, one_slot, 0)

    @pl.when(t == 0)
    def _():
        zero_ref[...] = jnp.zeros_like(zero_ref)

    zero_fills(lambda fill: fill.start())
    _to_slabs(h1_ref[...], rows_ref, tm)

    def issue(r0, carry):
        for k in range(DMA_BURST):
            r = r0 * DMA_BURST + k
            _slab_copy(rows_ref, r * SLAB, xs_ref, pos_ref[0, 0, r], sem).start(priority=k % 2)
        return carry

    lax.fori_loop(0, tm // DMA_BURST, issue, 0)
    pltpu.make_async_copy(rows_ref, xs_ref.at[pl.ds(0, tm * SLAB)], sem).wait()
    zero_fills(lambda fill: fill.wait())


def _dispatch(h1, pos8, pad_start8, tail_len, n_used, n_tiles, tm):
    T, D = h1.shape
    n_steps = T // tm
    pos8 = pos8.reshape(n_steps, 1, tm)
    grid_spec = pltpu.PrefetchScalarGridSpec(
        num_scalar_prefetch=3,
        grid=(n_steps,),
        in_specs=[pl.BlockSpec((1, 1, tm), lambda t, ps, tl, nu: (t, 0, 0),
                               memory_space=pltpu.SMEM),
                  pl.BlockSpec((tm, D), lambda t, ps, tl, nu: (t, 0))],
        out_specs=pl.BlockSpec(memory_space=pl.ANY),
        scratch_shapes=[pltpu.VMEM((tm * SLAB, LANES), F32),
                        pltpu.VMEM((MOE_TILE_ROWS * SLAB, LANES), F32),
                        pltpu.SemaphoreType.DMA,
                        pltpu.SemaphoreType.DMA],
    )
    return pl.pallas_call(
        functools.partial(_dispatch_kernel, tm=tm, n_tiles=n_tiles,
                          slots_per_step=pl.cdiv(N_CLASSES, n_steps)),
        grid_spec=grid_spec,
        out_shape=jax.ShapeDtypeStruct((n_tiles * MOE_TILE_ROWS * SLAB, LANES), F32),
        compiler_params=_cparams(("arbitrary",)),
        name="moe_dispatch",
    )(pad_start8, tail_len, n_used, pos8, h1)


def _experts_kernel(tlo_ref, thi_ref, nu_ref, x_ref, wg_lo_ref, wg_hi_ref, wu_lo_ref, wu_hi_ref,
                    wd_lo_ref, wd_hi_ref, wrt_ref, brt_ref, y_ref):
    j = pl.program_id(0)
    tr = MOE_TILE_ROWS

    @pl.when(j < nu_ref[0])
    def _():
        x = _from_slabs(x_ref, tr).astype(BF16)
        lo_e = tlo_ref[j]
        hi_e = thi_ref[j]
        logits = _dot(x, wrt_ref[...]) + brt_ref[...]
        lane = lax.broadcasted_iota(jnp.int32, logits.shape, 1)
        gl = jnp.where(lane < N_GROUPS, logits, -jnp.inf)
        ge = jnp.exp(gl - jnp.max(gl, axis=-1, keepdims=True))
        pick = lambda v, at: jnp.sum(jnp.where(lane == at, v, 0.0), axis=-1, keepdims=True)
        g_prob = pick(ge, lo_e // EXPERTS_PER_GROUP) / jnp.sum(ge, axis=-1, keepdims=True)
        e_lo = pick(logits, EXPERT_LANE0 + lo_e)
        e_hi = pick(logits, EXPERT_LANE0 + hi_e)
        gates = (g_prob / (1.0 + jnp.exp(e_hi - e_lo)), g_prob / (1.0 + jnp.exp(e_lo - e_hi)))

        y = None
        for gate, wg_ref, wu_ref, wd_ref in zip(gates, (wg_lo_ref, wg_hi_ref), (wu_lo_ref, wu_hi_ref),
                                                (wd_lo_ref, wd_hi_ref)):
            g = _dot(x, wg_ref[0])
            u = _dot(x, wu_ref[0])
            hid = g * (1.0 / (1.0 + jnp.exp(-g))) * u * gate
            y_k = _dot(hid.astype(BF16), wd_ref[0])
            y = y_k if y is None else y + y_k
        _to_slabs(y, y_ref, tr)


def _experts(xs, tile_lo, tile_hi, n_used, p, n_tiles):
    D = D_MODEL
    tr = MOE_TILE_ROWS
    last_used = lambda j, nu: jnp.minimum(j, nu[0] - 1)
    wgu = lambda pick: pl.BlockSpec((1, D, D_EXPERT),
                                    lambda j, tlo, thi, nu: (pick(tlo, thi)[j], 0, 0))
    wd = lambda pick: pl.BlockSpec((1, D_EXPERT, D),
                                   lambda j, tlo, thi, nu: (pick(tlo, thi)[j], 0, 0))
    lo = lambda tlo, thi: tlo
    hi = lambda tlo, thi: thi
    grid_spec = pltpu.PrefetchScalarGridSpec(
        num_scalar_prefetch=3,
        grid=(n_tiles,),
        in_specs=[pl.BlockSpec((tr * SLAB, LANES), lambda j, tlo, thi, nu: (last_used(j, nu), 0)),
                  wgu(lo), wgu(hi), wgu(lo), wgu(hi), wd(lo), wd(hi),
                  pl.BlockSpec((D, ROUTE_LANES), lambda j, tlo, thi, nu: (0, 0)),
                  pl.BlockSpec((1, ROUTE_LANES), lambda j, tlo, thi, nu: (0, 0))],
        out_specs=pl.BlockSpec((tr * SLAB, LANES), lambda j, tlo, thi, nu: (last_used(j, nu), 0)),
    )
    return pl.pallas_call(
        _experts_kernel,
        grid_spec=grid_spec,
        out_shape=jax.ShapeDtypeStruct(xs.shape, F32),
        input_output_aliases={3: 0},
        compiler_params=_cparams(("arbitrary",)),
        name="moe_experts",
    )(tile_lo, tile_hi, n_used, xs, p["w_gate"], p["w_gate"], p["w_up"], p["w_up"],
      p["w_down"], p["w_down"],
      p["w_route"], p["b_route"])


def _combine_kernel(pos_ref, pos_next_ref, h1_ref, ys_ref, g2_ref, b2_ref, o_ref,
                    ybuf_ref, sem, *, tm):
    t = pl.program_id(0)
    slot = t % 2

    def gather(idx_ref, dst_slot):
        def issue(r0, carry):
            for k in range(DMA_BURST):
                r = r0 * DMA_BURST + k
                _slab_copy(ys_ref, idx_ref[0, 0, r], ybuf_ref.at[dst_slot], r * SLAB,
                           sem.at[dst_slot]).start(priority=k % 2)
            return carry

        lax.fori_loop(0, tm // DMA_BURST, issue, 0)

    @pl.when(t == 0)
    def _():
        gather(pos_ref, 0)

    @pl.when(t + 1 < pl.num_programs(0))
    def _():
        gather(pos_next_ref, 1 - slot)

    pltpu.make_async_copy(ys_ref.at[pl.ds(0, tm * SLAB)], ybuf_ref.at[slot], sem.at[slot]).wait()
    ffn = _from_slabs(ybuf_ref.at[slot], tm)
    o_ref[...] = _layer_norm(ALPHA * h1_ref[...] + ffn, g2_ref[...], b2_ref[...])


def _combine(h1, pos8, ys, p, tm):
    T, D = h1.shape
    n_steps = T // tm
    pos8 = pos8.reshape(n_steps, 1, tm)
    full = lambda shape: pl.BlockSpec(shape, lambda t: (0,) * len(shape))
    return pl.pallas_call(
        functools.partial(_combine_kernel, tm=tm),
        grid=(n_steps,),
        in_specs=[pl.BlockSpec((1, 1, tm), lambda t: (t, 0, 0), memory_space=pltpu.SMEM),
                  pl.BlockSpec((1, 1, tm), lambda t: (jnp.minimum(t + 1, n_steps - 1), 0, 0),
                               memory_space=pltpu.SMEM),
                  pl.BlockSpec((tm, D), lambda t: (t, 0)),
                  pl.BlockSpec(memory_space=pl.ANY),
                  full((1, D)), full((1, D))],
        out_specs=pl.BlockSpec((tm, D), lambda t: (t, 0)),
        out_shape=jax.ShapeDtypeStruct((T, D), F32),
        scratch_shapes=[pltpu.VMEM((2, tm * SLAB, LANES), F32),
                        pltpu.SemaphoreType.DMA((2,))],
        compiler_params=_cparams(("arbitrary",)),
        name="moe_combine_ln2",
    )(pos8, pos8, h1, ys, p["ln2_g"], p["ln2_b"])


def _dispatch_plan(route, counts):
    T = route.shape[0] * route.shape[1] * route.shape[3]
    tr = MOE_TILE_ROWS
    n_tiles = T // tr + N_CLASSES
    cnt = counts[0, :N_CLASSES].astype(jnp.int32)
    padded = jnp.maximum((cnt + tr - 1) // tr, 1) * tr
    classes = jnp.arange(N_CLASSES, dtype=jnp.int32)
    seg_end = jnp.sum(jnp.where(classes[None, :] <= classes[:, None], padded[None, :], 0), axis=1)
    seg_start = seg_end - padded
    cls = route[:, :, ROUTE_CLS, :].reshape(T)
    rank = route[:, :, ROUTE_RANK, :].reshape(T)
    start_of = jnp.sum(jnp.where(cls[:, None] == classes, seg_start, 0), axis=-1)
    pos8 = (start_of + rank) * SLAB
    tile_row0 = jnp.arange(n_tiles, dtype=jnp.int32) * tr
    tile_cls = jnp.sum((seg_end[None, :] <= tile_row0[:, None]).astype(jnp.int32), axis=-1)
    tile_cls = jnp.minimum(tile_cls, N_CLASSES - 1)
    pairs = [(a, b) for a in range(EXPERTS_PER_GROUP) for b in range(a + 1, EXPERTS_PER_GROUP)]
    lo_of = jnp.asarray([g * EXPERTS_PER_GROUP + a for g in range(N_GROUPS) for a, _ in pairs],
                        jnp.int32)
    hi_of = jnp.asarray([g * EXPERTS_PER_GROUP + b for g in range(N_GROUPS) for _, b in pairs],
                        jnp.int32)
    is_cls = tile_cls[:, None] == classes
    tile_lo = jnp.sum(jnp.where(is_cls, lo_of, 0), axis=-1)
    tile_hi = jnp.sum(jnp.where(is_cls, hi_of, 0), axis=-1)
    n_used = (seg_end[-1:] // tr).astype(jnp.int32)
    return pos8, (seg_start + cnt) * SLAB, padded - cnt, tile_lo, tile_hi, n_used, n_tiles


def _rope_tables(pos):
    d = SWA_HEAD_DIM
    inv_freq = ROPE_THETA ** (-jnp.arange(0, d, 2, dtype=F32) / d)
    ang = pos[:, None] * inv_freq[None, :]
    c, s = jnp.cos(ang), jnp.sin(ang)
    return (jnp.concatenate([c, c, c, c], axis=-1), jnp.concatenate([-s, s, -s, s], axis=-1),
            jnp.concatenate([c, c], axis=-1).T, jnp.concatenate([s, s], axis=-1).T)


def _prep_params(ln_in_g, ln_in_b, w_in, swa_sinks, mla_q_norm_g, mla_w_uq, mla_kv_norm_g, mla_w_ukv,
                 swa_out_norm_g, mla_out_norm_g, w_o, ln1_g, ln1_b, moe_w_group, moe_b_group,
                 moe_w_router, moe_b_router, moe_w_gate, moe_w_up, moe_w_down, ln2_g, ln2_b):
    hd = SWA_HEAD_DIM

    def by_head(t, axis):
        return [lax.slice_in_dim(t, h * hd, (h + 1) * hd, axis=axis) for h in SWA_HEAD_ORDER]

    w = w_in[0]
    w_pad = jnp.zeros((D_MODEL, IN_COLS_PADDED - w.shape[1]), w.dtype)
    w_in_p = jnp.concatenate(by_head(w, 1) + [w[:, SWA_Q_COLS:], w_pad], axis=1).astype(BF16)

    dqk = MLA_NOPE_DIM + MLA_ROPE_DIM
    wuq = mla_w_uq[0].reshape(MLA_Q_RANK, MLA_HEADS, dqk)
    wuq = jnp.pad(wuq, ((0, 0), (0, 0), (0, MLA_QK_PAD - dqk)))
    wuq_t = jnp.transpose(wuq, (1, 2, 0)).astype(BF16)
    wukv = mla_w_ukv[0].reshape(MLA_KV_RANK, MLA_HEADS, MLA_NOPE_DIM + MLA_V_DIM)
    wuk = wukv[:, :, :MLA_NOPE_DIM].reshape(MLA_KV_RANK, MLA_HEADS * MLA_NOPE_DIM).astype(BF16)
    wuv_t = jnp.transpose(wukv[:, :, MLA_NOPE_DIM:], (1, 2, 0)).astype(BF16)

    w_o0 = w_o[0]
    w_o_p = jnp.concatenate(by_head(w_o0, 0) + [w_o0[SWA_Q_COLS:]], axis=0).astype(BF16)

    def route_lanes(group_part, expert_part):
        rows = group_part.shape[0]
        gap = jnp.zeros((rows, EXPERT_LANE0 - N_GROUPS), F32)
        tail = jnp.zeros((rows, ROUTE_LANES - EXPERT_LANE0 - N_EXPERTS), F32)
        return jnp.concatenate([group_part, gap, expert_part, tail], axis=1)

    w_route = route_lanes(moe_w_group[0], moe_w_router[0])
    w_route_hi = w_route.astype(BF16)
    w_route_lo = (w_route - w_route_hi.astype(F32)).astype(BF16)
    w_route = jnp.concatenate([w_route_hi, w_route_lo], axis=1)
    b_route = route_lanes(moe_b_group, moe_b_router)

    return {
        "ln_in_g": ln_in_g[None, :], "ln_in_b": ln_in_b[None, :],
        "w_in": w_in_p,
        "sinks": swa_sinks[0],
        "gq": mla_q_norm_g, "w_uq_t": wuq_t,
        "gkv": mla_kv_norm_g, "w_uk": wuk, "w_uv_t": wuv_t,
        "ga": jnp.concatenate(by_head(swa_out_norm_g, 1), axis=1), "gb": mla_out_norm_g,
        "w_o": w_o_p, "ln1_g": ln1_g, "ln1_b": ln1_b,
        "w_route": w_route, "b_route": b_route,
        "w_gate": moe_w_gate[0].astype(BF16), "w_up": moe_w_up[0].astype(BF16),
        "w_down": moe_w_down[0].astype(BF16),
        "ln2_g": ln2_g, "ln2_b": ln2_b,
    }


def kernel(x, meta_tokens, ln_in_g, ln_in_b, w_in, swa_sinks, mla_q_norm_g, mla_w_uq, mla_kv_norm_g, mla_w_ukv, swa_out_norm_g, mla_out_norm_g, w_o, ln1_g, ln1_b, moe_w_group, moe_b_group, moe_w_router, moe_b_router, moe_w_gate, moe_w_up, moe_w_down, ln2_g, ln2_b):
    B, S, D = x.shape
    p = _prep_params(ln_in_g, ln_in_b, w_in, swa_sinks, mla_q_norm_g, mla_w_uq, mla_kv_norm_g,
                     mla_w_ukv, swa_out_norm_g, mla_out_norm_g, w_o, ln1_g, ln1_b, moe_w_group,
                     moe_b_group, moe_w_router, moe_b_router, moe_w_gate, moe_w_up, moe_w_down,
                     ln2_g, ln2_b)
    T = B * S
    tm = min(PROJ_TILE, S)

    x_meta = jnp.concatenate([jnp.zeros((N_PAD, D), x.dtype), meta_tokens.astype(x.dtype)])[None]
    pos_meta = jnp.maximum(jnp.arange(BLOCK) - N_PAD, 0).astype(F32)
    pos_tok = (jnp.arange(S) + N_META).astype(F32)
    _, ka_m, va_m, _, km_m, vt_m = _in_proj(x_meta, _rope_tables(pos_meta), p, BLOCK)
    a_out, qt, km, vt = _in_proj_swa(x, _rope_tables(pos_tok), p, ka_m, va_m, tm)
    b_out = _mla(qt, km, vt, km_m[:, :, N_PAD:, :], vt_m[..., N_PAD:], tm)
    h1, route, counts = _out_proj(x, a_out, b_out, p, min(WIDE_TILE, S))
    h1 = h1.reshape(B * S, D)

    pos8, pad_start8, tail_len, tile_lo, tile_hi, n_used, n_tiles = _dispatch_plan(route, counts)
    xs = _dispatch(h1, pos8, pad_start8, tail_len, n_used, n_tiles, min(DISPATCH_TILE, T))
    ys = _experts(xs, tile_lo, tile_hi, n_used, p, n_tiles)
    out = _combine(h1, pos8, ys, p, min(COMBINE_TILE, T))
    return out.reshape(B, S, D)
```

```python
import functools

import jax
import jax.numpy as jnp
from jax import lax
from jax.experimental import pallas as pl
from jax.experimental.pallas import tpu as pltpu

D_MODEL = 1024
N_META = 16
BLOCK = 128
N_PAD = BLOCK - N_META
ROPE_THETA = 10000.0
SWA_HEADS = 8
SWA_KV_HEADS = 2
SWA_HEAD_DIM = 64
MLA_HEADS = 4
MLA_Q_RANK = 256
MLA_KV_RANK = 256
MLA_NOPE_DIM = 128
MLA_ROPE_DIM = 64
MLA_V_DIM = 128
SWA_Q_COLS = SWA_HEADS * SWA_HEAD_DIM
SWA_KV_COLS = SWA_KV_HEADS * SWA_HEAD_DIM
MLA_OUT_COLS = MLA_HEADS * MLA_V_DIM
N_GROUPS = 4
EXPERTS_PER_GROUP = 8
N_EXPERTS = N_GROUPS * EXPERTS_PER_GROUP
D_EXPERT = 256
LN_EPS = 1e-5
RMS_EPS = 1e-6
DEPTH = 1
ALPHA = (2.0 * DEPTH) ** 0.25
NEG = -1e30
LOG2_E = 1.4426950408889634

LANES = 128
COL_KA = SWA_Q_COLS
COL_VA = COL_KA + SWA_KV_COLS
COL_CQ = COL_VA + SWA_KV_COLS
COL_CKV = COL_CQ + MLA_Q_RANK
COL_KR = COL_CKV + MLA_KV_RANK
IN_COLS_PADDED = COL_KR + LANES
MLA_QK_PAD = 256
ROUTE_LANES = 128
EXPERT_LANE0 = 32
ROUTE_CLS, ROUTE_RANK = range(2)
ROUTE_ROWS = 8
PAIRS_PER_GROUP = EXPERTS_PER_GROUP * (EXPERTS_PER_GROUP - 1) // 2
N_CLASSES = N_GROUPS * PAIRS_PER_GROUP
MOE_TILE_ROWS = 256
PROJ_TILE = 1024
WIDE_TILE = 1024
DISPATCH_TILE = 2048
COMBINE_TILE = 1024
DMA_BURST = 8
VMEM_LIMIT = 56 * 1024 * 1024

SWA_HEAD_ORDER = (0, 4, 1, 5, 2, 6, 3, 7)

BF16 = jnp.bfloat16
F32 = jnp.float32


def _cparams(sem):
    return pltpu.CompilerParams(dimension_semantics=sem, vmem_limit_bytes=VMEM_LIMIT)


def _layer_norm(x, g, b):
    mu = jnp.mean(x, axis=-1, keepdims=True)
    xc = x - mu
    var = jnp.mean(xc * xc, axis=-1, keepdims=True)
    return xc * lax.rsqrt(var + LN_EPS) * g + b


def _rms_scale(x):
    return lax.rsqrt(jnp.mean(x * x, axis=-1, keepdims=True) + RMS_EPS)


def _rope128(x, cos, sin_signed, lane_lo):
    up = pltpu.roll(x, LANES - 32, 1)
    dn = pltpu.roll(x, 32, 1)
    return x * cos + jnp.where(lane_lo, up, dn) * sin_signed


def _dot(a, b):
    return jnp.dot(a, b, preferred_element_type=F32)


def _dot_nt(a, b):
    return lax.dot_general(a, b, (((1,), (1,)), ((), ())), preferred_element_type=F32)


def _in_proj_kernel(x_ref, g_ref, b_ref, w_ref, cos_ref, sin_ref, cost_ref, sint_ref, gq_ref,
                    wuqt_ref, gkv_ref, wuk_ref, wuvt_ref,
                    qa_ref, ka_ref, va_ref, qt_ref, km_ref, vt_ref):
    h = _layer_norm(x_ref[0], g_ref[...], b_ref[...])
    u = _dot(h.astype(BF16), w_ref[...])
    cos = cos_ref[...]
    sin = sin_ref[...]
    lane = lax.broadcasted_iota(jnp.int32, (1, LANES), 1)
    lane_lo = (lane % 64) < 32
    rope = functools.partial(_rope128, cos=cos, sin_signed=sin, lane_lo=lane_lo)

    swa_scale = SWA_HEAD_DIM ** -0.5
    for c in range(SWA_Q_COLS // LANES):
        qc = rope(u[:, c * LANES:(c + 1) * LANES]) * swa_scale
        qa_ref[0, :, c * LANES:(c + 1) * LANES] = qc.astype(BF16)
    ka_ref[0] = rope(u[:, COL_KA:COL_VA]).astype(BF16)
    va_ref[0] = u[:, COL_VA:COL_CQ].astype(BF16)

    cq = u[:, COL_CQ:COL_CKV]
    cqn = (cq * _rms_scale(cq) * gq_ref[...]).astype(BF16)
    ckv = u[:, COL_CKV:COL_KR]
    ckvn = (ckv * _rms_scale(ckv) * gkv_ref[...]).astype(BF16)
    k_nope = _dot(ckvn, wuk_ref[...])
    kr = rope(u[:, COL_KR:IN_COLS_PADDED]).astype(BF16)
    q_scale = (MLA_NOPE_DIM + MLA_ROPE_DIM) ** -0.5 * LOG2_E
    cost = cost_ref[...]
    sint = sint_ref[...]
    half = MLA_ROPE_DIM // 2
    for hd in range(MLA_HEADS):
        qt = _dot_nt(wuqt_ref[hd], cqn)
        qr = qt[MLA_NOPE_DIM:MLA_NOPE_DIM + MLA_ROPE_DIM]
        rot = jnp.concatenate([-qr[half:], qr[:half]], axis=0)
        qt_ref[0, hd, 0:MLA_NOPE_DIM, :] = (qt[:MLA_NOPE_DIM] * q_scale).astype(BF16)
        qt_ref[0, hd, MLA_NOPE_DIM:MLA_NOPE_DIM + MLA_ROPE_DIM, :] = (
            (qr * cost + rot * sint) * q_scale).astype(BF16)
        qt_ref[0, hd, MLA_NOPE_DIM + MLA_ROPE_DIM:, :] = (
            qt[MLA_NOPE_DIM + MLA_ROPE_DIM:]).astype(BF16)
        km_ref[0, hd, :, 0:LANES] = k_nope[:, hd * LANES:(hd + 1) * LANES].astype(BF16)
        km_ref[0, hd, :, LANES:2 * LANES] = kr
        vt_ref[0, hd, 0] = _dot_nt(wuvt_ref[hd], ckvn).astype(BF16)


def _in_proj(x, tables, p, tm):
    cos, sin, cos_t, sin_t = tables
    B, S, D = x.shape
    grid = (B, S // tm)
    full = lambda shape: pl.BlockSpec(shape, lambda b, i: (0,) * len(shape))
    out_shape = (
        jax.ShapeDtypeStruct((B, S, SWA_Q_COLS), BF16),
        jax.ShapeDtypeStruct((B, S, SWA_KV_COLS), BF16),
        jax.ShapeDtypeStruct((B, S, SWA_KV_COLS), BF16),
        jax.ShapeDtypeStruct((B, MLA_HEADS, MLA_QK_PAD, S), BF16),
        jax.ShapeDtypeStruct((B, MLA_HEADS, S, MLA_QK_PAD), BF16),
        jax.ShapeDtypeStruct((B, MLA_HEADS, S // tm, MLA_V_DIM, tm), BF16),
    )
    row = lambda w: pl.BlockSpec((1, tm, w), lambda b, i: (b, i, 0))
    head = lambda w: pl.BlockSpec((1, MLA_HEADS, tm, w), lambda b, i: (b, 0, i, 0))
    return pl.pallas_call(
        _in_proj_kernel,
        grid=grid,
        in_specs=[
            row(D),
            full((1, D)), full((1, D)),
            full((D, IN_COLS_PADDED)),
            pl.BlockSpec((tm, LANES), lambda b, i: (i, 0)),
            pl.BlockSpec((tm, LANES), lambda b, i: (i, 0)),
            pl.BlockSpec((MLA_ROPE_DIM, tm), lambda b, i: (0, i)),
            pl.BlockSpec((MLA_ROPE_DIM, tm), lambda b, i: (0, i)),
            full((1, MLA_Q_RANK)), full((MLA_HEADS, MLA_QK_PAD, MLA_Q_RANK)),
            full((1, MLA_KV_RANK)), full((MLA_KV_RANK, MLA_HEADS * MLA_NOPE_DIM)),
            full((MLA_HEADS, MLA_V_DIM, MLA_KV_RANK)),
        ],
        out_specs=(row(SWA_Q_COLS), row(SWA_KV_COLS), row(SWA_KV_COLS),
                   pl.BlockSpec((1, MLA_HEADS, MLA_QK_PAD, tm), lambda b, i: (b, 0, 0, i)),
                   head(MLA_QK_PAD),
                   pl.BlockSpec((1, MLA_HEADS, 1, MLA_V_DIM, tm), lambda b, i: (b, 0, i, 0, 0))),
        out_shape=out_shape,
        compiler_params=_cparams(("parallel", "parallel")),
        name="in_proj",
    )(x, p["ln_in_g"], p["ln_in_b"], p["w_in"], cos, sin, cos_t, sin_t, p["gq"], p["w_uq_t"],
      p["gkv"], p["w_uk"], p["w_uv_t"])


def _swa_blocks(sink_ref, qa_ref, kc_ref, vc_ref, kprev, vprev, first, o_ref, blocks_per_step):
    row = lax.broadcasted_iota(jnp.int32, (2 * BLOCK, BLOCK), 0) % BLOCK
    col = lax.broadcasted_iota(jnp.int32, (2 * BLOCK, BLOCK), 1)
    own = col <= row
    meta_ok = col >= jnp.where(first, N_PAD, 0)
    lane = lax.broadcasted_iota(jnp.int32, (1, LANES), 1)
    lane_kv0 = lane < SWA_HEAD_DIM
    top_rows = lax.broadcasted_iota(jnp.int32, (2 * BLOCK, 1), 0) < BLOCK

    for r in range(blocks_per_step):
        rows = slice(r * BLOCK, (r + 1) * BLOCK)
        if r == 0:
            kp, vp = kprev, vprev
        else:
            prev_rows = slice((r - 1) * BLOCK, r * BLOCK)
            kp, vp = kc_ref[0, prev_rows, :], vc_ref[0, prev_rows, :]
        kc, vc = kc_ref[0, rows, :], vc_ref[0, rows, :]
        for c in range(SWA_Q_COLS // LANES):
            qc = qa_ref[0, rows, c * LANES:(c + 1) * LANES]
            zero = jnp.zeros_like(qc)
            q2 = jnp.concatenate([jnp.where(lane_kv0, qc, zero),
                                  jnp.where(lane_kv0, zero, qc)], axis=0)
            s_prev = _dot_nt(q2, kp)
            if r == 0:
                s_prev = jnp.where(meta_ok, s_prev, NEG)
            s = jnp.where(own, _dot_nt(q2, kc), s_prev)
            sink = jnp.where(top_rows, sink_ref[c], sink_ref[c + 4])
            m = jnp.maximum(jnp.max(s, axis=-1, keepdims=True), sink)
            e = jnp.exp(s - m)
            denom = jnp.sum(e, axis=-1, keepdims=True) + jnp.exp(sink - m)
            pr = e * (1.0 / denom)
            o2 = (_dot(jnp.where(own, pr, 0.0).astype(BF16), vc)
                  + _dot(jnp.where(own, 0.0, pr).astype(BF16), vp))
            o = jnp.where(lane_kv0, o2[:BLOCK], o2[BLOCK:])
            o_ref[0, rows, c * LANES:(c + 1) * LANES] = o.astype(BF16)


def _in_proj_swa_kernel(sink_ref, x_ref, g_ref, b_ref, w_ref, cos_ref, sin_ref, cost_ref, sint_ref,
                        gq_ref, wuqt_ref, gkv_ref, wuk_ref, wuvt_ref, kmeta_ref, vmeta_ref,
                        a_ref, qt_ref, km_ref, vt_ref,
                        qa_s, ka_s, va_s, kprev_s, vprev_s):
    _in_proj_kernel(x_ref, g_ref, b_ref, w_ref, cos_ref, sin_ref, cost_ref, sint_ref, gq_ref,
                    wuqt_ref, gkv_ref, wuk_ref, wuvt_ref, qa_s, ka_s, va_s, qt_ref, km_ref, vt_ref)
    first = pl.program_id(1) == 0
    kprev = jnp.where(first, kmeta_ref[0], kprev_s[...])
    vprev = jnp.where(first, vmeta_ref[0], vprev_s[...])
    tm = qa_s.shape[1]
    _swa_blocks(sink_ref, qa_s, ka_s, va_s, kprev, vprev, first, a_ref, tm // BLOCK)
    kprev_s[...] = ka_s[0, tm - BLOCK:, :]
    vprev_s[...] = va_s[0, tm - BLOCK:, :]


def _in_proj_swa(x, tables, p, ka_meta, va_meta, tm):
    cos, sin, cos_t, sin_t = tables
    B, S, D = x.shape
    kvw = SWA_KV_COLS
    full = lambda shape: pl.BlockSpec(shape, lambda b, i: (0,) * len(shape))
    row = lambda w: pl.BlockSpec((1, tm, w), lambda b, i: (b, i, 0))
    out_shape = (
        jax.ShapeDtypeStruct((B, S, SWA_Q_COLS), BF16),
        jax.ShapeDtypeStruct((B, MLA_HEADS, MLA_QK_PAD, S), BF16),
        jax.ShapeDtypeStruct((B, MLA_HEADS, S, MLA_QK_PAD), BF16),
        jax.ShapeDtypeStruct((B, MLA_HEADS, S // tm, MLA_V_DIM, tm), BF16),
    )
    return pl.pallas_call(
        _in_proj_swa_kernel,
        grid=(B, S // tm),
        in_specs=[
            pl.BlockSpec(memory_space=pltpu.SMEM),
            row(D),
            full((1, D)), full((1, D)),
            full((D, IN_COLS_PADDED)),
            pl.BlockSpec((tm, LANES), lambda b, i: (i, 0)),
            pl.BlockSpec((tm, LANES), lambda b, i: (i, 0)),
            pl.BlockSpec((MLA_ROPE_DIM, tm), lambda b, i: (0, i)),
            pl.BlockSpec((MLA_ROPE_DIM, tm), lambda b, i: (0, i)),
            full((1, MLA_Q_RANK)), full((MLA_HEADS, MLA_QK_PAD, MLA_Q_RANK)),
            full((1, MLA_KV_RANK)), full((MLA_KV_RANK, MLA_HEADS * MLA_NOPE_DIM)),
            full((MLA_HEADS, MLA_V_DIM, MLA_KV_RANK)),
            full((1, BLOCK, kvw)), full((1, BLOCK, kvw)),
        ],
        out_specs=(row(SWA_Q_COLS),
                   pl.BlockSpec((1, MLA_HEADS, MLA_QK_PAD, tm), lambda b, i: (b, 0, 0, i)),
                   pl.BlockSpec((1, MLA_HEADS, tm, MLA_QK_PAD), lambda b, i: (b, 0, i, 0)),
                   pl.BlockSpec((1, MLA_HEADS, 1, MLA_V_DIM, tm), lambda b, i: (b, 0, i, 0, 0))),
        out_shape=out_shape,
        scratch_shapes=[pltpu.VMEM((1, tm, SWA_Q_COLS), BF16), pltpu.VMEM((1, tm, kvw), BF16),
                        pltpu.VMEM((1, tm, kvw), BF16), pltpu.VMEM((BLOCK, kvw), BF16),
                        pltpu.VMEM((BLOCK, kvw), BF16)],
        compiler_params=_cparams(("parallel", "arbitrary")),
        name="in_proj_swa",
    )(p["sinks"], x, p["ln_in_g"], p["ln_in_b"], p["w_in"], cos, sin, cos_t, sin_t, p["gq"],
      p["w_uq_t"], p["gkv"], p["w_uk"], p["w_uv_t"], ka_meta, va_meta)


def _mla_kernel(qt_ref, k_ref, vt_ref, kmeta_ref, vtmeta_ref, o_ref,
                s0a_ref, s0b_ref, s1a_ref, s1b_ref, smax_ref, m_ref, l_ref, acc_ref, *, tq):
    u = pl.program_id(2)
    qts = (qt_ref[0, 0, :, :tq], qt_ref[0, 0, :, tq:])
    bufs = ((s0a_ref, s0b_ref), (s1a_ref, s1b_ref))

    for g in range(2):
        s = _dot(kmeta_ref[0, 0], qts[g])
        m = jnp.max(s, axis=0, keepdims=True)
        p = jnp.exp2(s - m)
        m_ref[g] = m
        l_ref[g] = jnp.sum(p, axis=0, keepdims=True)
        acc_ref[g] = _dot(vtmeta_ref[0, 0, 0], p.astype(BF16))

    def scores_into(g, ab, j):
        start = pl.multiple_of(j * tq, tq)
        s = _dot(k_ref[0, 0, pl.ds(start, tq), :], qts[g])
        bufs[g][ab][...] = s
        smax_ref[g, ab] = jnp.max(s, axis=0, keepdims=True)

    def absorb(g, ab, j, causal=False):
        s = bufs[g][ab][...]
        if causal:
            key = lax.broadcasted_iota(jnp.int32, (tq, tq), 0)
            query = lax.broadcasted_iota(jnp.int32, (tq, tq), 1)
            s = jnp.where(key <= query, s, NEG)
            smax = jnp.max(s, axis=0, keepdims=True)
        else:
            smax = smax_ref[g, ab]
        m = m_ref[g]
        m_new = jnp.maximum(m, smax)
        alpha = jnp.exp2(m - m_new)
        p = jnp.exp2(s - m_new)
        m_ref[g] = m_new
        l_ref[g] = alpha * l_ref[g] + jnp.sum(p, axis=0, keepdims=True)
        acc_ref[g] = alpha * acc_ref[g] + _dot(vt_ref[0, 0, j], p.astype(BF16))

    buf_a, buf_b = 0, 1
    for g in range(2):
        scores_into(g, buf_a, 0)

    def pair(t, carry):
        scores_into(0, buf_b, 2 * t + 1)
        absorb(0, buf_a, 2 * t)
        scores_into(1, buf_b, 2 * t + 1)
        scores_into(0, buf_a, 2 * t + 2)
        absorb(1, buf_a, 2 * t)
        absorb(0, buf_b, 2 * t + 1)
        scores_into(1, buf_a, 2 * t + 2)
        absorb(1, buf_b, 2 * t + 1)
        return carry

    lax.fori_loop(0, u, pair, 0)

    scores_into(1, buf_b, 2 * u + 1)
    absorb(1, buf_a, 2 * u)
    absorb(0, buf_a, 2 * u, causal=True)
    absorb(1, buf_b, 2 * u + 1, causal=True)

    for g in range(2):
        o_ref[0, g * tq:(g + 1) * tq, :] = (acc_ref[g] * (1.0 / l_ref[g])).T.astype(BF16)


def _mla(qt, km, vt, km_meta, vt_meta, tq):
    B, H, _, S = qt.shape
    assert S % (2 * tq) == 0, "a step handles an (even, odd) pair of query tiles"
    return pl.pallas_call(
        functools.partial(_mla_kernel, tq=tq),
        grid=(B, H, S // (2 * tq)),
        in_specs=[
            pl.BlockSpec((1, 1, MLA_QK_PAD, 2 * tq), lambda b, h, u: (b, h, 0, u)),
            pl.BlockSpec((1, 1, S, MLA_QK_PAD), lambda b, h, u: (b, h, 0, 0)),
            pl.BlockSpec((1, 1, S // tq, MLA_V_DIM, tq), lambda b, h, u: (b, h, 0, 0, 0)),
            pl.BlockSpec((1, 1, N_META, MLA_QK_PAD), lambda b, h, u: (0, h, 0, 0)),
            pl.BlockSpec((1, 1, 1, MLA_V_DIM, N_META), lambda b, h, u: (0, h, 0, 0, 0)),
        ],
        out_specs=pl.BlockSpec((1, 2 * tq, MLA_V_DIM), lambda b, h, u: (b, u, h)),
        out_shape=jax.ShapeDtypeStruct((B, S, MLA_OUT_COLS), BF16),
        scratch_shapes=[pltpu.VMEM((tq, tq), F32), pltpu.VMEM((tq, tq), F32),
                        pltpu.VMEM((tq, tq), F32), pltpu.VMEM((tq, tq), F32),
                        pltpu.VMEM((2, 2, 1, tq), F32),
                        pltpu.VMEM((2, 1, tq), F32), pltpu.VMEM((2, 1, tq), F32),
                        pltpu.VMEM((2, MLA_V_DIM, tq), F32)],
        compiler_params=_cparams(("parallel", "parallel", "arbitrary")),
        name="mla_attn",
    )(qt, km, vt, km_meta, vt_meta)


def _out_proj_kernel(x_ref, a_ref, b_ref, gin_ref, bin_ref, ga_ref, gb_ref, wo_ref, g1_ref, b1_ref,
                     wrt_ref, brt_ref, h1_ref, route_ref, counts_out_ref, count_ref):
    h = _layer_norm(x_ref[0], gin_ref[...], bin_ref[...])
    a = a_ref[0].astype(F32)
    b = b_ref[0].astype(F32)
    an = (a * _rms_scale(a) * ga_ref[...]).astype(BF16)
    bn = (b * _rms_scale(b) * gb_ref[...]).astype(BF16)
    mix = _dot(jnp.concatenate([an, bn], axis=-1), wo_ref[...])
    h1 = _layer_norm(ALPHA * h + mix, g1_ref[...], b1_ref[...])
    h1_ref[0] = h1

    x_hi = h1.astype(BF16)
    x_lo = (h1 - x_hi.astype(F32)).astype(BF16)
    hi_both = _dot(x_hi, wrt_ref[...])
    logits = (hi_both[:, :ROUTE_LANES] + hi_both[:, ROUTE_LANES:]
              + _dot(x_lo, wrt_ref[:, :ROUTE_LANES]) + brt_ref[...])
    lane = lax.broadcasted_iota(jnp.int32, logits.shape, 1).astype(F32)
    ninf = -jnp.inf
    no_lane = float(ROUTE_LANES)
    gl = jnp.where(lane < N_GROUPS, logits, ninf)
    gmax = jnp.max(gl, axis=-1, keepdims=True)
    g_idx = jnp.min(jnp.where(gl == gmax, lane, no_lane), axis=-1, keepdims=True)
    lo = EXPERT_LANE0 + g_idx * EXPERTS_PER_GROUP
    el = jnp.where((lane >= lo) & (lane < lo + EXPERTS_PER_GROUP), logits, ninf)
    m1 = jnp.max(el, axis=-1, keepdims=True)
    i1 = jnp.min(jnp.where(el == m1, lane, no_lane), axis=-1, keepdims=True)
    el2 = jnp.where(lane == i1, ninf, el)
    m2 = jnp.max(el2, axis=-1, keepdims=True)
    i2 = jnp.min(jnp.where(el2 == m2, lane, no_lane), axis=-1, keepdims=True)

    a = jnp.minimum(i1, i2) - lo
    b = jnp.maximum(i1, i2) - lo
    cls = (g_idx * PAIRS_PER_GROUP + a * (2.0 * EXPERTS_PER_GROUP - 1.0 - a) * 0.5
           + (b - a - 1.0))

    @pl.when((pl.program_id(0) == 0) & (pl.program_id(1) == 0))
    def _():
        count_ref[...] = jnp.zeros_like(count_ref)

    tm = logits.shape[0]
    onehot = jnp.where(lane == cls, 1.0, 0.0)
    r_i = lax.broadcasted_iota(jnp.int32, (BLOCK, BLOCK), 0)
    c_i = lax.broadcasted_iota(jnp.int32, (BLOCK, BLOCK), 1)
    earlier = jnp.where(r_i > c_i, 1.0, 0.0).astype(BF16)
    running = count_ref[...]
    before = []
    for blk in range(tm // BLOCK):
        oh = onehot[blk * BLOCK:(blk + 1) * BLOCK]
        before.append(_dot(earlier, oh.astype(BF16)) + running)
        running = running + jnp.sum(oh, axis=0, keepdims=True)
    before = jnp.concatenate(before, axis=0)
    rank = jnp.sum(jnp.where(lane == cls, before, 0.0), axis=-1, keepdims=True)
    count_ref[...] = running
    counts_out_ref[...] = running

    route = jnp.where(lane == ROUTE_CLS, cls, 0.0)
    route = jnp.where(lane == ROUTE_RANK, rank, route)
    route_ref[0, 0] = route.T[:ROUTE_ROWS].astype(jnp.int32)


def _out_proj(x, a_out, b_out, p, tm):
    B, S, D = x.shape
    full = lambda shape: pl.BlockSpec(shape, lambda b, i: (0,) * len(shape))
    row = lambda w: pl.BlockSpec((1, tm, w), lambda b, i: (b, i, 0))
    return pl.pallas_call(
        _out_proj_kernel,
        grid=(B, S // tm),
        in_specs=[row(D), row(SWA_Q_COLS), row(MLA_OUT_COLS),
                  full((1, D)), full((1, D)), full((1, SWA_Q_COLS)), full((1, MLA_OUT_COLS)),
                  full((D, D)), full((1, D)), full((1, D)),
                  full((D, 2 * ROUTE_LANES)), full((1, ROUTE_LANES))],
        out_specs=(row(D), pl.BlockSpec((1, 1, ROUTE_ROWS, tm), lambda b, i: (b, i, 0, 0)),
                   full((1, ROUTE_LANES))),
        out_shape=(jax.ShapeDtypeStruct((B, S, D), F32),
                   jax.ShapeDtypeStruct((B, S // tm, ROUTE_ROWS, tm), jnp.int32),
                   jax.ShapeDtypeStruct((1, ROUTE_LANES), F32)),
        scratch_shapes=[pltpu.VMEM((1, ROUTE_LANES), F32)],
        compiler_params=_cparams(("arbitrary", "arbitrary")),
        name="out_proj_route",
    )(x, a_out, b_out, p["ln_in_g"], p["ln_in_b"], p["ga"], p["gb"], p["w_o"], p["ln1_g"],
      p["ln1_b"], p["w_route"], p["b_route"])


SLAB = D_MODEL // LANES
TAIL_CHUNKS = tuple(MOE_TILE_ROWS >> b for b in range(MOE_TILE_ROWS.bit_length()))


def _slab_copy(src_ref, src_tok8, dst_ref, dst_tok8, sem):
    return pltpu.make_async_copy(src_ref.at[pl.ds(pl.multiple_of(src_tok8, SLAB), SLAB)],
                                 dst_ref.at[pl.ds(pl.multiple_of(dst_tok8, SLAB), SLAB)], sem)


def _to_slabs(x, slab_ref, n):
    for s in range(SLAB):
        slab_ref[pl.ds(s, n, stride=SLAB), :] = x[:, s * LANES:(s + 1) * LANES]


def _from_slabs(slab_ref, n):
    return jnp.concatenate([slab_ref[pl.ds(s, n, stride=SLAB), :] for s in range(SLAB)], axis=1)


def _dispatch_kernel(padstart_ref, taillen_ref, nu_ref, pos_ref, h1_ref, xs_ref, rows_ref, zero_ref,
                     sem, zero_sem, *, tm, n_tiles, slots_per_step):
    t = pl.program_id(0)

    def zero_fill(start, rows):
        return pltpu.make_async_copy(
            zero_ref.at[pl.ds(0, rows * SLAB)],
            xs_ref.at[pl.ds(pl.multiple_of(start, SLAB), rows * SLAB)], zero_sem)

    def zero_fills(act):
        def one_slot(i, carry):
            c = t * slots_per_step + i

            @pl.when(c < N_CLASSES)
            def _():
                tail = taillen_ref[c]
                for rows in TAIL_CHUNKS:
                    @pl.when((tail & rows) != 0)
                    def _():
                        done = tail & ~(2 * rows - 1)
                        act(zero_fill(padstart_ref[c] + done * SLAB, rows))

            @pl.when(nu_ref[0] + c < n_tiles)
            def _():
                act(zero_fill((nu_ref[0] + c) * (MOE_TILE_ROWS * SLAB), MOE_TILE_ROWS))

            return carry

        lax.fori_loop(0, slots_per_step, one_slot, 0)

    @pl.when(t == 0)
    def _():
        zero_ref[...] = jnp.zeros_like(zero_ref)

    zero_fills(lambda fill: fill.start())
    _to_slabs(h1_ref[...], rows_ref, tm)

    def issue(r0, carry):
        for k in range(DMA_BURST):
            r = r0 * DMA_BURST + k
            _slab_copy(rows_ref, r * SLAB, xs_ref, pos_ref[0, 0, r], sem).start(priority=k % 2)
        return carry

    lax.fori_loop(0, tm // DMA_BURST, issue, 0)
    pltpu.make_async_copy(rows_ref, xs_ref.at[pl.ds(0, tm * SLAB)], sem).wait()
    zero_fills(lambda fill: fill.wait())


def _dispatch(h1, pos8, pad_start8, tail_len, n_used, n_tiles, tm):
    T, D = h1.shape
    n_steps = T // tm
    pos8 = pos8.reshape(n_steps, 1, tm)
    grid_spec = pltpu.PrefetchScalarGridSpec(
        num_scalar_prefetch=3,
        grid=(n_steps,),
        in_specs=[pl.BlockSpec((1, 1, tm), lambda t, ps, tl, nu: (t, 0, 0),
                               memory_space=pltpu.SMEM),
                  pl.BlockSpec((tm, D), lambda t, ps, tl, nu: (t, 0))],
        out_specs=pl.BlockSpec(memory_space=pl.ANY),
        scratch_shapes=[pltpu.VMEM((tm * SLAB, LANES), F32),
                        pltpu.VMEM((MOE_TILE_ROWS * SLAB, LANES), F32),
                        pltpu.SemaphoreType.DMA,
                        pltpu.SemaphoreType.DMA],
    )
    return pl.pallas_call(
        functools.partial(_dispatch_kernel, tm=tm, n_tiles=n_tiles,
                          slots_per_step=pl.cdiv(N_CLASSES, n_steps)),
        grid_spec=grid_spec,
        out_shape=jax.ShapeDtypeStruct((n_tiles * MOE_TILE_ROWS * SLAB, LANES), F32),
        compiler_params=_cparams(("arbitrary",)),
        name="moe_dispatch",
    )(pad_start8, tail_len, n_used, pos8, h1)


def _experts_kernel(tlo_ref, thi_ref, nu_ref, filled_ref, x_ref, wg_lo_ref, wg_hi_ref, wu_lo_ref,
                    wu_hi_ref, wd_lo_ref, wd_hi_ref, wrt_ref, brt_ref, y_ref):
    j = pl.program_id(0)
    used = j < nu_ref[0]
    half = MOE_TILE_ROWS // 2
    half_empty = filled_ref[j] <= half

    def run(tr):
        x = _from_slabs(x_ref, tr).astype(BF16)
        lo_e = tlo_ref[j]
        hi_e = thi_ref[j]
        logits = _dot(x, wrt_ref[...]) + brt_ref[...]
        lane = lax.broadcasted_iota(jnp.int32, logits.shape, 1)
        gl = jnp.where(lane < N_GROUPS, logits, -jnp.inf)
        ge = jnp.exp(gl - jnp.max(gl, axis=-1, keepdims=True))
        pick = lambda v, at: jnp.sum(jnp.where(lane == at, v, 0.0), axis=-1, keepdims=True)
        g_prob = pick(ge, lo_e // EXPERTS_PER_GROUP) / jnp.sum(ge, axis=-1, keepdims=True)
        e_lo = pick(logits, EXPERT_LANE0 + lo_e)
        e_hi = pick(logits, EXPERT_LANE0 + hi_e)
        gates = (g_prob / (1.0 + jnp.exp(e_hi - e_lo)), g_prob / (1.0 + jnp.exp(e_lo - e_hi)))

        y = None
        for gate, wg_ref, wu_ref, wd_ref in zip(gates, (wg_lo_ref, wg_hi_ref), (wu_lo_ref, wu_hi_ref),
                                                (wd_lo_ref, wd_hi_ref)):
            g = _dot(x, wg_ref[0])
            u = _dot(x, wu_ref[0])
            hid = g * (1.0 / (1.0 + jnp.exp(-g))) * u * gate
            y_k = _dot(hid.astype(BF16), wd_ref[0])
            y = y_k if y is None else y + y_k
        _to_slabs(y, y_ref, tr)
        if tr < MOE_TILE_ROWS:
            y_ref[tr * SLAB:, :] = jnp.zeros(((MOE_TILE_ROWS - tr) * SLAB, LANES), F32)

    @pl.when(used & half_empty)
    def _():
        run(half)

    @pl.when(used & jnp.logical_not(half_empty))
    def _():
        run(MOE_TILE_ROWS)


def _experts(xs, tile_lo, tile_hi, n_used, tile_filled, p, n_tiles):
    D = D_MODEL
    tr = MOE_TILE_ROWS
    last_used = lambda j, nu: jnp.minimum(j, nu[0] - 1)
    wgu = lambda pick: pl.BlockSpec((1, D, D_EXPERT),
                                    lambda j, tlo, thi, nu, tf: (pick(tlo, thi)[j], 0, 0))
    wd = lambda pick: pl.BlockSpec((1, D_EXPERT, D),
                                   lambda j, tlo, thi, nu, tf: (pick(tlo, thi)[j], 0, 0))
    lo = lambda tlo, thi: tlo
    hi = lambda tlo, thi: thi
    rows = pl.BlockSpec((tr * SLAB, LANES), lambda j, tlo, thi, nu, tf: (last_used(j, nu), 0))
    grid_spec = pltpu.PrefetchScalarGridSpec(
        num_scalar_prefetch=4,
        grid=(n_tiles,),
        in_specs=[rows,
                  wgu(lo), wgu(hi), wgu(lo), wgu(hi), wd(lo), wd(hi),
                  pl.BlockSpec((D, ROUTE_LANES), lambda j, tlo, thi, nu, tf: (0, 0)),
                  pl.BlockSpec((1, ROUTE_LANES), lambda j, tlo, thi, nu, tf: (0, 0))],
        out_specs=rows,
    )
    return pl.pallas_call(
        _experts_kernel,
        grid_spec=grid_spec,
        out_shape=jax.ShapeDtypeStruct(xs.shape, F32),
        input_output_aliases={4: 0},
        compiler_params=_cparams(("arbitrary",)),
        name="moe_experts",
    )(tile_lo, tile_hi, n_used, tile_filled, xs, p["w_gate"], p["w_gate"], p["w_up"], p["w_up"],
      p["w_down"], p["w_down"],
      p["w_route"], p["b_route"])


def _combine_kernel(pos_ref, pos_next_ref, h1_ref, ys_ref, g2_ref, b2_ref, o_ref,
                    ybuf_ref, sem, *, tm):
    t = pl.program_id(0)
    slot = t % 2

    def gather(idx_ref, dst_slot):
        def issue(r0, carry):
            for k in range(DMA_BURST):
                r = r0 * DMA_BURST + k
                _slab_copy(ys_ref, idx_ref[0, 0, r], ybuf_ref.at[dst_slot], r * SLAB,
                           sem.at[dst_slot]).start(priority=k % 2)
            return carry

        lax.fori_loop(0, tm // DMA_BURST, issue, 0)

    @pl.when(t == 0)
    def _():
        gather(pos_ref, 0)

    @pl.when(t + 1 < pl.num_programs(0))
    def _():
        gather(pos_next_ref, 1 - slot)

    pltpu.make_async_copy(ys_ref.at[pl.ds(0, tm * SLAB)], ybuf_ref.at[slot], sem.at[slot]).wait()
    ffn = _from_slabs(ybuf_ref.at[slot], tm)
    o_ref[...] = _layer_norm(ALPHA * h1_ref[...] + ffn, g2_ref[...], b2_ref[...])


def _combine(h1, pos8, ys, p, tm):
    T, D = h1.shape
    n_steps = T // tm
    pos8 = pos8.reshape(n_steps, 1, tm)
    full = lambda shape: pl.BlockSpec(shape, lambda t: (0,) * len(shape))
    return pl.pallas_call(
        functools.partial(_combine_kernel, tm=tm),
        grid=(n_steps,),
        in_specs=[pl.BlockSpec((1, 1, tm), lambda t: (t, 0, 0), memory_space=pltpu.SMEM),
                  pl.BlockSpec((1, 1, tm), lambda t: (jnp.minimum(t + 1, n_steps - 1), 0, 0),
                               memory_space=pltpu.SMEM),
                  pl.BlockSpec((tm, D), lambda t: (t, 0)),
                  pl.BlockSpec(memory_space=pl.ANY),
                  full((1, D)), full((1, D))],
        out_specs=pl.BlockSpec((tm, D), lambda t: (t, 0)),
        out_shape=jax.ShapeDtypeStruct((T, D), F32),
        scratch_shapes=[pltpu.VMEM((2, tm * SLAB, LANES), F32),
                        pltpu.SemaphoreType.DMA((2,))],
        compiler_params=_cparams(("arbitrary",)),
        name="moe_combine_ln2",
    )(pos8, pos8, h1, ys, p["ln2_g"], p["ln2_b"])


def _dispatch_plan(route, counts):
    T = route.shape[0] * route.shape[1] * route.shape[3]
    tr = MOE_TILE_ROWS
    n_tiles = T // tr + N_CLASSES
    cnt = counts[0, :N_CLASSES].astype(jnp.int32)
    padded = jnp.maximum((cnt + tr - 1) // tr, 1) * tr
    classes = jnp.arange(N_CLASSES, dtype=jnp.int32)
    seg_end = jnp.sum(jnp.where(classes[None, :] <= classes[:, None], padded[None, :], 0), axis=1)
    seg_start = seg_end - padded
    cls = route[:, :, ROUTE_CLS, :].reshape(T)
    rank = route[:, :, ROUTE_RANK, :].reshape(T)
    start_of = jnp.sum(jnp.where(cls[:, None] == classes, seg_start, 0), axis=-1)
    pos8 = (start_of + rank) * SLAB
    tile_row0 = jnp.arange(n_tiles, dtype=jnp.int32) * tr
    tile_cls = jnp.sum((seg_end[None, :] <= tile_row0[:, None]).astype(jnp.int32), axis=-1)
    tile_cls = jnp.minimum(tile_cls, N_CLASSES - 1)
    pairs = [(a, b) for a in range(EXPERTS_PER_GROUP) for b in range(a + 1, EXPERTS_PER_GROUP)]
    lo_of = jnp.asarray([g * EXPERTS_PER_GROUP + a for g in range(N_GROUPS) for a, _ in pairs],
                        jnp.int32)
    hi_of = jnp.asarray([g * EXPERTS_PER_GROUP + b for g in range(N_GROUPS) for _, b in pairs],
                        jnp.int32)
    is_cls = tile_cls[:, None] == classes
    tile_lo = jnp.sum(jnp.where(is_cls, lo_of, 0), axis=-1)
    tile_hi = jnp.sum(jnp.where(is_cls, hi_of, 0), axis=-1)
    n_used = (seg_end[-1:] // tr).astype(jnp.int32)
    fill_end = jnp.sum(jnp.where(is_cls, seg_start + cnt, 0), axis=-1)
    tile_filled = jnp.clip(fill_end - tile_row0, 0, tr)
    return (pos8, (seg_start + cnt) * SLAB, padded - cnt, tile_lo, tile_hi, n_used, tile_filled,
            n_tiles)


def _rope_tables(pos):
    d = SWA_HEAD_DIM
    inv_freq = ROPE_THETA ** (-jnp.arange(0, d, 2, dtype=F32) / d)
    ang = pos[:, None] * inv_freq[None, :]
    c, s = jnp.cos(ang), jnp.sin(ang)
    return (jnp.concatenate([c, c, c, c], axis=-1), jnp.concatenate([-s, s, -s, s], axis=-1),
            jnp.concatenate([c, c], axis=-1).T, jnp.concatenate([s, s], axis=-1).T)


def _prep_params(ln_in_g, ln_in_b, w_in, swa_sinks, mla_q_norm_g, mla_w_uq, mla_kv_norm_g, mla_w_ukv,
                 swa_out_norm_g, mla_out_norm_g, w_o, ln1_g, ln1_b, moe_w_group, moe_b_group,
                 moe_w_router, moe_b_router, moe_w_gate, moe_w_up, moe_w_down, ln2_g, ln2_b):
    hd = SWA_HEAD_DIM

    def by_head(t, axis):
        return [lax.slice_in_dim(t, h * hd, (h + 1) * hd, axis=axis) for h in SWA_HEAD_ORDER]

    w = w_in[0]
    w_pad = jnp.zeros((D_MODEL, IN_COLS_PADDED - w.shape[1]), w.dtype)
    w_in_p = jnp.concatenate(by_head(w, 1) + [w[:, SWA_Q_COLS:], w_pad], axis=1).astype(BF16)

    dqk = MLA_NOPE_DIM + MLA_ROPE_DIM
    wuq = mla_w_uq[0].reshape(MLA_Q_RANK, MLA_HEADS, dqk)
    wuq = jnp.pad(wuq, ((0, 0), (0, 0), (0, MLA_QK_PAD - dqk)))
    wuq_t = jnp.transpose(wuq, (1, 2, 0)).astype(BF16)
    wukv = mla_w_ukv[0].reshape(MLA_KV_RANK, MLA_HEADS, MLA_NOPE_DIM + MLA_V_DIM)
    wuk = wukv[:, :, :MLA_NOPE_DIM].reshape(MLA_KV_RANK, MLA_HEADS * MLA_NOPE_DIM).astype(BF16)
    wuv_t = jnp.transpose(wukv[:, :, MLA_NOPE_DIM:], (1, 2, 0)).astype(BF16)

    w_o0 = w_o[0]
    w_o_p = jnp.concatenate(by_head(w_o0, 0) + [w_o0[SWA_Q_COLS:]], axis=0).astype(BF16)

    def route_lanes(group_part, expert_part):
        rows = group_part.shape[0]
        gap = jnp.zeros((rows, EXPERT_LANE0 - N_GROUPS), F32)
        tail = jnp.zeros((rows, ROUTE_LANES - EXPERT_LANE0 - N_EXPERTS), F32)
        return jnp.concatenate([group_part, gap, expert_part, tail], axis=1)

    w_route = route_lanes(moe_w_group[0], moe_w_router[0])
    w_route_hi = w_route.astype(BF16)
    w_route_lo = (w_route - w_route_hi.astype(F32)).astype(BF16)
    w_route = jnp.concatenate([w_route_hi, w_route_lo], axis=1)
    b_route = route_lanes(moe_b_group, moe_b_router)

    return {
        "ln_in_g": ln_in_g[None, :], "ln_in_b": ln_in_b[None, :],
        "w_in": w_in_p,
        "sinks": swa_sinks[0],
        "gq": mla_q_norm_g, "w_uq_t": wuq_t,
        "gkv": mla_kv_norm_g, "w_uk": wuk, "w_uv_t": wuv_t,
        "ga": jnp.concatenate(by_head(swa_out_norm_g, 1), axis=1), "gb": mla_out_norm_g,
        "w_o": w_o_p, "ln1_g": ln1_g, "ln1_b": ln1_b,
        "w_route": w_route, "b_route": b_route,
        "w_gate": moe_w_gate[0].astype(BF16), "w_up": moe_w_up[0].astype(BF16),
        "w_down": moe_w_down[0].astype(BF16),
        "ln2_g": ln2_g, "ln2_b": ln2_b,
    }


def kernel(x, meta_tokens, ln_in_g, ln_in_b, w_in, swa_sinks, mla_q_norm_g, mla_w_uq, mla_kv_norm_g, mla_w_ukv, swa_out_norm_g, mla_out_norm_g, w_o, ln1_g, ln1_b, moe_w_group, moe_b_group, moe_w_router, moe_b_router, moe_w_gate, moe_w_up, moe_w_down, ln2_g, ln2_b):
    B, S, D = x.shape
    p = _prep_params(ln_in_g, ln_in_b, w_in, swa_sinks, mla_q_norm_g, mla_w_uq, mla_kv_norm_g,
                     mla_w_ukv, swa_out_norm_g, mla_out_norm_g, w_o, ln1_g, ln1_b, moe_w_group,
                     moe_b_group, moe_w_router, moe_b_router, moe_w_gate, moe_w_up, moe_w_down,
                     ln2_g, ln2_b)
    T = B * S
    tm = min(PROJ_TILE, S)

    x_meta = jnp.concatenate([jnp.zeros((N_PAD, D), x.dtype), meta_tokens.astype(x.dtype)])[None]
    pos_meta = jnp.maximum(jnp.arange(BLOCK) - N_PAD, 0).astype(F32)
    pos_tok = (jnp.arange(S) + N_META).astype(F32)
    _, ka_m, va_m, _, km_m, vt_m = _in_proj(x_meta, _rope_tables(pos_meta), p, BLOCK)
    a_out, qt, km, vt = _in_proj_swa(x, _rope_tables(pos_tok), p, ka_m, va_m, tm)
    b_out = _mla(qt, km, vt, km_m[:, :, N_PAD:, :], vt_m[..., N_PAD:], tm)
    h1, route, counts = _out_proj(x, a_out, b_out, p, min(WIDE_TILE, S))
    h1 = h1.reshape(B * S, D)

    (pos8, pad_start8, tail_len, tile_lo, tile_hi, n_used, tile_filled,
     n_tiles) = _dispatch_plan(route, counts)
    xs = _dispatch(h1, pos8, pad_start8, tail_len, n_used, n_tiles, min(DISPATCH_TILE, T))
    ys = _experts(xs, tile_lo, tile_hi, n_used, tile_filled, p, n_tiles)
    out = _combine(h1, pos8, ys, p, min(COMBINE_TILE, T))
    return out.reshape(B, S, D)
```

```python
import functools

import jax
import jax.numpy as jnp
from jax import lax
from jax.experimental import pallas as pl
from jax.experimental.pallas import tpu as pltpu

D_MODEL = 1024
N_META = 16
BLOCK = 128
N_PAD = BLOCK - N_META
ROPE_THETA = 10000.0
SWA_HEADS = 8
SWA_KV_HEADS = 2
SWA_HEAD_DIM = 64
MLA_HEADS = 4
MLA_Q_RANK = 256
MLA_KV_RANK = 256
MLA_NOPE_DIM = 128
MLA_ROPE_DIM = 64
MLA_V_DIM = 128
SWA_Q_COLS = SWA_HEADS * SWA_HEAD_DIM
SWA_KV_COLS = SWA_KV_HEADS * SWA_HEAD_DIM
MLA_OUT_COLS = MLA_HEADS * MLA_V_DIM
N_GROUPS = 4
EXPERTS_PER_GROUP = 8
N_EXPERTS = N_GROUPS * EXPERTS_PER_GROUP
D_EXPERT = 256
LN_EPS = 1e-5
RMS_EPS = 1e-6
DEPTH = 1
ALPHA = (2.0 * DEPTH) ** 0.25
NEG = -1e30
LOG2_E = 1.4426950408889634

LANES = 128
COL_KA = SWA_Q_COLS
COL_VA = COL_KA + SWA_KV_COLS
COL_CQ = COL_VA + SWA_KV_COLS
COL_CKV = COL_CQ + MLA_Q_RANK
COL_KR = COL_CKV + MLA_KV_RANK
IN_COLS_PADDED = COL_KR + LANES
MLA_QK_PAD = 256
ROUTE_LANES = 128
EXPERT_LANE0 = 32
ROUTE_CLS, ROUTE_RANK = range(2)
ROUTE_ROWS = 8
PAIRS_PER_GROUP = EXPERTS_PER_GROUP * (EXPERTS_PER_GROUP - 1) // 2
N_CLASSES = N_GROUPS * PAIRS_PER_GROUP
MOE_TILE_ROWS = 256
PROJ_TILE = 1024
WIDE_TILE = 1024
DISPATCH_TILE = 2048
COMBINE_TILE = 1024
DMA_BURST = 8
VMEM_LIMIT = 56 * 1024 * 1024

SWA_HEAD_ORDER = (0, 4, 1, 5, 2, 6, 3, 7)

BF16 = jnp.bfloat16
F32 = jnp.float32


def _cparams(sem):
    return pltpu.CompilerParams(dimension_semantics=sem, vmem_limit_bytes=VMEM_LIMIT)


def _layer_norm(x, g, b):
    mu = jnp.mean(x, axis=-1, keepdims=True)
    xc = x - mu
    var = jnp.mean(xc * xc, axis=-1, keepdims=True)
    return xc * lax.rsqrt(var + LN_EPS) * g + b


def _rms_scale(x):
    return lax.rsqrt(jnp.mean(x * x, axis=-1, keepdims=True) + RMS_EPS)


def _rope128(x, cos, sin_signed, lane_lo):
    up = pltpu.roll(x, LANES - 32, 1)
    dn = pltpu.roll(x, 32, 1)
    return x * cos + jnp.where(lane_lo, up, dn) * sin_signed


def _dot(a, b):
    return jnp.dot(a, b, preferred_element_type=F32)


def _dot_nt(a, b):
    return lax.dot_general(a, b, (((1,), (1,)), ((), ())), preferred_element_type=F32)


def _in_proj_kernel(x_ref, g_ref, b_ref, w_ref, cos_ref, sin_ref, cost_ref, sint_ref, gq_ref,
                    wuqt_ref, gkv_ref, wuk_ref, wuvt_ref,
                    qa_ref, ka_ref, va_ref, qt_ref, km_ref, vt_ref):
    h = _layer_norm(x_ref[0], g_ref[...], b_ref[...])
    u = _dot(h.astype(BF16), w_ref[...])
    cos = cos_ref[...]
    sin = sin_ref[...]
    lane = lax.broadcasted_iota(jnp.int32, (1, LANES), 1)
    lane_lo = (lane % 64) < 32
    rope = functools.partial(_rope128, cos=cos, sin_signed=sin, lane_lo=lane_lo)

    swa_scale = SWA_HEAD_DIM ** -0.5
    for c in range(SWA_Q_COLS // LANES):
        qc = rope(u[:, c * LANES:(c + 1) * LANES]) * swa_scale
        qa_ref[0, :, c * LANES:(c + 1) * LANES] = qc.astype(BF16)
    ka_ref[0] = rope(u[:, COL_KA:COL_VA]).astype(BF16)
    va_ref[0] = u[:, COL_VA:COL_CQ].astype(BF16)

    cq = u[:, COL_CQ:COL_CKV]
    cqn = (cq * _rms_scale(cq) * gq_ref[...]).astype(BF16)
    ckv = u[:, COL_CKV:COL_KR]
    ckvn = (ckv * _rms_scale(ckv) * gkv_ref[...]).astype(BF16)
    k_nope = _dot(ckvn, wuk_ref[...])
    kr = rope(u[:, COL_KR:IN_COLS_PADDED]).astype(BF16)
    q_scale = (MLA_NOPE_DIM + MLA_ROPE_DIM) ** -0.5 * LOG2_E
    cost = cost_ref[...]
    sint = sint_ref[...]
    half = MLA_ROPE_DIM // 2
    for hd in range(MLA_HEADS):
        qt = _dot_nt(wuqt_ref[hd], cqn)
        qr = qt[MLA_NOPE_DIM:MLA_NOPE_DIM + MLA_ROPE_DIM]
        rot = jnp.concatenate([-qr[half:], qr[:half]], axis=0)
        qt_ref[0, hd, 0:MLA_NOPE_DIM, :] = (qt[:MLA_NOPE_DIM] * q_scale).astype(BF16)
        qt_ref[0, hd, MLA_NOPE_DIM:MLA_NOPE_DIM + MLA_ROPE_DIM, :] = (
            (qr * cost + rot * sint) * q_scale).astype(BF16)
        qt_ref[0, hd, MLA_NOPE_DIM + MLA_ROPE_DIM:, :] = (
            qt[MLA_NOPE_DIM + MLA_ROPE_DIM:]).astype(BF16)
        km_ref[0, hd, :, 0:LANES] = k_nope[:, hd * LANES:(hd + 1) * LANES].astype(BF16)
        km_ref[0, hd, :, LANES:2 * LANES] = kr
        vt_ref[0, hd, 0] = _dot_nt(wuvt_ref[hd], ckvn).astype(BF16)


def _in_proj(x, tables, p, tm):
    cos, sin, cos_t, sin_t = tables
    B, S, D = x.shape
    grid = (B, S // tm)
    full = lambda shape: pl.BlockSpec(shape, lambda b, i: (0,) * len(shape))
    out_shape = (
        jax.ShapeDtypeStruct((B, S, SWA_Q_COLS), BF16),
        jax.ShapeDtypeStruct((B, S, SWA_KV_COLS), BF16),
        jax.ShapeDtypeStruct((B, S, SWA_KV_COLS), BF16),
        jax.ShapeDtypeStruct((B, MLA_HEADS, MLA_QK_PAD, S), BF16),
        jax.ShapeDtypeStruct((B, MLA_HEADS, S, MLA_QK_PAD), BF16),
        jax.ShapeDtypeStruct((B, MLA_HEADS, S // tm, MLA_V_DIM, tm), BF16),
    )
    row = lambda w: pl.BlockSpec((1, tm, w), lambda b, i: (b, i, 0))
    head = lambda w: pl.BlockSpec((1, MLA_HEADS, tm, w), lambda b, i: (b, 0, i, 0))
    return pl.pallas_call(
        _in_proj_kernel,
        grid=grid,
        in_specs=[
            row(D),
            full((1, D)), full((1, D)),
            full((D, IN_COLS_PADDED)),
            pl.BlockSpec((tm, LANES), lambda b, i: (i, 0)),
            pl.BlockSpec((tm, LANES), lambda b, i: (i, 0)),
            pl.BlockSpec((MLA_ROPE_DIM, tm), lambda b, i: (0, i)),
            pl.BlockSpec((MLA_ROPE_DIM, tm), lambda b, i: (0, i)),
            full((1, MLA_Q_RANK)), full((MLA_HEADS, MLA_QK_PAD, MLA_Q_RANK)),
            full((1, MLA_KV_RANK)), full((MLA_KV_RANK, MLA_HEADS * MLA_NOPE_DIM)),
            full((MLA_HEADS, MLA_V_DIM, MLA_KV_RANK)),
        ],
        out_specs=(row(SWA_Q_COLS), row(SWA_KV_COLS), row(SWA_KV_COLS),
                   pl.BlockSpec((1, MLA_HEADS, MLA_QK_PAD, tm), lambda b, i: (b, 0, 0, i)),
                   head(MLA_QK_PAD),
                   pl.BlockSpec((1, MLA_HEADS, 1, MLA_V_DIM, tm), lambda b, i: (b, 0, i, 0, 0))),
        out_shape=out_shape,
        compiler_params=_cparams(("parallel", "parallel")),
        name="in_proj",
    )(x, p["ln_in_g"], p["ln_in_b"], p["w_in"], cos, sin, cos_t, sin_t, p["gq"], p["w_uq_t"],
      p["gkv"], p["w_uk"], p["w_uv_t"])


def _swa_blocks(sink_ref, qa_ref, kc_ref, vc_ref, kprev, vprev, first, o_ref, blocks_per_step):
    row = lax.broadcasted_iota(jnp.int32, (2 * BLOCK, BLOCK), 0) % BLOCK
    col = lax.broadcasted_iota(jnp.int32, (2 * BLOCK, BLOCK), 1)
    own = col <= row
    meta_ok = col >= jnp.where(first, N_PAD, 0)
    lane = lax.broadcasted_iota(jnp.int32, (1, LANES), 1)
    lane_kv0 = lane < SWA_HEAD_DIM
    top_rows = lax.broadcasted_iota(jnp.int32, (2 * BLOCK, 1), 0) < BLOCK

    for r in range(blocks_per_step):
        rows = slice(r * BLOCK, (r + 1) * BLOCK)
        if r == 0:
            kp, vp = kprev, vprev
        else:
            prev_rows = slice((r - 1) * BLOCK, r * BLOCK)
            kp, vp = kc_ref[0, prev_rows, :], vc_ref[0, prev_rows, :]
        kc, vc = kc_ref[0, rows, :], vc_ref[0, rows, :]
        for c in range(SWA_Q_COLS // LANES):
            qc = qa_ref[0, rows, c * LANES:(c + 1) * LANES]
            zero = jnp.zeros_like(qc)
            q2 = jnp.concatenate([jnp.where(lane_kv0, qc, zero),
                                  jnp.where(lane_kv0, zero, qc)], axis=0)
            s_prev = _dot_nt(q2, kp)
            if r == 0:
                s_prev = jnp.where(meta_ok, s_prev, NEG)
            s = jnp.where(own, _dot_nt(q2, kc), s_prev)
            sink = jnp.where(top_rows, sink_ref[c], sink_ref[c + 4])
            m = jnp.maximum(jnp.max(s, axis=-1, keepdims=True), sink)
            e = jnp.exp(s - m)
            denom = jnp.sum(e, axis=-1, keepdims=True) + jnp.exp(sink - m)
            pr = e * (1.0 / denom)
            o2 = (_dot(jnp.where(own, pr, 0.0).astype(BF16), vc)
                  + _dot(jnp.where(own, 0.0, pr).astype(BF16), vp))
            o = jnp.where(lane_kv0, o2[:BLOCK], o2[BLOCK:])
            o_ref[0, rows, c * LANES:(c + 1) * LANES] = o.astype(BF16)


def _in_proj_swa_kernel(sink_ref, x_ref, g_ref, b_ref, w_ref, cos_ref, sin_ref, cost_ref, sint_ref,
                        gq_ref, wuqt_ref, gkv_ref, wuk_ref, wuvt_ref, kmeta_ref, vmeta_ref,
                        a_ref, qt_ref, km_ref, vt_ref,
                        qa_s, ka_s, va_s, kprev_s, vprev_s):
    _in_proj_kernel(x_ref, g_ref, b_ref, w_ref, cos_ref, sin_ref, cost_ref, sint_ref, gq_ref,
                    wuqt_ref, gkv_ref, wuk_ref, wuvt_ref, qa_s, ka_s, va_s, qt_ref, km_ref, vt_ref)
    first = pl.program_id(1) == 0
    kprev = jnp.where(first, kmeta_ref[0], kprev_s[...])
    vprev = jnp.where(first, vmeta_ref[0], vprev_s[...])
    tm = qa_s.shape[1]
    _swa_blocks(sink_ref, qa_s, ka_s, va_s, kprev, vprev, first, a_ref, tm // BLOCK)
    kprev_s[...] = ka_s[0, tm - BLOCK:, :]
    vprev_s[...] = va_s[0, tm - BLOCK:, :]


def _in_proj_swa(x, tables, p, ka_meta, va_meta, tm):
    cos, sin, cos_t, sin_t = tables
    B, S, D = x.shape
    kvw = SWA_KV_COLS
    full = lambda shape: pl.BlockSpec(shape, lambda b, i: (0,) * len(shape))
    row = lambda w: pl.BlockSpec((1, tm, w), lambda b, i: (b, i, 0))
    out_shape = (
        jax.ShapeDtypeStruct((B, S, SWA_Q_COLS), BF16),
        jax.ShapeDtypeStruct((B, MLA_HEADS, MLA_QK_PAD, S), BF16),
        jax.ShapeDtypeStruct((B, MLA_HEADS, S, MLA_QK_PAD), BF16),
        jax.ShapeDtypeStruct((B, MLA_HEADS, S // tm, MLA_V_DIM, tm), BF16),
    )
    return pl.pallas_call(
        _in_proj_swa_kernel,
        grid=(B, S // tm),
        in_specs=[
            pl.BlockSpec(memory_space=pltpu.SMEM),
            row(D),
            full((1, D)), full((1, D)),
            full((D, IN_COLS_PADDED)),
            pl.BlockSpec((tm, LANES), lambda b, i: (i, 0)),
            pl.BlockSpec((tm, LANES), lambda b, i: (i, 0)),
            pl.BlockSpec((MLA_ROPE_DIM, tm), lambda b, i: (0, i)),
            pl.BlockSpec((MLA_ROPE_DIM, tm), lambda b, i: (0, i)),
            full((1, MLA_Q_RANK)), full((MLA_HEADS, MLA_QK_PAD, MLA_Q_RANK)),
            full((1, MLA_KV_RANK)), full((MLA_KV_RANK, MLA_HEADS * MLA_NOPE_DIM)),
            full((MLA_HEADS, MLA_V_DIM, MLA_KV_RANK)),
            full((1, BLOCK, kvw)), full((1, BLOCK, kvw)),
        ],
        out_specs=(row(SWA_Q_COLS),
                   pl.BlockSpec((1, MLA_HEADS, MLA_QK_PAD, tm), lambda b, i: (b, 0, 0, i)),
                   pl.BlockSpec((1, MLA_HEADS, tm, MLA_QK_PAD), lambda b, i: (b, 0, i, 0)),
                   pl.BlockSpec((1, MLA_HEADS, 1, MLA_V_DIM, tm), lambda b, i: (b, 0, i, 0, 0))),
        out_shape=out_shape,
        scratch_shapes=[pltpu.VMEM((1, tm, SWA_Q_COLS), BF16), pltpu.VMEM((1, tm, kvw), BF16),
                        pltpu.VMEM((1, tm, kvw), BF16), pltpu.VMEM((BLOCK, kvw), BF16),
                        pltpu.VMEM((BLOCK, kvw), BF16)],
        compiler_params=_cparams(("parallel", "arbitrary")),
        name="in_proj_swa",
    )(p["sinks"], x, p["ln_in_g"], p["ln_in_b"], p["w_in"], cos, sin, cos_t, sin_t, p["gq"],
      p["w_uq_t"], p["gkv"], p["w_uk"], p["w_uv_t"], ka_meta, va_meta)


def _mla_kernel(qt_ref, k_ref, vt_ref, kmeta_ref, vtmeta_ref, o_ref,
                s0a_ref, s0b_ref, s1a_ref, s1b_ref, smax_ref, m_ref, l_ref, acc_ref, *, tq):
    u = pl.program_id(2)
    qts = (qt_ref[0, 0, :, :tq], qt_ref[0, 0, :, tq:])
    bufs = ((s0a_ref, s0b_ref), (s1a_ref, s1b_ref))

    for g in range(2):
        s = _dot(kmeta_ref[0, 0], qts[g])
        m = jnp.max(s, axis=0, keepdims=True)
        p = jnp.exp2(s - m)
        m_ref[g] = m
        l_ref[g] = jnp.sum(p, axis=0, keepdims=True)
        acc_ref[g] = _dot(vtmeta_ref[0, 0, 0], p.astype(BF16))

    def scores_into(g, ab, j):
        start = pl.multiple_of(j * tq, tq)
        s = _dot(k_ref[0, 0, pl.ds(start, tq), :], qts[g])
        bufs[g][ab][...] = s
        smax_ref[g, ab] = jnp.max(s, axis=0, keepdims=True)

    def absorb(g, ab, j, causal=False):
        s = bufs[g][ab][...]
        if causal:
            key = lax.broadcasted_iota(jnp.int32, (tq, tq), 0)
            query = lax.broadcasted_iota(jnp.int32, (tq, tq), 1)
            s = jnp.where(key <= query, s, NEG)
            smax = jnp.max(s, axis=0, keepdims=True)
        else:
            smax = smax_ref[g, ab]
        m = m_ref[g]
        m_new = jnp.maximum(m, smax)
        alpha = jnp.exp2(m - m_new)
        p = jnp.exp2(s - m_new)
        m_ref[g] = m_new
        l_ref[g] = alpha * l_ref[g] + jnp.sum(p, axis=0, keepdims=True)
        acc_ref[g] = alpha * acc_ref[g] + _dot(vt_ref[0, 0, j], p.astype(BF16))

    buf_a, buf_b = 0, 1
    for g in range(2):
        scores_into(g, buf_a, 0)

    def pair(t, carry):
        scores_into(0, buf_b, 2 * t + 1)
        absorb(0, buf_a, 2 * t)
        scores_into(1, buf_b, 2 * t + 1)
        scores_into(0, buf_a, 2 * t + 2)
        absorb(1, buf_a, 2 * t)
        absorb(0, buf_b, 2 * t + 1)
        scores_into(1, buf_a, 2 * t + 2)
        absorb(1, buf_b, 2 * t + 1)
        return carry

    lax.fori_loop(0, u, pair, 0)

    scores_into(1, buf_b, 2 * u + 1)
    absorb(1, buf_a, 2 * u)
    absorb(0, buf_a, 2 * u, causal=True)
    absorb(1, buf_b, 2 * u + 1, causal=True)

    for g in range(2):
        o_ref[0, g * tq:(g + 1) * tq, :] = (acc_ref[g] * (1.0 / l_ref[g])).T.astype(BF16)


def _mla(qt, km, vt, km_meta, vt_meta, tq):
    B, H, _, S = qt.shape
    assert S % (2 * tq) == 0, "a step handles an (even, odd) pair of query tiles"
    return pl.pallas_call(
        functools.partial(_mla_kernel, tq=tq),
        grid=(B, H, S // (2 * tq)),
        in_specs=[
            pl.BlockSpec((1, 1, MLA_QK_PAD, 2 * tq), lambda b, h, u: (b, h, 0, u)),
            pl.BlockSpec((1, 1, S, MLA_QK_PAD), lambda b, h, u: (b, h, 0, 0)),
            pl.BlockSpec((1, 1, S // tq, MLA_V_DIM, tq), lambda b, h, u: (b, h, 0, 0, 0)),
            pl.BlockSpec((1, 1, N_META, MLA_QK_PAD), lambda b, h, u: (0, h, 0, 0)),
            pl.BlockSpec((1, 1, 1, MLA_V_DIM, N_META), lambda b, h, u: (0, h, 0, 0, 0)),
        ],
        out_specs=pl.BlockSpec((1, 2 * tq, MLA_V_DIM), lambda b, h, u: (b, u, h)),
        out_shape=jax.ShapeDtypeStruct((B, S, MLA_OUT_COLS), BF16),
        scratch_shapes=[pltpu.VMEM((tq, tq), F32), pltpu.VMEM((tq, tq), F32),
                        pltpu.VMEM((tq, tq), F32), pltpu.VMEM((tq, tq), F32),
                        pltpu.VMEM((2, 2, 1, tq), F32),
                        pltpu.VMEM((2, 1, tq), F32), pltpu.VMEM((2, 1, tq), F32),
                        pltpu.VMEM((2, MLA_V_DIM, tq), F32)],
        compiler_params=_cparams(("parallel", "parallel", "arbitrary")),
        name="mla_attn",
    )(qt, km, vt, km_meta, vt_meta)


def _out_proj_kernel(x_ref, a_ref, b_ref, gin_ref, bin_ref, ga_ref, gb_ref, wo_ref, g1_ref, b1_ref,
                     wrt_ref, brt_ref, h1_ref, route_ref, counts_out_ref, count_ref):
    h = _layer_norm(x_ref[0], gin_ref[...], bin_ref[...])
    a = a_ref[0].astype(F32)
    b = b_ref[0].astype(F32)
    an = (a * _rms_scale(a) * ga_ref[...]).astype(BF16)
    bn = (b * _rms_scale(b) * gb_ref[...]).astype(BF16)
    mix = _dot(jnp.concatenate([an, bn], axis=-1), wo_ref[...])
    h1 = _layer_norm(ALPHA * h + mix, g1_ref[...], b1_ref[...])
    h1_ref[0] = h1

    x_hi = h1.astype(BF16)
    x_lo = (h1 - x_hi.astype(F32)).astype(BF16)
    hi_both = _dot(x_hi, wrt_ref[...])
    logits = (hi_both[:, :ROUTE_LANES] + hi_both[:, ROUTE_LANES:]
              + _dot(x_lo, wrt_ref[:, :ROUTE_LANES]) + brt_ref[...])
    lane = lax.broadcasted_iota(jnp.int32, logits.shape, 1).astype(F32)
    ninf = -jnp.inf
    no_lane = float(ROUTE_LANES)
    gl = jnp.where(lane < N_GROUPS, logits, ninf)
    gmax = jnp.max(gl, axis=-1, keepdims=True)
    g_idx = jnp.min(jnp.where(gl == gmax, lane, no_lane), axis=-1, keepdims=True)
    lo = EXPERT_LANE0 + g_idx * EXPERTS_PER_GROUP
    el = jnp.where((lane >= lo) & (lane < lo + EXPERTS_PER_GROUP), logits, ninf)
    m1 = jnp.max(el, axis=-1, keepdims=True)
    i1 = jnp.min(jnp.where(el == m1, lane, no_lane), axis=-1, keepdims=True)
    el2 = jnp.where(lane == i1, ninf, el)
    m2 = jnp.max(el2, axis=-1, keepdims=True)
    i2 = jnp.min(jnp.where(el2 == m2, lane, no_lane), axis=-1, keepdims=True)

    a = jnp.minimum(i1, i2) - lo
    b = jnp.maximum(i1, i2) - lo
    cls = (g_idx * PAIRS_PER_GROUP + a * (2.0 * EXPERTS_PER_GROUP - 1.0 - a) * 0.5
           + (b - a - 1.0))

    @pl.when((pl.program_id(0) == 0) & (pl.program_id(1) == 0))
    def _():
        count_ref[...] = jnp.zeros_like(count_ref)

    tm = logits.shape[0]
    onehot = jnp.where(lane == cls, 1.0, 0.0)
    r_i = lax.broadcasted_iota(jnp.int32, (BLOCK, BLOCK), 0)
    c_i = lax.broadcasted_iota(jnp.int32, (BLOCK, BLOCK), 1)
    earlier = jnp.where(r_i > c_i, 1.0, 0.0).astype(BF16)
    running = count_ref[...]
    before = []
    for blk in range(tm // BLOCK):
        oh = onehot[blk * BLOCK:(blk + 1) * BLOCK]
        before.append(_dot(earlier, oh.astype(BF16)) + running)
        running = running + jnp.sum(oh, axis=0, keepdims=True)
    before = jnp.concatenate(before, axis=0)
    rank = jnp.sum(jnp.where(lane == cls, before, 0.0), axis=-1, keepdims=True)
    count_ref[...] = running
    counts_out_ref[...] = running

    route = jnp.where(lane == ROUTE_CLS, cls, 0.0)
    route = jnp.where(lane == ROUTE_RANK, rank, route)
    route_ref[0, 0] = route.T[:ROUTE_ROWS].astype(jnp.int32)


def _out_proj(x, a_out, b_out, p, tm):
    B, S, D = x.shape
    full = lambda shape: pl.BlockSpec(shape, lambda b, i: (0,) * len(shape))
    row = lambda w: pl.BlockSpec((1, tm, w), lambda b, i: (b, i, 0))
    return pl.pallas_call(
        _out_proj_kernel,
        grid=(B, S // tm),
        in_specs=[row(D), row(SWA_Q_COLS), row(MLA_OUT_COLS),
                  full((1, D)), full((1, D)), full((1, SWA_Q_COLS)), full((1, MLA_OUT_COLS)),
                  full((D, D)), full((1, D)), full((1, D)),
                  full((D, 2 * ROUTE_LANES)), full((1, ROUTE_LANES))],
        out_specs=(row(D), pl.BlockSpec((1, 1, ROUTE_ROWS, tm), lambda b, i: (b, i, 0, 0)),
                   full((1, ROUTE_LANES))),
        out_shape=(jax.ShapeDtypeStruct((B, S, D), F32),
                   jax.ShapeDtypeStruct((B, S // tm, ROUTE_ROWS, tm), jnp.int32),
                   jax.ShapeDtypeStruct((1, ROUTE_LANES), F32)),
        scratch_shapes=[pltpu.VMEM((1, ROUTE_LANES), F32)],
        compiler_params=_cparams(("arbitrary", "arbitrary")),
        name="out_proj_route",
    )(x, a_out, b_out, p["ln_in_g"], p["ln_in_b"], p["ga"], p["gb"], p["w_o"], p["ln1_g"],
      p["ln1_b"], p["w_route"], p["b_route"])


SLAB = D_MODEL // LANES
TAIL_CHUNKS = tuple(MOE_TILE_ROWS >> b for b in range(MOE_TILE_ROWS.bit_length()))


def _slab_copy(src_ref, src_tok8, dst_ref, dst_tok8, sem):
    return pltpu.make_async_copy(src_ref.at[pl.ds(pl.multiple_of(src_tok8, SLAB), SLAB)],
                                 dst_ref.at[pl.ds(pl.multiple_of(dst_tok8, SLAB), SLAB)], sem)


def _to_slabs(x, slab_ref, n):
    for s in range(SLAB):
        slab_ref[pl.ds(s, n, stride=SLAB), :] = x[:, s * LANES:(s + 1) * LANES]


def _from_slabs(slab_ref, n):
    return jnp.concatenate([slab_ref[pl.ds(s, n, stride=SLAB), :] for s in range(SLAB)], axis=1)


def _dispatch_kernel(padstart_ref, taillen_ref, nu_ref, pos_ref, h1_ref, xs_ref, rows_ref, zero_ref,
                     sem, zero_sem, *, tm, n_tiles, slots_per_step):
    t = pl.program_id(0)

    def zero_fill(start, rows):
        return pltpu.make_async_copy(
            zero_ref.at[pl.ds(0, rows * SLAB)],
            xs_ref.at[pl.ds(pl.multiple_of(start, SLAB), rows * SLAB)], zero_sem)

    def zero_fills(act):
        def one_slot(i, carry):
            c = t * slots_per_step + i

            @pl.when(c < N_CLASSES)
            def _():
                tail = taillen_ref[c]
                for rows in TAIL_CHUNKS:
                    @pl.when((tail & rows) != 0)
                    def _():
                        done = tail & ~(2 * rows - 1)
                        act(zero_fill(padstart_ref[c] + done * SLAB, rows))

            @pl.when(nu_ref[0] + c < n_tiles)
            def _():
                act(zero_fill((nu_ref[0] + c) * (MOE_TILE_ROWS * SLAB), MOE_TILE_ROWS))

            return carry

        lax.fori_loop(0, slots_per_step, one_slot, 0)

    @pl.when(t == 0)
    def _():
        zero_ref[...] = jnp.zeros_like(zero_ref)

    zero_fills(lambda fill: fill.start())
    _to_slabs(h1_ref[...], rows_ref, tm)

    def issue(r0, carry):
        for k in range(DMA_BURST):
            r = r0 * DMA_BURST + k
            _slab_copy(rows_ref, r * SLAB, xs_ref, pos_ref[0, 0, r], sem).start(priority=k % 2)
        return carry

    lax.fori_loop(0, tm // DMA_BURST, issue, 0)
    pltpu.make_async_copy(rows_ref, xs_ref.at[pl.ds(0, tm * SLAB)], sem).wait()
    zero_fills(lambda fill: fill.wait())


def _dispatch(h1, pos8, pad_start8, tail_len, n_used, n_tiles, tm):
    T, D = h1.shape
    n_steps = T // tm
    pos8 = pos8.reshape(n_steps, 1, tm)
    grid_spec = pltpu.PrefetchScalarGridSpec(
        num_scalar_prefetch=3,
        grid=(n_steps,),
        in_specs=[pl.BlockSpec((1, 1, tm), lambda t, ps, tl, nu: (t, 0, 0),
                               memory_space=pltpu.SMEM),
                  pl.BlockSpec((tm, D), lambda t, ps, tl, nu: (t, 0))],
        out_specs=pl.BlockSpec(memory_space=pl.ANY),
        scratch_shapes=[pltpu.VMEM((tm * SLAB, LANES), F32),
                        pltpu.VMEM((MOE_TILE_ROWS * SLAB, LANES), F32),
                        pltpu.SemaphoreType.DMA,
                        pltpu.SemaphoreType.DMA],
    )
    return pl.pallas_call(
        functools.partial(_dispatch_kernel, tm=tm, n_tiles=n_tiles,
                          slots_per_step=pl.cdiv(N_CLASSES, n_steps)),
        grid_spec=grid_spec,
        out_shape=jax.ShapeDtypeStruct((n_tiles * MOE_TILE_ROWS * SLAB, LANES), F32),
        compiler_params=_cparams(("arbitrary",)),
        name="moe_dispatch",
    )(pad_start8, tail_len, n_used, pos8, h1)


def _experts_kernel(tlo_ref, thi_ref, nu_ref, x_ref, wg_lo_ref, wg_hi_ref, wu_lo_ref, wu_hi_ref,
                    wd_lo_ref, wd_hi_ref, wrt_ref, brt_ref, y_ref):
    j = pl.program_id(0)
    tr = MOE_TILE_ROWS

    @pl.when(j < nu_ref[0])
    def _():
        x = _from_slabs(x_ref, tr).astype(BF16)
        lo_e = tlo_ref[j]
        hi_e = thi_ref[j]
        logits = _dot(x, wrt_ref[...]) + brt_ref[...]
        lane = lax.broadcasted_iota(jnp.int32, logits.shape, 1)
        gl = jnp.where(lane < N_GROUPS, logits, -jnp.inf)
        ge = jnp.exp(gl - jnp.max(gl, axis=-1, keepdims=True))
        pick = lambda v, at: jnp.sum(jnp.where(lane == at, v, 0.0), axis=-1, keepdims=True)
        g_prob = pick(ge, lo_e // EXPERTS_PER_GROUP) / jnp.sum(ge, axis=-1, keepdims=True)
        e_lo = pick(logits, EXPERT_LANE0 + lo_e)
        e_hi = pick(logits, EXPERT_LANE0 + hi_e)
        gates = (g_prob / (1.0 + jnp.exp(e_hi - e_lo)), g_prob / (1.0 + jnp.exp(e_lo - e_hi)))

        y = None
        for gate, wg_ref, wu_ref, wd_ref in zip(gates, (wg_lo_ref, wg_hi_ref), (wu_lo_ref, wu_hi_ref),
                                                (wd_lo_ref, wd_hi_ref)):
            g = _dot(x, wg_ref[0])
            u = _dot(x, wu_ref[0])
            hid = g * (1.0 / (1.0 + jnp.exp(-g))) * u * gate
            y_k = _dot(hid.astype(BF16), wd_ref[0])
            y = y_k if y is None else y + y_k
        _to_slabs(y, y_ref, tr)


def _experts(xs, tile_lo, tile_hi, n_used, p, n_tiles):
    D = D_MODEL
    tr = MOE_TILE_ROWS
    last_used = lambda j, nu: jnp.minimum(j, nu[0] - 1)
    wgu = lambda pick: pl.BlockSpec((1, D, D_EXPERT),
                                    lambda j, tlo, thi, nu: (pick(tlo, thi)[j], 0, 0))
    wd = lambda pick: pl.BlockSpec((1, D_EXPERT, D),
                                   lambda j, tlo, thi, nu: (pick(tlo, thi)[j], 0, 0))
    lo = lambda tlo, thi: tlo
    hi = lambda tlo, thi: thi
    grid_spec = pltpu.PrefetchScalarGridSpec(
        num_scalar_prefetch=3,
        grid=(n_tiles,),
        in_specs=[pl.BlockSpec((tr * SLAB, LANES), lambda j, tlo, thi, nu: (last_used(j, nu), 0)),
                  wgu(lo), wgu(hi), wgu(lo), wgu(hi), wd(lo), wd(hi),
                  pl.BlockSpec((D, ROUTE_LANES), lambda j, tlo, thi, nu: (0, 0)),
                  pl.BlockSpec((1, ROUTE_LANES), lambda j, tlo, thi, nu: (0, 0))],
        out_specs=pl.BlockSpec((tr * SLAB, LANES), lambda j, tlo, thi, nu: (last_used(j, nu), 0)),
    )
    return pl.pallas_call(
        _experts_kernel,
        grid_spec=grid_spec,
        out_shape=jax.ShapeDtypeStruct(xs.shape, F32),
        input_output_aliases={3: 0},
        compiler_params=_cparams(("arbitrary",)),
        name="moe_experts",
    )(tile_lo, tile_hi, n_used, xs, p["w_gate"], p["w_gate"], p["w_up"], p["w_up"],
      p["w_down"], p["w_down"],
      p["w_route"], p["b_route"])


def _combine_kernel(pos_ref, pos_next_ref, h1_ref, ys_ref, g2_ref, b2_ref, o_ref,
                    ybuf_ref, sem, *, tm):
    t = pl.program_id(0)
    slot = t % 2

    def gather(idx_ref, dst_slot):
        def issue(r0, carry):
            for k in range(DMA_BURST):
                r = r0 * DMA_BURST + k
                _slab_copy(ys_ref, idx_ref[0, 0, r], ybuf_ref.at[dst_slot], r * SLAB,
                           sem.at[dst_slot]).start(priority=k % 2)
            return carry

        lax.fori_loop(0, tm // DMA_BURST, issue, 0)

    @pl.when(t == 0)
    def _():
        gather(pos_ref, 0)

    @pl.when(t + 1 < pl.num_programs(0))
    def _():
        gather(pos_next_ref, 1 - slot)

    pltpu.make_async_copy(ys_ref.at[pl.ds(0, tm * SLAB)], ybuf_ref.at[slot], sem.at[slot]).wait()
    ffn = _from_slabs(ybuf_ref.at[slot], tm)
    o_ref[...] = _layer_norm(ALPHA * h1_ref[...] + ffn, g2_ref[...], b2_ref[...])


def _combine(h1, pos8, ys, p, tm):
    T, D = h1.shape
    n_steps = T // tm
    pos8 = pos8.reshape(n_steps, 1, tm)
    full = lambda shape: pl.BlockSpec(shape, lambda t: (0,) * len(shape))
    return pl.pallas_call(
        functools.partial(_combine_kernel, tm=tm),
        grid=(n_steps,),
        in_specs=[pl.BlockSpec((1, 1, tm), lambda t: (t, 0, 0), memory_space=pltpu.SMEM),
                  pl.BlockSpec((1, 1, tm), lambda t: (jnp.minimum(t + 1, n_steps - 1), 0, 0),
                               memory_space=pltpu.SMEM),
                  pl.BlockSpec((tm, D), lambda t: (t, 0)),
                  pl.BlockSpec(memory_space=pl.ANY),
                  full((1, D)), full((1, D))],
        out_specs=pl.BlockSpec((tm, D), lambda t: (t, 0)),
        out_shape=jax.ShapeDtypeStruct((T, D), F32),
        scratch_shapes=[pltpu.VMEM((2, tm * SLAB, LANES), F32),
                        pltpu.SemaphoreType.DMA((2,))],
        compiler_params=_cparams(("arbitrary",)),
        name="moe_combine_ln2",
    )(pos8, pos8, h1, ys, p["ln2_g"], p["ln2_b"])


def _pair_layout(n):
    order, prev_end = [], None
    for a in range(n - 1):
        end = n - 1 if (a >= n - 2 or (n - 3 - a) % 2 == 0) else n - 2
        middle = [b for b in range(a + 1, n) if b not in (prev_end, end)]
        head = [] if prev_end in (None, end) else [prev_end]
        order += [(a, b) for b in head + middle + [end]]
        prev_end = end
    return order


def _dispatch_plan(route, counts):
    T = route.shape[0] * route.shape[1] * route.shape[3]
    tr = MOE_TILE_ROWS
    n_tiles = T // tr + N_CLASSES
    cnt = counts[0, :N_CLASSES].astype(jnp.int32)
    padded = jnp.maximum((cnt + tr - 1) // tr, 1) * tr
    classes = jnp.arange(N_CLASSES, dtype=jnp.int32)
    pairs = [(a, b) for a in range(EXPERTS_PER_GROUP) for b in range(a + 1, EXPERTS_PER_GROUP)]
    layout = _pair_layout(EXPERTS_PER_GROUP)
    place = jnp.asarray([g * PAIRS_PER_GROUP + layout.index(pair) for g in range(N_GROUPS)
                         for pair in pairs], jnp.int32)
    seg_end = jnp.sum(jnp.where(place[None, :] <= place[:, None], padded[None, :], 0), axis=1)
    seg_start = seg_end - padded
    cls = route[:, :, ROUTE_CLS, :].reshape(T)
    rank = route[:, :, ROUTE_RANK, :].reshape(T)
    start_of = jnp.sum(jnp.where(cls[:, None] == classes, seg_start, 0), axis=-1)
    pos8 = (start_of + rank) * SLAB
    tile_row0 = jnp.arange(n_tiles, dtype=jnp.int32) * tr
    tile_place = jnp.sum((seg_end[None, :] <= tile_row0[:, None]).astype(jnp.int32), axis=-1)
    tile_place = jnp.minimum(tile_place, N_CLASSES - 1)
    lo_of = jnp.asarray([g * EXPERTS_PER_GROUP + a for g in range(N_GROUPS) for a, _ in pairs],
                        jnp.int32)
    hi_of = jnp.asarray([g * EXPERTS_PER_GROUP + b for g in range(N_GROUPS) for _, b in pairs],
                        jnp.int32)
    is_cls = tile_place[:, None] == place
    tile_lo = jnp.sum(jnp.where(is_cls, lo_of, 0), axis=-1)
    tile_hi = jnp.sum(jnp.where(is_cls, hi_of, 0), axis=-1)
    n_used = (jnp.sum(padded, keepdims=True) // tr).astype(jnp.int32)
    return pos8, (seg_start + cnt) * SLAB, padded - cnt, tile_lo, tile_hi, n_used, n_tiles


def _rope_tables(pos):
    d = SWA_HEAD_DIM
    inv_freq = ROPE_THETA ** (-jnp.arange(0, d, 2, dtype=F32) / d)
    ang = pos[:, None] * inv_freq[None, :]
    c, s = jnp.cos(ang), jnp.sin(ang)
    return (jnp.concatenate([c, c, c, c], axis=-1), jnp.concatenate([-s, s, -s, s], axis=-1),
            jnp.concatenate([c, c], axis=-1).T, jnp.concatenate([s, s], axis=-1).T)


def _prep_params(ln_in_g, ln_in_b, w_in, swa_sinks, mla_q_norm_g, mla_w_uq, mla_kv_norm_g, mla_w_ukv,
                 swa_out_norm_g, mla_out_norm_g, w_o, ln1_g, ln1_b, moe_w_group, moe_b_group,
                 moe_w_router, moe_b_router, moe_w_gate, moe_w_up, moe_w_down, ln2_g, ln2_b):
    hd = SWA_HEAD_DIM

    def by_head(t, axis):
        return [lax.slice_in_dim(t, h * hd, (h + 1) * hd, axis=axis) for h in SWA_HEAD_ORDER]

    w = w_in[0]
    w_pad = jnp.zeros((D_MODEL, IN_COLS_PADDED - w.shape[1]), w.dtype)
    w_in_p = jnp.concatenate(by_head(w, 1) + [w[:, SWA_Q_COLS:], w_pad], axis=1).astype(BF16)

    dqk = MLA_NOPE_DIM + MLA_ROPE_DIM
    wuq = mla_w_uq[0].reshape(MLA_Q_RANK, MLA_HEADS, dqk)
    wuq = jnp.pad(wuq, ((0, 0), (0, 0), (0, MLA_QK_PAD - dqk)))
    wuq_t = jnp.transpose(wuq, (1, 2, 0)).astype(BF16)
    wukv = mla_w_ukv[0].reshape(MLA_KV_RANK, MLA_HEADS, MLA_NOPE_DIM + MLA_V_DIM)
    wuk = wukv[:, :, :MLA_NOPE_DIM].reshape(MLA_KV_RANK, MLA_HEADS * MLA_NOPE_DIM).astype(BF16)
    wuv_t = jnp.transpose(wukv[:, :, MLA_NOPE_DIM:], (1, 2, 0)).astype(BF16)

    w_o0 = w_o[0]
    w_o_p = jnp.concatenate(by_head(w_o0, 0) + [w_o0[SWA_Q_COLS:]], axis=0).astype(BF16)

    def route_lanes(group_part, expert_part):
        rows = group_part.shape[0]
        gap = jnp.zeros((rows, EXPERT_LANE0 - N_GROUPS), F32)
        tail = jnp.zeros((rows, ROUTE_LANES - EXPERT_LANE0 - N_EXPERTS), F32)
        return jnp.concatenate([group_part, gap, expert_part, tail], axis=1)

    w_route = route_lanes(moe_w_group[0], moe_w_router[0])
    w_route_hi = w_route.astype(BF16)
    w_route_lo = (w_route - w_route_hi.astype(F32)).astype(BF16)
    w_route = jnp.concatenate([w_route_hi, w_route_lo], axis=1)
    b_route = route_lanes(moe_b_group, moe_b_router)

    return {
        "ln_in_g": ln_in_g[None, :], "ln_in_b": ln_in_b[None, :],
        "w_in": w_in_p,
        "sinks": swa_sinks[0],
        "gq": mla_q_norm_g, "w_uq_t": wuq_t,
        "gkv": mla_kv_norm_g, "w_uk": wuk, "w_uv_t": wuv_t,
        "ga": jnp.concatenate(by_head(swa_out_norm_g, 1), axis=1), "gb": mla_out_norm_g,
        "w_o": w_o_p, "ln1_g": ln1_g, "ln1_b": ln1_b,
        "w_route": w_route, "b_route": b_route,
        "w_gate": moe_w_gate[0].astype(BF16), "w_up": moe_w_up[0].astype(BF16),
        "w_down": moe_w_down[0].astype(BF16),
        "ln2_g": ln2_g, "ln2_b": ln2_b,
    }


def kernel(x, meta_tokens, ln_in_g, ln_in_b, w_in, swa_sinks, mla_q_norm_g, mla_w_uq, mla_kv_norm_g, mla_w_ukv, swa_out_norm_g, mla_out_norm_g, w_o, ln1_g, ln1_b, moe_w_group, moe_b_group, moe_w_router, moe_b_router, moe_w_gate, moe_w_up, moe_w_down, ln2_g, ln2_b):
    B, S, D = x.shape
    p = _prep_params(ln_in_g, ln_in_b, w_in, swa_sinks, mla_q_norm_g, mla_w_uq, mla_kv_norm_g,
                     mla_w_ukv, swa_out_norm_g, mla_out_norm_g, w_o, ln1_g, ln1_b, moe_w_group,
                     moe_b_group, moe_w_router, moe_b_router, moe_w_gate, moe_w_up, moe_w_down,
                     ln2_g, ln2_b)
    T = B * S
    tm = min(PROJ_TILE, S)

    x_meta = jnp.concatenate([jnp.zeros((N_PAD, D), x.dtype), meta_tokens.astype(x.dtype)])[None]
    pos_meta = jnp.maximum(jnp.arange(BLOCK) - N_PAD, 0).astype(F32)
    pos_tok = (jnp.arange(S) + N_META).astype(F32)
    _, ka_m, va_m, _, km_m, vt_m = _in_proj(x_meta, _rope_tables(pos_meta), p, BLOCK)
    a_out, qt, km, vt = _in_proj_swa(x, _rope_tables(pos_tok), p, ka_m, va_m, tm)
    b_out = _mla(qt, km, vt, km_m[:, :, N_PAD:, :], vt_m[..., N_PAD:], tm)
    h1, route, counts = _out_proj(x, a_out, b_out, p, min(WIDE_TILE, S))
    h1 = h1.reshape(B * S, D)

    pos8, pad_start8, tail_len, tile_lo, tile_hi, n_used, n_tiles = _dispatch_plan(route, counts)
    xs = _dispatch(h1, pos8, pad_start8, tail_len, n_used, n_tiles, min(DISPATCH_TILE, T))
    ys = _experts(xs, tile_lo, tile_hi, n_used, p, n_tiles)
    out = _combine(h1, pos8, ys, p, min(COMBINE_TILE, T))
    return out.reshape(B, S, D)
```

```python
import functools

import jax
import jax.numpy as jnp
from jax import lax
from jax.experimental import pallas as pl
from jax.experimental.pallas import tpu as pltpu

D_MODEL = 1024
N_META = 16
BLOCK = 128
N_PAD = BLOCK - N_META
ROPE_THETA = 10000.0
SWA_HEADS = 8
SWA_KV_HEADS = 2
SWA_HEAD_DIM = 64
MLA_HEADS = 4
MLA_Q_RANK = 256
MLA_KV_RANK = 256
MLA_NOPE_DIM = 128
MLA_ROPE_DIM = 64
MLA_V_DIM = 128
SWA_Q_COLS = SWA_HEADS * SWA_HEAD_DIM
SWA_KV_COLS = SWA_KV_HEADS * SWA_HEAD_DIM
MLA_OUT_COLS = MLA_HEADS * MLA_V_DIM
N_GROUPS = 4
EXPERTS_PER_GROUP = 8
N_EXPERTS = N_GROUPS * EXPERTS_PER_GROUP
D_EXPERT = 256
LN_EPS = 1e-5
RMS_EPS = 1e-6
DEPTH = 1
ALPHA = (2.0 * DEPTH) ** 0.25
NEG = -1e30
LOG2_E = 1.4426950408889634

LANES = 128
COL_KA = SWA_Q_COLS
COL_VA = COL_KA + SWA_KV_COLS
COL_CQ = COL_VA + SWA_KV_COLS
COL_CKV = COL_CQ + MLA_Q_RANK
COL_KR = COL_CKV + MLA_KV_RANK
IN_COLS_PADDED = COL_KR + LANES
MLA_QK_PAD = 256
ROUTE_LANES = 128
EXPERT_LANE0 = 32
ROUTE_CLS, ROUTE_RANK = range(2)
ROUTE_ROWS = 8
PAIRS_PER_GROUP = EXPERTS_PER_GROUP * (EXPERTS_PER_GROUP - 1) // 2
N_CLASSES = N_GROUPS * PAIRS_PER_GROUP
MOE_TILE_ROWS = 256
PROJ_TILE = 1024
WIDE_TILE = 1024
DISPATCH_TILE = 2048
COMBINE_TILE = 1024
DMA_BURST = 8
VMEM_LIMIT = 56 * 1024 * 1024

SWA_HEAD_ORDER = (0, 4, 1, 5, 2, 6, 3, 7)

BF16 = jnp.bfloat16
F32 = jnp.float32


def _cparams(sem):
    return pltpu.CompilerParams(dimension_semantics=sem, vmem_limit_bytes=VMEM_LIMIT)


def _layer_norm(x, g, b):
    mu = jnp.mean(x, axis=-1, keepdims=True)
    xc = x - mu
    var = jnp.mean(xc * xc, axis=-1, keepdims=True)
    return xc * lax.rsqrt(var + LN_EPS) * g + b


def _rms_scale(x):
    return lax.rsqrt(jnp.mean(x * x, axis=-1, keepdims=True) + RMS_EPS)


def _rope128(x, cos, sin_signed, lane_lo):
    up = pltpu.roll(x, LANES - 32, 1)
    dn = pltpu.roll(x, 32, 1)
    return x * cos + jnp.where(lane_lo, up, dn) * sin_signed


def _dot(a, b):
    return jnp.dot(a, b, preferred_element_type=F32)


def _dot_nt(a, b):
    return lax.dot_general(a, b, (((1,), (1,)), ((), ())), preferred_element_type=F32)


def _in_proj_kernel(x_ref, g_ref, b_ref, w_ref, cos_ref, sin_ref, cost_ref, sint_ref, gq_ref,
                    wuqt_ref, gkv_ref, wuk_ref, wuvt_ref,
                    qa_ref, ka_ref, va_ref, qt_ref, km_ref, vt_ref):
    h = _layer_norm(x_ref[0], g_ref[...], b_ref[...])
    u = _dot(h.astype(BF16), w_ref[...])
    cos = cos_ref[...]
    sin = sin_ref[...]
    lane = lax.broadcasted_iota(jnp.int32, (1, LANES), 1)
    lane_lo = (lane % 64) < 32
    rope = functools.partial(_rope128, cos=cos, sin_signed=sin, lane_lo=lane_lo)

    swa_scale = SWA_HEAD_DIM ** -0.5
    for c in range(SWA_Q_COLS // LANES):
        qc = rope(u[:, c * LANES:(c + 1) * LANES]) * swa_scale
        qa_ref[0, :, c * LANES:(c + 1) * LANES] = qc.astype(BF16)
    ka_ref[0] = rope(u[:, COL_KA:COL_VA]).astype(BF16)
    va_ref[0] = u[:, COL_VA:COL_CQ].astype(BF16)

    cq = u[:, COL_CQ:COL_CKV]
    cqn = (cq * _rms_scale(cq) * gq_ref[...]).astype(BF16)
    ckv = u[:, COL_CKV:COL_KR]
    ckvn = (ckv * _rms_scale(ckv) * gkv_ref[...]).astype(BF16)
    k_nope = _dot(ckvn, wuk_ref[...])
    kr = rope(u[:, COL_KR:IN_COLS_PADDED]).astype(BF16)
    q_scale = (MLA_NOPE_DIM + MLA_ROPE_DIM) ** -0.5 * LOG2_E
    cost = cost_ref[...]
    sint = sint_ref[...]
    half = MLA_ROPE_DIM // 2
    for hd in range(MLA_HEADS):
        qt = _dot_nt(wuqt_ref[hd], cqn)
        qr = qt[MLA_NOPE_DIM:MLA_NOPE_DIM + MLA_ROPE_DIM]
        rot = jnp.concatenate([-qr[half:], qr[:half]], axis=0)
        qt_ref[0, hd, 0:MLA_NOPE_DIM, :] = (qt[:MLA_NOPE_DIM] * q_scale).astype(BF16)
        qt_ref[0, hd, MLA_NOPE_DIM:MLA_NOPE_DIM + MLA_ROPE_DIM, :] = (
            (qr * cost + rot * sint) * q_scale).astype(BF16)
        qt_ref[0, hd, MLA_NOPE_DIM + MLA_ROPE_DIM:, :] = (
            qt[MLA_NOPE_DIM + MLA_ROPE_DIM:]).astype(BF16)
        km_ref[0, hd, :, 0:LANES] = k_nope[:, hd * LANES:(hd + 1) * LANES].astype(BF16)
        km_ref[0, hd, :, LANES:2 * LANES] = kr
        vt_ref[0, hd, 0] = _dot_nt(wuvt_ref[hd], ckvn).astype(BF16)


def _in_proj(x, tables, p, tm):
    cos, sin, cos_t, sin_t = tables
    B, S, D = x.shape
    grid = (B, S // tm)
    full = lambda shape: pl.BlockSpec(shape, lambda b, i: (0,) * len(shape))
    out_shape = (
        jax.ShapeDtypeStruct((B, S, SWA_Q_COLS), BF16),
        jax.ShapeDtypeStruct((B, S, SWA_KV_COLS), BF16),
        jax.ShapeDtypeStruct((B, S, SWA_KV_COLS), BF16),
        jax.ShapeDtypeStruct((B, MLA_HEADS, MLA_QK_PAD, S), BF16),
        jax.ShapeDtypeStruct((B, MLA_HEADS, S, MLA_QK_PAD), BF16),
        jax.ShapeDtypeStruct((B, MLA_HEADS, S // tm, MLA_V_DIM, tm), BF16),
    )
    row = lambda w: pl.BlockSpec((1, tm, w), lambda b, i: (b, i, 0))
    head = lambda w: pl.BlockSpec((1, MLA_HEADS, tm, w), lambda b, i: (b, 0, i, 0))
    return pl.pallas_call(
        _in_proj_kernel,
        grid=grid,
        in_specs=[
            row(D),
            full((1, D)), full((1, D)),
            full((D, IN_COLS_PADDED)),
            pl.BlockSpec((tm, LANES), lambda b, i: (i, 0)),
            pl.BlockSpec((tm, LANES), lambda b, i: (i, 0)),
            pl.BlockSpec((MLA_ROPE_DIM, tm), lambda b, i: (0, i)),
            pl.BlockSpec((MLA_ROPE_DIM, tm), lambda b, i: (0, i)),
            full((1, MLA_Q_RANK)), full((MLA_HEADS, MLA_QK_PAD, MLA_Q_RANK)),
            full((1, MLA_KV_RANK)), full((MLA_KV_RANK, MLA_HEADS * MLA_NOPE_DIM)),
            full((MLA_HEADS, MLA_V_DIM, MLA_KV_RANK)),
        ],
        out_specs=(row(SWA_Q_COLS), row(SWA_KV_COLS), row(SWA_KV_COLS),
                   pl.BlockSpec((1, MLA_HEADS, MLA_QK_PAD, tm), lambda b, i: (b, 0, 0, i)),
                   head(MLA_QK_PAD),
                   pl.BlockSpec((1, MLA_HEADS, 1, MLA_V_DIM, tm), lambda b, i: (b, 0, i, 0, 0))),
        out_shape=out_shape,
        compiler_params=_cparams(("parallel", "parallel")),
        name="in_proj",
    )(x, p["ln_in_g"], p["ln_in_b"], p["w_in"], cos, sin, cos_t, sin_t, p["gq"], p["w_uq_t"],
      p["gkv"], p["w_uk"], p["w_uv_t"])


def _swa_blocks(sink_ref, qa_ref, kc_ref, vc_ref, kprev, vprev, first, o_ref, blocks_per_step):
    row = lax.broadcasted_iota(jnp.int32, (2 * BLOCK, BLOCK), 0) % BLOCK
    col = lax.broadcasted_iota(jnp.int32, (2 * BLOCK, BLOCK), 1)
    own = col <= row
    meta_ok = col >= jnp.where(first, N_PAD, 0)
    lane = lax.broadcasted_iota(jnp.int32, (1, LANES), 1)
    lane_kv0 = lane < SWA_HEAD_DIM
    top_rows = lax.broadcasted_iota(jnp.int32, (2 * BLOCK, 1), 0) < BLOCK

    for r in range(blocks_per_step):
        rows = slice(r * BLOCK, (r + 1) * BLOCK)
        if r == 0:
            kp, vp = kprev, vprev
        else:
            prev_rows = slice((r - 1) * BLOCK, r * BLOCK)
            kp, vp = kc_ref[0, prev_rows, :], vc_ref[0, prev_rows, :]
        kc, vc = kc_ref[0, rows, :], vc_ref[0, rows, :]
        for c in range(SWA_Q_COLS // LANES):
            qc = qa_ref[0, rows, c * LANES:(c + 1) * LANES]
            zero = jnp.zeros_like(qc)
            q2 = jnp.concatenate([jnp.where(lane_kv0, qc, zero),
                                  jnp.where(lane_kv0, zero, qc)], axis=0)
            s_prev = _dot_nt(q2, kp)
            if r == 0:
                s_prev = jnp.where(meta_ok, s_prev, NEG)
            s = jnp.where(own, _dot_nt(q2, kc), s_prev)
            sink = jnp.where(top_rows, sink_ref[c], sink_ref[c + 4])
            m = jnp.maximum(jnp.max(s, axis=-1, keepdims=True), sink)
            e = jnp.exp(s - m)
            denom = jnp.sum(e, axis=-1, keepdims=True) + jnp.exp(sink - m)
            pr = e * (1.0 / denom)
            o2 = (_dot(jnp.where(own, pr, 0.0).astype(BF16), vc)
                  + _dot(jnp.where(own, 0.0, pr).astype(BF16), vp))
            o = jnp.where(lane_kv0, o2[:BLOCK], o2[BLOCK:])
            o_ref[0, rows, c * LANES:(c + 1) * LANES] = o.astype(BF16)


def _in_proj_swa_kernel(sink_ref, x_ref, g_ref, b_ref, w_ref, cos_ref, sin_ref, cost_ref, sint_ref,
                        gq_ref, wuqt_ref, gkv_ref, wuk_ref, wuvt_ref, kmeta_ref, vmeta_ref,
                        a_ref, qt_ref, km_ref, vt_ref,
                        qa_s, ka_s, va_s, kprev_s, vprev_s):
    _in_proj_kernel(x_ref, g_ref, b_ref, w_ref, cos_ref, sin_ref, cost_ref, sint_ref, gq_ref,
                    wuqt_ref, gkv_ref, wuk_ref, wuvt_ref, qa_s, ka_s, va_s, qt_ref, km_ref, vt_ref)
    first = pl.program_id(1) == 0
    kprev = jnp.where(first, kmeta_ref[0], kprev_s[...])
    vprev = jnp.where(first, vmeta_ref[0], vprev_s[...])
    tm = qa_s.shape[1]
    _swa_blocks(sink_ref, qa_s, ka_s, va_s, kprev, vprev, first, a_ref, tm // BLOCK)
    kprev_s[...] = ka_s[0, tm - BLOCK:, :]
    vprev_s[...] = va_s[0, tm - BLOCK:, :]


def _in_proj_swa(x, tables, p, ka_meta, va_meta, tm):
    cos, sin, cos_t, sin_t = tables
    B, S, D = x.shape
    kvw = SWA_KV_COLS
    full = lambda shape: pl.BlockSpec(shape, lambda b, i: (0,) * len(shape))
    row = lambda w: pl.BlockSpec((1, tm, w), lambda b, i: (b, i, 0))
    out_shape = (
        jax.ShapeDtypeStruct((B, S, SWA_Q_COLS), BF16),
        jax.ShapeDtypeStruct((B, MLA_HEADS, MLA_QK_PAD, S), BF16),
        jax.ShapeDtypeStruct((B, MLA_HEADS, S, MLA_QK_PAD), BF16),
        jax.ShapeDtypeStruct((B, MLA_HEADS, S // tm, MLA_V_DIM, tm), BF16),
    )
    return pl.pallas_call(
        _in_proj_swa_kernel,
        grid=(B, S // tm),
        in_specs=[
            pl.BlockSpec(memory_space=pltpu.SMEM),
            row(D),
            full((1, D)), full((1, D)),
            full((D, IN_COLS_PADDED)),
            pl.BlockSpec((tm, LANES), lambda b, i: (i, 0)),
            pl.BlockSpec((tm, LANES), lambda b, i: (i, 0)),
            pl.BlockSpec((MLA_ROPE_DIM, tm), lambda b, i: (0, i)),
            pl.BlockSpec((MLA_ROPE_DIM, tm), lambda b, i: (0, i)),
            full((1, MLA_Q_RANK)), full((MLA_HEADS, MLA_QK_PAD, MLA_Q_RANK)),
            full((1, MLA_KV_RANK)), full((MLA_KV_RANK, MLA_HEADS * MLA_NOPE_DIM)),
            full((MLA_HEADS, MLA_V_DIM, MLA_KV_RANK)),
            full((1, BLOCK, kvw)), full((1, BLOCK, kvw)),
        ],
        out_specs=(row(SWA_Q_COLS),
                   pl.BlockSpec((1, MLA_HEADS, MLA_QK_PAD, tm), lambda b, i: (b, 0, 0, i)),
                   pl.BlockSpec((1, MLA_HEADS, tm, MLA_QK_PAD), lambda b, i: (b, 0, i, 0)),
                   pl.BlockSpec((1, MLA_HEADS, 1, MLA_V_DIM, tm), lambda b, i: (b, 0, i, 0, 0))),
        out_shape=out_shape,
        scratch_shapes=[pltpu.VMEM((1, tm, SWA_Q_COLS), BF16), pltpu.VMEM((1, tm, kvw), BF16),
                        pltpu.VMEM((1, tm, kvw), BF16), pltpu.VMEM((BLOCK, kvw), BF16),
                        pltpu.VMEM((BLOCK, kvw), BF16)],
        compiler_params=_cparams(("parallel", "arbitrary")),
        name="in_proj_swa",
    )(p["sinks"], x, p["ln_in_g"], p["ln_in_b"], p["w_in"], cos, sin, cos_t, sin_t, p["gq"],
      p["w_uq_t"], p["gkv"], p["w_uk"], p["w_uv_t"], ka_meta, va_meta)


def _mla_kernel(qt_ref, k_ref, vt_ref, kmeta_ref, vtmeta_ref, o_ref,
                s0a_ref, s0b_ref, s1a_ref, s1b_ref, smax_ref, m_ref, l_ref, acc_ref, *, tq):
    u = pl.program_id(2)
    qts = (qt_ref[0, 0, :, :tq], qt_ref[0, 0, :, tq:])
    bufs = ((s0a_ref, s0b_ref), (s1a_ref, s1b_ref))

    for g in range(2):
        s = _dot(kmeta_ref[0, 0], qts[g])
        m = jnp.max(s, axis=0, keepdims=True)
        p = jnp.exp2(s - m)
        m_ref[g] = m
        l_ref[g] = jnp.sum(p, axis=0, keepdims=True)
        acc_ref[g] = _dot(vtmeta_ref[0, 0, 0], p.astype(BF16))

    def scores_into(g, ab, j):
        start = pl.multiple_of(j * tq, tq)
        s = _dot(k_ref[0, 0, pl.ds(start, tq), :], qts[g])
        bufs[g][ab][...] = s
        smax_ref[g, ab] = jnp.max(s, axis=0, keepdims=True)

    def absorb(g, ab, j, causal=False):
        s = bufs[g][ab][...]
        if causal:
            key = lax.broadcasted_iota(jnp.int32, (tq, tq), 0)
            query = lax.broadcasted_iota(jnp.int32, (tq, tq), 1)
            s = jnp.where(key <= query, s, NEG)
            smax = jnp.max(s, axis=0, keepdims=True)
        else:
            smax = smax_ref[g, ab]
        m = m_ref[g]
        m_new = jnp.maximum(m, smax)
        alpha = jnp.exp2(m - m_new)
        p = jnp.exp2(s - m_new)
        m_ref[g] = m_new
        l_ref[g] = alpha * l_ref[g] + jnp.sum(p, axis=0, keepdims=True)
        acc_ref[g] = alpha * acc_ref[g] + _dot(vt_ref[0, 0, j], p.astype(BF16))

    buf_a, buf_b = 0, 1
    for g in range(2):
        scores_into(g, buf_a, 0)

    def pair(t, carry):
        scores_into(0, buf_b, 2 * t + 1)
        absorb(0, buf_a, 2 * t)
        scores_into(1, buf_b, 2 * t + 1)
        scores_into(0, buf_a, 2 * t + 2)
        absorb(1, buf_a, 2 * t)
        absorb(0, buf_b, 2 * t + 1)
        scores_into(1, buf_a, 2 * t + 2)
        absorb(1, buf_b, 2 * t + 1)
        return carry

    lax.fori_loop(0, u, pair, 0)

    scores_into(1, buf_b, 2 * u + 1)
    absorb(1, buf_a, 2 * u)
    absorb(0, buf_a, 2 * u, causal=True)
    absorb(1, buf_b, 2 * u + 1, causal=True)

    for g in range(2):
        o_ref[0, g * tq:(g + 1) * tq, :] = (acc_ref[g] * (1.0 / l_ref[g])).T.astype(BF16)


def _mla(qt, km, vt, km_meta, vt_meta, tq):
    B, H, _, S = qt.shape
    assert S % (2 * tq) == 0, "a step handles an (even, odd) pair of query tiles"
    return pl.pallas_call(
        functools.partial(_mla_kernel, tq=tq),
        grid=(B, H, S // (2 * tq)),
        in_specs=[
            pl.BlockSpec((1, 1, MLA_QK_PAD, 2 * tq), lambda b, h, u: (b, h, 0, u)),
            pl.BlockSpec((1, 1, S, MLA_QK_PAD), lambda b, h, u: (b, h, 0, 0)),
            pl.BlockSpec((1, 1, S // tq, MLA_V_DIM, tq), lambda b, h, u: (b, h, 0, 0, 0)),
            pl.BlockSpec((1, 1, N_META, MLA_QK_PAD), lambda b, h, u: (0, h, 0, 0)),
            pl.BlockSpec((1, 1, 1, MLA_V_DIM, N_META), lambda b, h, u: (0, h, 0, 0, 0)),
        ],
        out_specs=pl.BlockSpec((1, 2 * tq, MLA_V_DIM), lambda b, h, u: (b, u, h)),
        out_shape=jax.ShapeDtypeStruct((B, S, MLA_OUT_COLS), BF16),
        scratch_shapes=[pltpu.VMEM((tq, tq), F32), pltpu.VMEM((tq, tq), F32),
                        pltpu.VMEM((tq, tq), F32), pltpu.VMEM((tq, tq), F32),
                        pltpu.VMEM((2, 2, 1, tq), F32),
                        pltpu.VMEM((2, 1, tq), F32), pltpu.VMEM((2, 1, tq), F32),
                        pltpu.VMEM((2, MLA_V_DIM, tq), F32)],
        compiler_params=_cparams(("parallel", "parallel", "arbitrary")),
        name="mla_attn",
    )(qt, km, vt, km_meta, vt_meta)


def _out_proj_kernel(x_ref, a_ref, b_ref, gin_ref, bin_ref, ga_ref, gb_ref, wo_ref, g1_ref, b1_ref,
                     wrt_ref, brt_ref, h1_ref, h1_slabs_ref, route_ref, counts_out_ref, count_ref):
    h = _layer_norm(x_ref[0], gin_ref[...], bin_ref[...])
    a = a_ref[0].astype(F32)
    b = b_ref[0].astype(F32)
    an = (a * _rms_scale(a) * ga_ref[...]).astype(BF16)
    bn = (b * _rms_scale(b) * gb_ref[...]).astype(BF16)
    mix = _dot(jnp.concatenate([an, bn], axis=-1), wo_ref[...])
    h1 = _layer_norm(ALPHA * h + mix, g1_ref[...], b1_ref[...])
    h1_ref[0] = h1
    _to_slabs(h1, h1_slabs_ref, h1.shape[0])

    x_hi = h1.astype(BF16)
    x_lo = (h1 - x_hi.astype(F32)).astype(BF16)
    hi_both = _dot(x_hi, wrt_ref[...])
    logits = (hi_both[:, :ROUTE_LANES] + hi_both[:, ROUTE_LANES:]
              + _dot(x_lo, wrt_ref[:, :ROUTE_LANES]) + brt_ref[...])
    lane = lax.broadcasted_iota(jnp.int32, logits.shape, 1).astype(F32)
    ninf = -jnp.inf
    no_lane = float(ROUTE_LANES)
    gl = jnp.where(lane < N_GROUPS, logits, ninf)
    gmax = jnp.max(gl, axis=-1, keepdims=True)
    g_idx = jnp.min(jnp.where(gl == gmax, lane, no_lane), axis=-1, keepdims=True)
    lo = EXPERT_LANE0 + g_idx * EXPERTS_PER_GROUP
    el = jnp.where((lane >= lo) & (lane < lo + EXPERTS_PER_GROUP), logits, ninf)
    m1 = jnp.max(el, axis=-1, keepdims=True)
    i1 = jnp.min(jnp.where(el == m1, lane, no_lane), axis=-1, keepdims=True)
    el2 = jnp.where(lane == i1, ninf, el)
    m2 = jnp.max(el2, axis=-1, keepdims=True)
    i2 = jnp.min(jnp.where(el2 == m2, lane, no_lane), axis=-1, keepdims=True)

    a = jnp.minimum(i1, i2) - lo
    b = jnp.maximum(i1, i2) - lo
    cls = (g_idx * PAIRS_PER_GROUP + a * (2.0 * EXPERTS_PER_GROUP - 1.0 - a) * 0.5
           + (b - a - 1.0))

    @pl.when((pl.program_id(0) == 0) & (pl.program_id(1) == 0))
    def _():
        count_ref[...] = jnp.zeros_like(count_ref)

    tm = logits.shape[0]
    onehot = jnp.where(lane == cls, 1.0, 0.0)
    r_i = lax.broadcasted_iota(jnp.int32, (BLOCK, BLOCK), 0)
    c_i = lax.broadcasted_iota(jnp.int32, (BLOCK, BLOCK), 1)
    earlier = jnp.where(r_i > c_i, 1.0, 0.0).astype(BF16)
    running = count_ref[...]
    before = []
    for blk in range(tm // BLOCK):
        oh = onehot[blk * BLOCK:(blk + 1) * BLOCK]
        before.append(_dot(earlier, oh.astype(BF16)) + running)
        running = running + jnp.sum(oh, axis=0, keepdims=True)
    before = jnp.concatenate(before, axis=0)
    rank = jnp.sum(jnp.where(lane == cls, before, 0.0), axis=-1, keepdims=True)
    count_ref[...] = running
    counts_out_ref[...] = running

    route = jnp.where(lane == ROUTE_CLS, cls, 0.0)
    route = jnp.where(lane == ROUTE_RANK, rank, route)
    route_ref[0, 0] = route.T[:ROUTE_ROWS].astype(jnp.int32)


def _out_proj(x, a_out, b_out, p, tm):
    B, S, D = x.shape
    full = lambda shape: pl.BlockSpec(shape, lambda b, i: (0,) * len(shape))
    row = lambda w: pl.BlockSpec((1, tm, w), lambda b, i: (b, i, 0))
    return pl.pallas_call(
        _out_proj_kernel,
        grid=(B, S // tm),
        in_specs=[row(D), row(SWA_Q_COLS), row(MLA_OUT_COLS),
                  full((1, D)), full((1, D)), full((1, SWA_Q_COLS)), full((1, MLA_OUT_COLS)),
                  full((D, D)), full((1, D)), full((1, D)),
                  full((D, 2 * ROUTE_LANES)), full((1, ROUTE_LANES))],
        out_specs=(row(D),
                   pl.BlockSpec((tm * SLAB, LANES), lambda b, i: (b * (S // tm) + i, 0)),
                   pl.BlockSpec((1, 1, ROUTE_ROWS, tm), lambda b, i: (b, i, 0, 0)),
                   full((1, ROUTE_LANES))),
        out_shape=(jax.ShapeDtypeStruct((B, S, D), F32),
                   jax.ShapeDtypeStruct((B * S * SLAB, LANES), F32),
                   jax.ShapeDtypeStruct((B, S // tm, ROUTE_ROWS, tm), jnp.int32),
                   jax.ShapeDtypeStruct((1, ROUTE_LANES), F32)),
        scratch_shapes=[pltpu.VMEM((1, ROUTE_LANES), F32)],
        compiler_params=_cparams(("arbitrary", "arbitrary")),
        name="out_proj_route",
    )(x, a_out, b_out, p["ln_in_g"], p["ln_in_b"], p["ga"], p["gb"], p["w_o"], p["ln1_g"],
      p["ln1_b"], p["w_route"], p["b_route"])


SLAB = D_MODEL // LANES
TAIL_CHUNKS = tuple(MOE_TILE_ROWS >> b for b in range(MOE_TILE_ROWS.bit_length()))


def _slab_copy(src_ref, src_tok8, dst_ref, dst_tok8, sem):
    return pltpu.make_async_copy(src_ref.at[pl.ds(pl.multiple_of(src_tok8, SLAB), SLAB)],
                                 dst_ref.at[pl.ds(pl.multiple_of(dst_tok8, SLAB), SLAB)], sem)


def _to_slabs(x, slab_ref, n):
    for s in range(SLAB):
        slab_ref[pl.ds(s, n, stride=SLAB), :] = x[:, s * LANES:(s + 1) * LANES]


def _from_slabs(slab_ref, n):
    return jnp.concatenate([slab_ref[pl.ds(s, n, stride=SLAB), :] for s in range(SLAB)], axis=1)


def _dispatch_kernel(padstart_ref, taillen_ref, nu_ref, pos_ref, h1_slabs_ref, xs_ref, zero_ref,
                     sem, zero_sem, *, tm, n_tiles, slots_per_step):
    t = pl.program_id(0)

    def zero_fill(start, rows):
        return pltpu.make_async_copy(
            zero_ref.at[pl.ds(0, rows * SLAB)],
            xs_ref.at[pl.ds(pl.multiple_of(start, SLAB), rows * SLAB)], zero_sem)

    def zero_fills(act):
        def one_slot(i, carry):
            c = t * slots_per_step + i

            @pl.when(c < N_CLASSES)
            def _():
                tail = taillen_ref[c]
                for rows in TAIL_CHUNKS:
                    @pl.when((tail & rows) != 0)
                    def _():
                        done = tail & ~(2 * rows - 1)
                        act(zero_fill(padstart_ref[c] + done * SLAB, rows))

            @pl.when(nu_ref[0] + c < n_tiles)
            def _():
                act(zero_fill((nu_ref[0] + c) * (MOE_TILE_ROWS * SLAB), MOE_TILE_ROWS))

            return carry

        lax.fori_loop(0, slots_per_step, one_slot, 0)

    @pl.when(t == 0)
    def _():
        zero_ref[...] = jnp.zeros_like(zero_ref)

    zero_fills(lambda fill: fill.start())

    def issue(r0, carry):
        for k in range(DMA_BURST):
            r = r0 * DMA_BURST + k
            _slab_copy(h1_slabs_ref, (t * tm + r) * SLAB, xs_ref, pos_ref[0, 0, r],
                       sem).start(priority=k % 2)
        return carry

    lax.fori_loop(0, tm // DMA_BURST, issue, 0)
    pltpu.make_async_copy(h1_slabs_ref.at[pl.ds(0, tm * SLAB)], xs_ref.at[pl.ds(0, tm * SLAB)],
                          sem).wait()
    zero_fills(lambda fill: fill.wait())


def _dispatch(h1_slabs, pos8, pad_start8, tail_len, n_used, n_tiles, tm):
    n_steps = h1_slabs.shape[0] // (tm * SLAB)
    pos8 = pos8.reshape(n_steps, 1, tm)
    grid_spec = pltpu.PrefetchScalarGridSpec(
        num_scalar_prefetch=3,
        grid=(n_steps,),
        in_specs=[pl.BlockSpec((1, 1, tm), lambda t, ps, tl, nu: (t, 0, 0),
                               memory_space=pltpu.SMEM),
                  pl.BlockSpec(memory_space=pl.ANY)],
        out_specs=pl.BlockSpec(memory_space=pl.ANY),
        scratch_shapes=[pltpu.VMEM((MOE_TILE_ROWS * SLAB, LANES), F32),
                        pltpu.SemaphoreType.DMA,
                        pltpu.SemaphoreType.DMA],
    )
    return pl.pallas_call(
        functools.partial(_dispatch_kernel, tm=tm, n_tiles=n_tiles,
                          slots_per_step=pl.cdiv(N_CLASSES, n_steps)),
        grid_spec=grid_spec,
        out_shape=jax.ShapeDtypeStruct((n_tiles * MOE_TILE_ROWS * SLAB, LANES), F32),
        compiler_params=_cparams(("arbitrary",)),
        name="moe_dispatch",
    )(pad_start8, tail_len, n_used, pos8, h1_slabs)


def _experts_kernel(tlo_ref, thi_ref, nu_ref, x_ref, wg_lo_ref, wg_hi_ref, wu_lo_ref, wu_hi_ref,
                    wd_lo_ref, wd_hi_ref, wrt_ref, brt_ref, y_ref):
    j = pl.program_id(0)
    tr = MOE_TILE_ROWS

    @pl.when(j < nu_ref[0])
    def _():
        x = _from_slabs(x_ref, tr).astype(BF16)
        lo_e = tlo_ref[j]
        hi_e = thi_ref[j]
        logits = _dot(x, wrt_ref[...]) + brt_ref[...]
        lane = lax.broadcasted_iota(jnp.int32, logits.shape, 1)
        gl = jnp.where(lane < N_GROUPS, logits, -jnp.inf)
        ge = jnp.exp(gl - jnp.max(gl, axis=-1, keepdims=True))
        pick = lambda v, at: jnp.sum(jnp.where(lane == at, v, 0.0), axis=-1, keepdims=True)
        g_prob = pick(ge, lo_e // EXPERTS_PER_GROUP) / jnp.sum(ge, axis=-1, keepdims=True)
        e_lo = pick(logits, EXPERT_LANE0 + lo_e)
        e_hi = pick(logits, EXPERT_LANE0 + hi_e)
        gates = (g_prob / (1.0 + jnp.exp(e_hi - e_lo)), g_prob / (1.0 + jnp.exp(e_lo - e_hi)))

        y = None
        for gate, wg_ref, wu_ref, wd_ref in zip(gates, (wg_lo_ref, wg_hi_ref), (wu_lo_ref, wu_hi_ref),
                                                (wd_lo_ref, wd_hi_ref)):
            g = _dot(x, wg_ref[0])
            u = _dot(x, wu_ref[0])
            hid = g * (1.0 / (1.0 + jnp.exp(-g))) * u * gate
            y_k = _dot(hid.astype(BF16), wd_ref[0])
            y = y_k if y is None else y + y_k
        _to_slabs(y, y_ref, tr)


def _experts(xs, tile_lo, tile_hi, n_used, p, n_tiles):
    D = D_MODEL
    tr = MOE_TILE_ROWS
    last_used = lambda j, nu: jnp.minimum(j, nu[0] - 1)
    wgu = lambda pick: pl.BlockSpec((1, D, D_EXPERT),
                                    lambda j, tlo, thi, nu: (pick(tlo, thi)[j], 0, 0))
    wd = lambda pick: pl.BlockSpec((1, D_EXPERT, D),
                                   lambda j, tlo, thi, nu: (pick(tlo, thi)[j], 0, 0))
    lo = lambda tlo, thi: tlo
    hi = lambda tlo, thi: thi
    grid_spec = pltpu.PrefetchScalarGridSpec(
        num_scalar_prefetch=3,
        grid=(n_tiles,),
        in_specs=[pl.BlockSpec((tr * SLAB, LANES), lambda j, tlo, thi, nu: (last_used(j, nu), 0)),
                  wgu(lo), wgu(hi), wgu(lo), wgu(hi), wd(lo), wd(hi),
                  pl.BlockSpec((D, ROUTE_LANES), lambda j, tlo, thi, nu: (0, 0)),
                  pl.BlockSpec((1, ROUTE_LANES), lambda j, tlo, thi, nu: (0, 0))],
        out_specs=pl.BlockSpec((tr * SLAB, LANES), lambda j, tlo, thi, nu: (last_used(j, nu), 0)),
    )
    return pl.pallas_call(
        _experts_kernel,
        grid_spec=grid_spec,
        out_shape=jax.ShapeDtypeStruct(xs.shape, F32),
        input_output_aliases={3: 0},
        compiler_params=_cparams(("arbitrary",)),
        name="moe_experts",
    )(tile_lo, tile_hi, n_used, xs, p["w_gate"], p["w_gate"], p["w_up"], p["w_up"],
      p["w_down"], p["w_down"],
      p["w_route"], p["b_route"])


def _combine_kernel(pos_ref, pos_next_ref, h1_ref, ys_ref, g2_ref, b2_ref, o_ref,
                    ybuf_ref, sem, *, tm):
    t = pl.program_id(0)
    slot = t % 2

    def gather(idx_ref, dst_slot):
        def issue(r0, carry):
            for k in range(DMA_BURST):
                r = r0 * DMA_BURST + k
                _slab_copy(ys_ref, idx_ref[0, 0, r], ybuf_ref.at[dst_slot], r * SLAB,
                           sem.at[dst_slot]).start(priority=k % 2)
            return carry

        lax.fori_loop(0, tm // DMA_BURST, issue, 0)

    @pl.when(t == 0)
    def _():
        gather(pos_ref, 0)

    @pl.when(t + 1 < pl.num_programs(0))
    def _():
        gather(pos_next_ref, 1 - slot)

    pltpu.make_async_copy(ys_ref.at[pl.ds(0, tm * SLAB)], ybuf_ref.at[slot], sem.at[slot]).wait()
    ffn = _from_slabs(ybuf_ref.at[slot], tm)
    o_ref[...] = _layer_norm(ALPHA * h1_ref[...] + ffn, g2_ref[...], b2_ref[...])


def _combine(h1, pos8, ys, p, tm):
    T, D = h1.shape
    n_steps = T // tm
    pos8 = pos8.reshape(n_steps, 1, tm)
    full = lambda shape: pl.BlockSpec(shape, lambda t: (0,) * len(shape))
    return pl.pallas_call(
        functools.partial(_combine_kernel, tm=tm),
        grid=(n_steps,),
        in_specs=[pl.BlockSpec((1, 1, tm), lambda t: (t, 0, 0), memory_space=pltpu.SMEM),
                  pl.BlockSpec((1, 1, tm), lambda t: (jnp.minimum(t + 1, n_steps - 1), 0, 0),
                               memory_space=pltpu.SMEM),
                  pl.BlockSpec((tm, D), lambda t: (t, 0)),
                  pl.BlockSpec(memory_space=pl.ANY),
                  full((1, D)), full((1, D))],
        out_specs=pl.BlockSpec((tm, D), lambda t: (t, 0)),
        out_shape=jax.ShapeDtypeStruct((T, D), F32),
        scratch_shapes=[pltpu.VMEM((2, tm * SLAB, LANES), F32),
                        pltpu.SemaphoreType.DMA((2,))],
        compiler_params=_cparams(("arbitrary",)),
        name="moe_combine_ln2",
    )(pos8, pos8, h1, ys, p["ln2_g"], p["ln2_b"])


def _pair_layout(n):
    order, prev_end = [], None
    for a in range(n - 1):
        end = n - 1 if (a >= n - 2 or (n - 3 - a) % 2 == 0) else n - 2
        middle = [b for b in range(a + 1, n) if b not in (prev_end, end)]
        head = [] if prev_end in (None, end) else [prev_end]
        order += [(a, b) for b in head + middle + [end]]
        prev_end = end
    return order


def _dispatch_plan(route, counts):
    T = route.shape[0] * route.shape[1] * route.shape[3]
    tr = MOE_TILE_ROWS
    n_tiles = T // tr + N_CLASSES
    cnt = counts[0, :N_CLASSES].astype(jnp.int32)
    padded = jnp.maximum((cnt + tr - 1) // tr, 1) * tr
    classes = jnp.arange(N_CLASSES, dtype=jnp.int32)
    pairs = [(a, b) for a in range(EXPERTS_PER_GROUP) for b in range(a + 1, EXPERTS_PER_GROUP)]
    layout = _pair_layout(EXPERTS_PER_GROUP)
    place = jnp.asarray([g * PAIRS_PER_GROUP + layout.index(pair) for g in range(N_GROUPS)
                         for pair in pairs], jnp.int32)
    seg_end = jnp.sum(jnp.where(place[None, :] <= place[:, None], padded[None, :], 0), axis=1)
    seg_start = seg_end - padded
    cls = route[:, :, ROUTE_CLS, :].reshape(T)
    rank = route[:, :, ROUTE_RANK, :].reshape(T)
    start_of = jnp.sum(jnp.where(cls[:, None] == classes, seg_start, 0), axis=-1)
    pos8 = (start_of + rank) * SLAB
    tile_row0 = jnp.arange(n_tiles, dtype=jnp.int32) * tr
    tile_place = jnp.sum((seg_end[None, :] <= tile_row0[:, None]).astype(jnp.int32), axis=-1)
    tile_place = jnp.minimum(tile_place, N_CLASSES - 1)
    lo_of = jnp.asarray([g * EXPERTS_PER_GROUP + a for g in range(N_GROUPS) for a, _ in pairs],
                        jnp.int32)
    hi_of = jnp.asarray([g * EXPERTS_PER_GROUP + b for g in range(N_GROUPS) for _, b in pairs],
                        jnp.int32)
    is_cls = tile_place[:, None] == place
    tile_lo = jnp.sum(jnp.where(is_cls, lo_of, 0), axis=-1)
    tile_hi = jnp.sum(jnp.where(is_cls, hi_of, 0), axis=-1)
    n_used = (jnp.sum(padded, keepdims=True) // tr).astype(jnp.int32)
    return pos8, (seg_start + cnt) * SLAB, padded - cnt, tile_lo, tile_hi, n_used, n_tiles


def _rope_tables(pos):
    d = SWA_HEAD_DIM
    inv_freq = ROPE_THETA ** (-jnp.arange(0, d, 2, dtype=F32) / d)
    ang = pos[:, None] * inv_freq[None, :]
    c, s = jnp.cos(ang), jnp.sin(ang)
    return (jnp.concatenate([c, c, c, c], axis=-1), jnp.concatenate([-s, s, -s, s], axis=-1),
            jnp.concatenate([c, c], axis=-1).T, jnp.concatenate([s, s], axis=-1).T)


def _prep_params(ln_in_g, ln_in_b, w_in, swa_sinks, mla_q_norm_g, mla_w_uq, mla_kv_norm_g, mla_w_ukv,
                 swa_out_norm_g, mla_out_norm_g, w_o, ln1_g, ln1_b, moe_w_group, moe_b_group,
                 moe_w_router, moe_b_router, moe_w_gate, moe_w_up, moe_w_down, ln2_g, ln2_b):
    hd = SWA_HEAD_DIM

    def by_head(t, axis):
        return [lax.slice_in_dim(t, h * hd, (h + 1) * hd, axis=axis) for h in SWA_HEAD_ORDER]

    w = w_in[0]
    w_pad = jnp.zeros((D_MODEL, IN_COLS_PADDED - w.shape[1]), w.dtype)
    w_in_p = jnp.concatenate(by_head(w, 1) + [w[:, SWA_Q_COLS:], w_pad], axis=1).astype(BF16)

    dqk = MLA_NOPE_DIM + MLA_ROPE_DIM
    wuq = mla_w_uq[0].reshape(MLA_Q_RANK, MLA_HEADS, dqk)
    wuq = jnp.pad(wuq, ((0, 0), (0, 0), (0, MLA_QK_PAD - dqk)))
    wuq_t = jnp.transpose(wuq, (1, 2, 0)).astype(BF16)
    wukv = mla_w_ukv[0].reshape(MLA_KV_RANK, MLA_HEADS, MLA_NOPE_DIM + MLA_V_DIM)
    wuk = wukv[:, :, :MLA_NOPE_DIM].reshape(MLA_KV_RANK, MLA_HEADS * MLA_NOPE_DIM).astype(BF16)
    wuv_t = jnp.transpose(wukv[:, :, MLA_NOPE_DIM:], (1, 2, 0)).astype(BF16)

    w_o0 = w_o[0]
    w_o_p = jnp.concatenate(by_head(w_o0, 0) + [w_o0[SWA_Q_COLS:]], axis=0).astype(BF16)

    def route_lanes(group_part, expert_part):
        rows = group_part.shape[0]
        gap = jnp.zeros((rows, EXPERT_LANE0 - N_GROUPS), F32)
        tail = jnp.zeros((rows, ROUTE_LANES - EXPERT_LANE0 - N_EXPERTS), F32)
        return jnp.concatenate([group_part, gap, expert_part, tail], axis=1)

    w_route = route_lanes(moe_w_group[0], moe_w_router[0])
    w_route_hi = w_route.astype(BF16)
    w_route_lo = (w_route - w_route_hi.astype(F32)).astype(BF16)
    w_route = jnp.concatenate([w_route_hi, w_route_lo], axis=1)
    b_route = route_lanes(moe_b_group, moe_b_router)

    return {
        "ln_in_g": ln_in_g[None, :], "ln_in_b": ln_in_b[None, :],
        "w_in": w_in_p,
        "sinks": swa_sinks[0],
        "gq": mla_q_norm_g, "w_uq_t": wuq_t,
        "gkv": mla_kv_norm_g, "w_uk": wuk, "w_uv_t": wuv_t,
        "ga": jnp.concatenate(by_head(swa_out_norm_g, 1), axis=1), "gb": mla_out_norm_g,
        "w_o": w_o_p, "ln1_g": ln1_g, "ln1_b": ln1_b,
        "w_route": w_route, "b_route": b_route,
        "w_gate": moe_w_gate[0].astype(BF16), "w_up": moe_w_up[0].astype(BF16),
        "w_down": moe_w_down[0].astype(BF16),
        "ln2_g": ln2_g, "ln2_b": ln2_b,
    }


def kernel(x, meta_tokens, ln_in_g, ln_in_b, w_in, swa_sinks, mla_q_norm_g, mla_w_uq, mla_kv_norm_g, mla_w_ukv, swa_out_norm_g, mla_out_norm_g, w_o, ln1_g, ln1_b, moe_w_group, moe_b_group, moe_w_router, moe_b_router, moe_w_gate, moe_w_up, moe_w_down, ln2_g, ln2_b):
    B, S, D = x.shape
    p = _prep_params(ln_in_g, ln_in_b, w_in, swa_sinks, mla_q_norm_g, mla_w_uq, mla_kv_norm_g,
                     mla_w_ukv, swa_out_norm_g, mla_out_norm_g, w_o, ln1_g, ln1_b, moe_w_group,
                     moe_b_group, moe_w_router, moe_b_router, moe_w_gate, moe_w_up, moe_w_down,
                     ln2_g, ln2_b)
    T = B * S
    tm = min(PROJ_TILE, S)

    x_meta = jnp.concatenate([jnp.zeros((N_PAD, D), x.dtype), meta_tokens.astype(x.dtype)])[None]
    pos_meta = jnp.maximum(jnp.arange(BLOCK) - N_PAD, 0).astype(F32)
    pos_tok = (jnp.arange(S) + N_META).astype(F32)
    _, ka_m, va_m, _, km_m, vt_m = _in_proj(x_meta, _rope_tables(pos_meta), p, BLOCK)
    a_out, qt, km, vt = _in_proj_swa(x, _rope_tables(pos_tok), p, ka_m, va_m, tm)
    b_out = _mla(qt, km, vt, km_m[:, :, N_PAD:, :], vt_m[..., N_PAD:], tm)
    h1, h1_slabs, route, counts = _out_proj(x, a_out, b_out, p, min(WIDE_TILE, S))
    h1 = h1.reshape(B * S, D)

    pos8, pad_start8, tail_len, tile_lo, tile_hi, n_used, n_tiles = _dispatch_plan(route, counts)
    xs = _dispatch(h1_slabs, pos8, pad_start8, tail_len, n_used, n_tiles, min(DISPATCH_TILE, T))
    ys = _experts(xs, tile_lo, tile_hi, n_used, p, n_tiles)
    out = _combine(h1, pos8, ys, p, min(COMBINE_TILE, T))
    return out.reshape(B, S, D)
```

```python
import functools

import jax
import jax.numpy as jnp
from jax import lax
from jax.experimental import pallas as pl
from jax.experimental.pallas import tpu as pltpu

D_MODEL = 1024
N_META = 16
BLOCK = 128
N_PAD = BLOCK - N_META
ROPE_THETA = 10000.0
SWA_HEADS = 8
SWA_KV_HEADS = 2
SWA_HEAD_DIM = 64
MLA_HEADS = 4
MLA_Q_RANK = 256
MLA_KV_RANK = 256
MLA_NOPE_DIM = 128
MLA_ROPE_DIM = 64
MLA_V_DIM = 128
SWA_Q_COLS = SWA_HEADS * SWA_HEAD_DIM
SWA_KV_COLS = SWA_KV_HEADS * SWA_HEAD_DIM
MLA_OUT_COLS = MLA_HEADS * MLA_V_DIM
N_GROUPS = 4
EXPERTS_PER_GROUP = 8
N_EXPERTS = N_GROUPS * EXPERTS_PER_GROUP
D_EXPERT = 256
LN_EPS = 1e-5
RMS_EPS = 1e-6
DEPTH = 1
ALPHA = (2.0 * DEPTH) ** 0.25
NEG = -1e30
LOG2_E = 1.4426950408889634

LANES = 128
COL_KA = SWA_Q_COLS
COL_VA = COL_KA + SWA_KV_COLS
COL_CQ = COL_VA + SWA_KV_COLS
COL_CKV = COL_CQ + MLA_Q_RANK
COL_KR = COL_CKV + MLA_KV_RANK
IN_COLS_PADDED = COL_KR + LANES
MLA_QK_PAD = 256
ROUTE_LANES = 128
EXPERT_LANE0 = 32
ROUTE_CLS, ROUTE_RANK = range(2)
ROUTE_ROWS = 8
PAIRS_PER_GROUP = EXPERTS_PER_GROUP * (EXPERTS_PER_GROUP - 1) // 2
N_CLASSES = N_GROUPS * PAIRS_PER_GROUP
MOE_TILE_ROWS = 256
PROJ_TILE = 1024
WIDE_TILE = 1024
DISPATCH_TILE = 2048
COMBINE_TILE = 1024
DMA_BURST = 8
VMEM_LIMIT = 56 * 1024 * 1024

SWA_HEAD_ORDER = (0, 4, 1, 5, 2, 6, 3, 7)

BF16 = jnp.bfloat16
F32 = jnp.float32


def _cparams(sem):
    return pltpu.CompilerParams(dimension_semantics=sem, vmem_limit_bytes=VMEM_LIMIT)


def _layer_norm(x, g, b):
    mu = jnp.mean(x, axis=-1, keepdims=True)
    xc = x - mu
    var = jnp.mean(xc * xc, axis=-1, keepdims=True)
    return xc * lax.rsqrt(var + LN_EPS) * g + b


def _rms_scale(x):
    return lax.rsqrt(jnp.mean(x * x, axis=-1, keepdims=True) + RMS_EPS)


def _rope128(x, cos, sin_signed, lane_lo):
    up = pltpu.roll(x, LANES - 32, 1)
    dn = pltpu.roll(x, 32, 1)
    return x * cos + jnp.where(lane_lo, up, dn) * sin_signed


def _dot(a, b):
    return jnp.dot(a, b, preferred_element_type=F32)


def _dot_nt(a, b):
    return lax.dot_general(a, b, (((1,), (1,)), ((), ())), preferred_element_type=F32)


def _in_proj_kernel(x_ref, g_ref, b_ref, w_ref, cos_ref, sin_ref, cost_ref, sint_ref, gq_ref,
                    wuqt_ref, gkv_ref, wuk_ref, wuvt_ref,
                    qa_ref, ka_ref, va_ref, qt_ref, km_ref, vt_ref):
    h = _layer_norm(x_ref[0], g_ref[...], b_ref[...])
    u = _dot(h.astype(BF16), w_ref[...])
    cos = cos_ref[...]
    sin = sin_ref[...]
    lane = lax.broadcasted_iota(jnp.int32, (1, LANES), 1)
    lane_lo = (lane % 64) < 32
    rope = functools.partial(_rope128, cos=cos, sin_signed=sin, lane_lo=lane_lo)

    swa_scale = SWA_HEAD_DIM ** -0.5
    for c in range(SWA_Q_COLS // LANES):
        qc = rope(u[:, c * LANES:(c + 1) * LANES]) * swa_scale
        qa_ref[0, :, c * LANES:(c + 1) * LANES] = qc.astype(BF16)
    ka_ref[0] = rope(u[:, COL_KA:COL_VA]).astype(BF16)
    va_ref[0] = u[:, COL_VA:COL_CQ].astype(BF16)

    cq = u[:, COL_CQ:COL_CKV]
    cqn = (cq * _rms_scale(cq) * gq_ref[...]).astype(BF16)
    ckv = u[:, COL_CKV:COL_KR]
    ckvn = (ckv * _rms_scale(ckv) * gkv_ref[...]).astype(BF16)
    k_nope = _dot(ckvn, wuk_ref[...])
    kr = rope(u[:, COL_KR:IN_COLS_PADDED]).astype(BF16)
    q_scale = (MLA_NOPE_DIM + MLA_ROPE_DIM) ** -0.5 * LOG2_E
    cost = cost_ref[...]
    sint = sint_ref[...]
    half = MLA_ROPE_DIM // 2
    for hd in range(MLA_HEADS):
        qt = _dot_nt(wuqt_ref[hd], cqn)
        qr = qt[MLA_NOPE_DIM:MLA_NOPE_DIM + MLA_ROPE_DIM]
        rot = jnp.concatenate([-qr[half:], qr[:half]], axis=0)
        qt_ref[0, hd, 0:MLA_NOPE_DIM, :] = (qt[:MLA_NOPE_DIM] * q_scale).astype(BF16)
        qt_ref[0, hd, MLA_NOPE_DIM:MLA_NOPE_DIM + MLA_ROPE_DIM, :] = (
            (qr * cost + rot * sint) * q_scale).astype(BF16)
        qt_ref[0, hd, MLA_NOPE_DIM + MLA_ROPE_DIM:, :] = (
            qt[MLA_NOPE_DIM + MLA_ROPE_DIM:]).astype(BF16)
        km_ref[0, hd, :, 0:LANES] = k_nope[:, hd * LANES:(hd + 1) * LANES].astype(BF16)
        km_ref[0, hd, :, LANES:2 * LANES] = kr
        vt_ref[0, hd, 0] = _dot_nt(wuvt_ref[hd], ckvn).astype(BF16)


def _in_proj(x, tables, p, tm):
    cos, sin, cos_t, sin_t = tables
    B, S, D = x.shape
    grid = (B, S // tm)
    full = lambda shape: pl.BlockSpec(shape, lambda b, i: (0,) * len(shape))
    out_shape = (
        jax.ShapeDtypeStruct((B, S, SWA_Q_COLS), BF16),
        jax.ShapeDtypeStruct((B, S, SWA_KV_COLS), BF16),
        jax.ShapeDtypeStruct((B, S, SWA_KV_COLS), BF16),
        jax.ShapeDtypeStruct((B, MLA_HEADS, MLA_QK_PAD, S), BF16),
        jax.ShapeDtypeStruct((B, MLA_HEADS, S, MLA_QK_PAD), BF16),
        jax.ShapeDtypeStruct((B, MLA_HEADS, S // tm, MLA_V_DIM, tm), BF16),
    )
    row = lambda w: pl.BlockSpec((1, tm, w), lambda b, i: (b, i, 0))
    head = lambda w: pl.BlockSpec((1, MLA_HEADS, tm, w), lambda b, i: (b, 0, i, 0))
    return pl.pallas_call(
        _in_proj_kernel,
        grid=grid,
        in_specs=[
            row(D),
            full((1, D)), full((1, D)),
            full((D, IN_COLS_PADDED)),
            pl.BlockSpec((tm, LANES), lambda b, i: (i, 0)),
            pl.BlockSpec((tm, LANES), lambda b, i: (i, 0)),
            pl.BlockSpec((MLA_ROPE_DIM, tm), lambda b, i: (0, i)),
            pl.BlockSpec((MLA_ROPE_DIM, tm), lambda b, i: (0, i)),
            full((1, MLA_Q_RANK)), full((MLA_HEADS, MLA_QK_PAD, MLA_Q_RANK)),
            full((1, MLA_KV_RANK)), full((MLA_KV_RANK, MLA_HEADS * MLA_NOPE_DIM)),
            full((MLA_HEADS, MLA_V_DIM, MLA_KV_RANK)),
        ],
        out_specs=(row(SWA_Q_COLS), row(SWA_KV_COLS), row(SWA_KV_COLS),
                   pl.BlockSpec((1, MLA_HEADS, MLA_QK_PAD, tm), lambda b, i: (b, 0, 0, i)),
                   head(MLA_QK_PAD),
                   pl.BlockSpec((1, MLA_HEADS, 1, MLA_V_DIM, tm), lambda b, i: (b, 0, i, 0, 0))),
        out_shape=out_shape,
        compiler_params=_cparams(("parallel", "parallel")),
        name="in_proj",
    )(x, p["ln_in_g"], p["ln_in_b"], p["w_in"], cos, sin, cos_t, sin_t, p["gq"], p["w_uq_t"],
      p["gkv"], p["w_uk"], p["w_uv_t"])


def _swa_blocks(sink_ref, qa_ref, kc_ref, vc_ref, kprev, vprev, first, o_ref, blocks_per_step):
    row = lax.broadcasted_iota(jnp.int32, (2 * BLOCK, BLOCK), 0) % BLOCK
    col = lax.broadcasted_iota(jnp.int32, (2 * BLOCK, BLOCK), 1)
    own = col <= row
    meta_ok = col >= jnp.where(first, N_PAD, 0)
    lane = lax.broadcasted_iota(jnp.int32, (1, LANES), 1)
    lane_kv0 = lane < SWA_HEAD_DIM
    top_rows = lax.broadcasted_iota(jnp.int32, (2 * BLOCK, 1), 0) < BLOCK

    for r in range(blocks_per_step):
        rows = slice(r * BLOCK, (r + 1) * BLOCK)
        if r == 0:
            kp, vp = kprev, vprev
        else:
            prev_rows = slice((r - 1) * BLOCK, r * BLOCK)
            kp, vp = kc_ref[0, prev_rows, :], vc_ref[0, prev_rows, :]
        kc, vc = kc_ref[0, rows, :], vc_ref[0, rows, :]
        for c in range(SWA_Q_COLS // LANES):
            qc = qa_ref[0, rows, c * LANES:(c + 1) * LANES]
            zero = jnp.zeros_like(qc)
            q2 = jnp.concatenate([jnp.where(lane_kv0, qc, zero),
                                  jnp.where(lane_kv0, zero, qc)], axis=0)
            s_prev = _dot_nt(q2, kp)
            if r == 0:
                s_prev = jnp.where(meta_ok, s_prev, NEG)
            s = jnp.where(own, _dot_nt(q2, kc), s_prev)
            sink = jnp.where(top_rows, sink_ref[c], sink_ref[c + 4])
            m = jnp.maximum(jnp.max(s, axis=-1, keepdims=True), sink)
            e = jnp.exp(s - m)
            denom = jnp.sum(e, axis=-1, keepdims=True) + jnp.exp(sink - m)
            pr = e * (1.0 / denom)
            o2 = (_dot(jnp.where(own, pr, 0.0).astype(BF16), vc)
                  + _dot(jnp.where(own, 0.0, pr).astype(BF16), vp))
            o = jnp.where(lane_kv0, o2[:BLOCK], o2[BLOCK:])
            o_ref[0, rows, c * LANES:(c + 1) * LANES] = o.astype(BF16)


def _in_proj_swa_kernel(sink_ref, x_ref, g_ref, b_ref, w_ref, cos_ref, sin_ref, cost_ref, sint_ref,
                        gq_ref, wuqt_ref, gkv_ref, wuk_ref, wuvt_ref, kmeta_ref, vmeta_ref,
                        a_ref, qt_ref, km_ref, vt_ref,
                        qa_s, ka_s, va_s, kprev_s, vprev_s):
    _in_proj_kernel(x_ref, g_ref, b_ref, w_ref, cos_ref, sin_ref, cost_ref, sint_ref, gq_ref,
                    wuqt_ref, gkv_ref, wuk_ref, wuvt_ref, qa_s, ka_s, va_s, qt_ref, km_ref, vt_ref)
    first = pl.program_id(1) == 0
    kprev = jnp.where(first, kmeta_ref[0], kprev_s[...])
    vprev = jnp.where(first, vmeta_ref[0], vprev_s[...])
    tm = qa_s.shape[1]
    _swa_blocks(sink_ref, qa_s, ka_s, va_s, kprev, vprev, first, a_ref, tm // BLOCK)
    kprev_s[...] = ka_s[0, tm - BLOCK:, :]
    vprev_s[...] = va_s[0, tm - BLOCK:, :]


def _in_proj_swa(x, tables, p, ka_meta, va_meta, tm):
    cos, sin, cos_t, sin_t = tables
    B, S, D = x.shape
    kvw = SWA_KV_COLS
    full = lambda shape: pl.BlockSpec(shape, lambda b, i: (0,) * len(shape))
    row = lambda w: pl.BlockSpec((1, tm, w), lambda b, i: (b, i, 0))
    out_shape = (
        jax.ShapeDtypeStruct((B, S, SWA_Q_COLS), BF16),
        jax.ShapeDtypeStruct((B, MLA_HEADS, MLA_QK_PAD, S), BF16),
        jax.ShapeDtypeStruct((B, MLA_HEADS, S, MLA_QK_PAD), BF16),
        jax.ShapeDtypeStruct((B, MLA_HEADS, S // tm, MLA_V_DIM, tm), BF16),
    )
    return pl.pallas_call(
        _in_proj_swa_kernel,
        grid=(B, S // tm),
        in_specs=[
            pl.BlockSpec(memory_space=pltpu.SMEM),
            row(D),
            full((1, D)), full((1, D)),
            full((D, IN_COLS_PADDED)),
            pl.BlockSpec((tm, LANES), lambda b, i: (i, 0)),
            pl.BlockSpec((tm, LANES), lambda b, i: (i, 0)),
            pl.BlockSpec((MLA_ROPE_DIM, tm), lambda b, i: (0, i)),
            pl.BlockSpec((MLA_ROPE_DIM, tm), lambda b, i: (0, i)),
            full((1, MLA_Q_RANK)), full((MLA_HEADS, MLA_QK_PAD, MLA_Q_RANK)),
            full((1, MLA_KV_RANK)), full((MLA_KV_RANK, MLA_HEADS * MLA_NOPE_DIM)),
            full((MLA_HEADS, MLA_V_DIM, MLA_KV_RANK)),
            full((1, BLOCK, kvw)), full((1, BLOCK, kvw)),
        ],
        out_specs=(row(SWA_Q_COLS),
                   pl.BlockSpec((1, MLA_HEADS, MLA_QK_PAD, tm), lambda b, i: (b, 0, 0, i)),
                   pl.BlockSpec((1, MLA_HEADS, tm, MLA_QK_PAD), lambda b, i: (b, 0, i, 0)),
                   pl.BlockSpec((1, MLA_HEADS, 1, MLA_V_DIM, tm), lambda b, i: (b, 0, i, 0, 0))),
        out_shape=out_shape,
        scratch_shapes=[pltpu.VMEM((1, tm, SWA_Q_COLS), BF16), pltpu.VMEM((1, tm, kvw), BF16),
                        pltpu.VMEM((1, tm, kvw), BF16), pltpu.VMEM((BLOCK, kvw), BF16),
                        pltpu.VMEM((BLOCK, kvw), BF16)],
        compiler_params=_cparams(("parallel", "arbitrary")),
        name="in_proj_swa",
    )(p["sinks"], x, p["ln_in_g"], p["ln_in_b"], p["w_in"], cos, sin, cos_t, sin_t, p["gq"],
      p["w_uq_t"], p["gkv"], p["w_uk"], p["w_uv_t"], ka_meta, va_meta)


def _mla_kernel(qt_ref, k_ref, vt_ref, kmeta_ref, vtmeta_ref, o_ref,
                s0a_ref, s0b_ref, s1a_ref, s1b_ref, smax_ref, m_ref, l_ref, acc_ref, *, tq):
    u = pl.program_id(2)
    qts = (qt_ref[0, 0, :, :tq], qt_ref[0, 0, :, tq:])
    bufs = ((s0a_ref, s0b_ref), (s1a_ref, s1b_ref))

    for g in range(2):
        s = _dot(kmeta_ref[0, 0], qts[g])
        m = jnp.max(s, axis=0, keepdims=True)
        p = jnp.exp2(s - m)
        m_ref[g] = m
        l_ref[g] = jnp.sum(p, axis=0, keepdims=True)
        acc_ref[g] = _dot(vtmeta_ref[0, 0, 0], p.astype(BF16))

    def scores_into(g, ab, j):
        start = pl.multiple_of(j * tq, tq)
        s = _dot(k_ref[0, 0, pl.ds(start, tq), :], qts[g])
        bufs[g][ab][...] = s
        smax_ref[g, ab] = jnp.max(s, axis=0, keepdims=True)

    def absorb(g, ab, j, causal=False):
        s = bufs[g][ab][...]
        if causal:
            key = lax.broadcasted_iota(jnp.int32, (tq, tq), 0)
            query = lax.broadcasted_iota(jnp.int32, (tq, tq), 1)
            s = jnp.where(key <= query, s, NEG)
            smax = jnp.max(s, axis=0, keepdims=True)
        else:
            smax = smax_ref[g, ab]
        m = m_ref[g]
        m_new = jnp.maximum(m, smax)
        alpha = jnp.exp2(m - m_new)
        p = jnp.exp2(s - m_new)
        m_ref[g] = m_new
        l_ref[g] = alpha * l_ref[g] + jnp.sum(p, axis=0, keepdims=True)
        acc_ref[g] = alpha * acc_ref[g] + _dot(vt_ref[0, 0, j], p.astype(BF16))

    buf_a, buf_b = 0, 1
    for g in range(2):
        scores_into(g, buf_a, 0)

    def pair(t, carry):
        scores_into(0, buf_b, 2 * t + 1)
        absorb(0, buf_a, 2 * t)
        scores_into(1, buf_b, 2 * t + 1)
        scores_into(0, buf_a, 2 * t + 2)
        absorb(1, buf_a, 2 * t)
        absorb(0, buf_b, 2 * t + 1)
        scores_into(1, buf_a, 2 * t + 2)
        absorb(1, buf_b, 2 * t + 1)
        return carry

    lax.fori_loop(0, u, pair, 0)

    scores_into(1, buf_b, 2 * u + 1)
    absorb(1, buf_a, 2 * u)
    absorb(0, buf_a, 2 * u, causal=True)
    absorb(1, buf_b, 2 * u + 1, causal=True)

    for g in range(2):
        o_ref[0, g * tq:(g + 1) * tq, :] = (acc_ref[g] * (1.0 / l_ref[g])).T.astype(BF16)


def _mla(qt, km, vt, km_meta, vt_meta, tq):
    B, H, _, S = qt.shape
    assert S % (2 * tq) == 0, "a step handles an (even, odd) pair of query tiles"
    return pl.pallas_call(
        functools.partial(_mla_kernel, tq=tq),
        grid=(B, H, S // (2 * tq)),
        in_specs=[
            pl.BlockSpec((1, 1, MLA_QK_PAD, 2 * tq), lambda b, h, u: (b, h, 0, u)),
            pl.BlockSpec((1, 1, S, MLA_QK_PAD), lambda b, h, u: (b, h, 0, 0)),
            pl.BlockSpec((1, 1, S // tq, MLA_V_DIM, tq), lambda b, h, u: (b, h, 0, 0, 0)),
            pl.BlockSpec((1, 1, N_META, MLA_QK_PAD), lambda b, h, u: (0, h, 0, 0)),
            pl.BlockSpec((1, 1, 1, MLA_V_DIM, N_META), lambda b, h, u: (0, h, 0, 0, 0)),
        ],
        out_specs=pl.BlockSpec((1, 2 * tq, MLA_V_DIM), lambda b, h, u: (b, u, h)),
        out_shape=jax.ShapeDtypeStruct((B, S, MLA_OUT_COLS), BF16),
        scratch_shapes=[pltpu.VMEM((tq, tq), F32), pltpu.VMEM((tq, tq), F32),
                        pltpu.VMEM((tq, tq), F32), pltpu.VMEM((tq, tq), F32),
                        pltpu.VMEM((2, 2, 1, tq), F32),
                        pltpu.VMEM((2, 1, tq), F32), pltpu.VMEM((2, 1, tq), F32),
                        pltpu.VMEM((2, MLA_V_DIM, tq), F32)],
        compiler_params=_cparams(("parallel", "parallel", "arbitrary")),
        name="mla_attn",
    )(qt, km, vt, km_meta, vt_meta)


def _out_proj_kernel(x_ref, a_ref, b_ref, gin_ref, bin_ref, ga_ref, gb_ref, wo_ref, g1_ref, b1_ref,
                     wrt_ref, brt_ref, h1_ref, route_ref, counts_out_ref, count_ref):
    h = _layer_norm(x_ref[0], gin_ref[...], bin_ref[...])
    a = a_ref[0].astype(F32)
    b = b_ref[0].astype(F32)
    an = (a * _rms_scale(a) * ga_ref[...]).astype(BF16)
    bn = (b * _rms_scale(b) * gb_ref[...]).astype(BF16)
    mix = _dot(jnp.concatenate([an, bn], axis=-1), wo_ref[...])
    h1 = _layer_norm(ALPHA * h + mix, g1_ref[...], b1_ref[...])
    h1_ref[0] = h1

    x_hi = h1.astype(BF16)
    x_lo = (h1 - x_hi.astype(F32)).astype(BF16)
    hi_both = _dot(x_hi, wrt_ref[...])
    logits = (hi_both[:, :ROUTE_LANES] + hi_both[:, ROUTE_LANES:]
              + _dot(x_lo, wrt_ref[:, :ROUTE_LANES]) + brt_ref[...])
    lane = lax.broadcasted_iota(jnp.int32, logits.shape, 1).astype(F32)
    ninf = -jnp.inf
    no_lane = float(ROUTE_LANES)
    gl = jnp.where(lane < N_GROUPS, logits, ninf)
    gmax = jnp.max(gl, axis=-1, keepdims=True)
    g_idx = jnp.min(jnp.where(gl == gmax, lane, no_lane), axis=-1, keepdims=True)
    lo = EXPERT_LANE0 + g_idx * EXPERTS_PER_GROUP
    el = jnp.where((lane >= lo) & (lane < lo + EXPERTS_PER_GROUP), logits, ninf)
    m1 = jnp.max(el, axis=-1, keepdims=True)
    i1 = jnp.min(jnp.where(el == m1, lane, no_lane), axis=-1, keepdims=True)
    el2 = jnp.where(lane == i1, ninf, el)
    m2 = jnp.max(el2, axis=-1, keepdims=True)
    i2 = jnp.min(jnp.where(el2 == m2, lane, no_lane), axis=-1, keepdims=True)

    a = jnp.minimum(i1, i2) - lo
    b = jnp.maximum(i1, i2) - lo
    cls = (g_idx * PAIRS_PER_GROUP + a * (2.0 * EXPERTS_PER_GROUP - 1.0 - a) * 0.5
           + (b - a - 1.0))

    @pl.when((pl.program_id(0) == 0) & (pl.program_id(1) == 0))
    def _():
        count_ref[...] = jnp.zeros_like(count_ref)

    tm = logits.shape[0]
    onehot = jnp.where(lane == cls, 1.0, 0.0)
    r_i = lax.broadcasted_iota(jnp.int32, (BLOCK, BLOCK), 0)
    c_i = lax.broadcasted_iota(jnp.int32, (BLOCK, BLOCK), 1)
    earlier = jnp.where(r_i > c_i, 1.0, 0.0).astype(BF16)
    running = count_ref[...]
    before = []
    for blk in range(tm // BLOCK):
        oh = onehot[blk * BLOCK:(blk + 1) * BLOCK]
        before.append(_dot(earlier, oh.astype(BF16)) + running)
        running = running + jnp.sum(oh, axis=0, keepdims=True)
    before = jnp.concatenate(before, axis=0)
    rank = jnp.sum(jnp.where(lane == cls, before, 0.0), axis=-1, keepdims=True)
    count_ref[...] = running
    counts_out_ref[...] = running

    route = jnp.where(lane == ROUTE_CLS, cls, 0.0)
    route = jnp.where(lane == ROUTE_RANK, rank, route)
    route_ref[0, 0] = route.T[:ROUTE_ROWS].astype(jnp.int32)


def _out_proj(x, a_out, b_out, p, tm):
    B, S, D = x.shape
    full = lambda shape: pl.BlockSpec(shape, lambda b, i: (0,) * len(shape))
    row = lambda w: pl.BlockSpec((1, tm, w), lambda b, i: (b, i, 0))
    return pl.pallas_call(
        _out_proj_kernel,
        grid=(B, S // tm),
        in_specs=[row(D), row(SWA_Q_COLS), row(MLA_OUT_COLS),
                  full((1, D)), full((1, D)), full((1, SWA_Q_COLS)), full((1, MLA_OUT_COLS)),
                  full((D, D)), full((1, D)), full((1, D)),
                  full((D, 2 * ROUTE_LANES)), full((1, ROUTE_LANES))],
        out_specs=(row(D), pl.BlockSpec((1, 1, ROUTE_ROWS, tm), lambda b, i: (b, i, 0, 0)),
                   full((1, ROUTE_LANES))),
        out_shape=(jax.ShapeDtypeStruct((B, S, D), F32),
                   jax.ShapeDtypeStruct((B, S // tm, ROUTE_ROWS, tm), jnp.int32),
                   jax.ShapeDtypeStruct((1, ROUTE_LANES), F32)),
        scratch_shapes=[pltpu.VMEM((1, ROUTE_LANES), F32)],
        compiler_params=_cparams(("arbitrary", "arbitrary")),
        name="out_proj_route",
    )(x, a_out, b_out, p["ln_in_g"], p["ln_in_b"], p["ga"], p["gb"], p["w_o"], p["ln1_g"],
      p["ln1_b"], p["w_route"], p["b_route"])


SLAB = D_MODEL // LANES
TAIL_CHUNKS = tuple(MOE_TILE_ROWS >> b for b in range(MOE_TILE_ROWS.bit_length()))


def _slab_copy(src_ref, src_tok8, dst_ref, dst_tok8, sem):
    return pltpu.make_async_copy(src_ref.at[pl.ds(pl.multiple_of(src_tok8, SLAB), SLAB)],
                                 dst_ref.at[pl.ds(pl.multiple_of(dst_tok8, SLAB), SLAB)], sem)


def _to_slabs(x, slab_ref, n):
    for s in range(SLAB):
        slab_ref[pl.ds(s, n, stride=SLAB), :] = x[:, s * LANES:(s + 1) * LANES]


def _from_slabs(slab_ref, n):
    return jnp.concatenate([slab_ref[pl.ds(s, n, stride=SLAB), :] for s in range(SLAB)], axis=1)


def _dispatch_kernel(padstart_ref, taillen_ref, nu_ref, pos_ref, h1_ref, xs_ref, rows_ref, zero_ref,
                     sem, zero_sem, *, tm, n_tiles, slots_per_step):
    t = pl.program_id(0)
    first = t == 0
    last = t == pl.num_programs(0) - 1
    staged = rows_ref.at[t % 2]
    moving = rows_ref.at[(t + 1) % 2]

    def zero_fill(start, rows):
        return pltpu.make_async_copy(
            zero_ref.at[pl.ds(0, rows * SLAB)],
            xs_ref.at[pl.ds(pl.multiple_of(start, SLAB), rows * SLAB)], zero_sem)

    def zero_fills(act):
        def one_slot(i, carry):
            c = t * slots_per_step + i

            @pl.when(c < N_CLASSES)
            def _():
                tail = taillen_ref[c]
                for rows in TAIL_CHUNKS:
                    @pl.when((tail & rows) != 0)
                    def _():
                        done = tail & ~(2 * rows - 1)
                        act(zero_fill(padstart_ref[c] + done * SLAB, rows))

            @pl.when(nu_ref[0] + c < n_tiles)
            def _():
                act(zero_fill((nu_ref[0] + c) * (MOE_TILE_ROWS * SLAB), MOE_TILE_ROWS))

            return carry

        lax.fori_loop(0, slots_per_step, one_slot, 0)

    @pl.when(t == 0)
    def _():
        zero_ref[...] = jnp.zeros_like(zero_ref)

    zero_fills(lambda fill: fill.start())

    def stage(r0):
        row0 = pl.multiple_of(r0 * DMA_BURST, DMA_BURST)
        _to_slabs(h1_ref[pl.ds(row0, DMA_BURST), :],
                  staged.at[pl.ds(row0 * SLAB, DMA_BURST * SLAB)], DMA_BURST)

    def scatter(r0):
        for k in range(DMA_BURST):
            r = r0 * DMA_BURST + k
            _slab_copy(moving, r * SLAB, xs_ref, pos_ref[0, 0, r], sem).start(priority=k % 2)

    def bursts(*steps):
        def body(r0, carry):
            for step in steps:
                step(r0)
            return carry

        lax.fori_loop(0, tm // DMA_BURST, body, 0)

    @pl.when(first)
    def _():
        _to_slabs(h1_ref[...], staged, tm)

    @pl.when(jnp.logical_not(first | last))
    def _():
        bursts(scatter, stage)

    @pl.when(last)
    def _():
        bursts(scatter)

    @pl.when(jnp.logical_not(first))
    def _():
        pltpu.make_async_copy(moving, xs_ref.at[pl.ds(0, tm * SLAB)], sem).wait()

    zero_fills(lambda fill: fill.wait())


def _dispatch(h1, pos8, pad_start8, tail_len, n_used, n_tiles, tm):
    T, D = h1.shape
    n_blocks = T // tm
    n_steps = n_blocks + 1
    pos8 = pos8.reshape(n_blocks, 1, tm)
    grid_spec = pltpu.PrefetchScalarGridSpec(
        num_scalar_prefetch=3,
        grid=(n_steps,),
        in_specs=[pl.BlockSpec((1, 1, tm), lambda t, ps, tl, nu: (jnp.maximum(t - 1, 0), 0, 0),
                               memory_space=pltpu.SMEM),
                  pl.BlockSpec((tm, D), lambda t, ps, tl, nu: (jnp.minimum(t, n_blocks - 1), 0))],
        out_specs=pl.BlockSpec(memory_space=pl.ANY),
        scratch_shapes=[pltpu.VMEM((2, tm * SLAB, LANES), F32),
                        pltpu.VMEM((MOE_TILE_ROWS * SLAB, LANES), F32),
                        pltpu.SemaphoreType.DMA,
                        pltpu.SemaphoreType.DMA],
    )
    return pl.pallas_call(
        functools.partial(_dispatch_kernel, tm=tm, n_tiles=n_tiles,
                          slots_per_step=pl.cdiv(N_CLASSES, n_steps)),
        grid_spec=grid_spec,
        out_shape=jax.ShapeDtypeStruct((n_tiles * MOE_TILE_ROWS * SLAB, LANES), F32),
        compiler_params=_cparams(("arbitrary",)),
        name="moe_dispatch",
    )(pad_start8, tail_len, n_used, pos8, h1)


def _experts_kernel(tlo_ref, thi_ref, nu_ref, x_ref, wg_lo_ref, wg_hi_ref, wu_lo_ref, wu_hi_ref,
                    wd_lo_ref, wd_hi_ref, wrt_ref, brt_ref, y_ref):
    j = pl.program_id(0)
    tr = MOE_TILE_ROWS

    @pl.when(j < nu_ref[0])
    def _():
        x = _from_slabs(x_ref, tr).astype(BF16)
        lo_e = tlo_ref[j]
        hi_e = thi_ref[j]
        logits = _dot(x, wrt_ref[...]) + brt_ref[...]
        lane = lax.broadcasted_iota(jnp.int32, logits.shape, 1)
        gl = jnp.where(lane < N_GROUPS, logits, -jnp.inf)
        ge = jnp.exp(gl - jnp.max(gl, axis=-1, keepdims=True))
        pick = lambda v, at: jnp.sum(jnp.where(lane == at, v, 0.0), axis=-1, keepdims=True)
        g_prob = pick(ge, lo_e // EXPERTS_PER_GROUP) / jnp.sum(ge, axis=-1, keepdims=True)
        e_lo = pick(logits, EXPERT_LANE0 + lo_e)
        e_hi = pick(logits, EXPERT_LANE0 + hi_e)
        gates = (g_prob / (1.0 + jnp.exp(e_hi - e_lo)), g_prob / (1.0 + jnp.exp(e_lo - e_hi)))

        y = None
        for gate, wg_ref, wu_ref, wd_ref in zip(gates, (wg_lo_ref, wg_hi_ref), (wu_lo_ref, wu_hi_ref),
                                                (wd_lo_ref, wd_hi_ref)):
            g = _dot(x, wg_ref[0])
            u = _dot(x, wu_ref[0])
            hid = g * (1.0 / (1.0 + jnp.exp(-g))) * u * gate
            y_k = _dot(hid.astype(BF16), wd_ref[0])
            y = y_k if y is None else y + y_k
        _to_slabs(y, y_ref, tr)


def _experts(xs, tile_lo, tile_hi, n_used, p, n_tiles):
    D = D_MODEL
    tr = MOE_TILE_ROWS
    last_used = lambda j, nu: jnp.minimum(j, nu[0] - 1)
    wgu = lambda pick: pl.BlockSpec((1, D, D_EXPERT),
                                    lambda j, tlo, thi, nu: (pick(tlo, thi)[j], 0, 0))
    wd = lambda pick: pl.BlockSpec((1, D_EXPERT, D),
                                   lambda j, tlo, thi, nu: (pick(tlo, thi)[j], 0, 0))
    lo = lambda tlo, thi: tlo
    hi = lambda tlo, thi: thi
    grid_spec = pltpu.PrefetchScalarGridSpec(
        num_scalar_prefetch=3,
        grid=(n_tiles,),
        in_specs=[pl.BlockSpec((tr * SLAB, LANES), lambda j, tlo, thi, nu: (last_used(j, nu), 0)),
                  wgu(lo), wgu(hi), wgu(lo), wgu(hi), wd(lo), wd(hi),
                  pl.BlockSpec((D, ROUTE_LANES), lambda j, tlo, thi, nu: (0, 0)),
                  pl.BlockSpec((1, ROUTE_LANES), lambda j, tlo, thi, nu: (0, 0))],
        out_specs=pl.BlockSpec((tr * SLAB, LANES), lambda j, tlo, thi, nu: (last_used(j, nu), 0)),
    )
    return pl.pallas_call(
        _experts_kernel,
        grid_spec=grid_spec,
        out_shape=jax.ShapeDtypeStruct(xs.shape, F32),
        input_output_aliases={3: 0},
        compiler_params=_cparams(("arbitrary",)),
        name="moe_experts",
    )(tile_lo, tile_hi, n_used, xs, p["w_gate"], p["w_gate"], p["w_up"], p["w_up"],
      p["w_down"], p["w_down"],
      p["w_route"], p["b_route"])


def _combine_kernel(pos_ref, pos_next_ref, h1_ref, ys_ref, g2_ref, b2_ref, o_ref,
                    ybuf_ref, sem, *, tm):
    t = pl.program_id(0)
    slot = t % 2

    def gather(idx_ref, dst_slot):
        def issue(r0, carry):
            for k in range(DMA_BURST):
                r = r0 * DMA_BURST + k
                _slab_copy(ys_ref, idx_ref[0, 0, r], ybuf_ref.at[dst_slot], r * SLAB,
                           sem.at[dst_slot]).start(priority=k % 2)
            return carry

        lax.fori_loop(0, tm // DMA_BURST, issue, 0)

    @pl.when(t == 0)
    def _():
        gather(pos_ref, 0)

    @pl.when(t + 1 < pl.num_programs(0))
    def _():
        gather(pos_next_ref, 1 - slot)

    pltpu.make_async_copy(ys_ref.at[pl.ds(0, tm * SLAB)], ybuf_ref.at[slot], sem.at[slot]).wait()
    ffn = _from_slabs(ybuf_ref.at[slot], tm)
    o_ref[...] = _layer_norm(ALPHA * h1_ref[...] + ffn, g2_ref[...], b2_ref[...])


def _combine(h1, pos8, ys, p, tm):
    T, D = h1.shape
    n_steps = T // tm
    pos8 = pos8.reshape(n_steps, 1, tm)
    full = lambda shape: pl.BlockSpec(shape, lambda t: (0,) * len(shape))
    return pl.pallas_call(
        functools.partial(_combine_kernel, tm=tm),
        grid=(n_steps,),
        in_specs=[pl.BlockSpec((1, 1, tm), lambda t: (t, 0, 0), memory_space=pltpu.SMEM),
                  pl.BlockSpec((1, 1, tm), lambda t: (jnp.minimum(t + 1, n_steps - 1), 0, 0),
                               memory_space=pltpu.SMEM),
                  pl.BlockSpec((tm, D), lambda t: (t, 0)),
                  pl.BlockSpec(memory_space=pl.ANY),
                  full((1, D)), full((1, D))],
        out_specs=pl.BlockSpec((tm, D), lambda t: (t, 0)),
        out_shape=jax.ShapeDtypeStruct((T, D), F32),
        scratch_shapes=[pltpu.VMEM((2, tm * SLAB, LANES), F32),
                        pltpu.SemaphoreType.DMA((2,))],
        compiler_params=_cparams(("arbitrary",)),
        name="moe_combine_ln2",
    )(pos8, pos8, h1, ys, p["ln2_g"], p["ln2_b"])


def _pair_layout(n):
    order, prev_end = [], None
    for a in range(n - 1):
        end = n - 1 if (a >= n - 2 or (n - 3 - a) % 2 == 0) else n - 2
        middle = [b for b in range(a + 1, n) if b not in (prev_end, end)]
        head = [] if prev_end in (None, end) else [prev_end]
        order += [(a, b) for b in head + middle + [end]]
        prev_end = end
    return order


def _dispatch_plan(route, counts):
    T = route.shape[0] * route.shape[1] * route.shape[3]
    tr = MOE_TILE_ROWS
    n_tiles = T // tr + N_CLASSES
    cnt = counts[0, :N_CLASSES].astype(jnp.int32)
    padded = jnp.maximum((cnt + tr - 1) // tr, 1) * tr
    classes = jnp.arange(N_CLASSES, dtype=jnp.int32)
    pairs = [(a, b) for a in range(EXPERTS_PER_GROUP) for b in range(a + 1, EXPERTS_PER_GROUP)]
    layout = _pair_layout(EXPERTS_PER_GROUP)
    place = jnp.asarray([g * PAIRS_PER_GROUP + layout.index(pair) for g in range(N_GROUPS)
                         for pair in pairs], jnp.int32)
    seg_end = jnp.sum(jnp.where(place[None, :] <= place[:, None], padded[None, :], 0), axis=1)
    seg_start = seg_end - padded
    cls = route[:, :, ROUTE_CLS, :].reshape(T)
    rank = route[:, :, ROUTE_RANK, :].reshape(T)
    start_of = jnp.sum(jnp.where(cls[:, None] == classes, seg_start, 0), axis=-1)
    pos8 = (start_of + rank) * SLAB
    tile_row0 = jnp.arange(n_tiles, dtype=jnp.int32) * tr
    tile_place = jnp.sum((seg_end[None, :] <= tile_row0[:, None]).astype(jnp.int32), axis=-1)
    tile_place = jnp.minimum(tile_place, N_CLASSES - 1)
    lo_of = jnp.asarray([g * EXPERTS_PER_GROUP + a for g in range(N_GROUPS) for a, _ in pairs],
                        jnp.int32)
    hi_of = jnp.asarray([g * EXPERTS_PER_GROUP + b for g in range(N_GROUPS) for _, b in pairs],
                        jnp.int32)
    is_cls = tile_place[:, None] == place
    tile_lo = jnp.sum(jnp.where(is_cls, lo_of, 0), axis=-1)
    tile_hi = jnp.sum(jnp.where(is_cls, hi_of, 0), axis=-1)
    n_used = (jnp.sum(padded, keepdims=True) // tr).astype(jnp.int32)
    return pos8, (seg_start + cnt) * SLAB, padded - cnt, tile_lo, tile_hi, n_used, n_tiles


def _rope_tables(pos):
    d = SWA_HEAD_DIM
    inv_freq = ROPE_THETA ** (-jnp.arange(0, d, 2, dtype=F32) / d)
    ang = pos[:, None] * inv_freq[None, :]
    c, s = jnp.cos(ang), jnp.sin(ang)
    return (jnp.concatenate([c, c, c, c], axis=-1), jnp.concatenate([-s, s, -s, s], axis=-1),
            jnp.concatenate([c, c], axis=-1).T, jnp.concatenate([s, s], axis=-1).T)


def _prep_params(ln_in_g, ln_in_b, w_in, swa_sinks, mla_q_norm_g, mla_w_uq, mla_kv_norm_g, mla_w_ukv,
                 swa_out_norm_g, mla_out_norm_g, w_o, ln1_g, ln1_b, moe_w_group, moe_b_group,
                 moe_w_router, moe_b_router, moe_w_gate, moe_w_up, moe_w_down, ln2_g, ln2_b):
    hd = SWA_HEAD_DIM

    def by_head(t, axis):
        return [lax.slice_in_dim(t, h * hd, (h + 1) * hd, axis=axis) for h in SWA_HEAD_ORDER]

    w = w_in[0]
    w_pad = jnp.zeros((D_MODEL, IN_COLS_PADDED - w.shape[1]), w.dtype)
    w_in_p = jnp.concatenate(by_head(w, 1) + [w[:, SWA_Q_COLS:], w_pad], axis=1).astype(BF16)

    dqk = MLA_NOPE_DIM + MLA_ROPE_DIM
    wuq = mla_w_uq[0].reshape(MLA_Q_RANK, MLA_HEADS, dqk)
    wuq = jnp.pad(wuq, ((0, 0), (0, 0), (0, MLA_QK_PAD - dqk)))
    wuq_t = jnp.transpose(wuq, (1, 2, 0)).astype(BF16)
    wukv = mla_w_ukv[0].reshape(MLA_KV_RANK, MLA_HEADS, MLA_NOPE_DIM + MLA_V_DIM)
    wuk = wukv[:, :, :MLA_NOPE_DIM].reshape(MLA_KV_RANK, MLA_HEADS * MLA_NOPE_DIM).astype(BF16)
    wuv_t = jnp.transpose(wukv[:, :, MLA_NOPE_DIM:], (1, 2, 0)).astype(BF16)

    w_o0 = w_o[0]
    w_o_p = jnp.concatenate(by_head(w_o0, 0) + [w_o0[SWA_Q_COLS:]], axis=0).astype(BF16)

    def route_lanes(group_part, expert_part):
        rows = group_part.shape[0]
        gap = jnp.zeros((rows, EXPERT_LANE0 - N_GROUPS), F32)
        tail = jnp.zeros((rows, ROUTE_LANES - EXPERT_LANE0 - N_EXPERTS), F32)
        return jnp.concatenate([group_part, gap, expert_part, tail], axis=1)

    w_route = route_lanes(moe_w_group[0], moe_w_router[0])
    w_route_hi = w_route.astype(BF16)
    w_route_lo = (w_route - w_route_hi.astype(F32)).astype(BF16)
    w_route = jnp.concatenate([w_route_hi, w_route_lo], axis=1)
    b_route = route_lanes(moe_b_group, moe_b_router)

    return {
        "ln_in_g": ln_in_g[None, :], "ln_in_b": ln_in_b[None, :],
        "w_in": w_in_p,
        "sinks": swa_sinks[0],
        "gq": mla_q_norm_g, "w_uq_t": wuq_t,
        "gkv": mla_kv_norm_g, "w_uk": wuk, "w_uv_t": wuv_t,
        "ga": jnp.concatenate(by_head(swa_out_norm_g, 1), axis=1), "gb": mla_out_norm_g,
        "w_o": w_o_p, "ln1_g": ln1_g, "ln1_b": ln1_b,
        "w_route": w_route, "b_route": b_route,
        "w_gate": moe_w_gate[0].astype(BF16), "w_up": moe_w_up[0].astype(BF16),
        "w_down": moe_w_down[0].astype(BF16),
        "ln2_g": ln2_g, "ln2_b": ln2_b,
    }


def kernel(x, meta_tokens, ln_in_g, ln_in_b, w_in, swa_sinks, mla_q_norm_g, mla_w_uq, mla_kv_norm_g, mla_w_ukv, swa_out_norm_g, mla_out_norm_g, w_o, ln1_g, ln1_b, moe_w_group, moe_b_group, moe_w_router, moe_b_router, moe_w_gate, moe_w_up, moe_w_down, ln2_g, ln2_b):
    B, S, D = x.shape
    p = _prep_params(ln_in_g, ln_in_b, w_in, swa_sinks, mla_q_norm_g, mla_w_uq, mla_kv_norm_g,
                     mla_w_ukv, swa_out_norm_g, mla_out_norm_g, w_o, ln1_g, ln1_b, moe_w_group,
                     moe_b_group, moe_w_router, moe_b_router, moe_w_gate, moe_w_up, moe_w_down,
                     ln2_g, ln2_b)
    T = B * S
    tm = min(PROJ_TILE, S)

    x_meta = jnp.concatenate([jnp.zeros((N_PAD, D), x.dtype), meta_tokens.astype(x.dtype)])[None]
    pos_meta = jnp.maximum(jnp.arange(BLOCK) - N_PAD, 0).astype(F32)
    pos_tok = (jnp.arange(S) + N_META).astype(F32)
    _, ka_m, va_m, _, km_m, vt_m = _in_proj(x_meta, _rope_tables(pos_meta), p, BLOCK)
    a_out, qt, km, vt = _in_proj_swa(x, _rope_tables(pos_tok), p, ka_m, va_m, tm)
    b_out = _mla(qt, km, vt, km_m[:, :, N_PAD:, :], vt_m[..., N_PAD:], tm)
    h1, route, counts = _out_proj(x, a_out, b_out, p, min(WIDE_TILE, S))
    h1 = h1.reshape(B * S, D)

    pos8, pad_start8, tail_len, tile_lo, tile_hi, n_used, n_tiles = _dispatch_plan(route, counts)
    xs = _dispatch(h1, pos8, pad_start8, tail_len, n_used, n_tiles, min(DISPATCH_TILE, T))
    ys = _experts(xs, tile_lo, tile_hi, n_used, p, n_tiles)
    out = _combine(h1, pos8, ys, p, min(COMBINE_TILE, T))
    return out.reshape(B, S, D)
```

```python
import functools

import jax
import jax.numpy as jnp
from jax import lax
from jax.experimental import pallas as pl
from jax.experimental.pallas import tpu as pltpu

D_MODEL = 1024
N_META = 16
BLOCK = 128
N_PAD = BLOCK - N_META
ROPE_THETA = 10000.0
SWA_HEADS = 8
SWA_KV_HEADS = 2
SWA_HEAD_DIM = 64
MLA_HEADS = 4
MLA_Q_RANK = 256
MLA_KV_RANK = 256
MLA_NOPE_DIM = 128
MLA_ROPE_DIM = 64
MLA_V_DIM = 128
SWA_Q_COLS = SWA_HEADS * SWA_HEAD_DIM
SWA_KV_COLS = SWA_KV_HEADS * SWA_HEAD_DIM
MLA_OUT_COLS = MLA_HEADS * MLA_V_DIM
N_GROUPS = 4
EXPERTS_PER_GROUP = 8
N_EXPERTS = N_GROUPS * EXPERTS_PER_GROUP
D_EXPERT = 256
LN_EPS = 1e-5
RMS_EPS = 1e-6
DEPTH = 1
ALPHA = (2.0 * DEPTH) ** 0.25
NEG = -1e30
LOG2_E = 1.4426950408889634

LANES = 128
COL_KA = SWA_Q_COLS
COL_VA = COL_KA + SWA_KV_COLS
COL_CQ = COL_VA + SWA_KV_COLS
COL_CKV = COL_CQ + MLA_Q_RANK
COL_KR = COL_CKV + MLA_KV_RANK
IN_COLS_PADDED = COL_KR + LANES
MLA_QK_PAD = 256
ROUTE_LANES = 128
EXPERT_LANE0 = 32
ROUTE_CLS, ROUTE_RANK = range(2)
ROUTE_ROWS = 8
PAIRS_PER_GROUP = EXPERTS_PER_GROUP * (EXPERTS_PER_GROUP - 1) // 2
N_CLASSES = N_GROUPS * PAIRS_PER_GROUP
MOE_TILE_ROWS = 256
PROJ_TILE = 1024
WIDE_TILE = 1024
DISPATCH_TILE = 2048
COMBINE_TILE = 1024
DMA_BURST = 8
VMEM_LIMIT = 56 * 1024 * 1024

SWA_HEAD_ORDER = (0, 4, 1, 5, 2, 6, 3, 7)

BF16 = jnp.bfloat16
F32 = jnp.float32


def _cparams(sem):
    return pltpu.CompilerParams(dimension_semantics=sem, vmem_limit_bytes=VMEM_LIMIT)


def _layer_norm(x, g, b):
    mu = jnp.mean(x, axis=-1, keepdims=True)
    xc = x - mu
    var = jnp.mean(xc * xc, axis=-1, keepdims=True)
    return xc * lax.rsqrt(var + LN_EPS) * g + b


def _rms_scale(x):
    return lax.rsqrt(jnp.mean(x * x, axis=-1, keepdims=True) + RMS_EPS)


def _rope128(x, cos, sin_signed, lane_lo):
    up = pltpu.roll(x, LANES - 32, 1)
    dn = pltpu.roll(x, 32, 1)
    return x * cos + jnp.where(lane_lo, up, dn) * sin_signed


def _dot(a, b):
    return jnp.dot(a, b, preferred_element_type=F32)


def _dot_nt(a, b):
    return lax.dot_general(a, b, (((1,), (1,)), ((), ())), preferred_element_type=F32)


def _in_proj_kernel(x_ref, g_ref, b_ref, w_ref, cos_ref, sin_ref, cost_ref, sint_ref, gq_ref,
                    wuqt_ref, gkv_ref, wuk_ref, wuvt_ref,
                    qa_ref, ka_ref, va_ref, qt_ref, km_ref, vt_ref, h_ref=None):
    h = _layer_norm(x_ref[0], g_ref[...], b_ref[...])
    if h_ref is not None:
        h_ref[0] = h
    u = _dot(h.astype(BF16), w_ref[...])
    cos = cos_ref[...]
    sin = sin_ref[...]
    lane = lax.broadcasted_iota(jnp.int32, (1, LANES), 1)
    lane_lo = (lane % 64) < 32
    rope = functools.partial(_rope128, cos=cos, sin_signed=sin, lane_lo=lane_lo)

    swa_scale = SWA_HEAD_DIM ** -0.5
    for c in range(SWA_Q_COLS // LANES):
        qc = rope(u[:, c * LANES:(c + 1) * LANES]) * swa_scale
        qa_ref[0, :, c * LANES:(c + 1) * LANES] = qc.astype(BF16)
    ka_ref[0] = rope(u[:, COL_KA:COL_VA]).astype(BF16)
    va_ref[0] = u[:, COL_VA:COL_CQ].astype(BF16)

    cq = u[:, COL_CQ:COL_CKV]
    cqn = (cq * _rms_scale(cq) * gq_ref[...]).astype(BF16)
    ckv = u[:, COL_CKV:COL_KR]
    ckvn = (ckv * _rms_scale(ckv) * gkv_ref[...]).astype(BF16)
    k_nope = _dot(ckvn, wuk_ref[...])
    kr = rope(u[:, COL_KR:IN_COLS_PADDED]).astype(BF16)
    q_scale = (MLA_NOPE_DIM + MLA_ROPE_DIM) ** -0.5 * LOG2_E
    cost = cost_ref[...]
    sint = sint_ref[...]
    half = MLA_ROPE_DIM // 2
    for hd in range(MLA_HEADS):
        qt = _dot_nt(wuqt_ref[hd], cqn)
        qr = qt[MLA_NOPE_DIM:MLA_NOPE_DIM + MLA_ROPE_DIM]
        rot = jnp.concatenate([-qr[half:], qr[:half]], axis=0)
        qt_ref[0, hd, 0:MLA_NOPE_DIM, :] = (qt[:MLA_NOPE_DIM] * q_scale).astype(BF16)
        qt_ref[0, hd, MLA_NOPE_DIM:MLA_NOPE_DIM + MLA_ROPE_DIM, :] = (
            (qr * cost + rot * sint) * q_scale).astype(BF16)
        qt_ref[0, hd, MLA_NOPE_DIM + MLA_ROPE_DIM:, :] = (
            qt[MLA_NOPE_DIM + MLA_ROPE_DIM:]).astype(BF16)
        km_ref[0, hd, :, 0:LANES] = k_nope[:, hd * LANES:(hd + 1) * LANES].astype(BF16)
        km_ref[0, hd, :, LANES:2 * LANES] = kr
        vt_ref[0, hd, 0] = _dot_nt(wuvt_ref[hd], ckvn).astype(BF16)


def _in_proj(x, tables, p, tm):
    cos, sin, cos_t, sin_t = tables
    B, S, D = x.shape
    grid = (B, S // tm)
    full = lambda shape: pl.BlockSpec(shape, lambda b, i: (0,) * len(shape))
    out_shape = (
        jax.ShapeDtypeStruct((B, S, SWA_Q_COLS), BF16),
        jax.ShapeDtypeStruct((B, S, SWA_KV_COLS), BF16),
        jax.ShapeDtypeStruct((B, S, SWA_KV_COLS), BF16),
        jax.ShapeDtypeStruct((B, MLA_HEADS, MLA_QK_PAD, S), BF16),
        jax.ShapeDtypeStruct((B, MLA_HEADS, S, MLA_QK_PAD), BF16),
        jax.ShapeDtypeStruct((B, MLA_HEADS, S // tm, MLA_V_DIM, tm), BF16),
    )
    row = lambda w: pl.BlockSpec((1, tm, w), lambda b, i: (b, i, 0))
    head = lambda w: pl.BlockSpec((1, MLA_HEADS, tm, w), lambda b, i: (b, 0, i, 0))
    return pl.pallas_call(
        _in_proj_kernel,
        grid=grid,
        in_specs=[
            row(D),
            full((1, D)), full((1, D)),
            full((D, IN_COLS_PADDED)),
            pl.BlockSpec((tm, LANES), lambda b, i: (i, 0)),
            pl.BlockSpec((tm, LANES), lambda b, i: (i, 0)),
            pl.BlockSpec((MLA_ROPE_DIM, tm), lambda b, i: (0, i)),
            pl.BlockSpec((MLA_ROPE_DIM, tm), lambda b, i: (0, i)),
            full((1, MLA_Q_RANK)), full((MLA_HEADS, MLA_QK_PAD, MLA_Q_RANK)),
            full((1, MLA_KV_RANK)), full((MLA_KV_RANK, MLA_HEADS * MLA_NOPE_DIM)),
            full((MLA_HEADS, MLA_V_DIM, MLA_KV_RANK)),
        ],
        out_specs=(row(SWA_Q_COLS), row(SWA_KV_COLS), row(SWA_KV_COLS),
                   pl.BlockSpec((1, MLA_HEADS, MLA_QK_PAD, tm), lambda b, i: (b, 0, 0, i)),
                   head(MLA_QK_PAD),
                   pl.BlockSpec((1, MLA_HEADS, 1, MLA_V_DIM, tm), lambda b, i: (b, 0, i, 0, 0))),
        out_shape=out_shape,
        compiler_params=_cparams(("parallel", "parallel")),
        name="in_proj",
    )(x, p["ln_in_g"], p["ln_in_b"], p["w_in"], cos, sin, cos_t, sin_t, p["gq"], p["w_uq_t"],
      p["gkv"], p["w_uk"], p["w_uv_t"])


def _swa_blocks(sink_ref, qa_ref, kc_ref, vc_ref, kprev, vprev, first, o_ref, blocks_per_step):
    row = lax.broadcasted_iota(jnp.int32, (2 * BLOCK, BLOCK), 0) % BLOCK
    col = lax.broadcasted_iota(jnp.int32, (2 * BLOCK, BLOCK), 1)
    own = col <= row
    meta_ok = col >= jnp.where(first, N_PAD, 0)
    lane = lax.broadcasted_iota(jnp.int32, (1, LANES), 1)
    lane_kv0 = lane < SWA_HEAD_DIM
    top_rows = lax.broadcasted_iota(jnp.int32, (2 * BLOCK, 1), 0) < BLOCK

    for r in range(blocks_per_step):
        rows = slice(r * BLOCK, (r + 1) * BLOCK)
        if r == 0:
            kp, vp = kprev, vprev
        else:
            prev_rows = slice((r - 1) * BLOCK, r * BLOCK)
            kp, vp = kc_ref[0, prev_rows, :], vc_ref[0, prev_rows, :]
        kc, vc = kc_ref[0, rows, :], vc_ref[0, rows, :]
        for c in range(SWA_Q_COLS // LANES):
            qc = qa_ref[0, rows, c * LANES:(c + 1) * LANES]
            zero = jnp.zeros_like(qc)
            q2 = jnp.concatenate([jnp.where(lane_kv0, qc, zero),
                                  jnp.where(lane_kv0, zero, qc)], axis=0)
            s_prev = _dot_nt(q2, kp)
            if r == 0:
                s_prev = jnp.where(meta_ok, s_prev, NEG)
            s = jnp.where(own, _dot_nt(q2, kc), s_prev)
            sink = jnp.where(top_rows, sink_ref[c], sink_ref[c + 4])
            m = jnp.maximum(jnp.max(s, axis=-1, keepdims=True), sink)
            e = jnp.exp(s - m)
            denom = jnp.sum(e, axis=-1, keepdims=True) + jnp.exp(sink - m)
            pr = e * (1.0 / denom)
            o2 = (_dot(jnp.where(own, pr, 0.0).astype(BF16), vc)
                  + _dot(jnp.where(own, 0.0, pr).astype(BF16), vp))
            o = jnp.where(lane_kv0, o2[:BLOCK], o2[BLOCK:])
            o_ref[0, rows, c * LANES:(c + 1) * LANES] = o.astype(BF16)


def _in_proj_swa_kernel(sink_ref, x_ref, g_ref, b_ref, w_ref, cos_ref, sin_ref, cost_ref, sint_ref,
                        gq_ref, wuqt_ref, gkv_ref, wuk_ref, wuvt_ref, kmeta_ref, vmeta_ref,
                        a_ref, qt_ref, km_ref, vt_ref, h_ref,
                        qa_s, ka_s, va_s, kprev_s, vprev_s):
    _in_proj_kernel(x_ref, g_ref, b_ref, w_ref, cos_ref, sin_ref, cost_ref, sint_ref, gq_ref,
                    wuqt_ref, gkv_ref, wuk_ref, wuvt_ref, qa_s, ka_s, va_s, qt_ref, km_ref, vt_ref,
                    h_ref=h_ref)
    first = pl.program_id(1) == 0
    kprev = jnp.where(first, kmeta_ref[0], kprev_s[...])
    vprev = jnp.where(first, vmeta_ref[0], vprev_s[...])
    tm = qa_s.shape[1]
    _swa_blocks(sink_ref, qa_s, ka_s, va_s, kprev, vprev, first, a_ref, tm // BLOCK)
    kprev_s[...] = ka_s[0, tm - BLOCK:, :]
    vprev_s[...] = va_s[0, tm - BLOCK:, :]


def _in_proj_swa(x, tables, p, ka_meta, va_meta, tm):
    cos, sin, cos_t, sin_t = tables
    B, S, D = x.shape
    kvw = SWA_KV_COLS
    full = lambda shape: pl.BlockSpec(shape, lambda b, i: (0,) * len(shape))
    row = lambda w: pl.BlockSpec((1, tm, w), lambda b, i: (b, i, 0))
    out_shape = (
        jax.ShapeDtypeStruct((B, S, SWA_Q_COLS), BF16),
        jax.ShapeDtypeStruct((B, MLA_HEADS, MLA_QK_PAD, S), BF16),
        jax.ShapeDtypeStruct((B, MLA_HEADS, S, MLA_QK_PAD), BF16),
        jax.ShapeDtypeStruct((B, MLA_HEADS, S // tm, MLA_V_DIM, tm), BF16),
        jax.ShapeDtypeStruct((B, S, D), F32),
    )
    return pl.pallas_call(
        _in_proj_swa_kernel,
        grid=(B, S // tm),
        in_specs=[
            pl.BlockSpec(memory_space=pltpu.SMEM),
            row(D),
            full((1, D)), full((1, D)),
            full((D, IN_COLS_PADDED)),
            pl.BlockSpec((tm, LANES), lambda b, i: (i, 0)),
            pl.BlockSpec((tm, LANES), lambda b, i: (i, 0)),
            pl.BlockSpec((MLA_ROPE_DIM, tm), lambda b, i: (0, i)),
            pl.BlockSpec((MLA_ROPE_DIM, tm), lambda b, i: (0, i)),
            full((1, MLA_Q_RANK)), full((MLA_HEADS, MLA_QK_PAD, MLA_Q_RANK)),
            full((1, MLA_KV_RANK)), full((MLA_KV_RANK, MLA_HEADS * MLA_NOPE_DIM)),
            full((MLA_HEADS, MLA_V_DIM, MLA_KV_RANK)),
            full((1, BLOCK, kvw)), full((1, BLOCK, kvw)),
        ],
        out_specs=(row(SWA_Q_COLS),
                   pl.BlockSpec((1, MLA_HEADS, MLA_QK_PAD, tm), lambda b, i: (b, 0, 0, i)),
                   pl.BlockSpec((1, MLA_HEADS, tm, MLA_QK_PAD), lambda b, i: (b, 0, i, 0)),
                   pl.BlockSpec((1, MLA_HEADS, 1, MLA_V_DIM, tm), lambda b, i: (b, 0, i, 0, 0)),
                   row(D)),
        out_shape=out_shape,
        scratch_shapes=[pltpu.VMEM((1, tm, SWA_Q_COLS), BF16), pltpu.VMEM((1, tm, kvw), BF16),
                        pltpu.VMEM((1, tm, kvw), BF16), pltpu.VMEM((BLOCK, kvw), BF16),
                        pltpu.VMEM((BLOCK, kvw), BF16)],
        compiler_params=_cparams(("parallel", "arbitrary")),
        name="in_proj_swa",
    )(p["sinks"], x, p["ln_in_g"], p["ln_in_b"], p["w_in"], cos, sin, cos_t, sin_t, p["gq"],
      p["w_uq_t"], p["gkv"], p["w_uk"], p["w_uv_t"], ka_meta, va_meta)


def _mla_kernel(qt_ref, k_ref, vt_ref, kmeta_ref, vtmeta_ref, o_ref,
                s0a_ref, s0b_ref, s1a_ref, s1b_ref, smax_ref, m_ref, l_ref, acc_ref, *, tq):
    u = pl.program_id(2)
    qts = (qt_ref[0, 0, :, :tq], qt_ref[0, 0, :, tq:])
    bufs = ((s0a_ref, s0b_ref), (s1a_ref, s1b_ref))

    for g in range(2):
        s = _dot(kmeta_ref[0, 0], qts[g])
        m = jnp.max(s, axis=0, keepdims=True)
        p = jnp.exp2(s - m)
        m_ref[g] = m
        l_ref[g] = jnp.sum(p, axis=0, keepdims=True)
        acc_ref[g] = _dot(vtmeta_ref[0, 0, 0], p.astype(BF16))

    def scores_into(g, ab, j):
        start = pl.multiple_of(j * tq, tq)
        s = _dot(k_ref[0, 0, pl.ds(start, tq), :], qts[g])
        bufs[g][ab][...] = s
        smax_ref[g, ab] = jnp.max(s, axis=0, keepdims=True)

    def absorb(g, ab, j, causal=False):
        s = bufs[g][ab][...]
        if causal:
            key = lax.broadcasted_iota(jnp.int32, (tq, tq), 0)
            query = lax.broadcasted_iota(jnp.int32, (tq, tq), 1)
            s = jnp.where(key <= query, s, NEG)
            smax = jnp.max(s, axis=0, keepdims=True)
        else:
            smax = smax_ref[g, ab]
        m = m_ref[g]
        m_new = jnp.maximum(m, smax)
        alpha = jnp.exp2(m - m_new)
        p = jnp.exp2(s - m_new)
        m_ref[g] = m_new
        l_ref[g] = alpha * l_ref[g] + jnp.sum(p, axis=0, keepdims=True)
        acc_ref[g] = alpha * acc_ref[g] + _dot(vt_ref[0, 0, j], p.astype(BF16))

    buf_a, buf_b = 0, 1
    for g in range(2):
        scores_into(g, buf_a, 0)

    def pair(t, carry):
        scores_into(0, buf_b, 2 * t + 1)
        absorb(0, buf_a, 2 * t)
        scores_into(1, buf_b, 2 * t + 1)
        scores_into(0, buf_a, 2 * t + 2)
        absorb(1, buf_a, 2 * t)
        absorb(0, buf_b, 2 * t + 1)
        scores_into(1, buf_a, 2 * t + 2)
        absorb(1, buf_b, 2 * t + 1)
        return carry

    lax.fori_loop(0, u, pair, 0)

    scores_into(1, buf_b, 2 * u + 1)
    absorb(1, buf_a, 2 * u)
    absorb(0, buf_a, 2 * u, causal=True)
    absorb(1, buf_b, 2 * u + 1, causal=True)

    for g in range(2):
        o_ref[0, g * tq:(g + 1) * tq, :] = (acc_ref[g] * (1.0 / l_ref[g])).T.astype(BF16)


def _mla(qt, km, vt, km_meta, vt_meta, tq):
    B, H, _, S = qt.shape
    assert S % (2 * tq) == 0, "a step handles an (even, odd) pair of query tiles"
    return pl.pallas_call(
        functools.partial(_mla_kernel, tq=tq),
        grid=(B, H, S // (2 * tq)),
        in_specs=[
            pl.BlockSpec((1, 1, MLA_QK_PAD, 2 * tq), lambda b, h, u: (b, h, 0, u)),
            pl.BlockSpec((1, 1, S, MLA_QK_PAD), lambda b, h, u: (b, h, 0, 0)),
            pl.BlockSpec((1, 1, S // tq, MLA_V_DIM, tq), lambda b, h, u: (b, h, 0, 0, 0)),
            pl.BlockSpec((1, 1, N_META, MLA_QK_PAD), lambda b, h, u: (0, h, 0, 0)),
            pl.BlockSpec((1, 1, 1, MLA_V_DIM, N_META), lambda b, h, u: (0, h, 0, 0, 0)),
        ],
        out_specs=pl.BlockSpec((1, 2 * tq, MLA_V_DIM), lambda b, h, u: (b, u, h)),
        out_shape=jax.ShapeDtypeStruct((B, S, MLA_OUT_COLS), BF16),
        scratch_shapes=[pltpu.VMEM((tq, tq), F32), pltpu.VMEM((tq, tq), F32),
                        pltpu.VMEM((tq, tq), F32), pltpu.VMEM((tq, tq), F32),
                        pltpu.VMEM((2, 2, 1, tq), F32),
                        pltpu.VMEM((2, 1, tq), F32), pltpu.VMEM((2, 1, tq), F32),
                        pltpu.VMEM((2, MLA_V_DIM, tq), F32)],
        compiler_params=_cparams(("parallel", "parallel", "arbitrary")),
        name="mla_attn",
    )(qt, km, vt, km_meta, vt_meta)


def _out_proj_kernel(h_ref, a_ref, b_ref, ga_ref, gb_ref, wo_ref, g1_ref, b1_ref,
                     wrt_ref, brt_ref, h1_ref, route_ref, counts_out_ref, count_ref):
    h = h_ref[0]
    a = a_ref[0].astype(F32)
    b = b_ref[0].astype(F32)
    an = (a * _rms_scale(a) * ga_ref[...]).astype(BF16)
    bn = (b * _rms_scale(b) * gb_ref[...]).astype(BF16)
    mix = _dot(jnp.concatenate([an, bn], axis=-1), wo_ref[...])
    h1 = _layer_norm(ALPHA * h + mix, g1_ref[...], b1_ref[...])
    h1_ref[0] = h1

    x_hi = h1.astype(BF16)
    x_lo = (h1 - x_hi.astype(F32)).astype(BF16)
    hi_both = _dot(x_hi, wrt_ref[...])
    logits = (hi_both[:, :ROUTE_LANES] + hi_both[:, ROUTE_LANES:]
              + _dot(x_lo, wrt_ref[:, :ROUTE_LANES]) + brt_ref[...])
    lane = lax.broadcasted_iota(jnp.int32, logits.shape, 1).astype(F32)
    ninf = -jnp.inf
    no_lane = float(ROUTE_LANES)
    gl = jnp.where(lane < N_GROUPS, logits, ninf)
    gmax = jnp.max(gl, axis=-1, keepdims=True)
    g_idx = jnp.min(jnp.where(gl == gmax, lane, no_lane), axis=-1, keepdims=True)
    lo = EXPERT_LANE0 + g_idx * EXPERTS_PER_GROUP
    el = jnp.where((lane >= lo) & (lane < lo + EXPERTS_PER_GROUP), logits, ninf)
    m1 = jnp.max(el, axis=-1, keepdims=True)
    i1 = jnp.min(jnp.where(el == m1, lane, no_lane), axis=-1, keepdims=True)
    el2 = jnp.where(lane == i1, ninf, el)
    m2 = jnp.max(el2, axis=-1, keepdims=True)
    i2 = jnp.min(jnp.where(el2 == m2, lane, no_lane), axis=-1, keepdims=True)

    a = jnp.minimum(i1, i2) - lo
    b = jnp.maximum(i1, i2) - lo
    cls = (g_idx * PAIRS_PER_GROUP + a * (2.0 * EXPERTS_PER_GROUP - 1.0 - a) * 0.5
           + (b - a - 1.0))

    @pl.when((pl.program_id(0) == 0) & (pl.program_id(1) == 0))
    def _():
        count_ref[...] = jnp.zeros_like(count_ref)

    tm = logits.shape[0]
    onehot = jnp.where(lane == cls, 1.0, 0.0)
    r_i = lax.broadcasted_iota(jnp.int32, (BLOCK, BLOCK), 0)
    c_i = lax.broadcasted_iota(jnp.int32, (BLOCK, BLOCK), 1)
    earlier = jnp.where(r_i > c_i, 1.0, 0.0).astype(BF16)
    running = count_ref[...]
    before = []
    for blk in range(tm // BLOCK):
        oh = onehot[blk * BLOCK:(blk + 1) * BLOCK]
        before.append(_dot(earlier, oh.astype(BF16)) + running)
        running = running + jnp.sum(oh, axis=0, keepdims=True)
    before = jnp.concatenate(before, axis=0)
    rank = jnp.sum(jnp.where(lane == cls, before, 0.0), axis=-1, keepdims=True)
    count_ref[...] = running
    counts_out_ref[...] = running

    route = jnp.where(lane == ROUTE_CLS, cls, 0.0)
    route = jnp.where(lane == ROUTE_RANK, rank, route)
    route_ref[0, 0] = route.T[:ROUTE_ROWS].astype(jnp.int32)


def _out_proj(h, a_out, b_out, p, tm):
    B, S, D = h.shape
    full = lambda shape: pl.BlockSpec(shape, lambda b, i: (0,) * len(shape))
    row = lambda w: pl.BlockSpec((1, tm, w), lambda b, i: (b, i, 0))
    return pl.pallas_call(
        _out_proj_kernel,
        grid=(B, S // tm),
        in_specs=[row(D), row(SWA_Q_COLS), row(MLA_OUT_COLS),
                  full((1, SWA_Q_COLS)), full((1, MLA_OUT_COLS)),
                  full((D, D)), full((1, D)), full((1, D)),
                  full((D, 2 * ROUTE_LANES)), full((1, ROUTE_LANES))],
        out_specs=(row(D), pl.BlockSpec((1, 1, ROUTE_ROWS, tm), lambda b, i: (b, i, 0, 0)),
                   full((1, ROUTE_LANES))),
        out_shape=(jax.ShapeDtypeStruct((B, S, D), F32),
                   jax.ShapeDtypeStruct((B, S // tm, ROUTE_ROWS, tm), jnp.int32),
                   jax.ShapeDtypeStruct((1, ROUTE_LANES), F32)),
        scratch_shapes=[pltpu.VMEM((1, ROUTE_LANES), F32)],
        compiler_params=_cparams(("arbitrary", "arbitrary")),
        name="out_proj_route",
    )(h, a_out, b_out, p["ga"], p["gb"], p["w_o"], p["ln1_g"], p["ln1_b"], p["w_route"],
      p["b_route"])


SLAB = D_MODEL // LANES
TAIL_CHUNKS = tuple(MOE_TILE_ROWS >> b for b in range(MOE_TILE_ROWS.bit_length()))


def _slab_copy(src_ref, src_tok8, dst_ref, dst_tok8, sem):
    return pltpu.make_async_copy(src_ref.at[pl.ds(pl.multiple_of(src_tok8, SLAB), SLAB)],
                                 dst_ref.at[pl.ds(pl.multiple_of(dst_tok8, SLAB), SLAB)], sem)


def _to_slabs(x, slab_ref, n):
    for s in range(SLAB):
        slab_ref[pl.ds(s, n, stride=SLAB), :] = x[:, s * LANES:(s + 1) * LANES]


def _from_slabs(slab_ref, n):
    return jnp.concatenate([slab_ref[pl.ds(s, n, stride=SLAB), :] for s in range(SLAB)], axis=1)


def _dispatch_kernel(padstart_ref, taillen_ref, nu_ref, pos_ref, h1_ref, xs_ref, rows_ref, zero_ref,
                     sem, zero_sem, *, tm, n_tiles, slots_per_step):
    t = pl.program_id(0)

    def zero_fill(start, rows):
        return pltpu.make_async_copy(
            zero_ref.at[pl.ds(0, rows * SLAB)],
            xs_ref.at[pl.ds(pl.multiple_of(start, SLAB), rows * SLAB)], zero_sem)

    def zero_fills(act):
        def one_slot(i, carry):
            c = t * slots_per_step + i

            @pl.when(c < N_CLASSES)
            def _():
                tail = taillen_ref[c]
                for rows in TAIL_CHUNKS:
                    @pl.when((tail & rows) != 0)
                    def _():
                        done = tail & ~(2 * rows - 1)
                        act(zero_fill(padstart_ref[c] + done * SLAB, rows))

            @pl.when(nu_ref[0] + c < n_tiles)
            def _():
                act(zero_fill((nu_ref[0] + c) * (MOE_TILE_ROWS * SLAB), MOE_TILE_ROWS))

            return carry

        lax.fori_loop(0, slots_per_step, one_slot, 0)

    @pl.when(t == 0)
    def _():
        zero_ref[...] = jnp.zeros_like(zero_ref)

    zero_fills(lambda fill: fill.start())
    _to_slabs(h1_ref[...], rows_ref, tm)

    def issue(r0, carry):
        for k in range(DMA_BURST):
            r = r0 * DMA_BURST + k
            _slab_copy(rows_ref, r * SLAB, xs_ref, pos_ref[0, 0, r], sem).start(priority=k % 2)
        return carry

    lax.fori_loop(0, tm // DMA_BURST, issue, 0)
    pltpu.make_async_copy(rows_ref, xs_ref.at[pl.ds(0, tm * SLAB)], sem).wait()
    zero_fills(lambda fill: fill.wait())


def _dispatch(h1, pos8, pad_start8, tail_len, n_used, n_tiles, tm):
    T, D = h1.shape
    n_steps = T // tm
    pos8 = pos8.reshape(n_steps, 1, tm)
    grid_spec = pltpu.PrefetchScalarGridSpec(
        num_scalar_prefetch=3,
        grid=(n_steps,),
        in_specs=[pl.BlockSpec((1, 1, tm), lambda t, ps, tl, nu: (t, 0, 0),
                               memory_space=pltpu.SMEM),
                  pl.BlockSpec((tm, D), lambda t, ps, tl, nu: (t, 0))],
        out_specs=pl.BlockSpec(memory_space=pl.ANY),
        scratch_shapes=[pltpu.VMEM((tm * SLAB, LANES), F32),
                        pltpu.VMEM((MOE_TILE_ROWS * SLAB, LANES), F32),
                        pltpu.SemaphoreType.DMA,
                        pltpu.SemaphoreType.DMA],
    )
    return pl.pallas_call(
        functools.partial(_dispatch_kernel, tm=tm, n_tiles=n_tiles,
                          slots_per_step=pl.cdiv(N_CLASSES, n_steps)),
        grid_spec=grid_spec,
        out_shape=jax.ShapeDtypeStruct((n_tiles * MOE_TILE_ROWS * SLAB, LANES), F32),
        compiler_params=_cparams(("arbitrary",)),
        name="moe_dispatch",
    )(pad_start8, tail_len, n_used, pos8, h1)


def _experts_kernel(tlo_ref, thi_ref, nu_ref, x_ref, wg_lo_ref, wg_hi_ref, wu_lo_ref, wu_hi_ref,
                    wd_lo_ref, wd_hi_ref, wrt_ref, brt_ref, y_ref):
    j = pl.program_id(0)
    tr = MOE_TILE_ROWS

    @pl.when(j < nu_ref[0])
    def _():
        x = _from_slabs(x_ref, tr).astype(BF16)
        lo_e = tlo_ref[j]
        hi_e = thi_ref[j]
        logits = _dot(x, wrt_ref[...]) + brt_ref[...]
        lane = lax.broadcasted_iota(jnp.int32, logits.shape, 1)
        gl = jnp.where(lane < N_GROUPS, logits, -jnp.inf)
        ge = jnp.exp(gl - jnp.max(gl, axis=-1, keepdims=True))
        pick = lambda v, at: jnp.sum(jnp.where(lane == at, v, 0.0), axis=-1, keepdims=True)
        g_prob = pick(ge, lo_e // EXPERTS_PER_GROUP) / jnp.sum(ge, axis=-1, keepdims=True)
        e_lo = pick(logits, EXPERT_LANE0 + lo_e)
        e_hi = pick(logits, EXPERT_LANE0 + hi_e)
        gates = (g_prob / (1.0 + jnp.exp(e_hi - e_lo)), g_prob / (1.0 + jnp.exp(e_lo - e_hi)))

        y = None
        for gate, wg_ref, wu_ref, wd_ref in zip(gates, (wg_lo_ref, wg_hi_ref), (wu_lo_ref, wu_hi_ref),
                                                (wd_lo_ref, wd_hi_ref)):
            g = _dot(x, wg_ref[0])
            u = _dot(x, wu_ref[0])
            hid = g * (1.0 / (1.0 + jnp.exp(-g))) * u * gate
            y_k = _dot(hid.astype(BF16), wd_ref[0])
            y = y_k if y is None else y + y_k
        _to_slabs(y, y_ref, tr)


def _experts(xs, tile_lo, tile_hi, n_used, p, n_tiles):
    D = D_MODEL
    tr = MOE_TILE_ROWS
    last_used = lambda j, nu: jnp.minimum(j, nu[0] - 1)
    wgu = lambda pick: pl.BlockSpec((1, D, D_EXPERT),
                                    lambda j, tlo, thi, nu: (pick(tlo, thi)[j], 0, 0))
    wd = lambda pick: pl.BlockSpec((1, D_EXPERT, D),
                                   lambda j, tlo, thi, nu: (pick(tlo, thi)[j], 0, 0))
    lo = lambda tlo, thi: tlo
    hi = lambda tlo, thi: thi
    grid_spec = pltpu.PrefetchScalarGridSpec(
        num_scalar_prefetch=3,
        grid=(n_tiles,),
        in_specs=[pl.BlockSpec((tr * SLAB, LANES), lambda j, tlo, thi, nu: (last_used(j, nu), 0)),
                  wgu(lo), wgu(hi), wgu(lo), wgu(hi), wd(lo), wd(hi),
                  pl.BlockSpec((D, ROUTE_LANES), lambda j, tlo, thi, nu: (0, 0)),
                  pl.BlockSpec((1, ROUTE_LANES), lambda j, tlo, thi, nu: (0, 0))],
        out_specs=pl.BlockSpec((tr * SLAB, LANES), lambda j, tlo, thi, nu: (last_used(j, nu), 0)),
    )
    return pl.pallas_call(
        _experts_kernel,
        grid_spec=grid_spec,
        out_shape=jax.ShapeDtypeStruct(xs.shape, F32),
        input_output_aliases={3: 0},
        compiler_params=_cparams(("arbitrary",)),
        name="moe_experts",
    )(tile_lo, tile_hi, n_used, xs, p["w_gate"], p["w_gate"], p["w_up"], p["w_up"],
      p["w_down"], p["w_down"],
      p["w_route"], p["b_route"])


def _combine_kernel(pos_ref, pos_next_ref, h1_ref, ys_ref, g2_ref, b2_ref, o_ref,
                    ybuf_ref, sem, *, tm):
    t = pl.program_id(0)
    slot = t % 2

    def gather(idx_ref, dst_slot):
        def issue(r0, carry):
            for k in range(DMA_BURST):
                r = r0 * DMA_BURST + k
                _slab_copy(ys_ref, idx_ref[0, 0, r], ybuf_ref.at[dst_slot], r * SLAB,
                           sem.at[dst_slot]).start(priority=k % 2)
            return carry

        lax.fori_loop(0, tm // DMA_BURST, issue, 0)

    @pl.when(t == 0)
    def _():
        gather(pos_ref, 0)

    @pl.when(t + 1 < pl.num_programs(0))
    def _():
        gather(pos_next_ref, 1 - slot)

    pltpu.make_async_copy(ys_ref.at[pl.ds(0, tm * SLAB)], ybuf_ref.at[slot], sem.at[slot]).wait()
    ffn = _from_slabs(ybuf_ref.at[slot], tm)
    o_ref[...] = _layer_norm(ALPHA * h1_ref[...] + ffn, g2_ref[...], b2_ref[...])


def _combine(h1, pos8, ys, p, tm):
    T, D = h1.shape
    n_steps = T // tm
    pos8 = pos8.reshape(n_steps, 1, tm)
    full = lambda shape: pl.BlockSpec(shape, lambda t: (0,) * len(shape))
    return pl.pallas_call(
        functools.partial(_combine_kernel, tm=tm),
        grid=(n_steps,),
        in_specs=[pl.BlockSpec((1, 1, tm), lambda t: (t, 0, 0), memory_space=pltpu.SMEM),
                  pl.BlockSpec((1, 1, tm), lambda t: (jnp.minimum(t + 1, n_steps - 1), 0, 0),
                               memory_space=pltpu.SMEM),
                  pl.BlockSpec((tm, D), lambda t: (t, 0)),
                  pl.BlockSpec(memory_space=pl.ANY),
                  full((1, D)), full((1, D))],
        out_specs=pl.BlockSpec((tm, D), lambda t: (t, 0)),
        out_shape=jax.ShapeDtypeStruct((T, D), F32),
        scratch_shapes=[pltpu.VMEM((2, tm * SLAB, LANES), F32),
                        pltpu.SemaphoreType.DMA((2,))],
        compiler_params=_cparams(("arbitrary",)),
        name="moe_combine_ln2",
    )(pos8, pos8, h1, ys, p["ln2_g"], p["ln2_b"])


def _pair_layout(n):
    order, prev_end = [], None
    for a in range(n - 1):
        end = n - 1 if (a >= n - 2 or (n - 3 - a) % 2 == 0) else n - 2
        middle = [b for b in range(a + 1, n) if b not in (prev_end, end)]
        head = [] if prev_end in (None, end) else [prev_end]
        order += [(a, b) for b in head + middle + [end]]
        prev_end = end
    return order


def _dispatch_plan(route, counts):
    T = route.shape[0] * route.shape[1] * route.shape[3]
    tr = MOE_TILE_ROWS
    n_tiles = T // tr + N_CLASSES
    cnt = counts[0, :N_CLASSES].astype(jnp.int32)
    padded = jnp.maximum((cnt + tr - 1) // tr, 1) * tr
    classes = jnp.arange(N_CLASSES, dtype=jnp.int32)
    pairs = [(a, b) for a in range(EXPERTS_PER_GROUP) for b in range(a + 1, EXPERTS_PER_GROUP)]
    layout = _pair_layout(EXPERTS_PER_GROUP)
    place = jnp.asarray([g * PAIRS_PER_GROUP + layout.index(pair) for g in range(N_GROUPS)
                         for pair in pairs], jnp.int32)
    seg_end = jnp.sum(jnp.where(place[None, :] <= place[:, None], padded[None, :], 0), axis=1)
    seg_start = seg_end - padded
    cls = route[:, :, ROUTE_CLS, :].reshape(T)
    rank = route[:, :, ROUTE_RANK, :].reshape(T)
    start_of = jnp.sum(jnp.where(cls[:, None] == classes, seg_start, 0), axis=-1)
    pos8 = (start_of + rank) * SLAB
    tile_row0 = jnp.arange(n_tiles, dtype=jnp.int32) * tr
    tile_place = jnp.sum((seg_end[None, :] <= tile_row0[:, None]).astype(jnp.int32), axis=-1)
    tile_place = jnp.minimum(tile_place, N_CLASSES - 1)
    lo_of = jnp.asarray([g * EXPERTS_PER_GROUP + a for g in range(N_GROUPS) for a, _ in pairs],
                        jnp.int32)
    hi_of = jnp.asarray([g * EXPERTS_PER_GROUP + b for g in range(N_GROUPS) for _, b in pairs],
                        jnp.int32)
    is_cls = tile_place[:, None] == place
    tile_lo = jnp.sum(jnp.where(is_cls, lo_of, 0), axis=-1)
    tile_hi = jnp.sum(jnp.where(is_cls, hi_of, 0), axis=-1)
    n_used = (jnp.sum(padded, keepdims=True) // tr).astype(jnp.int32)
    return pos8, (seg_start + cnt) * SLAB, padded - cnt, tile_lo, tile_hi, n_used, n_tiles


def _rope_tables(pos):
    d = SWA_HEAD_DIM
    inv_freq = ROPE_THETA ** (-jnp.arange(0, d, 2, dtype=F32) / d)
    ang = pos[:, None] * inv_freq[None, :]
    c, s = jnp.cos(ang), jnp.sin(ang)
    return (jnp.concatenate([c, c, c, c], axis=-1), jnp.concatenate([-s, s, -s, s], axis=-1),
            jnp.concatenate([c, c], axis=-1).T, jnp.concatenate([s, s], axis=-1).T)


def _prep_params(ln_in_g, ln_in_b, w_in, swa_sinks, mla_q_norm_g, mla_w_uq, mla_kv_norm_g, mla_w_ukv,
                 swa_out_norm_g, mla_out_norm_g, w_o, ln1_g, ln1_b, moe_w_group, moe_b_group,
                 moe_w_router, moe_b_router, moe_w_gate, moe_w_up, moe_w_down, ln2_g, ln2_b):
    hd = SWA_HEAD_DIM

    def by_head(t, axis):
        return [lax.slice_in_dim(t, h * hd, (h + 1) * hd, axis=axis) for h in SWA_HEAD_ORDER]

    w = w_in[0]
    w_pad = jnp.zeros((D_MODEL, IN_COLS_PADDED - w.shape[1]), w.dtype)
    w_in_p = jnp.concatenate(by_head(w, 1) + [w[:, SWA_Q_COLS:], w_pad], axis=1).astype(BF16)

    dqk = MLA_NOPE_DIM + MLA_ROPE_DIM
    wuq = mla_w_uq[0].reshape(MLA_Q_RANK, MLA_HEADS, dqk)
    wuq = jnp.pad(wuq, ((0, 0), (0, 0), (0, MLA_QK_PAD - dqk)))
    wuq_t = jnp.transpose(wuq, (1, 2, 0)).astype(BF16)
    wukv = mla_w_ukv[0].reshape(MLA_KV_RANK, MLA_HEADS, MLA_NOPE_DIM + MLA_V_DIM)
    wuk = wukv[:, :, :MLA_NOPE_DIM].reshape(MLA_KV_RANK, MLA_HEADS * MLA_NOPE_DIM).astype(BF16)
    wuv_t = jnp.transpose(wukv[:, :, MLA_NOPE_DIM:], (1, 2, 0)).astype(BF16)

    w_o0 = w_o[0]
    w_o_p = jnp.concatenate(by_head(w_o0, 0) + [w_o0[SWA_Q_COLS:]], axis=0).astype(BF16)

    def route_lanes(group_part, expert_part):
        rows = group_part.shape[0]
        gap = jnp.zeros((rows, EXPERT_LANE0 - N_GROUPS), F32)
        tail = jnp.zeros((rows, ROUTE_LANES - EXPERT_LANE0 - N_EXPERTS), F32)
        return jnp.concatenate([group_part, gap, expert_part, tail], axis=1)

    w_route = route_lanes(moe_w_group[0], moe_w_router[0])
    w_route_hi = w_route.astype(BF16)
    w_route_lo = (w_route - w_route_hi.astype(F32)).astype(BF16)
    w_route = jnp.concatenate([w_route_hi, w_route_lo], axis=1)
    b_route = route_lanes(moe_b_group, moe_b_router)

    return {
        "ln_in_g": ln_in_g[None, :], "ln_in_b": ln_in_b[None, :],
        "w_in": w_in_p,
        "sinks": swa_sinks[0],
        "gq": mla_q_norm_g, "w_uq_t": wuq_t,
        "gkv": mla_kv_norm_g, "w_uk": wuk, "w_uv_t": wuv_t,
        "ga": jnp.concatenate(by_head(swa_out_norm_g, 1), axis=1), "gb": mla_out_norm_g,
        "w_o": w_o_p, "ln1_g": ln1_g, "ln1_b": ln1_b,
        "w_route": w_route, "b_route": b_route,
        "w_gate": moe_w_gate[0].astype(BF16), "w_up": moe_w_up[0].astype(BF16),
        "w_down": moe_w_down[0].astype(BF16),
        "ln2_g": ln2_g, "ln2_b": ln2_b,
    }


def kernel(x, meta_tokens, ln_in_g, ln_in_b, w_in, swa_sinks, mla_q_norm_g, mla_w_uq, mla_kv_norm_g, mla_w_ukv, swa_out_norm_g, mla_out_norm_g, w_o, ln1_g, ln1_b, moe_w_group, moe_b_group, moe_w_router, moe_b_router, moe_w_gate, moe_w_up, moe_w_down, ln2_g, ln2_b):
    B, S, D = x.shape
    p = _prep_params(ln_in_g, ln_in_b, w_in, swa_sinks, mla_q_norm_g, mla_w_uq, mla_kv_norm_g,
                     mla_w_ukv, swa_out_norm_g, mla_out_norm_g, w_o, ln1_g, ln1_b, moe_w_group,
                     moe_b_group, moe_w_router, moe_b_router, moe_w_gate, moe_w_up, moe_w_down,
                     ln2_g, ln2_b)
    T = B * S
    tm = min(PROJ_TILE, S)

    x_meta = jnp.concatenate([jnp.zeros((N_PAD, D), x.dtype), meta_tokens.astype(x.dtype)])[None]
    pos_meta = jnp.maximum(jnp.arange(BLOCK) - N_PAD, 0).astype(F32)
    pos_tok = (jnp.arange(S) + N_META).astype(F32)
    _, ka_m, va_m, _, km_m, vt_m = _in_proj(x_meta, _rope_tables(pos_meta), p, BLOCK)
    a_out, qt, km, vt, h = _in_proj_swa(x, _rope_tables(pos_tok), p, ka_m, va_m, tm)
    b_out = _mla(qt, km, vt, km_m[:, :, N_PAD:, :], vt_m[..., N_PAD:], tm)
    h1, route, counts = _out_proj(h, a_out, b_out, p, min(WIDE_TILE, S))
    h1 = h1.reshape(B * S, D)

    pos8, pad_start8, tail_len, tile_lo, tile_hi, n_used, n_tiles = _dispatch_plan(route, counts)
    xs = _dispatch(h1, pos8, pad_start8, tail_len, n_used, n_tiles, min(DISPATCH_TILE, T))
    ys = _experts(xs, tile_lo, tile_hi, n_used, p, n_tiles)
    out = _combine(h1, pos8, ys, p, min(COMBINE_TILE, T))
    return out.reshape(B, S, D)
```

```python
import functools

import jax
import jax.numpy as jnp
from jax import lax
from jax.experimental import pallas as pl
from jax.experimental.pallas import tpu as pltpu

D_MODEL = 1024
N_META = 16
BLOCK = 128
N_PAD = BLOCK - N_META
ROPE_THETA = 10000.0
SWA_HEADS = 8
SWA_KV_HEADS = 2
SWA_HEAD_DIM = 64
MLA_HEADS = 4
MLA_Q_RANK = 256
MLA_KV_RANK = 256
MLA_NOPE_DIM = 128
MLA_ROPE_DIM = 64
MLA_V_DIM = 128
SWA_Q_COLS = SWA_HEADS * SWA_HEAD_DIM
SWA_KV_COLS = SWA_KV_HEADS * SWA_HEAD_DIM
MLA_OUT_COLS = MLA_HEADS * MLA_V_DIM
N_GROUPS = 4
EXPERTS_PER_GROUP = 8
N_EXPERTS = N_GROUPS * EXPERTS_PER_GROUP
D_EXPERT = 256
LN_EPS = 1e-5
RMS_EPS = 1e-6
DEPTH = 1
ALPHA = (2.0 * DEPTH) ** 0.25
NEG = -1e30
LOG2_E = 1.4426950408889634

LANES = 128
COL_KA = SWA_Q_COLS
COL_VA = COL_KA + SWA_KV_COLS
COL_CQ = COL_VA + SWA_KV_COLS
COL_CKV = COL_CQ + MLA_Q_RANK
COL_KR = COL_CKV + MLA_KV_RANK
IN_COLS_PADDED = COL_KR + LANES
MLA_QK_PAD = 256
ROUTE_LANES = 128
EXPERT_LANE0 = 32
ROUTE_CLS, ROUTE_RANK = range(2)
ROUTE_ROWS = 8
PAIRS_PER_GROUP = EXPERTS_PER_GROUP * (EXPERTS_PER_GROUP - 1) // 2
N_CLASSES = N_GROUPS * PAIRS_PER_GROUP
MOE_TILE_ROWS = 256
PROJ_TILE = 1024
WIDE_TILE = 1024
DISPATCH_TILE = 2048
COMBINE_TILE = 1024
DMA_BURST = 8
VMEM_LIMIT = 56 * 1024 * 1024

SWA_HEAD_ORDER = (0, 4, 1, 5, 2, 6, 3, 7)

BF16 = jnp.bfloat16
F32 = jnp.float32


def _cparams(sem):
    return pltpu.CompilerParams(dimension_semantics=sem, vmem_limit_bytes=VMEM_LIMIT)


def _layer_norm(x, g, b):
    mu = jnp.mean(x, axis=-1, keepdims=True)
    xc = x - mu
    var = jnp.mean(xc * xc, axis=-1, keepdims=True)
    return xc * lax.rsqrt(var + LN_EPS) * g + b


def _rms_scale(x):
    return lax.rsqrt(jnp.mean(x * x, axis=-1, keepdims=True) + RMS_EPS)


def _rope128(x, cos, sin_signed, lane_lo):
    up = pltpu.roll(x, LANES - 32, 1)
    dn = pltpu.roll(x, 32, 1)
    return x * cos + jnp.where(lane_lo, up, dn) * sin_signed


def _dot(a, b):
    return jnp.dot(a, b, preferred_element_type=F32)


def _dot_nt(a, b):
    return lax.dot_general(a, b, (((1,), (1,)), ((), ())), preferred_element_type=F32)


def _in_proj_kernel(x_ref, g_ref, b_ref, w_ref, cos_ref, sin_ref, cost_ref, sint_ref, gq_ref,
                    wuqt_ref, gkv_ref, wuk_ref, wuvt_ref,
                    qa_ref, ka_ref, va_ref, qt_ref, km_ref, vt_ref, h_ref=None):
    h = _layer_norm(x_ref[0], g_ref[...], b_ref[...])
    if h_ref is not None:
        h_ref[0] = h
    u = _dot(h.astype(BF16), w_ref[...])
    cos = cos_ref[...]
    sin = sin_ref[...]
    lane = lax.broadcasted_iota(jnp.int32, (1, LANES), 1)
    lane_lo = (lane % 64) < 32
    rope = functools.partial(_rope128, cos=cos, sin_signed=sin, lane_lo=lane_lo)

    swa_scale = SWA_HEAD_DIM ** -0.5
    for c in range(SWA_Q_COLS // LANES):
        qc = rope(u[:, c * LANES:(c + 1) * LANES]) * swa_scale
        qa_ref[0, :, c * LANES:(c + 1) * LANES] = qc.astype(BF16)
    ka_ref[0] = rope(u[:, COL_KA:COL_VA]).astype(BF16)
    va_ref[0] = u[:, COL_VA:COL_CQ].astype(BF16)

    cq = u[:, COL_CQ:COL_CKV]
    cqn = (cq * _rms_scale(cq) * gq_ref[...]).astype(BF16)
    ckv = u[:, COL_CKV:COL_KR]
    ckvn = (ckv * _rms_scale(ckv) * gkv_ref[...]).astype(BF16)
    k_nope = _dot(ckvn, wuk_ref[...])
    kr = rope(u[:, COL_KR:IN_COLS_PADDED]).astype(BF16)
    q_scale = (MLA_NOPE_DIM + MLA_ROPE_DIM) ** -0.5 * LOG2_E
    cost = cost_ref[...]
    sint = sint_ref[...]
    half = MLA_ROPE_DIM // 2
    for hd in range(MLA_HEADS):
        qt = _dot_nt(wuqt_ref[hd], cqn)
        qr = qt[MLA_NOPE_DIM:MLA_NOPE_DIM + MLA_ROPE_DIM]
        rot = jnp.concatenate([-qr[half:], qr[:half]], axis=0)
        qt_ref[0, hd, 0:MLA_NOPE_DIM, :] = (qt[:MLA_NOPE_DIM] * q_scale).astype(BF16)
        qt_ref[0, hd, MLA_NOPE_DIM:MLA_NOPE_DIM + MLA_ROPE_DIM, :] = (
            (qr * cost + rot * sint) * q_scale).astype(BF16)
        qt_ref[0, hd, MLA_NOPE_DIM + MLA_ROPE_DIM:, :] = (
            qt[MLA_NOPE_DIM + MLA_ROPE_DIM:]).astype(BF16)
        km_ref[0, hd, :, 0:LANES] = k_nope[:, hd * LANES:(hd + 1) * LANES].astype(BF16)
        km_ref[0, hd, :, LANES:2 * LANES] = kr
        vt_ref[0, hd, 0] = _dot_nt(wuvt_ref[hd], ckvn).astype(BF16)


def _in_proj(x, tables, p, tm):
    cos, sin, cos_t, sin_t = tables
    B, S, D = x.shape
    grid = (B, S // tm)
    full = lambda shape: pl.BlockSpec(shape, lambda b, i: (0,) * len(shape))
    out_shape = (
        jax.ShapeDtypeStruct((B, S, SWA_Q_COLS), BF16),
        jax.ShapeDtypeStruct((B, S, SWA_KV_COLS), BF16),
        jax.ShapeDtypeStruct((B, S, SWA_KV_COLS), BF16),
        jax.ShapeDtypeStruct((B, MLA_HEADS, MLA_QK_PAD, S), BF16),
        jax.ShapeDtypeStruct((B, MLA_HEADS, S, MLA_QK_PAD), BF16),
        jax.ShapeDtypeStruct((B, MLA_HEADS, S // tm, MLA_V_DIM, tm), BF16),
    )
    row = lambda w: pl.BlockSpec((1, tm, w), lambda b, i: (b, i, 0))
    head = lambda w: pl.BlockSpec((1, MLA_HEADS, tm, w), lambda b, i: (b, 0, i, 0))
    return pl.pallas_call(
        _in_proj_kernel,
        grid=grid,
        in_specs=[
            row(D),
            full((1, D)), full((1, D)),
            full((D, IN_COLS_PADDED)),
            pl.BlockSpec((tm, LANES), lambda b, i: (i, 0)),
            pl.BlockSpec((tm, LANES), lambda b, i: (i, 0)),
            pl.BlockSpec((MLA_ROPE_DIM, tm), lambda b, i: (0, i)),
            pl.BlockSpec((MLA_ROPE_DIM, tm), lambda b, i: (0, i)),
            full((1, MLA_Q_RANK)), full((MLA_HEADS, MLA_QK_PAD, MLA_Q_RANK)),
            full((1, MLA_KV_RANK)), full((MLA_KV_RANK, MLA_HEADS * MLA_NOPE_DIM)),
            full((MLA_HEADS, MLA_V_DIM, MLA_KV_RANK)),
        ],
        out_specs=(row(SWA_Q_COLS), row(SWA_KV_COLS), row(SWA_KV_COLS),
                   pl.BlockSpec((1, MLA_HEADS, MLA_QK_PAD, tm), lambda b, i: (b, 0, 0, i)),
                   head(MLA_QK_PAD),
                   pl.BlockSpec((1, MLA_HEADS, 1, MLA_V_DIM, tm), lambda b, i: (b, 0, i, 0, 0))),
        out_shape=out_shape,
        compiler_params=_cparams(("parallel", "parallel")),
        name="in_proj",
    )(x, p["ln_in_g"], p["ln_in_b"], p["w_in"], cos, sin, cos_t, sin_t, p["gq"], p["w_uq_t"],
      p["gkv"], p["w_uk"], p["w_uv_t"])


def _swa_blocks(sink_ref, qa_ref, kc_ref, vc_ref, kprev, vprev, first, o_ref, blocks_per_step):
    row = lax.broadcasted_iota(jnp.int32, (2 * BLOCK, BLOCK), 0) % BLOCK
    col = lax.broadcasted_iota(jnp.int32, (2 * BLOCK, BLOCK), 1)
    own = col <= row
    meta_ok = col >= jnp.where(first, N_PAD, 0)
    lane = lax.broadcasted_iota(jnp.int32, (1, LANES), 1)
    lane_kv0 = lane < SWA_HEAD_DIM
    top_rows = lax.broadcasted_iota(jnp.int32, (2 * BLOCK, 1), 0) < BLOCK

    for r in range(blocks_per_step):
        rows = slice(r * BLOCK, (r + 1) * BLOCK)
        if r == 0:
            kp, vp = kprev, vprev
        else:
            prev_rows = slice((r - 1) * BLOCK, r * BLOCK)
            kp, vp = kc_ref[0, prev_rows, :], vc_ref[0, prev_rows, :]
        kc, vc = kc_ref[0, rows, :], vc_ref[0, rows, :]
        for c in range(SWA_Q_COLS // LANES):
            qc = qa_ref[0, rows, c * LANES:(c + 1) * LANES]
            zero = jnp.zeros_like(qc)
            q2 = jnp.concatenate([jnp.where(lane_kv0, qc, zero),
                                  jnp.where(lane_kv0, zero, qc)], axis=0)
            s_prev = _dot_nt(q2, kp)
            if r == 0:
                s_prev = jnp.where(meta_ok, s_prev, NEG)
            s = jnp.where(own, _dot_nt(q2, kc), s_prev)
            sink = jnp.where(top_rows, sink_ref[c], sink_ref[c + 4])
            m = jnp.maximum(jnp.max(s, axis=-1, keepdims=True), sink)
            e = jnp.exp(s - m)
            denom = jnp.sum(e, axis=-1, keepdims=True) + jnp.exp(sink - m)
            pr = e * (1.0 / denom)
            o2 = (_dot(jnp.where(own, pr, 0.0).astype(BF16), vc)
                  + _dot(jnp.where(own, 0.0, pr).astype(BF16), vp))
            o = jnp.where(lane_kv0, o2[:BLOCK], o2[BLOCK:])
            o_ref[0, rows, c * LANES:(c + 1) * LANES] = o.astype(BF16)


def _in_proj_swa_kernel(sink_ref, x_ref, g_ref, b_ref, w_ref, cos_ref, sin_ref, cost_ref, sint_ref,
                        gq_ref, wuqt_ref, gkv_ref, wuk_ref, wuvt_ref, kmeta_ref, vmeta_ref,
                        a_ref, qt_ref, km_ref, vt_ref, h_ref,
                        qa_s, ka_s, va_s, kprev_s, vprev_s):
    _in_proj_kernel(x_ref, g_ref, b_ref, w_ref, cos_ref, sin_ref, cost_ref, sint_ref, gq_ref,
                    wuqt_ref, gkv_ref, wuk_ref, wuvt_ref, qa_s, ka_s, va_s, qt_ref, km_ref, vt_ref,
                    h_ref=h_ref)
    first = pl.program_id(1) == 0
    kprev = jnp.where(first, kmeta_ref[0], kprev_s[...])
    vprev = jnp.where(first, vmeta_ref[0], vprev_s[...])
    tm = qa_s.shape[1]
    _swa_blocks(sink_ref, qa_s, ka_s, va_s, kprev, vprev, first, a_ref, tm // BLOCK)
    kprev_s[...] = ka_s[0, tm - BLOCK:, :]
    vprev_s[...] = va_s[0, tm - BLOCK:, :]


def _in_proj_swa(x, tables, p, ka_meta, va_meta, tm):
    cos, sin, cos_t, sin_t = tables
    B, S, D = x.shape
    kvw = SWA_KV_COLS
    full = lambda shape: pl.BlockSpec(shape, lambda b, i: (0,) * len(shape))
    row = lambda w: pl.BlockSpec((1, tm, w), lambda b, i: (b, i, 0))
    out_shape = (
        jax.ShapeDtypeStruct((B, S, SWA_Q_COLS), BF16),
        jax.ShapeDtypeStruct((B, MLA_HEADS, MLA_QK_PAD, S), BF16),
        jax.ShapeDtypeStruct((B, MLA_HEADS, S, MLA_QK_PAD), BF16),
        jax.ShapeDtypeStruct((B, MLA_HEADS, S // tm, MLA_V_DIM, tm), BF16),
        jax.ShapeDtypeStruct((B, S, D), F32),
    )
    return pl.pallas_call(
        _in_proj_swa_kernel,
        grid=(B, S // tm),
        in_specs=[
            pl.BlockSpec(memory_space=pltpu.SMEM),
            row(D),
            full((1, D)), full((1, D)),
            full((D, IN_COLS_PADDED)),
            pl.BlockSpec((tm, LANES), lambda b, i: (i, 0)),
            pl.BlockSpec((tm, LANES), lambda b, i: (i, 0)),
            pl.BlockSpec((MLA_ROPE_DIM, tm), lambda b, i: (0, i)),
            pl.BlockSpec((MLA_ROPE_DIM, tm), lambda b, i: (0, i)),
            full((1, MLA_Q_RANK)), full((MLA_HEADS, MLA_QK_PAD, MLA_Q_RANK)),
            full((1, MLA_KV_RANK)), full((MLA_KV_RANK, MLA_HEADS * MLA_NOPE_DIM)),
            full((MLA_HEADS, MLA_V_DIM, MLA_KV_RANK)),
            full((1, BLOCK, kvw)), full((1, BLOCK, kvw)),
        ],
        out_specs=(row(SWA_Q_COLS),
                   pl.BlockSpec((1, MLA_HEADS, MLA_QK_PAD, tm), lambda b, i: (b, 0, 0, i)),
                   pl.BlockSpec((1, MLA_HEADS, tm, MLA_QK_PAD), lambda b, i: (b, 0, i, 0)),
                   pl.BlockSpec((1, MLA_HEADS, 1, MLA_V_DIM, tm), lambda b, i: (b, 0, i, 0, 0)),
                   row(D)),
        out_shape=out_shape,
        scratch_shapes=[pltpu.VMEM((1, tm, SWA_Q_COLS), BF16), pltpu.VMEM((1, tm, kvw), BF16),
                        pltpu.VMEM((1, tm, kvw), BF16), pltpu.VMEM((BLOCK, kvw), BF16),
                        pltpu.VMEM((BLOCK, kvw), BF16)],
        compiler_params=_cparams(("parallel", "arbitrary")),
        name="in_proj_swa",
    )(p["sinks"], x, p["ln_in_g"], p["ln_in_b"], p["w_in"], cos, sin, cos_t, sin_t, p["gq"],
      p["w_uq_t"], p["gkv"], p["w_uk"], p["w_uv_t"], ka_meta, va_meta)


def _mla_kernel(qt_ref, k_ref, vt_ref, kmeta_ref, vtmeta_ref, o_ref,
                s0a_ref, s0b_ref, s1a_ref, s1b_ref, smax_ref, m_ref, l_ref, acc_ref, *, tq):
    u = pl.program_id(2)
    qts = (qt_ref[0, 0, :, :tq], qt_ref[0, 0, :, tq:])
    bufs = ((s0a_ref, s0b_ref), (s1a_ref, s1b_ref))

    for g in range(2):
        s = _dot(kmeta_ref[0, 0], qts[g])
        m = jnp.max(s, axis=0, keepdims=True)
        p = jnp.exp2(s - m)
        m_ref[g] = m
        l_ref[g] = jnp.sum(p, axis=0, keepdims=True)
        acc_ref[g] = _dot(vtmeta_ref[0, 0, 0], p.astype(BF16))

    def scores_into(g, ab, j):
        start = pl.multiple_of(j * tq, tq)
        s = _dot(k_ref[0, 0, pl.ds(start, tq), :], qts[g])
        bufs[g][ab][...] = s
        smax_ref[g, ab] = jnp.max(s, axis=0, keepdims=True)

    def absorb(g, ab, j, causal=False):
        s = bufs[g][ab][...]
        if causal:
            key = lax.broadcasted_iota(jnp.int32, (tq, tq), 0)
            query = lax.broadcasted_iota(jnp.int32, (tq, tq), 1)
            s = jnp.where(key <= query, s, NEG)
            smax = jnp.max(s, axis=0, keepdims=True)
        else:
            smax = smax_ref[g, ab]
        m = m_ref[g]
        m_new = jnp.maximum(m, smax)
        alpha = jnp.exp2(m - m_new)
        p = jnp.exp2(s - m_new)
        m_ref[g] = m_new
        l_ref[g] = alpha * l_ref[g] + jnp.sum(p, axis=0, keepdims=True)
        acc_ref[g] = alpha * acc_ref[g] + _dot(vt_ref[0, 0, j], p.astype(BF16))

    buf_a, buf_b = 0, 1
    for g in range(2):
        scores_into(g, buf_a, 0)

    def pair(t, carry):
        scores_into(0, buf_b, 2 * t + 1)
        absorb(0, buf_a, 2 * t)
        scores_into(1, buf_b, 2 * t + 1)
        scores_into(0, buf_a, 2 * t + 2)
        absorb(1, buf_a, 2 * t)
        absorb(0, buf_b, 2 * t + 1)
        scores_into(1, buf_a, 2 * t + 2)
        absorb(1, buf_b, 2 * t + 1)
        return carry

    lax.fori_loop(0, u, pair, 0)

    scores_into(1, buf_b, 2 * u + 1)
    absorb(1, buf_a, 2 * u)
    absorb(0, buf_a, 2 * u, causal=True)
    absorb(1, buf_b, 2 * u + 1, causal=True)

    for g in range(2):
        o_ref[0, g * tq:(g + 1) * tq, :] = (acc_ref[g] * (1.0 / l_ref[g])).T.astype(BF16)


def _mla(qt, km, vt, km_meta, vt_meta, tq):
    B, H, _, S = qt.shape
    assert S % (2 * tq) == 0, "a step handles an (even, odd) pair of query tiles"
    return pl.pallas_call(
        functools.partial(_mla_kernel, tq=tq),
        grid=(B, H, S // (2 * tq)),
        in_specs=[
            pl.BlockSpec((1, 1, MLA_QK_PAD, 2 * tq), lambda b, h, u: (b, h, 0, u)),
            pl.BlockSpec((1, 1, S, MLA_QK_PAD), lambda b, h, u: (b, h, 0, 0)),
            pl.BlockSpec((1, 1, S // tq, MLA_V_DIM, tq), lambda b, h, u: (b, h, 0, 0, 0)),
            pl.BlockSpec((1, 1, N_META, MLA_QK_PAD), lambda b, h, u: (0, h, 0, 0)),
            pl.BlockSpec((1, 1, 1, MLA_V_DIM, N_META), lambda b, h, u: (0, h, 0, 0, 0)),
        ],
        out_specs=pl.BlockSpec((1, 2 * tq, MLA_V_DIM), lambda b, h, u: (b, u, h)),
        out_shape=jax.ShapeDtypeStruct((B, S, MLA_OUT_COLS), BF16),
        scratch_shapes=[pltpu.VMEM((tq, tq), F32), pltpu.VMEM((tq, tq), F32),
                        pltpu.VMEM((tq, tq), F32), pltpu.VMEM((tq, tq), F32),
                        pltpu.VMEM((2, 2, 1, tq), F32),
                        pltpu.VMEM((2, 1, tq), F32), pltpu.VMEM((2, 1, tq), F32),
                        pltpu.VMEM((2, MLA_V_DIM, tq), F32)],
        compiler_params=_cparams(("parallel", "parallel", "arbitrary")),
        name="mla_attn",
    )(qt, km, vt, km_meta, vt_meta)


def _out_proj_kernel(h_ref, a_ref, b_ref, ga_ref, gb_ref, wo_ref, g1_ref, b1_ref,
                     wrt_ref, brt_ref, h1_ref, route_ref, counts_out_ref, count_ref):
    h = h_ref[0]
    a = a_ref[0].astype(F32)
    b = b_ref[0].astype(F32)
    an = (a * _rms_scale(a) * ga_ref[...]).astype(BF16)
    bn = (b * _rms_scale(b) * gb_ref[...]).astype(BF16)
    mix = _dot(jnp.concatenate([an, bn], axis=-1), wo_ref[...])
    h1 = _layer_norm(ALPHA * h + mix, g1_ref[...], b1_ref[...])
    h1_ref[0] = h1

    x_hi = h1.astype(BF16)
    x_lo = (h1 - x_hi.astype(F32)).astype(BF16)
    hi_both = _dot(x_hi, wrt_ref[...])
    logits = (hi_both[:, :ROUTE_LANES] + hi_both[:, ROUTE_LANES:]
              + _dot(x_lo, wrt_ref[:, :ROUTE_LANES]) + brt_ref[...])
    lane = lax.broadcasted_iota(jnp.int32, logits.shape, 1).astype(F32)
    ninf = -jnp.inf
    no_lane = float(ROUTE_LANES)
    gl = jnp.where(lane < N_GROUPS, logits, ninf)
    gmax = jnp.max(gl, axis=-1, keepdims=True)
    g_idx = jnp.min(jnp.where(gl == gmax, lane, no_lane), axis=-1, keepdims=True)
    lo = EXPERT_LANE0 + g_idx * EXPERTS_PER_GROUP
    el = jnp.where((lane >= lo) & (lane < lo + EXPERTS_PER_GROUP), logits, ninf)
    m1 = jnp.max(el, axis=-1, keepdims=True)
    i1 = jnp.min(jnp.where(el == m1, lane, no_lane), axis=-1, keepdims=True)
    el2 = jnp.where(lane == i1, ninf, el)
    m2 = jnp.max(el2, axis=-1, keepdims=True)
    i2 = jnp.min(jnp.where(el2 == m2, lane, no_lane), axis=-1, keepdims=True)

    a = jnp.minimum(i1, i2) - lo
    b = jnp.maximum(i1, i2) - lo
    cls = (g_idx * PAIRS_PER_GROUP + a * (2.0 * EXPERTS_PER_GROUP - 1.0 - a) * 0.5
           + (b - a - 1.0))

    @pl.when((pl.program_id(0) == 0) & (pl.program_id(1) == 0))
    def _():
        count_ref[...] = jnp.zeros_like(count_ref)

    tm = logits.shape[0]
    onehot = jnp.where(lane == cls, 1.0, 0.0)
    r_i = lax.broadcasted_iota(jnp.int32, (BLOCK, BLOCK), 0)
    c_i = lax.broadcasted_iota(jnp.int32, (BLOCK, BLOCK), 1)
    earlier = jnp.where(r_i > c_i, 1.0, 0.0).astype(BF16)
    running = count_ref[...]
    before = []
    for blk in range(tm // BLOCK):
        oh = onehot[blk * BLOCK:(blk + 1) * BLOCK]
        before.append(_dot(earlier, oh.astype(BF16)) + running)
        running = running + jnp.sum(oh, axis=0, keepdims=True)
    before = jnp.concatenate(before, axis=0)
    rank = jnp.sum(jnp.where(lane == cls, before, 0.0), axis=-1, keepdims=True)
    count_ref[...] = running
    counts_out_ref[...] = running

    route = jnp.where(lane == ROUTE_CLS, cls, 0.0)
    route = jnp.where(lane == ROUTE_RANK, rank, route)
    route_ref[0, 0] = route.T[:ROUTE_ROWS].astype(jnp.int32)


def _out_proj(h, a_out, b_out, p, tm):
    B, S, D = h.shape
    full = lambda shape: pl.BlockSpec(shape, lambda b, i: (0,) * len(shape))
    row = lambda w: pl.BlockSpec((1, tm, w), lambda b, i: (b, i, 0))
    return pl.pallas_call(
        _out_proj_kernel,
        grid=(B, S // tm),
        in_specs=[row(D), row(SWA_Q_COLS), row(MLA_OUT_COLS),
                  full((1, SWA_Q_COLS)), full((1, MLA_OUT_COLS)),
                  full((D, D)), full((1, D)), full((1, D)),
                  full((D, 2 * ROUTE_LANES)), full((1, ROUTE_LANES))],
        out_specs=(row(D), pl.BlockSpec((1, 1, ROUTE_ROWS, tm), lambda b, i: (b, i, 0, 0)),
                   full((1, ROUTE_LANES))),
        out_shape=(jax.ShapeDtypeStruct((B, S, D), F32),
                   jax.ShapeDtypeStruct((B, S // tm, ROUTE_ROWS, tm), jnp.int32),
                   jax.ShapeDtypeStruct((1, ROUTE_LANES), F32)),
        scratch_shapes=[pltpu.VMEM((1, ROUTE_LANES), F32)],
        compiler_params=_cparams(("arbitrary", "arbitrary")),
        name="out_proj_route",
    )(h, a_out, b_out, p["ga"], p["gb"], p["w_o"], p["ln1_g"], p["ln1_b"], p["w_route"],
      p["b_route"])


SLAB = D_MODEL // LANES
TAIL_CHUNKS = tuple(MOE_TILE_ROWS >> b for b in range(MOE_TILE_ROWS.bit_length()))


def _slab_copy(src_ref, src_tok8, dst_ref, dst_tok8, sem):
    return pltpu.make_async_copy(src_ref.at[pl.ds(pl.multiple_of(src_tok8, SLAB), SLAB)],
                                 dst_ref.at[pl.ds(pl.multiple_of(dst_tok8, SLAB), SLAB)], sem)


def _to_slabs(x, slab_ref, n):
    for s in range(SLAB):
        slab_ref[pl.ds(s, n, stride=SLAB), :] = x[:, s * LANES:(s + 1) * LANES]


def _from_slabs(slab_ref, n):
    return jnp.concatenate([slab_ref[pl.ds(s, n, stride=SLAB), :] for s in range(SLAB)], axis=1)


def _dispatch_kernel(padstart_ref, taillen_ref, nu_ref, pos_ref, h1_ref, xs_ref, rows_ref, zero_ref,
                     sem, zero_sem, *, tm, n_tiles, slots_per_step):
    t = pl.program_id(0)

    def zero_fill(start, rows):
        return pltpu.make_async_copy(
            zero_ref.at[pl.ds(0, rows * SLAB)],
            xs_ref.at[pl.ds(pl.multiple_of(start, SLAB), rows * SLAB)], zero_sem)

    def zero_fills(act):
        def one_slot(i, carry):
            c = t * slots_per_step + i

            @pl.when(c < N_CLASSES)
            def _():
                tail = taillen_ref[c]
                for rows in TAIL_CHUNKS:
                    @pl.when((tail & rows) != 0)
                    def _():
                        done = tail & ~(2 * rows - 1)
                        act(zero_fill(padstart_ref[c] + done * SLAB, rows))

            @pl.when(nu_ref[0] + c < n_tiles)
            def _():
                act(zero_fill((nu_ref[0] + c) * (MOE_TILE_ROWS * SLAB), MOE_TILE_ROWS))

            return carry

        lax.fori_loop(0, slots_per_step, one_slot, 0)

    @pl.when(t == 0)
    def _():
        zero_ref[...] = jnp.zeros_like(zero_ref)

    zero_fills(lambda fill: fill.start())
    _to_slabs(h1_ref[...], rows_ref, tm)

    def issue(r0, carry):
        for k in range(DMA_BURST):
            r = r0 * DMA_BURST + k
            _slab_copy(rows_ref, r * SLAB, xs_ref, pos_ref[0, 0, r], sem).start(priority=k % 2)
        return carry

    lax.fori_loop(0, tm // DMA_BURST, issue, 0)
    pltpu.make_async_copy(rows_ref, xs_ref.at[pl.ds(0, tm * SLAB)], sem).wait()
    zero_fills(lambda fill: fill.wait())


def _dispatch(h1, pos8, pad_start8, tail_len, n_used, n_tiles, tm):
    T, D = h1.shape
    n_steps = T // tm
    pos8 = pos8.reshape(n_steps, 1, tm)
    grid_spec = pltpu.PrefetchScalarGridSpec(
        num_scalar_prefetch=3,
        grid=(n_steps,),
        in_specs=[pl.BlockSpec((1, 1, tm), lambda t, ps, tl, nu: (t, 0, 0),
                               memory_space=pltpu.SMEM),
                  pl.BlockSpec((tm, D), lambda t, ps, tl, nu: (t, 0))],
        out_specs=pl.BlockSpec(memory_space=pl.ANY),
        scratch_shapes=[pltpu.VMEM((tm * SLAB, LANES), F32),
                        pltpu.VMEM((MOE_TILE_ROWS * SLAB, LANES), F32),
                        pltpu.SemaphoreType.DMA,
                        pltpu.SemaphoreType.DMA],
    )
    return pl.pallas_call(
        functools.partial(_dispatch_kernel, tm=tm, n_tiles=n_tiles,
                          slots_per_step=pl.cdiv(N_CLASSES, n_steps)),
        grid_spec=grid_spec,
        out_shape=jax.ShapeDtypeStruct((n_tiles * MOE_TILE_ROWS * SLAB, LANES), F32),
        compiler_params=_cparams(("arbitrary",)),
        name="moe_dispatch",
    )(pad_start8, tail_len, n_used, pos8, h1)


def _experts_kernel(tlo_ref, thi_ref, nu_ref, x_ref, wg_lo_ref, wg_hi_ref, wu_lo_ref, wu_hi_ref,
                    wd_lo_ref, wd_hi_ref, wrt_ref, brt_ref, y_ref):
    j = pl.program_id(0)
    tr = MOE_TILE_ROWS

    @pl.when(j < nu_ref[0])
    def _():
        x = _from_slabs(x_ref, tr).astype(BF16)
        lo_e = tlo_ref[j]
        hi_e = thi_ref[j]
        logits = _dot(x, wrt_ref[...]) + brt_ref[...]
        lane = lax.broadcasted_iota(jnp.int32, logits.shape, 1)
        gl = jnp.where(lane < N_GROUPS, logits, -jnp.inf)
        ge = jnp.exp(gl - jnp.max(gl, axis=-1, keepdims=True))
        pick = lambda v, at: jnp.sum(jnp.where(lane == at, v, 0.0), axis=-1, keepdims=True)
        g_prob = pick(ge, lo_e // EXPERTS_PER_GROUP) / jnp.sum(ge, axis=-1, keepdims=True)
        e_lo = pick(logits, EXPERT_LANE0 + lo_e)
        e_hi = pick(logits, EXPERT_LANE0 + hi_e)
        gates = (g_prob / (1.0 + jnp.exp(e_hi - e_lo)), g_prob / (1.0 + jnp.exp(e_lo - e_hi)))

        hids = []
        for gate, wg_ref, wu_ref in zip(gates, (wg_lo_ref, wg_hi_ref), (wu_lo_ref, wu_hi_ref)):
            g = _dot(x, wg_ref[0])
            u = _dot(x, wu_ref[0])
            hids.append((g * (1.0 / (1.0 + jnp.exp(-g))) * u * gate).astype(BF16))
        w_down = jnp.concatenate([wd_lo_ref[0], wd_hi_ref[0]], axis=0)
        _to_slabs(_dot(jnp.concatenate(hids, axis=1), w_down), y_ref, tr)


def _experts(xs, tile_lo, tile_hi, n_used, p, n_tiles):
    D = D_MODEL
    tr = MOE_TILE_ROWS
    last_used = lambda j, nu: jnp.minimum(j, nu[0] - 1)
    wgu = lambda pick: pl.BlockSpec((1, D, D_EXPERT),
                                    lambda j, tlo, thi, nu: (pick(tlo, thi)[j], 0, 0))
    wd = lambda pick: pl.BlockSpec((1, D_EXPERT, D),
                                   lambda j, tlo, thi, nu: (pick(tlo, thi)[j], 0, 0))
    lo = lambda tlo, thi: tlo
    hi = lambda tlo, thi: thi
    grid_spec = pltpu.PrefetchScalarGridSpec(
        num_scalar_prefetch=3,
        grid=(n_tiles,),
        in_specs=[pl.BlockSpec((tr * SLAB, LANES), lambda j, tlo, thi, nu: (last_used(j, nu), 0)),
                  wgu(lo), wgu(hi), wgu(lo), wgu(hi), wd(lo), wd(hi),
                  pl.BlockSpec((D, ROUTE_LANES), lambda j, tlo, thi, nu: (0, 0)),
                  pl.BlockSpec((1, ROUTE_LANES), lambda j, tlo, thi, nu: (0, 0))],
        out_specs=pl.BlockSpec((tr * SLAB, LANES), lambda j, tlo, thi, nu: (last_used(j, nu), 0)),
    )
    return pl.pallas_call(
        _experts_kernel,
        grid_spec=grid_spec,
        out_shape=jax.ShapeDtypeStruct(xs.shape, F32),
        input_output_aliases={3: 0},
        compiler_params=_cparams(("arbitrary",)),
        name="moe_experts",
    )(tile_lo, tile_hi, n_used, xs, p["w_gate"], p["w_gate"], p["w_up"], p["w_up"],
      p["w_down"], p["w_down"],
      p["w_route"], p["b_route"])


def _combine_kernel(pos_ref, pos_next_ref, h1_ref, ys_ref, g2_ref, b2_ref, o_ref,
                    ybuf_ref, sem, *, tm):
    t = pl.program_id(0)
    slot = t % 2

    def gather(idx_ref, dst_slot):
        def issue(r0, carry):
            for k in range(DMA_BURST):
                r = r0 * DMA_BURST + k
                _slab_copy(ys_ref, idx_ref[0, 0, r], ybuf_ref.at[dst_slot], r * SLAB,
                           sem.at[dst_slot]).start(priority=k % 2)
            return carry

        lax.fori_loop(0, tm // DMA_BURST, issue, 0)

    @pl.when(t == 0)
    def _():
        gather(pos_ref, 0)

    @pl.when(t + 1 < pl.num_programs(0))
    def _():
        gather(pos_next_ref, 1 - slot)

    pltpu.make_async_copy(ys_ref.at[pl.ds(0, tm * SLAB)], ybuf_ref.at[slot], sem.at[slot]).wait()
    ffn = _from_slabs(ybuf_ref.at[slot], tm)
    o_ref[...] = _layer_norm(ALPHA * h1_ref[...] + ffn, g2_ref[...], b2_ref[...])


def _combine(h1, pos8, ys, p, tm):
    T, D = h1.shape
    n_steps = T // tm
    pos8 = pos8.reshape(n_steps, 1, tm)
    full = lambda shape: pl.BlockSpec(shape, lambda t: (0,) * len(shape))
    return pl.pallas_call(
        functools.partial(_combine_kernel, tm=tm),
        grid=(n_steps,),
        in_specs=[pl.BlockSpec((1, 1, tm), lambda t: (t, 0, 0), memory_space=pltpu.SMEM),
                  pl.BlockSpec((1, 1, tm), lambda t: (jnp.minimum(t + 1, n_steps - 1), 0, 0),
                               memory_space=pltpu.SMEM),
                  pl.BlockSpec((tm, D), lambda t: (t, 0)),
                  pl.BlockSpec(memory_space=pl.ANY),
                  full((1, D)), full((1, D))],
        out_specs=pl.BlockSpec((tm, D), lambda t: (t, 0)),
        out_shape=jax.ShapeDtypeStruct((T, D), F32),
        scratch_shapes=[pltpu.VMEM((2, tm * SLAB, LANES), F32),
                        pltpu.SemaphoreType.DMA((2,))],
        compiler_params=_cparams(("arbitrary",)),
        name="moe_combine_ln2",
    )(pos8, pos8, h1, ys, p["ln2_g"], p["ln2_b"])


def _pair_layout(n):
    order, prev_end = [], None
    for a in range(n - 1):
        end = n - 1 if (a >= n - 2 or (n - 3 - a) % 2 == 0) else n - 2
        middle = [b for b in range(a + 1, n) if b not in (prev_end, end)]
        head = [] if prev_end in (None, end) else [prev_end]
        order += [(a, b) for b in head + middle + [end]]
        prev_end = end
    return order


def _dispatch_plan(route, counts):
    T = route.shape[0] * route.shape[1] * route.shape[3]
    tr = MOE_TILE_ROWS
    n_tiles = T // tr + N_CLASSES
    cnt = counts[0, :N_CLASSES].astype(jnp.int32)
    padded = jnp.maximum((cnt + tr - 1) // tr, 1) * tr
    classes = jnp.arange(N_CLASSES, dtype=jnp.int32)
    pairs = [(a, b) for a in range(EXPERTS_PER_GROUP) for b in range(a + 1, EXPERTS_PER_GROUP)]
    layout = _pair_layout(EXPERTS_PER_GROUP)
    place = jnp.asarray([g * PAIRS_PER_GROUP + layout.index(pair) for g in range(N_GROUPS)
                         for pair in pairs], jnp.int32)
    seg_end = jnp.sum(jnp.where(place[None, :] <= place[:, None], padded[None, :], 0), axis=1)
    seg_start = seg_end - padded
    cls = route[:, :, ROUTE_CLS, :].reshape(T)
    rank = route[:, :, ROUTE_RANK, :].reshape(T)
    start_of = jnp.sum(jnp.where(cls[:, None] == classes, seg_start, 0), axis=-1)
    pos8 = (start_of + rank) * SLAB
    tile_row0 = jnp.arange(n_tiles, dtype=jnp.int32) * tr
    tile_place = jnp.sum((seg_end[None, :] <= tile_row0[:, None]).astype(jnp.int32), axis=-1)
    tile_place = jnp.minimum(tile_place, N_CLASSES - 1)
    lo_of = jnp.asarray([g * EXPERTS_PER_GROUP + a for g in range(N_GROUPS) for a, _ in pairs],
                        jnp.int32)
    hi_of = jnp.asarray([g * EXPERTS_PER_GROUP + b for g in range(N_GROUPS) for _, b in pairs],
                        jnp.int32)
    is_cls = tile_place[:, None] == place
    tile_lo = jnp.sum(jnp.where(is_cls, lo_of, 0), axis=-1)
    tile_hi = jnp.sum(jnp.where(is_cls, hi_of, 0), axis=-1)
    n_used = (jnp.sum(padded, keepdims=True) // tr).astype(jnp.int32)
    return pos8, (seg_start + cnt) * SLAB, padded - cnt, tile_lo, tile_hi, n_used, n_tiles


def _rope_tables(pos):
    d = SWA_HEAD_DIM
    inv_freq = ROPE_THETA ** (-jnp.arange(0, d, 2, dtype=F32) / d)
    ang = pos[:, None] * inv_freq[None, :]
    c, s = jnp.cos(ang), jnp.sin(ang)
    return (jnp.concatenate([c, c, c, c], axis=-1), jnp.concatenate([-s, s, -s, s], axis=-1),
            jnp.concatenate([c, c], axis=-1).T, jnp.concatenate([s, s], axis=-1).T)


def _prep_params(ln_in_g, ln_in_b, w_in, swa_sinks, mla_q_norm_g, mla_w_uq, mla_kv_norm_g, mla_w_ukv,
                 swa_out_norm_g, mla_out_norm_g, w_o, ln1_g, ln1_b, moe_w_group, moe_b_group,
                 moe_w_router, moe_b_router, moe_w_gate, moe_w_up, moe_w_down, ln2_g, ln2_b):
    hd = SWA_HEAD_DIM

    def by_head(t, axis):
        return [lax.slice_in_dim(t, h * hd, (h + 1) * hd, axis=axis) for h in SWA_HEAD_ORDER]

    w = w_in[0]
    w_pad = jnp.zeros((D_MODEL, IN_COLS_PADDED - w.shape[1]), w.dtype)
    w_in_p = jnp.concatenate(by_head(w, 1) + [w[:, SWA_Q_COLS:], w_pad], axis=1).astype(BF16)

    dqk = MLA_NOPE_DIM + MLA_ROPE_DIM
    wuq = mla_w_uq[0].reshape(MLA_Q_RANK, MLA_HEADS, dqk)
    wuq = jnp.pad(wuq, ((0, 0), (0, 0), (0, MLA_QK_PAD - dqk)))
    wuq_t = jnp.transpose(wuq, (1, 2, 0)).astype(BF16)
    wukv = mla_w_ukv[0].reshape(MLA_KV_RANK, MLA_HEADS, MLA_NOPE_DIM + MLA_V_DIM)
    wuk = wukv[:, :, :MLA_NOPE_DIM].reshape(MLA_KV_RANK, MLA_HEADS * MLA_NOPE_DIM).astype(BF16)
    wuv_t = jnp.transpose(wukv[:, :, MLA_NOPE_DIM:], (1, 2, 0)).astype(BF16)

    w_o0 = w_o[0]
    w_o_p = jnp.concatenate(by_head(w_o0, 0) + [w_o0[SWA_Q_COLS:]], axis=0).astype(BF16)

    def route_lanes(group_part, expert_part):
        rows = group_part.shape[0]
        gap = jnp.zeros((rows, EXPERT_LANE0 - N_GROUPS), F32)
        tail = jnp.zeros((rows, ROUTE_LANES - EXPERT_LANE0 - N_EXPERTS), F32)
        return jnp.concatenate([group_part, gap, expert_part, tail], axis=1)

    w_route = route_lanes(moe_w_group[0], moe_w_router[0])
    w_route_hi = w_route.astype(BF16)
    w_route_lo = (w_route - w_route_hi.astype(F32)).astype(BF16)
    w_route = jnp.concatenate([w_route_hi, w_route_lo], axis=1)
    b_route = route_lanes(moe_b_group, moe_b_router)

    return {
        "ln_in_g": ln_in_g[None, :], "ln_in_b": ln_in_b[None, :],
        "w_in": w_in_p,
        "sinks": swa_sinks[0],
        "gq": mla_q_norm_g, "w_uq_t": wuq_t,
        "gkv": mla_kv_norm_g, "w_uk": wuk, "w_uv_t": wuv_t,
        "ga": jnp.concatenate(by_head(swa_out_norm_g, 1), axis=1), "gb": mla_out_norm_g,
        "w_o": w_o_p, "ln1_g": ln1_g, "ln1_b": ln1_b,
        "w_route": w_route, "b_route": b_route,
        "w_gate": moe_w_gate[0].astype(BF16), "w_up": moe_w_up[0].astype(BF16),
        "w_down": moe_w_down[0].astype(BF16),
        "ln2_g": ln2_g, "ln2_b": ln2_b,
    }


def kernel(x, meta_tokens, ln_in_g, ln_in_b, w_in, swa_sinks, mla_q_norm_g, mla_w_uq, mla_kv_norm_g, mla_w_ukv, swa_out_norm_g, mla_out_norm_g, w_o, ln1_g, ln1_b, moe_w_group, moe_b_group, moe_w_router, moe_b_router, moe_w_gate, moe_w_up, moe_w_down, ln2_g, ln2_b):
    B, S, D = x.shape
    p = _prep_params(ln_in_g, ln_in_b, w_in, swa_sinks, mla_q_norm_g, mla_w_uq, mla_kv_norm_g,
                     mla_w_ukv, swa_out_norm_g, mla_out_norm_g, w_o, ln1_g, ln1_b, moe_w_group,
                     moe_b_group, moe_w_router, moe_b_router, moe_w_gate, moe_w_up, moe_w_down,
                     ln2_g, ln2_b)
    T = B * S
    tm = min(PROJ_TILE, S)

    x_meta = jnp.concatenate([jnp.zeros((N_PAD, D), x.dtype), meta_tokens.astype(x.dtype)])[None]
    pos_meta = jnp.maximum(jnp.arange(BLOCK) - N_PAD, 0).astype(F32)
    pos_tok = (jnp.arange(S) + N_META).astype(F32)
    _, ka_m, va_m, _, km_m, vt_m = _in_proj(x_meta, _rope_tables(pos_meta), p, BLOCK)
    a_out, qt, km, vt, h = _in_proj_swa(x, _rope_tables(pos_tok), p, ka_m, va_m, tm)
    b_out = _mla(qt, km, vt, km_m[:, :, N_PAD:, :], vt_m[..., N_PAD:], tm)
    h1, route, counts = _out_proj(h, a_out, b_out, p, min(WIDE_TILE, S))
    h1 = h1.reshape(B * S, D)

    pos8, pad_start8, tail_len, tile_lo, tile_hi, n_used, n_tiles = _dispatch_plan(route, counts)
    xs = _dispatch(h1, pos8, pad_start8, tail_len, n_used, n_tiles, min(DISPATCH_TILE, T))
    ys = _experts(xs, tile_lo, tile_hi, n_used, p, n_tiles)
    out = _combine(h1, pos8, ys, p, min(COMBINE_TILE, T))
    return out.reshape(B, S, D)
```
